```python
import math
import jax, jax.numpy as jnp
from jax import lax
import numpy as np

D_MODEL = 2048
BATCH = 8
SEQ = 2048
DEPTH = 4

CHUNK = 64
EPS = 1e-6
N_EVEN = (DEPTH + 1) // 2
N_ODD = DEPTH // 2

GM_BLOCK = 128
GM_GROUPS = 8
GM_WIDTH = D_MODEL
GM_GROUP_DIM = GM_WIDTH // GM_GROUPS
SSM_D_INNER = D_MODEL
SSM_HEAD_DIM = 64
SSM_HEADS = SSM_D_INNER // SSM_HEAD_DIM
SSM_GROUPS = 4
SSM_HEADS_PER_GROUP = SSM_HEADS // SSM_GROUPS
SSM_STATE = 128
SSM_CONV = 4
SSM_CHUNK = CHUNK
SSM_CONV_DIM = SSM_D_INNER + 2 * SSM_GROUPS * SSM_STATE
EVEN_IN = 2 * GM_WIDTH + SSM_D_INNER + SSM_CONV_DIM + SSM_HEADS
EVEN_MIX = GM_WIDTH + SSM_D_INNER
MLA_HEADS = 16
MLA_Q_RANK = 512
MLA_KV_RANK = 512
MLA_NOPE = 128
MLA_ROPE = 64
MLA_V = 128
MLA_QK = MLA_NOPE + MLA_ROPE
ODD_IN = MLA_Q_RANK + MLA_KV_RANK + MLA_ROPE
ATTN_BLOCK = 128
ROPE_THETA = 10000.0
MAX_OFFSET_CHUNKS = 64
D_FF = 5632
FFN_CONV = 3

kernel_name = "hybrid_gmlp_ssd_mla_convffn"


def rmsnorm(x, w):
    xf = x.astype(jnp.float32)
    y = xf * lax.rsqrt(jnp.mean(xf * xf, -1, keepdims=True) + EPS)
    return (y * w.astype(jnp.float32)).astype(x.dtype)


def causal_dwconv(x, w, b):
    k = w.shape[0]
    c = x.shape[-1]
    y = lax.conv_general_dilated(
        x, w[:, None, :].astype(x.dtype), window_strides=(1,),
        padding=[(k - 1, 0)], dimension_numbers=("NWC", "WIO", "NWC"),
        feature_group_count=c)
    return y + b.astype(x.dtype)


def gmlp_sgu(u, v, ln_g, ln_b, w_s, b_s):
    bsz, s, _ = u.shape
    nb = s // GM_BLOCK
    vf = v.reshape(bsz, nb, GM_BLOCK, GM_GROUPS, GM_GROUP_DIM).astype(jnp.float32)
    mu = jnp.mean(vf, -1, keepdims=True)
    var = jnp.mean(jnp.square(vf - mu), -1, keepdims=True)
    vn = ((vf - mu) * lax.rsqrt(var + EPS) * ln_g + ln_b).astype(u.dtype)
    chunk_id = jnp.arange(GM_BLOCK) // CHUNK
    mask = chunk_id[:, None] >= chunk_id[None, :]
    ws = jnp.where(mask, w_s, jnp.zeros((), w_s.dtype)).astype(u.dtype)
    gate = jnp.einsum("gij,bnjgc->bnigc", ws, vn) + b_s.T[:, :, None].astype(u.dtype)
    return u * gate.reshape(bsz, s, GM_WIDTH)


def segsum(a):
    t = a.shape[-1]
    cs = jnp.cumsum(a, -1)
    d = cs[..., :, None] - cs[..., None, :]
    mask = jnp.tril(jnp.ones((t, t), dtype=bool))
    return jnp.where(mask, d, -jnp.inf)


def ssd_scan(x, dt, a, b, c):
    bsz, s = x.shape[:2]
    nc = s // SSM_CHUNK
    G, E, P, N, L = SSM_GROUPS, SSM_HEADS_PER_GROUP, SSM_HEAD_DIM, SSM_STATE, SSM_CHUNK
    xd = (x * dt[..., None]).reshape(bsz, nc, L, G, E, P)
    da = jnp.moveaxis((dt * a).reshape(bsz, nc, L, G, E), 2, -1)
    bc = b.reshape(bsz, nc, L, G, N)
    cc = c.reshape(bsz, nc, L, G, N)
    a_cum = jnp.cumsum(da, -1)
    decay = jnp.exp(segsum(da))
    cb = jnp.einsum("bclgn,bcsgn->bcgls", cc, bc)
    y_diag = jnp.einsum("bcgls,bcgels,bcsgep->bclgep", cb, decay, xd)
    decay_states = jnp.exp(a_cum[..., -1:] - a_cum)
    states = jnp.einsum("bclgn,bcgel,bclgep->bcgepn", bc, decay_states, xd)
    chunk_decay = jnp.exp(a_cum[..., -1])

    def step(h, inp):
        s_c, d_c = inp
        return d_c[..., None, None] * h + s_c, h

    h0 = jnp.zeros((bsz, G, E, P, N), jnp.float32)
    _, prev = lax.scan(step, h0, (jnp.moveaxis(states, 1, 0), jnp.moveaxis(chunk_decay, 1, 0)))
    prev = jnp.moveaxis(prev, 0, 1)
    y_off = jnp.einsum("bclgn,bcgepn,bcgel->bclgep", cc, prev, jnp.exp(a_cum))
    return (y_diag + y_off).reshape(bsz, s, SSM_HEADS, P)


def mamba_branch(z, xbc, dt_raw, conv_w, conv_b, dt_bias, a_log, d_skip, norm_w):
    bsz, s, _ = z.shape
    xbc = jax.nn.silu(causal_dwconv(xbc, conv_w, conv_b)).astype(jnp.float32)
    xs, bs, cs = jnp.split(xbc, [SSM_D_INNER, SSM_D_INNER + SSM_GROUPS * SSM_STATE], axis=-1)
    dt = jax.nn.softplus(dt_raw.astype(jnp.float32) + dt_bias.astype(jnp.float32))
    a = -jnp.exp(a_log.astype(jnp.float32))
    xh = xs.reshape(bsz, s, SSM_HEADS, SSM_HEAD_DIM)
    y = ssd_scan(xh, dt, a,
                 bs.reshape(bsz, s, SSM_GROUPS, SSM_STATE),
                 cs.reshape(bsz, s, SSM_GROUPS, SSM_STATE))
    y = y + d_skip.astype(jnp.float32)[:, None] * xh
    y = y.reshape(bsz, s, SSM_D_INNER) * jax.nn.silu(z.astype(jnp.float32))
    y = y.reshape(bsz, s, SSM_GROUPS, SSM_D_INNER // SSM_GROUPS)
    y = y * lax.rsqrt(jnp.mean(y * y, -1, keepdims=True) + EPS)
    y = y.reshape(bsz, s, SSM_D_INNER) * norm_w.astype(jnp.float32)
    return y.astype(z.dtype)


def even_mixer(h, w_in, gm_ln_g, gm_ln_b, gm_ws, gm_bs, conv_w, conv_b,
               dt_bias, a_log, d_skip, ssm_norm_w, w_out):
    proj = h @ w_in
    o1 = GM_WIDTH
    o2 = 2 * GM_WIDTH
    o3 = o2 + SSM_D_INNER
    o4 = o3 + SSM_CONV_DIM
    u, v, z, xbc, dt_raw = jnp.split(proj, [o1, o2, o3, o4], axis=-1)
    ya = gmlp_sgu(jax.nn.gelu(u), jax.nn.gelu(v), gm_ln_g, gm_ln_b, gm_ws, gm_bs)
    yb = mamba_branch(z, xbc, dt_raw, conv_w, conv_b, dt_bias, a_log, d_skip, ssm_norm_w)
    return jnp.concatenate([ya, yb], axis=-1) @ w_out


def rope(x, cos, sin):
    half = x.shape[-1] // 2
    x1, x2 = x[..., :half], x[..., half:]
    return jnp.concatenate([x1 * cos - x2 * sin, x1 * sin + x2 * cos], axis=-1)


def mla_mixer(h, cos, sin, w_in, q_norm_w, kv_norm_w, w_uq, w_ukv, w_o):
    bsz, s, _ = h.shape
    proj = h @ w_in
    cq, ckv, kr = jnp.split(proj, [MLA_Q_RANK, MLA_Q_RANK + MLA_KV_RANK], axis=-1)
    cq = rmsnorm(cq, q_norm_w)
    ckv = rmsnorm(ckv, kv_norm_w)
    q = (cq @ w_uq).reshape(bsz, s, MLA_HEADS, MLA_QK)
    q_nope = q[..., :MLA_NOPE]
    q_pe = rope(q[..., MLA_NOPE:], cos[:, :, None, :], sin[:, :, None, :])
    kv = (ckv @ w_ukv).reshape(bsz, s, MLA_HEADS, MLA_NOPE + MLA_V)
    k_nope, v = kv[..., :MLA_NOPE], kv[..., MLA_NOPE:]
    k_pe = rope(kr, cos, sin)
    scale = MLA_QK ** -0.5
    outs = []
    for i in range(s // ATTN_BLOCK):
        q0 = i * ATTN_BLOCK
        kend = q0 + ATTN_BLOCK
        sc = (jnp.einsum("bqhd,bkhd->bhqk", q_nope[:, q0:kend], k_nope[:, :kend])
              + jnp.einsum("bqhr,bkr->bhqk", q_pe[:, q0:kend], k_pe[:, :kend]))
        sc = sc.astype(jnp.float32) * scale
        qc = (q0 + jnp.arange(ATTN_BLOCK)) // CHUNK
        kc = jnp.arange(kend) // CHUNK
        sc = jnp.where(kc[None, :] <= qc[:, None], sc, -jnp.inf)
        p = jax.nn.softmax(sc, axis=-1).astype(v.dtype)
        outs.append(jnp.einsum("bhqk,bkhd->bqhd", p, v[:, :kend]))
    o = jnp.concatenate(outs, axis=1).reshape(bsz, s, MLA_HEADS * MLA_V)
    return o @ w_o


def conv_ffn(h, w_up, conv_w, conv_b, w_down):
    up = h @ w_up
    g, val = up[..., :D_FF], up[..., D_FF:]
    g = causal_dwconv(g, conv_w, conv_b)
    return (jax.nn.gelu(g) * val) @ w_down


def _fwd_setup_inputs(seed: int = 0) -> dict:
    key = jax.random.key(seed)
    ks = iter(jax.random.split(key, 40))

    def nrm(shape, scale):
        return jax.random.normal(next(ks), shape, jnp.float32) * scale

    def gain(shape):
        return 1.0 + nrm(shape, 0.02)

    x = jax.random.normal(next(ks), (BATCH, SEQ, D_MODEL), jnp.float32)
    offset = jax.random.randint(next(ks), (BATCH,), 0, MAX_OFFSET_CHUNKS) * CHUNK
    positions = (offset[:, None] + jnp.arange(SEQ)[None, :]).astype(jnp.int32)

    dt = jnp.exp(jax.random.uniform(next(ks), (N_EVEN, SSM_HEADS), jnp.float32)
                 * (math.log(0.1) - math.log(0.001)) + math.log(0.001))
    dt_bias = dt + jnp.log(-jnp.expm1(-dt))
    a_log = jnp.log(jax.random.uniform(next(ks), (N_EVEN, SSM_HEADS), jnp.float32, 1.0, 16.0))

    return {
        "x": x,
        "positions": positions,
        "norm_mix": gain((DEPTH, D_MODEL)),
        "norm_ffn": gain((DEPTH, D_MODEL)),
        "norm_final": gain((D_MODEL,)),
        "ev_w_in": nrm((N_EVEN, D_MODEL, EVEN_IN), D_MODEL ** -0.5),
        "ev_gm_ln_g": gain((N_EVEN, GM_GROUPS, GM_GROUP_DIM)),
        "ev_gm_ln_b": nrm((N_EVEN, GM_GROUPS, GM_GROUP_DIM), 0.02),
        "ev_gm_ws": nrm((N_EVEN, GM_GROUPS, GM_BLOCK, GM_BLOCK), GM_BLOCK ** -0.5),
        "ev_gm_bs": gain((N_EVEN, GM_GROUPS, GM_BLOCK)),
        "ev_conv_w": nrm((N_EVEN, SSM_CONV, SSM_CONV_DIM), SSM_CONV ** -0.5),
        "ev_conv_b": nrm((N_EVEN, SSM_CONV_DIM), 0.02),
        "ev_dt_bias": dt_bias,
        "ev_a_log": a_log,
        "ev_d_skip": gain((N_EVEN, SSM_HEADS)),
        "ev_ssm_norm_w": gain((N_EVEN, SSM_D_INNER)),
        "ev_w_out": nrm((N_EVEN, EVEN_MIX, D_MODEL), EVEN_MIX ** -0.5),
        "od_w_in": nrm((N_ODD, D_MODEL, ODD_IN), D_MODEL ** -0.5),
        "od_q_norm": gain((N_ODD, MLA_Q_RANK)),
        "od_kv_norm": gain((N_ODD, MLA_KV_RANK)),
        "od_w_uq": nrm((N_ODD, MLA_Q_RANK, MLA_HEADS * MLA_QK), MLA_Q_RANK ** -0.5),
        "od_w_ukv": nrm((N_ODD, MLA_KV_RANK, MLA_HEADS * (MLA_NOPE + MLA_V)), MLA_KV_RANK ** -0.5),
        "od_w_o": nrm((N_ODD, MLA_HEADS * MLA_V, D_MODEL), (MLA_HEADS * MLA_V) ** -0.5),
        "ff_w_up": nrm((DEPTH, D_MODEL, 2 * D_FF), D_MODEL ** -0.5),
        "ff_conv_w": nrm((DEPTH, FFN_CONV, D_FF), FFN_CONV ** -0.5),
        "ff_conv_b": nrm((DEPTH, D_FF), 0.02),
        "ff_w_down": nrm((DEPTH, D_FF, D_MODEL), D_FF ** -0.5),
    }


def _fwd_reference(x, positions, norm_mix, norm_ffn, norm_final,
              ev_w_in, ev_gm_ln_g, ev_gm_ln_b, ev_gm_ws, ev_gm_bs,
              ev_conv_w, ev_conv_b, ev_dt_bias, ev_a_log, ev_d_skip, ev_ssm_norm_w, ev_w_out,
              od_w_in, od_q_norm, od_kv_norm, od_w_uq, od_w_ukv, od_w_o,
              ff_w_up, ff_conv_w, ff_conv_b, ff_w_down):
    inv_freq = ROPE_THETA ** (-jnp.arange(0, MLA_ROPE, 2, dtype=jnp.float32) / MLA_ROPE)
    ang = positions.astype(jnp.float32)[..., None] * inv_freq
    cos = jnp.cos(ang).astype(x.dtype)
    sin = jnp.sin(ang).astype(x.dtype)

    h = x
    for layer in range(DEPTH):
        j = layer // 2
        hn = rmsnorm(h, norm_mix[layer])
        if layer % 2 == 0:
            mix = even_mixer(hn, ev_w_in[j], ev_gm_ln_g[j], ev_gm_ln_b[j], ev_gm_ws[j], ev_gm_bs[j],
                             ev_conv_w[j], ev_conv_b[j], ev_dt_bias[j], ev_a_log[j], ev_d_skip[j],
                             ev_ssm_norm_w[j], ev_w_out[j])
        else:
            mix = mla_mixer(hn, cos, sin, od_w_in[j], od_q_norm[j], od_kv_norm[j],
                            od_w_uq[j], od_w_ukv[j], od_w_o[j])
        h = h + mix
        h = h + conv_ffn(rmsnorm(h, norm_ffn[layer]), ff_w_up[layer], ff_conv_w[layer],
                         ff_conv_b[layer], ff_w_down[layer])
    return rmsnorm(h, norm_final)


import jax as _jax
import jax.numpy as _jnp

TWIN_FORMAT = 'train_step'
FWD_PARAMS = ['x', 'positions', 'norm_mix', 'norm_ffn', 'norm_final', 'ev_w_in', 'ev_gm_ln_g', 'ev_gm_ln_b', 'ev_gm_ws', 'ev_gm_bs', 'ev_conv_w', 'ev_conv_b', 'ev_dt_bias', 'ev_a_log', 'ev_d_skip', 'ev_ssm_norm_w', 'ev_w_out', 'od_w_in', 'od_q_norm', 'od_kv_norm', 'od_w_uq', 'od_w_ukv', 'od_w_o', 'ff_w_up', 'ff_conv_w', 'ff_conv_b', 'ff_w_down']
TWIN_WEIGHTS = ['norm_mix', 'norm_ffn', 'norm_final', 'ev_w_in', 'ev_gm_ln_g', 'ev_gm_ln_b', 'ev_gm_ws', 'ev_gm_bs', 'ev_conv_w', 'ev_conv_b', 'ev_dt_bias', 'ev_a_log', 'ev_d_skip', 'ev_ssm_norm_w', 'ev_w_out', 'od_w_in', 'od_q_norm', 'od_kv_norm', 'od_w_uq', 'od_w_ukv', 'od_w_o', 'ff_w_up', 'ff_conv_w', 'ff_conv_b', 'ff_w_down']
TWIN_DIFF_INPUT = 'x'
TWIN_INPUTS = ['x', 'positions', 'norm_mix', 'norm_ffn', 'norm_final', 'ev_w_in', 'ev_gm_ln_g', 'ev_gm_ln_b', 'ev_gm_ws', 'ev_gm_bs', 'ev_conv_w', 'ev_conv_b', 'ev_dt_bias', 'ev_a_log', 'ev_d_skip', 'ev_ssm_norm_w', 'ev_w_out', 'od_w_in', 'od_q_norm', 'od_kv_norm', 'od_w_uq', 'od_w_ukv', 'od_w_o', 'ff_w_up', 'ff_conv_w', 'ff_conv_b', 'ff_w_down', 'loss_target', 'm_norm_mix', 'm_norm_ffn', 'm_norm_final', 'm_ev_w_in', 'm_ev_gm_ln_g', 'm_ev_gm_ln_b', 'm_ev_gm_ws', 'm_ev_gm_bs', 'm_ev_conv_w', 'm_ev_conv_b', 'm_ev_dt_bias', 'm_ev_a_log', 'm_ev_d_skip', 'm_ev_ssm_norm_w', 'm_ev_w_out', 'm_od_w_in', 'm_od_q_norm', 'm_od_kv_norm', 'm_od_w_uq', 'm_od_w_ukv', 'm_od_w_o', 'm_ff_w_up', 'm_ff_conv_w', 'm_ff_conv_b', 'm_ff_w_down', 'v_norm_mix', 'v_norm_ffn', 'v_norm_final', 'v_ev_w_in', 'v_ev_gm_ln_g', 'v_ev_gm_ln_b', 'v_ev_gm_ws', 'v_ev_gm_bs', 'v_ev_conv_w', 'v_ev_conv_b', 'v_ev_dt_bias', 'v_ev_a_log', 'v_ev_d_skip', 'v_ev_ssm_norm_w', 'v_ev_w_out', 'v_od_w_in', 'v_od_q_norm', 'v_od_kv_norm', 'v_od_w_uq', 'v_od_w_ukv', 'v_od_w_o', 'v_ff_w_up', 'v_ff_conv_w', 'v_ff_conv_b', 'v_ff_w_down']
TWIN_OUTPUTS = ['loss', 'grad_x', 'grad_norm_mix', 'grad_norm_ffn', 'grad_norm_final', 'grad_ev_w_in', 'grad_ev_gm_ln_g', 'grad_ev_gm_ln_b', 'grad_ev_gm_ws', 'grad_ev_gm_bs', 'grad_ev_conv_w', 'grad_ev_conv_b', 'grad_ev_dt_bias', 'grad_ev_a_log', 'grad_ev_d_skip', 'grad_ev_ssm_norm_w', 'grad_ev_w_out', 'grad_od_w_in', 'grad_od_q_norm', 'grad_od_kv_norm', 'grad_od_w_uq', 'grad_od_w_ukv', 'grad_od_w_o', 'grad_ff_w_up', 'grad_ff_conv_w', 'grad_ff_conv_b', 'grad_ff_w_down', 'delta_norm_mix', 'delta_norm_ffn', 'delta_norm_final', 'delta_ev_w_in', 'delta_ev_gm_ln_g', 'delta_ev_gm_ln_b', 'delta_ev_gm_ws', 'delta_ev_gm_bs', 'delta_ev_conv_w', 'delta_ev_conv_b', 'delta_ev_dt_bias', 'delta_ev_a_log', 'delta_ev_d_skip', 'delta_ev_ssm_norm_w', 'delta_ev_w_out', 'delta_od_w_in', 'delta_od_q_norm', 'delta_od_kv_norm', 'delta_od_w_uq', 'delta_od_w_ukv', 'delta_od_w_o', 'delta_ff_w_up', 'delta_ff_conv_w', 'delta_ff_conv_b', 'delta_ff_w_down', 'new_m_norm_mix', 'new_m_norm_ffn', 'new_m_norm_final', 'new_m_ev_w_in', 'new_m_ev_gm_ln_g', 'new_m_ev_gm_ln_b', 'new_m_ev_gm_ws', 'new_m_ev_gm_bs', 'new_m_ev_conv_w', 'new_m_ev_conv_b', 'new_m_ev_dt_bias', 'new_m_ev_a_log', 'new_m_ev_d_skip', 'new_m_ev_ssm_norm_w', 'new_m_ev_w_out', 'new_m_od_w_in', 'new_m_od_q_norm', 'new_m_od_kv_norm', 'new_m_od_w_uq', 'new_m_od_w_ukv', 'new_m_od_w_o', 'new_m_ff_w_up', 'new_m_ff_conv_w', 'new_m_ff_conv_b', 'new_m_ff_w_down', 'new_v_norm_mix', 'new_v_norm_ffn', 'new_v_norm_final', 'new_v_ev_w_in', 'new_v_ev_gm_ln_g', 'new_v_ev_gm_ln_b', 'new_v_ev_gm_ws', 'new_v_ev_gm_bs', 'new_v_ev_conv_w', 'new_v_ev_conv_b', 'new_v_ev_dt_bias', 'new_v_ev_a_log', 'new_v_ev_d_skip', 'new_v_ev_ssm_norm_w', 'new_v_ev_w_out', 'new_v_od_w_in', 'new_v_od_q_norm', 'new_v_od_kv_norm', 'new_v_od_w_uq', 'new_v_od_w_ukv', 'new_v_od_w_o', 'new_v_ff_w_up', 'new_v_ff_conv_w', 'new_v_ff_conv_b', 'new_v_ff_w_down']
TWIN_LEAF_KINDS = {'loss': 'loss', 'grad_x': 'grad_x', 'grad_norm_mix': 'grad_w', 'grad_norm_ffn': 'grad_w', 'grad_norm_final': 'grad_w', 'grad_ev_w_in': 'grad_w', 'grad_ev_gm_ln_g': 'grad_w', 'grad_ev_gm_ln_b': 'grad_w', 'grad_ev_gm_ws': 'grad_w', 'grad_ev_gm_bs': 'grad_w', 'grad_ev_conv_w': 'grad_w', 'grad_ev_conv_b': 'grad_w', 'grad_ev_dt_bias': 'grad_w', 'grad_ev_a_log': 'grad_w', 'grad_ev_d_skip': 'grad_w', 'grad_ev_ssm_norm_w': 'grad_w', 'grad_ev_w_out': 'grad_w', 'grad_od_w_in': 'grad_w', 'grad_od_q_norm': 'grad_w', 'grad_od_kv_norm': 'grad_w', 'grad_od_w_uq': 'grad_w', 'grad_od_w_ukv': 'grad_w', 'grad_od_w_o': 'grad_w', 'grad_ff_w_up': 'grad_w', 'grad_ff_conv_w': 'grad_w', 'grad_ff_conv_b': 'grad_w', 'grad_ff_w_down': 'grad_w', 'delta_norm_mix': 'delta_w', 'delta_norm_ffn': 'delta_w', 'delta_norm_final': 'delta_w', 'delta_ev_w_in': 'delta_w', 'delta_ev_gm_ln_g': 'delta_w', 'delta_ev_gm_ln_b': 'delta_w', 'delta_ev_gm_ws': 'delta_w', 'delta_ev_gm_bs': 'delta_w', 'delta_ev_conv_w': 'delta_w', 'delta_ev_conv_b': 'delta_w', 'delta_ev_dt_bias': 'delta_w', 'delta_ev_a_log': 'delta_w', 'delta_ev_d_skip': 'delta_w', 'delta_ev_ssm_norm_w': 'delta_w', 'delta_ev_w_out': 'delta_w', 'delta_od_w_in': 'delta_w', 'delta_od_q_norm': 'delta_w', 'delta_od_kv_norm': 'delta_w', 'delta_od_w_uq': 'delta_w', 'delta_od_w_ukv': 'delta_w', 'delta_od_w_o': 'delta_w', 'delta_ff_w_up': 'delta_w', 'delta_ff_conv_w': 'delta_w', 'delta_ff_conv_b': 'delta_w', 'delta_ff_w_down': 'delta_w', 'new_m_norm_mix': 'new_m', 'new_m_norm_ffn': 'new_m', 'new_m_norm_final': 'new_m', 'new_m_ev_w_in': 'new_m', 'new_m_ev_gm_ln_g': 'new_m', 'new_m_ev_gm_ln_b': 'new_m', 'new_m_ev_gm_ws': 'new_m', 'new_m_ev_gm_bs': 'new_m', 'new_m_ev_conv_w': 'new_m', 'new_m_ev_conv_b': 'new_m', 'new_m_ev_dt_bias': 'new_m', 'new_m_ev_a_log': 'new_m', 'new_m_ev_d_skip': 'new_m', 'new_m_ev_ssm_norm_w': 'new_m', 'new_m_ev_w_out': 'new_m', 'new_m_od_w_in': 'new_m', 'new_m_od_q_norm': 'new_m', 'new_m_od_kv_norm': 'new_m', 'new_m_od_w_uq': 'new_m', 'new_m_od_w_ukv': 'new_m', 'new_m_od_w_o': 'new_m', 'new_m_ff_w_up': 'new_m', 'new_m_ff_conv_w': 'new_m', 'new_m_ff_conv_b': 'new_m', 'new_m_ff_w_down': 'new_m', 'new_v_norm_mix': 'new_v', 'new_v_norm_ffn': 'new_v', 'new_v_norm_final': 'new_v', 'new_v_ev_w_in': 'new_v', 'new_v_ev_gm_ln_g': 'new_v', 'new_v_ev_gm_ln_b': 'new_v', 'new_v_ev_gm_ws': 'new_v', 'new_v_ev_gm_bs': 'new_v', 'new_v_ev_conv_w': 'new_v', 'new_v_ev_conv_b': 'new_v', 'new_v_ev_dt_bias': 'new_v', 'new_v_ev_a_log': 'new_v', 'new_v_ev_d_skip': 'new_v', 'new_v_ev_ssm_norm_w': 'new_v', 'new_v_ev_w_out': 'new_v', 'new_v_od_w_in': 'new_v', 'new_v_od_q_norm': 'new_v', 'new_v_od_kv_norm': 'new_v', 'new_v_od_w_uq': 'new_v', 'new_v_od_w_ukv': 'new_v', 'new_v_od_w_o': 'new_v', 'new_v_ff_w_up': 'new_v', 'new_v_ff_conv_w': 'new_v', 'new_v_ff_conv_b': 'new_v', 'new_v_ff_w_down': 'new_v'}


def _forward(args):
    return _fwd_reference(*[args[k] for k in FWD_PARAMS])


def _output_shape():
    out = _jax.eval_shape(lambda: _forward(_fwd_setup_inputs(0)))
    return out.shape, out.dtype

N_MICROBATCH = 1
ADAM_LR = 0.001
ADAM_B1 = 0.9
ADAM_B2 = 0.999
ADAM_EPS = 1e-08
ADAM_WD = 0.01
ADAM_STEP = 10
PER_EXAMPLE_BATCH_AXIS = {'x': 0, 'positions': 0, 'loss_target': 0}
SHARED_INPUTS = []
_WEIGHT_DTYPES = {'norm_mix': _jnp.float32, 'norm_ffn': _jnp.float32, 'norm_final': _jnp.float32, 'ev_w_in': _jnp.float32, 'ev_gm_ln_g': _jnp.float32, 'ev_gm_ln_b': _jnp.float32, 'ev_gm_ws': _jnp.float32, 'ev_gm_bs': _jnp.float32, 'ev_conv_w': _jnp.float32, 'ev_conv_b': _jnp.float32, 'ev_dt_bias': _jnp.float32, 'ev_a_log': _jnp.float32, 'ev_d_skip': _jnp.float32, 'ev_ssm_norm_w': _jnp.float32, 'ev_w_out': _jnp.float32, 'od_w_in': _jnp.float32, 'od_q_norm': _jnp.float32, 'od_kv_norm': _jnp.float32, 'od_w_uq': _jnp.float32, 'od_w_ukv': _jnp.float32, 'od_w_o': _jnp.float32, 'ff_w_up': _jnp.float32, 'ff_conv_w': _jnp.float32, 'ff_conv_b': _jnp.float32, 'ff_w_down': _jnp.float32}
MOMENT_SCALE = {'norm_mix': 5.090660e-02, 'norm_ffn': 3.931259e-02, 'norm_final': 8.027260e+00, 'ev_w_in': 3.313339e-02, 'ev_gm_ln_g': 2.135624e-02, 'ev_gm_ln_b': 2.091696e-02, 'ev_gm_ws': 2.991946e-02, 'ev_gm_bs': 3.529014e-02, 'ev_conv_w': 3.302724e-02, 'ev_conv_b': 4.834344e-02, 'ev_dt_bias': 7.217161e-02, 'ev_a_log': 9.734146e-02, 'ev_d_skip': 1.716533e-01, 'ev_ssm_norm_w': 3.859756e-02, 'ev_w_out': 5.054435e-02, 'od_w_in': 1.883363e-02, 'od_q_norm': 1.255979e-02, 'od_kv_norm': 2.473418e-02, 'od_w_uq': 5.032731e-03, 'od_w_ukv': 8.412245e-03, 'od_w_o': 1.070247e-02, 'ff_w_up': 1.692484e-02, 'ff_conv_w': 1.710704e-02, 'ff_conv_b': 1.641376e-02, 'ff_w_down': 2.765787e-02}


def _to_microbatches(a, axis):
    t = _jnp.moveaxis(a, axis, 0)
    t = t.reshape((N_MICROBATCH, t.shape[0] // N_MICROBATCH) + t.shape[1:])
    return _jnp.moveaxis(t, 1, axis + 1)


def setup_inputs(seed: int = 0) -> dict:
    inp = _fwd_setup_inputs(seed)
    key = _jax.random.fold_in(_jax.random.key(seed), 7919)
    shape, _ = _output_shape()
    out = dict(inp)
    out["loss_target"] = _jax.random.normal(_jax.random.fold_in(key, 0), shape, _jnp.float32)
    for i, name in enumerate(TWIN_WEIGHTS):
        w = inp[name].astype(_jnp.float32)
        if MOMENT_SCALE is None:
            s = _jnp.sqrt(_jnp.mean(_jnp.square(w)) + 1e-30)
        else:
            s = MOMENT_SCALE[name]
        km, kv = _jax.random.split(_jax.random.fold_in(key, i + 1))
        out[name] = w
        out["m_" + name] = s * _jax.random.normal(km, w.shape, _jnp.float32)
        out["v_" + name] = (s * s) * _jax.random.uniform(kv, w.shape, _jnp.float32, 0.5, 1.5)
    if N_MICROBATCH > 1:
        for name, axis in PER_EXAMPLE_BATCH_AXIS.items():
            out[name] = _to_microbatches(out[name], axis)
    return {'x': out['x'], 'positions': out['positions'], 'norm_mix': out['norm_mix'], 'norm_ffn': out['norm_ffn'], 'norm_final': out['norm_final'], 'ev_w_in': out['ev_w_in'], 'ev_gm_ln_g': out['ev_gm_ln_g'], 'ev_gm_ln_b': out['ev_gm_ln_b'], 'ev_gm_ws': out['ev_gm_ws'], 'ev_gm_bs': out['ev_gm_bs'], 'ev_conv_w': out['ev_conv_w'], 'ev_conv_b': out['ev_conv_b'], 'ev_dt_bias': out['ev_dt_bias'], 'ev_a_log': out['ev_a_log'], 'ev_d_skip': out['ev_d_skip'], 'ev_ssm_norm_w': out['ev_ssm_norm_w'], 'ev_w_out': out['ev_w_out'], 'od_w_in': out['od_w_in'], 'od_q_norm': out['od_q_norm'], 'od_kv_norm': out['od_kv_norm'], 'od_w_uq': out['od_w_uq'], 'od_w_ukv': out['od_w_ukv'], 'od_w_o': out['od_w_o'], 'ff_w_up': out['ff_w_up'], 'ff_conv_w': out['ff_conv_w'], 'ff_conv_b': out['ff_conv_b'], 'ff_w_down': out['ff_w_down'], 'loss_target': out['loss_target'], 'm_norm_mix': out['m_norm_mix'], 'm_norm_ffn': out['m_norm_ffn'], 'm_norm_final': out['m_norm_final'], 'm_ev_w_in': out['m_ev_w_in'], 'm_ev_gm_ln_g': out['m_ev_gm_ln_g'], 'm_ev_gm_ln_b': out['m_ev_gm_ln_b'], 'm_ev_gm_ws': out['m_ev_gm_ws'], 'm_ev_gm_bs': out['m_ev_gm_bs'], 'm_ev_conv_w': out['m_ev_conv_w'], 'm_ev_conv_b': out['m_ev_conv_b'], 'm_ev_dt_bias': out['m_ev_dt_bias'], 'm_ev_a_log': out['m_ev_a_log'], 'm_ev_d_skip': out['m_ev_d_skip'], 'm_ev_ssm_norm_w': out['m_ev_ssm_norm_w'], 'm_ev_w_out': out['m_ev_w_out'], 'm_od_w_in': out['m_od_w_in'], 'm_od_q_norm': out['m_od_q_norm'], 'm_od_kv_norm': out['m_od_kv_norm'], 'm_od_w_uq': out['m_od_w_uq'], 'm_od_w_ukv': out['m_od_w_ukv'], 'm_od_w_o': out['m_od_w_o'], 'm_ff_w_up': out['m_ff_w_up'], 'm_ff_conv_w': out['m_ff_conv_w'], 'm_ff_conv_b': out['m_ff_conv_b'], 'm_ff_w_down': out['m_ff_w_down'], 'v_norm_mix': out['v_norm_mix'], 'v_norm_ffn': out['v_norm_ffn'], 'v_norm_final': out['v_norm_final'], 'v_ev_w_in': out['v_ev_w_in'], 'v_ev_gm_ln_g': out['v_ev_gm_ln_g'], 'v_ev_gm_ln_b': out['v_ev_gm_ln_b'], 'v_ev_gm_ws': out['v_ev_gm_ws'], 'v_ev_gm_bs': out['v_ev_gm_bs'], 'v_ev_conv_w': out['v_ev_conv_w'], 'v_ev_conv_b': out['v_ev_conv_b'], 'v_ev_dt_bias': out['v_ev_dt_bias'], 'v_ev_a_log': out['v_ev_a_log'], 'v_ev_d_skip': out['v_ev_d_skip'], 'v_ev_ssm_norm_w': out['v_ev_ssm_norm_w'], 'v_ev_w_out': out['v_ev_w_out'], 'v_od_w_in': out['v_od_w_in'], 'v_od_q_norm': out['v_od_q_norm'], 'v_od_kv_norm': out['v_od_kv_norm'], 'v_od_w_uq': out['v_od_w_uq'], 'v_od_w_ukv': out['v_od_w_ukv'], 'v_od_w_o': out['v_od_w_o'], 'v_ff_w_up': out['v_ff_w_up'], 'v_ff_conv_w': out['v_ff_conv_w'], 'v_ff_conv_b': out['v_ff_conv_b'], 'v_ff_w_down': out['v_ff_w_down']}


def _loss(weights, diff, rest, loss_target):
    with _jax.named_scope("forward"):
        args = {**rest, TWIN_DIFF_INPUT: diff, **{k: w.astype(_WEIGHT_DTYPES[k]) for k, w in weights.items()}}
        y = _forward(args)
    with _jax.named_scope("loss_head"):
        err = _jnp.square(y.astype(_jnp.float32) - loss_target)
        return 0.5 * _jnp.sum(_jnp.mean(err, axis=-1)) if err.ndim else 0.5 * err


def _adamw(w, g, m, v):
    m = ADAM_B1 * m + (1.0 - ADAM_B1) * g
    v = ADAM_B2 * v + (1.0 - ADAM_B2) * _jnp.square(g)
    m_hat = m / (1.0 - ADAM_B1 ** ADAM_STEP)
    v_hat = v / (1.0 - ADAM_B2 ** ADAM_STEP)
    delta = -ADAM_LR * (m_hat / (_jnp.sqrt(v_hat) + ADAM_EPS) + ADAM_WD * w)
    return delta, m, v


def reference(x, positions, norm_mix, norm_ffn, norm_final, ev_w_in, ev_gm_ln_g, ev_gm_ln_b, ev_gm_ws, ev_gm_bs, ev_conv_w, ev_conv_b, ev_dt_bias, ev_a_log, ev_d_skip, ev_ssm_norm_w, ev_w_out, od_w_in, od_q_norm, od_kv_norm, od_w_uq, od_w_ukv, od_w_o, ff_w_up, ff_conv_w, ff_conv_b, ff_w_down, loss_target, m_norm_mix, m_norm_ffn, m_norm_final, m_ev_w_in, m_ev_gm_ln_g, m_ev_gm_ln_b, m_ev_gm_ws, m_ev_gm_bs, m_ev_conv_w, m_ev_conv_b, m_ev_dt_bias, m_ev_a_log, m_ev_d_skip, m_ev_ssm_norm_w, m_ev_w_out, m_od_w_in, m_od_q_norm, m_od_kv_norm, m_od_w_uq, m_od_w_ukv, m_od_w_o, m_ff_w_up, m_ff_conv_w, m_ff_conv_b, m_ff_w_down, v_norm_mix, v_norm_ffn, v_norm_final, v_ev_w_in, v_ev_gm_ln_g, v_ev_gm_ln_b, v_ev_gm_ws, v_ev_gm_bs, v_ev_conv_w, v_ev_conv_b, v_ev_dt_bias, v_ev_a_log, v_ev_d_skip, v_ev_ssm_norm_w, v_ev_w_out, v_od_w_in, v_od_q_norm, v_od_kv_norm, v_od_w_uq, v_od_w_ukv, v_od_w_o, v_ff_w_up, v_ff_conv_w, v_ff_conv_b, v_ff_w_down):
    given = dict(x=x, positions=positions, norm_mix=norm_mix, norm_ffn=norm_ffn, norm_final=norm_final, ev_w_in=ev_w_in, ev_gm_ln_g=ev_gm_ln_g, ev_gm_ln_b=ev_gm_ln_b, ev_gm_ws=ev_gm_ws, ev_gm_bs=ev_gm_bs, ev_conv_w=ev_conv_w, ev_conv_b=ev_conv_b, ev_dt_bias=ev_dt_bias, ev_a_log=ev_a_log, ev_d_skip=ev_d_skip, ev_ssm_norm_w=ev_ssm_norm_w, ev_w_out=ev_w_out, od_w_in=od_w_in, od_q_norm=od_q_norm, od_kv_norm=od_kv_norm, od_w_uq=od_w_uq, od_w_ukv=od_w_ukv, od_w_o=od_w_o, ff_w_up=ff_w_up, ff_conv_w=ff_conv_w, ff_conv_b=ff_conv_b, ff_w_down=ff_w_down, loss_target=loss_target, m_norm_mix=m_norm_mix, m_norm_ffn=m_norm_ffn, m_norm_final=m_norm_final, m_ev_w_in=m_ev_w_in, m_ev_gm_ln_g=m_ev_gm_ln_g, m_ev_gm_ln_b=m_ev_gm_ln_b, m_ev_gm_ws=m_ev_gm_ws, m_ev_gm_bs=m_ev_gm_bs, m_ev_conv_w=m_ev_conv_w, m_ev_conv_b=m_ev_conv_b, m_ev_dt_bias=m_ev_dt_bias, m_ev_a_log=m_ev_a_log, m_ev_d_skip=m_ev_d_skip, m_ev_ssm_norm_w=m_ev_ssm_norm_w, m_ev_w_out=m_ev_w_out, m_od_w_in=m_od_w_in, m_od_q_norm=m_od_q_norm, m_od_kv_norm=m_od_kv_norm, m_od_w_uq=m_od_w_uq, m_od_w_ukv=m_od_w_ukv, m_od_w_o=m_od_w_o, m_ff_w_up=m_ff_w_up, m_ff_conv_w=m_ff_conv_w, m_ff_conv_b=m_ff_conv_b, m_ff_w_down=m_ff_w_down, v_norm_mix=v_norm_mix, v_norm_ffn=v_norm_ffn, v_norm_final=v_norm_final, v_ev_w_in=v_ev_w_in, v_ev_gm_ln_g=v_ev_gm_ln_g, v_ev_gm_ln_b=v_ev_gm_ln_b, v_ev_gm_ws=v_ev_gm_ws, v_ev_gm_bs=v_ev_gm_bs, v_ev_conv_w=v_ev_conv_w, v_ev_conv_b=v_ev_conv_b, v_ev_dt_bias=v_ev_dt_bias, v_ev_a_log=v_ev_a_log, v_ev_d_skip=v_ev_d_skip, v_ev_ssm_norm_w=v_ev_ssm_norm_w, v_ev_w_out=v_ev_w_out, v_od_w_in=v_od_w_in, v_od_q_norm=v_od_q_norm, v_od_kv_norm=v_od_kv_norm, v_od_w_uq=v_od_w_uq, v_od_w_ukv=v_od_w_ukv, v_od_w_o=v_od_w_o, v_ff_w_up=v_ff_w_up, v_ff_conv_w=v_ff_conv_w, v_ff_conv_b=v_ff_conv_b, v_ff_w_down=v_ff_w_down)
    weights = {n: given[n] for n in TWIN_WEIGHTS}
    shared = {n: given[n] for n in SHARED_INPUTS}
    per_example = {n: given[n] for n in ['x', 'positions']}
    grad_fn = _jax.value_and_grad(_loss, argnums=(0, 1))

    def one_microbatch(ex, loss_target):
        ex = dict(ex)
        diff = ex.pop(TWIN_DIFF_INPUT)
        return grad_fn(weights, diff, {**shared, **ex}, loss_target)

    if N_MICROBATCH == 1:
        loss, (grad_w, grad_x) = one_microbatch(per_example, given["loss_target"])
    else:
        def body(carry, xs):
            loss_sum, grad_sum = carry
            l_k, (gw_k, gx_k) = one_microbatch(xs[0], xs[1])
            with _jax.named_scope("update"):
                return (loss_sum + l_k, _jax.tree.map(_jnp.add, grad_sum, gw_k)), gx_k

        init = (_jnp.zeros((), _jnp.float32), _jax.tree.map(_jnp.zeros_like, weights))
        (loss, grad_w), grad_x = _jax.lax.scan(body, init, (per_example, given["loss_target"]))
    with _jax.named_scope("update"):
        delta_w, new_m, new_v = {}, {}, {}
        for n in TWIN_WEIGHTS:
            delta_w[n], new_m[n], new_v[n] = _adamw(weights[n], grad_w[n], given["m_" + n], given["v_" + n])
    return (loss, grad_x, *[grad_w[n] for n in TWIN_WEIGHTS], *[delta_w[n] for n in TWIN_WEIGHTS],
            *[new_m[n] for n in TWIN_WEIGHTS], *[new_v[n] for n in TWIN_WEIGHTS])
```

```python
import functools
import math

import jax
import jax.numpy as jnp
from jax import lax
from jax.experimental import pallas as pl
from jax.experimental.pallas import tpu as pltpu

F32 = jnp.float32
BF16 = jnp.bfloat16
MESH = pl.DeviceIdType.MESH

V7X_VMEM_LIMIT_BYTES = 56 * 1024 * 1024
LANES = 128

EPS = 1e-6
D_MODEL = 2048
CHUNK = 64
GM_BLOCK = 128
GM_GROUPS = 8
GM_GROUP_DIM = D_MODEL // GM_GROUPS
SSM_HEADS = 32
SSM_HEAD_DIM = 64
SSM_GROUPS = 4
SSM_STATE = 128
SSM_CONV = 4
SSM_BC = SSM_GROUPS * SSM_STATE
SSM_CONV_DIM = D_MODEL + 2 * SSM_BC
SSD_HEADS_PER_STEP = 4
SSD_STEPS = SSM_HEADS // SSD_HEADS_PER_STEP
SSD_X_WIDTH = SSD_HEADS_PER_STEP * SSM_HEAD_DIM
MLA_HEADS = 16
MLA_RANK = 512
MLA_NOPE = 128
MLA_ROPE = 64
MLA_V = 128
MLA_QK = MLA_NOPE + MLA_ROPE
MLA_QPAD = 2 * LANES
ODD_IN = 2 * MLA_RANK + MLA_ROPE
ODD_IN_PAD = 2 * MLA_RANK + LANES
D_FF = 5632
ROPE_THETA = 10000.0
N_DEV = 8

ADAM_LR, ADAM_B1, ADAM_B2, ADAM_EPS, ADAM_WD, ADAM_STEP = 0.001, 0.9, 0.999, 1e-08, 0.01, 10


def _params(*sem):
    return pltpu.CompilerParams(dimension_semantics=sem, vmem_limit_bytes=V7X_VMEM_LIMIT_BYTES)


def _pick(dim, target):
    if dim <= target:
        return dim
    t = (target // LANES) * LANES
    while t >= LANES:
        if dim % t == 0:
            return t
        t -= LANES
    raise ValueError(f"no tile for {dim} under {target}")


def matmul(name, a, b, *, ta=False, tb=False, res=None, out_dtype=F32, tm=1024, tn=512, tk=2048):
    m, k = (a.shape[1], a.shape[0]) if ta else a.shape
    n = b.shape[0] if tb else b.shape[1]
    assert k == (b.shape[1] if tb else b.shape[0]), (name, a.shape, b.shape)
    tm, tn, tk = _pick(m, tm), _pick(n, tn), _pick(k, tk)
    nk = k // tk
    dims = (((0 if ta else 1,), (1 if tb else 0,)), ((), ()))

    def body(*refs):
        a_ref, b_ref = refs[0], refs[1]
        r_ref = refs[2] if res is not None else None
        o_ref = refs[3] if res is not None else refs[2]
        part = lax.dot_general(a_ref[...].astype(BF16), b_ref[...].astype(BF16), dims, preferred_element_type=F32)

        def finish(total):
            if r_ref is not None:
                total = total + r_ref[...]
            o_ref[...] = total.astype(o_ref.dtype)

        if nk == 1:
            finish(part)
        else:
            acc = refs[-1]
            kk = pl.program_id(2)

            @pl.when(kk == 0)
            def _():
                acc[...] = part

            @pl.when(kk > 0)
            def _():
                acc[...] += part

            @pl.when(kk == nk - 1)
            def _():
                finish(acc[...])

    a_spec = pl.BlockSpec((tk, tm), lambda i, j, kk: (kk, i)) if ta else pl.BlockSpec((tm, tk), lambda i, j, kk: (i, kk))
    b_spec = pl.BlockSpec((tn, tk), lambda i, j, kk: (j, kk)) if tb else pl.BlockSpec((tk, tn), lambda i, j, kk: (kk, j))
    o_spec = pl.BlockSpec((tm, tn), lambda i, j, kk: (i, j))
    ins, specs = [a, b], [a_spec, b_spec]
    if res is not None:
        ins.append(res)
        specs.append(o_spec)
    return pl.pallas_call(
        body, name=name, grid=(m // tm, n // tn, nk), in_specs=specs, out_specs=o_spec,
        out_shape=jax.ShapeDtypeStruct((m, n), out_dtype),
        scratch_shapes=[pltpu.VMEM((tm, tn), F32)] if nk > 1 else [],
        compiler_params=_params("parallel", "parallel", "arbitrary"),
    )(*ins)


@functools.partial(jax.custom_vjp, nondiff_argnums=(1, 2))
def _roll(x, shift, axis):
    return pltpu.roll(x, shift, axis)


def _roll_fwd(x, shift, axis):
    return pltpu.roll(x, shift, axis), None


def _roll_bwd(shift, axis, _, g):
    return (pltpu.roll(g, (g.shape[axis] - shift) % g.shape[axis], axis),)


_roll.defvjp(_roll_fwd, _roll_bwd)


def _shift_down(x, s):
    rows = lax.broadcasted_iota(jnp.int32, x.shape, 0)
    return jnp.where(rows >= s, _roll(x, s, 0), 0.0)


def _dwconv(x, w, b):
    taps = w.shape[0]
    y = b + w[taps - 1:taps, :] * x
    for kk in range(taps - 1):
        y = y + w[kk:kk + 1, :] * _shift_down(x, taps - 1 - kk)
    return y


def _rms(x, w):
    return x * lax.rsqrt(jnp.mean(x * x, -1, keepdims=True) + EPS) * w


def _rms_f(h, w):
    return (_rms(h, w),)


def _gmlp_f(uv, lng, lnb, ws, bst):
    r = lax.broadcasted_iota(jnp.int32, (GM_BLOCK, GM_BLOCK), 0) // CHUNK
    c = lax.broadcasted_iota(jnp.int32, (GM_BLOCK, GM_BLOCK), 1) // CHUNK
    outs = []
    for g in range(GM_GROUPS):
        lo, hi = g * GM_GROUP_DIM, (g + 1) * GM_GROUP_DIM
        gu = jax.nn.gelu(uv[:, lo:hi])
        gv = jax.nn.gelu(uv[:, D_MODEL + lo:D_MODEL + hi])
        xc = gv - jnp.mean(gv, -1, keepdims=True)
        var = jnp.mean(xc * xc, -1, keepdims=True)
        vn = xc * lax.rsqrt(var + EPS) * lng[:, lo:hi] + lnb[:, lo:hi]
        wm = jnp.where(r >= c, ws[g], 0.0).astype(BF16)
        gate = jnp.dot(wm, vn.astype(BF16), preferred_element_type=F32) + bst[:, g:g + 1]
        outs.append(gu * gate)
    return (jnp.concatenate(outs, axis=1),)


def _conv_silu_f(x, w, b):
    return (jax.nn.silu(_dwconv(x, w, b)),)


def _ffn_act_f(g, val, w, b):
    return (jax.nn.gelu(_dwconv(g, w, b)) * val,)


def _gate_norm_f(y, z, nw):
    y2 = y * jax.nn.silu(z)
    width = D_MODEL // SSM_GROUPS
    outs = []
    for g in range(SSM_GROUPS):
        blk = y2[:, g * width:(g + 1) * width]
        outs.append(blk * lax.rsqrt(jnp.mean(blk * blk, -1, keepdims=True) + EPS))
    return (jnp.concatenate(outs, axis=1) * nw,)


def _rope(x, cos, sin):
    lane = lax.broadcasted_iota(jnp.int32, x.shape, 1)
    half = MLA_ROPE // 2
    swapped = jnp.where(lane < half, _roll(x, LANES - half, 1), _roll(x, half, 1))
    return x * cos + swapped * sin


def _qkv_norm_f(proj, cos, sin, qn, kvn):
    cq = _rms(proj[:, :MLA_RANK], qn)
    ckv = _rms(proj[:, MLA_RANK:2 * MLA_RANK], kvn)
    kpe = _rope(proj[:, 2 * MLA_RANK:], cos, sin)
    return cq, ckv, kpe


def _attn_f(q0, qh, kn, kpe, v, cos, sin):
    qn = qh[:, :MLA_NOPE]
    qp = _rope(qh[:, MLA_NOPE:], cos, sin)
    nt = (((1,), (1,)), ((), ()))
    s = lax.dot_general(qn.astype(BF16), kn.astype(BF16), nt, preferred_element_type=F32)
    s = s + lax.dot_general(qp.astype(BF16), kpe.astype(BF16), nt, preferred_element_type=F32)
    s = s * (MLA_QK ** -0.5)
    qc = (q0 + lax.broadcasted_iota(jnp.int32, s.shape, 0)) // CHUNK
    kc = lax.broadcasted_iota(jnp.int32, s.shape, 1) // CHUNK
    s = jnp.where(kc <= qc, s, -jnp.inf)
    p = jax.nn.softmax(s, axis=-1)
    return (jnp.dot(p.astype(BF16), v.astype(BF16), preferred_element_type=F32),)


def _ssd_chunk_f(x, bm, cm, pdt, hp, sprev):
    nh = SSD_HEADS_PER_STEP
    dt = jax.nn.softplus(pdt[:, :nh] + hp[0:1, :nh])
    da = dt * (-jnp.exp(hp[1:2, :nh]))
    r = lax.broadcasted_iota(jnp.int32, (CHUNK, CHUNK), 0)
    c = lax.broadcasted_iota(jnp.int32, (CHUNK, CHUNK), 1)
    tril = r >= c
    cs = jnp.dot(tril.astype(F32), da, precision=lax.Precision.HIGHEST, preferred_element_type=F32)
    cst = cs.T
    nt = (((1,), (1,)), ((), ()))
    tn = (((0,), (0,)), ((), ()))
    cb = lax.dot_general(cm.astype(BF16), bm.astype(BF16), nt, preferred_element_type=F32)
    ys, snew = [], []
    for e in range(nh):
        xe = x[:, e * SSM_HEAD_DIM:(e + 1) * SSM_HEAD_DIM]
        xd = xe * dt[:, e:e + 1]
        cse = cs[:, e:e + 1]
        decay = jnp.exp(jnp.where(tril, cse - cst[e:e + 1, :], -jnp.inf))
        y = jnp.dot((cb * decay).astype(BF16), xd.astype(BF16), preferred_element_type=F32)
        tot = cse[CHUNK - 1:CHUNK, :]
        st = lax.dot_general((xd * jnp.exp(tot - cse)).astype(BF16), bm.astype(BF16), tn, preferred_element_type=F32)
        yoff = lax.dot_general(cm.astype(BF16), sprev[e].astype(BF16), nt, preferred_element_type=F32)
        ys.append(y + yoff * jnp.exp(cse) + hp[2:3, e:e + 1] * xe)
        snew.append((jnp.exp(tot) * sprev[e] + st)[None])
    return jnp.concatenate(ys, axis=1), jnp.concatenate(snew, axis=0)


def _full_spec(a):
    nd = a.ndim
    return pl.BlockSpec(a.shape, lambda i, nd=nd: (0,) * nd)


def rows_fwd(name, f, rows, params, outs, tr):
    t = rows[0][0].shape[0]
    nr, npar = len(rows), len(params)

    def body(*refs):
        vals = f(*[x[...].astype(F32) for x in refs[:nr + npar]])
        for o_ref, val in zip(refs[nr + npar:], vals):
            o_ref[...] = val.astype(o_ref.dtype)

    in_specs = [pl.BlockSpec((tr, w), lambda i, cb=cb: (i, cb)) for _, w, cb in rows] + [_full_spec(p) for p in params]
    out = pl.pallas_call(
        body, name=name, grid=(t // tr,), in_specs=in_specs,
        out_specs=[pl.BlockSpec((tr, w), lambda i: (i, 0)) for w, _ in outs],
        out_shape=[jax.ShapeDtypeStruct((t, w), dt) for w, dt in outs],
        compiler_params=_params("parallel"),
    )(*[a for a, _, _ in rows], *params)
    return out


def rows_bwd(name, f, rows, params, cots, tr, d_dtypes, n_nondiff=0, add=None):
    t = rows[0][0].shape[0]
    nr, npar, nc = len(rows), len(params), len(cots)
    nd = nr - n_nondiff
    has_add = add is not None

    def body(*refs):
        i = pl.program_id(0)
        row_vals = [x[...].astype(F32) for x in refs[:nr]]
        par_vals = [x[...].astype(F32) for x in refs[nr:nr + npar]]
        cot_refs = refs[nr + npar:nr + npar + nc]
        pos = nr + npar + nc
        add_ref = refs[pos] if has_add else None
        pos += int(has_add)
        drow_refs = refs[pos:pos + nd]
        dpar_refs = refs[pos + nd:]

        def g(*diff):
            return f(*diff[:nd], *row_vals[nd:], *diff[nd:])

        _, vjp = jax.vjp(g, *row_vals[:nd], *par_vals)
        grads = vjp(tuple(cr[...].astype(F32) for cr in cot_refs))
        for j, d_ref in enumerate(drow_refs):
            val = grads[j]
            if j == 0 and has_add:
                val = val + add_ref[...]
            d_ref[...] = val.astype(d_ref.dtype)
        for j, d_ref in enumerate(dpar_refs):
            @pl.when(i == 0)
            def _(d_ref=d_ref, j=j):
                d_ref[...] = grads[nd + j]

            @pl.when(i > 0)
            def _(d_ref=d_ref, j=j):
                d_ref[...] += grads[nd + j]

    in_specs = [pl.BlockSpec((tr, w), lambda i, cb=cb: (i, cb)) for _, w, cb in rows] + [_full_spec(p) for p in params]
    in_specs += [pl.BlockSpec((tr, w), lambda i, cb=cb: (i, cb)) for _, w, cb in cots]
    ins = [a for a, _, _ in rows] + list(params) + [a for a, _, _ in cots]
    if has_add:
        in_specs.append(pl.BlockSpec((tr, rows[0][1]), lambda i: (i, 0)))
        ins.append(add)
    out_specs = [pl.BlockSpec((tr, rows[j][1]), lambda i: (i, 0)) for j in range(nd)] + [_full_spec(p) for p in params]
    out_shape = [jax.ShapeDtypeStruct((t, rows[j][1]), d_dtypes[j]) for j in range(nd)]
    out_shape += [jax.ShapeDtypeStruct(p.shape, F32) for p in params]
    out = pl.pallas_call(
        body, name=name, grid=(t // tr,), in_specs=in_specs, out_specs=out_specs, out_shape=out_shape,
        compiler_params=_params("arbitrary"),
    )(*ins)
    return out[:nd], out[nd:]


def cols_fwd(name, f, cols, cparams, out_dtype, tc):
    t = cols[0][0].shape[0]
    width = cparams[0].shape[1]
    ncol = len(cols)

    def body(*refs):
        (val,) = f(*[x[...].astype(F32) for x in refs[:-1]])
        refs[-1][...] = val.astype(refs[-1].dtype)

    in_specs = [pl.BlockSpec((t, tc), lambda j, o=o: (0, o + j)) for _, o in cols]
    in_specs += [pl.BlockSpec((p.shape[0], tc), lambda j: (0, j)) for p in cparams]
    return pl.pallas_call(
        body, name=name, grid=(width // tc,), in_specs=in_specs,
        out_specs=pl.BlockSpec((t, tc), lambda j: (0, j)),
        out_shape=jax.ShapeDtypeStruct((t, width), out_dtype),
        compiler_params=_params("parallel"),
    )(*[a for a, _ in cols], *cparams)


def cols_bwd(name, f, cols, cparams, cot, tc, d_dtype):
    t = cols[0][0].shape[0]
    width = cparams[0].shape[1]
    ncol, npar = len(cols), len(cparams)

    def body(*refs):
        vals = [x[...].astype(F32) for x in refs[:ncol + npar]]
        _, vjp = jax.vjp(f, *vals)
        grads = vjp((refs[ncol + npar][...].astype(F32),))
        for d_ref, gval in zip(refs[ncol + npar + 1:], grads):
            d_ref[...] = gval.astype(d_ref.dtype)

    in_specs = [pl.BlockSpec((t, tc), lambda j, o=o: (0, o + j)) for _, o in cols]
    in_specs += [pl.BlockSpec((p.shape[0], tc), lambda j: (0, j)) for p in cparams]
    in_specs.append(pl.BlockSpec((t, tc), lambda j: (0, j)))
    out_specs = [pl.BlockSpec((t, tc), lambda j: (0, j)) for _ in cols]
    out_specs += [pl.BlockSpec((p.shape[0], tc), lambda j: (0, j)) for p in cparams]
    out_shape = [jax.ShapeDtypeStruct((t, width), d_dtype) for _ in cols]
    out_shape += [jax.ShapeDtypeStruct(p.shape, F32) for p in cparams]
    out = pl.pallas_call(
        body, name=name, grid=(width // tc,), in_specs=in_specs, out_specs=out_specs, out_shape=out_shape,
        compiler_params=_params("parallel"),
    )(*[a for a, _ in cols], *cparams, cot)
    return out[:ncol], out[ncol:]


def _ssd_in_specs(t):
    heads_per_group = SSM_HEADS // SSM_GROUPS
    steps_per_group = heads_per_group // SSD_HEADS_PER_STEP
    b_blk = D_MODEL // LANES
    c_blk = (D_MODEL + SSM_BC) // LANES
    return [
        pl.BlockSpec((t, SSD_X_WIDTH), lambda s: (0, s)),
        pl.BlockSpec((t, LANES), lambda s: (0, b_blk + s // steps_per_group)),
        pl.BlockSpec((t, LANES), lambda s: (0, c_blk + s // steps_per_group)),
        pl.BlockSpec((t, LANES), lambda s: (0, s)),
        pl.BlockSpec((3, LANES), lambda s: (0, s)),
    ]


def ssd_fwd(name, xa, pdt, hp):
    t = xa.shape[0]
    nc = t // CHUNK
    nh = SSD_HEADS_PER_STEP

    def body(x_ref, b_ref, c_ref, pdt_ref, hp_ref, y_ref, st_ref, s_scr):
        s_scr[...] = jnp.zeros_like(s_scr)

        def step(ci, carry):
            sl = pl.ds(pl.multiple_of(ci * CHUNK, CHUNK), CHUNK)
            sprev = s_scr[...]
            st_ref[0, ci] = sprev
            y, snew = _ssd_chunk_f(x_ref[sl, :], b_ref[sl, :], c_ref[sl, :], pdt_ref[sl, :], hp_ref[...], sprev)
            y_ref[sl, :] = y
            s_scr[...] = snew
            return carry

        lax.fori_loop(0, nc, step, 0)

    return pl.pallas_call(
        body, name=name, grid=(SSD_STEPS,), in_specs=_ssd_in_specs(t),
        out_specs=[pl.BlockSpec((t, SSD_X_WIDTH), lambda s: (0, s)),
                   pl.BlockSpec((1, nc, nh, SSM_HEAD_DIM, SSM_STATE), lambda s: (s, 0, 0, 0, 0))],
        out_shape=[jax.ShapeDtypeStruct((t, D_MODEL), F32),
                   jax.ShapeDtypeStruct((SSD_STEPS, nc, nh, SSM_HEAD_DIM, SSM_STATE), F32)],
        scratch_shapes=[pltpu.VMEM((nh, SSM_HEAD_DIM, SSM_STATE), F32)],
        compiler_params=_params("parallel"),
    )(xa, xa, xa, pdt, hp)


def ssd_bwd(name, xa, pdt, hp, states, dy):
    t = xa.shape[0]
    nc = t // CHUNK
    nh = SSD_HEADS_PER_STEP
    steps_per_group = SSM_HEADS // SSM_GROUPS // nh

    def body(x_ref, b_ref, c_ref, pdt_ref, hp_ref, st_ref, dy_ref, dx_ref, db_ref, dc_ref, dpdt_ref, dhp_ref, ds_scr, dhp_scr):
        first = pl.program_id(0) % steps_per_group == 0
        ds_scr[...] = jnp.zeros_like(ds_scr)
        dhp_scr[...] = jnp.zeros_like(dhp_scr)

        def step(i, carry):
            ci = nc - 1 - i
            sl = pl.ds(pl.multiple_of(ci * CHUNK, CHUNK), CHUNK)
            _, vjp = jax.vjp(_ssd_chunk_f, x_ref[sl, :], b_ref[sl, :], c_ref[sl, :], pdt_ref[sl, :], hp_ref[...], st_ref[0, ci])
            dx, db, dc, dpdt, dhp, dsprev = vjp((dy_ref[sl, :], ds_scr[...]))
            dx_ref[sl, :] = dx
            dpdt_ref[sl, :] = dpdt.astype(dpdt_ref.dtype)

            @pl.when(first)
            def _():
                db_ref[sl, :] = db
                dc_ref[sl, :] = dc

            @pl.when(jnp.logical_not(first))
            def _():
                db_ref[sl, :] += db
                dc_ref[sl, :] += dc

            ds_scr[...] = dsprev
            dhp_scr[...] += dhp
            return carry

        lax.fori_loop(0, nc, step, 0)
        dhp_ref[...] = dhp_scr[...]

    in_specs = _ssd_in_specs(t) + [
        pl.BlockSpec((1, nc, nh, SSM_HEAD_DIM, SSM_STATE), lambda s: (s, 0, 0, 0, 0)),
        pl.BlockSpec((t, SSD_X_WIDTH), lambda s: (0, s)),
    ]
    out_specs = [
        pl.BlockSpec((t, SSD_X_WIDTH), lambda s: (0, s)),
        pl.BlockSpec((t, LANES), lambda s: (0, s // steps_per_group)),
        pl.BlockSpec((t, LANES), lambda s: (0, s // steps_per_group)),
        pl.BlockSpec((t, LANES), lambda s: (0, s)),
        pl.BlockSpec((3, LANES), lambda s: (0, s)),
    ]
    out_shape = [
        jax.ShapeDtypeStruct((t, D_MODEL), F32),
        jax.ShapeDtypeStruct((t, SSM_BC), F32),
        jax.ShapeDtypeStruct((t, SSM_BC), F32),
        jax.ShapeDtypeStruct((t, SSD_STEPS * LANES), BF16),
        jax.ShapeDtypeStruct((3, SSD_STEPS * LANES), F32),
    ]
    return pl.pallas_call(
        body, name=name, grid=(SSD_STEPS,), in_specs=in_specs, out_specs=out_specs, out_shape=out_shape,
        scratch_shapes=[pltpu.VMEM((nh, SSM_HEAD_DIM, SSM_STATE), F32), pltpu.VMEM((3, LANES), F32)],
        compiler_params=_params("arbitrary"),
    )(xa, xa, xa, pdt, hp, states, dy)


ATTN_TQ = 256


def _attn_in_specs(t):
    return [
        pl.BlockSpec((ATTN_TQ, MLA_QPAD), lambda h, qi: (qi, h)),
        pl.BlockSpec((t, MLA_NOPE), lambda h, qi: (0, h)),
        pl.BlockSpec((t, LANES), lambda h, qi: (0, 0)),
        pl.BlockSpec((t, MLA_V), lambda h, qi: (0, h)),
        pl.BlockSpec((ATTN_TQ, LANES), lambda h, qi: (qi, 0)),
        pl.BlockSpec((ATTN_TQ, LANES), lambda h, qi: (qi, 0)),
    ]


def attn_fwd(name, q, kn, kpe, v, cos, sin):
    t = q.shape[0]

    def body(q_ref, kn_ref, kpe_ref, v_ref, cos_ref, sin_ref, o_ref):
        q0 = pl.program_id(1) * ATTN_TQ
        (o,) = _attn_f(q0, q_ref[...], kn_ref[...], kpe_ref[...], v_ref[...], cos_ref[...], sin_ref[...])
        o_ref[...] = o.astype(o_ref.dtype)

    return pl.pallas_call(
        body, name=name, grid=(MLA_HEADS, t // ATTN_TQ), in_specs=_attn_in_specs(t),
        out_specs=pl.BlockSpec((ATTN_TQ, MLA_V), lambda h, qi: (qi, h)),
        out_shape=jax.ShapeDtypeStruct((t, MLA_HEADS * MLA_V), BF16),
        compiler_params=_params("parallel", "parallel"),
    )(q, kn, kpe, v, cos, sin)


def attn_bwd(name, q, kn, kpe, v, cos, sin, do):
    t = q.shape[0]

    def body(q_ref, kn_ref, kpe_ref, v_ref, cos_ref, sin_ref, do_ref, dq_ref, dkn_ref, dkpe_ref, dv_ref):
        h, qi = pl.program_id(0), pl.program_id(1)
        q0 = qi * ATTN_TQ
        cos, sin = cos_ref[...], sin_ref[...]

        def g(qh, knv, kpev, vv):
            return _attn_f(q0, qh, knv, kpev, vv, cos, sin)

        _, vjp = jax.vjp(g, q_ref[...].astype(F32), kn_ref[...].astype(F32), kpe_ref[...].astype(F32), v_ref[...].astype(F32))
        dq, dkn, dkpe, dv = vjp((do_ref[...].astype(F32),))
        dq_ref[...] = dq.astype(dq_ref.dtype)

        @pl.when(qi == 0)
        def _():
            dkn_ref[...] = dkn
            dv_ref[...] = dv

        @pl.when(qi > 0)
        def _():
            dkn_ref[...] += dkn
            dv_ref[...] += dv

        start = jnp.logical_and(h == 0, qi == 0)

        @pl.when(start)
        def _():
            dkpe_ref[...] = dkpe

        @pl.when(jnp.logical_not(start))
        def _():
            dkpe_ref[...] += dkpe

    in_specs = _attn_in_specs(t) + [pl.BlockSpec((ATTN_TQ, MLA_V), lambda h, qi: (qi, h))]
    out_specs = [
        pl.BlockSpec((ATTN_TQ, MLA_QPAD), lambda h, qi: (qi, h)),
        pl.BlockSpec((t, MLA_NOPE), lambda h, qi: (0, h)),
        pl.BlockSpec((t, LANES), lambda h, qi: (0, 0)),
        pl.BlockSpec((t, MLA_V), lambda h, qi: (0, h)),
    ]
    out_shape = [
        jax.ShapeDtypeStruct((t, MLA_HEADS * MLA_QPAD), BF16),
        jax.ShapeDtypeStruct((t, MLA_HEADS * MLA_NOPE), F32),
        jax.ShapeDtypeStruct((t, LANES), F32),
        jax.ShapeDtypeStruct((t, MLA_HEADS * MLA_V), F32),
    ]
    return pl.pallas_call(
        body, name=name, grid=(MLA_HEADS, t // ATTN_TQ), in_specs=in_specs, out_specs=out_specs, out_shape=out_shape,
        compiler_params=_params("arbitrary", "arbitrary"),
    )(q, kn, kpe, v, cos, sin, do)


def final_loss(name, h, nf, target, tr=256):
    t, d = h.shape

    def body(h_ref, w_ref, t_ref, loss_ref, dh_ref, dw_ref):
        i = pl.program_id(0)
        tgt = t_ref[...]

        def f(hv, wv):
            err = _rms(hv, wv) - tgt
            return 0.5 * jnp.sum(jnp.mean(err * err, -1, keepdims=True), 0, keepdims=True)

        val, vjp = jax.vjp(f, h_ref[...], w_ref[...])
        dh, dw = vjp(jnp.ones((1, 1), F32))
        dh_ref[...] = dh
        tile = jnp.broadcast_to(val, loss_ref.shape)

        @pl.when(i == 0)
        def _():
            loss_ref[...] = tile
            dw_ref[...] = dw

        @pl.when(i > 0)
        def _():
            loss_ref[...] += tile
            dw_ref[...] += dw

    row = pl.BlockSpec((tr, d), lambda i: (i, 0))
    return pl.pallas_call(
        body, name=name, grid=(t // tr,), in_specs=[row, _full_spec(nf), row],
        out_specs=[pl.BlockSpec((8, LANES), lambda i: (0, 0)), row, _full_spec(nf)],
        out_shape=[jax.ShapeDtypeStruct((8, LANES), F32), jax.ShapeDtypeStruct((t, d), F32), jax.ShapeDtypeStruct(nf.shape, F32)],
        compiler_params=_params("arbitrary"),
    )(h, nf, target)


ANY = pl.BlockSpec(memory_space=pl.ANY)
CHIP_ORDER = ((0, 0), (0, 1), (1, 0), (1, 1))


def _place():
    return lax.axis_index("x"), lax.axis_index("y"), lax.axis_index("c")


def _other_chips(x, y):
    return [(1 - x, y), (x, 1 - y), (1 - x, 1 - y)]


def _device_slot():
    x, y, c = _place()
    return 4 * x + 2 * y + c


def all_gather(name, shards):
    n = len(shards)

    def body(*refs):
        ins, outs = refs[:n], refs[n:2 * n]
        send_sems, recv_sems, local_sems = refs[2 * n:]
        x, y, c = _place()
        me, sibling = (x, y, c), (x, y, 1 - c)
        chips = _other_chips(x, y)

        def copy(k, j, block, to, from_input=False):
            dst = outs[k].at[4 * block[0] + 2 * block[1] + block[2]]
            return pltpu.make_async_remote_copy(
                src_ref=ins[k] if from_input else dst, dst_ref=dst,
                send_sem=send_sems.at[7 * k + j], recv_sem=recv_sems.at[7 * k + j],
                device_id=to, device_id_type=MESH)

        mine = [pltpu.make_async_copy(ins[k], outs[k].at[4 * x + 2 * y + c], local_sems.at[k]) for k in range(n)]
        for cp in mine:
            cp.start()
        first = []
        for k in range(n):
            first.append(copy(k, 0, me, sibling, True))
            first += [copy(k, 1 + j, me, (*chip, c), True) for j, chip in enumerate(chips)]
        for cp in first:
            cp.start()
        passed = []
        for j, chip in enumerate(chips):
            for k in range(n):
                copy(k, 1 + j, (*chip, c), me).wait_recv()
                fwd = copy(k, 4 + j, (*chip, c), sibling)
                fwd.start()
                passed.append(fwd)
        for k in range(n):
            copy(k, 0, sibling, me).wait_recv()
        for j, chip in enumerate(chips):
            for k in range(n):
                copy(k, 4 + j, (*chip, 1 - c), me).wait_recv()
        for cp in first + passed:
            cp.wait_send()
        for cp in mine:
            cp.wait()

    return pl.pallas_call(
        body, name=name, in_specs=[ANY] * n, out_specs=[ANY] * n,
        out_shape=[jax.ShapeDtypeStruct((N_DEV,) + s.shape, s.dtype) for s in shards],
        scratch_shapes=[pltpu.SemaphoreType.DMA((7 * n,)), pltpu.SemaphoreType.DMA((7 * n,)), pltpu.SemaphoreType.DMA((n,))],
    )(*shards)


def exchange_sibling(name, gs):
    n = len(gs)

    def body(*refs):
        ins, outs = refs[:n], refs[n:2 * n]
        send_sems, recv_sems = refs[2 * n:]
        x, y, c = _place()
        copies = []
        for k in range(n):
            for q, (cx, cy) in enumerate(CHIP_ORDER):
                copies.append(pltpu.make_async_remote_copy(
                    src_ref=ins[k].at[4 * cx + 2 * cy + (1 - c)], dst_ref=outs[k].at[q],
                    send_sem=send_sems.at[4 * k + q], recv_sem=recv_sems.at[4 * k + q],
                    device_id=(x, y, 1 - c), device_id_type=MESH))
        for cp in copies:
            cp.start()
        for cp in copies:
            cp.wait()

    return pl.pallas_call(
        body, name=name, in_specs=[ANY] * n, out_specs=[ANY] * n,
        out_shape=[jax.ShapeDtypeStruct((4,) + g.shape[1:], g.dtype) for g in gs],
        scratch_shapes=[pltpu.SemaphoreType.DMA((4 * n,)), pltpu.SemaphoreType.DMA((4 * n,))],
    )(*gs)


def exchange_chips(name, ps):
    n = len(ps)

    def body(*refs):
        ins, outs = refs[:n], refs[n:2 * n]
        send_sems, recv_sems = refs[2 * n:]
        x, y, c = _place()
        copies = []
        for k in range(n):
            for j, chip in enumerate(_other_chips(x, y)):
                copies.append(pltpu.make_async_remote_copy(
                    src_ref=ins[k].at[1 + j], dst_ref=outs[k].at[j],
                    send_sem=send_sems.at[3 * k + j], recv_sem=recv_sems.at[3 * k + j],
                    device_id=(*chip, c), device_id_type=MESH))
        for cp in copies:
            cp.start()
        for cp in copies:
            cp.wait()

    return pl.pallas_call(
        body, name=name, in_specs=[ANY] * n, out_specs=[ANY] * n,
        out_shape=[jax.ShapeDtypeStruct((3,) + p.shape[1:], p.dtype) for p in ps],
        scratch_shapes=[pltpu.SemaphoreType.DMA((3 * n,)), pltpu.SemaphoreType.DMA((3 * n,))],
    )(*ps)


def chip_sums(name, g, recv, tr=128):
    _, r, c = g.shape

    def body(g_ref, r_ref, o_ref):
        o_ref[...] = (g_ref[...].astype(F32) + r_ref[...].astype(F32)).astype(o_ref.dtype)

    def chip(i):
        x, y, _ = _place()
        return jnp.where(i % 2 == 1, 1 - x, x), jnp.where(i >= 2, 1 - y, y)

    def g_index(i, j):
        cx, cy = chip(i)
        return 4 * cx + 2 * cy + lax.axis_index("c"), j, 0

    def recv_index(i, j):
        cx, cy = chip(i)
        return 2 * cx + cy, j, 0

    return pl.pallas_call(
        body, name=name, grid=(4, r // tr),
        in_specs=[pl.BlockSpec((1, tr, c), g_index), pl.BlockSpec((1, tr, c), recv_index)],
        out_specs=pl.BlockSpec((1, tr, c), lambda i, j: (i, j, 0)),
        out_shape=jax.ShapeDtypeStruct((4, r, c), g.dtype),
        compiler_params=_params("parallel", "parallel"),
    )(g, recv)


def sum_parts(name, parts, tr=128):
    _, r, c = parts[0][0].shape

    def body(*refs):
        total = refs[0][0].astype(F32)
        for ref in refs[1:-1]:
            total = total + ref[0].astype(F32)
        refs[-1][...] = total

    return pl.pallas_call(
        body, name=name, grid=(r // tr,),
        in_specs=[pl.BlockSpec((1, tr, c), lambda i, s=s: (s, i, 0)) for _, s in parts],
        out_specs=pl.BlockSpec((tr, c), lambda i: (i, 0)), out_shape=jax.ShapeDtypeStruct((r, c), F32),
        compiler_params=_params("parallel"),
    )(*[a for a, _ in parts])


def adamw(name, parts, w, m, v, tr=128):
    r, c = w.shape
    np_ = len(parts)

    def body(*refs):
        g = refs[0][0].astype(F32)
        for ref in refs[1:np_]:
            g = g + ref[0].astype(F32)
        w_ref, m_ref, v_ref, g_out, d_out, m_out, v_out = refs[np_:]
        new_m = ADAM_B1 * m_ref[...] + (1.0 - ADAM_B1) * g
        new_v = ADAM_B2 * v_ref[...] + (1.0 - ADAM_B2) * (g * g)
        m_hat = new_m / (1.0 - ADAM_B1 ** ADAM_STEP)
        v_hat = new_v / (1.0 - ADAM_B2 ** ADAM_STEP)
        g_out[...] = g
        d_out[...] = -ADAM_LR * (m_hat / (jnp.sqrt(v_hat) + ADAM_EPS) + ADAM_WD * w_ref[...])
        m_out[...] = new_m
        v_out[...] = new_v

    tile = pl.BlockSpec((tr, c), lambda i: (i, 0))
    return pl.pallas_call(
        body, name=name, grid=(r // tr,),
        in_specs=[pl.BlockSpec((1, tr, c), lambda i, s=s: (s, i, 0)) for _, s in parts] + [tile] * 3,
        out_specs=[tile] * 4, out_shape=[jax.ShapeDtypeStruct((r, c), F32)] * 4,
        compiler_params=_params("parallel"),
    )(*[a for a, _ in parts], w, m, v)


ROW_TILE = 256
COL_TILE = 256


def _rms_fwd(tag, h, w):
    return rows_fwd(tag, _rms_f, [(h, D_MODEL, 0)], [w], [(D_MODEL, BF16)], ROW_TILE)[0]


def _rms_bwd(tag, h, w, dhn, dres):
    (dh,), (dw,) = rows_bwd(tag, _rms_f, [(h, D_MODEL, 0)], [w], [(dhn, D_MODEL, 0)], ROW_TILE, [F32], add=dres)
    return dh, dw


def even_fwd(tag, h, w, p):
    hn = _rms_fwd(tag + "_rms", h, p["nm"])
    uv = matmul(tag + "_uv", hn, w["uv"])
    z = matmul(tag + "_z", hn, w["z"])
    xbc = matmul(tag + "_xbc", hn, w["xbc"])
    pdt = matmul(tag + "_dt", hn, w["dt"])
    gm = [p["lng"], p["lnb"], p["ws"], p["bst"]]
    ya = rows_fwd(tag + "_gmlp", _gmlp_f, [(uv, 2 * D_MODEL, 0)], gm, [(D_MODEL, BF16)], GM_BLOCK)[0]
    xa = cols_fwd(tag + "_conv", _conv_silu_f, [(xbc, 0)], [p["cw"], p["cb"]], F32, COL_TILE)
    y, states = ssd_fwd(tag + "_ssd", xa, pdt, p["hp"])
    yb = rows_fwd(tag + "_gate", _gate_norm_f, [(y, D_MODEL, 0), (z, D_MODEL, 0)], [p["nw"]], [(D_MODEL, BF16)], ROW_TILE)[0]
    h1 = matmul(tag + "_out_b", yb, w["out_bot"], res=matmul(tag + "_out_a", ya, w["out_top"], res=h))
    return h1, dict(h=h, hn=hn, uv=uv, z=z, xbc=xbc, pdt=pdt, xa=xa, y=y, states=states, ya=ya, yb=yb)


def even_bwd(tag, dh1, s, w, p):
    dya = matmul(tag + "_dya", dh1, w["out_top"], tb=True)
    dyb = matmul(tag + "_dyb", dh1, w["out_bot"], tb=True)
    gw = dict(out_top=matmul(tag + "_gwa", s["ya"], dh1, ta=True, out_dtype=BF16),
              out_bot=matmul(tag + "_gwb", s["yb"], dh1, ta=True, out_dtype=BF16))
    (dy, dz), (dnw,) = rows_bwd(tag + "_gate_b", _gate_norm_f, [(s["y"], D_MODEL, 0), (s["z"], D_MODEL, 0)], [p["nw"]],
                                [(dyb, D_MODEL, 0)], ROW_TILE, [F32, BF16])
    dxs, dbm, dcm, dpdt, dhp = ssd_bwd(tag + "_ssd_b", s["xa"], s["pdt"], p["hp"], s["states"], dy)
    dxa = jnp.concatenate([dxs, dbm, dcm], axis=1)
    (dxbc,), (dcw, dcb) = cols_bwd(tag + "_conv_b", _conv_silu_f, [(s["xbc"], 0)], [p["cw"], p["cb"]], dxa, COL_TILE, BF16)
    gm = [p["lng"], p["lnb"], p["ws"], p["bst"]]
    (duv,), (dlng, dlnb, dws, dbst) = rows_bwd(tag + "_gmlp_b", _gmlp_f, [(s["uv"], 2 * D_MODEL, 0)], gm,
                                               [(dya, D_MODEL, 0)], GM_BLOCK, [BF16])
    dhn = None
    for key, d in (("uv", duv), ("z", dz), ("xbc", dxbc), ("dt", dpdt)):
        dhn = matmul(f"{tag}_dx_{key}", d, w[key], tb=True, res=dhn)
        gw[key] = matmul(f"{tag}_gw_{key}", s["hn"], d, ta=True, out_dtype=BF16)
    dh, dnm = _rms_bwd(tag + "_rms_b", s["h"], p["nm"], dhn, dh1)
    gp = dict(nm=dnm, lng=dlng, lnb=dlnb, ws=dws, bst=dbst, cw=dcw, cb=dcb, hp=dhp, nw=dnw)
    return dh, gw, gp


def odd_fwd(tag, h, w, p, cos, sin):
    hn = _rms_fwd(tag + "_rms", h, p["nm"])
    proj = matmul(tag + "_in", hn, w["in"])
    cq, ckv, kpe = rows_fwd(tag + "_qkvn", _qkv_norm_f, [(proj, ODD_IN_PAD, 0), (cos, LANES, 0), (sin, LANES, 0)],
                            [p["qn"], p["kvn"]], [(MLA_RANK, BF16), (MLA_RANK, BF16), (LANES, F32)], ROW_TILE)
    q = matmul(tag + "_q", cq, w["uq"])
    kn = matmul(tag + "_kn", ckv, w["kn"], out_dtype=BF16)
    v = matmul(tag + "_v", ckv, w["v"], out_dtype=BF16)
    o = attn_fwd(tag + "_attn", q, kn, kpe, v, cos, sin)
    h1 = matmul(tag + "_o", o, w["o"], res=h)
    return h1, dict(h=h, hn=hn, proj=proj, cq=cq, ckv=ckv, kpe=kpe, q=q, kn=kn, v=v, o=o)


def odd_bwd(tag, dh1, s, w, p, cos, sin):
    do = matmul(tag + "_do", dh1, w["o"], tb=True)
    gw = dict(o=matmul(tag + "_gw_o", s["o"], dh1, ta=True, out_dtype=BF16))
    dq, dkn, dkpe, dv = attn_bwd(tag + "_attn_b", s["q"], s["kn"], s["kpe"], s["v"], cos, sin, do)
    dcq = matmul(tag + "_dcq", dq, w["uq"], tb=True)
    gw["uq"] = matmul(tag + "_gw_uq", s["cq"], dq, ta=True, out_dtype=BF16)
    dckv = matmul(tag + "_dckv_v", dv, w["v"], tb=True, res=matmul(tag + "_dckv_k", dkn, w["kn"], tb=True))
    gw["kn"] = matmul(tag + "_gw_kn", s["ckv"], dkn, ta=True, out_dtype=BF16)
    gw["v"] = matmul(tag + "_gw_v", s["ckv"], dv, ta=True, out_dtype=BF16)
    (dproj,), (dqn, dkvn) = rows_bwd(
        tag + "_qkvn_b", _qkv_norm_f, [(s["proj"], ODD_IN_PAD, 0), (cos, LANES, 0), (sin, LANES, 0)], [p["qn"], p["kvn"]],
        [(dcq, MLA_RANK, 0), (dckv, MLA_RANK, 0), (dkpe, LANES, 0)], ROW_TILE, [BF16], n_nondiff=2)
    dhn = matmul(tag + "_dx_in", dproj, w["in"], tb=True)
    gw["in"] = matmul(tag + "_gw_in", s["hn"], dproj, ta=True, out_dtype=BF16)
    dh, dnm = _rms_bwd(tag + "_rms_b", s["h"], p["nm"], dhn, dh1)
    return dh, gw, dict(nm=dnm, qn=dqn, kvn=dkvn)


def ffn_fwd(tag, h, w, p):
    hn = _rms_fwd(tag + "_rms", h, p["nf"])
    g = matmul(tag + "_up_g", hn, w["up_g"])
    val = matmul(tag + "_up_v", hn, w["up_v"])
    act = cols_fwd(tag + "_act", _ffn_act_f, [(g, 0), (val, 0)], [p["fcw"], p["fcb"]], BF16, COL_TILE)
    h2 = matmul(tag + "_down", act, w["down"], res=h)
    return h2, dict(h=h, hn=hn, g=g, val=val, act=act)


def ffn_bwd(tag, dh2, s, w, p):
    dact = matmul(tag + "_dact", dh2, w["down"], tb=True)
    gw = dict(down=matmul(tag + "_gw_down", s["act"], dh2, ta=True, out_dtype=BF16))
    (dg, dval), (dfcw, dfcb) = cols_bwd(tag + "_act_b", _ffn_act_f, [(s["g"], 0), (s["val"], 0)], [p["fcw"], p["fcb"]],
                                        dact, COL_TILE, BF16)
    dhn = matmul(tag + "_dx_v", dval, w["up_v"], tb=True, res=matmul(tag + "_dx_g", dg, w["up_g"], tb=True))
    gw["up_g"] = matmul(tag + "_gw_up_g", s["hn"], dg, ta=True, out_dtype=BF16)
    gw["up_v"] = matmul(tag + "_gw_up_v", s["hn"], dval, ta=True, out_dtype=BF16)
    dh, dnf = _rms_bwd(tag + "_rms_b", s["h"], p["nf"], dhn, dh2)
    return dh, gw, dict(nf=dnf, fcw=dfcw, fcb=dfcb)


def _cols_from_slots(g):
    return jnp.moveaxis(g, 0, 1).reshape(g.shape[1], N_DEV * g.shape[2])


def _slots_from_cols(wmat):
    k, n = wmat.shape
    return jnp.moveaxis(wmat.reshape(k, N_DEV, n // N_DEV), 1, 0)


def _pad_last(a, width):
    return jnp.pad(a, [(0, 0)] * (a.ndim - 1) + [(0, width - a.shape[-1])])


def _heads_to_lanes(a):
    lead = a.shape[:-1]
    return _pad_last(a.reshape(lead + (SSD_STEPS, SSD_HEADS_PER_STEP)), LANES).reshape(lead + (SSD_STEPS * LANES,))


def _lanes_to_heads(a):
    lead = a.shape[:-1]
    return a.reshape(lead + (SSD_STEPS, LANES))[..., :SSD_HEADS_PER_STEP].reshape(lead + (SSM_HEADS,))


def prep_even(g_in, g_out):
    wn = _cols_from_slots(g_in)
    o1, o2, o3 = 2 * D_MODEL, 3 * D_MODEL, 3 * D_MODEL + SSM_CONV_DIM
    out = g_out.reshape(2 * D_MODEL, D_MODEL)
    return dict(uv=wn[:, :o1], z=wn[:, o1:o2], xbc=wn[:, o2:o3], dt=_heads_to_lanes(wn[:, o3:]),
                out_top=out[:D_MODEL], out_bot=out[D_MODEL:])


def unprep_even(gw):
    wn = jnp.concatenate([gw["uv"], gw["z"], gw["xbc"], _lanes_to_heads(gw["dt"])], axis=1)
    return _slots_from_cols(wn), jnp.concatenate([gw["out_top"], gw["out_bot"]], axis=0).reshape(N_DEV, -1, D_MODEL)


def prep_odd(g_in, g_uq, g_ukv, g_o):
    uq = _cols_from_slots(g_uq).reshape(MLA_RANK, MLA_HEADS, MLA_QK)
    ukv = _cols_from_slots(g_ukv).reshape(MLA_RANK, MLA_HEADS, MLA_NOPE + MLA_V)
    return dict(**{"in": _pad_last(g_in.reshape(D_MODEL, ODD_IN), ODD_IN_PAD)},
                uq=_pad_last(uq, MLA_QPAD).reshape(MLA_RANK, MLA_HEADS * MLA_QPAD),
                kn=ukv[:, :, :MLA_NOPE].reshape(MLA_RANK, MLA_HEADS * MLA_NOPE),
                v=ukv[:, :, MLA_NOPE:].reshape(MLA_RANK, MLA_HEADS * MLA_V),
                o=g_o.reshape(MLA_HEADS * MLA_V, D_MODEL))


def unprep_odd(gw):
    uq = gw["uq"].reshape(MLA_RANK, MLA_HEADS, MLA_QPAD)[:, :, :MLA_QK].reshape(MLA_RANK, MLA_HEADS * MLA_QK)
    ukv = jnp.concatenate([gw["kn"].reshape(MLA_RANK, MLA_HEADS, MLA_NOPE), gw["v"].reshape(MLA_RANK, MLA_HEADS, MLA_V)], axis=2)
    return (gw["in"][:, :ODD_IN].reshape(N_DEV, -1, ODD_IN), _slots_from_cols(uq),
            _slots_from_cols(ukv.reshape(MLA_RANK, -1)), gw["o"].reshape(N_DEV, -1, D_MODEL))


def prep_ffn(g_up, g_down):
    up = _cols_from_slots(g_up)
    return dict(up_g=up[:, :D_FF], up_v=up[:, D_FF:], down=g_down.reshape(D_FF, D_MODEL))


def unprep_ffn(gw):
    return _slots_from_cols(jnp.concatenate([gw["up_g"], gw["up_v"]], axis=1)), gw["down"].reshape(N_DEV, -1, D_MODEL)


SMALL_TILE = LANES * LANES


def _pack(arrs):
    flat = jnp.concatenate([a.reshape(-1).astype(F32) for a in arrs])
    size = -(-flat.shape[0] // SMALL_TILE) * SMALL_TILE
    return jnp.pad(flat, (0, size - flat.shape[0])).reshape(-1, LANES)


def _unpack(packed, shapes, lead=()):
    flat = packed.reshape(lead + (-1,))
    out, off = [], 0
    for shp in shapes:
        size = math.prod(shp)
        out.append(flat[..., off:off + size].reshape(lead + tuple(shp)))
        off += size
    return out


SMALL_SHARDED = {"ev_gm_ln_g": 2, "ev_gm_ln_b": 2, "ev_conv_w": 2, "od_q_norm": 1, "od_kv_norm": 1, "ff_conv_w": 2}
SMALL_REPLICATED = ["norm_mix", "norm_ffn", "norm_final", "ev_gm_ws", "ev_gm_bs", "ev_conv_b", "ev_dt_bias", "ev_a_log",
                    "ev_d_skip", "ev_ssm_norm_w", "ff_conv_b"]
MATRICES = {"ev_w_in": (2, 2048, 1156), "ev_w_out": (2, 512, 2048), "od_w_in": (2, 256, 1088), "od_w_uq": (2, 512, 384),
            "od_w_ukv": (2, 512, 512), "od_w_o": (2, 256, 2048), "ff_w_up": (4, 2048, 1408), "ff_w_down": (4, 704, 2048)}
WEIGHT_ORDER = ["norm_mix", "norm_ffn", "norm_final", "ev_w_in", "ev_gm_ln_g", "ev_gm_ln_b", "ev_gm_ws", "ev_gm_bs",
                "ev_conv_w", "ev_conv_b", "ev_dt_bias", "ev_a_log", "ev_d_skip", "ev_ssm_norm_w", "ev_w_out", "od_w_in",
                "od_q_norm", "od_kv_norm", "od_w_uq", "od_w_ukv", "od_w_o", "ff_w_up", "ff_conv_w", "ff_conv_b", "ff_w_down"]


def _full_from_shards(name, gathered):
    ax = SMALL_SHARDED[name]
    moved = jnp.moveaxis(gathered, 0, ax)
    shp = moved.shape
    return moved.reshape(shp[:ax] + (shp[ax] * shp[ax + 1],) + shp[ax + 2:])


def _my_shard(name, full, dev):
    ax = SMALL_SHARDED[name]
    shp = full.shape
    split = full.reshape(shp[:ax] + (N_DEV, shp[ax] // N_DEV) + shp[ax + 1:])
    return lax.dynamic_index_in_dim(split, dev, axis=ax, keepdims=False)


def _even_small(sm, j):
    row = lambda a: a.reshape(1, -1)
    hp = jnp.stack([sm["ev_dt_bias"][j], sm["ev_a_log"][j], sm["ev_d_skip"][j]])
    return dict(nm=row(sm["norm_mix"][2 * j]), lng=row(sm["ev_gm_ln_g"][j]), lnb=row(sm["ev_gm_ln_b"][j]),
                ws=sm["ev_gm_ws"][j], bst=sm["ev_gm_bs"][j].T, cw=sm["ev_conv_w"][j], cb=row(sm["ev_conv_b"][j]),
                hp=_heads_to_lanes(hp), nw=row(sm["ev_ssm_norm_w"][j]))


def _odd_small(sm, j):
    row = lambda a: a.reshape(1, -1)
    return dict(nm=row(sm["norm_mix"][2 * j + 1]), qn=row(sm["od_q_norm"][j]), kvn=row(sm["od_kv_norm"][j]))


def _ffn_small(sm, layer):
    row = lambda a: a.reshape(1, -1)
    return dict(nf=row(sm["norm_ffn"][layer]), fcw=sm["ff_conv_w"][layer], fcb=row(sm["ff_conv_b"][layer]))


def _rope_tables(positions):
    inv_freq = ROPE_THETA ** (-jnp.arange(0, MLA_ROPE, 2, dtype=F32) / MLA_ROPE)
    ang = positions.astype(F32).reshape(-1, 1) * inv_freq
    cos, sin = jnp.cos(ang), jnp.sin(ang)
    return _pad_last(jnp.concatenate([cos, cos], axis=1), LANES), _pad_last(jnp.concatenate([-sin, sin], axis=1), LANES)


def local_step(x, positions, target, weights, sm):
    cos, sin = _rope_tables(positions)
    h, saved = x, []
    for layer in range(4):
        j, tag = layer // 2, f"l{layer}"
        if layer % 2 == 0:
            pm = _even_small(sm, j)
            h, sv = even_fwd(tag, h, weights[layer]["mix"], pm)
        else:
            pm = _odd_small(sm, j)
            h, sv = odd_fwd(tag, h, weights[layer]["mix"], pm, cos, sin)
        pf = _ffn_small(sm, layer)
        h, sf = ffn_fwd(tag + "f", h, weights[layer]["ffn"], pf)
        saved.append((pm, sv, pf, sf))
    loss_tile, dh, dnfinal = final_loss("final_loss", h, sm["norm_final"].reshape(1, -1), target)
    gmats = [None] * 4
    gs = {k: [None] * v.shape[0] for k, v in sm.items() if k != "norm_final"}
    gs["norm_final"] = dnfinal.reshape(-1)
    for layer in reversed(range(4)):
        j, tag = layer // 2, f"l{layer}"
        pm, sv, pf, sf = saved[layer]
        dh, gwf, gpf = ffn_bwd(tag + "f", dh, sf, weights[layer]["ffn"], pf)
        gs["norm_ffn"][layer], gs["ff_conv_w"][layer], gs["ff_conv_b"][layer] = gpf["nf"][0], gpf["fcw"], gpf["fcb"][0]
        if layer % 2 == 0:
            dh, gwm, gp = even_bwd(tag, dh, sv, weights[layer]["mix"], pm)
            hp = _lanes_to_heads(gp["hp"])
            gs["norm_mix"][layer] = gp["nm"][0]
            gs["ev_gm_ln_g"][j], gs["ev_gm_ln_b"][j] = gp["lng"].reshape(GM_GROUPS, -1), gp["lnb"].reshape(GM_GROUPS, -1)
            gs["ev_gm_ws"][j], gs["ev_gm_bs"][j] = gp["ws"], gp["bst"].T
            gs["ev_conv_w"][j], gs["ev_conv_b"][j] = gp["cw"], gp["cb"][0]
            gs["ev_dt_bias"][j], gs["ev_a_log"][j], gs["ev_d_skip"][j] = hp[0], hp[1], hp[2]
            gs["ev_ssm_norm_w"][j] = gp["nw"][0]
        else:
            dh, gwm, gp = odd_bwd(tag, dh, sv, weights[layer]["mix"], pm, cos, sin)
            gs["norm_mix"][layer] = gp["nm"][0]
            gs["od_q_norm"][j], gs["od_kv_norm"][j] = gp["qn"][0], gp["kvn"][0]
        gmats[layer] = dict(mix=gwm, ffn=gwf)
    gs = {k: (v if k == "norm_final" else jnp.stack(v)) for k, v in gs.items()}
    return loss_tile[0, 0], dh, gmats, gs


def kernel(x, positions, norm_mix, norm_ffn, norm_final, ev_w_in, ev_gm_ln_g, ev_gm_ln_b, ev_gm_ws, ev_gm_bs, ev_conv_w, ev_conv_b, ev_dt_bias, ev_a_log, ev_d_skip, ev_ssm_norm_w, ev_w_out, od_w_in, od_q_norm, od_kv_norm, od_w_uq, od_w_ukv, od_w_o, ff_w_up, ff_conv_w, ff_conv_b, ff_w_down, loss_target, m_norm_mix, m_norm_ffn, m_norm_final, m_ev_w_in, m_ev_gm_ln_g, m_ev_gm_ln_b, m_ev_gm_ws, m_ev_gm_bs, m_ev_conv_w, m_ev_conv_b, m_ev_dt_bias, m_ev_a_log, m_ev_d_skip, m_ev_ssm_norm_w, m_ev_w_out, m_od_w_in, m_od_q_norm, m_od_kv_norm, m_od_w_uq, m_od_w_ukv, m_od_w_o, m_ff_w_up, m_ff_conv_w, m_ff_conv_b, m_ff_w_down, v_norm_mix, v_norm_ffn, v_norm_final, v_ev_w_in, v_ev_gm_ln_g, v_ev_gm_ln_b, v_ev_gm_ws, v_ev_gm_bs, v_ev_conv_w, v_ev_conv_b, v_ev_dt_bias, v_ev_a_log, v_ev_d_skip, v_ev_ssm_norm_w, v_ev_w_out, v_od_w_in, v_od_q_norm, v_od_kv_norm, v_od_w_uq, v_od_w_ukv, v_od_w_o, v_ff_w_up, v_ff_conv_w, v_ff_conv_b, v_ff_w_down):
    args = dict(locals())
    wts = {n: args[n] for n in WEIGHT_ORDER}
    mom = {n: args["m_" + n] for n in WEIGHT_ORDER}
    var = {n: args["v_" + n] for n in WEIGHT_ORDER}
    dev = _device_slot()

    small_names = list(SMALL_SHARDED)
    small_shapes = [wts[n].shape for n in small_names]
    (small_all,) = all_gather("ag_small", [_pack([wts[n] for n in small_names])])
    small_full = _unpack(small_all, small_shapes, lead=(N_DEV,))
    sm = {n: _full_from_shards(n, g) for n, g in zip(small_names, small_full)}
    sm.update({n: wts[n] for n in SMALL_REPLICATED})

    bf = {n: wts[n].astype(BF16) for n in MATRICES}
    weights = []
    for layer in range(4):
        j = layer // 2
        if layer % 2 == 0:
            g = all_gather(f"ag_l{layer}", [bf["ev_w_in"][j], bf["ev_w_out"][j], bf["ff_w_up"][layer], bf["ff_w_down"][layer]])
            mix = prep_even(g[0], g[1])
        else:
            g = all_gather(f"ag_l{layer}", [bf["od_w_in"][j], bf["od_w_uq"][j], bf["od_w_ukv"][j], bf["od_w_o"][j],
                                             bf["ff_w_up"][layer], bf["ff_w_down"][layer]])
            mix = prep_odd(g[0], g[1], g[2], g[3])
        weights.append(dict(mix=mix, ffn=prep_ffn(g[-2], g[-1])))

    loss_local, dx, gmats, gs = local_step(x[0], positions[0], loss_target[0], weights, sm)
    loss = lax.psum(loss_local, ("x", "y", "c"))

    slots = {n: [] for n in MATRICES}
    for layer in range(4):
        up, down = unprep_ffn(gmats[layer]["ffn"])
        slots["ff_w_up"].append(up)
        slots["ff_w_down"].append(down)
        if layer % 2 == 0:
            w_in, w_out = unprep_even(gmats[layer]["mix"])
            slots["ev_w_in"].append(w_in)
            slots["ev_w_out"].append(w_out)
        else:
            w_in, w_uq, w_ukv, w_o = unprep_odd(gmats[layer]["mix"])
            for n, a in (("od_w_in", w_in), ("od_w_uq", w_uq), ("od_w_ukv", w_ukv), ("od_w_o", w_o)):
                slots[n].append(a)
    names = list(MATRICES)
    send = [jnp.concatenate(slots[n], axis=1) for n in names]
    from_sibling = exchange_sibling("rs_sibling", send)
    sums = [chip_sums("rs_add_" + n, g, r) for n, g, r in zip(names, send, from_sibling)]
    from_chips = exchange_chips("rs_chips", sums)
    out = {}
    for n, p_, r_ in zip(names, sums, from_chips):
        layers, rows, cols = MATRICES[n]
        two_d = lambda a: a.reshape(layers * rows, cols)
        res = adamw("adamw_" + n, [(p_, 0), (r_, 0), (r_, 1), (r_, 2)], two_d(wts[n]), two_d(mom[n]), two_d(var[n]))
        out[n] = [a.reshape(wts[n].shape) for a in res]

    all_small = small_names + SMALL_REPLICATED
    (partials,) = all_gather("ar_small", [_pack([gs[n] for n in all_small])])
    total = sum_parts("ar_small_sum", [(partials, s) for s in range(N_DEV)])
    g_full = dict(zip(all_small, _unpack(total, [gs[n].shape for n in all_small])))
    g_mine = {n: (_my_shard(n, g_full[n], dev) if n in SMALL_SHARDED else g_full[n]) for n in all_small}
    packed = [_pack([d[n] for n in all_small]) for d in (g_mine, wts, mom, var)]
    res = adamw("adamw_small", [(packed[0][None], 0)], packed[1], packed[2], packed[3])
    unpacked = [_unpack(a, [wts[n].shape for n in all_small]) for a in res]
    for i, n in enumerate(all_small):
        out[n] = [u[i] for u in unpacked]

    return (loss, dx[None], *[out[n][0] for n in WEIGHT_ORDER], *[out[n][1] for n in WEIGHT_ORDER],
            *[out[n][2] for n in WEIGHT_ORDER], *[out[n][3] for n in WEIGHT_ORDER])
```

```python
import functools
import math

import jax
import jax.numpy as jnp
from jax import lax
from jax.experimental import pallas as pl
from jax.experimental.pallas import tpu as pltpu

F32 = jnp.float32
BF16 = jnp.bfloat16
MESH = pl.DeviceIdType.MESH

V7X_VMEM_LIMIT_BYTES = 56 * 1024 * 1024
LANES = 128

EPS = 1e-6
D_MODEL = 2048
CHUNK = 64
GM_BLOCK = 128
GM_GROUPS = 8
GM_GROUP_DIM = D_MODEL // GM_GROUPS
SSM_HEADS = 32
SSM_HEAD_DIM = 64
SSM_GROUPS = 4
SSM_STATE = 128
SSM_CONV = 4
SSM_BC = SSM_GROUPS * SSM_STATE
SSM_CONV_DIM = D_MODEL + 2 * SSM_BC
SSD_HEADS_PER_STEP = 4
SSD_STEPS = SSM_HEADS // SSD_HEADS_PER_STEP
SSD_X_WIDTH = SSD_HEADS_PER_STEP * SSM_HEAD_DIM
MLA_HEADS = 16
MLA_RANK = 512
MLA_NOPE = 128
MLA_ROPE = 64
MLA_V = 128
MLA_QK = MLA_NOPE + MLA_ROPE
MLA_QPAD = 2 * LANES
ODD_IN = 2 * MLA_RANK + MLA_ROPE
ODD_IN_PAD = 2 * MLA_RANK + LANES
D_FF = 5632
ROPE_THETA = 10000.0
N_DEV = 8

ADAM_LR, ADAM_B1, ADAM_B2, ADAM_EPS, ADAM_WD, ADAM_STEP = 0.001, 0.9, 0.999, 1e-08, 0.01, 10


def _params(*sem):
    return pltpu.CompilerParams(dimension_semantics=sem, vmem_limit_bytes=V7X_VMEM_LIMIT_BYTES)


def _pick(dim, target):
    if dim <= target:
        return dim
    t = (target // LANES) * LANES
    while t >= LANES:
        if dim % t == 0:
            return t
        t -= LANES
    raise ValueError(f"no tile for {dim} under {target}")


def matmul(name, a, b, *, ta=False, tb=False, res=None, out_dtype=F32, tm=1024, tn=512, tk=2048):
    m, k = (a.shape[1], a.shape[0]) if ta else a.shape
    n = b.shape[0] if tb else b.shape[1]
    assert k == (b.shape[1] if tb else b.shape[0]), (name, a.shape, b.shape)
    tm, tn, tk = _pick(m, tm), _pick(n, tn), _pick(k, tk)
    nk = k // tk
    dims = (((0 if ta else 1,), (1 if tb else 0,)), ((), ()))

    def body(*refs):
        a_ref, b_ref = refs[0], refs[1]
        r_ref = refs[2] if res is not None else None
        o_ref = refs[3] if res is not None else refs[2]
        part = lax.dot_general(a_ref[...].astype(BF16), b_ref[...].astype(BF16), dims, preferred_element_type=F32)

        def finish(total):
            if r_ref is not None:
                total = total + r_ref[...]
            o_ref[...] = total.astype(o_ref.dtype)

        if nk == 1:
            finish(part)
        else:
            acc = refs[-1]
            kk = pl.program_id(2)

            @pl.when(kk == 0)
            def _():
                acc[...] = part

            @pl.when(kk > 0)
            def _():
                acc[...] += part

            @pl.when(kk == nk - 1)
            def _():
                finish(acc[...])

    a_spec = pl.BlockSpec((tk, tm), lambda i, j, kk: (kk, i)) if ta else pl.BlockSpec((tm, tk), lambda i, j, kk: (i, kk))
    b_spec = pl.BlockSpec((tn, tk), lambda i, j, kk: (j, kk)) if tb else pl.BlockSpec((tk, tn), lambda i, j, kk: (kk, j))
    o_spec = pl.BlockSpec((tm, tn), lambda i, j, kk: (i, j))
    ins, specs = [a, b], [a_spec, b_spec]
    if res is not None:
        ins.append(res)
        specs.append(o_spec)
    return pl.pallas_call(
        body, name=name, grid=(m // tm, n // tn, nk), in_specs=specs, out_specs=o_spec,
        out_shape=jax.ShapeDtypeStruct((m, n), out_dtype),
        scratch_shapes=[pltpu.VMEM((tm, tn), F32)] if nk > 1 else [],
        compiler_params=_params("parallel", "parallel", "arbitrary"),
    )(*ins)


@functools.partial(jax.custom_vjp, nondiff_argnums=(1, 2))
def _roll(x, shift, axis):
    return pltpu.roll(x, shift, axis)


def _roll_fwd(x, shift, axis):
    return pltpu.roll(x, shift, axis), None


def _roll_bwd(shift, axis, _, g):
    return (pltpu.roll(g, (g.shape[axis] - shift) % g.shape[axis], axis),)


_roll.defvjp(_roll_fwd, _roll_bwd)


def _shift_down(x, s):
    rows = lax.broadcasted_iota(jnp.int32, x.shape, 0)
    return jnp.where(rows >= s, _roll(x, s, 0), 0.0)


def _dwconv(x, w, b):
    taps = w.shape[0]
    y = b + w[taps - 1:taps, :] * x
    for kk in range(taps - 1):
        y = y + w[kk:kk + 1, :] * _shift_down(x, taps - 1 - kk)
    return y


def _rms(x, w):
    return x * lax.rsqrt(jnp.mean(x * x, -1, keepdims=True) + EPS) * w


def _rms_f(h, w):
    return (_rms(h, w),)


def _gmlp_f(uv, lng, lnb, ws, bst):
    r = lax.broadcasted_iota(jnp.int32, (GM_BLOCK, GM_BLOCK), 0) // CHUNK
    c = lax.broadcasted_iota(jnp.int32, (GM_BLOCK, GM_BLOCK), 1) // CHUNK
    outs = []
    for g in range(GM_GROUPS):
        lo, hi = g * GM_GROUP_DIM, (g + 1) * GM_GROUP_DIM
        gu = jax.nn.gelu(uv[:, lo:hi])
        gv = jax.nn.gelu(uv[:, D_MODEL + lo:D_MODEL + hi])
        xc = gv - jnp.mean(gv, -1, keepdims=True)
        var = jnp.mean(xc * xc, -1, keepdims=True)
        vn = xc * lax.rsqrt(var + EPS) * lng[:, lo:hi] + lnb[:, lo:hi]
        wm = jnp.where(r >= c, ws[g], 0.0).astype(BF16)
        gate = jnp.dot(wm, vn.astype(BF16), preferred_element_type=F32) + bst[:, g:g + 1]
        outs.append(gu * gate)
    return (jnp.concatenate(outs, axis=1),)


def _conv_silu_f(x, w, b):
    return (jax.nn.silu(_dwconv(x, w, b)),)


def _ffn_act_f(g, val, w, b):
    return (jax.nn.gelu(_dwconv(g, w, b)) * val,)


def _gate_norm_f(y, z, nw):
    y2 = y * jax.nn.silu(z)
    width = D_MODEL // SSM_GROUPS
    outs = []
    for g in range(SSM_GROUPS):
        blk = y2[:, g * width:(g + 1) * width]
        outs.append(blk * lax.rsqrt(jnp.mean(blk * blk, -1, keepdims=True) + EPS))
    return (jnp.concatenate(outs, axis=1) * nw,)


def _rope(x, cos, sin):
    lane = lax.broadcasted_iota(jnp.int32, x.shape, 1)
    half = MLA_ROPE // 2
    swapped = jnp.where(lane < half, _roll(x, LANES - half, 1), _roll(x, half, 1))
    return x * cos + swapped * sin


def _qkv_norm_f(proj, cos, sin, qn, kvn):
    cq = _rms(proj[:, :MLA_RANK], qn)
    ckv = _rms(proj[:, MLA_RANK:2 * MLA_RANK], kvn)
    kpe = _rope(proj[:, 2 * MLA_RANK:], cos, sin)
    return cq, ckv, kpe


def _attn_f(q0, qh, kn, kpe, v, cos, sin):
    qn = qh[:, :MLA_NOPE]
    qp = _rope(qh[:, MLA_NOPE:], cos, sin)
    nt = (((1,), (1,)), ((), ()))
    s = lax.dot_general(qn.astype(BF16), kn.astype(BF16), nt, preferred_element_type=F32)
    s = s + lax.dot_general(qp.astype(BF16), kpe.astype(BF16), nt, preferred_element_type=F32)
    s = s * (MLA_QK ** -0.5)
    qc = (q0 + lax.broadcasted_iota(jnp.int32, s.shape, 0)) // CHUNK
    kc = lax.broadcasted_iota(jnp.int32, s.shape, 1) // CHUNK
    s = jnp.where(kc <= qc, s, -jnp.inf)
    p = jax.nn.softmax(s, axis=-1)
    return (jnp.dot(p.astype(BF16), v.astype(BF16), preferred_element_type=F32),)


def _ssd_chunk_f(x, bm, cm, pdt, hp, sprev):
    nh = SSD_HEADS_PER_STEP
    dt = jax.nn.softplus(pdt[:, :nh] + hp[0:1, :nh])
    da = dt * (-jnp.exp(hp[1:2, :nh]))
    r = lax.broadcasted_iota(jnp.int32, (CHUNK, CHUNK), 0)
    c = lax.broadcasted_iota(jnp.int32, (CHUNK, CHUNK), 1)
    tril = r >= c
    cs = jnp.dot(tril.astype(F32), da, precision=lax.Precision.HIGHEST, preferred_element_type=F32)
    cst = cs.T
    nt = (((1,), (1,)), ((), ()))
    tn = (((0,), (0,)), ((), ()))
    cb = lax.dot_general(cm.astype(BF16), bm.astype(BF16), nt, preferred_element_type=F32)
    ys, snew = [], []
    for e in range(nh):
        xe = x[:, e * SSM_HEAD_DIM:(e + 1) * SSM_HEAD_DIM]
        xd = xe * dt[:, e:e + 1]
        cse = cs[:, e:e + 1]
        decay = jnp.exp(jnp.where(tril, cse - cst[e:e + 1, :], -jnp.inf))
        y = jnp.dot((cb * decay).astype(BF16), xd.astype(BF16), preferred_element_type=F32)
        tot = cse[CHUNK - 1:CHUNK, :]
        st = lax.dot_general((xd * jnp.exp(tot - cse)).astype(BF16), bm.astype(BF16), tn, preferred_element_type=F32)
        yoff = lax.dot_general(cm.astype(BF16), sprev[e].astype(BF16), nt, preferred_element_type=F32)
        ys.append(y + yoff * jnp.exp(cse) + hp[2:3, e:e + 1] * xe)
        snew.append((jnp.exp(tot) * sprev[e] + st)[None])
    return jnp.concatenate(ys, axis=1), jnp.concatenate(snew, axis=0)


def _full_spec(a):
    nd = a.ndim
    return pl.BlockSpec(a.shape, lambda i, nd=nd: (0,) * nd)


def rows_fwd(name, f, rows, params, outs, tr):
    t = rows[0][0].shape[0]
    nr, npar = len(rows), len(params)

    def body(*refs):
        vals = f(*[x[...].astype(F32) for x in refs[:nr + npar]])
        for o_ref, val in zip(refs[nr + npar:], vals):
            o_ref[...] = val.astype(o_ref.dtype)

    in_specs = [pl.BlockSpec((tr, w), lambda i, cb=cb: (i, cb)) for _, w, cb in rows] + [_full_spec(p) for p in params]
    out = pl.pallas_call(
        body, name=name, grid=(t // tr,), in_specs=in_specs,
        out_specs=[pl.BlockSpec((tr, w), lambda i: (i, 0)) for w, _ in outs],
        out_shape=[jax.ShapeDtypeStruct((t, w), dt) for w, dt in outs],
        compiler_params=_params("parallel"),
    )(*[a for a, _, _ in rows], *params)
    return out


def rows_bwd(name, f, rows, params, cots, tr, d_dtypes, n_nondiff=0, add=None):
    t = rows[0][0].shape[0]
    nr, npar, nc = len(rows), len(params), len(cots)
    nd = nr - n_nondiff
    has_add = add is not None

    def body(*refs):
        i = pl.program_id(0)
        row_vals = [x[...].astype(F32) for x in refs[:nr]]
        par_vals = [x[...].astype(F32) for x in refs[nr:nr + npar]]
        cot_refs = refs[nr + npar:nr + npar + nc]
        pos = nr + npar + nc
        add_ref = refs[pos] if has_add else None
        pos += int(has_add)
        drow_refs = refs[pos:pos + nd]
        dpar_refs = refs[pos + nd:]

        def g(*diff):
            return f(*diff[:nd], *row_vals[nd:], *diff[nd:])

        _, vjp = jax.vjp(g, *row_vals[:nd], *par_vals)
        grads = vjp(tuple(cr[...].astype(F32) for cr in cot_refs))
        for j, d_ref in enumerate(drow_refs):
            val = grads[j]
            if j == 0 and has_add:
                val = val + add_ref[...]
            d_ref[...] = val.astype(d_ref.dtype)
        for j, d_ref in enumerate(dpar_refs):
            @pl.when(i == 0)
            def _(d_ref=d_ref, j=j):
                d_ref[...] = grads[nd + j]

            @pl.when(i > 0)
            def _(d_ref=d_ref, j=j):
                d_ref[...] += grads[nd + j]

    in_specs = [pl.BlockSpec((tr, w), lambda i, cb=cb: (i, cb)) for _, w, cb in rows] + [_full_spec(p) for p in params]
    in_specs += [pl.BlockSpec((tr, w), lambda i, cb=cb: (i, cb)) for _, w, cb in cots]
    ins = [a for a, _, _ in rows] + list(params) + [a for a, _, _ in cots]
    if has_add:
        in_specs.append(pl.BlockSpec((tr, rows[0][1]), lambda i: (i, 0)))
        ins.append(add)
    out_specs = [pl.BlockSpec((tr, rows[j][1]), lambda i: (i, 0)) for j in range(nd)] + [_full_spec(p) for p in params]
    out_shape = [jax.ShapeDtypeStruct((t, rows[j][1]), d_dtypes[j]) for j in range(nd)]
    out_shape += [jax.ShapeDtypeStruct(p.shape, F32) for p in params]
    out = pl.pallas_call(
        body, name=name, grid=(t // tr,), in_specs=in_specs, out_specs=out_specs, out_shape=out_shape,
        compiler_params=_params("arbitrary"),
    )(*ins)
    return out[:nd], out[nd:]


def cols_fwd(name, f, cols, cparams, out_dtype, tc):
    t = cols[0][0].shape[0]
    width = cparams[0].shape[1]
    ncol = len(cols)

    def body(*refs):
        (val,) = f(*[x[...].astype(F32) for x in refs[:-1]])
        refs[-1][...] = val.astype(refs[-1].dtype)

    in_specs = [pl.BlockSpec((t, tc), lambda j, o=o: (0, o + j)) for _, o in cols]
    in_specs += [pl.BlockSpec((p.shape[0], tc), lambda j: (0, j)) for p in cparams]
    return pl.pallas_call(
        body, name=name, grid=(width // tc,), in_specs=in_specs,
        out_specs=pl.BlockSpec((t, tc), lambda j: (0, j)),
        out_shape=jax.ShapeDtypeStruct((t, width), out_dtype),
        compiler_params=_params("parallel"),
    )(*[a for a, _ in cols], *cparams)


def cols_bwd(name, f, cols, cparams, cot, tc, d_dtype):
    t = cols[0][0].shape[0]
    width = cparams[0].shape[1]
    ncol, npar = len(cols), len(cparams)

    def body(*refs):
        vals = [x[...].astype(F32) for x in refs[:ncol + npar]]
        _, vjp = jax.vjp(f, *vals)
        grads = vjp((refs[ncol + npar][...].astype(F32),))
        for d_ref, gval in zip(refs[ncol + npar + 1:], grads):
            d_ref[...] = gval.astype(d_ref.dtype)

    in_specs = [pl.BlockSpec((t, tc), lambda j, o=o: (0, o + j)) for _, o in cols]
    in_specs += [pl.BlockSpec((p.shape[0], tc), lambda j: (0, j)) for p in cparams]
    in_specs.append(pl.BlockSpec((t, tc), lambda j: (0, j)))
    out_specs = [pl.BlockSpec((t, tc), lambda j: (0, j)) for _ in cols]
    out_specs += [pl.BlockSpec((p.shape[0], tc), lambda j: (0, j)) for p in cparams]
    out_shape = [jax.ShapeDtypeStruct((t, width), d_dtype) for _ in cols]
    out_shape += [jax.ShapeDtypeStruct(p.shape, F32) for p in cparams]
    out = pl.pallas_call(
        body, name=name, grid=(width // tc,), in_specs=in_specs, out_specs=out_specs, out_shape=out_shape,
        compiler_params=_params("parallel"),
    )(*[a for a, _ in cols], *cparams, cot)
    return out[:ncol], out[ncol:]


def _ssd_in_specs(t):
    heads_per_group = SSM_HEADS // SSM_GROUPS
    steps_per_group = heads_per_group // SSD_HEADS_PER_STEP
    b_blk = D_MODEL // LANES
    c_blk = (D_MODEL + SSM_BC) // LANES
    return [
        pl.BlockSpec((t, SSD_X_WIDTH), lambda s: (0, s)),
        pl.BlockSpec((t, LANES), lambda s: (0, b_blk + s // steps_per_group)),
        pl.BlockSpec((t, LANES), lambda s: (0, c_blk + s // steps_per_group)),
        pl.BlockSpec((t, LANES), lambda s: (0, s)),
        pl.BlockSpec((3, LANES), lambda s: (0, s)),
    ]


def ssd_fwd(name, xa, pdt, hp):
    t = xa.shape[0]
    nc = t // CHUNK
    nh = SSD_HEADS_PER_STEP

    def body(x_ref, b_ref, c_ref, pdt_ref, hp_ref, y_ref, st_ref, s_scr):
        s_scr[...] = jnp.zeros_like(s_scr)

        def step(ci, carry):
            sl = pl.ds(pl.multiple_of(ci * CHUNK, CHUNK), CHUNK)
            sprev = s_scr[...]
            st_ref[0, ci] = sprev
            y, snew = _ssd_chunk_f(x_ref[sl, :], b_ref[sl, :], c_ref[sl, :], pdt_ref[sl, :], hp_ref[...], sprev)
            y_ref[sl, :] = y
            s_scr[...] = snew
            return carry

        lax.fori_loop(0, nc, step, 0)

    return pl.pallas_call(
        body, name=name, grid=(SSD_STEPS,), in_specs=_ssd_in_specs(t),
        out_specs=[pl.BlockSpec((t, SSD_X_WIDTH), lambda s: (0, s)),
                   pl.BlockSpec((1, nc, nh, SSM_HEAD_DIM, SSM_STATE), lambda s: (s, 0, 0, 0, 0))],
        out_shape=[jax.ShapeDtypeStruct((t, D_MODEL), F32),
                   jax.ShapeDtypeStruct((SSD_STEPS, nc, nh, SSM_HEAD_DIM, SSM_STATE), F32)],
        scratch_shapes=[pltpu.VMEM((nh, SSM_HEAD_DIM, SSM_STATE), F32)],
        compiler_params=_params("parallel"),
    )(xa, xa, xa, pdt, hp)


def ssd_bwd(name, xa, pdt, hp, states, dy):
    t = xa.shape[0]
    nc = t // CHUNK
    nh = SSD_HEADS_PER_STEP
    steps_per_group = SSM_HEADS // SSM_GROUPS // nh

    def body(x_ref, b_ref, c_ref, pdt_ref, hp_ref, st_ref, dy_ref, dx_ref, db_ref, dc_ref, dpdt_ref, dhp_ref, ds_scr, dhp_scr):
        first = pl.program_id(0) % steps_per_group == 0
        ds_scr[...] = jnp.zeros_like(ds_scr)
        dhp_scr[...] = jnp.zeros_like(dhp_scr)

        def step(i, carry):
            ci = nc - 1 - i
            sl = pl.ds(pl.multiple_of(ci * CHUNK, CHUNK), CHUNK)
            _, vjp = jax.vjp(_ssd_chunk_f, x_ref[sl, :], b_ref[sl, :], c_ref[sl, :], pdt_ref[sl, :], hp_ref[...], st_ref[0, ci])
            dx, db, dc, dpdt, dhp, dsprev = vjp((dy_ref[sl, :], ds_scr[...]))
            dx_ref[sl, :] = dx
            dpdt_ref[sl, :] = dpdt.astype(dpdt_ref.dtype)

            @pl.when(first)
            def _():
                db_ref[sl, :] = db
                dc_ref[sl, :] = dc

            @pl.when(jnp.logical_not(first))
            def _():
                db_ref[sl, :] += db
                dc_ref[sl, :] += dc

            ds_scr[...] = dsprev
            dhp_scr[...] += dhp
            return carry

        lax.fori_loop(0, nc, step, 0)
        dhp_ref[...] = dhp_scr[...]

    in_specs = _ssd_in_specs(t) + [
        pl.BlockSpec((1, nc, nh, SSM_HEAD_DIM, SSM_STATE), lambda s: (s, 0, 0, 0, 0)),
        pl.BlockSpec((t, SSD_X_WIDTH), lambda s: (0, s)),
    ]
    out_specs = [
        pl.BlockSpec((t, SSD_X_WIDTH), lambda s: (0, s)),
        pl.BlockSpec((t, LANES), lambda s: (0, s // steps_per_group)),
        pl.BlockSpec((t, LANES), lambda s: (0, s // steps_per_group)),
        pl.BlockSpec((t, LANES), lambda s: (0, s)),
        pl.BlockSpec((3, LANES), lambda s: (0, s)),
    ]
    out_shape = [
        jax.ShapeDtypeStruct((t, D_MODEL), F32),
        jax.ShapeDtypeStruct((t, SSM_BC), F32),
        jax.ShapeDtypeStruct((t, SSM_BC), F32),
        jax.ShapeDtypeStruct((t, SSD_STEPS * LANES), BF16),
        jax.ShapeDtypeStruct((3, SSD_STEPS * LANES), F32),
    ]
    return pl.pallas_call(
        body, name=name, grid=(SSD_STEPS,), in_specs=in_specs, out_specs=out_specs, out_shape=out_shape,
        scratch_shapes=[pltpu.VMEM((nh, SSM_HEAD_DIM, SSM_STATE), F32), pltpu.VMEM((3, LANES), F32)],
        compiler_params=_params("arbitrary"),
    )(xa, xa, xa, pdt, hp, states, dy)


ATTN_TQ = 256


def _attn_in_specs(t):
    return [
        pl.BlockSpec((ATTN_TQ, MLA_QPAD), lambda h, qi: (qi, h)),
        pl.BlockSpec((t, MLA_NOPE), lambda h, qi: (0, h)),
        pl.BlockSpec((t, LANES), lambda h, qi: (0, 0)),
        pl.BlockSpec((t, MLA_V), lambda h, qi: (0, h)),
        pl.BlockSpec((ATTN_TQ, LANES), lambda h, qi: (qi, 0)),
        pl.BlockSpec((ATTN_TQ, LANES), lambda h, qi: (qi, 0)),
    ]


def attn_fwd(name, q, kn, kpe, v, cos, sin):
    t = q.shape[0]

    def body(q_ref, kn_ref, kpe_ref, v_ref, cos_ref, sin_ref, o_ref):
        q0 = pl.program_id(1) * ATTN_TQ
        (o,) = _attn_f(q0, q_ref[...], kn_ref[...], kpe_ref[...], v_ref[...], cos_ref[...], sin_ref[...])
        o_ref[...] = o.astype(o_ref.dtype)

    return pl.pallas_call(
        body, name=name, grid=(MLA_HEADS, t // ATTN_TQ), in_specs=_attn_in_specs(t),
        out_specs=pl.BlockSpec((ATTN_TQ, MLA_V), lambda h, qi: (qi, h)),
        out_shape=jax.ShapeDtypeStruct((t, MLA_HEADS * MLA_V), BF16),
        compiler_params=_params("parallel", "parallel"),
    )(q, kn, kpe, v, cos, sin)


def attn_bwd(name, q, kn, kpe, v, cos, sin, do):
    t = q.shape[0]

    def body(q_ref, kn_ref, kpe_ref, v_ref, cos_ref, sin_ref, do_ref, dq_ref, dkn_ref, dkpe_ref, dv_ref):
        h, qi = pl.program_id(0), pl.program_id(1)
        q0 = qi * ATTN_TQ
        cos, sin = cos_ref[...], sin_ref[...]

        def g(qh, knv, kpev, vv):
            return _attn_f(q0, qh, knv, kpev, vv, cos, sin)

        _, vjp = jax.vjp(g, q_ref[...].astype(F32), kn_ref[...].astype(F32), kpe_ref[...].astype(F32), v_ref[...].astype(F32))
        dq, dkn, dkpe, dv = vjp((do_ref[...].astype(F32),))
        dq_ref[...] = dq.astype(dq_ref.dtype)

        @pl.when(qi == 0)
        def _():
            dkn_ref[...] = dkn
            dv_ref[...] = dv

        @pl.when(qi > 0)
        def _():
            dkn_ref[...] += dkn
            dv_ref[...] += dv

        start = jnp.logical_and(h == 0, qi == 0)

        @pl.when(start)
        def _():
            dkpe_ref[...] = dkpe

        @pl.when(jnp.logical_not(start))
        def _():
            dkpe_ref[...] += dkpe

    in_specs = _attn_in_specs(t) + [pl.BlockSpec((ATTN_TQ, MLA_V), lambda h, qi: (qi, h))]
    out_specs = [
        pl.BlockSpec((ATTN_TQ, MLA_QPAD), lambda h, qi: (qi, h)),
        pl.BlockSpec((t, MLA_NOPE), lambda h, qi: (0, h)),
        pl.BlockSpec((t, LANES), lambda h, qi: (0, 0)),
        pl.BlockSpec((t, MLA_V), lambda h, qi: (0, h)),
    ]
    out_shape = [
        jax.ShapeDtypeStruct((t, MLA_HEADS * MLA_QPAD), BF16),
        jax.ShapeDtypeStruct((t, MLA_HEADS * MLA_NOPE), F32),
        jax.ShapeDtypeStruct((t, LANES), F32),
        jax.ShapeDtypeStruct((t, MLA_HEADS * MLA_V), F32),
    ]
    return pl.pallas_call(
        body, name=name, grid=(MLA_HEADS, t // ATTN_TQ), in_specs=in_specs, out_specs=out_specs, out_shape=out_shape,
        compiler_params=_params("arbitrary", "arbitrary"),
    )(q, kn, kpe, v, cos, sin, do)


def final_loss(name, h, nf, target, tr=256):
    t, d = h.shape

    def body(h_ref, w_ref, t_ref, loss_ref, dh_ref, dw_ref):
        i = pl.program_id(0)
        tgt = t_ref[...]

        def f(hv, wv):
            err = _rms(hv, wv) - tgt
            return 0.5 * jnp.sum(jnp.mean(err * err, -1, keepdims=True), 0, keepdims=True)

        val, vjp = jax.vjp(f, h_ref[...], w_ref[...])
        dh, dw = vjp(jnp.ones((1, 1), F32))
        dh_ref[...] = dh
        tile = jnp.broadcast_to(val, loss_ref.shape)

        @pl.when(i == 0)
        def _():
            loss_ref[...] = tile
            dw_ref[...] = dw

        @pl.when(i > 0)
        def _():
            loss_ref[...] += tile
            dw_ref[...] += dw

    row = pl.BlockSpec((tr, d), lambda i: (i, 0))
    return pl.pallas_call(
        body, name=name, grid=(t // tr,), in_specs=[row, _full_spec(nf), row],
        out_specs=[pl.BlockSpec((8, LANES), lambda i: (0, 0)), row, _full_spec(nf)],
        out_shape=[jax.ShapeDtypeStruct((8, LANES), F32), jax.ShapeDtypeStruct((t, d), F32), jax.ShapeDtypeStruct(nf.shape, F32)],
        compiler_params=_params("arbitrary"),
    )(h, nf, target)


ANY = pl.BlockSpec(memory_space=pl.ANY)
CHIP_ORDER = ((0, 0), (0, 1), (1, 0), (1, 1))


def _place():
    return lax.axis_index("x"), lax.axis_index("y"), lax.axis_index("c")


def _other_chips(x, y):
    return [(1 - x, y), (x, 1 - y), (1 - x, 1 - y)]


def _device_slot():
    x, y, c = _place()
    return 4 * x + 2 * y + c


def _row_tile(rows):
    return next(t for t in (128, 64, 32, 16) if rows % t == 0)


def all_gather(name, shards):
    n = len(shards)

    def body(*refs):
        ins, outs = refs[:n], refs[n:2 * n]
        send_sems, recv_sems, local_sems = refs[2 * n:]
        x, y, c = _place()
        me, sibling = (x, y, c), (x, y, 1 - c)
        chips = _other_chips(x, y)

        def copy(k, j, block, to, from_input=False):
            dst = outs[k].at[4 * block[0] + 2 * block[1] + block[2]]
            return pltpu.make_async_remote_copy(
                src_ref=ins[k] if from_input else dst, dst_ref=dst,
                send_sem=send_sems.at[7 * k + j], recv_sem=recv_sems.at[7 * k + j],
                device_id=to, device_id_type=MESH)

        mine = [pltpu.make_async_copy(ins[k], outs[k].at[4 * x + 2 * y + c], local_sems.at[k]) for k in range(n)]
        for cp in mine:
            cp.start()
        first = []
        for k in range(n):
            first.append(copy(k, 0, me, sibling, True))
            first += [copy(k, 1 + j, me, (*chip, c), True) for j, chip in enumerate(chips)]
        for cp in first:
            cp.start()
        passed = []
        for j, chip in enumerate(chips):
            for k in range(n):
                copy(k, 1 + j, (*chip, c), me).wait_recv()
                fwd = copy(k, 4 + j, (*chip, c), sibling)
                fwd.start()
                passed.append(fwd)
        for k in range(n):
            copy(k, 0, sibling, me).wait_recv()
        for j, chip in enumerate(chips):
            for k in range(n):
                copy(k, 4 + j, (*chip, 1 - c), me).wait_recv()
        for cp in first + passed:
            cp.wait_send()
        for cp in mine:
            cp.wait()

    return pl.pallas_call(
        body, name=name, in_specs=[ANY] * n, out_specs=[ANY] * n,
        out_shape=[jax.ShapeDtypeStruct((N_DEV,) + s.shape, s.dtype) for s in shards],
        scratch_shapes=[pltpu.SemaphoreType.DMA((7 * n,)), pltpu.SemaphoreType.DMA((7 * n,)), pltpu.SemaphoreType.DMA((n,))],
    )(*shards)


def exchange_sibling(name, gs):
    n = len(gs)

    def body(*refs):
        ins, outs = refs[:n], refs[n:2 * n]
        send_sems, recv_sems = refs[2 * n:]
        x, y, c = _place()
        copies = []
        for k in range(n):
            for q, (cx, cy) in enumerate(CHIP_ORDER):
                copies.append(pltpu.make_async_remote_copy(
                    src_ref=ins[k].at[4 * cx + 2 * cy + (1 - c)], dst_ref=outs[k].at[q],
                    send_sem=send_sems.at[4 * k + q], recv_sem=recv_sems.at[4 * k + q],
                    device_id=(x, y, 1 - c), device_id_type=MESH))
        for cp in copies:
            cp.start()
        for cp in copies:
            cp.wait()

    return pl.pallas_call(
        body, name=name, in_specs=[ANY] * n, out_specs=[ANY] * n,
        out_shape=[jax.ShapeDtypeStruct((4,) + g.shape[1:], g.dtype) for g in gs],
        scratch_shapes=[pltpu.SemaphoreType.DMA((4 * n,)), pltpu.SemaphoreType.DMA((4 * n,))],
    )(*gs)


def exchange_chips(name, ps):
    n = len(ps)

    def body(*refs):
        ins, outs = refs[:n], refs[n:2 * n]
        send_sems, recv_sems = refs[2 * n:]
        x, y, c = _place()
        copies = []
        for k in range(n):
            for j, chip in enumerate(_other_chips(x, y)):
                copies.append(pltpu.make_async_remote_copy(
                    src_ref=ins[k].at[1 + j], dst_ref=outs[k].at[j],
                    send_sem=send_sems.at[3 * k + j], recv_sem=recv_sems.at[3 * k + j],
                    device_id=(*chip, c), device_id_type=MESH))
        for cp in copies:
            cp.start()
        for cp in copies:
            cp.wait()

    return pl.pallas_call(
        body, name=name, in_specs=[ANY] * n, out_specs=[ANY] * n,
        out_shape=[jax.ShapeDtypeStruct((3,) + p.shape[1:], p.dtype) for p in ps],
        scratch_shapes=[pltpu.SemaphoreType.DMA((3 * n,)), pltpu.SemaphoreType.DMA((3 * n,))],
    )(*ps)


def chip_sums(name, g, recv, tr=128):
    _, r, c = g.shape

    def body(g_ref, r_ref, o_ref):
        o_ref[...] = (g_ref[...].astype(F32) + r_ref[...].astype(F32)).astype(o_ref.dtype)

    def chip(i):
        x, y, _ = _place()
        return jnp.where(i % 2 == 1, 1 - x, x), jnp.where(i >= 2, 1 - y, y)

    def g_index(i, j):
        cx, cy = chip(i)
        return 4 * cx + 2 * cy + lax.axis_index("c"), j, 0

    def recv_index(i, j):
        cx, cy = chip(i)
        return 2 * cx + cy, j, 0

    return pl.pallas_call(
        body, name=name, grid=(4, r // tr),
        in_specs=[pl.BlockSpec((1, tr, c), g_index), pl.BlockSpec((1, tr, c), recv_index)],
        out_specs=pl.BlockSpec((1, tr, c), lambda i, j: (i, j, 0)),
        out_shape=jax.ShapeDtypeStruct((4, r, c), g.dtype),
        compiler_params=_params("parallel", "parallel"),
    )(g, recv)


def sum_parts(name, parts, tr=128):
    _, r, c = parts[0][0].shape

    def body(*refs):
        total = refs[0][0].astype(F32)
        for ref in refs[1:-1]:
            total = total + ref[0].astype(F32)
        refs[-1][...] = total

    return pl.pallas_call(
        body, name=name, grid=(r // tr,),
        in_specs=[pl.BlockSpec((1, tr, c), lambda i, s=s: (s, i, 0)) for _, s in parts],
        out_specs=pl.BlockSpec((tr, c), lambda i: (i, 0)), out_shape=jax.ShapeDtypeStruct((r, c), F32),
        compiler_params=_params("parallel"),
    )(*[a for a, _ in parts])


def adamw(name, parts, w, m, v, tr=128, part=0, prev=None):
    _, r, c = parts[0][0].shape
    np_ = len(parts)
    first = part * (r // tr)

    def body(*refs):
        g = refs[0][0].astype(F32)
        for ref in refs[1:np_]:
            g = g + ref[0].astype(F32)
        w_ref, m_ref, v_ref = refs[np_:np_ + 3]
        g_out, d_out, m_out, v_out = refs[-4:]
        new_m = ADAM_B1 * m_ref[...] + (1.0 - ADAM_B1) * g
        new_v = ADAM_B2 * v_ref[...] + (1.0 - ADAM_B2) * (g * g)
        m_hat = new_m / (1.0 - ADAM_B1 ** ADAM_STEP)
        v_hat = new_v / (1.0 - ADAM_B2 ** ADAM_STEP)
        g_out[...] = g
        d_out[...] = -ADAM_LR * (m_hat / (jnp.sqrt(v_hat) + ADAM_EPS) + ADAM_WD * w_ref[...])
        m_out[...] = new_m
        v_out[...] = new_v

    tile = pl.BlockSpec((tr, c), lambda i: (first + i, 0))
    in_specs = [pl.BlockSpec((1, tr, c), lambda i, s=s: (s, i, 0)) for _, s in parts] + [tile] * 3
    ins = [a for a, _ in parts] + [w, m, v]
    aliases = {}
    if prev is not None:
        aliases = {len(ins) + k: k for k in range(4)}
        in_specs += [ANY] * 4
        ins += list(prev)
    return pl.pallas_call(
        body, name=name, grid=(r // tr,), in_specs=in_specs,
        out_specs=[tile] * 4, out_shape=[jax.ShapeDtypeStruct(w.shape, F32)] * 4,
        input_output_aliases=aliases, compiler_params=_params("parallel"),
    )(*ins)


HBM = pl.BlockSpec(memory_space=pltpu.HBM)
SEM = pl.BlockSpec(memory_space=pltpu.SEMAPHORE)
SIDE_EFFECT = pltpu.SideEffectType.DATAFLOW_SIDE_EFFECTING


def _split_copies(plan, src_refs, land_refs, send_sems, recv_sems):
    copies = []
    for i, (k, src_slot, land_slot, device) in enumerate(plan(*_place())):
        copies.append(pltpu.make_async_remote_copy(
            src_ref=src_refs[k] if src_slot is None else src_refs[k].at[src_slot], dst_ref=land_refs[k].at[land_slot],
            send_sem=send_sems.at[i], recv_sem=recv_sems.at[i], device_id=device, device_id_type=MESH))
    return copies


def split_start(name, srcs, land_shapes, plan, n_copies):
    n = len(srcs)

    def body(*refs):
        src_refs, land_refs = refs[:n], refs[n:2 * n]
        send_sems, recv_sems, token = refs[2 * n], refs[2 * n + 1], refs[-1]
        for cp in _split_copies(plan, src_refs, land_refs, send_sems, recv_sems):
            cp.start()
        token[...] = jnp.zeros_like(token)

    lands = [lax.empty(shape, s.dtype) for shape, s in zip(land_shapes, srcs)]
    ins = [pltpu.with_memory_space_constraint(a, pltpu.HBM) for a in list(srcs) + lands]
    out = pl.pallas_call(
        body, name=name,
        out_shape=(pltpu.SemaphoreType.DMA((n_copies,)), pltpu.SemaphoreType.DMA((n_copies,)),
                   *[pltpu.HBM(a.shape, a.dtype) for a in ins], jax.ShapeDtypeStruct((8, LANES), F32)),
        in_specs=[HBM] * (2 * n), out_specs=(SEM, SEM, *[HBM] * (2 * n), pl.BlockSpec(memory_space=pltpu.VMEM)),
        input_output_aliases={i: 2 + i for i in range(2 * n)},
        compiler_params=pltpu.CompilerParams(has_side_effects=SIDE_EFFECT),
    )(*ins)
    return out[0], out[1], list(out[2:2 + n]), list(out[2 + n:2 + 2 * n]), out[-1]


def split_wait(name, handle, plan, after):
    send_sems, recv_sems, srcs, lands, _ = handle
    n = len(srcs)

    def body(*refs):
        src_refs, land_refs = refs[:n], refs[n:2 * n]
        for cp in _split_copies(plan, src_refs, land_refs, refs[2 * n], refs[2 * n + 1]):
            cp.wait_send()
            cp.wait_recv()

    out = pl.pallas_call(
        body, name=name, out_shape=tuple(pltpu.HBM(a.shape, a.dtype) for a in srcs + lands),
        in_specs=[HBM] * (2 * n) + [SEM, SEM, ANY], out_specs=tuple([HBM] * (2 * n)),
        input_output_aliases={i: i for i in range(2 * n)},
        compiler_params=pltpu.CompilerParams(has_side_effects=SIDE_EFFECT),
    )(*srcs, *lands, send_sems, recv_sems, after)
    return list(out[:n]), list(out[n:])


def gather_plan(n):
    def plan(x, y, c):
        me = 4 * x + 2 * y + c
        peers = [(x, y, 1 - c)] + [(*chip, c) for chip in _other_chips(x, y)]
        return [(k, None, me, peer) for k in range(n) for peer in peers]
    return plan


def chips_plan(n):
    def plan(x, y, c):
        return [(k, 1 + j, j, (*chip, c)) for k in range(n) for j, chip in enumerate(_other_chips(x, y))]
    return plan


def gather_finish(name, shards, gathered):
    n = len(shards)

    def body(*refs):
        ins, outs = refs[:n], refs[2 * n:3 * n]
        send_sems, recv_sems, local_sems = refs[3 * n:]
        x, y, c = _place()
        mine = [pltpu.make_async_copy(ins[k], outs[k].at[4 * x + 2 * y + c], local_sems.at[k]) for k in range(n)]
        for cp in mine:
            cp.start()

        def passed_on(k, j, core):
            cx, cy = _other_chips(x, y)[j]
            blk = outs[k].at[4 * cx + 2 * cy + core]
            return pltpu.make_async_remote_copy(
                src_ref=blk, dst_ref=blk, send_sem=send_sems.at[3 * k + j], recv_sem=recv_sems.at[3 * k + j],
                device_id=(x, y, 1 - c), device_id_type=MESH)

        pairs = [(k, j) for k in range(n) for j in range(3)]
        sends = [passed_on(k, j, c) for k, j in pairs]
        for cp in sends:
            cp.start()
        for k, j in pairs:
            passed_on(k, j, 1 - c).wait_recv()
        for cp in sends:
            cp.wait_send()
        for cp in mine:
            cp.wait()

    return pl.pallas_call(
        body, name=name, in_specs=[ANY] * (2 * n), out_specs=[ANY] * n,
        out_shape=[jax.ShapeDtypeStruct(g.shape, g.dtype) for g in gathered],
        input_output_aliases={n + k: k for k in range(n)},
        scratch_shapes=[pltpu.SemaphoreType.DMA((3 * n,)), pltpu.SemaphoreType.DMA((3 * n,)), pltpu.SemaphoreType.DMA((n,))],
    )(*shards, *gathered)


ROW_TILE = 256
COL_TILE = 256


def _rms_fwd(tag, h, w):
    return rows_fwd(tag, _rms_f, [(h, D_MODEL, 0)], [w], [(D_MODEL, BF16)], ROW_TILE)[0]


def _rms_bwd(tag, h, w, dhn, dres):
    (dh,), (dw,) = rows_bwd(tag, _rms_f, [(h, D_MODEL, 0)], [w], [(dhn, D_MODEL, 0)], ROW_TILE, [F32], add=dres)
    return dh, dw


def even_fwd(tag, h, w, p):
    hn = _rms_fwd(tag + "_rms", h, p["nm"])
    uv = matmul(tag + "_uv", hn, w["uv"])
    z = matmul(tag + "_z", hn, w["z"])
    xbc = matmul(tag + "_xbc", hn, w["xbc"])
    pdt = matmul(tag + "_dt", hn, w["dt"])
    gm = [p["lng"], p["lnb"], p["ws"], p["bst"]]
    ya = rows_fwd(tag + "_gmlp", _gmlp_f, [(uv, 2 * D_MODEL, 0)], gm, [(D_MODEL, BF16)], GM_BLOCK)[0]
    xa = cols_fwd(tag + "_conv", _conv_silu_f, [(xbc, 0)], [p["cw"], p["cb"]], F32, COL_TILE)
    y, states = ssd_fwd(tag + "_ssd", xa, pdt, p["hp"])
    yb = rows_fwd(tag + "_gate", _gate_norm_f, [(y, D_MODEL, 0), (z, D_MODEL, 0)], [p["nw"]], [(D_MODEL, BF16)], ROW_TILE)[0]
    h1 = matmul(tag + "_out_b", yb, w["out_bot"], res=matmul(tag + "_out_a", ya, w["out_top"], res=h))
    return h1, dict(h=h, hn=hn, uv=uv, z=z, xbc=xbc, pdt=pdt, xa=xa, y=y, states=states, ya=ya, yb=yb)


def even_bwd(tag, dh1, s, w, p):
    dya = matmul(tag + "_dya", dh1, w["out_top"], tb=True)
    dyb = matmul(tag + "_dyb", dh1, w["out_bot"], tb=True)
    gw = dict(out_top=matmul(tag + "_gwa", s["ya"], dh1, ta=True, out_dtype=BF16),
              out_bot=matmul(tag + "_gwb", s["yb"], dh1, ta=True, out_dtype=BF16))
    (dy, dz), (dnw,) = rows_bwd(tag + "_gate_b", _gate_norm_f, [(s["y"], D_MODEL, 0), (s["z"], D_MODEL, 0)], [p["nw"]],
                                [(dyb, D_MODEL, 0)], ROW_TILE, [F32, BF16])
    dxs, dbm, dcm, dpdt, dhp = ssd_bwd(tag + "_ssd_b", s["xa"], s["pdt"], p["hp"], s["states"], dy)
    dxa = jnp.concatenate([dxs, dbm, dcm], axis=1)
    (dxbc,), (dcw, dcb) = cols_bwd(tag + "_conv_b", _conv_silu_f, [(s["xbc"], 0)], [p["cw"], p["cb"]], dxa, COL_TILE, BF16)
    gm = [p["lng"], p["lnb"], p["ws"], p["bst"]]
    (duv,), (dlng, dlnb, dws, dbst) = rows_bwd(tag + "_gmlp_b", _gmlp_f, [(s["uv"], 2 * D_MODEL, 0)], gm,
                                               [(dya, D_MODEL, 0)], GM_BLOCK, [BF16])
    dhn = None
    for key, d in (("uv", duv), ("z", dz), ("xbc", dxbc), ("dt", dpdt)):
        dhn = matmul(f"{tag}_dx_{key}", d, w[key], tb=True, res=dhn)
        gw[key] = matmul(f"{tag}_gw_{key}", s["hn"], d, ta=True, out_dtype=BF16)
    dh, dnm = _rms_bwd(tag + "_rms_b", s["h"], p["nm"], dhn, dh1)
    gp = dict(nm=dnm, lng=dlng, lnb=dlnb, ws=dws, bst=dbst, cw=dcw, cb=dcb, hp=dhp, nw=dnw)
    return dh, gw, gp


def odd_fwd(tag, h, w, p, cos, sin):
    hn = _rms_fwd(tag + "_rms", h, p["nm"])
    proj = matmul(tag + "_in", hn, w["in"])
    cq, ckv, kpe = rows_fwd(tag + "_qkvn", _qkv_norm_f, [(proj, ODD_IN_PAD, 0), (cos, LANES, 0), (sin, LANES, 0)],
                            [p["qn"], p["kvn"]], [(MLA_RANK, BF16), (MLA_RANK, BF16), (LANES, F32)], ROW_TILE)
    q = matmul(tag + "_q", cq, w["uq"])
    kn = matmul(tag + "_kn", ckv, w["kn"], out_dtype=BF16)
    v = matmul(tag + "_v", ckv, w["v"], out_dtype=BF16)
    o = attn_fwd(tag + "_attn", q, kn, kpe, v, cos, sin)
    h1 = matmul(tag + "_o", o, w["o"], res=h)
    return h1, dict(h=h, hn=hn, proj=proj, cq=cq, ckv=ckv, kpe=kpe, q=q, kn=kn, v=v, o=o)


def odd_bwd(tag, dh1, s, w, p, cos, sin):
    do = matmul(tag + "_do", dh1, w["o"], tb=True)
    gw = dict(o=matmul(tag + "_gw_o", s["o"], dh1, ta=True, out_dtype=BF16))
    dq, dkn, dkpe, dv = attn_bwd(tag + "_attn_b", s["q"], s["kn"], s["kpe"], s["v"], cos, sin, do)
    dcq = matmul(tag + "_dcq", dq, w["uq"], tb=True)
    gw["uq"] = matmul(tag + "_gw_uq", s["cq"], dq, ta=True, out_dtype=BF16)
    dckv = matmul(tag + "_dckv_v", dv, w["v"], tb=True, res=matmul(tag + "_dckv_k", dkn, w["kn"], tb=True))
    gw["kn"] = matmul(tag + "_gw_kn", s["ckv"], dkn, ta=True, out_dtype=BF16)
    gw["v"] = matmul(tag + "_gw_v", s["ckv"], dv, ta=True, out_dtype=BF16)
    (dproj,), (dqn, dkvn) = rows_bwd(
        tag + "_qkvn_b", _qkv_norm_f, [(s["proj"], ODD_IN_PAD, 0), (cos, LANES, 0), (sin, LANES, 0)], [p["qn"], p["kvn"]],
        [(dcq, MLA_RANK, 0), (dckv, MLA_RANK, 0), (dkpe, LANES, 0)], ROW_TILE, [BF16], n_nondiff=2)
    dhn = matmul(tag + "_dx_in", dproj, w["in"], tb=True)
    gw["in"] = matmul(tag + "_gw_in", s["hn"], dproj, ta=True, out_dtype=BF16)
    dh, dnm = _rms_bwd(tag + "_rms_b", s["h"], p["nm"], dhn, dh1)
    return dh, gw, dict(nm=dnm, qn=dqn, kvn=dkvn)


def ffn_fwd(tag, h, w, p):
    hn = _rms_fwd(tag + "_rms", h, p["nf"])
    g = matmul(tag + "_up_g", hn, w["up_g"])
    val = matmul(tag + "_up_v", hn, w["up_v"])
    act = cols_fwd(tag + "_act", _ffn_act_f, [(g, 0), (val, 0)], [p["fcw"], p["fcb"]], BF16, COL_TILE)
    h2 = matmul(tag + "_down", act, w["down"], res=h)
    return h2, dict(h=h, hn=hn, g=g, val=val, act=act)


def ffn_bwd(tag, dh2, s, w, p):
    dact = matmul(tag + "_dact", dh2, w["down"], tb=True)
    gw = dict(down=matmul(tag + "_gw_down", s["act"], dh2, ta=True, out_dtype=BF16))
    (dg, dval), (dfcw, dfcb) = cols_bwd(tag + "_act_b", _ffn_act_f, [(s["g"], 0), (s["val"], 0)], [p["fcw"], p["fcb"]],
                                        dact, COL_TILE, BF16)
    dhn = matmul(tag + "_dx_v", dval, w["up_v"], tb=True, res=matmul(tag + "_dx_g", dg, w["up_g"], tb=True))
    gw["up_g"] = matmul(tag + "_gw_up_g", s["hn"], dg, ta=True, out_dtype=BF16)
    gw["up_v"] = matmul(tag + "_gw_up_v", s["hn"], dval, ta=True, out_dtype=BF16)
    dh, dnf = _rms_bwd(tag + "_rms_b", s["h"], p["nf"], dhn, dh2)
    return dh, gw, dict(nf=dnf, fcw=dfcw, fcb=dfcb)


def _cols_from_slots(g):
    return jnp.moveaxis(g, 0, 1).reshape(g.shape[1], N_DEV * g.shape[2])


def _slots_from_cols(wmat):
    k, n = wmat.shape
    return jnp.moveaxis(wmat.reshape(k, N_DEV, n // N_DEV), 1, 0)


def _pad_last(a, width):
    return jnp.pad(a, [(0, 0)] * (a.ndim - 1) + [(0, width - a.shape[-1])])


def _heads_to_lanes(a):
    lead = a.shape[:-1]
    return _pad_last(a.reshape(lead + (SSD_STEPS, SSD_HEADS_PER_STEP)), LANES).reshape(lead + (SSD_STEPS * LANES,))


def _lanes_to_heads(a):
    lead = a.shape[:-1]
    return a.reshape(lead + (SSD_STEPS, LANES))[..., :SSD_HEADS_PER_STEP].reshape(lead + (SSM_HEADS,))


def prep_even(g_in, g_out):
    wn = _cols_from_slots(g_in)
    o1, o2, o3 = 2 * D_MODEL, 3 * D_MODEL, 3 * D_MODEL + SSM_CONV_DIM
    out = g_out.reshape(2 * D_MODEL, D_MODEL)
    return dict(uv=wn[:, :o1], z=wn[:, o1:o2], xbc=wn[:, o2:o3], dt=_heads_to_lanes(wn[:, o3:]),
                out_top=out[:D_MODEL], out_bot=out[D_MODEL:])


def unprep_even(gw):
    wn = jnp.concatenate([gw["uv"], gw["z"], gw["xbc"], _lanes_to_heads(gw["dt"])], axis=1)
    return _slots_from_cols(wn), jnp.concatenate([gw["out_top"], gw["out_bot"]], axis=0).reshape(N_DEV, -1, D_MODEL)


def prep_odd(g_in, g_uq, g_ukv, g_o):
    uq = _cols_from_slots(g_uq).reshape(MLA_RANK, MLA_HEADS, MLA_QK)
    ukv = _cols_from_slots(g_ukv).reshape(MLA_RANK, MLA_HEADS, MLA_NOPE + MLA_V)
    return dict(**{"in": _pad_last(g_in.reshape(D_MODEL, ODD_IN), ODD_IN_PAD)},
                uq=_pad_last(uq, MLA_QPAD).reshape(MLA_RANK, MLA_HEADS * MLA_QPAD),
                kn=ukv[:, :, :MLA_NOPE].reshape(MLA_RANK, MLA_HEADS * MLA_NOPE),
                v=ukv[:, :, MLA_NOPE:].reshape(MLA_RANK, MLA_HEADS * MLA_V),
                o=g_o.reshape(MLA_HEADS * MLA_V, D_MODEL))


def unprep_odd(gw):
    uq = gw["uq"].reshape(MLA_RANK, MLA_HEADS, MLA_QPAD)[:, :, :MLA_QK].reshape(MLA_RANK, MLA_HEADS * MLA_QK)
    ukv = jnp.concatenate([gw["kn"].reshape(MLA_RANK, MLA_HEADS, MLA_NOPE), gw["v"].reshape(MLA_RANK, MLA_HEADS, MLA_V)], axis=2)
    return (gw["in"][:, :ODD_IN].reshape(N_DEV, -1, ODD_IN), _slots_from_cols(uq),
            _slots_from_cols(ukv.reshape(MLA_RANK, -1)), gw["o"].reshape(N_DEV, -1, D_MODEL))


def prep_ffn(g_up, g_down):
    up = _cols_from_slots(g_up)
    return dict(up_g=up[:, :D_FF], up_v=up[:, D_FF:], down=g_down.reshape(D_FF, D_MODEL))


def unprep_ffn(gw):
    return _slots_from_cols(jnp.concatenate([gw["up_g"], gw["up_v"]], axis=1)), gw["down"].reshape(N_DEV, -1, D_MODEL)


SMALL_TILE = LANES * LANES


def _pack(arrs):
    flat = jnp.concatenate([a.reshape(-1).astype(F32) for a in arrs])
    size = -(-flat.shape[0] // SMALL_TILE) * SMALL_TILE
    return jnp.pad(flat, (0, size - flat.shape[0])).reshape(-1, LANES)


def _unpack(packed, shapes, lead=()):
    flat = packed.reshape(lead + (-1,))
    out, off = [], 0
    for shp in shapes:
        size = math.prod(shp)
        out.append(flat[..., off:off + size].reshape(lead + tuple(shp)))
        off += size
    return out


SMALL_SHARDED = {"ev_gm_ln_g": 2, "ev_gm_ln_b": 2, "ev_conv_w": 2, "od_q_norm": 1, "od_kv_norm": 1, "ff_conv_w": 2}
SMALL_REPLICATED = ["norm_mix", "norm_ffn", "norm_final", "ev_gm_ws", "ev_gm_bs", "ev_conv_b", "ev_dt_bias", "ev_a_log",
                    "ev_d_skip", "ev_ssm_norm_w", "ff_conv_b"]
MATRICES = {"ev_w_in": (2, 2048, 1156), "ev_w_out": (2, 512, 2048), "od_w_in": (2, 256, 1088), "od_w_uq": (2, 512, 384),
            "od_w_ukv": (2, 512, 512), "od_w_o": (2, 256, 2048), "ff_w_up": (4, 2048, 1408), "ff_w_down": (4, 704, 2048)}
WEIGHT_ORDER = ["norm_mix", "norm_ffn", "norm_final", "ev_w_in", "ev_gm_ln_g", "ev_gm_ln_b", "ev_gm_ws", "ev_gm_bs",
                "ev_conv_w", "ev_conv_b", "ev_dt_bias", "ev_a_log", "ev_d_skip", "ev_ssm_norm_w", "ev_w_out", "od_w_in",
                "od_q_norm", "od_kv_norm", "od_w_uq", "od_w_ukv", "od_w_o", "ff_w_up", "ff_conv_w", "ff_conv_b", "ff_w_down"]


def _full_from_shards(name, gathered):
    ax = SMALL_SHARDED[name]
    moved = jnp.moveaxis(gathered, 0, ax)
    shp = moved.shape
    return moved.reshape(shp[:ax] + (shp[ax] * shp[ax + 1],) + shp[ax + 2:])


def _my_shard(name, full, dev):
    ax = SMALL_SHARDED[name]
    shp = full.shape
    split = full.reshape(shp[:ax] + (N_DEV, shp[ax] // N_DEV) + shp[ax + 1:])
    return lax.dynamic_index_in_dim(split, dev, axis=ax, keepdims=False)


def _even_small(sm, j):
    row = lambda a: a.reshape(1, -1)
    hp = jnp.stack([sm["ev_dt_bias"][j], sm["ev_a_log"][j], sm["ev_d_skip"][j]])
    return dict(nm=row(sm["norm_mix"][2 * j]), lng=row(sm["ev_gm_ln_g"][j]), lnb=row(sm["ev_gm_ln_b"][j]),
                ws=sm["ev_gm_ws"][j], bst=sm["ev_gm_bs"][j].T, cw=sm["ev_conv_w"][j], cb=row(sm["ev_conv_b"][j]),
                hp=_heads_to_lanes(hp), nw=row(sm["ev_ssm_norm_w"][j]))


def _odd_small(sm, j):
    row = lambda a: a.reshape(1, -1)
    return dict(nm=row(sm["norm_mix"][2 * j + 1]), qn=row(sm["od_q_norm"][j]), kvn=row(sm["od_kv_norm"][j]))


def _ffn_small(sm, layer):
    row = lambda a: a.reshape(1, -1)
    return dict(nf=row(sm["norm_ffn"][layer]), fcw=sm["ff_conv_w"][layer], fcb=row(sm["ff_conv_b"][layer]))


def _rope_tables(positions):
    inv_freq = ROPE_THETA ** (-jnp.arange(0, MLA_ROPE, 2, dtype=F32) / MLA_ROPE)
    ang = positions.astype(F32).reshape(-1, 1) * inv_freq
    cos, sin = jnp.cos(ang), jnp.sin(ang)
    return _pad_last(jnp.concatenate([cos, cos], axis=1), LANES), _pad_last(jnp.concatenate([-sin, sin], axis=1), LANES)


def local_step(x, positions, target, sm, fetch_weights, emit_grads):
    cos, sin = _rope_tables(positions)
    h, saved, weights = x, [], []
    for layer in range(4):
        j, tag = layer // 2, f"l{layer}"
        w, zero = fetch_weights(layer, h)
        weights.append(w)
        if layer % 2 == 0:
            pm = _even_small(sm, j)
            pm["nm"] = pm["nm"] + zero
            h, sv = even_fwd(tag, h, w["mix"], pm)
        else:
            pm = _odd_small(sm, j)
            pm["nm"] = pm["nm"] + zero
            h, sv = odd_fwd(tag, h, w["mix"], pm, cos, sin)
        pf = _ffn_small(sm, layer)
        h, sf = ffn_fwd(tag + "f", h, w["ffn"], pf)
        saved.append((pm, sv, pf, sf))
    loss_tile, dh, dnfinal = final_loss("final_loss", h, sm["norm_final"].reshape(1, -1), target)
    gs = {k: [None] * v.shape[0] for k, v in sm.items() if k != "norm_final"}
    gs["norm_final"] = dnfinal.reshape(-1)
    zero = 0.0
    for layer in reversed(range(4)):
        j, tag = layer // 2, f"l{layer}"
        pm, sv, pf, sf = saved[layer]
        pf = dict(pf, nf=pf["nf"] + zero)
        dh, gwf, gpf = ffn_bwd(tag + "f", dh, sf, weights[layer]["ffn"], pf)
        gs["norm_ffn"][layer], gs["ff_conv_w"][layer], gs["ff_conv_b"][layer] = gpf["nf"][0], gpf["fcw"], gpf["fcb"][0]
        if layer % 2 == 0:
            dh, gwm, gp = even_bwd(tag, dh, sv, weights[layer]["mix"], pm)
            hp = _lanes_to_heads(gp["hp"])
            gs["norm_mix"][layer] = gp["nm"][0]
            gs["ev_gm_ln_g"][j], gs["ev_gm_ln_b"][j] = gp["lng"].reshape(GM_GROUPS, -1), gp["lnb"].reshape(GM_GROUPS, -1)
            gs["ev_gm_ws"][j], gs["ev_gm_bs"][j] = gp["ws"], gp["bst"].T
            gs["ev_conv_w"][j], gs["ev_conv_b"][j] = gp["cw"], gp["cb"][0]
            gs["ev_dt_bias"][j], gs["ev_a_log"][j], gs["ev_d_skip"][j] = hp[0], hp[1], hp[2]
            gs["ev_ssm_norm_w"][j] = gp["nw"][0]
        else:
            dh, gwm, gp = odd_bwd(tag, dh, sv, weights[layer]["mix"], pm, cos, sin)
            gs["norm_mix"][layer] = gp["nm"][0]
            gs["od_q_norm"][j], gs["od_kv_norm"][j] = gp["qn"][0], gp["kvn"][0]
        zero = emit_grads(layer, gwm, gwf, dh)
    gs = {k: (v if k == "norm_final" else jnp.stack(v)) for k, v in gs.items()}
    return loss_tile[0, 0], dh, gs


def kernel(x, positions, norm_mix, norm_ffn, norm_final, ev_w_in, ev_gm_ln_g, ev_gm_ln_b, ev_gm_ws, ev_gm_bs, ev_conv_w, ev_conv_b, ev_dt_bias, ev_a_log, ev_d_skip, ev_ssm_norm_w, ev_w_out, od_w_in, od_q_norm, od_kv_norm, od_w_uq, od_w_ukv, od_w_o, ff_w_up, ff_conv_w, ff_conv_b, ff_w_down, loss_target, m_norm_mix, m_norm_ffn, m_norm_final, m_ev_w_in, m_ev_gm_ln_g, m_ev_gm_ln_b, m_ev_gm_ws, m_ev_gm_bs, m_ev_conv_w, m_ev_conv_b, m_ev_dt_bias, m_ev_a_log, m_ev_d_skip, m_ev_ssm_norm_w, m_ev_w_out, m_od_w_in, m_od_q_norm, m_od_kv_norm, m_od_w_uq, m_od_w_ukv, m_od_w_o, m_ff_w_up, m_ff_conv_w, m_ff_conv_b, m_ff_w_down, v_norm_mix, v_norm_ffn, v_norm_final, v_ev_w_in, v_ev_gm_ln_g, v_ev_gm_ln_b, v_ev_gm_ws, v_ev_gm_bs, v_ev_conv_w, v_ev_conv_b, v_ev_dt_bias, v_ev_a_log, v_ev_d_skip, v_ev_ssm_norm_w, v_ev_w_out, v_od_w_in, v_od_q_norm, v_od_kv_norm, v_od_w_uq, v_od_w_ukv, v_od_w_o, v_ff_w_up, v_ff_conv_w, v_ff_conv_b, v_ff_w_down):
    args = dict(locals())
    wts = {n: args[n] for n in WEIGHT_ORDER}
    mom = {n: args["m_" + n] for n in WEIGHT_ORDER}
    var = {n: args["v_" + n] for n in WEIGHT_ORDER}
    dev = _device_slot()

    small_names = list(SMALL_SHARDED)
    small_shapes = [wts[n].shape for n in small_names]
    (small_all,) = all_gather("ag_small", [_pack([wts[n] for n in small_names])])
    small_full = _unpack(small_all, small_shapes, lead=(N_DEV,))
    sm = {n: _full_from_shards(n, g) for n, g in zip(small_names, small_full)}
    sm.update({n: wts[n] for n in SMALL_REPLICATED})

    bf = {n: wts[n].astype(BF16) for n in MATRICES}

    def layer_matrices(layer):
        mix = ["ev_w_in", "ev_w_out"] if layer % 2 == 0 else ["od_w_in", "od_w_uq", "od_w_ukv", "od_w_o"]
        return [(n, layer // 2) for n in mix] + [("ff_w_up", layer), ("ff_w_down", layer)]

    def start_gather(layer, earlier=None):
        shards = [bf[n][i] for n, i in layer_matrices(layer)]
        if earlier is not None:
            shards, _ = lax.optimization_barrier((shards, earlier))
        return split_start(f"ag_l{layer}_start", shards, [(N_DEV,) + s.shape for s in shards],
                           gather_plan(len(shards)), 4 * len(shards))

    gathers = {0: start_gather(0)}

    def fetch_weights(layer, h):
        shards, landed = split_wait(f"ag_l{layer}_wait", gathers.pop(layer), gather_plan(len(layer_matrices(layer))), h)
        g = gather_finish(f"ag_l{layer}_finish", shards, landed)
        zero = 0.0
        if layer < 3:
            gathers[layer + 1] = start_gather(layer + 1, g[0])
            zero = gathers[layer + 1][4][0, 0]
        mix = prep_even(g[0], g[1]) if layer % 2 == 0 else prep_odd(g[0], g[1], g[2], g[3])
        return dict(mix=mix, ffn=prep_ffn(g[-2], g[-1])), zero

    scatters = []
    out = {n: None for n in MATRICES}

    def finish_scatter(after):
        layer, handle = scatters.pop()
        mats = layer_matrices(layer)
        sums, recv = split_wait(f"rs_l{layer}_wait", handle, chips_plan(len(mats)), after)
        for (n, i), p_, r_ in zip(mats, sums, recv):
            layers, rows, cols = MATRICES[n]
            two_d = lambda a: a.reshape(layers * rows, cols)
            out[n] = adamw(f"adamw_{n}_{i}", [(p_, 0), (r_, 0), (r_, 1), (r_, 2)], two_d(wts[n]), two_d(mom[n]),
                           two_d(var[n]), tr=_row_tile(rows), part=i, prev=out[n])

    def emit_grads(layer, gwm, gwf, dh):
        if scatters:
            finish_scatter(dh)
        send = list(unprep_even(gwm) if layer % 2 == 0 else unprep_odd(gwm)) + list(unprep_ffn(gwf))
        from_sibling = exchange_sibling(f"rs_l{layer}_sibling", send)
        sums = [chip_sums(f"rs_l{layer}_add{k}", g, r, tr=_row_tile(g.shape[1])) for k, (g, r) in enumerate(zip(send, from_sibling))]
        handle = split_start(f"rs_l{layer}_start", sums, [(3,) + s.shape[1:] for s in sums], chips_plan(len(sums)), 3 * len(sums))
        scatters.append((layer, handle))
        return handle[4][0, 0]

    loss_local, dx, gs = local_step(x[0], positions[0], loss_target[0], sm, fetch_weights, emit_grads)
    loss = lax.psum(loss_local, ("x", "y", "c"))

    all_small = small_names + SMALL_REPLICATED
    (partials,) = all_gather("ar_small", [_pack([gs[n] for n in all_small])])
    total = sum_parts("ar_small_sum", [(partials, s) for s in range(N_DEV)])
    g_full = dict(zip(all_small, _unpack(total, [gs[n].shape for n in all_small])))
    g_mine = {n: (_my_shard(n, g_full[n], dev) if n in SMALL_SHARDED else g_full[n]) for n in all_small}
    packed = [_pack([d[n] for n in all_small]) for d in (g_mine, wts, mom, var)]
    res = adamw("adamw_small", [(packed[0][None], 0)], packed[1], packed[2], packed[3])
    unpacked = [_unpack(a, [wts[n].shape for n in all_small]) for a in res]
    for i, n in enumerate(all_small):
        out[n] = [u[i] for u in unpacked]
    finish_scatter(res[0])
    for n in MATRICES:
        out[n] = [a.reshape(wts[n].shape) for a in out[n]]

    return (loss, dx[None], *[out[n][0] for n in WEIGHT_ORDER], *[out[n][1] for n in WEIGHT_ORDER],
            *[out[n][2] for n in WEIGHT_ORDER], *[out[n][3] for n in WEIGHT_ORDER])
```

```python
import functools
import math

import jax
import jax.numpy as jnp
from jax import lax
from jax.experimental import pallas as pl
from jax.experimental.pallas import tpu as pltpu

F32 = jnp.float32
BF16 = jnp.bfloat16
MESH = pl.DeviceIdType.MESH

V7X_VMEM_LIMIT_BYTES = 56 * 1024 * 1024
LANES = 128

EPS = 1e-6
D_MODEL = 2048
CHUNK = 64
GM_BLOCK = 128
GM_GROUPS = 8
GM_GROUP_DIM = D_MODEL // GM_GROUPS
SSM_HEADS = 32
SSM_HEAD_DIM = 64
SSM_GROUPS = 4
SSM_STATE = 128
SSM_CONV = 4
SSM_BC = SSM_GROUPS * SSM_STATE
SSM_CONV_DIM = D_MODEL + 2 * SSM_BC
SSD_HEADS_PER_STEP = 4
SSD_STEPS = SSM_HEADS // SSD_HEADS_PER_STEP
SSD_X_WIDTH = SSD_HEADS_PER_STEP * SSM_HEAD_DIM
MLA_HEADS = 16
MLA_RANK = 512
MLA_NOPE = 128
MLA_ROPE = 64
MLA_V = 128
MLA_QK = MLA_NOPE + MLA_ROPE
MLA_QPAD = 2 * LANES
ODD_IN = 2 * MLA_RANK + MLA_ROPE
ODD_IN_PAD = 2 * MLA_RANK + LANES
D_FF = 5632
ROPE_THETA = 10000.0
N_DEV = 8

ADAM_LR, ADAM_B1, ADAM_B2, ADAM_EPS, ADAM_WD, ADAM_STEP = 0.001, 0.9, 0.999, 1e-08, 0.01, 10


def _params(*sem):
    return pltpu.CompilerParams(dimension_semantics=sem, vmem_limit_bytes=V7X_VMEM_LIMIT_BYTES)


def _pick(dim, target):
    if dim <= target:
        return dim
    t = (target // LANES) * LANES
    while t >= LANES:
        if dim % t == 0:
            return t
        t -= LANES
    raise ValueError(f"no tile for {dim} under {target}")


MATMUL_VMEM_BUDGET = 32 * 1024 * 1024


def _matmul_tiles(m, n, k, a_bytes, b_bytes, o_bytes, has_res, ta):
    def fits(tm, tn):
        per_out = o_bytes + (4 if has_res else 0)
        return 2 * (tm * k * a_bytes + tn * k * b_bytes + tm * tn * per_out) <= MATMUL_VMEM_BUDGET

    tns = (2048, 1024, 512, 256, 128) if ta else (512, 256, 128)
    tms = (512, 256, 128) if ta else (2048, 1024, 512, 256, 128)
    for tn in tns:
        tn = _pick(n, tn)
        for tm in tms:
            tm = _pick(m, tm)
            if fits(tm, tn):
                return tm, tn
    raise ValueError(f"no matmul tiles for {m}x{n}x{k}")


def matmul(name, a, b, *, ta=False, tb=False, res=None, out_dtype=F32):
    m, k = (a.shape[1], a.shape[0]) if ta else a.shape
    n = b.shape[0] if tb else b.shape[1]
    assert k == (b.shape[1] if tb else b.shape[0]), (name, a.shape, b.shape)
    tm, tn = _matmul_tiles(m, n, k, a.dtype.itemsize, b.dtype.itemsize, jnp.dtype(out_dtype).itemsize, res is not None, ta)
    dims = (((0 if ta else 1,), (1 if tb else 0,)), ((), ()))

    def body(*refs):
        a_ref, b_ref, o_ref = refs[0], refs[1], refs[-1]
        total = lax.dot_general(a_ref[...].astype(BF16), b_ref[...].astype(BF16), dims, preferred_element_type=F32)
        if res is not None:
            total = total + refs[2][...]
        o_ref[...] = total.astype(o_ref.dtype)

    a_spec = pl.BlockSpec((k, tm), lambda i, j: (0, i)) if ta else pl.BlockSpec((tm, k), lambda i, j: (i, 0))
    b_spec = pl.BlockSpec((tn, k), lambda i, j: (j, 0)) if tb else pl.BlockSpec((k, tn), lambda i, j: (0, j))
    o_spec = pl.BlockSpec((tm, tn), lambda i, j: (i, j))
    ins, specs = [a, b], [a_spec, b_spec]
    if res is not None:
        ins.append(res)
        specs.append(o_spec)
    return pl.pallas_call(
        body, name=name, grid=(m // tm, n // tn), in_specs=specs, out_specs=o_spec,
        out_shape=jax.ShapeDtypeStruct((m, n), out_dtype),
        compiler_params=_params("parallel", "parallel"),
    )(*ins)


@functools.partial(jax.custom_vjp, nondiff_argnums=(1, 2))
def _roll(x, shift, axis):
    return pltpu.roll(x, shift, axis)


def _roll_fwd(x, shift, axis):
    return pltpu.roll(x, shift, axis), None


def _roll_bwd(shift, axis, _, g):
    return (pltpu.roll(g, (g.shape[axis] - shift) % g.shape[axis], axis),)


_roll.defvjp(_roll_fwd, _roll_bwd)


def _shift_down(x, s):
    rows = lax.broadcasted_iota(jnp.int32, x.shape, 0)
    return jnp.where(rows >= s, _roll(x, s, 0), 0.0)


def _dwconv(x, w, b):
    taps = w.shape[0]
    y = b + w[taps - 1:taps, :] * x
    for kk in range(taps - 1):
        y = y + w[kk:kk + 1, :] * _shift_down(x, taps - 1 - kk)
    return y


def _rms(x, w):
    return x * lax.rsqrt(jnp.mean(x * x, -1, keepdims=True) + EPS) * w


def _rms_f(h, w):
    return (_rms(h, w),)


def _gmlp_f(uv, lng, lnb, ws, bst):
    r = lax.broadcasted_iota(jnp.int32, (GM_BLOCK, GM_BLOCK), 0) // CHUNK
    c = lax.broadcasted_iota(jnp.int32, (GM_BLOCK, GM_BLOCK), 1) // CHUNK
    outs = []
    for g in range(GM_GROUPS):
        lo, hi = g * GM_GROUP_DIM, (g + 1) * GM_GROUP_DIM
        gu = jax.nn.gelu(uv[:, lo:hi])
        gv = jax.nn.gelu(uv[:, D_MODEL + lo:D_MODEL + hi])
        xc = gv - jnp.mean(gv, -1, keepdims=True)
        var = jnp.mean(xc * xc, -1, keepdims=True)
        vn = xc * lax.rsqrt(var + EPS) * lng[:, lo:hi] + lnb[:, lo:hi]
        wm = jnp.where(r >= c, ws[g], 0.0).astype(BF16)
        gate = jnp.dot(wm, vn.astype(BF16), preferred_element_type=F32) + bst[:, g:g + 1]
        outs.append(gu * gate)
    return (jnp.concatenate(outs, axis=1),)


def _conv_silu_f(x, w, b):
    return (jax.nn.silu(_dwconv(x, w, b)),)


def _ffn_act_f(g, val, w, b):
    return (jax.nn.gelu(_dwconv(g, w, b)) * val,)


def _gate_norm_f(y, z, nw):
    y2 = y * jax.nn.silu(z)
    width = D_MODEL // SSM_GROUPS
    outs = []
    for g in range(SSM_GROUPS):
        blk = y2[:, g * width:(g + 1) * width]
        outs.append(blk * lax.rsqrt(jnp.mean(blk * blk, -1, keepdims=True) + EPS))
    return (jnp.concatenate(outs, axis=1) * nw,)


def _rope(x, cos, sin):
    lane = lax.broadcasted_iota(jnp.int32, x.shape, 1)
    half = MLA_ROPE // 2
    swapped = jnp.where(lane < half, _roll(x, LANES - half, 1), _roll(x, half, 1))
    return x * cos + swapped * sin


def _qkv_norm_f(proj, cos, sin, qn, kvn):
    cq = _rms(proj[:, :MLA_RANK], qn)
    ckv = _rms(proj[:, MLA_RANK:2 * MLA_RANK], kvn)
    kpe = _rope(proj[:, 2 * MLA_RANK:], cos, sin)
    return cq, ckv, kpe


def _attn_scores(q0, k0, qh, kn, kpe, cos, sin):
    qn = qh[:, :MLA_NOPE]
    qp = _rope(qh[:, MLA_NOPE:], cos, sin)
    nt = (((1,), (1,)), ((), ()))
    s = lax.dot_general(qn.astype(BF16), kn.astype(BF16), nt, preferred_element_type=F32)
    s = s + lax.dot_general(qp.astype(BF16), kpe.astype(BF16), nt, preferred_element_type=F32)
    s = s * (MLA_QK ** -0.5)
    qc = (q0 + lax.broadcasted_iota(jnp.int32, s.shape, 0)) // CHUNK
    kc = (k0 + lax.broadcasted_iota(jnp.int32, s.shape, 1)) // CHUNK
    return jnp.where(kc <= qc, s, -jnp.inf)


def _ssd_chunk_f(x, bm, cm, pdt, hp, sprev):
    nh = SSD_HEADS_PER_STEP
    dt = jax.nn.softplus(pdt[:, :nh] + hp[0:1, :nh])
    da = dt * (-jnp.exp(hp[1:2, :nh]))
    r = lax.broadcasted_iota(jnp.int32, (CHUNK, CHUNK), 0)
    c = lax.broadcasted_iota(jnp.int32, (CHUNK, CHUNK), 1)
    tril = r >= c
    cs = jnp.dot(tril.astype(F32), da, precision=lax.Precision.HIGHEST, preferred_element_type=F32)
    cst = cs.T
    nt = (((1,), (1,)), ((), ()))
    tn = (((0,), (0,)), ((), ()))
    cb = lax.dot_general(cm.astype(BF16), bm.astype(BF16), nt, preferred_element_type=F32)
    ys, snew = [], []
    for e in range(nh):
        xe = x[:, e * SSM_HEAD_DIM:(e + 1) * SSM_HEAD_DIM]
        xd = xe * dt[:, e:e + 1]
        cse = cs[:, e:e + 1]
        decay = jnp.exp(jnp.where(tril, cse - cst[e:e + 1, :], -jnp.inf))
        y = jnp.dot((cb * decay).astype(BF16), xd.astype(BF16), preferred_element_type=F32)
        tot = cse[CHUNK - 1:CHUNK, :]
        st = lax.dot_general((xd * jnp.exp(tot - cse)).astype(BF16), bm.astype(BF16), tn, preferred_element_type=F32)
        yoff = lax.dot_general(cm.astype(BF16), sprev[e].astype(BF16), nt, preferred_element_type=F32)
        ys.append(y + yoff * jnp.exp(cse) + hp[2:3, e:e + 1] * xe)
        snew.append((jnp.exp(tot) * sprev[e] + st)[None])
    return jnp.concatenate(ys, axis=1), jnp.concatenate(snew, axis=0)


def _full_spec(a):
    nd = a.ndim
    return pl.BlockSpec(a.shape, lambda i, nd=nd: (0,) * nd)


def rows_fwd(name, f, rows, params, outs, tr):
    t = rows[0][0].shape[0]
    nr, npar = len(rows), len(params)

    def body(*refs):
        vals = f(*[x[...].astype(F32) for x in refs[:nr + npar]])
        for o_ref, val in zip(refs[nr + npar:], vals):
            o_ref[...] = val.astype(o_ref.dtype)

    in_specs = [pl.BlockSpec((tr, w), lambda i, cb=cb: (i, cb)) for _, w, cb in rows] + [_full_spec(p) for p in params]
    out = pl.pallas_call(
        body, name=name, grid=(t // tr,), in_specs=in_specs,
        out_specs=[pl.BlockSpec((tr, w), lambda i: (i, 0)) for w, _ in outs],
        out_shape=[jax.ShapeDtypeStruct((t, w), dt) for w, dt in outs],
        compiler_params=_params("parallel"),
    )(*[a for a, _, _ in rows], *params)
    return out


def rows_bwd(name, f, rows, params, cots, tr, d_dtypes, n_nondiff=0, add=None):
    t = rows[0][0].shape[0]
    nr, npar, nc = len(rows), len(params), len(cots)
    nd = nr - n_nondiff
    has_add = add is not None

    def body(*refs):
        i = pl.program_id(0)
        row_vals = [x[...].astype(F32) for x in refs[:nr]]
        par_vals = [x[...].astype(F32) for x in refs[nr:nr + npar]]
        cot_refs = refs[nr + npar:nr + npar + nc]
        pos = nr + npar + nc
        add_ref = refs[pos] if has_add else None
        pos += int(has_add)
        drow_refs = refs[pos:pos + nd]
        dpar_refs = refs[pos + nd:]

        def g(*diff):
            return f(*diff[:nd], *row_vals[nd:], *diff[nd:])

        _, vjp = jax.vjp(g, *row_vals[:nd], *par_vals)
        grads = vjp(tuple(cr[...].astype(F32) for cr in cot_refs))
        for j, d_ref in enumerate(drow_refs):
            val = grads[j]
            if j == 0 and has_add:
                val = val + add_ref[...]
            d_ref[...] = val.astype(d_ref.dtype)
        for j, d_ref in enumerate(dpar_refs):
            @pl.when(i == 0)
            def _(d_ref=d_ref, j=j):
                d_ref[...] = grads[nd + j]

            @pl.when(i > 0)
            def _(d_ref=d_ref, j=j):
                d_ref[...] += grads[nd + j]

    in_specs = [pl.BlockSpec((tr, w), lambda i, cb=cb: (i, cb)) for _, w, cb in rows] + [_full_spec(p) for p in params]
    in_specs += [pl.BlockSpec((tr, w), lambda i, cb=cb: (i, cb)) for _, w, cb in cots]
    ins = [a for a, _, _ in rows] + list(params) + [a for a, _, _ in cots]
    if has_add:
        in_specs.append(pl.BlockSpec((tr, rows[0][1]), lambda i: (i, 0)))
        ins.append(add)
    out_specs = [pl.BlockSpec((tr, rows[j][1]), lambda i: (i, 0)) for j in range(nd)] + [_full_spec(p) for p in params]
    out_shape = [jax.ShapeDtypeStruct((t, rows[j][1]), d_dtypes[j]) for j in range(nd)]
    out_shape += [jax.ShapeDtypeStruct(p.shape, F32) for p in params]
    out = pl.pallas_call(
        body, name=name, grid=(t // tr,), in_specs=in_specs, out_specs=out_specs, out_shape=out_shape,
        compiler_params=_params("arbitrary"),
    )(*ins)
    return out[:nd], out[nd:]


def cols_fwd(name, f, cols, cparams, out_dtype, tc):
    t = cols[0][0].shape[0]
    width = cparams[0].shape[1]
    ncol = len(cols)

    def body(*refs):
        (val,) = f(*[x[...].astype(F32) for x in refs[:-1]])
        refs[-1][...] = val.astype(refs[-1].dtype)

    in_specs = [pl.BlockSpec((t, tc), lambda j, o=o: (0, o + j)) for _, o in cols]
    in_specs += [pl.BlockSpec((p.shape[0], tc), lambda j: (0, j)) for p in cparams]
    return pl.pallas_call(
        body, name=name, grid=(width // tc,), in_specs=in_specs,
        out_specs=pl.BlockSpec((t, tc), lambda j: (0, j)),
        out_shape=jax.ShapeDtypeStruct((t, width), out_dtype),
        compiler_params=_params("parallel"),
    )(*[a for a, _ in cols], *cparams)


def cols_bwd(name, f, cols, cparams, cot, tc, d_dtype):
    t = cols[0][0].shape[0]
    width = cparams[0].shape[1]
    ncol, npar = len(cols), len(cparams)

    def body(*refs):
        vals = [x[...].astype(F32) for x in refs[:ncol + npar]]
        _, vjp = jax.vjp(f, *vals)
        grads = vjp((refs[ncol + npar][...].astype(F32),))
        for d_ref, gval in zip(refs[ncol + npar + 1:], grads):
            d_ref[...] = gval.astype(d_ref.dtype)

    in_specs = [pl.BlockSpec((t, tc), lambda j, o=o: (0, o + j)) for _, o in cols]
    in_specs += [pl.BlockSpec((p.shape[0], tc), lambda j: (0, j)) for p in cparams]
    in_specs.append(pl.BlockSpec((t, tc), lambda j: (0, j)))
    out_specs = [pl.BlockSpec((t, tc), lambda j: (0, j)) for _ in cols]
    out_specs += [pl.BlockSpec((p.shape[0], tc), lambda j: (0, j)) for p in cparams]
    out_shape = [jax.ShapeDtypeStruct((t, width), d_dtype) for _ in cols]
    out_shape += [jax.ShapeDtypeStruct(p.shape, F32) for p in cparams]
    out = pl.pallas_call(
        body, name=name, grid=(width // tc,), in_specs=in_specs, out_specs=out_specs, out_shape=out_shape,
        compiler_params=_params("parallel"),
    )(*[a for a, _ in cols], *cparams, cot)
    return out[:ncol], out[ncol:]


def _ssd_in_specs(t):
    heads_per_group = SSM_HEADS // SSM_GROUPS
    steps_per_group = heads_per_group // SSD_HEADS_PER_STEP
    b_blk = D_MODEL // LANES
    c_blk = (D_MODEL + SSM_BC) // LANES
    return [
        pl.BlockSpec((t, SSD_X_WIDTH), lambda s: (0, s)),
        pl.BlockSpec((t, LANES), lambda s: (0, b_blk + s // steps_per_group)),
        pl.BlockSpec((t, LANES), lambda s: (0, c_blk + s // steps_per_group)),
        pl.BlockSpec((t, LANES), lambda s: (0, s)),
        pl.BlockSpec((3, LANES), lambda s: (0, s)),
    ]


def ssd_fwd(name, xa, pdt, hp):
    t = xa.shape[0]
    nc = t // CHUNK
    nh = SSD_HEADS_PER_STEP

    def body(x_ref, b_ref, c_ref, pdt_ref, hp_ref, y_ref, st_ref, s_scr):
        s_scr[...] = jnp.zeros_like(s_scr)

        def step(ci, carry):
            sl = pl.ds(pl.multiple_of(ci * CHUNK, CHUNK), CHUNK)
            sprev = s_scr[...]
            st_ref[0, ci] = sprev
            y, snew = _ssd_chunk_f(x_ref[sl, :], b_ref[sl, :], c_ref[sl, :], pdt_ref[sl, :], hp_ref[...], sprev)
            y_ref[sl, :] = y
            s_scr[...] = snew
            return carry

        lax.fori_loop(0, nc, step, 0)

    return pl.pallas_call(
        body, name=name, grid=(SSD_STEPS,), in_specs=_ssd_in_specs(t),
        out_specs=[pl.BlockSpec((t, SSD_X_WIDTH), lambda s: (0, s)),
                   pl.BlockSpec((1, nc, nh, SSM_HEAD_DIM, SSM_STATE), lambda s: (s, 0, 0, 0, 0))],
        out_shape=[jax.ShapeDtypeStruct((t, D_MODEL), F32),
                   jax.ShapeDtypeStruct((SSD_STEPS, nc, nh, SSM_HEAD_DIM, SSM_STATE), F32)],
        scratch_shapes=[pltpu.VMEM((nh, SSM_HEAD_DIM, SSM_STATE), F32)],
        compiler_params=_params("parallel"),
    )(xa, xa, xa, pdt, hp)


def ssd_bwd(name, xa, pdt, hp, states, dy):
    t = xa.shape[0]
    nc = t // CHUNK
    nh = SSD_HEADS_PER_STEP
    steps_per_group = SSM_HEADS // SSM_GROUPS // nh

    def body(x_ref, b_ref, c_ref, pdt_ref, hp_ref, st_ref, dy_ref, dx_ref, db_ref, dc_ref, dpdt_ref, dhp_ref, ds_scr, dhp_scr):
        first = pl.program_id(0) % steps_per_group == 0
        ds_scr[...] = jnp.zeros_like(ds_scr)
        dhp_scr[...] = jnp.zeros_like(dhp_scr)

        def step(i, carry):
            ci = nc - 1 - i
            sl = pl.ds(pl.multiple_of(ci * CHUNK, CHUNK), CHUNK)
            _, vjp = jax.vjp(_ssd_chunk_f, x_ref[sl, :], b_ref[sl, :], c_ref[sl, :], pdt_ref[sl, :], hp_ref[...], st_ref[0, ci])
            dx, db, dc, dpdt, dhp, dsprev = vjp((dy_ref[sl, :], ds_scr[...]))
            dx_ref[sl, :] = dx
            dpdt_ref[sl, :] = dpdt.astype(dpdt_ref.dtype)

            @pl.when(first)
            def _():
                db_ref[sl, :] = db
                dc_ref[sl, :] = dc

            @pl.when(jnp.logical_not(first))
            def _():
                db_ref[sl, :] += db
                dc_ref[sl, :] += dc

            ds_scr[...] = dsprev
            dhp_scr[...] += dhp
            return carry

        lax.fori_loop(0, nc, step, 0)
        dhp_ref[...] = dhp_scr[...]

    in_specs = _ssd_in_specs(t) + [
        pl.BlockSpec((1, nc, nh, SSM_HEAD_DIM, SSM_STATE), lambda s: (s, 0, 0, 0, 0)),
        pl.BlockSpec((t, SSD_X_WIDTH), lambda s: (0, s)),
    ]
    out_specs = [
        pl.BlockSpec((t, SSD_X_WIDTH), lambda s: (0, s)),
        pl.BlockSpec((t, LANES), lambda s: (0, s // steps_per_group)),
        pl.BlockSpec((t, LANES), lambda s: (0, s // steps_per_group)),
        pl.BlockSpec((t, LANES), lambda s: (0, s)),
        pl.BlockSpec((3, LANES), lambda s: (0, s)),
    ]
    out_shape = [
        jax.ShapeDtypeStruct((t, D_MODEL), F32),
        jax.ShapeDtypeStruct((t, SSM_BC), F32),
        jax.ShapeDtypeStruct((t, SSM_BC), F32),
        jax.ShapeDtypeStruct((t, SSD_STEPS * LANES), BF16),
        jax.ShapeDtypeStruct((3, SSD_STEPS * LANES), F32),
    ]
    return pl.pallas_call(
        body, name=name, grid=(SSD_STEPS,), in_specs=in_specs, out_specs=out_specs, out_shape=out_shape,
        scratch_shapes=[pltpu.VMEM((nh, SSM_HEAD_DIM, SSM_STATE), F32), pltpu.VMEM((3, LANES), F32)],
        compiler_params=_params("arbitrary"),
    )(xa, xa, xa, pdt, hp, states, dy)


ATTN_TQ = 256


def _attn_in_specs(t):
    return [
        pl.BlockSpec((ATTN_TQ, MLA_QPAD), lambda h, qi: (qi, h)),
        pl.BlockSpec((t, MLA_NOPE), lambda h, qi: (0, h)),
        pl.BlockSpec((t, LANES), lambda h, qi: (0, 0)),
        pl.BlockSpec((t, MLA_V), lambda h, qi: (0, h)),
        pl.BlockSpec((ATTN_TQ, LANES), lambda h, qi: (qi, 0)),
        pl.BlockSpec((ATTN_TQ, LANES), lambda h, qi: (qi, 0)),
    ]


def attn_fwd(name, q, kn, kpe, v, cos, sin):
    t = q.shape[0]

    def body(q_ref, kn_ref, kpe_ref, v_ref, cos_ref, sin_ref, o_ref, of_ref, lse_ref):
        qi = pl.program_id(1)
        q0 = qi * ATTN_TQ
        qh, cos, sin = q_ref[...], cos_ref[...], sin_ref[...]

        def step(kb, carry):
            m, l, acc = carry
            ks = pl.ds(pl.multiple_of(kb * ATTN_TQ, ATTN_TQ), ATTN_TQ)
            s = _attn_scores(q0, kb * ATTN_TQ, qh, kn_ref[ks, :], kpe_ref[ks, :], cos, sin)
            m_new = jnp.maximum(m, jnp.max(s, -1, keepdims=True))
            p = jnp.exp(s - m_new)
            corr = jnp.exp(m - m_new)
            l = l * corr + jnp.sum(p, -1, keepdims=True)
            acc = acc * corr + jnp.dot(p.astype(BF16), v_ref[ks, :].astype(BF16), preferred_element_type=F32)
            return m_new, l, acc

        init = (jnp.full((ATTN_TQ, 1), -jnp.inf, F32), jnp.zeros((ATTN_TQ, 1), F32), jnp.zeros((ATTN_TQ, MLA_V), F32))
        m, l, acc = lax.fori_loop(0, qi + 1, step, init)
        o = acc / l
        o_ref[...] = o.astype(o_ref.dtype)
        of_ref[...] = o
        lse_ref[0] = m + jnp.log(l)

    o_spec = pl.BlockSpec((ATTN_TQ, MLA_V), lambda h, qi: (qi, h))
    return pl.pallas_call(
        body, name=name, grid=(MLA_HEADS, t // ATTN_TQ), in_specs=_attn_in_specs(t),
        out_specs=[o_spec, o_spec, pl.BlockSpec((1, ATTN_TQ, 1), lambda h, qi: (h, qi, 0))],
        out_shape=[jax.ShapeDtypeStruct((t, MLA_HEADS * MLA_V), BF16), jax.ShapeDtypeStruct((t, MLA_HEADS * MLA_V), F32),
                   jax.ShapeDtypeStruct((MLA_HEADS, t, 1), F32)],
        compiler_params=_params("parallel", "parallel"),
    )(q, kn, kpe, v, cos, sin)


def attn_bwd(name, q, kn, kpe, v, cos, sin, o, lse, do):
    t = q.shape[0]

    def body(q_ref, kn_ref, kpe_ref, v_ref, cos_ref, sin_ref, o_ref, lse_ref, do_ref, dq_ref, dkn_ref, dkpe_ref, dv_ref):
        h, qi = pl.program_id(0), pl.program_id(1)
        q0 = qi * ATTN_TQ
        qh, cos, sin = q_ref[...], cos_ref[...], sin_ref[...]
        do, lse = do_ref[...], lse_ref[0]
        delta = jnp.sum(do * o_ref[...], -1, keepdims=True)

        @pl.when(qi == 0)
        def _():
            dkn_ref[...] = jnp.zeros_like(dkn_ref)
            dv_ref[...] = jnp.zeros_like(dv_ref)

        @pl.when(jnp.logical_and(h == 0, qi == 0))
        def _():
            dkpe_ref[...] = jnp.zeros_like(dkpe_ref)

        def step(kb, dq):
            ks = pl.ds(pl.multiple_of(kb * ATTN_TQ, ATTN_TQ), ATTN_TQ)

            def objective(qv, knv, kpev, vv):
                p = jnp.exp(_attn_scores(q0, kb * ATTN_TQ, qv, knv, kpev, cos, sin) - lse)
                pv = jnp.dot(p.astype(BF16), vv.astype(BF16), preferred_element_type=F32)
                return jnp.sum(do * pv) - jnp.sum(p * delta)

            g = jax.grad(objective, argnums=(0, 1, 2, 3))(qh, kn_ref[ks, :].astype(F32), kpe_ref[ks, :], v_ref[ks, :].astype(F32))
            dkn_ref[ks, :] += g[1]
            dkpe_ref[ks, :] += g[2]
            dv_ref[ks, :] += g[3]
            return dq + g[0]

        dq = lax.fori_loop(0, qi + 1, step, jnp.zeros((ATTN_TQ, MLA_QPAD), F32))
        dq_ref[...] = dq.astype(dq_ref.dtype)

    o_spec = pl.BlockSpec((ATTN_TQ, MLA_V), lambda h, qi: (qi, h))
    in_specs = _attn_in_specs(t) + [o_spec, pl.BlockSpec((1, ATTN_TQ, 1), lambda h, qi: (h, qi, 0)), o_spec]
    out_specs = [
        pl.BlockSpec((ATTN_TQ, MLA_QPAD), lambda h, qi: (qi, h)),
        pl.BlockSpec((t, MLA_NOPE), lambda h, qi: (0, h)),
        pl.BlockSpec((t, LANES), lambda h, qi: (0, 0)),
        pl.BlockSpec((t, MLA_V), lambda h, qi: (0, h)),
    ]
    out_shape = [
        jax.ShapeDtypeStruct((t, MLA_HEADS * MLA_QPAD), BF16),
        jax.ShapeDtypeStruct((t, MLA_HEADS * MLA_NOPE), F32),
        jax.ShapeDtypeStruct((t, LANES), F32),
        jax.ShapeDtypeStruct((t, MLA_HEADS * MLA_V), F32),
    ]
    return pl.pallas_call(
        body, name=name, grid=(MLA_HEADS, t // ATTN_TQ), in_specs=in_specs, out_specs=out_specs, out_shape=out_shape,
        compiler_params=_params("arbitrary", "arbitrary"),
    )(q, kn, kpe, v, cos, sin, o, lse, do)


def final_loss(name, h, nf, target, tr=256):
    t, d = h.shape

    def body(h_ref, w_ref, t_ref, loss_ref, dh_ref, dw_ref):
        i = pl.program_id(0)
        tgt = t_ref[...]

        def f(hv, wv):
            err = _rms(hv, wv) - tgt
            return 0.5 * jnp.sum(jnp.mean(err * err, -1, keepdims=True), 0, keepdims=True)

        val, vjp = jax.vjp(f, h_ref[...], w_ref[...])
        dh, dw = vjp(jnp.ones((1, 1), F32))
        dh_ref[...] = dh
        tile = jnp.broadcast_to(val, loss_ref.shape)

        @pl.when(i == 0)
        def _():
            loss_ref[...] = tile
            dw_ref[...] = dw

        @pl.when(i > 0)
        def _():
            loss_ref[...] += tile
            dw_ref[...] += dw

    row = pl.BlockSpec((tr, d), lambda i: (i, 0))
    return pl.pallas_call(
        body, name=name, grid=(t // tr,), in_specs=[row, _full_spec(nf), row],
        out_specs=[pl.BlockSpec((8, LANES), lambda i: (0, 0)), row, _full_spec(nf)],
        out_shape=[jax.ShapeDtypeStruct((8, LANES), F32), jax.ShapeDtypeStruct((t, d), F32), jax.ShapeDtypeStruct(nf.shape, F32)],
        compiler_params=_params("arbitrary"),
    )(h, nf, target)


ANY = pl.BlockSpec(memory_space=pl.ANY)
CHIP_ORDER = ((0, 0), (0, 1), (1, 0), (1, 1))


def _place():
    return lax.axis_index("x"), lax.axis_index("y"), lax.axis_index("c")


def _other_chips(x, y):
    return [(1 - x, y), (x, 1 - y), (1 - x, 1 - y)]


def _device_slot():
    x, y, c = _place()
    return 4 * x + 2 * y + c


def _row_tile(rows):
    return next(t for t in (128, 64, 32, 16) if rows % t == 0)


def all_gather(name, shards):
    n = len(shards)

    def body(*refs):
        ins, outs = refs[:n], refs[n:2 * n]
        send_sems, recv_sems, local_sems = refs[2 * n:]
        x, y, c = _place()
        me, sibling = (x, y, c), (x, y, 1 - c)
        chips = _other_chips(x, y)

        def copy(k, j, block, to, from_input=False):
            dst = outs[k].at[4 * block[0] + 2 * block[1] + block[2]]
            return pltpu.make_async_remote_copy(
                src_ref=ins[k] if from_input else dst, dst_ref=dst,
                send_sem=send_sems.at[7 * k + j], recv_sem=recv_sems.at[7 * k + j],
                device_id=to, device_id_type=MESH)

        mine = [pltpu.make_async_copy(ins[k], outs[k].at[4 * x + 2 * y + c], local_sems.at[k]) for k in range(n)]
        for cp in mine:
            cp.start()
        first = []
        for k in range(n):
            first.append(copy(k, 0, me, sibling, True))
            first += [copy(k, 1 + j, me, (*chip, c), True) for j, chip in enumerate(chips)]
        for cp in first:
            cp.start()
        passed = []
        for j, chip in enumerate(chips):
            for k in range(n):
                copy(k, 1 + j, (*chip, c), me).wait_recv()
                fwd = copy(k, 4 + j, (*chip, c), sibling)
                fwd.start()
                passed.append(fwd)
        for k in range(n):
            copy(k, 0, sibling, me).wait_recv()
        for j, chip in enumerate(chips):
            for k in range(n):
                copy(k, 4 + j, (*chip, 1 - c), me).wait_recv()
        for cp in first + passed:
            cp.wait_send()
        for cp in mine:
            cp.wait()

    return pl.pallas_call(
        body, name=name, in_specs=[ANY] * n, out_specs=[ANY] * n,
        out_shape=[jax.ShapeDtypeStruct((N_DEV,) + s.shape, s.dtype) for s in shards],
        scratch_shapes=[pltpu.SemaphoreType.DMA((7 * n,)), pltpu.SemaphoreType.DMA((7 * n,)), pltpu.SemaphoreType.DMA((n,))],
    )(*shards)


def exchange_sibling(name, gs):
    n = len(gs)

    def body(*refs):
        ins, outs = refs[:n], refs[n:2 * n]
        send_sems, recv_sems = refs[2 * n:]
        x, y, c = _place()
        copies = []
        for k in range(n):
            for q, (cx, cy) in enumerate(CHIP_ORDER):
                copies.append(pltpu.make_async_remote_copy(
                    src_ref=ins[k].at[4 * cx + 2 * cy + (1 - c)], dst_ref=outs[k].at[q],
                    send_sem=send_sems.at[4 * k + q], recv_sem=recv_sems.at[4 * k + q],
                    device_id=(x, y, 1 - c), device_id_type=MESH))
        for cp in copies:
            cp.start()
        for cp in copies:
            cp.wait()

    return pl.pallas_call(
        body, name=name, in_specs=[ANY] * n, out_specs=[ANY] * n,
        out_shape=[jax.ShapeDtypeStruct((4,) + g.shape[1:], g.dtype) for g in gs],
        scratch_shapes=[pltpu.SemaphoreType.DMA((4 * n,)), pltpu.SemaphoreType.DMA((4 * n,))],
    )(*gs)


def chip_sums(name, g, recv, tr=128):
    _, r, c = g.shape

    def body(g_ref, r_ref, o_ref):
        o_ref[...] = (g_ref[...].astype(F32) + r_ref[...].astype(F32)).astype(o_ref.dtype)

    def chip(i):
        x, y, _ = _place()
        return jnp.where(i % 2 == 1, 1 - x, x), jnp.where(i >= 2, 1 - y, y)

    def g_index(i, j):
        cx, cy = chip(i)
        return 4 * cx + 2 * cy + lax.axis_index("c"), j, 0

    def recv_index(i, j):
        cx, cy = chip(i)
        return 2 * cx + cy, j, 0

    return pl.pallas_call(
        body, name=name, grid=(4, r // tr),
        in_specs=[pl.BlockSpec((1, tr, c), g_index), pl.BlockSpec((1, tr, c), recv_index)],
        out_specs=pl.BlockSpec((1, tr, c), lambda i, j: (i, j, 0)),
        out_shape=jax.ShapeDtypeStruct((4, r, c), g.dtype),
        compiler_params=_params("parallel", "parallel"),
    )(g, recv)


def sum_parts(name, parts, tr=128):
    _, r, c = parts[0][0].shape

    def body(*refs):
        total = refs[0][0].astype(F32)
        for ref in refs[1:-1]:
            total = total + ref[0].astype(F32)
        refs[-1][...] = total

    return pl.pallas_call(
        body, name=name, grid=(r // tr,),
        in_specs=[pl.BlockSpec((1, tr, c), lambda i, s=s: (s, i, 0)) for _, s in parts],
        out_specs=pl.BlockSpec((tr, c), lambda i: (i, 0)), out_shape=jax.ShapeDtypeStruct((r, c), F32),
        compiler_params=_params("parallel"),
    )(*[a for a, _ in parts])


def adamw(name, parts, w, m, v, tr=128, part=0, prev=None):
    _, r, c = parts[0][0].shape
    np_ = len(parts)
    first = part * (r // tr)

    def body(*refs):
        g = refs[0][0].astype(F32)
        for ref in refs[1:np_]:
            g = g + ref[0].astype(F32)
        w_ref, m_ref, v_ref = refs[np_:np_ + 3]
        g_out, d_out, m_out, v_out = refs[-4:]
        new_m = ADAM_B1 * m_ref[...] + (1.0 - ADAM_B1) * g
        new_v = ADAM_B2 * v_ref[...] + (1.0 - ADAM_B2) * (g * g)
        m_hat = new_m / (1.0 - ADAM_B1 ** ADAM_STEP)
        v_hat = new_v / (1.0 - ADAM_B2 ** ADAM_STEP)
        g_out[...] = g
        d_out[...] = -ADAM_LR * (m_hat / (jnp.sqrt(v_hat) + ADAM_EPS) + ADAM_WD * w_ref[...])
        m_out[...] = new_m
        v_out[...] = new_v

    tile = pl.BlockSpec((tr, c), lambda i: (first + i, 0))
    in_specs = [pl.BlockSpec((1, tr, c), lambda i, s=s: (s, i, 0)) for _, s in parts] + [tile] * 3
    ins = [a for a, _ in parts] + [w, m, v]
    aliases = {}
    if prev is not None:
        aliases = {len(ins) + k: k for k in range(4)}
        in_specs += [ANY] * 4
        ins += list(prev)
    return pl.pallas_call(
        body, name=name, grid=(r // tr,), in_specs=in_specs,
        out_specs=[tile] * 4, out_shape=[jax.ShapeDtypeStruct(w.shape, F32)] * 4,
        input_output_aliases=aliases, compiler_params=_params("parallel"),
    )(*ins)


HBM = pl.BlockSpec(memory_space=pltpu.HBM)
SEM = pl.BlockSpec(memory_space=pltpu.SEMAPHORE)
SIDE_EFFECT = pltpu.SideEffectType.DATAFLOW_SIDE_EFFECTING


def _split_copies(plan, src_refs, land_refs, send_sems, recv_sems):
    copies = []
    for i, (k, src_slot, land_slot, device) in enumerate(plan(*_place())):
        copies.append(pltpu.make_async_remote_copy(
            src_ref=src_refs[k] if src_slot is None else src_refs[k].at[src_slot], dst_ref=land_refs[k].at[land_slot],
            send_sem=send_sems.at[i], recv_sem=recv_sems.at[i], device_id=device, device_id_type=MESH))
    return copies


def split_start(name, srcs, land_shapes, plan, n_copies):
    n = len(srcs)

    def body(*refs):
        src_refs, land_refs = refs[:n], refs[n:2 * n]
        send_sems, recv_sems, token = refs[2 * n], refs[2 * n + 1], refs[-1]
        for cp in _split_copies(plan, src_refs, land_refs, send_sems, recv_sems):
            cp.start()
        token[...] = jnp.zeros_like(token)

    lands = [lax.empty(shape, s.dtype) for shape, s in zip(land_shapes, srcs)]
    ins = [pltpu.with_memory_space_constraint(a, pltpu.HBM) for a in list(srcs) + lands]
    out = pl.pallas_call(
        body, name=name,
        out_shape=(pltpu.SemaphoreType.DMA((n_copies,)), pltpu.SemaphoreType.DMA((n_copies,)),
                   *[pltpu.HBM(a.shape, a.dtype) for a in ins], jax.ShapeDtypeStruct((8, LANES), F32)),
        in_specs=[HBM] * (2 * n), out_specs=(SEM, SEM, *[HBM] * (2 * n), pl.BlockSpec(memory_space=pltpu.VMEM)),
        input_output_aliases={i: 2 + i for i in range(2 * n)},
        compiler_params=pltpu.CompilerParams(has_side_effects=SIDE_EFFECT),
    )(*ins)
    return out[0], out[1], list(out[2:2 + n]), list(out[2 + n:2 + 2 * n]), out[-1]


def split_wait(name, handle, plan, after):
    send_sems, recv_sems, srcs, lands, _ = handle
    n = len(srcs)

    def body(*refs):
        src_refs, land_refs = refs[:n], refs[n:2 * n]
        for cp in _split_copies(plan, src_refs, land_refs, refs[2 * n], refs[2 * n + 1]):
            cp.wait_send()
            cp.wait_recv()

    out = pl.pallas_call(
        body, name=name, out_shape=tuple(pltpu.HBM(a.shape, a.dtype) for a in srcs + lands),
        in_specs=[HBM] * (2 * n) + [SEM, SEM, ANY], out_specs=tuple([HBM] * (2 * n)),
        input_output_aliases={i: i for i in range(2 * n)},
        compiler_params=pltpu.CompilerParams(has_side_effects=SIDE_EFFECT),
    )(*srcs, *lands, send_sems, recv_sems, after)
    return list(out[:n]), list(out[n:])


def gather_plan(n):
    def plan(x, y, c):
        me = 4 * x + 2 * y + c
        peers = [(x, y, 1 - c)] + [(*chip, c) for chip in _other_chips(x, y)]
        return [(k, None, me, peer) for k in range(n) for peer in peers]
    return plan


def chips_plan(n):
    def plan(x, y, c):
        return [(k, 1 + j, j, (*chip, c)) for k in range(n) for j, chip in enumerate(_other_chips(x, y))]
    return plan


def gather_finish(name, gathered):
    n = len(gathered)

    def body(*refs):
        outs = refs[n:2 * n]
        send_sems, recv_sems = refs[2 * n:]
        x, y, c = _place()

        def passed_on(k, j, core):
            cx, cy = _other_chips(x, y)[j]
            blk = outs[k].at[4 * cx + 2 * cy + core]
            return pltpu.make_async_remote_copy(
                src_ref=blk, dst_ref=blk, send_sem=send_sems.at[3 * k + j], recv_sem=recv_sems.at[3 * k + j],
                device_id=(x, y, 1 - c), device_id_type=MESH)

        pairs = [(k, j) for k in range(n) for j in range(3)]
        sends = [passed_on(k, j, c) for k, j in pairs]
        for cp in sends:
            cp.start()
        for k, j in pairs:
            passed_on(k, j, 1 - c).wait_recv()
        for cp in sends:
            cp.wait_send()

    return pl.pallas_call(
        body, name=name, in_specs=[ANY] * n, out_specs=[ANY] * n,
        out_shape=[jax.ShapeDtypeStruct(g.shape, g.dtype) for g in gathered],
        input_output_aliases={k: k for k in range(n)},
        scratch_shapes=[pltpu.SemaphoreType.DMA((3 * n,)), pltpu.SemaphoreType.DMA((3 * n,))],
    )(*gathered)


ROW_TILE = 256
COL_TILE = 256


def _rms_fwd(tag, h, w):
    return rows_fwd(tag, _rms_f, [(h, D_MODEL, 0)], [w], [(D_MODEL, BF16)], ROW_TILE)[0]


def _rms_bwd(tag, h, w, dhn, dres):
    (dh,), (dw,) = rows_bwd(tag, _rms_f, [(h, D_MODEL, 0)], [w], [(dhn, D_MODEL, 0)], ROW_TILE, [F32], add=dres)
    return dh, dw


def even_fwd(tag, h, w, p):
    hn = _rms_fwd(tag + "_rms", h, p["nm"])
    uv = matmul(tag + "_uv", hn, w["uv"])
    z = matmul(tag + "_z", hn, w["z"])
    xbc = matmul(tag + "_xbc", hn, w["xbc"])
    pdt = matmul(tag + "_dt", hn, w["dt"])
    gm = [p["lng"], p["lnb"], p["ws"], p["bst"]]
    ya = rows_fwd(tag + "_gmlp", _gmlp_f, [(uv, 2 * D_MODEL, 0)], gm, [(D_MODEL, BF16)], GM_BLOCK)[0]
    xa = cols_fwd(tag + "_conv", _conv_silu_f, [(xbc, 0)], [p["cw"], p["cb"]], F32, COL_TILE)
    y, states = ssd_fwd(tag + "_ssd", xa, pdt, p["hp"])
    yb = rows_fwd(tag + "_gate", _gate_norm_f, [(y, D_MODEL, 0), (z, D_MODEL, 0)], [p["nw"]], [(D_MODEL, BF16)], ROW_TILE)[0]
    h1 = matmul(tag + "_out_b", yb, w["out_bot"], res=matmul(tag + "_out_a", ya, w["out_top"], res=h))
    return h1, dict(h=h, hn=hn, uv=uv, z=z, xbc=xbc, pdt=pdt, xa=xa, y=y, states=states, ya=ya, yb=yb)


def even_bwd(tag, dh1, s, w, p):
    dya = matmul(tag + "_dya", dh1, w["out_top"], tb=True)
    dyb = matmul(tag + "_dyb", dh1, w["out_bot"], tb=True)
    gw = dict(out_top=matmul(tag + "_gwa", s["ya"], dh1, ta=True, out_dtype=BF16),
              out_bot=matmul(tag + "_gwb", s["yb"], dh1, ta=True, out_dtype=BF16))
    (dy, dz), (dnw,) = rows_bwd(tag + "_gate_b", _gate_norm_f, [(s["y"], D_MODEL, 0), (s["z"], D_MODEL, 0)], [p["nw"]],
                                [(dyb, D_MODEL, 0)], ROW_TILE, [F32, BF16])
    dxs, dbm, dcm, dpdt, dhp = ssd_bwd(tag + "_ssd_b", s["xa"], s["pdt"], p["hp"], s["states"], dy)
    dxa = jnp.concatenate([dxs, dbm, dcm], axis=1)
    (dxbc,), (dcw, dcb) = cols_bwd(tag + "_conv_b", _conv_silu_f, [(s["xbc"], 0)], [p["cw"], p["cb"]], dxa, COL_TILE, BF16)
    gm = [p["lng"], p["lnb"], p["ws"], p["bst"]]
    (duv,), (dlng, dlnb, dws, dbst) = rows_bwd(tag + "_gmlp_b", _gmlp_f, [(s["uv"], 2 * D_MODEL, 0)], gm,
                                               [(dya, D_MODEL, 0)], GM_BLOCK, [BF16])
    dhn = None
    for key, d in (("uv", duv), ("z", dz), ("xbc", dxbc), ("dt", dpdt)):
        dhn = matmul(f"{tag}_dx_{key}", d, w[key], tb=True, res=dhn)
        gw[key] = matmul(f"{tag}_gw_{key}", s["hn"], d, ta=True, out_dtype=BF16)
    dh, dnm = _rms_bwd(tag + "_rms_b", s["h"], p["nm"], dhn, dh1)
    gp = dict(nm=dnm, lng=dlng, lnb=dlnb, ws=dws, bst=dbst, cw=dcw, cb=dcb, hp=dhp, nw=dnw)
    return dh, gw, gp


def odd_fwd(tag, h, w, p, cos, sin):
    hn = _rms_fwd(tag + "_rms", h, p["nm"])
    proj = matmul(tag + "_in", hn, w["in"])
    cq, ckv, kpe = rows_fwd(tag + "_qkvn", _qkv_norm_f, [(proj, ODD_IN_PAD, 0), (cos, LANES, 0), (sin, LANES, 0)],
                            [p["qn"], p["kvn"]], [(MLA_RANK, BF16), (MLA_RANK, BF16), (LANES, F32)], ROW_TILE)
    q = matmul(tag + "_q", cq, w["uq"])
    kn = matmul(tag + "_kn", ckv, w["kn"], out_dtype=BF16)
    v = matmul(tag + "_v", ckv, w["v"], out_dtype=BF16)
    o, o_f32, lse = attn_fwd(tag + "_attn", q, kn, kpe, v, cos, sin)
    h1 = matmul(tag + "_o", o, w["o"], res=h)
    return h1, dict(h=h, hn=hn, proj=proj, cq=cq, ckv=ckv, kpe=kpe, q=q, kn=kn, v=v, o=o, o_f32=o_f32, lse=lse)


def odd_bwd(tag, dh1, s, w, p, cos, sin):
    do = matmul(tag + "_do", dh1, w["o"], tb=True)
    gw = dict(o=matmul(tag + "_gw_o", s["o"], dh1, ta=True, out_dtype=BF16))
    dq, dkn, dkpe, dv = attn_bwd(tag + "_attn_b", s["q"], s["kn"], s["kpe"], s["v"], cos, sin, s["o_f32"], s["lse"], do)
    dcq = matmul(tag + "_dcq", dq, w["uq"], tb=True)
    gw["uq"] = matmul(tag + "_gw_uq", s["cq"], dq, ta=True, out_dtype=BF16)
    dckv = matmul(tag + "_dckv_v", dv, w["v"], tb=True, res=matmul(tag + "_dckv_k", dkn, w["kn"], tb=True))
    gw["kn"] = matmul(tag + "_gw_kn", s["ckv"], dkn, ta=True, out_dtype=BF16)
    gw["v"] = matmul(tag + "_gw_v", s["ckv"], dv, ta=True, out_dtype=BF16)
    (dproj,), (dqn, dkvn) = rows_bwd(
        tag + "_qkvn_b", _qkv_norm_f, [(s["proj"], ODD_IN_PAD, 0), (cos, LANES, 0), (sin, LANES, 0)], [p["qn"], p["kvn"]],
        [(dcq, MLA_RANK, 0), (dckv, MLA_RANK, 0), (dkpe, LANES, 0)], ROW_TILE, [BF16], n_nondiff=2)
    dhn = matmul(tag + "_dx_in", dproj, w["in"], tb=True)
    gw["in"] = matmul(tag + "_gw_in", s["hn"], dproj, ta=True, out_dtype=BF16)
    dh, dnm = _rms_bwd(tag + "_rms_b", s["h"], p["nm"], dhn, dh1)
    return dh, gw, dict(nm=dnm, qn=dqn, kvn=dkvn)


def ffn_fwd(tag, h, w, p):
    hn = _rms_fwd(tag + "_rms", h, p["nf"])
    g = matmul(tag + "_up_g", hn, w["up_g"])
    val = matmul(tag + "_up_v", hn, w["up_v"])
    act = cols_fwd(tag + "_act", _ffn_act_f, [(g, 0), (val, 0)], [p["fcw"], p["fcb"]], BF16, COL_TILE)
    h2 = matmul(tag + "_down", act, w["down"], res=h)
    return h2, dict(h=h, hn=hn, g=g, val=val, act=act)


def ffn_bwd(tag, dh2, s, w, p):
    dact = matmul(tag + "_dact", dh2, w["down"], tb=True)
    gw = dict(down=matmul(tag + "_gw_down", s["act"], dh2, ta=True, out_dtype=BF16))
    (dg, dval), (dfcw, dfcb) = cols_bwd(tag + "_act_b", _ffn_act_f, [(s["g"], 0), (s["val"], 0)], [p["fcw"], p["fcb"]],
                                        dact, COL_TILE, BF16)
    dhn = matmul(tag + "_dx_v", dval, w["up_v"], tb=True, res=matmul(tag + "_dx_g", dg, w["up_g"], tb=True))
    gw["up_g"] = matmul(tag + "_gw_up_g", s["hn"], dg, ta=True, out_dtype=BF16)
    gw["up_v"] = matmul(tag + "_gw_up_v", s["hn"], dval, ta=True, out_dtype=BF16)
    dh, dnf = _rms_bwd(tag + "_rms_b", s["h"], p["nf"], dhn, dh2)
    return dh, gw, dict(nf=dnf, fcw=dfcw, fcb=dfcb)


def _cols_from_slots(g):
    return jnp.moveaxis(g, 0, 1).reshape(g.shape[1], N_DEV * g.shape[2])


def _slots_from_cols(wmat):
    k, n = wmat.shape
    return jnp.moveaxis(wmat.reshape(k, N_DEV, n // N_DEV), 1, 0)


def _pad_last(a, width):
    return jnp.pad(a, [(0, 0)] * (a.ndim - 1) + [(0, width - a.shape[-1])])


def _heads_to_lanes(a):
    lead = a.shape[:-1]
    return _pad_last(a.reshape(lead + (SSD_STEPS, SSD_HEADS_PER_STEP)), LANES).reshape(lead + (SSD_STEPS * LANES,))


def _lanes_to_heads(a):
    lead = a.shape[:-1]
    return a.reshape(lead + (SSD_STEPS, LANES))[..., :SSD_HEADS_PER_STEP].reshape(lead + (SSM_HEADS,))


def prep_even(g_in, g_out):
    wn = _cols_from_slots(g_in)
    o1, o2, o3 = 2 * D_MODEL, 3 * D_MODEL, 3 * D_MODEL + SSM_CONV_DIM
    out = g_out.reshape(2 * D_MODEL, D_MODEL)
    return dict(uv=wn[:, :o1], z=wn[:, o1:o2], xbc=wn[:, o2:o3], dt=_heads_to_lanes(wn[:, o3:]),
                out_top=out[:D_MODEL], out_bot=out[D_MODEL:])


def unprep_even(gw):
    wn = jnp.concatenate([gw["uv"], gw["z"], gw["xbc"], _lanes_to_heads(gw["dt"])], axis=1)
    return _slots_from_cols(wn), jnp.concatenate([gw["out_top"], gw["out_bot"]], axis=0).reshape(N_DEV, -1, D_MODEL)


def prep_odd(g_in, g_uq, g_ukv, g_o):
    uq = _cols_from_slots(g_uq).reshape(MLA_RANK, MLA_HEADS, MLA_QK)
    ukv = _cols_from_slots(g_ukv).reshape(MLA_RANK, MLA_HEADS, MLA_NOPE + MLA_V)
    return dict(**{"in": _pad_last(g_in.reshape(D_MODEL, ODD_IN), ODD_IN_PAD)},
                uq=_pad_last(uq, MLA_QPAD).reshape(MLA_RANK, MLA_HEADS * MLA_QPAD),
                kn=ukv[:, :, :MLA_NOPE].reshape(MLA_RANK, MLA_HEADS * MLA_NOPE),
                v=ukv[:, :, MLA_NOPE:].reshape(MLA_RANK, MLA_HEADS * MLA_V),
                o=g_o.reshape(MLA_HEADS * MLA_V, D_MODEL))


def unprep_odd(gw):
    uq = gw["uq"].reshape(MLA_RANK, MLA_HEADS, MLA_QPAD)[:, :, :MLA_QK].reshape(MLA_RANK, MLA_HEADS * MLA_QK)
    ukv = jnp.concatenate([gw["kn"].reshape(MLA_RANK, MLA_HEADS, MLA_NOPE), gw["v"].reshape(MLA_RANK, MLA_HEADS, MLA_V)], axis=2)
    return (gw["in"][:, :ODD_IN].reshape(N_DEV, -1, ODD_IN), _slots_from_cols(uq),
            _slots_from_cols(ukv.reshape(MLA_RANK, -1)), gw["o"].reshape(N_DEV, -1, D_MODEL))


def prep_ffn(g_up, g_down):
    up = _cols_from_slots(g_up)
    return dict(up_g=up[:, :D_FF], up_v=up[:, D_FF:], down=g_down.reshape(D_FF, D_MODEL))


def unprep_ffn(gw):
    return _slots_from_cols(jnp.concatenate([gw["up_g"], gw["up_v"]], axis=1)), gw["down"].reshape(N_DEV, -1, D_MODEL)


SMALL_TILE = LANES * LANES


def _pack(arrs):
    flat = jnp.concatenate([a.reshape(-1).astype(F32) for a in arrs])
    size = -(-flat.shape[0] // SMALL_TILE) * SMALL_TILE
    return jnp.pad(flat, (0, size - flat.shape[0])).reshape(-1, LANES)


def _unpack(packed, shapes, lead=()):
    flat = packed.reshape(lead + (-1,))
    out, off = [], 0
    for shp in shapes:
        size = math.prod(shp)
        out.append(flat[..., off:off + size].reshape(lead + tuple(shp)))
        off += size
    return out


SMALL_SHARDED = {"ev_gm_ln_g": 2, "ev_gm_ln_b": 2, "ev_conv_w": 2, "od_q_norm": 1, "od_kv_norm": 1, "ff_conv_w": 2}
SMALL_REPLICATED = ["norm_mix", "norm_ffn", "norm_final", "ev_gm_ws", "ev_gm_bs", "ev_conv_b", "ev_dt_bias", "ev_a_log",
                    "ev_d_skip", "ev_ssm_norm_w", "ff_conv_b"]
MATRICES = {"ev_w_in": (2, 2048, 1156), "ev_w_out": (2, 512, 2048), "od_w_in": (2, 256, 1088), "od_w_uq": (2, 512, 384),
            "od_w_ukv": (2, 512, 512), "od_w_o": (2, 256, 2048), "ff_w_up": (4, 2048, 1408), "ff_w_down": (4, 704, 2048)}
WEIGHT_ORDER = ["norm_mix", "norm_ffn", "norm_final", "ev_w_in", "ev_gm_ln_g", "ev_gm_ln_b", "ev_gm_ws", "ev_gm_bs",
                "ev_conv_w", "ev_conv_b", "ev_dt_bias", "ev_a_log", "ev_d_skip", "ev_ssm_norm_w", "ev_w_out", "od_w_in",
                "od_q_norm", "od_kv_norm", "od_w_uq", "od_w_ukv", "od_w_o", "ff_w_up", "ff_conv_w", "ff_conv_b", "ff_w_down"]


def _full_from_shards(name, gathered):
    ax = SMALL_SHARDED[name]
    moved = jnp.moveaxis(gathered, 0, ax)
    shp = moved.shape
    return moved.reshape(shp[:ax] + (shp[ax] * shp[ax + 1],) + shp[ax + 2:])


def _my_shard(name, full, dev):
    ax = SMALL_SHARDED[name]
    shp = full.shape
    split = full.reshape(shp[:ax] + (N_DEV, shp[ax] // N_DEV) + shp[ax + 1:])
    return lax.dynamic_index_in_dim(split, dev, axis=ax, keepdims=False)


def _even_small(sm, j):
    row = lambda a: a.reshape(1, -1)
    hp = jnp.stack([sm["ev_dt_bias"][j], sm["ev_a_log"][j], sm["ev_d_skip"][j]])
    return dict(nm=row(sm["norm_mix"][2 * j]), lng=row(sm["ev_gm_ln_g"][j]), lnb=row(sm["ev_gm_ln_b"][j]),
                ws=sm["ev_gm_ws"][j], bst=sm["ev_gm_bs"][j].T, cw=sm["ev_conv_w"][j], cb=row(sm["ev_conv_b"][j]),
                hp=_heads_to_lanes(hp), nw=row(sm["ev_ssm_norm_w"][j]))


def _odd_small(sm, j):
    row = lambda a: a.reshape(1, -1)
    return dict(nm=row(sm["norm_mix"][2 * j + 1]), qn=row(sm["od_q_norm"][j]), kvn=row(sm["od_kv_norm"][j]))


def _ffn_small(sm, layer):
    row = lambda a: a.reshape(1, -1)
    return dict(nf=row(sm["norm_ffn"][layer]), fcw=sm["ff_conv_w"][layer], fcb=row(sm["ff_conv_b"][layer]))


def _rope_tables(positions):
    inv_freq = ROPE_THETA ** (-jnp.arange(0, MLA_ROPE, 2, dtype=F32) / MLA_ROPE)
    ang = positions.astype(F32).reshape(-1, 1) * inv_freq
    cos, sin = jnp.cos(ang), jnp.sin(ang)
    return _pad_last(jnp.concatenate([cos, cos], axis=1), LANES), _pad_last(jnp.concatenate([-sin, sin], axis=1), LANES)


def local_step(x, positions, target, sm, fetch_weights, emit_grads):
    cos, sin = _rope_tables(positions)
    h, saved = x, []
    for layer in range(4):
        j, tag = layer // 2, f"l{layer}"
        wm, zero = fetch_weights(2 * layer, h)
        if layer % 2 == 0:
            pm = _even_small(sm, j)
            pm["nm"] = pm["nm"] + zero
            h, sv = even_fwd(tag, h, wm, pm)
        else:
            pm = _odd_small(sm, j)
            pm["nm"] = pm["nm"] + zero
            h, sv = odd_fwd(tag, h, wm, pm, cos, sin)
        wf, zero = fetch_weights(2 * layer + 1, h)
        pf = _ffn_small(sm, layer)
        pf["nf"] = pf["nf"] + zero
        h, sf = ffn_fwd(tag + "f", h, wf, pf)
        saved.append((pm, sv, pf, sf, wm, wf))
    loss_tile, dh, dnfinal = final_loss("final_loss", h, sm["norm_final"].reshape(1, -1), target)
    gs = {k: [None] * v.shape[0] for k, v in sm.items() if k != "norm_final"}
    gs["norm_final"] = dnfinal.reshape(-1)
    zero = 0.0
    for layer in reversed(range(4)):
        j, tag = layer // 2, f"l{layer}"
        pm, sv, pf, sf, wm, wf = saved[layer]
        pf = dict(pf, nf=pf["nf"] + zero)
        dh, gwf, gpf = ffn_bwd(tag + "f", dh, sf, wf, pf)
        gs["norm_ffn"][layer], gs["ff_conv_w"][layer], gs["ff_conv_b"][layer] = gpf["nf"][0], gpf["fcw"], gpf["fcb"][0]
        zero = emit_grads(2 * layer + 1, gwf, dh)
        pm = dict(pm, nm=pm["nm"] + zero)
        if layer % 2 == 0:
            dh, gwm, gp = even_bwd(tag, dh, sv, wm, pm)
            hp = _lanes_to_heads(gp["hp"])
            gs["norm_mix"][layer] = gp["nm"][0]
            gs["ev_gm_ln_g"][j], gs["ev_gm_ln_b"][j] = gp["lng"].reshape(GM_GROUPS, -1), gp["lnb"].reshape(GM_GROUPS, -1)
            gs["ev_gm_ws"][j], gs["ev_gm_bs"][j] = gp["ws"], gp["bst"].T
            gs["ev_conv_w"][j], gs["ev_conv_b"][j] = gp["cw"], gp["cb"][0]
            gs["ev_dt_bias"][j], gs["ev_a_log"][j], gs["ev_d_skip"][j] = hp[0], hp[1], hp[2]
            gs["ev_ssm_norm_w"][j] = gp["nw"][0]
        else:
            dh, gwm, gp = odd_bwd(tag, dh, sv, wm, pm, cos, sin)
            gs["norm_mix"][layer] = gp["nm"][0]
            gs["od_q_norm"][j], gs["od_kv_norm"][j] = gp["qn"][0], gp["kvn"][0]
        zero = emit_grads(2 * layer, gwm, dh)
    gs = {k: (v if k == "norm_final" else jnp.stack(v)) for k, v in gs.items()}
    return loss_tile[0, 0], dh, gs


def kernel(x, positions, norm_mix, norm_ffn, norm_final, ev_w_in, ev_gm_ln_g, ev_gm_ln_b, ev_gm_ws, ev_gm_bs, ev_conv_w, ev_conv_b, ev_dt_bias, ev_a_log, ev_d_skip, ev_ssm_norm_w, ev_w_out, od_w_in, od_q_norm, od_kv_norm, od_w_uq, od_w_ukv, od_w_o, ff_w_up, ff_conv_w, ff_conv_b, ff_w_down, loss_target, m_norm_mix, m_norm_ffn, m_norm_final, m_ev_w_in, m_ev_gm_ln_g, m_ev_gm_ln_b, m_ev_gm_ws, m_ev_gm_bs, m_ev_conv_w, m_ev_conv_b, m_ev_dt_bias, m_ev_a_log, m_ev_d_skip, m_ev_ssm_norm_w, m_ev_w_out, m_od_w_in, m_od_q_norm, m_od_kv_norm, m_od_w_uq, m_od_w_ukv, m_od_w_o, m_ff_w_up, m_ff_conv_w, m_ff_conv_b, m_ff_w_down, v_norm_mix, v_norm_ffn, v_norm_final, v_ev_w_in, v_ev_gm_ln_g, v_ev_gm_ln_b, v_ev_gm_ws, v_ev_gm_bs, v_ev_conv_w, v_ev_conv_b, v_ev_dt_bias, v_ev_a_log, v_ev_d_skip, v_ev_ssm_norm_w, v_ev_w_out, v_od_w_in, v_od_q_norm, v_od_kv_norm, v_od_w_uq, v_od_w_ukv, v_od_w_o, v_ff_w_up, v_ff_conv_w, v_ff_conv_b, v_ff_w_down):
    args = dict(locals())
    wts = {n: args[n] for n in WEIGHT_ORDER}
    mom = {n: args["m_" + n] for n in WEIGHT_ORDER}
    var = {n: args["v_" + n] for n in WEIGHT_ORDER}
    dev = _device_slot()

    small_names = list(SMALL_SHARDED)
    small_shapes = [wts[n].shape for n in small_names]
    (small_all,) = all_gather("ag_small", [_pack([wts[n] for n in small_names])])
    small_full = _unpack(small_all, small_shapes, lead=(N_DEV,))
    sm = {n: _full_from_shards(n, g) for n, g in zip(small_names, small_full)}
    sm.update({n: wts[n] for n in SMALL_REPLICATED})

    bf = {n: wts[n].astype(BF16) for n in MATRICES}

    def stage_matrices(stage):
        layer, is_ffn = divmod(stage, 2)
        if is_ffn:
            return [("ff_w_up", layer), ("ff_w_down", layer)]
        return [(n, layer // 2) for n in (["ev_w_in", "ev_w_out"] if layer % 2 == 0 else ["od_w_in", "od_w_uq", "od_w_ukv", "od_w_o"])]

    def stage_fns(stage):
        layer, is_ffn = divmod(stage, 2)
        if is_ffn:
            return prep_ffn, unprep_ffn
        return (prep_even, unprep_even) if layer % 2 == 0 else (prep_odd, unprep_odd)

    n_stages, ahead = 8, 2

    def start_gather(stage, earlier=None):
        shards = [bf[n][i] for n, i in stage_matrices(stage)]
        if earlier is not None:
            shards, _ = lax.optimization_barrier((shards, earlier))
        return split_start(f"ag_s{stage}_start", shards, [(N_DEV,) + s.shape for s in shards],
                           gather_plan(len(shards)), 4 * len(shards))

    gathers = {}
    for stage in range(ahead):
        gathers[stage] = start_gather(stage, gathers[stage - 1][4] if stage else None)

    def fetch_weights(stage, h):
        plan = gather_plan(len(stage_matrices(stage)))
        shards, landed = split_wait(f"ag_s{stage}_wait", gathers.pop(stage), plan, h)
        g = gather_finish(f"ag_s{stage}_finish", landed)
        g = [lax.dynamic_update_index_in_dim(gk, sk, dev, 0) for gk, sk in zip(g, shards)]
        zero = 0.0
        if stage + ahead < n_stages:
            gathers[stage + ahead] = start_gather(stage + ahead, g[0])
            zero = gathers[stage + ahead][4][0, 0]
        return stage_fns(stage)[0](*g), zero

    scatters = []
    out = {n: None for n in MATRICES}

    def finish_scatter(after):
        stage, handle = scatters.pop(0)
        mats = stage_matrices(stage)
        sums, recv = split_wait(f"rs_s{stage}_wait", handle, chips_plan(len(mats)), after)
        for (n, i), p_, r_ in zip(mats, sums, recv):
            layers, rows, cols = MATRICES[n]
            two_d = lambda a: a.reshape(layers * rows, cols)
            out[n] = adamw(f"adamw_{n}_{i}", [(p_, 0), (r_, 0), (r_, 1), (r_, 2)], two_d(wts[n]), two_d(mom[n]),
                           two_d(var[n]), tr=_row_tile(rows), part=i, prev=out[n])

    def emit_grads(stage, gw, dh):
        if len(scatters) >= ahead:
            finish_scatter(dh)
        send = list(stage_fns(stage)[1](gw))
        from_sibling = exchange_sibling(f"rs_s{stage}_sibling", send)
        sums = [chip_sums(f"rs_s{stage}_add{k}", g, r, tr=_row_tile(g.shape[1])) for k, (g, r) in enumerate(zip(send, from_sibling))]
        handle = split_start(f"rs_s{stage}_start", sums, [(3,) + s.shape[1:] for s in sums], chips_plan(len(sums)), 3 * len(sums))
        scatters.append((stage, handle))
        return handle[4][0, 0]

    loss_local, dx, gs = local_step(x[0], positions[0], loss_target[0], sm, fetch_weights, emit_grads)
    loss = lax.psum(loss_local, ("x", "y", "c"))

    all_small = small_names + SMALL_REPLICATED
    (partials,) = all_gather("ar_small", [_pack([gs[n] for n in all_small])])
    total = sum_parts("ar_small_sum", [(partials, s) for s in range(N_DEV)])
    g_full = dict(zip(all_small, _unpack(total, [gs[n].shape for n in all_small])))
    g_mine = {n: (_my_shard(n, g_full[n], dev) if n in SMALL_SHARDED else g_full[n]) for n in all_small}
    packed = [_pack([d[n] for n in all_small]) for d in (g_mine, wts, mom, var)]
    res = adamw("adamw_small", [(packed[0][None], 0)], packed[1], packed[2], packed[3])
    unpacked = [_unpack(a, [wts[n].shape for n in all_small]) for a in res]
    for i, n in enumerate(all_small):
        out[n] = [u[i] for u in unpacked]
    while scatters:
        finish_scatter(res[0])
    for n in MATRICES:
        out[n] = [a.reshape(wts[n].shape) for a in out[n]]

    return (loss, dx[None], *[out[n][0] for n in WEIGHT_ORDER], *[out[n][1] for n in WEIGHT_ORDER],
            *[out[n][2] for n in WEIGHT_ORDER], *[out[n][3] for n in WEIGHT_ORDER])
```

```python
import functools
import math

import jax
import jax.numpy as jnp
from jax import lax
from jax.experimental import pallas as pl
from jax.experimental.pallas import tpu as pltpu

F32 = jnp.float32
BF16 = jnp.bfloat16
MESH = pl.DeviceIdType.MESH

V7X_VMEM_LIMIT_BYTES = 56 * 1024 * 1024
LANES = 128

EPS = 1e-6
D_MODEL = 2048
CHUNK = 64
GM_BLOCK = 128
GM_GROUPS = 8
GM_GROUP_DIM = D_MODEL // GM_GROUPS
SSM_HEADS = 32
SSM_HEAD_DIM = 64
SSM_GROUPS = 4
SSM_STATE = 128
SSM_CONV = 4
SSM_BC = SSM_GROUPS * SSM_STATE
SSM_CONV_DIM = D_MODEL + 2 * SSM_BC
SSD_HEADS_PER_STEP = 4
SSD_STEPS = SSM_HEADS // SSD_HEADS_PER_STEP
SSD_X_WIDTH = SSD_HEADS_PER_STEP * SSM_HEAD_DIM
MLA_HEADS = 16
MLA_RANK = 512
MLA_NOPE = 128
MLA_ROPE = 64
MLA_V = 128
MLA_QK = MLA_NOPE + MLA_ROPE
MLA_QPAD = 2 * LANES
ODD_IN = 2 * MLA_RANK + MLA_ROPE
ODD_IN_PAD = 2 * MLA_RANK + LANES
D_FF = 5632
ROPE_THETA = 10000.0
N_DEV = 8

ADAM_LR, ADAM_B1, ADAM_B2, ADAM_EPS, ADAM_WD, ADAM_STEP = 0.001, 0.9, 0.999, 1e-08, 0.01, 10


def _params(*sem):
    return pltpu.CompilerParams(dimension_semantics=sem, vmem_limit_bytes=V7X_VMEM_LIMIT_BYTES)


def _pick(dim, target):
    if dim <= target:
        return dim
    t = (target // LANES) * LANES
    while t >= LANES:
        if dim % t == 0:
            return t
        t -= LANES
    raise ValueError(f"no tile for {dim} under {target}")


MATMUL_VMEM_BUDGET = 32 * 1024 * 1024


def _matmul_tiles(m, n, k, a_bytes, b_bytes, o_bytes, has_res, ta):
    def fits(tm, tn):
        per_out = o_bytes + (4 if has_res else 0)
        return 2 * (tm * k * a_bytes + tn * k * b_bytes + tm * tn * per_out) <= MATMUL_VMEM_BUDGET

    tns = (2048, 1024, 512, 256, 128) if ta else (512, 256, 128)
    tms = (512, 256, 128) if ta else (2048, 1024, 512, 256, 128)
    for tn in tns:
        tn = _pick(n, tn)
        for tm in tms:
            tm = _pick(m, tm)
            if fits(tm, tn):
                return tm, tn
    raise ValueError(f"no matmul tiles for {m}x{n}x{k}")


def matmul(name, a, b, *, ta=False, tb=False, res=None, out_dtype=F32):
    m, k = (a.shape[1], a.shape[0]) if ta else a.shape
    n = b.shape[0] if tb else b.shape[1]
    assert k == (b.shape[1] if tb else b.shape[0]), (name, a.shape, b.shape)
    tm, tn = _matmul_tiles(m, n, k, a.dtype.itemsize, b.dtype.itemsize, jnp.dtype(out_dtype).itemsize, res is not None, ta)
    dims = (((0 if ta else 1,), (1 if tb else 0,)), ((), ()))

    def body(*refs):
        a_ref, b_ref, o_ref = refs[0], refs[1], refs[-1]
        total = lax.dot_general(a_ref[...].astype(BF16), b_ref[...].astype(BF16), dims, preferred_element_type=F32)
        if res is not None:
            total = total + refs[2][...]
        o_ref[...] = total.astype(o_ref.dtype)

    a_spec = pl.BlockSpec((k, tm), lambda i, j: (0, i)) if ta else pl.BlockSpec((tm, k), lambda i, j: (i, 0))
    b_spec = pl.BlockSpec((tn, k), lambda i, j: (j, 0)) if tb else pl.BlockSpec((k, tn), lambda i, j: (0, j))
    o_spec = pl.BlockSpec((tm, tn), lambda i, j: (i, j))
    ins, specs = [a, b], [a_spec, b_spec]
    if res is not None:
        ins.append(res)
        specs.append(o_spec)
    return pl.pallas_call(
        body, name=name, grid=(m // tm, n // tn), in_specs=specs, out_specs=o_spec,
        out_shape=jax.ShapeDtypeStruct((m, n), out_dtype),
        compiler_params=_params("parallel", "parallel"),
    )(*ins)


@functools.partial(jax.custom_vjp, nondiff_argnums=(1, 2))
def _roll(x, shift, axis):
    return pltpu.roll(x, shift, axis)


def _roll_fwd(x, shift, axis):
    return pltpu.roll(x, shift, axis), None


def _roll_bwd(shift, axis, _, g):
    return (pltpu.roll(g, (g.shape[axis] - shift) % g.shape[axis], axis),)


_roll.defvjp(_roll_fwd, _roll_bwd)


def _shift_down(x, s):
    rows = lax.broadcasted_iota(jnp.int32, x.shape, 0)
    return jnp.where(rows >= s, _roll(x, s, 0), 0.0)


def _dwconv(x, w, b):
    taps = w.shape[0]
    y = b + w[taps - 1:taps, :] * x
    for kk in range(taps - 1):
        y = y + w[kk:kk + 1, :] * _shift_down(x, taps - 1 - kk)
    return y


def _rms(x, w):
    return x * lax.rsqrt(jnp.mean(x * x, -1, keepdims=True) + EPS) * w


def _rms_f(h, w):
    return (_rms(h, w),)


def _gmlp_f(uv, lng, lnb, ws, bst):
    r = lax.broadcasted_iota(jnp.int32, (GM_BLOCK, GM_BLOCK), 0) // CHUNK
    c = lax.broadcasted_iota(jnp.int32, (GM_BLOCK, GM_BLOCK), 1) // CHUNK
    outs = []
    for g in range(GM_GROUPS):
        lo, hi = g * GM_GROUP_DIM, (g + 1) * GM_GROUP_DIM
        gu = jax.nn.gelu(uv[:, lo:hi])
        gv = jax.nn.gelu(uv[:, D_MODEL + lo:D_MODEL + hi])
        xc = gv - jnp.mean(gv, -1, keepdims=True)
        var = jnp.mean(xc * xc, -1, keepdims=True)
        vn = xc * lax.rsqrt(var + EPS) * lng[:, lo:hi] + lnb[:, lo:hi]
        wm = jnp.where(r >= c, ws[g], 0.0).astype(BF16)
        gate = jnp.dot(wm, vn.astype(BF16), preferred_element_type=F32) + bst[:, g:g + 1]
        outs.append(gu * gate)
    return (jnp.concatenate(outs, axis=1),)


def _conv_silu_f(x, w, b):
    return (jax.nn.silu(_dwconv(x, w, b)),)


def _ffn_act_f(g, val, w, b):
    return (jax.nn.gelu(_dwconv(g, w, b)) * val,)


def _gate_norm_f(y, z, nw):
    y2 = y * jax.nn.silu(z)
    width = D_MODEL // SSM_GROUPS
    outs = []
    for g in range(SSM_GROUPS):
        blk = y2[:, g * width:(g + 1) * width]
        outs.append(blk * lax.rsqrt(jnp.mean(blk * blk, -1, keepdims=True) + EPS))
    return (jnp.concatenate(outs, axis=1) * nw,)


def _rope(x, cos, sin):
    lane = lax.broadcasted_iota(jnp.int32, x.shape, 1)
    half = MLA_ROPE // 2
    swapped = jnp.where(lane < half, _roll(x, LANES - half, 1), _roll(x, half, 1))
    return x * cos + swapped * sin


def _qkv_norm_f(proj, cos, sin, qn, kvn):
    cq = _rms(proj[:, :MLA_RANK], qn)
    ckv = _rms(proj[:, MLA_RANK:2 * MLA_RANK], kvn)
    kpe = _rope(proj[:, 2 * MLA_RANK:], cos, sin)
    return cq, ckv, kpe


def _attn_scores(q0, k0, qh, kn, kpe, cos, sin):
    qn = qh[:, :MLA_NOPE]
    qp = _rope(qh[:, MLA_NOPE:], cos, sin)
    nt = (((1,), (1,)), ((), ()))
    s = lax.dot_general(qn.astype(BF16), kn.astype(BF16), nt, preferred_element_type=F32)
    s = s + lax.dot_general(qp.astype(BF16), kpe.astype(BF16), nt, preferred_element_type=F32)
    s = s * (MLA_QK ** -0.5)
    qc = (q0 + lax.broadcasted_iota(jnp.int32, s.shape, 0)) // CHUNK
    kc = (k0 + lax.broadcasted_iota(jnp.int32, s.shape, 1)) // CHUNK
    return jnp.where(kc <= qc, s, -jnp.inf)


def _ssd_chunk_f(x, bm, cm, pdt, hp, sprev):
    nh = SSD_HEADS_PER_STEP
    dt = jax.nn.softplus(pdt[:, :nh] + hp[0:1, :nh])
    da = dt * (-jnp.exp(hp[1:2, :nh]))
    r = lax.broadcasted_iota(jnp.int32, (CHUNK, CHUNK), 0)
    c = lax.broadcasted_iota(jnp.int32, (CHUNK, CHUNK), 1)
    tril = r >= c
    cs = jnp.dot(tril.astype(F32), da, precision=lax.Precision.HIGHEST, preferred_element_type=F32)
    cst = cs.T
    nt = (((1,), (1,)), ((), ()))
    tn = (((0,), (0,)), ((), ()))
    cb = lax.dot_general(cm.astype(BF16), bm.astype(BF16), nt, preferred_element_type=F32)
    ys, snew = [], []
    for e in range(nh):
        xe = x[:, e * SSM_HEAD_DIM:(e + 1) * SSM_HEAD_DIM]
        xd = xe * dt[:, e:e + 1]
        cse = cs[:, e:e + 1]
        decay = jnp.exp(jnp.where(tril, cse - cst[e:e + 1, :], -jnp.inf))
        y = jnp.dot((cb * decay).astype(BF16), xd.astype(BF16), preferred_element_type=F32)
        tot = cse[CHUNK - 1:CHUNK, :]
        st = lax.dot_general((xd * jnp.exp(tot - cse)).astype(BF16), bm.astype(BF16), tn, preferred_element_type=F32)
        yoff = lax.dot_general(cm.astype(BF16), sprev[e].astype(BF16), nt, preferred_element_type=F32)
        ys.append(y + yoff * jnp.exp(cse) + hp[2:3, e:e + 1] * xe)
        snew.append((jnp.exp(tot) * sprev[e] + st)[None])
    return jnp.concatenate(ys, axis=1), jnp.concatenate(snew, axis=0)


def _full_spec(a):
    nd = a.ndim
    return pl.BlockSpec(a.shape, lambda i, nd=nd: (0,) * nd)


def rows_fwd(name, f, rows, params, outs, tr):
    t = rows[0][0].shape[0]
    nr, npar = len(rows), len(params)

    def body(*refs):
        vals = f(*[x[...].astype(F32) for x in refs[:nr + npar]])
        for o_ref, val in zip(refs[nr + npar:], vals):
            o_ref[...] = val.astype(o_ref.dtype)

    in_specs = [pl.BlockSpec((tr, w), lambda i, cb=cb: (i, cb)) for _, w, cb in rows] + [_full_spec(p) for p in params]
    out = pl.pallas_call(
        body, name=name, grid=(t // tr,), in_specs=in_specs,
        out_specs=[pl.BlockSpec((tr, w), lambda i: (i, 0)) for w, _ in outs],
        out_shape=[jax.ShapeDtypeStruct((t, w), dt) for w, dt in outs],
        compiler_params=_params("parallel"),
    )(*[a for a, _, _ in rows], *params)
    return out


def rows_bwd(name, f, rows, params, cots, tr, d_dtypes, n_nondiff=0, add=None):
    t = rows[0][0].shape[0]
    nr, npar, nc = len(rows), len(params), len(cots)
    nd = nr - n_nondiff
    has_add = add is not None

    def body(*refs):
        i = pl.program_id(0)
        row_vals = [x[...].astype(F32) for x in refs[:nr]]
        par_vals = [x[...].astype(F32) for x in refs[nr:nr + npar]]
        cot_refs = refs[nr + npar:nr + npar + nc]
        pos = nr + npar + nc
        add_ref = refs[pos] if has_add else None
        pos += int(has_add)
        drow_refs = refs[pos:pos + nd]
        dpar_refs = refs[pos + nd:]

        def g(*diff):
            return f(*diff[:nd], *row_vals[nd:], *diff[nd:])

        _, vjp = jax.vjp(g, *row_vals[:nd], *par_vals)
        grads = vjp(tuple(cr[...].astype(F32) for cr in cot_refs))
        for j, d_ref in enumerate(drow_refs):
            val = grads[j]
            if j == 0 and has_add:
                val = val + add_ref[...]
            d_ref[...] = val.astype(d_ref.dtype)
        for j, d_ref in enumerate(dpar_refs):
            @pl.when(i == 0)
            def _(d_ref=d_ref, j=j):
                d_ref[...] = grads[nd + j]

            @pl.when(i > 0)
            def _(d_ref=d_ref, j=j):
                d_ref[...] += grads[nd + j]

    in_specs = [pl.BlockSpec((tr, w), lambda i, cb=cb: (i, cb)) for _, w, cb in rows] + [_full_spec(p) for p in params]
    in_specs += [pl.BlockSpec((tr, w), lambda i, cb=cb: (i, cb)) for _, w, cb in cots]
    ins = [a for a, _, _ in rows] + list(params) + [a for a, _, _ in cots]
    if has_add:
        in_specs.append(pl.BlockSpec((tr, rows[0][1]), lambda i: (i, 0)))
        ins.append(add)
    out_specs = [pl.BlockSpec((tr, rows[j][1]), lambda i: (i, 0)) for j in range(nd)] + [_full_spec(p) for p in params]
    out_shape = [jax.ShapeDtypeStruct((t, rows[j][1]), d_dtypes[j]) for j in range(nd)]
    out_shape += [jax.ShapeDtypeStruct(p.shape, F32) for p in params]
    out = pl.pallas_call(
        body, name=name, grid=(t // tr,), in_specs=in_specs, out_specs=out_specs, out_shape=out_shape,
        compiler_params=_params("arbitrary"),
    )(*ins)
    return out[:nd], out[nd:]


def cols_fwd(name, f, cols, cparams, out_dtype, tc):
    t = cols[0][0].shape[0]
    width = cparams[0].shape[1]
    ncol = len(cols)

    def body(*refs):
        (val,) = f(*[x[...].astype(F32) for x in refs[:-1]])
        refs[-1][...] = val.astype(refs[-1].dtype)

    in_specs = [pl.BlockSpec((t, tc), lambda j, o=o: (0, o + j)) for _, o in cols]
    in_specs += [pl.BlockSpec((p.shape[0], tc), lambda j: (0, j)) for p in cparams]
    return pl.pallas_call(
        body, name=name, grid=(width // tc,), in_specs=in_specs,
        out_specs=pl.BlockSpec((t, tc), lambda j: (0, j)),
        out_shape=jax.ShapeDtypeStruct((t, width), out_dtype),
        compiler_params=_params("parallel"),
    )(*[a for a, _ in cols], *cparams)


def cols_bwd(name, f, cols, cparams, cot, tc, d_dtype):
    t = cols[0][0].shape[0]
    width = cparams[0].shape[1]
    ncol, npar = len(cols), len(cparams)

    def body(*refs):
        vals = [x[...].astype(F32) for x in refs[:ncol + npar]]
        _, vjp = jax.vjp(f, *vals)
        grads = vjp((refs[ncol + npar][...].astype(F32),))
        for d_ref, gval in zip(refs[ncol + npar + 1:], grads):
            d_ref[...] = gval.astype(d_ref.dtype)

    in_specs = [pl.BlockSpec((t, tc), lambda j, o=o: (0, o + j)) for _, o in cols]
    in_specs += [pl.BlockSpec((p.shape[0], tc), lambda j: (0, j)) for p in cparams]
    in_specs.append(pl.BlockSpec((t, tc), lambda j: (0, j)))
    out_specs = [pl.BlockSpec((t, tc), lambda j: (0, j)) for _ in cols]
    out_specs += [pl.BlockSpec((p.shape[0], tc), lambda j: (0, j)) for p in cparams]
    out_shape = [jax.ShapeDtypeStruct((t, width), d_dtype) for _ in cols]
    out_shape += [jax.ShapeDtypeStruct(p.shape, F32) for p in cparams]
    out = pl.pallas_call(
        body, name=name, grid=(width // tc,), in_specs=in_specs, out_specs=out_specs, out_shape=out_shape,
        compiler_params=_params("parallel"),
    )(*[a for a, _ in cols], *cparams, cot)
    return out[:ncol], out[ncol:]


def _ssd_in_specs(t):
    heads_per_group = SSM_HEADS // SSM_GROUPS
    steps_per_group = heads_per_group // SSD_HEADS_PER_STEP
    b_blk = D_MODEL // LANES
    c_blk = (D_MODEL + SSM_BC) // LANES
    return [
        pl.BlockSpec((t, SSD_X_WIDTH), lambda s: (0, s)),
        pl.BlockSpec((t, LANES), lambda s: (0, b_blk + s // steps_per_group)),
        pl.BlockSpec((t, LANES), lambda s: (0, c_blk + s // steps_per_group)),
        pl.BlockSpec((t, LANES), lambda s: (0, s)),
        pl.BlockSpec((3, LANES), lambda s: (0, s)),
    ]


def ssd_fwd(name, xa, pdt, hp):
    t = xa.shape[0]
    nc = t // CHUNK
    nh = SSD_HEADS_PER_STEP

    def body(x_ref, b_ref, c_ref, pdt_ref, hp_ref, y_ref, st_ref, s_scr):
        s_scr[...] = jnp.zeros_like(s_scr)

        def step(ci, carry):
            sl = pl.ds(pl.multiple_of(ci * CHUNK, CHUNK), CHUNK)
            sprev = s_scr[...]
            st_ref[0, ci] = sprev
            y, snew = _ssd_chunk_f(x_ref[sl, :], b_ref[sl, :], c_ref[sl, :], pdt_ref[sl, :], hp_ref[...], sprev)
            y_ref[sl, :] = y
            s_scr[...] = snew
            return carry

        lax.fori_loop(0, nc, step, 0)

    return pl.pallas_call(
        body, name=name, grid=(SSD_STEPS,), in_specs=_ssd_in_specs(t),
        out_specs=[pl.BlockSpec((t, SSD_X_WIDTH), lambda s: (0, s)),
                   pl.BlockSpec((1, nc, nh, SSM_HEAD_DIM, SSM_STATE), lambda s: (s, 0, 0, 0, 0))],
        out_shape=[jax.ShapeDtypeStruct((t, D_MODEL), F32),
                   jax.ShapeDtypeStruct((SSD_STEPS, nc, nh, SSM_HEAD_DIM, SSM_STATE), F32)],
        scratch_shapes=[pltpu.VMEM((nh, SSM_HEAD_DIM, SSM_STATE), F32)],
        compiler_params=_params("parallel"),
    )(xa, xa, xa, pdt, hp)


def ssd_bwd(name, xa, pdt, hp, states, dy):
    t = xa.shape[0]
    nc = t // CHUNK
    nh = SSD_HEADS_PER_STEP
    steps_per_group = SSM_HEADS // SSM_GROUPS // nh

    def body(x_ref, b_ref, c_ref, pdt_ref, hp_ref, st_ref, dy_ref, dx_ref, db_ref, dc_ref, dpdt_ref, dhp_ref, ds_scr, dhp_scr):
        first = pl.program_id(0) % steps_per_group == 0
        ds_scr[...] = jnp.zeros_like(ds_scr)
        dhp_scr[...] = jnp.zeros_like(dhp_scr)

        def step(i, carry):
            ci = nc - 1 - i
            sl = pl.ds(pl.multiple_of(ci * CHUNK, CHUNK), CHUNK)
            _, vjp = jax.vjp(_ssd_chunk_f, x_ref[sl, :], b_ref[sl, :], c_ref[sl, :], pdt_ref[sl, :], hp_ref[...], st_ref[0, ci])
            dx, db, dc, dpdt, dhp, dsprev = vjp((dy_ref[sl, :], ds_scr[...]))
            dx_ref[sl, :] = dx
            dpdt_ref[sl, :] = dpdt.astype(dpdt_ref.dtype)

            @pl.when(first)
            def _():
                db_ref[sl, :] = db
                dc_ref[sl, :] = dc

            @pl.when(jnp.logical_not(first))
            def _():
                db_ref[sl, :] += db
                dc_ref[sl, :] += dc

            ds_scr[...] = dsprev
            dhp_scr[...] += dhp
            return carry

        lax.fori_loop(0, nc, step, 0)
        dhp_ref[...] = dhp_scr[...]

    in_specs = _ssd_in_specs(t) + [
        pl.BlockSpec((1, nc, nh, SSM_HEAD_DIM, SSM_STATE), lambda s: (s, 0, 0, 0, 0)),
        pl.BlockSpec((t, SSD_X_WIDTH), lambda s: (0, s)),
    ]
    out_specs = [
        pl.BlockSpec((t, SSD_X_WIDTH), lambda s: (0, s)),
        pl.BlockSpec((t, LANES), lambda s: (0, s // steps_per_group)),
        pl.BlockSpec((t, LANES), lambda s: (0, s // steps_per_group)),
        pl.BlockSpec((t, LANES), lambda s: (0, s)),
        pl.BlockSpec((3, LANES), lambda s: (0, s)),
    ]
    out_shape = [
        jax.ShapeDtypeStruct((t, D_MODEL), F32),
        jax.ShapeDtypeStruct((t, SSM_BC), F32),
        jax.ShapeDtypeStruct((t, SSM_BC), F32),
        jax.ShapeDtypeStruct((t, SSD_STEPS * LANES), BF16),
        jax.ShapeDtypeStruct((3, SSD_STEPS * LANES), F32),
    ]
    return pl.pallas_call(
        body, name=name, grid=(SSD_STEPS,), in_specs=in_specs, out_specs=out_specs, out_shape=out_shape,
        scratch_shapes=[pltpu.VMEM((nh, SSM_HEAD_DIM, SSM_STATE), F32), pltpu.VMEM((3, LANES), F32)],
        compiler_params=_params("arbitrary"),
    )(xa, xa, xa, pdt, hp, states, dy)


ATTN_TQ = 256
ATTN_KSTEP = 512


def _attn_extents(t):
    return [min(t, (g + 1) * ATTN_KSTEP) for g in range(-(-t // ATTN_KSTEP))]


def _attn_f(q0, qh, kn, kpe, v, cos, sin):
    p = jax.nn.softmax(_attn_scores(q0, 0, qh, kn, kpe, cos, sin), axis=-1)
    return (jnp.dot(p.astype(BF16), v.astype(BF16), preferred_element_type=F32),)


def _attn_in_specs(t):
    return [
        pl.BlockSpec((ATTN_TQ, MLA_QPAD), lambda h, qi: (qi, h)),
        pl.BlockSpec((t, MLA_NOPE), lambda h, qi: (0, h)),
        pl.BlockSpec((t, LANES), lambda h, qi: (0, 0)),
        pl.BlockSpec((t, MLA_V), lambda h, qi: (0, h)),
        pl.BlockSpec((ATTN_TQ, LANES), lambda h, qi: (qi, 0)),
        pl.BlockSpec((ATTN_TQ, LANES), lambda h, qi: (qi, 0)),
    ]


def attn_fwd(name, q, kn, kpe, v, cos, sin):
    t = q.shape[0]

    def body(q_ref, kn_ref, kpe_ref, v_ref, cos_ref, sin_ref, o_ref):
        qi = pl.program_id(1)
        for span, ext in enumerate(_attn_extents(t)):
            @pl.when(qi // (ATTN_KSTEP // ATTN_TQ) == span)
            def _(ext=ext):
                (o,) = _attn_f(qi * ATTN_TQ, q_ref[...], kn_ref[0:ext, :], kpe_ref[0:ext, :], v_ref[0:ext, :],
                               cos_ref[...], sin_ref[...])
                o_ref[...] = o.astype(o_ref.dtype)

    return pl.pallas_call(
        body, name=name, grid=(MLA_HEADS, t // ATTN_TQ), in_specs=_attn_in_specs(t),
        out_specs=pl.BlockSpec((ATTN_TQ, MLA_V), lambda h, qi: (qi, h)),
        out_shape=jax.ShapeDtypeStruct((t, MLA_HEADS * MLA_V), BF16),
        compiler_params=_params("parallel", "parallel"),
    )(q, kn, kpe, v, cos, sin)


def attn_bwd(name, q, kn, kpe, v, cos, sin, do):
    t = q.shape[0]

    def body(q_ref, kn_ref, kpe_ref, v_ref, cos_ref, sin_ref, do_ref, dq_ref, dkn_ref, dkpe_ref, dv_ref):
        h, qi = pl.program_id(0), pl.program_id(1)
        q0 = qi * ATTN_TQ
        cos, sin = cos_ref[...], sin_ref[...]

        @pl.when(qi == 0)
        def _():
            dkn_ref[...] = jnp.zeros_like(dkn_ref)
            dv_ref[...] = jnp.zeros_like(dv_ref)

        @pl.when(jnp.logical_and(h == 0, qi == 0))
        def _():
            dkpe_ref[...] = jnp.zeros_like(dkpe_ref)

        for span, ext in enumerate(_attn_extents(t)):
            @pl.when(qi // (ATTN_KSTEP // ATTN_TQ) == span)
            def _(ext=ext):
                def g(qh, knv, kpev, vv):
                    return _attn_f(q0, qh, knv, kpev, vv, cos, sin)

                _, vjp = jax.vjp(g, q_ref[...].astype(F32), kn_ref[0:ext, :].astype(F32), kpe_ref[0:ext, :].astype(F32),
                                 v_ref[0:ext, :].astype(F32))
                dq, dkn, dkpe, dv = vjp((do_ref[...].astype(F32),))
                dq_ref[...] = dq.astype(dq_ref.dtype)
                dkn_ref[0:ext, :] += dkn
                dkpe_ref[0:ext, :] += dkpe
                dv_ref[0:ext, :] += dv

    in_specs = _attn_in_specs(t) + [pl.BlockSpec((ATTN_TQ, MLA_V), lambda h, qi: (qi, h))]
    out_specs = [
        pl.BlockSpec((ATTN_TQ, MLA_QPAD), lambda h, qi: (qi, h)),
        pl.BlockSpec((t, MLA_NOPE), lambda h, qi: (0, h)),
        pl.BlockSpec((t, LANES), lambda h, qi: (0, 0)),
        pl.BlockSpec((t, MLA_V), lambda h, qi: (0, h)),
    ]
    out_shape = [
        jax.ShapeDtypeStruct((t, MLA_HEADS * MLA_QPAD), BF16),
        jax.ShapeDtypeStruct((t, MLA_HEADS * MLA_NOPE), F32),
        jax.ShapeDtypeStruct((t, LANES), F32),
        jax.ShapeDtypeStruct((t, MLA_HEADS * MLA_V), F32),
    ]
    return pl.pallas_call(
        body, name=name, grid=(MLA_HEADS, t // ATTN_TQ), in_specs=in_specs, out_specs=out_specs, out_shape=out_shape,
        compiler_params=_params("arbitrary", "arbitrary"),
    )(q, kn, kpe, v, cos, sin, do)


def final_loss(name, h, nf, target, tr=256):
    t, d = h.shape

    def body(h_ref, w_ref, t_ref, loss_ref, dh_ref, dw_ref):
        i = pl.program_id(0)
        tgt = t_ref[...]

        def f(hv, wv):
            err = _rms(hv, wv) - tgt
            return 0.5 * jnp.sum(jnp.mean(err * err, -1, keepdims=True), 0, keepdims=True)

        val, vjp = jax.vjp(f, h_ref[...], w_ref[...])
        dh, dw = vjp(jnp.ones((1, 1), F32))
        dh_ref[...] = dh
        tile = jnp.broadcast_to(val, loss_ref.shape)

        @pl.when(i == 0)
        def _():
            loss_ref[...] = tile
            dw_ref[...] = dw

        @pl.when(i > 0)
        def _():
            loss_ref[...] += tile
            dw_ref[...] += dw

    row = pl.BlockSpec((tr, d), lambda i: (i, 0))
    return pl.pallas_call(
        body, name=name, grid=(t // tr,), in_specs=[row, _full_spec(nf), row],
        out_specs=[pl.BlockSpec((8, LANES), lambda i: (0, 0)), row, _full_spec(nf)],
        out_shape=[jax.ShapeDtypeStruct((8, LANES), F32), jax.ShapeDtypeStruct((t, d), F32), jax.ShapeDtypeStruct(nf.shape, F32)],
        compiler_params=_params("arbitrary"),
    )(h, nf, target)


ANY = pl.BlockSpec(memory_space=pl.ANY)
CHIP_ORDER = ((0, 0), (0, 1), (1, 0), (1, 1))


def _place():
    return lax.axis_index("x"), lax.axis_index("y"), lax.axis_index("c")


def _other_chips(x, y):
    return [(1 - x, y), (x, 1 - y), (1 - x, 1 - y)]


def _device_slot():
    x, y, c = _place()
    return 4 * x + 2 * y + c


def _row_tile(rows):
    return next(t for t in (128, 64, 32, 16) if rows % t == 0)


def all_gather(name, shards):
    n = len(shards)

    def body(*refs):
        ins, outs = refs[:n], refs[n:2 * n]
        send_sems, recv_sems, local_sems = refs[2 * n:]
        x, y, c = _place()
        me, sibling = (x, y, c), (x, y, 1 - c)
        chips = _other_chips(x, y)

        def copy(k, j, block, to, from_input=False):
            dst = outs[k].at[4 * block[0] + 2 * block[1] + block[2]]
            return pltpu.make_async_remote_copy(
                src_ref=ins[k] if from_input else dst, dst_ref=dst,
                send_sem=send_sems.at[7 * k + j], recv_sem=recv_sems.at[7 * k + j],
                device_id=to, device_id_type=MESH)

        mine = [pltpu.make_async_copy(ins[k], outs[k].at[4 * x + 2 * y + c], local_sems.at[k]) for k in range(n)]
        for cp in mine:
            cp.start()
        first = []
        for k in range(n):
            first.append(copy(k, 0, me, sibling, True))
            first += [copy(k, 1 + j, me, (*chip, c), True) for j, chip in enumerate(chips)]
        for cp in first:
            cp.start()
        passed = []
        for j, chip in enumerate(chips):
            for k in range(n):
                copy(k, 1 + j, (*chip, c), me).wait_recv()
                fwd = copy(k, 4 + j, (*chip, c), sibling)
                fwd.start()
                passed.append(fwd)
        for k in range(n):
            copy(k, 0, sibling, me).wait_recv()
        for j, chip in enumerate(chips):
            for k in range(n):
                copy(k, 4 + j, (*chip, 1 - c), me).wait_recv()
        for cp in first + passed:
            cp.wait_send()
        for cp in mine:
            cp.wait()

    return pl.pallas_call(
        body, name=name, in_specs=[ANY] * n, out_specs=[ANY] * n,
        out_shape=[jax.ShapeDtypeStruct((N_DEV,) + s.shape, s.dtype) for s in shards],
        scratch_shapes=[pltpu.SemaphoreType.DMA((7 * n,)), pltpu.SemaphoreType.DMA((7 * n,)), pltpu.SemaphoreType.DMA((n,))],
    )(*shards)


def exchange_sibling(name, gs):
    n = len(gs)

    def body(*refs):
        ins, outs = refs[:n], refs[n:2 * n]
        send_sems, recv_sems = refs[2 * n:]
        x, y, c = _place()
        copies = []
        for k in range(n):
            for q, (cx, cy) in enumerate(CHIP_ORDER):
                copies.append(pltpu.make_async_remote_copy(
                    src_ref=ins[k].at[4 * cx + 2 * cy + (1 - c)], dst_ref=outs[k].at[q],
                    send_sem=send_sems.at[4 * k + q], recv_sem=recv_sems.at[4 * k + q],
                    device_id=(x, y, 1 - c), device_id_type=MESH))
        for cp in copies:
            cp.start()
        for cp in copies:
            cp.wait()

    return pl.pallas_call(
        body, name=name, in_specs=[ANY] * n, out_specs=[ANY] * n,
        out_shape=[jax.ShapeDtypeStruct((4,) + g.shape[1:], g.dtype) for g in gs],
        scratch_shapes=[pltpu.SemaphoreType.DMA((4 * n,)), pltpu.SemaphoreType.DMA((4 * n,))],
    )(*gs)


def chip_sums(name, g, recv, tr=128):
    _, r, c = g.shape

    def body(g_ref, r_ref, o_ref):
        o_ref[...] = (g_ref[...].astype(F32) + r_ref[...].astype(F32)).astype(o_ref.dtype)

    def chip(i):
        x, y, _ = _place()
        return jnp.where(i % 2 == 1, 1 - x, x), jnp.where(i >= 2, 1 - y, y)

    def g_index(i, j):
        cx, cy = chip(i)
        return 4 * cx + 2 * cy + lax.axis_index("c"), j, 0

    def recv_index(i, j):
        cx, cy = chip(i)
        return 2 * cx + cy, j, 0

    return pl.pallas_call(
        body, name=name, grid=(4, r // tr),
        in_specs=[pl.BlockSpec((1, tr, c), g_index), pl.BlockSpec((1, tr, c), recv_index)],
        out_specs=pl.BlockSpec((1, tr, c), lambda i, j: (i, j, 0)),
        out_shape=jax.ShapeDtypeStruct((4, r, c), g.dtype),
        compiler_params=_params("parallel", "parallel"),
    )(g, recv)


def sum_parts(name, parts, tr=128):
    _, r, c = parts[0][0].shape

    def body(*refs):
        total = refs[0][0].astype(F32)
        for ref in refs[1:-1]:
            total = total + ref[0].astype(F32)
        refs[-1][...] = total

    return pl.pallas_call(
        body, name=name, grid=(r // tr,),
        in_specs=[pl.BlockSpec((1, tr, c), lambda i, s=s: (s, i, 0)) for _, s in parts],
        out_specs=pl.BlockSpec((tr, c), lambda i: (i, 0)), out_shape=jax.ShapeDtypeStruct((r, c), F32),
        compiler_params=_params("parallel"),
    )(*[a for a, _ in parts])


def adamw(name, parts, w, m, v, tr=128, part=0, prev=None):
    _, r, c = parts[0][0].shape
    np_ = len(parts)
    first = part * (r // tr)

    def body(*refs):
        g = refs[0][0].astype(F32)
        for ref in refs[1:np_]:
            g = g + ref[0].astype(F32)
        w_ref, m_ref, v_ref = refs[np_:np_ + 3]
        g_out, d_out, m_out, v_out = refs[-4:]
        new_m = ADAM_B1 * m_ref[...] + (1.0 - ADAM_B1) * g
        new_v = ADAM_B2 * v_ref[...] + (1.0 - ADAM_B2) * (g * g)
        m_hat = new_m / (1.0 - ADAM_B1 ** ADAM_STEP)
        v_hat = new_v / (1.0 - ADAM_B2 ** ADAM_STEP)
        g_out[...] = g
        d_out[...] = -ADAM_LR * (m_hat / (jnp.sqrt(v_hat) + ADAM_EPS) + ADAM_WD * w_ref[...])
        m_out[...] = new_m
        v_out[...] = new_v

    tile = pl.BlockSpec((tr, c), lambda i: (first + i, 0))
    in_specs = [pl.BlockSpec((1, tr, c), lambda i, s=s: (s, i, 0)) for _, s in parts] + [tile] * 3
    ins = [a for a, _ in parts] + [w, m, v]
    aliases = {}
    if prev is not None:
        aliases = {len(ins) + k: k for k in range(4)}
        in_specs += [ANY] * 4
        ins += list(prev)
    return pl.pallas_call(
        body, name=name, grid=(r // tr,), in_specs=in_specs,
        out_specs=[tile] * 4, out_shape=[jax.ShapeDtypeStruct(w.shape, F32)] * 4,
        input_output_aliases=aliases, compiler_params=_params("parallel"),
    )(*ins)


HBM = pl.BlockSpec(memory_space=pltpu.HBM)
SEM = pl.BlockSpec(memory_space=pltpu.SEMAPHORE)
SIDE_EFFECT = pltpu.SideEffectType.DATAFLOW_SIDE_EFFECTING


def _split_copies(plan, src_refs, land_refs, send_sems, recv_sems):
    copies = []
    for i, (k, src_slot, land_slot, device) in enumerate(plan(*_place())):
        copies.append(pltpu.make_async_remote_copy(
            src_ref=src_refs[k] if src_slot is None else src_refs[k].at[src_slot], dst_ref=land_refs[k].at[land_slot],
            send_sem=send_sems.at[i], recv_sem=recv_sems.at[i], device_id=device, device_id_type=MESH))
    return copies


def split_start(name, srcs, land_shapes, plan, n_copies):
    n = len(srcs)

    def body(*refs):
        src_refs, land_refs = refs[:n], refs[n:2 * n]
        send_sems, recv_sems, token = refs[2 * n], refs[2 * n + 1], refs[-1]
        for cp in _split_copies(plan, src_refs, land_refs, send_sems, recv_sems):
            cp.start()
        token[...] = jnp.zeros_like(token)

    lands = [lax.empty(shape, s.dtype) for shape, s in zip(land_shapes, srcs)]
    ins = [pltpu.with_memory_space_constraint(a, pltpu.HBM) for a in list(srcs) + lands]
    out = pl.pallas_call(
        body, name=name,
        out_shape=(pltpu.SemaphoreType.DMA((n_copies,)), pltpu.SemaphoreType.DMA((n_copies,)),
                   *[pltpu.HBM(a.shape, a.dtype) for a in ins], jax.ShapeDtypeStruct((8, LANES), F32)),
        in_specs=[HBM] * (2 * n), out_specs=(SEM, SEM, *[HBM] * (2 * n), pl.BlockSpec(memory_space=pltpu.VMEM)),
        input_output_aliases={i: 2 + i for i in range(2 * n)},
        compiler_params=pltpu.CompilerParams(has_side_effects=SIDE_EFFECT),
    )(*ins)
    return out[0], out[1], list(out[2:2 + n]), list(out[2 + n:2 + 2 * n]), out[-1]


def split_wait(name, handle, plan, after):
    send_sems, recv_sems, srcs, lands, _ = handle
    n = len(srcs)

    def body(*refs):
        src_refs, land_refs = refs[:n], refs[n:2 * n]
        for cp in _split_copies(plan, src_refs, land_refs, refs[2 * n], refs[2 * n + 1]):
            cp.wait_send()
            cp.wait_recv()

    out = pl.pallas_call(
        body, name=name, out_shape=tuple(pltpu.HBM(a.shape, a.dtype) for a in srcs + lands),
        in_specs=[HBM] * (2 * n) + [SEM, SEM, ANY], out_specs=tuple([HBM] * (2 * n)),
        input_output_aliases={i: i for i in range(2 * n)},
        compiler_params=pltpu.CompilerParams(has_side_effects=SIDE_EFFECT),
    )(*srcs, *lands, send_sems, recv_sems, after)
    return list(out[:n]), list(out[n:])


def gather_plan(n):
    def plan(x, y, c):
        me = 4 * x + 2 * y + c
        peers = [(x, y, 1 - c)] + [(*chip, c) for chip in _other_chips(x, y)]
        return [(k, None, me, peer) for k in range(n) for peer in peers]
    return plan


def chips_plan(n):
    def plan(x, y, c):
        return [(k, 1 + j, j, (*chip, c)) for k in range(n) for j, chip in enumerate(_other_chips(x, y))]
    return plan


def gather_finish(name, gathered):
    n = len(gathered)

    def body(*refs):
        outs = refs[n:2 * n]
        send_sems, recv_sems = refs[2 * n:]
        x, y, c = _place()

        def passed_on(k, j, core):
            cx, cy = _other_chips(x, y)[j]
            blk = outs[k].at[4 * cx + 2 * cy + core]
            return pltpu.make_async_remote_copy(
                src_ref=blk, dst_ref=blk, send_sem=send_sems.at[3 * k + j], recv_sem=recv_sems.at[3 * k + j],
                device_id=(x, y, 1 - c), device_id_type=MESH)

        pairs = [(k, j) for k in range(n) for j in range(3)]
        sends = [passed_on(k, j, c) for k, j in pairs]
        for cp in sends:
            cp.start()
        for k, j in pairs:
            passed_on(k, j, 1 - c).wait_recv()
        for cp in sends:
            cp.wait_send()

    return pl.pallas_call(
        body, name=name, in_specs=[ANY] * n, out_specs=[ANY] * n,
        out_shape=[jax.ShapeDtypeStruct(g.shape, g.dtype) for g in gathered],
        input_output_aliases={k: k for k in range(n)},
        scratch_shapes=[pltpu.SemaphoreType.DMA((3 * n,)), pltpu.SemaphoreType.DMA((3 * n,))],
    )(*gathered)


ROW_TILE = 256
COL_TILE = 256


def _rms_fwd(tag, h, w):
    return rows_fwd(tag, _rms_f, [(h, D_MODEL, 0)], [w], [(D_MODEL, BF16)], ROW_TILE)[0]


def _rms_bwd(tag, h, w, dhn, dres):
    (dh,), (dw,) = rows_bwd(tag, _rms_f, [(h, D_MODEL, 0)], [w], [(dhn, D_MODEL, 0)], ROW_TILE, [F32], add=dres)
    return dh, dw


def even_fwd(tag, h, w, p):
    hn = _rms_fwd(tag + "_rms", h, p["nm"])
    uv = matmul(tag + "_uv", hn, w["uv"])
    z = matmul(tag + "_z", hn, w["z"])
    xbc = matmul(tag + "_xbc", hn, w["xbc"])
    pdt = matmul(tag + "_dt", hn, w["dt"])
    gm = [p["lng"], p["lnb"], p["ws"], p["bst"]]
    ya = rows_fwd(tag + "_gmlp", _gmlp_f, [(uv, 2 * D_MODEL, 0)], gm, [(D_MODEL, BF16)], GM_BLOCK)[0]
    xa = cols_fwd(tag + "_conv", _conv_silu_f, [(xbc, 0)], [p["cw"], p["cb"]], F32, COL_TILE)
    y, states = ssd_fwd(tag + "_ssd", xa, pdt, p["hp"])
    yb = rows_fwd(tag + "_gate", _gate_norm_f, [(y, D_MODEL, 0), (z, D_MODEL, 0)], [p["nw"]], [(D_MODEL, BF16)], ROW_TILE)[0]
    h1 = matmul(tag + "_out_b", yb, w["out_bot"], res=matmul(tag + "_out_a", ya, w["out_top"], res=h))
    return h1, dict(h=h, hn=hn, uv=uv, z=z, xbc=xbc, pdt=pdt, xa=xa, y=y, states=states, ya=ya, yb=yb)


def even_bwd(tag, dh1, s, w, p):
    dya = matmul(tag + "_dya", dh1, w["out_top"], tb=True)
    dyb = matmul(tag + "_dyb", dh1, w["out_bot"], tb=True)
    gw = dict(out_top=matmul(tag + "_gwa", s["ya"], dh1, ta=True, out_dtype=BF16),
              out_bot=matmul(tag + "_gwb", s["yb"], dh1, ta=True, out_dtype=BF16))
    (dy, dz), (dnw,) = rows_bwd(tag + "_gate_b", _gate_norm_f, [(s["y"], D_MODEL, 0), (s["z"], D_MODEL, 0)], [p["nw"]],
                                [(dyb, D_MODEL, 0)], ROW_TILE, [F32, BF16])
    dxs, dbm, dcm, dpdt, dhp = ssd_bwd(tag + "_ssd_b", s["xa"], s["pdt"], p["hp"], s["states"], dy)
    dxa = jnp.concatenate([dxs, dbm, dcm], axis=1)
    (dxbc,), (dcw, dcb) = cols_bwd(tag + "_conv_b", _conv_silu_f, [(s["xbc"], 0)], [p["cw"], p["cb"]], dxa, COL_TILE, BF16)
    gm = [p["lng"], p["lnb"], p["ws"], p["bst"]]
    (duv,), (dlng, dlnb, dws, dbst) = rows_bwd(tag + "_gmlp_b", _gmlp_f, [(s["uv"], 2 * D_MODEL, 0)], gm,
                                               [(dya, D_MODEL, 0)], GM_BLOCK, [BF16])
    dhn = None
    for key, d in (("uv", duv), ("z", dz), ("xbc", dxbc), ("dt", dpdt)):
        dhn = matmul(f"{tag}_dx_{key}", d, w[key], tb=True, res=dhn)
        gw[key] = matmul(f"{tag}_gw_{key}", s["hn"], d, ta=True, out_dtype=BF16)
    dh, dnm = _rms_bwd(tag + "_rms_b", s["h"], p["nm"], dhn, dh1)
    gp = dict(nm=dnm, lng=dlng, lnb=dlnb, ws=dws, bst=dbst, cw=dcw, cb=dcb, hp=dhp, nw=dnw)
    return dh, gw, gp


def odd_fwd(tag, h, w, p, cos, sin):
    hn = _rms_fwd(tag + "_rms", h, p["nm"])
    proj = matmul(tag + "_in", hn, w["in"])
    cq, ckv, kpe = rows_fwd(tag + "_qkvn", _qkv_norm_f, [(proj, ODD_IN_PAD, 0), (cos, LANES, 0), (sin, LANES, 0)],
                            [p["qn"], p["kvn"]], [(MLA_RANK, BF16), (MLA_RANK, BF16), (LANES, F32)], ROW_TILE)
    q = matmul(tag + "_q", cq, w["uq"])
    kn = matmul(tag + "_kn", ckv, w["kn"], out_dtype=BF16)
    v = matmul(tag + "_v", ckv, w["v"], out_dtype=BF16)
    o = attn_fwd(tag + "_attn", q, kn, kpe, v, cos, sin)
    h1 = matmul(tag + "_o", o, w["o"], res=h)
    return h1, dict(h=h, hn=hn, proj=proj, cq=cq, ckv=ckv, kpe=kpe, q=q, kn=kn, v=v, o=o)


def odd_bwd(tag, dh1, s, w, p, cos, sin):
    do = matmul(tag + "_do", dh1, w["o"], tb=True)
    gw = dict(o=matmul(tag + "_gw_o", s["o"], dh1, ta=True, out_dtype=BF16))
    dq, dkn, dkpe, dv = attn_bwd(tag + "_attn_b", s["q"], s["kn"], s["kpe"], s["v"], cos, sin, do)
    dcq = matmul(tag + "_dcq", dq, w["uq"], tb=True)
    gw["uq"] = matmul(tag + "_gw_uq", s["cq"], dq, ta=True, out_dtype=BF16)
    dckv = matmul(tag + "_dckv_v", dv, w["v"], tb=True, res=matmul(tag + "_dckv_k", dkn, w["kn"], tb=True))
    gw["kn"] = matmul(tag + "_gw_kn", s["ckv"], dkn, ta=True, out_dtype=BF16)
    gw["v"] = matmul(tag + "_gw_v", s["ckv"], dv, ta=True, out_dtype=BF16)
    (dproj,), (dqn, dkvn) = rows_bwd(
        tag + "_qkvn_b", _qkv_norm_f, [(s["proj"], ODD_IN_PAD, 0), (cos, LANES, 0), (sin, LANES, 0)], [p["qn"], p["kvn"]],
        [(dcq, MLA_RANK, 0), (dckv, MLA_RANK, 0), (dkpe, LANES, 0)], ROW_TILE, [BF16], n_nondiff=2)
    dhn = matmul(tag + "_dx_in", dproj, w["in"], tb=True)
    gw["in"] = matmul(tag + "_gw_in", s["hn"], dproj, ta=True, out_dtype=BF16)
    dh, dnm = _rms_bwd(tag + "_rms_b", s["h"], p["nm"], dhn, dh1)
    return dh, gw, dict(nm=dnm, qn=dqn, kvn=dkvn)


def ffn_fwd(tag, h, w, p):
    hn = _rms_fwd(tag + "_rms", h, p["nf"])
    g = matmul(tag + "_up_g", hn, w["up_g"])
    val = matmul(tag + "_up_v", hn, w["up_v"])
    act = cols_fwd(tag + "_act", _ffn_act_f, [(g, 0), (val, 0)], [p["fcw"], p["fcb"]], BF16, COL_TILE)
    h2 = matmul(tag + "_down", act, w["down"], res=h)
    return h2, dict(h=h, hn=hn, g=g, val=val, act=act)


def ffn_bwd(tag, dh2, s, w, p):
    dact = matmul(tag + "_dact", dh2, w["down"], tb=True)
    gw = dict(down=matmul(tag + "_gw_down", s["act"], dh2, ta=True, out_dtype=BF16))
    (dg, dval), (dfcw, dfcb) = cols_bwd(tag + "_act_b", _ffn_act_f, [(s["g"], 0), (s["val"], 0)], [p["fcw"], p["fcb"]],
                                        dact, COL_TILE, BF16)
    dhn = matmul(tag + "_dx_v", dval, w["up_v"], tb=True, res=matmul(tag + "_dx_g", dg, w["up_g"], tb=True))
    gw["up_g"] = matmul(tag + "_gw_up_g", s["hn"], dg, ta=True, out_dtype=BF16)
    gw["up_v"] = matmul(tag + "_gw_up_v", s["hn"], dval, ta=True, out_dtype=BF16)
    dh, dnf = _rms_bwd(tag + "_rms_b", s["h"], p["nf"], dhn, dh2)
    return dh, gw, dict(nf=dnf, fcw=dfcw, fcb=dfcb)


def _cols_from_slots(g):
    return jnp.moveaxis(g, 0, 1).reshape(g.shape[1], N_DEV * g.shape[2])


def _slots_from_cols(wmat):
    k, n = wmat.shape
    return jnp.moveaxis(wmat.reshape(k, N_DEV, n // N_DEV), 1, 0)


def _pad_last(a, width):
    return jnp.pad(a, [(0, 0)] * (a.ndim - 1) + [(0, width - a.shape[-1])])


def _heads_to_lanes(a):
    lead = a.shape[:-1]
    return _pad_last(a.reshape(lead + (SSD_STEPS, SSD_HEADS_PER_STEP)), LANES).reshape(lead + (SSD_STEPS * LANES,))


def _lanes_to_heads(a):
    lead = a.shape[:-1]
    return a.reshape(lead + (SSD_STEPS, LANES))[..., :SSD_HEADS_PER_STEP].reshape(lead + (SSM_HEADS,))


def prep_even(g_in, g_out):
    wn = _cols_from_slots(g_in)
    o1, o2, o3 = 2 * D_MODEL, 3 * D_MODEL, 3 * D_MODEL + SSM_CONV_DIM
    out = g_out.reshape(2 * D_MODEL, D_MODEL)
    return dict(uv=wn[:, :o1], z=wn[:, o1:o2], xbc=wn[:, o2:o3], dt=_heads_to_lanes(wn[:, o3:]),
                out_top=out[:D_MODEL], out_bot=out[D_MODEL:])


def unprep_even(gw):
    wn = jnp.concatenate([gw["uv"], gw["z"], gw["xbc"], _lanes_to_heads(gw["dt"])], axis=1)
    return _slots_from_cols(wn), jnp.concatenate([gw["out_top"], gw["out_bot"]], axis=0).reshape(N_DEV, -1, D_MODEL)


def prep_odd(g_in, g_uq, g_ukv, g_o):
    uq = _cols_from_slots(g_uq).reshape(MLA_RANK, MLA_HEADS, MLA_QK)
    ukv = _cols_from_slots(g_ukv).reshape(MLA_RANK, MLA_HEADS, MLA_NOPE + MLA_V)
    return dict(**{"in": _pad_last(g_in.reshape(D_MODEL, ODD_IN), ODD_IN_PAD)},
                uq=_pad_last(uq, MLA_QPAD).reshape(MLA_RANK, MLA_HEADS * MLA_QPAD),
                kn=ukv[:, :, :MLA_NOPE].reshape(MLA_RANK, MLA_HEADS * MLA_NOPE),
                v=ukv[:, :, MLA_NOPE:].reshape(MLA_RANK, MLA_HEADS * MLA_V),
                o=g_o.reshape(MLA_HEADS * MLA_V, D_MODEL))


def unprep_odd(gw):
    uq = gw["uq"].reshape(MLA_RANK, MLA_HEADS, MLA_QPAD)[:, :, :MLA_QK].reshape(MLA_RANK, MLA_HEADS * MLA_QK)
    ukv = jnp.concatenate([gw["kn"].reshape(MLA_RANK, MLA_HEADS, MLA_NOPE), gw["v"].reshape(MLA_RANK, MLA_HEADS, MLA_V)], axis=2)
    return (gw["in"][:, :ODD_IN].reshape(N_DEV, -1, ODD_IN), _slots_from_cols(uq),
            _slots_from_cols(ukv.reshape(MLA_RANK, -1)), gw["o"].reshape(N_DEV, -1, D_MODEL))


def prep_ffn(g_up, g_down):
    up = _cols_from_slots(g_up)
    return dict(up_g=up[:, :D_FF], up_v=up[:, D_FF:], down=g_down.reshape(D_FF, D_MODEL))


def unprep_ffn(gw):
    return _slots_from_cols(jnp.concatenate([gw["up_g"], gw["up_v"]], axis=1)), gw["down"].reshape(N_DEV, -1, D_MODEL)


SMALL_TILE = LANES * LANES


def _pack(arrs):
    flat = jnp.concatenate([a.reshape(-1).astype(F32) for a in arrs])
    size = -(-flat.shape[0] // SMALL_TILE) * SMALL_TILE
    return jnp.pad(flat, (0, size - flat.shape[0])).reshape(-1, LANES)


def _unpack(packed, shapes, lead=()):
    flat = packed.reshape(lead + (-1,))
    out, off = [], 0
    for shp in shapes:
        size = math.prod(shp)
        out.append(flat[..., off:off + size].reshape(lead + tuple(shp)))
        off += size
    return out


SMALL_SHARDED = {"ev_gm_ln_g": 2, "ev_gm_ln_b": 2, "ev_conv_w": 2, "od_q_norm": 1, "od_kv_norm": 1, "ff_conv_w": 2}
SMALL_REPLICATED = ["norm_mix", "norm_ffn", "norm_final", "ev_gm_ws", "ev_gm_bs", "ev_conv_b", "ev_dt_bias", "ev_a_log",
                    "ev_d_skip", "ev_ssm_norm_w", "ff_conv_b"]
MATRICES = {"ev_w_in": (2, 2048, 1156), "ev_w_out": (2, 512, 2048), "od_w_in": (2, 256, 1088), "od_w_uq": (2, 512, 384),
            "od_w_ukv": (2, 512, 512), "od_w_o": (2, 256, 2048), "ff_w_up": (4, 2048, 1408), "ff_w_down": (4, 704, 2048)}
WEIGHT_ORDER = ["norm_mix", "norm_ffn", "norm_final", "ev_w_in", "ev_gm_ln_g", "ev_gm_ln_b", "ev_gm_ws", "ev_gm_bs",
                "ev_conv_w", "ev_conv_b", "ev_dt_bias", "ev_a_log", "ev_d_skip", "ev_ssm_norm_w", "ev_w_out", "od_w_in",
                "od_q_norm", "od_kv_norm", "od_w_uq", "od_w_ukv", "od_w_o", "ff_w_up", "ff_conv_w", "ff_conv_b", "ff_w_down"]


def _full_from_shards(name, gathered):
    ax = SMALL_SHARDED[name]
    moved = jnp.moveaxis(gathered, 0, ax)
    shp = moved.shape
    return moved.reshape(shp[:ax] + (shp[ax] * shp[ax + 1],) + shp[ax + 2:])


def _my_shard(name, full, dev):
    ax = SMALL_SHARDED[name]
    shp = full.shape
    split = full.reshape(shp[:ax] + (N_DEV, shp[ax] // N_DEV) + shp[ax + 1:])
    return lax.dynamic_index_in_dim(split, dev, axis=ax, keepdims=False)


def _even_small(sm, j):
    row = lambda a: a.reshape(1, -1)
    hp = jnp.stack([sm["ev_dt_bias"][j], sm["ev_a_log"][j], sm["ev_d_skip"][j]])
    return dict(nm=row(sm["norm_mix"][2 * j]), lng=row(sm["ev_gm_ln_g"][j]), lnb=row(sm["ev_gm_ln_b"][j]),
                ws=sm["ev_gm_ws"][j], bst=sm["ev_gm_bs"][j].T, cw=sm["ev_conv_w"][j], cb=row(sm["ev_conv_b"][j]),
                hp=_heads_to_lanes(hp), nw=row(sm["ev_ssm_norm_w"][j]))


def _odd_small(sm, j):
    row = lambda a: a.reshape(1, -1)
    return dict(nm=row(sm["norm_mix"][2 * j + 1]), qn=row(sm["od_q_norm"][j]), kvn=row(sm["od_kv_norm"][j]))


def _ffn_small(sm, layer):
    row = lambda a: a.reshape(1, -1)
    return dict(nf=row(sm["norm_ffn"][layer]), fcw=sm["ff_conv_w"][layer], fcb=row(sm["ff_conv_b"][layer]))


def _rope_tables(positions):
    inv_freq = ROPE_THETA ** (-jnp.arange(0, MLA_ROPE, 2, dtype=F32) / MLA_ROPE)
    ang = positions.astype(F32).reshape(-1, 1) * inv_freq
    cos, sin = jnp.cos(ang), jnp.sin(ang)
    return _pad_last(jnp.concatenate([cos, cos], axis=1), LANES), _pad_last(jnp.concatenate([-sin, sin], axis=1), LANES)


def _tie(x, dep):
    return x if dep is None else lax.optimization_barrier((x, dep))[0]


def local_step(x, positions, target, sm, fetch_weights, emit_grads):
    cos, sin = _rope_tables(positions)
    h, saved = x, []
    for layer in range(4):
        j, tag = layer // 2, f"l{layer}"
        wm, dep = fetch_weights(2 * layer, h)
        h = _tie(h, dep)
        if layer % 2 == 0:
            pm = _even_small(sm, j)
            h, sv = even_fwd(tag, h, wm, pm)
        else:
            pm = _odd_small(sm, j)
            h, sv = odd_fwd(tag, h, wm, pm, cos, sin)
        wf, dep = fetch_weights(2 * layer + 1, h)
        h = _tie(h, dep)
        pf = _ffn_small(sm, layer)
        h, sf = ffn_fwd(tag + "f", h, wf, pf)
        saved.append((pm, sv, pf, sf, wm, wf))
    loss_tile, dh, dnfinal = final_loss("final_loss", h, sm["norm_final"].reshape(1, -1), target)
    gs = {k: [None] * v.shape[0] for k, v in sm.items() if k != "norm_final"}
    gs["norm_final"] = dnfinal.reshape(-1)
    for layer in reversed(range(4)):
        j, tag = layer // 2, f"l{layer}"
        pm, sv, pf, sf, wm, wf = saved[layer]
        dh, gwf, gpf = ffn_bwd(tag + "f", dh, sf, wf, pf)
        gs["norm_ffn"][layer], gs["ff_conv_w"][layer], gs["ff_conv_b"][layer] = gpf["nf"][0], gpf["fcw"], gpf["fcb"][0]
        dh = _tie(dh, emit_grads(2 * layer + 1, gwf, dh))
        if layer % 2 == 0:
            dh, gwm, gp = even_bwd(tag, dh, sv, wm, pm)
            hp = _lanes_to_heads(gp["hp"])
            gs["norm_mix"][layer] = gp["nm"][0]
            gs["ev_gm_ln_g"][j], gs["ev_gm_ln_b"][j] = gp["lng"].reshape(GM_GROUPS, -1), gp["lnb"].reshape(GM_GROUPS, -1)
            gs["ev_gm_ws"][j], gs["ev_gm_bs"][j] = gp["ws"], gp["bst"].T
            gs["ev_conv_w"][j], gs["ev_conv_b"][j] = gp["cw"], gp["cb"][0]
            gs["ev_dt_bias"][j], gs["ev_a_log"][j], gs["ev_d_skip"][j] = hp[0], hp[1], hp[2]
            gs["ev_ssm_norm_w"][j] = gp["nw"][0]
        else:
            dh, gwm, gp = odd_bwd(tag, dh, sv, wm, pm, cos, sin)
            gs["norm_mix"][layer] = gp["nm"][0]
            gs["od_q_norm"][j], gs["od_kv_norm"][j] = gp["qn"][0], gp["kvn"][0]
        if layer > 0:
            dh = _tie(dh, emit_grads(2 * layer, gwm, dh))
    gs = {k: (v if k == "norm_final" else jnp.stack(v)) for k, v in gs.items()}
    return loss_tile[0, 0], dh, gs, gwm


def kernel(x, positions, norm_mix, norm_ffn, norm_final, ev_w_in, ev_gm_ln_g, ev_gm_ln_b, ev_gm_ws, ev_gm_bs, ev_conv_w, ev_conv_b, ev_dt_bias, ev_a_log, ev_d_skip, ev_ssm_norm_w, ev_w_out, od_w_in, od_q_norm, od_kv_norm, od_w_uq, od_w_ukv, od_w_o, ff_w_up, ff_conv_w, ff_conv_b, ff_w_down, loss_target, m_norm_mix, m_norm_ffn, m_norm_final, m_ev_w_in, m_ev_gm_ln_g, m_ev_gm_ln_b, m_ev_gm_ws, m_ev_gm_bs, m_ev_conv_w, m_ev_conv_b, m_ev_dt_bias, m_ev_a_log, m_ev_d_skip, m_ev_ssm_norm_w, m_ev_w_out, m_od_w_in, m_od_q_norm, m_od_kv_norm, m_od_w_uq, m_od_w_ukv, m_od_w_o, m_ff_w_up, m_ff_conv_w, m_ff_conv_b, m_ff_w_down, v_norm_mix, v_norm_ffn, v_norm_final, v_ev_w_in, v_ev_gm_ln_g, v_ev_gm_ln_b, v_ev_gm_ws, v_ev_gm_bs, v_ev_conv_w, v_ev_conv_b, v_ev_dt_bias, v_ev_a_log, v_ev_d_skip, v_ev_ssm_norm_w, v_ev_w_out, v_od_w_in, v_od_q_norm, v_od_kv_norm, v_od_w_uq, v_od_w_ukv, v_od_w_o, v_ff_w_up, v_ff_conv_w, v_ff_conv_b, v_ff_w_down):
    args = dict(locals())
    wts = {n: args[n] for n in WEIGHT_ORDER}
    mom = {n: args["m_" + n] for n in WEIGHT_ORDER}
    var = {n: args["v_" + n] for n in WEIGHT_ORDER}
    dev = _device_slot()

    small_names = list(SMALL_SHARDED)
    small_shapes = [wts[n].shape for n in small_names]
    (small_all,) = all_gather("ag_small", [_pack([wts[n] for n in small_names])])
    small_full = _unpack(small_all, small_shapes, lead=(N_DEV,))
    sm = {n: _full_from_shards(n, g) for n, g in zip(small_names, small_full)}
    sm.update({n: wts[n] for n in SMALL_REPLICATED})

    bf = {n: wts[n].astype(BF16) for n in MATRICES}

    def stage_matrices(stage):
        layer, is_ffn = divmod(stage, 2)
        if is_ffn:
            return [("ff_w_up", layer), ("ff_w_down", layer)]
        return [(n, layer // 2) for n in (["ev_w_in", "ev_w_out"] if layer % 2 == 0 else ["od_w_in", "od_w_uq", "od_w_ukv", "od_w_o"])]

    def stage_fns(stage):
        layer, is_ffn = divmod(stage, 2)
        if is_ffn:
            return prep_ffn, unprep_ffn
        return (prep_even, unprep_even) if layer % 2 == 0 else (prep_odd, unprep_odd)

    n_stages, ahead = 8, 2

    def start_gather(stage, earlier=None):
        shards = [bf[n][i] for n, i in stage_matrices(stage)]
        if earlier is not None:
            shards, _ = lax.optimization_barrier((shards, earlier))
        return split_start(f"ag_s{stage}_start", shards, [(N_DEV,) + s.shape for s in shards],
                           gather_plan(len(shards)), 4 * len(shards))

    gathers = {}
    for stage in range(ahead):
        gathers[stage] = start_gather(stage, gathers[stage - 1][4] if stage else None)

    def fetch_weights(stage, h):
        plan = gather_plan(len(stage_matrices(stage)))
        shards, landed = split_wait(f"ag_s{stage}_wait", gathers.pop(stage), plan, h)
        g = gather_finish(f"ag_s{stage}_finish", landed)
        g = [lax.dynamic_update_index_in_dim(gk, sk, dev, 0) for gk, sk in zip(g, shards)]
        started = None
        if stage + ahead < n_stages:
            gathers[stage + ahead] = start_gather(stage + ahead, g[0])
            started = gathers[stage + ahead][4]
        return stage_fns(stage)[0](*g), started

    scatters = []
    out = {n: None for n in MATRICES}

    def finish_scatter(after):
        stage, handle = scatters.pop(0)
        mats = stage_matrices(stage)
        sums, recv = split_wait(f"rs_s{stage}_wait", handle, chips_plan(len(mats)), after)
        for (n, i), p_, r_ in zip(mats, sums, recv):
            layers, rows, cols = MATRICES[n]
            two_d = lambda a: a.reshape(layers * rows, cols)
            out[n] = adamw(f"adamw_{n}_{i}", [(p_, 0), (r_, 0), (r_, 1), (r_, 2)], two_d(wts[n]), two_d(mom[n]),
                           two_d(var[n]), tr=_row_tile(rows), part=i, prev=out[n])

    def emit_grads(stage, gw, dh):
        if len(scatters) >= ahead:
            finish_scatter(dh)
        send = list(stage_fns(stage)[1](gw))
        from_sibling = exchange_sibling(f"rs_s{stage}_sibling", send)
        sums = [chip_sums(f"rs_s{stage}_add{k}", g, r, tr=_row_tile(g.shape[1])) for k, (g, r) in enumerate(zip(send, from_sibling))]
        handle = split_start(f"rs_s{stage}_start", sums, [(3,) + s.shape[1:] for s in sums], chips_plan(len(sums)), 3 * len(sums))
        scatters.append((stage, handle))
        return handle[4]

    loss_local, dx, gs, gw_first = local_step(x[0], positions[0], loss_target[0], sm, fetch_weights, emit_grads)
    loss = lax.psum(loss_local, ("x", "y", "c"))

    all_small = small_names + SMALL_REPLICATED
    (partials,) = all_gather("ar_small", [_pack([gs[n] for n in all_small])])
    emit_grads(0, {k: _tie(v, partials) for k, v in gw_first.items()}, partials)
    total = sum_parts("ar_small_sum", [(partials, s) for s in range(N_DEV)])
    g_full = dict(zip(all_small, _unpack(total, [gs[n].shape for n in all_small])))
    g_mine = {n: (_my_shard(n, g_full[n], dev) if n in SMALL_SHARDED else g_full[n]) for n in all_small}
    packed = [_pack([d[n] for n in all_small]) for d in (g_mine, wts, mom, var)]
    res = adamw("adamw_small", [(packed[0][None], 0)], packed[1], packed[2], packed[3])
    unpacked = [_unpack(a, [wts[n].shape for n in all_small]) for a in res]
    for i, n in enumerate(all_small):
        out[n] = [u[i] for u in unpacked]
    while scatters:
        finish_scatter(res[0])
    for n in MATRICES:
        out[n] = [a.reshape(wts[n].shape) for a in out[n]]

    return (loss, dx[None], *[out[n][0] for n in WEIGHT_ORDER], *[out[n][1] for n in WEIGHT_ORDER],
            *[out[n][2] for n in WEIGHT_ORDER], *[out[n][3] for n in WEIGHT_ORDER])
```

```python
import functools
import math

import jax
import jax.numpy as jnp
from jax import lax
from jax.experimental import pallas as pl
from jax.experimental.pallas import tpu as pltpu

F32 = jnp.float32
BF16 = jnp.bfloat16
MESH = pl.DeviceIdType.MESH

V7X_VMEM_LIMIT_BYTES = 56 * 1024 * 1024
LANES = 128

EPS = 1e-6
D_MODEL = 2048
CHUNK = 64
GM_BLOCK = 128
GM_GROUPS = 8
GM_GROUP_DIM = D_MODEL // GM_GROUPS
SSM_HEADS = 32
SSM_HEAD_DIM = 64
SSM_GROUPS = 4
SSM_STATE = 128
SSM_CONV = 4
SSM_BC = SSM_GROUPS * SSM_STATE
SSM_CONV_DIM = D_MODEL + 2 * SSM_BC
SSD_HEADS_PER_STEP = 4
SSD_STEPS = SSM_HEADS // SSD_HEADS_PER_STEP
SSD_X_WIDTH = SSD_HEADS_PER_STEP * SSM_HEAD_DIM
MLA_HEADS = 16
MLA_RANK = 512
MLA_NOPE = 128
MLA_ROPE = 64
MLA_V = 128
MLA_QK = MLA_NOPE + MLA_ROPE
MLA_QPAD = 2 * LANES
ODD_IN = 2 * MLA_RANK + MLA_ROPE
ODD_IN_PAD = 2 * MLA_RANK + LANES
D_FF = 5632
ROPE_THETA = 10000.0
N_DEV = 8

ADAM_LR, ADAM_B1, ADAM_B2, ADAM_EPS, ADAM_WD, ADAM_STEP = 0.001, 0.9, 0.999, 1e-08, 0.01, 10


def _params(*sem):
    return pltpu.CompilerParams(dimension_semantics=sem, vmem_limit_bytes=V7X_VMEM_LIMIT_BYTES)


def _pick(dim, target):
    if dim <= target:
        return dim
    t = (target // LANES) * LANES
    while t >= LANES:
        if dim % t == 0:
            return t
        t -= LANES
    raise ValueError(f"no tile for {dim} under {target}")


MATMUL_VMEM_BUDGET = 32 * 1024 * 1024


def _matmul_tiles(m, n, k, a_bytes, b_bytes, o_bytes, has_res, ta):
    def fits(tm, tn):
        per_out = o_bytes + (4 if has_res else 0)
        return 2 * (tm * k * a_bytes + tn * k * b_bytes + tm * tn * per_out) <= MATMUL_VMEM_BUDGET

    tns = (2048, 1024, 512, 256, 128) if ta else (512, 256, 128)
    tms = (512, 256, 128) if ta else (2048, 1024, 512, 256, 128)
    for tn in tns:
        tn = _pick(n, tn)
        for tm in tms:
            tm = _pick(m, tm)
            if fits(tm, tn):
                return tm, tn
    raise ValueError(f"no matmul tiles for {m}x{n}x{k}")


def matmul(name, a, b, *, ta=False, tb=False, res=None, out_dtype=F32, after=None):
    m, k = (a.shape[1], a.shape[0]) if ta else a.shape
    n = b.shape[0] if tb else b.shape[1]
    assert k == (b.shape[1] if tb else b.shape[0]), (name, a.shape, b.shape)
    tm, tn = _matmul_tiles(m, n, k, a.dtype.itemsize, b.dtype.itemsize, jnp.dtype(out_dtype).itemsize, res is not None, ta)
    dims = (((0 if ta else 1,), (1 if tb else 0,)), ((), ()))

    def body(*refs):
        a_ref, b_ref, o_ref = refs[0], refs[1], refs[-1]
        total = lax.dot_general(a_ref[...].astype(BF16), b_ref[...].astype(BF16), dims, preferred_element_type=F32)
        if res is not None:
            total = total + refs[2][...]
        o_ref[...] = total.astype(o_ref.dtype)

    a_spec = pl.BlockSpec((k, tm), lambda i, j: (0, i)) if ta else pl.BlockSpec((tm, k), lambda i, j: (i, 0))
    b_spec = pl.BlockSpec((tn, k), lambda i, j: (j, 0)) if tb else pl.BlockSpec((k, tn), lambda i, j: (0, j))
    o_spec = pl.BlockSpec((tm, tn), lambda i, j: (i, j))
    ins, specs = [a, b], [a_spec, b_spec]
    if res is not None:
        ins.append(res)
        specs.append(o_spec)
    if after is not None:
        ins.append(after)
        specs.append(pl.BlockSpec(memory_space=pl.ANY))
    return pl.pallas_call(
        body, name=name, grid=(m // tm, n // tn), in_specs=specs, out_specs=o_spec,
        out_shape=jax.ShapeDtypeStruct((m, n), out_dtype),
        compiler_params=_params("parallel", "parallel"),
    )(*ins)


@functools.partial(jax.custom_vjp, nondiff_argnums=(1, 2))
def _roll(x, shift, axis):
    return pltpu.roll(x, shift, axis)


def _roll_fwd(x, shift, axis):
    return pltpu.roll(x, shift, axis), None


def _roll_bwd(shift, axis, _, g):
    return (pltpu.roll(g, (g.shape[axis] - shift) % g.shape[axis], axis),)


_roll.defvjp(_roll_fwd, _roll_bwd)


def _shift_down(x, s):
    rows = lax.broadcasted_iota(jnp.int32, x.shape, 0)
    return jnp.where(rows >= s, _roll(x, s, 0), 0.0)


def _dwconv(x, w, b):
    taps = w.shape[0]
    y = b + w[taps - 1:taps, :] * x
    for kk in range(taps - 1):
        y = y + w[kk:kk + 1, :] * _shift_down(x, taps - 1 - kk)
    return y


def _rms(x, w):
    return x * lax.rsqrt(jnp.mean(x * x, -1, keepdims=True) + EPS) * w


def _rms_f(h, w):
    return (_rms(h, w),)


def _gmlp_f(uv, lng, lnb, ws, bst):
    r = lax.broadcasted_iota(jnp.int32, (GM_BLOCK, GM_BLOCK), 0) // CHUNK
    c = lax.broadcasted_iota(jnp.int32, (GM_BLOCK, GM_BLOCK), 1) // CHUNK
    outs = []
    for g in range(GM_GROUPS):
        lo, hi = g * GM_GROUP_DIM, (g + 1) * GM_GROUP_DIM
        gu = jax.nn.gelu(uv[:, lo:hi])
        gv = jax.nn.gelu(uv[:, D_MODEL + lo:D_MODEL + hi])
        xc = gv - jnp.mean(gv, -1, keepdims=True)
        var = jnp.mean(xc * xc, -1, keepdims=True)
        vn = xc * lax.rsqrt(var + EPS) * lng[:, lo:hi] + lnb[:, lo:hi]
        wm = jnp.where(r >= c, ws[g], 0.0).astype(BF16)
        gate = jnp.dot(wm, vn.astype(BF16), preferred_element_type=F32) + bst[:, g:g + 1]
        outs.append(gu * gate)
    return (jnp.concatenate(outs, axis=1),)


def _conv_silu_f(x, w, b):
    return (jax.nn.silu(_dwconv(x, w, b)),)


def _ffn_act_f(g, val, w, b):
    return (jax.nn.gelu(_dwconv(g, w, b)) * val,)


def _gate_norm_f(y, z, nw):
    y2 = y * jax.nn.silu(z)
    width = D_MODEL // SSM_GROUPS
    outs = []
    for g in range(SSM_GROUPS):
        blk = y2[:, g * width:(g + 1) * width]
        outs.append(blk * lax.rsqrt(jnp.mean(blk * blk, -1, keepdims=True) + EPS))
    return (jnp.concatenate(outs, axis=1) * nw,)


def _rope(x, cos, sin):
    lane = lax.broadcasted_iota(jnp.int32, x.shape, 1)
    half = MLA_ROPE // 2
    swapped = jnp.where(lane < half, _roll(x, LANES - half, 1), _roll(x, half, 1))
    return x * cos + swapped * sin


def _qkv_norm_f(proj, cos, sin, qn, kvn):
    cq = _rms(proj[:, :MLA_RANK], qn)
    ckv = _rms(proj[:, MLA_RANK:2 * MLA_RANK], kvn)
    kpe = _rope(proj[:, 2 * MLA_RANK:], cos, sin)
    return cq, ckv, kpe


def _attn_scores(q0, k0, qh, kn, kpe, cos, sin):
    qn = qh[:, :MLA_NOPE]
    qp = _rope(qh[:, MLA_NOPE:], cos, sin)
    nt = (((1,), (1,)), ((), ()))
    s = lax.dot_general(qn.astype(BF16), kn.astype(BF16), nt, preferred_element_type=F32)
    s = s + lax.dot_general(qp.astype(BF16), kpe.astype(BF16), nt, preferred_element_type=F32)
    s = s * (MLA_QK ** -0.5)
    qc = (q0 + lax.broadcasted_iota(jnp.int32, s.shape, 0)) // CHUNK
    kc = (k0 + lax.broadcasted_iota(jnp.int32, s.shape, 1)) // CHUNK
    return jnp.where(kc <= qc, s, -jnp.inf)


def _ssd_chunk_f(x, bm, cm, pdt, hp, sprev):
    nh = SSD_HEADS_PER_STEP
    dt = jax.nn.softplus(pdt[:, :nh] + hp[0:1, :nh])
    da = dt * (-jnp.exp(hp[1:2, :nh]))
    r = lax.broadcasted_iota(jnp.int32, (CHUNK, CHUNK), 0)
    c = lax.broadcasted_iota(jnp.int32, (CHUNK, CHUNK), 1)
    tril = r >= c
    cs = jnp.dot(tril.astype(F32), da, precision=lax.Precision.HIGHEST, preferred_element_type=F32)
    cst = cs.T
    nt = (((1,), (1,)), ((), ()))
    tn = (((0,), (0,)), ((), ()))
    cb = lax.dot_general(cm.astype(BF16), bm.astype(BF16), nt, preferred_element_type=F32)
    ys, snew = [], []
    for e in range(nh):
        xe = x[:, e * SSM_HEAD_DIM:(e + 1) * SSM_HEAD_DIM]
        xd = xe * dt[:, e:e + 1]
        cse = cs[:, e:e + 1]
        decay = jnp.exp(jnp.where(tril, cse - cst[e:e + 1, :], -jnp.inf))
        y = jnp.dot((cb * decay).astype(BF16), xd.astype(BF16), preferred_element_type=F32)
        tot = cse[CHUNK - 1:CHUNK, :]
        st = lax.dot_general((xd * jnp.exp(tot - cse)).astype(BF16), bm.astype(BF16), tn, preferred_element_type=F32)
        yoff = lax.dot_general(cm.astype(BF16), sprev[e].astype(BF16), nt, preferred_element_type=F32)
        ys.append(y + yoff * jnp.exp(cse) + hp[2:3, e:e + 1] * xe)
        snew.append((jnp.exp(tot) * sprev[e] + st)[None])
    return jnp.concatenate(ys, axis=1), jnp.concatenate(snew, axis=0)


def _full_spec(a):
    nd = a.ndim
    return pl.BlockSpec(a.shape, lambda i, nd=nd: (0,) * nd)


def rows_fwd(name, f, rows, params, outs, tr, after=None):
    t = rows[0][0].shape[0]
    nr, npar = len(rows), len(params)
    extra = [] if after is None else [after]

    def body(*refs):
        vals = f(*[x[...].astype(F32) for x in refs[:nr + npar]])
        for o_ref, val in zip(refs[nr + npar + len(extra):], vals):
            o_ref[...] = val.astype(o_ref.dtype)

    in_specs = [pl.BlockSpec((tr, w), lambda i, cb=cb: (i, cb)) for _, w, cb in rows] + [_full_spec(p) for p in params]
    in_specs += [pl.BlockSpec(memory_space=pl.ANY)] * len(extra)
    out = pl.pallas_call(
        body, name=name, grid=(t // tr,), in_specs=in_specs,
        out_specs=[pl.BlockSpec((tr, w), lambda i: (i, 0)) for w, _ in outs],
        out_shape=[jax.ShapeDtypeStruct((t, w), dt) for w, dt in outs],
        compiler_params=_params("parallel"),
    )(*[a for a, _, _ in rows], *params, *extra)
    return out


def rows_bwd(name, f, rows, params, cots, tr, d_dtypes, n_nondiff=0, add=None):
    t = rows[0][0].shape[0]
    nr, npar, nc = len(rows), len(params), len(cots)
    nd = nr - n_nondiff
    has_add = add is not None

    def body(*refs):
        i = pl.program_id(0)
        row_vals = [x[...].astype(F32) for x in refs[:nr]]
        par_vals = [x[...].astype(F32) for x in refs[nr:nr + npar]]
        cot_refs = refs[nr + npar:nr + npar + nc]
        pos = nr + npar + nc
        add_ref = refs[pos] if has_add else None
        pos += int(has_add)
        drow_refs = refs[pos:pos + nd]
        dpar_refs = refs[pos + nd:]

        def g(*diff):
            return f(*diff[:nd], *row_vals[nd:], *diff[nd:])

        _, vjp = jax.vjp(g, *row_vals[:nd], *par_vals)
        grads = vjp(tuple(cr[...].astype(F32) for cr in cot_refs))
        for j, d_ref in enumerate(drow_refs):
            val = grads[j]
            if j == 0 and has_add:
                val = val + add_ref[...]
            d_ref[...] = val.astype(d_ref.dtype)
        for j, d_ref in enumerate(dpar_refs):
            @pl.when(i == 0)
            def _(d_ref=d_ref, j=j):
                d_ref[...] = grads[nd + j]

            @pl.when(i > 0)
            def _(d_ref=d_ref, j=j):
                d_ref[...] += grads[nd + j]

    in_specs = [pl.BlockSpec((tr, w), lambda i, cb=cb: (i, cb)) for _, w, cb in rows] + [_full_spec(p) for p in params]
    in_specs += [pl.BlockSpec((tr, w), lambda i, cb=cb: (i, cb)) for _, w, cb in cots]
    ins = [a for a, _, _ in rows] + list(params) + [a for a, _, _ in cots]
    if has_add:
        in_specs.append(pl.BlockSpec((tr, rows[0][1]), lambda i: (i, 0)))
        ins.append(add)
    out_specs = [pl.BlockSpec((tr, rows[j][1]), lambda i: (i, 0)) for j in range(nd)] + [_full_spec(p) for p in params]
    out_shape = [jax.ShapeDtypeStruct((t, rows[j][1]), d_dtypes[j]) for j in range(nd)]
    out_shape += [jax.ShapeDtypeStruct(p.shape, F32) for p in params]
    out = pl.pallas_call(
        body, name=name, grid=(t // tr,), in_specs=in_specs, out_specs=out_specs, out_shape=out_shape,
        compiler_params=_params("arbitrary"),
    )(*ins)
    return out[:nd], out[nd:]


def cols_fwd(name, f, cols, cparams, out_dtype, tc):
    t = cols[0][0].shape[0]
    width = cparams[0].shape[1]
    ncol = len(cols)

    def body(*refs):
        (val,) = f(*[x[...].astype(F32) for x in refs[:-1]])
        refs[-1][...] = val.astype(refs[-1].dtype)

    in_specs = [pl.BlockSpec((t, tc), lambda j, o=o: (0, o + j)) for _, o in cols]
    in_specs += [pl.BlockSpec((p.shape[0], tc), lambda j: (0, j)) for p in cparams]
    return pl.pallas_call(
        body, name=name, grid=(width // tc,), in_specs=in_specs,
        out_specs=pl.BlockSpec((t, tc), lambda j: (0, j)),
        out_shape=jax.ShapeDtypeStruct((t, width), out_dtype),
        compiler_params=_params("parallel"),
    )(*[a for a, _ in cols], *cparams)


def cols_bwd(name, f, cols, cparams, cot, tc, d_dtype):
    t = cols[0][0].shape[0]
    width = cparams[0].shape[1]
    ncol, npar = len(cols), len(cparams)

    def body(*refs):
        vals = [x[...].astype(F32) for x in refs[:ncol + npar]]
        _, vjp = jax.vjp(f, *vals)
        grads = vjp((refs[ncol + npar][...].astype(F32),))
        for d_ref, gval in zip(refs[ncol + npar + 1:], grads):
            d_ref[...] = gval.astype(d_ref.dtype)

    in_specs = [pl.BlockSpec((t, tc), lambda j, o=o: (0, o + j)) for _, o in cols]
    in_specs += [pl.BlockSpec((p.shape[0], tc), lambda j: (0, j)) for p in cparams]
    in_specs.append(pl.BlockSpec((t, tc), lambda j: (0, j)))
    out_specs = [pl.BlockSpec((t, tc), lambda j: (0, j)) for _ in cols]
    out_specs += [pl.BlockSpec((p.shape[0], tc), lambda j: (0, j)) for p in cparams]
    out_shape = [jax.ShapeDtypeStruct((t, width), d_dtype) for _ in cols]
    out_shape += [jax.ShapeDtypeStruct(p.shape, F32) for p in cparams]
    out = pl.pallas_call(
        body, name=name, grid=(width // tc,), in_specs=in_specs, out_specs=out_specs, out_shape=out_shape,
        compiler_params=_params("parallel"),
    )(*[a for a, _ in cols], *cparams, cot)
    return out[:ncol], out[ncol:]


def _ssd_in_specs(t):
    heads_per_group = SSM_HEADS // SSM_GROUPS
    steps_per_group = heads_per_group // SSD_HEADS_PER_STEP
    b_blk = D_MODEL // LANES
    c_blk = (D_MODEL + SSM_BC) // LANES
    return [
        pl.BlockSpec((t, SSD_X_WIDTH), lambda s: (0, s)),
        pl.BlockSpec((t, LANES), lambda s: (0, b_blk + s // steps_per_group)),
        pl.BlockSpec((t, LANES), lambda s: (0, c_blk + s // steps_per_group)),
        pl.BlockSpec((t, LANES), lambda s: (0, s)),
        pl.BlockSpec((3, LANES), lambda s: (0, s)),
    ]


def ssd_fwd(name, xa, pdt, hp):
    t = xa.shape[0]
    nc = t // CHUNK
    nh = SSD_HEADS_PER_STEP

    def body(x_ref, b_ref, c_ref, pdt_ref, hp_ref, y_ref, st_ref, s_scr):
        s_scr[...] = jnp.zeros_like(s_scr)

        def step(ci, carry):
            sl = pl.ds(pl.multiple_of(ci * CHUNK, CHUNK), CHUNK)
            sprev = s_scr[...]
            st_ref[0, ci] = sprev
            y, snew = _ssd_chunk_f(x_ref[sl, :], b_ref[sl, :], c_ref[sl, :], pdt_ref[sl, :], hp_ref[...], sprev)
            y_ref[sl, :] = y
            s_scr[...] = snew
            return carry

        lax.fori_loop(0, nc, step, 0)

    return pl.pallas_call(
        body, name=name, grid=(SSD_STEPS,), in_specs=_ssd_in_specs(t),
        out_specs=[pl.BlockSpec((t, SSD_X_WIDTH), lambda s: (0, s)),
                   pl.BlockSpec((1, nc, nh, SSM_HEAD_DIM, SSM_STATE), lambda s: (s, 0, 0, 0, 0))],
        out_shape=[jax.ShapeDtypeStruct((t, D_MODEL), F32),
                   jax.ShapeDtypeStruct((SSD_STEPS, nc, nh, SSM_HEAD_DIM, SSM_STATE), F32)],
        scratch_shapes=[pltpu.VMEM((nh, SSM_HEAD_DIM, SSM_STATE), F32)],
        compiler_params=_params("parallel"),
    )(xa, xa, xa, pdt, hp)


def ssd_bwd(name, xa, pdt, hp, states, dy):
    t = xa.shape[0]
    nc = t // CHUNK
    nh = SSD_HEADS_PER_STEP
    steps_per_group = SSM_HEADS // SSM_GROUPS // nh

    def body(x_ref, b_ref, c_ref, pdt_ref, hp_ref, st_ref, dy_ref, dx_ref, db_ref, dc_ref, dpdt_ref, dhp_ref, ds_scr, dhp_scr):
        first = pl.program_id(0) % steps_per_group == 0
        ds_scr[...] = jnp.zeros_like(ds_scr)
        dhp_scr[...] = jnp.zeros_like(dhp_scr)

        def step(i, carry):
            ci = nc - 1 - i
            sl = pl.ds(pl.multiple_of(ci * CHUNK, CHUNK), CHUNK)
            _, vjp = jax.vjp(_ssd_chunk_f, x_ref[sl, :], b_ref[sl, :], c_ref[sl, :], pdt_ref[sl, :], hp_ref[...], st_ref[0, ci])
            dx, db, dc, dpdt, dhp, dsprev = vjp((dy_ref[sl, :], ds_scr[...]))
            dx_ref[sl, :] = dx
            dpdt_ref[sl, :] = dpdt.astype(dpdt_ref.dtype)

            @pl.when(first)
            def _():
                db_ref[sl, :] = db
                dc_ref[sl, :] = dc

            @pl.when(jnp.logical_not(first))
            def _():
                db_ref[sl, :] += db
                dc_ref[sl, :] += dc

            ds_scr[...] = dsprev
            dhp_scr[...] += dhp
            return carry

        lax.fori_loop(0, nc, step, 0)
        dhp_ref[...] = dhp_scr[...]

    in_specs = _ssd_in_specs(t) + [
        pl.BlockSpec((1, nc, nh, SSM_HEAD_DIM, SSM_STATE), lambda s: (s, 0, 0, 0, 0)),
        pl.BlockSpec((t, SSD_X_WIDTH), lambda s: (0, s)),
    ]
    out_specs = [
        pl.BlockSpec((t, SSD_X_WIDTH), lambda s: (0, s)),
        pl.BlockSpec((t, LANES), lambda s: (0, s // steps_per_group)),
        pl.BlockSpec((t, LANES), lambda s: (0, s // steps_per_group)),
        pl.BlockSpec((t, LANES), lambda s: (0, s)),
        pl.BlockSpec((3, LANES), lambda s: (0, s)),
    ]
    out_shape = [
        jax.ShapeDtypeStruct((t, D_MODEL), F32),
        jax.ShapeDtypeStruct((t, SSM_BC), F32),
        jax.ShapeDtypeStruct((t, SSM_BC), F32),
        jax.ShapeDtypeStruct((t, SSD_STEPS * LANES), BF16),
        jax.ShapeDtypeStruct((3, SSD_STEPS * LANES), F32),
    ]
    return pl.pallas_call(
        body, name=name, grid=(SSD_STEPS,), in_specs=in_specs, out_specs=out_specs, out_shape=out_shape,
        scratch_shapes=[pltpu.VMEM((nh, SSM_HEAD_DIM, SSM_STATE), F32), pltpu.VMEM((3, LANES), F32)],
        compiler_params=_params("arbitrary"),
    )(xa, xa, xa, pdt, hp, states, dy)


ATTN_TQ = 256
ATTN_KSTEP = 512


def _attn_extents(t):
    return [min(t, (g + 1) * ATTN_KSTEP) for g in range(-(-t // ATTN_KSTEP))]


def _attn_f(q0, qh, kn, kpe, v, cos, sin):
    p = jax.nn.softmax(_attn_scores(q0, 0, qh, kn, kpe, cos, sin), axis=-1)
    return (jnp.dot(p.astype(BF16), v.astype(BF16), preferred_element_type=F32),)


def _attn_in_specs(t):
    return [
        pl.BlockSpec((ATTN_TQ, MLA_QPAD), lambda h, qi: (qi, h)),
        pl.BlockSpec((t, MLA_NOPE), lambda h, qi: (0, h)),
        pl.BlockSpec((t, LANES), lambda h, qi: (0, 0)),
        pl.BlockSpec((t, MLA_V), lambda h, qi: (0, h)),
        pl.BlockSpec((ATTN_TQ, LANES), lambda h, qi: (qi, 0)),
        pl.BlockSpec((ATTN_TQ, LANES), lambda h, qi: (qi, 0)),
    ]


def attn_fwd(name, q, kn, kpe, v, cos, sin):
    t = q.shape[0]

    def body(q_ref, kn_ref, kpe_ref, v_ref, cos_ref, sin_ref, o_ref):
        qi = pl.program_id(1)
        for span, ext in enumerate(_attn_extents(t)):
            @pl.when(qi // (ATTN_KSTEP // ATTN_TQ) == span)
            def _(ext=ext):
                (o,) = _attn_f(qi * ATTN_TQ, q_ref[...], kn_ref[0:ext, :], kpe_ref[0:ext, :], v_ref[0:ext, :],
                               cos_ref[...], sin_ref[...])
                o_ref[...] = o.astype(o_ref.dtype)

    return pl.pallas_call(
        body, name=name, grid=(MLA_HEADS, t // ATTN_TQ), in_specs=_attn_in_specs(t),
        out_specs=pl.BlockSpec((ATTN_TQ, MLA_V), lambda h, qi: (qi, h)),
        out_shape=jax.ShapeDtypeStruct((t, MLA_HEADS * MLA_V), BF16),
        compiler_params=_params("parallel", "parallel"),
    )(q, kn, kpe, v, cos, sin)


def attn_bwd(name, q, kn, kpe, v, cos, sin, do):
    t = q.shape[0]

    def body(q_ref, kn_ref, kpe_ref, v_ref, cos_ref, sin_ref, do_ref, dq_ref, dkn_ref, dkpe_ref, dv_ref):
        h, qi = pl.program_id(0), pl.program_id(1)
        q0 = qi * ATTN_TQ
        cos, sin = cos_ref[...], sin_ref[...]

        @pl.when(qi == 0)
        def _():
            dkn_ref[...] = jnp.zeros_like(dkn_ref)
            dv_ref[...] = jnp.zeros_like(dv_ref)

        @pl.when(jnp.logical_and(h == 0, qi == 0))
        def _():
            dkpe_ref[...] = jnp.zeros_like(dkpe_ref)

        for span, ext in enumerate(_attn_extents(t)):
            @pl.when(qi // (ATTN_KSTEP // ATTN_TQ) == span)
            def _(ext=ext):
                def g(qh, knv, kpev, vv):
                    return _attn_f(q0, qh, knv, kpev, vv, cos, sin)

                _, vjp = jax.vjp(g, q_ref[...].astype(F32), kn_ref[0:ext, :].astype(F32), kpe_ref[0:ext, :].astype(F32),
                                 v_ref[0:ext, :].astype(F32))
                dq, dkn, dkpe, dv = vjp((do_ref[...].astype(F32),))
                dq_ref[...] = dq.astype(dq_ref.dtype)
                dkn_ref[0:ext, :] += dkn
                dkpe_ref[0:ext, :] += dkpe
                dv_ref[0:ext, :] += dv

    in_specs = _attn_in_specs(t) + [pl.BlockSpec((ATTN_TQ, MLA_V), lambda h, qi: (qi, h))]
    out_specs = [
        pl.BlockSpec((ATTN_TQ, MLA_QPAD), lambda h, qi: (qi, h)),
        pl.BlockSpec((t, MLA_NOPE), lambda h, qi: (0, h)),
        pl.BlockSpec((t, LANES), lambda h, qi: (0, 0)),
        pl.BlockSpec((t, MLA_V), lambda h, qi: (0, h)),
    ]
    out_shape = [
        jax.ShapeDtypeStruct((t, MLA_HEADS * MLA_QPAD), BF16),
        jax.ShapeDtypeStruct((t, MLA_HEADS * MLA_NOPE), F32),
        jax.ShapeDtypeStruct((t, LANES), F32),
        jax.ShapeDtypeStruct((t, MLA_HEADS * MLA_V), F32),
    ]
    return pl.pallas_call(
        body, name=name, grid=(MLA_HEADS, t // ATTN_TQ), in_specs=in_specs, out_specs=out_specs, out_shape=out_shape,
        compiler_params=_params("arbitrary", "arbitrary"),
    )(q, kn, kpe, v, cos, sin, do)


def final_loss(name, h, nf, target, tr=256):
    t, d = h.shape

    def body(h_ref, w_ref, t_ref, loss_ref, dh_ref, dw_ref):
        i = pl.program_id(0)
        tgt = t_ref[...]

        def f(hv, wv):
            err = _rms(hv, wv) - tgt
            return 0.5 * jnp.sum(jnp.mean(err * err, -1, keepdims=True), 0, keepdims=True)

        val, vjp = jax.vjp(f, h_ref[...], w_ref[...])
        dh, dw = vjp(jnp.ones((1, 1), F32))
        dh_ref[...] = dh
        tile = jnp.broadcast_to(val, loss_ref.shape)

        @pl.when(i == 0)
        def _():
            loss_ref[...] = tile
            dw_ref[...] = dw

        @pl.when(i > 0)
        def _():
            loss_ref[...] += tile
            dw_ref[...] += dw

    row = pl.BlockSpec((tr, d), lambda i: (i, 0))
    return pl.pallas_call(
        body, name=name, grid=(t // tr,), in_specs=[row, _full_spec(nf), row],
        out_specs=[pl.BlockSpec((8, LANES), lambda i: (0, 0)), row, _full_spec(nf)],
        out_shape=[jax.ShapeDtypeStruct((8, LANES), F32), jax.ShapeDtypeStruct((t, d), F32), jax.ShapeDtypeStruct(nf.shape, F32)],
        compiler_params=_params("arbitrary"),
    )(h, nf, target)


ANY = pl.BlockSpec(memory_space=pl.ANY)
CHIP_ORDER = ((0, 0), (0, 1), (1, 0), (1, 1))


def _place():
    return lax.axis_index("x"), lax.axis_index("y"), lax.axis_index("c")


def _other_chips(x, y):
    return [(1 - x, y), (x, 1 - y), (1 - x, 1 - y)]


def _device_slot():
    x, y, c = _place()
    return 4 * x + 2 * y + c


def _row_tile(rows):
    return next(t for t in (128, 64, 32, 16) if rows % t == 0)


def all_gather(name, shards):
    n = len(shards)

    def body(*refs):
        ins, outs = refs[:n], refs[n:2 * n]
        send_sems, recv_sems, local_sems = refs[2 * n:]
        x, y, c = _place()
        me, sibling = (x, y, c), (x, y, 1 - c)
        chips = _other_chips(x, y)

        def copy(k, j, block, to, from_input=False):
            dst = outs[k].at[4 * block[0] + 2 * block[1] + block[2]]
            return pltpu.make_async_remote_copy(
                src_ref=ins[k] if from_input else dst, dst_ref=dst,
                send_sem=send_sems.at[7 * k + j], recv_sem=recv_sems.at[7 * k + j],
                device_id=to, device_id_type=MESH)

        mine = [pltpu.make_async_copy(ins[k], outs[k].at[4 * x + 2 * y + c], local_sems.at[k]) for k in range(n)]
        for cp in mine:
            cp.start()
        first = []
        for k in range(n):
            first.append(copy(k, 0, me, sibling, True))
            first += [copy(k, 1 + j, me, (*chip, c), True) for j, chip in enumerate(chips)]
        for cp in first:
            cp.start()
        passed = []
        for j, chip in enumerate(chips):
            for k in range(n):
                copy(k, 1 + j, (*chip, c), me).wait_recv()
                fwd = copy(k, 4 + j, (*chip, c), sibling)
                fwd.start()
                passed.append(fwd)
        for k in range(n):
            copy(k, 0, sibling, me).wait_recv()
        for j, chip in enumerate(chips):
            for k in range(n):
                copy(k, 4 + j, (*chip, 1 - c), me).wait_recv()
        for cp in first + passed:
            cp.wait_send()
        for cp in mine:
            cp.wait()

    return pl.pallas_call(
        body, name=name, in_specs=[ANY] * n, out_specs=[ANY] * n,
        out_shape=[jax.ShapeDtypeStruct((N_DEV,) + s.shape, s.dtype) for s in shards],
        scratch_shapes=[pltpu.SemaphoreType.DMA((7 * n,)), pltpu.SemaphoreType.DMA((7 * n,)), pltpu.SemaphoreType.DMA((n,))],
    )(*shards)


def exchange_sibling(name, gs, after=None):
    n = len(gs)
    extra = [] if after is None else [after]

    def body(*refs):
        ins, outs = refs[:n], refs[n + len(extra):2 * n + len(extra)]
        send_sems, recv_sems = refs[2 * n + len(extra):]
        x, y, c = _place()
        copies = []
        for k in range(n):
            for q, (cx, cy) in enumerate(CHIP_ORDER):
                copies.append(pltpu.make_async_remote_copy(
                    src_ref=ins[k].at[4 * cx + 2 * cy + (1 - c)], dst_ref=outs[k].at[q],
                    send_sem=send_sems.at[4 * k + q], recv_sem=recv_sems.at[4 * k + q],
                    device_id=(x, y, 1 - c), device_id_type=MESH))
        for cp in copies:
            cp.start()
        for cp in copies:
            cp.wait()

    return pl.pallas_call(
        body, name=name, in_specs=[ANY] * (n + len(extra)), out_specs=[ANY] * n,
        out_shape=[jax.ShapeDtypeStruct((4,) + g.shape[1:], g.dtype) for g in gs],
        scratch_shapes=[pltpu.SemaphoreType.DMA((4 * n,)), pltpu.SemaphoreType.DMA((4 * n,))],
    )(*gs, *extra)


def chip_sums(name, g, recv, tr=128):
    _, r, c = g.shape

    def body(g_ref, r_ref, o_ref):
        o_ref[...] = (g_ref[...].astype(F32) + r_ref[...].astype(F32)).astype(o_ref.dtype)

    def chip(i):
        x, y, _ = _place()
        return jnp.where(i % 2 == 1, 1 - x, x), jnp.where(i >= 2, 1 - y, y)

    def g_index(i, j):
        cx, cy = chip(i)
        return 4 * cx + 2 * cy + lax.axis_index("c"), j, 0

    def recv_index(i, j):
        cx, cy = chip(i)
        return 2 * cx + cy, j, 0

    return pl.pallas_call(
        body, name=name, grid=(4, r // tr),
        in_specs=[pl.BlockSpec((1, tr, c), g_index), pl.BlockSpec((1, tr, c), recv_index)],
        out_specs=pl.BlockSpec((1, tr, c), lambda i, j: (i, j, 0)),
        out_shape=jax.ShapeDtypeStruct((4, r, c), g.dtype),
        compiler_params=_params("parallel", "parallel"),
    )(g, recv)


def sum_parts(name, parts, tr=128):
    _, r, c = parts[0][0].shape

    def body(*refs):
        total = refs[0][0].astype(F32)
        for ref in refs[1:-1]:
            total = total + ref[0].astype(F32)
        refs[-1][...] = total

    return pl.pallas_call(
        body, name=name, grid=(r // tr,),
        in_specs=[pl.BlockSpec((1, tr, c), lambda i, s=s: (s, i, 0)) for _, s in parts],
        out_specs=pl.BlockSpec((tr, c), lambda i: (i, 0)), out_shape=jax.ShapeDtypeStruct((r, c), F32),
        compiler_params=_params("parallel"),
    )(*[a for a, _ in parts])


def adamw(name, parts, w, m, v, tr=128, part=0, prev=None, after=None):
    _, r, c = parts[0][0].shape
    np_ = len(parts)
    first = part * (r // tr)

    def body(*refs):
        g = refs[0][0].astype(F32)
        for ref in refs[1:np_]:
            g = g + ref[0].astype(F32)
        w_ref, m_ref, v_ref = refs[np_:np_ + 3]
        g_out, d_out, m_out, v_out = refs[-4:]
        new_m = ADAM_B1 * m_ref[...] + (1.0 - ADAM_B1) * g
        new_v = ADAM_B2 * v_ref[...] + (1.0 - ADAM_B2) * (g * g)
        m_hat = new_m / (1.0 - ADAM_B1 ** ADAM_STEP)
        v_hat = new_v / (1.0 - ADAM_B2 ** ADAM_STEP)
        g_out[...] = g
        d_out[...] = -ADAM_LR * (m_hat / (jnp.sqrt(v_hat) + ADAM_EPS) + ADAM_WD * w_ref[...])
        m_out[...] = new_m
        v_out[...] = new_v

    tile = pl.BlockSpec((tr, c), lambda i: (first + i, 0))
    in_specs = [pl.BlockSpec((1, tr, c), lambda i, s=s: (s, i, 0)) for _, s in parts] + [tile] * 3
    ins = [a for a, _ in parts] + [w, m, v]
    aliases = {}
    if prev is not None:
        aliases = {len(ins) + k: k for k in range(4)}
        in_specs += [ANY] * 4
        ins += list(prev)
    if after is not None:
        in_specs.append(ANY)
        ins.append(after)
    return pl.pallas_call(
        body, name=name, grid=(r // tr,), in_specs=in_specs,
        out_specs=[tile] * 4, out_shape=[jax.ShapeDtypeStruct(w.shape, F32)] * 4,
        input_output_aliases=aliases, compiler_params=_params("parallel"),
    )(*ins)


HBM = pl.BlockSpec(memory_space=pltpu.HBM)
SEM = pl.BlockSpec(memory_space=pltpu.SEMAPHORE)
SIDE_EFFECT = pltpu.SideEffectType.DATAFLOW_SIDE_EFFECTING


def _split_copies(plan, src_refs, land_refs, send_sems, recv_sems):
    copies = []
    for i, (k, src_slot, land_slot, device) in enumerate(plan(*_place())):
        copies.append(pltpu.make_async_remote_copy(
            src_ref=src_refs[k] if src_slot is None else src_refs[k].at[src_slot], dst_ref=land_refs[k].at[land_slot],
            send_sem=send_sems.at[i], recv_sem=recv_sems.at[i], device_id=device, device_id_type=MESH))
    return copies


def split_start(name, srcs, land_shapes, plan, n_copies):
    n = len(srcs)

    def body(*refs):
        src_refs, land_refs = refs[:n], refs[n:2 * n]
        send_sems, recv_sems, token = refs[2 * n], refs[2 * n + 1], refs[-1]
        for cp in _split_copies(plan, src_refs, land_refs, send_sems, recv_sems):
            cp.start()
        token[...] = jnp.zeros_like(token)

    lands = [lax.empty(shape, s.dtype) for shape, s in zip(land_shapes, srcs)]
    ins = [pltpu.with_memory_space_constraint(a, pltpu.HBM) for a in list(srcs) + lands]
    out = pl.pallas_call(
        body, name=name,
        out_shape=(pltpu.SemaphoreType.DMA((n_copies,)), pltpu.SemaphoreType.DMA((n_copies,)),
                   *[pltpu.HBM(a.shape, a.dtype) for a in ins], jax.ShapeDtypeStruct((8, LANES), F32)),
        in_specs=[HBM] * (2 * n), out_specs=(SEM, SEM, *[HBM] * (2 * n), pl.BlockSpec(memory_space=pltpu.VMEM)),
        input_output_aliases={i: 2 + i for i in range(2 * n)},
        compiler_params=pltpu.CompilerParams(has_side_effects=SIDE_EFFECT),
    )(*ins)
    return out[0], out[1], list(out[2:2 + n]), list(out[2 + n:2 + 2 * n]), out[-1]


def split_wait(name, handle, plan, after):
    send_sems, recv_sems, srcs, lands, _ = handle
    n = len(srcs)

    def body(*refs):
        src_refs, land_refs = refs[:n], refs[n:2 * n]
        for cp in _split_copies(plan, src_refs, land_refs, refs[2 * n], refs[2 * n + 1]):
            cp.wait_send()
            cp.wait_recv()

    out = pl.pallas_call(
        body, name=name, out_shape=tuple(pltpu.HBM(a.shape, a.dtype) for a in srcs + lands),
        in_specs=[HBM] * (2 * n) + [SEM, SEM, ANY], out_specs=tuple([HBM] * (2 * n)),
        input_output_aliases={i: i for i in range(2 * n)},
        compiler_params=pltpu.CompilerParams(has_side_effects=SIDE_EFFECT),
    )(*srcs, *lands, send_sems, recv_sems, after)
    return list(out[:n]), list(out[n:])


def gather_plan(n):
    def plan(x, y, c):
        me = 4 * x + 2 * y + c
        peers = [(x, y, 1 - c)] + [(*chip, c) for chip in _other_chips(x, y)]
        return [(k, None, me, peer) for k in range(n) for peer in peers]
    return plan


def chips_plan(n):
    def plan(x, y, c):
        return [(k, 1 + j, j, (*chip, c)) for k in range(n) for j, chip in enumerate(_other_chips(x, y))]
    return plan


def gather_finish(name, gathered):
    n = len(gathered)

    def body(*refs):
        outs = refs[n:2 * n]
        send_sems, recv_sems = refs[2 * n:]
        x, y, c = _place()

        def passed_on(k, j, core):
            cx, cy = _other_chips(x, y)[j]
            blk = outs[k].at[4 * cx + 2 * cy + core]
            return pltpu.make_async_remote_copy(
                src_ref=blk, dst_ref=blk, send_sem=send_sems.at[3 * k + j], recv_sem=recv_sems.at[3 * k + j],
                device_id=(x, y, 1 - c), device_id_type=MESH)

        pairs = [(k, j) for k in range(n) for j in range(3)]
        sends = [passed_on(k, j, c) for k, j in pairs]
        for cp in sends:
            cp.start()
        for k, j in pairs:
            passed_on(k, j, 1 - c).wait_recv()
        for cp in sends:
            cp.wait_send()

    return pl.pallas_call(
        body, name=name, in_specs=[ANY] * n, out_specs=[ANY] * n,
        out_shape=[jax.ShapeDtypeStruct(g.shape, g.dtype) for g in gathered],
        input_output_aliases={k: k for k in range(n)},
        scratch_shapes=[pltpu.SemaphoreType.DMA((3 * n,)), pltpu.SemaphoreType.DMA((3 * n,))],
    )(*gathered)


ROW_TILE = 256
COL_TILE = 256


def _rms_fwd(tag, h, w, after=None):
    return rows_fwd(tag, _rms_f, [(h, D_MODEL, 0)], [w], [(D_MODEL, BF16)], ROW_TILE, after=after)[0]


def _rms_bwd(tag, h, w, dhn, dres):
    (dh,), (dw,) = rows_bwd(tag, _rms_f, [(h, D_MODEL, 0)], [w], [(dhn, D_MODEL, 0)], ROW_TILE, [F32], add=dres)
    return dh, dw


def even_fwd(tag, h, w, p, after=None):
    hn = _rms_fwd(tag + "_rms", h, p["nm"], after)
    uv = matmul(tag + "_uv", hn, w["uv"])
    z = matmul(tag + "_z", hn, w["z"])
    xbc = matmul(tag + "_xbc", hn, w["xbc"])
    pdt = matmul(tag + "_dt", hn, w["dt"])
    gm = [p["lng"], p["lnb"], p["ws"], p["bst"]]
    ya = rows_fwd(tag + "_gmlp", _gmlp_f, [(uv, 2 * D_MODEL, 0)], gm, [(D_MODEL, BF16)], GM_BLOCK)[0]
    xa = cols_fwd(tag + "_conv", _conv_silu_f, [(xbc, 0)], [p["cw"], p["cb"]], F32, COL_TILE)
    y, states = ssd_fwd(tag + "_ssd", xa, pdt, p["hp"])
    yb = rows_fwd(tag + "_gate", _gate_norm_f, [(y, D_MODEL, 0), (z, D_MODEL, 0)], [p["nw"]], [(D_MODEL, BF16)], ROW_TILE)[0]
    h1 = matmul(tag + "_out_b", yb, w["out_bot"], res=matmul(tag + "_out_a", ya, w["out_top"], res=h))
    return h1, dict(h=h, hn=hn, uv=uv, z=z, xbc=xbc, pdt=pdt, xa=xa, y=y, states=states, ya=ya, yb=yb)


def even_bwd(tag, dh1, s, w, p, after=None):
    dya = matmul(tag + "_dya", dh1, w["out_top"], tb=True, after=after)
    dyb = matmul(tag + "_dyb", dh1, w["out_bot"], tb=True, after=after)
    gw = dict(out_top=matmul(tag + "_gwa", s["ya"], dh1, ta=True, out_dtype=BF16),
              out_bot=matmul(tag + "_gwb", s["yb"], dh1, ta=True, out_dtype=BF16))
    (dy, dz), (dnw,) = rows_bwd(tag + "_gate_b", _gate_norm_f, [(s["y"], D_MODEL, 0), (s["z"], D_MODEL, 0)], [p["nw"]],
                                [(dyb, D_MODEL, 0)], ROW_TILE, [F32, BF16])
    dxs, dbm, dcm, dpdt, dhp = ssd_bwd(tag + "_ssd_b", s["xa"], s["pdt"], p["hp"], s["states"], dy)
    dxa = jnp.concatenate([dxs, dbm, dcm], axis=1)
    (dxbc,), (dcw, dcb) = cols_bwd(tag + "_conv_b", _conv_silu_f, [(s["xbc"], 0)], [p["cw"], p["cb"]], dxa, COL_TILE, BF16)
    gm = [p["lng"], p["lnb"], p["ws"], p["bst"]]
    (duv,), (dlng, dlnb, dws, dbst) = rows_bwd(tag + "_gmlp_b", _gmlp_f, [(s["uv"], 2 * D_MODEL, 0)], gm,
                                               [(dya, D_MODEL, 0)], GM_BLOCK, [BF16])
    dhn = None
    for key, d in (("uv", duv), ("z", dz), ("xbc", dxbc), ("dt", dpdt)):
        dhn = matmul(f"{tag}_dx_{key}", d, w[key], tb=True, res=dhn)
        gw[key] = matmul(f"{tag}_gw_{key}", s["hn"], d, ta=True, out_dtype=BF16)
    dh, dnm = _rms_bwd(tag + "_rms_b", s["h"], p["nm"], dhn, dh1)
    gp = dict(nm=dnm, lng=dlng, lnb=dlnb, ws=dws, bst=dbst, cw=dcw, cb=dcb, hp=dhp, nw=dnw)
    return dh, gw, gp


def odd_fwd(tag, h, w, p, cos, sin, after=None):
    hn = _rms_fwd(tag + "_rms", h, p["nm"], after)
    proj = matmul(tag + "_in", hn, w["in"])
    cq, ckv, kpe = rows_fwd(tag + "_qkvn", _qkv_norm_f, [(proj, ODD_IN_PAD, 0), (cos, LANES, 0), (sin, LANES, 0)],
                            [p["qn"], p["kvn"]], [(MLA_RANK, BF16), (MLA_RANK, BF16), (LANES, F32)], ROW_TILE)
    q = matmul(tag + "_q", cq, w["uq"])
    kn = matmul(tag + "_kn", ckv, w["kn"], out_dtype=BF16)
    v = matmul(tag + "_v", ckv, w["v"], out_dtype=BF16)
    o = attn_fwd(tag + "_attn", q, kn, kpe, v, cos, sin)
    h1 = matmul(tag + "_o", o, w["o"], res=h)
    return h1, dict(h=h, hn=hn, proj=proj, cq=cq, ckv=ckv, kpe=kpe, q=q, kn=kn, v=v, o=o)


def odd_bwd(tag, dh1, s, w, p, cos, sin, after=None):
    do = matmul(tag + "_do", dh1, w["o"], tb=True, after=after)
    gw = dict(o=matmul(tag + "_gw_o", s["o"], dh1, ta=True, out_dtype=BF16))
    dq, dkn, dkpe, dv = attn_bwd(tag + "_attn_b", s["q"], s["kn"], s["kpe"], s["v"], cos, sin, do)
    dcq = matmul(tag + "_dcq", dq, w["uq"], tb=True)
    gw["uq"] = matmul(tag + "_gw_uq", s["cq"], dq, ta=True, out_dtype=BF16)
    dckv = matmul(tag + "_dckv_v", dv, w["v"], tb=True, res=matmul(tag + "_dckv_k", dkn, w["kn"], tb=True))
    gw["kn"] = matmul(tag + "_gw_kn", s["ckv"], dkn, ta=True, out_dtype=BF16)
    gw["v"] = matmul(tag + "_gw_v", s["ckv"], dv, ta=True, out_dtype=BF16)
    (dproj,), (dqn, dkvn) = rows_bwd(
        tag + "_qkvn_b", _qkv_norm_f, [(s["proj"], ODD_IN_PAD, 0), (cos, LANES, 0), (sin, LANES, 0)], [p["qn"], p["kvn"]],
        [(dcq, MLA_RANK, 0), (dckv, MLA_RANK, 0), (dkpe, LANES, 0)], ROW_TILE, [BF16], n_nondiff=2)
    dhn = matmul(tag + "_dx_in", dproj, w["in"], tb=True)
    gw["in"] = matmul(tag + "_gw_in", s["hn"], dproj, ta=True, out_dtype=BF16)
    dh, dnm = _rms_bwd(tag + "_rms_b", s["h"], p["nm"], dhn, dh1)
    return dh, gw, dict(nm=dnm, qn=dqn, kvn=dkvn)


def ffn_fwd(tag, h, w, p, after=None):
    hn = _rms_fwd(tag + "_rms", h, p["nf"], after)
    g = matmul(tag + "_up_g", hn, w["up_g"])
    val = matmul(tag + "_up_v", hn, w["up_v"])
    act = cols_fwd(tag + "_act", _ffn_act_f, [(g, 0), (val, 0)], [p["fcw"], p["fcb"]], BF16, COL_TILE)
    h2 = matmul(tag + "_down", act, w["down"], res=h)
    return h2, dict(h=h, hn=hn, g=g, val=val, act=act)


def ffn_bwd(tag, dh2, s, w, p, after=None):
    dact = matmul(tag + "_dact", dh2, w["down"], tb=True, after=after)
    gw = dict(down=matmul(tag + "_gw_down", s["act"], dh2, ta=True, out_dtype=BF16))
    (dg, dval), (dfcw, dfcb) = cols_bwd(tag + "_act_b", _ffn_act_f, [(s["g"], 0), (s["val"], 0)], [p["fcw"], p["fcb"]],
                                        dact, COL_TILE, BF16)
    dhn = matmul(tag + "_dx_v", dval, w["up_v"], tb=True, res=matmul(tag + "_dx_g", dg, w["up_g"], tb=True))
    gw["up_g"] = matmul(tag + "_gw_up_g", s["hn"], dg, ta=True, out_dtype=BF16)
    gw["up_v"] = matmul(tag + "_gw_up_v", s["hn"], dval, ta=True, out_dtype=BF16)
    dh, dnf = _rms_bwd(tag + "_rms_b", s["h"], p["nf"], dhn, dh2)
    return dh, gw, dict(nf=dnf, fcw=dfcw, fcb=dfcb)


def _cols_from_slots(g):
    return jnp.moveaxis(g, 0, 1).reshape(g.shape[1], N_DEV * g.shape[2])


def _slots_from_cols(wmat):
    k, n = wmat.shape
    return jnp.moveaxis(wmat.reshape(k, N_DEV, n // N_DEV), 1, 0)


def _pad_last(a, width):
    return jnp.pad(a, [(0, 0)] * (a.ndim - 1) + [(0, width - a.shape[-1])])


def _heads_to_lanes(a):
    lead = a.shape[:-1]
    return _pad_last(a.reshape(lead + (SSD_STEPS, SSD_HEADS_PER_STEP)), LANES).reshape(lead + (SSD_STEPS * LANES,))


def _lanes_to_heads(a):
    lead = a.shape[:-1]
    return a.reshape(lead + (SSD_STEPS, LANES))[..., :SSD_HEADS_PER_STEP].reshape(lead + (SSM_HEADS,))


def prep_even(g_in, g_out):
    wn = _cols_from_slots(g_in)
    o1, o2, o3 = 2 * D_MODEL, 3 * D_MODEL, 3 * D_MODEL + SSM_CONV_DIM
    out = g_out.reshape(2 * D_MODEL, D_MODEL)
    return dict(uv=wn[:, :o1], z=wn[:, o1:o2], xbc=wn[:, o2:o3], dt=_heads_to_lanes(wn[:, o3:]),
                out_top=out[:D_MODEL], out_bot=out[D_MODEL:])


def unprep_even(gw):
    wn = jnp.concatenate([gw["uv"], gw["z"], gw["xbc"], _lanes_to_heads(gw["dt"])], axis=1)
    return _slots_from_cols(wn), jnp.concatenate([gw["out_top"], gw["out_bot"]], axis=0).reshape(N_DEV, -1, D_MODEL)


def prep_odd(g_in, g_uq, g_ukv, g_o):
    uq = _cols_from_slots(g_uq).reshape(MLA_RANK, MLA_HEADS, MLA_QK)
    ukv = _cols_from_slots(g_ukv).reshape(MLA_RANK, MLA_HEADS, MLA_NOPE + MLA_V)
    return dict(**{"in": _pad_last(g_in.reshape(D_MODEL, ODD_IN), ODD_IN_PAD)},
                uq=_pad_last(uq, MLA_QPAD).reshape(MLA_RANK, MLA_HEADS * MLA_QPAD),
                kn=ukv[:, :, :MLA_NOPE].reshape(MLA_RANK, MLA_HEADS * MLA_NOPE),
                v=ukv[:, :, MLA_NOPE:].reshape(MLA_RANK, MLA_HEADS * MLA_V),
                o=g_o.reshape(MLA_HEADS * MLA_V, D_MODEL))


def unprep_odd(gw):
    uq = gw["uq"].reshape(MLA_RANK, MLA_HEADS, MLA_QPAD)[:, :, :MLA_QK].reshape(MLA_RANK, MLA_HEADS * MLA_QK)
    ukv = jnp.concatenate([gw["kn"].reshape(MLA_RANK, MLA_HEADS, MLA_NOPE), gw["v"].reshape(MLA_RANK, MLA_HEADS, MLA_V)], axis=2)
    return (gw["in"][:, :ODD_IN].reshape(N_DEV, -1, ODD_IN), _slots_from_cols(uq),
            _slots_from_cols(ukv.reshape(MLA_RANK, -1)), gw["o"].reshape(N_DEV, -1, D_MODEL))


def prep_ffn(g_up, g_down):
    up = _cols_from_slots(g_up)
    return dict(up_g=up[:, :D_FF], up_v=up[:, D_FF:], down=g_down.reshape(D_FF, D_MODEL))


def unprep_ffn(gw):
    return _slots_from_cols(jnp.concatenate([gw["up_g"], gw["up_v"]], axis=1)), gw["down"].reshape(N_DEV, -1, D_MODEL)


SMALL_TILE = LANES * LANES


def _pack(arrs):
    flat = jnp.concatenate([a.reshape(-1).astype(F32) for a in arrs])
    size = -(-flat.shape[0] // SMALL_TILE) * SMALL_TILE
    return jnp.pad(flat, (0, size - flat.shape[0])).reshape(-1, LANES)


def _unpack(packed, shapes, lead=()):
    flat = packed.reshape(lead + (-1,))
    out, off = [], 0
    for shp in shapes:
        size = math.prod(shp)
        out.append(flat[..., off:off + size].reshape(lead + tuple(shp)))
        off += size
    return out


SMALL_SHARDED = {"ev_gm_ln_g": 2, "ev_gm_ln_b": 2, "ev_conv_w": 2, "od_q_norm": 1, "od_kv_norm": 1, "ff_conv_w": 2}
SMALL_REPLICATED = ["norm_mix", "norm_ffn", "norm_final", "ev_gm_ws", "ev_gm_bs", "ev_conv_b", "ev_dt_bias", "ev_a_log",
                    "ev_d_skip", "ev_ssm_norm_w", "ff_conv_b"]
MATRICES = {"ev_w_in": (2, 2048, 1156), "ev_w_out": (2, 512, 2048), "od_w_in": (2, 256, 1088), "od_w_uq": (2, 512, 384),
            "od_w_ukv": (2, 512, 512), "od_w_o": (2, 256, 2048), "ff_w_up": (4, 2048, 1408), "ff_w_down": (4, 704, 2048)}
WEIGHT_ORDER = ["norm_mix", "norm_ffn", "norm_final", "ev_w_in", "ev_gm_ln_g", "ev_gm_ln_b", "ev_gm_ws", "ev_gm_bs",
                "ev_conv_w", "ev_conv_b", "ev_dt_bias", "ev_a_log", "ev_d_skip", "ev_ssm_norm_w", "ev_w_out", "od_w_in",
                "od_q_norm", "od_kv_norm", "od_w_uq", "od_w_ukv", "od_w_o", "ff_w_up", "ff_conv_w", "ff_conv_b", "ff_w_down"]


def _full_from_shards(name, gathered):
    ax = SMALL_SHARDED[name]
    moved = jnp.moveaxis(gathered, 0, ax)
    shp = moved.shape
    return moved.reshape(shp[:ax] + (shp[ax] * shp[ax + 1],) + shp[ax + 2:])


def _my_shard(name, full, dev):
    ax = SMALL_SHARDED[name]
    shp = full.shape
    split = full.reshape(shp[:ax] + (N_DEV, shp[ax] // N_DEV) + shp[ax + 1:])
    return lax.dynamic_index_in_dim(split, dev, axis=ax, keepdims=False)


def _even_small(sm, j):
    row = lambda a: a.reshape(1, -1)
    hp = jnp.stack([sm["ev_dt_bias"][j], sm["ev_a_log"][j], sm["ev_d_skip"][j]])
    return dict(nm=row(sm["norm_mix"][2 * j]), lng=row(sm["ev_gm_ln_g"][j]), lnb=row(sm["ev_gm_ln_b"][j]),
                ws=sm["ev_gm_ws"][j], bst=sm["ev_gm_bs"][j].T, cw=sm["ev_conv_w"][j], cb=row(sm["ev_conv_b"][j]),
                hp=_heads_to_lanes(hp), nw=row(sm["ev_ssm_norm_w"][j]))


def _odd_small(sm, j):
    row = lambda a: a.reshape(1, -1)
    return dict(nm=row(sm["norm_mix"][2 * j + 1]), qn=row(sm["od_q_norm"][j]), kvn=row(sm["od_kv_norm"][j]))


def _ffn_small(sm, layer):
    row = lambda a: a.reshape(1, -1)
    return dict(nf=row(sm["norm_ffn"][layer]), fcw=sm["ff_conv_w"][layer], fcb=row(sm["ff_conv_b"][layer]))


def _rope_tables(positions):
    inv_freq = ROPE_THETA ** (-jnp.arange(0, MLA_ROPE, 2, dtype=F32) / MLA_ROPE)
    ang = positions.astype(F32).reshape(-1, 1) * inv_freq
    cos, sin = jnp.cos(ang), jnp.sin(ang)
    return _pad_last(jnp.concatenate([cos, cos], axis=1), LANES), _pad_last(jnp.concatenate([-sin, sin], axis=1), LANES)


def local_step(x, positions, target, sm, fetch_weights, emit_grads):
    cos, sin = _rope_tables(positions)
    h, saved = x, []
    for layer in range(4):
        j, tag = layer // 2, f"l{layer}"
        wm, dep = fetch_weights(2 * layer, h)
        if layer % 2 == 0:
            pm = _even_small(sm, j)
            h, sv = even_fwd(tag, h, wm, pm, dep)
        else:
            pm = _odd_small(sm, j)
            h, sv = odd_fwd(tag, h, wm, pm, cos, sin, dep)
        wf, dep = fetch_weights(2 * layer + 1, h)
        pf = _ffn_small(sm, layer)
        h, sf = ffn_fwd(tag + "f", h, wf, pf, dep)
        saved.append((pm, sv, pf, sf, wm, wf))
    loss_tile, dh, dnfinal = final_loss("final_loss", h, sm["norm_final"].reshape(1, -1), target)
    gs = {k: [None] * v.shape[0] for k, v in sm.items() if k != "norm_final"}
    gs["norm_final"] = dnfinal.reshape(-1)
    dep = None
    for layer in reversed(range(4)):
        j, tag = layer // 2, f"l{layer}"
        pm, sv, pf, sf, wm, wf = saved[layer]
        dh, gwf, gpf = ffn_bwd(tag + "f", dh, sf, wf, pf, dep)
        gs["norm_ffn"][layer], gs["ff_conv_w"][layer], gs["ff_conv_b"][layer] = gpf["nf"][0], gpf["fcw"], gpf["fcb"][0]
        dep = emit_grads(2 * layer + 1, gwf, dh)
        if layer % 2 == 0:
            dh, gwm, gp = even_bwd(tag, dh, sv, wm, pm, dep)
            hp = _lanes_to_heads(gp["hp"])
            gs["norm_mix"][layer] = gp["nm"][0]
            gs["ev_gm_ln_g"][j], gs["ev_gm_ln_b"][j] = gp["lng"].reshape(GM_GROUPS, -1), gp["lnb"].reshape(GM_GROUPS, -1)
            gs["ev_gm_ws"][j], gs["ev_gm_bs"][j] = gp["ws"], gp["bst"].T
            gs["ev_conv_w"][j], gs["ev_conv_b"][j] = gp["cw"], gp["cb"][0]
            gs["ev_dt_bias"][j], gs["ev_a_log"][j], gs["ev_d_skip"][j] = hp[0], hp[1], hp[2]
            gs["ev_ssm_norm_w"][j] = gp["nw"][0]
        else:
            dh, gwm, gp = odd_bwd(tag, dh, sv, wm, pm, cos, sin, dep)
            gs["norm_mix"][layer] = gp["nm"][0]
            gs["od_q_norm"][j], gs["od_kv_norm"][j] = gp["qn"][0], gp["kvn"][0]
        if layer > 0:
            dep = emit_grads(2 * layer, gwm, dh)
    gs = {k: (v if k == "norm_final" else jnp.stack(v)) for k, v in gs.items()}
    return loss_tile[0, 0], dh, gs, gwm


def kernel(x, positions, norm_mix, norm_ffn, norm_final, ev_w_in, ev_gm_ln_g, ev_gm_ln_b, ev_gm_ws, ev_gm_bs, ev_conv_w, ev_conv_b, ev_dt_bias, ev_a_log, ev_d_skip, ev_ssm_norm_w, ev_w_out, od_w_in, od_q_norm, od_kv_norm, od_w_uq, od_w_ukv, od_w_o, ff_w_up, ff_conv_w, ff_conv_b, ff_w_down, loss_target, m_norm_mix, m_norm_ffn, m_norm_final, m_ev_w_in, m_ev_gm_ln_g, m_ev_gm_ln_b, m_ev_gm_ws, m_ev_gm_bs, m_ev_conv_w, m_ev_conv_b, m_ev_dt_bias, m_ev_a_log, m_ev_d_skip, m_ev_ssm_norm_w, m_ev_w_out, m_od_w_in, m_od_q_norm, m_od_kv_norm, m_od_w_uq, m_od_w_ukv, m_od_w_o, m_ff_w_up, m_ff_conv_w, m_ff_conv_b, m_ff_w_down, v_norm_mix, v_norm_ffn, v_norm_final, v_ev_w_in, v_ev_gm_ln_g, v_ev_gm_ln_b, v_ev_gm_ws, v_ev_gm_bs, v_ev_conv_w, v_ev_conv_b, v_ev_dt_bias, v_ev_a_log, v_ev_d_skip, v_ev_ssm_norm_w, v_ev_w_out, v_od_w_in, v_od_q_norm, v_od_kv_norm, v_od_w_uq, v_od_w_ukv, v_od_w_o, v_ff_w_up, v_ff_conv_w, v_ff_conv_b, v_ff_w_down):
    args = dict(locals())
    wts = {n: args[n] for n in WEIGHT_ORDER}
    mom = {n: args["m_" + n] for n in WEIGHT_ORDER}
    var = {n: args["v_" + n] for n in WEIGHT_ORDER}
    dev = _device_slot()

    small_names = list(SMALL_SHARDED)
    small_shapes = [wts[n].shape for n in small_names]
    (small_all,) = all_gather("ag_small", [_pack([wts[n] for n in small_names])])
    small_full = _unpack(small_all, small_shapes, lead=(N_DEV,))
    sm = {n: _full_from_shards(n, g) for n, g in zip(small_names, small_full)}
    sm.update({n: wts[n] for n in SMALL_REPLICATED})

    bf = {n: wts[n].astype(BF16) for n in MATRICES}

    def stage_matrices(stage):
        layer, is_ffn = divmod(stage, 2)
        if is_ffn:
            return [("ff_w_up", layer), ("ff_w_down", layer)]
        return [(n, layer // 2) for n in (["ev_w_in", "ev_w_out"] if layer % 2 == 0 else ["od_w_in", "od_w_uq", "od_w_ukv", "od_w_o"])]

    def stage_fns(stage):
        layer, is_ffn = divmod(stage, 2)
        if is_ffn:
            return prep_ffn, unprep_ffn
        return (prep_even, unprep_even) if layer % 2 == 0 else (prep_odd, unprep_odd)

    n_stages, ahead = 8, 2

    def start_gather(stage, earlier=None):
        shards = [bf[n][i] for n, i in stage_matrices(stage)]
        if earlier is not None:
            shards, _ = lax.optimization_barrier((shards, earlier))
        return split_start(f"ag_s{stage}_start", shards, [(N_DEV,) + s.shape for s in shards],
                           gather_plan(len(shards)), 4 * len(shards))

    gathers = {}
    for stage in range(ahead):
        gathers[stage] = start_gather(stage, gathers[stage - 1][4] if stage else None)

    def fetch_weights(stage, h):
        plan = gather_plan(len(stage_matrices(stage)))
        shards, landed = split_wait(f"ag_s{stage}_wait", gathers.pop(stage), plan, h)
        g = gather_finish(f"ag_s{stage}_finish", landed)
        g = [lax.dynamic_update_index_in_dim(gk, sk, dev, 0) for gk, sk in zip(g, shards)]
        started = None
        if stage + ahead < n_stages:
            gathers[stage + ahead] = start_gather(stage + ahead, g[0])
            started = gathers[stage + ahead][4]
        return stage_fns(stage)[0](*g), started

    scatters = []
    out = {n: None for n in MATRICES}

    held = []
    hold_below = 4

    def update(mats, sums, recv, after=None):
        for (n, i), p_, r_ in zip(mats, sums, recv):
            layers, rows, cols = MATRICES[n]
            two_d = lambda a: a.reshape(layers * rows, cols)
            out[n] = adamw(f"adamw_{n}_{i}", [(p_, 0), (r_, 0), (r_, 1), (r_, 2)], two_d(wts[n]), two_d(mom[n]),
                           two_d(var[n]), tr=_row_tile(rows), part=i, prev=out[n], after=after)
            if after is not None:
                after = out[n][0]
        return after

    def finish_scatter(after):
        stage, handle = scatters.pop(0)
        mats = stage_matrices(stage)
        sums, recv = split_wait(f"rs_s{stage}_wait", handle, chips_plan(len(mats)), after)
        if 0 < stage < hold_below:
            held.append((mats, sums, recv))
        else:
            update(mats, sums, recv)

    def emit_grads(stage, gw, dh, after=None):
        if len(scatters) >= ahead:
            finish_scatter(dh)
        send = list(stage_fns(stage)[1](gw))
        from_sibling = exchange_sibling(f"rs_s{stage}_sibling", send, after)
        sums = [chip_sums(f"rs_s{stage}_add{k}", g, r, tr=_row_tile(g.shape[1])) for k, (g, r) in enumerate(zip(send, from_sibling))]
        handle = split_start(f"rs_s{stage}_start", sums, [(3,) + s.shape[1:] for s in sums], chips_plan(len(sums)), 3 * len(sums))
        scatters.append((stage, handle))
        return handle[4]

    loss_local, dx, gs, gw_first = local_step(x[0], positions[0], loss_target[0], sm, fetch_weights, emit_grads)
    loss = lax.psum(loss_local, ("x", "y", "c"))

    all_small = small_names + SMALL_REPLICATED
    (partials,) = all_gather("ar_small", [_pack([gs[n] for n in all_small])])
    last_started = emit_grads(0, gw_first, partials, after=partials)
    total = sum_parts("ar_small_sum", [(partials, s) for s in range(N_DEV)])
    g_full = dict(zip(all_small, _unpack(total, [gs[n].shape for n in all_small])))
    g_mine = {n: (_my_shard(n, g_full[n], dev) if n in SMALL_SHARDED else g_full[n]) for n in all_small}
    packed = [_pack([d[n] for n in all_small]) for d in (g_mine, wts, mom, var)]
    res = adamw("adamw_small", [(packed[0][None], 0)], packed[1], packed[2], packed[3], after=last_started)
    unpacked = [_unpack(a, [wts[n].shape for n in all_small]) for a in res]
    for i, n in enumerate(all_small):
        out[n] = [u[i] for u in unpacked]
    while len(scatters) > 1:
        finish_scatter(res[0])
    follow = res[0]
    for job in held:
        follow = update(*job, after=follow)
    finish_scatter(follow)
    for n in MATRICES:
        out[n] = [a.reshape(wts[n].shape) for a in out[n]]

    return (loss, dx[None], *[out[n][0] for n in WEIGHT_ORDER], *[out[n][1] for n in WEIGHT_ORDER],
            *[out[n][2] for n in WEIGHT_ORDER], *[out[n][3] for n in WEIGHT_ORDER])
```

```python
import functools
import math

import jax
import jax.numpy as jnp
from jax import lax
from jax.experimental import pallas as pl
from jax.experimental.pallas import tpu as pltpu

F32 = jnp.float32
BF16 = jnp.bfloat16
MESH = pl.DeviceIdType.MESH

V7X_VMEM_LIMIT_BYTES = 56 * 1024 * 1024
LANES = 128

EPS = 1e-6
D_MODEL = 2048
CHUNK = 64
GM_BLOCK = 128
GM_GROUPS = 8
GM_GROUP_DIM = D_MODEL // GM_GROUPS
SSM_HEADS = 32
SSM_HEAD_DIM = 64
SSM_GROUPS = 4
SSM_STATE = 128
SSM_CONV = 4
SSM_BC = SSM_GROUPS * SSM_STATE
SSM_CONV_DIM = D_MODEL + 2 * SSM_BC
SSD_HEADS_PER_STEP = 4
SSD_STEPS = SSM_HEADS // SSD_HEADS_PER_STEP
SSD_X_WIDTH = SSD_HEADS_PER_STEP * SSM_HEAD_DIM
MLA_HEADS = 16
MLA_RANK = 512
MLA_NOPE = 128
MLA_ROPE = 64
MLA_V = 128
MLA_QK = MLA_NOPE + MLA_ROPE
MLA_QPAD = 2 * LANES
ODD_IN = 2 * MLA_RANK + MLA_ROPE
ODD_IN_PAD = 2 * MLA_RANK + LANES
D_FF = 5632
ROPE_THETA = 10000.0
N_DEV = 8

ADAM_LR, ADAM_B1, ADAM_B2, ADAM_EPS, ADAM_WD, ADAM_STEP = 0.001, 0.9, 0.999, 1e-08, 0.01, 10


def _params(*sem):
    return pltpu.CompilerParams(dimension_semantics=sem, vmem_limit_bytes=V7X_VMEM_LIMIT_BYTES)


def _pick(dim, target):
    if dim <= target:
        return dim
    t = (target // LANES) * LANES
    while t >= LANES:
        if dim % t == 0:
            return t
        t -= LANES
    raise ValueError(f"no tile for {dim} under {target}")


MATMUL_VMEM_BUDGET = 32 * 1024 * 1024


def _matmul_tiles(m, n, k, a_bytes, b_bytes, o_bytes, has_res, ta):
    def fits(tm, tn):
        per_out = o_bytes + (4 if has_res else 0)
        return 2 * (tm * k * a_bytes + tn * k * b_bytes + tm * tn * per_out) <= MATMUL_VMEM_BUDGET

    tns = (2048, 1024, 512, 256, 128) if ta else (512, 256, 128)
    tms = (512, 256, 128) if ta else (2048, 1024, 512, 256, 128)
    for tn in tns:
        tn = _pick(n, tn)
        for tm in tms:
            tm = _pick(m, tm)
            if fits(tm, tn):
                return tm, tn
    raise ValueError(f"no matmul tiles for {m}x{n}x{k}")


def matmul(name, a, b, *, ta=False, tb=False, res=None, out_dtype=F32, after=None):
    m, k = (a.shape[1], a.shape[0]) if ta else a.shape
    n = b.shape[0] if tb else b.shape[1]
    assert k == (b.shape[1] if tb else b.shape[0]), (name, a.shape, b.shape)
    tm, tn = _matmul_tiles(m, n, k, a.dtype.itemsize, b.dtype.itemsize, jnp.dtype(out_dtype).itemsize, res is not None, ta)
    dims = (((0 if ta else 1,), (1 if tb else 0,)), ((), ()))

    def body(*refs):
        a_ref, b_ref, o_ref = refs[0], refs[1], refs[-1]
        total = lax.dot_general(a_ref[...].astype(BF16), b_ref[...].astype(BF16), dims, preferred_element_type=F32)
        if res is not None:
            total = total + refs[2][...]
        o_ref[...] = total.astype(o_ref.dtype)

    a_spec = pl.BlockSpec((k, tm), lambda i, j: (0, i)) if ta else pl.BlockSpec((tm, k), lambda i, j: (i, 0))
    b_spec = pl.BlockSpec((tn, k), lambda i, j: (j, 0)) if tb else pl.BlockSpec((k, tn), lambda i, j: (0, j))
    o_spec = pl.BlockSpec((tm, tn), lambda i, j: (i, j))
    ins, specs = [a, b], [a_spec, b_spec]
    if res is not None:
        ins.append(res)
        specs.append(o_spec)
    if after is not None:
        ins.append(after)
        specs.append(pl.BlockSpec(memory_space=pl.ANY))
    return pl.pallas_call(
        body, name=name, grid=(m // tm, n // tn), in_specs=specs, out_specs=o_spec,
        out_shape=jax.ShapeDtypeStruct((m, n), out_dtype),
        compiler_params=_params("parallel", "parallel"),
    )(*ins)


@functools.partial(jax.custom_vjp, nondiff_argnums=(1, 2))
def _roll(x, shift, axis):
    return pltpu.roll(x, shift, axis)


def _roll_fwd(x, shift, axis):
    return pltpu.roll(x, shift, axis), None


def _roll_bwd(shift, axis, _, g):
    return (pltpu.roll(g, (g.shape[axis] - shift) % g.shape[axis], axis),)


_roll.defvjp(_roll_fwd, _roll_bwd)


def _shift_down(x, s):
    rows = lax.broadcasted_iota(jnp.int32, x.shape, 0)
    return jnp.where(rows >= s, _roll(x, s, 0), 0.0)


def _dwconv(x, w, b):
    taps = w.shape[0]
    y = b + w[taps - 1:taps, :] * x
    for kk in range(taps - 1):
        y = y + w[kk:kk + 1, :] * _shift_down(x, taps - 1 - kk)
    return y


def _rms(x, w):
    return x * lax.rsqrt(jnp.mean(x * x, -1, keepdims=True) + EPS) * w


def _rms_f(h, w):
    return (_rms(h, w),)


def _gmlp_f(uv, lng, lnb, ws, bst):
    r = lax.broadcasted_iota(jnp.int32, (GM_BLOCK, GM_BLOCK), 0) // CHUNK
    c = lax.broadcasted_iota(jnp.int32, (GM_BLOCK, GM_BLOCK), 1) // CHUNK
    outs = []
    for g in range(GM_GROUPS):
        lo, hi = g * GM_GROUP_DIM, (g + 1) * GM_GROUP_DIM
        gu = jax.nn.gelu(uv[:, lo:hi])
        gv = jax.nn.gelu(uv[:, D_MODEL + lo:D_MODEL + hi])
        xc = gv - jnp.mean(gv, -1, keepdims=True)
        var = jnp.mean(xc * xc, -1, keepdims=True)
        vn = xc * lax.rsqrt(var + EPS) * lng[:, lo:hi] + lnb[:, lo:hi]
        wm = jnp.where(r >= c, ws[g], 0.0).astype(BF16)
        gate = jnp.dot(wm, vn.astype(BF16), preferred_element_type=F32) + bst[:, g:g + 1]
        outs.append(gu * gate)
    return (jnp.concatenate(outs, axis=1),)


def _conv_silu_f(x, w, b):
    return (jax.nn.silu(_dwconv(x, w, b)),)


def _ffn_act_f(g, val, w, b):
    return (jax.nn.gelu(_dwconv(g, w, b)) * val,)


def _gate_norm_f(y, z, nw):
    y2 = y * jax.nn.silu(z)
    width = D_MODEL // SSM_GROUPS
    outs = []
    for g in range(SSM_GROUPS):
        blk = y2[:, g * width:(g + 1) * width]
        outs.append(blk * lax.rsqrt(jnp.mean(blk * blk, -1, keepdims=True) + EPS))
    return (jnp.concatenate(outs, axis=1) * nw,)


def _rope(x, cos, sin):
    lane = lax.broadcasted_iota(jnp.int32, x.shape, 1)
    half = MLA_ROPE // 2
    swapped = jnp.where(lane < half, _roll(x, LANES - half, 1), _roll(x, half, 1))
    return x * cos + swapped * sin


def _qkv_norm_f(proj, cos, sin, qn, kvn):
    cq = _rms(proj[:, :MLA_RANK], qn)
    ckv = _rms(proj[:, MLA_RANK:2 * MLA_RANK], kvn)
    kpe = _rope(proj[:, 2 * MLA_RANK:], cos, sin)
    return cq, ckv, kpe


def _attn_scores(q0, k0, qh, kn, kpe, cos, sin):
    qn = qh[:, :MLA_NOPE]
    qp = _rope(qh[:, MLA_NOPE:], cos, sin)
    nt = (((1,), (1,)), ((), ()))
    s = lax.dot_general(qn.astype(BF16), kn.astype(BF16), nt, preferred_element_type=F32)
    s = s + lax.dot_general(qp.astype(BF16), kpe.astype(BF16), nt, preferred_element_type=F32)
    s = s * (MLA_QK ** -0.5)
    qc = (q0 + lax.broadcasted_iota(jnp.int32, s.shape, 0)) // CHUNK
    kc = (k0 + lax.broadcasted_iota(jnp.int32, s.shape, 1)) // CHUNK
    return jnp.where(kc <= qc, s, -jnp.inf)


def _ssd_chunk_f(x, bm, cm, pdt, hp, sprev):
    nh, hd = SSD_HEADS_PER_STEP, SSM_HEAD_DIM
    dt = jax.nn.softplus(pdt + hp[0:1, :])
    cs = dt * (-jnp.exp(hp[1:2, :]))
    shift = 1
    while shift < CHUNK:
        cs = cs + _shift_down(cs, shift)
        shift *= 2
    cst = cs.T
    tot = cs[CHUNK - 1:CHUNK, :]

    def lanes(vals):
        return jnp.concatenate([jnp.broadcast_to(vals[:, e:e + 1], (vals.shape[0], hd)) for e in range(nh)], axis=1)

    r = lax.broadcasted_iota(jnp.int32, (CHUNK, CHUNK), 0)
    c = lax.broadcasted_iota(jnp.int32, (CHUNK, CHUNK), 1)
    tril = r >= c
    nt = (((1,), (1,)), ((), ()))
    tn = (((0,), (0,)), ((), ()))
    xd = x * lanes(dt)
    cb = lax.dot_general(cm.astype(BF16), bm.astype(BF16), nt, preferred_element_type=F32)
    ys = []
    for e in range(nh):
        decay = jnp.exp(jnp.where(tril, cs[:, e:e + 1] - cst[e:e + 1, :], -jnp.inf))
        ys.append(jnp.dot((cb * decay).astype(BF16), xd[:, e * hd:(e + 1) * hd].astype(BF16), preferred_element_type=F32))
    st = lax.dot_general((xd * lanes(jnp.exp(tot - cs))).astype(BF16), bm.astype(BF16), tn, preferred_element_type=F32)
    yoff = lax.dot_general(cm.astype(BF16), sprev.astype(BF16), nt, preferred_element_type=F32)
    y = jnp.concatenate(ys, axis=1) + yoff * lanes(jnp.exp(cs)) + lanes(hp[2:3, :]) * x
    carry = jnp.concatenate([jnp.broadcast_to(jnp.exp(tot[:, e:e + 1]), (hd, 1)) for e in range(nh)], axis=0)
    return y, carry * sprev + st


def _full_spec(a):
    nd = a.ndim
    return pl.BlockSpec(a.shape, lambda i, nd=nd: (0,) * nd)


def rows_fwd(name, f, rows, params, outs, tr, after=None):
    t = rows[0][0].shape[0]
    nr, npar = len(rows), len(params)
    extra = [] if after is None else [after]

    def body(*refs):
        vals = f(*[x[...].astype(F32) for x in refs[:nr + npar]])
        for o_ref, val in zip(refs[nr + npar + len(extra):], vals):
            o_ref[...] = val.astype(o_ref.dtype)

    in_specs = [pl.BlockSpec((tr, w), lambda i, cb=cb: (i, cb)) for _, w, cb in rows] + [_full_spec(p) for p in params]
    in_specs += [pl.BlockSpec(memory_space=pl.ANY)] * len(extra)
    out = pl.pallas_call(
        body, name=name, grid=(t // tr,), in_specs=in_specs,
        out_specs=[pl.BlockSpec((tr, w), lambda i: (i, 0)) for w, _ in outs],
        out_shape=[jax.ShapeDtypeStruct((t, w), dt) for w, dt in outs],
        compiler_params=_params("parallel"),
    )(*[a for a, _, _ in rows], *params, *extra)
    return out


def rows_bwd(name, f, rows, params, cots, tr, d_dtypes, n_nondiff=0, add=None):
    t = rows[0][0].shape[0]
    nr, npar, nc = len(rows), len(params), len(cots)
    nd = nr - n_nondiff
    has_add = add is not None
    copies = [(j, dt) for j in range(nd) for dt in (d_dtypes[j] if isinstance(d_dtypes[j], tuple) else (d_dtypes[j],))]
    ncp = len(copies)

    def body(*refs):
        i = pl.program_id(0)
        row_vals = [x[...].astype(F32) for x in refs[:nr]]
        par_vals = [x[...].astype(F32) for x in refs[nr:nr + npar]]
        cot_refs = refs[nr + npar:nr + npar + nc]
        pos = nr + npar + nc
        add_ref = refs[pos] if has_add else None
        pos += int(has_add)
        drow_refs = refs[pos:pos + ncp]
        dpar_refs = refs[pos + ncp:]

        def g(*diff):
            return f(*diff[:nd], *row_vals[nd:], *diff[nd:])

        _, vjp = jax.vjp(g, *row_vals[:nd], *par_vals)
        grads = vjp(tuple(cr[...].astype(F32) for cr in cot_refs))
        for (j, _), d_ref in zip(copies, drow_refs):
            val = grads[j]
            if j == 0 and has_add:
                val = val + add_ref[...]
            d_ref[...] = val.astype(d_ref.dtype)
        for j, d_ref in enumerate(dpar_refs):
            @pl.when(i == 0)
            def _(d_ref=d_ref, j=j):
                d_ref[...] = grads[nd + j]

            @pl.when(i > 0)
            def _(d_ref=d_ref, j=j):
                d_ref[...] += grads[nd + j]

    in_specs = [pl.BlockSpec((tr, w), lambda i, cb=cb: (i, cb)) for _, w, cb in rows] + [_full_spec(p) for p in params]
    in_specs += [pl.BlockSpec((tr, w), lambda i, cb=cb: (i, cb)) for _, w, cb in cots]
    ins = [a for a, _, _ in rows] + list(params) + [a for a, _, _ in cots]
    if has_add:
        in_specs.append(pl.BlockSpec((tr, rows[0][1]), lambda i: (i, 0)))
        ins.append(add)
    out_specs = [pl.BlockSpec((tr, rows[j][1]), lambda i: (i, 0)) for j, _ in copies] + [_full_spec(p) for p in params]
    out_shape = [jax.ShapeDtypeStruct((t, rows[j][1]), dt) for j, dt in copies]
    out_shape += [jax.ShapeDtypeStruct(p.shape, F32) for p in params]
    out = pl.pallas_call(
        body, name=name, grid=(t // tr,), in_specs=in_specs, out_specs=out_specs, out_shape=out_shape,
        compiler_params=_params("arbitrary"),
    )(*ins)
    return out[:ncp], out[ncp:]


def cols_fwd(name, f, cols, cparams, out_dtype, tc):
    t = cols[0][0].shape[0]
    width = cparams[0].shape[1]
    ncol = len(cols)

    def body(*refs):
        (val,) = f(*[x[...].astype(F32) for x in refs[:-1]])
        refs[-1][...] = val.astype(refs[-1].dtype)

    in_specs = [pl.BlockSpec((t, tc), lambda j, o=o: (0, o + j)) for _, o in cols]
    in_specs += [pl.BlockSpec((p.shape[0], tc), lambda j: (0, j)) for p in cparams]
    return pl.pallas_call(
        body, name=name, grid=(width // tc,), in_specs=in_specs,
        out_specs=pl.BlockSpec((t, tc), lambda j: (0, j)),
        out_shape=jax.ShapeDtypeStruct((t, width), out_dtype),
        compiler_params=_params("parallel"),
    )(*[a for a, _ in cols], *cparams)


def cols_bwd(name, f, cols, cparams, cot, tc, d_dtype):
    t = cols[0][0].shape[0]
    width = cparams[0].shape[1]
    ncol, npar = len(cols), len(cparams)

    def body(*refs):
        vals = [x[...].astype(F32) for x in refs[:ncol + npar]]
        _, vjp = jax.vjp(f, *vals)
        grads = vjp((refs[ncol + npar][...].astype(F32),))
        for d_ref, gval in zip(refs[ncol + npar + 1:], grads):
            d_ref[...] = gval.astype(d_ref.dtype)

    in_specs = [pl.BlockSpec((t, tc), lambda j, o=o: (0, o + j)) for _, o in cols]
    in_specs += [pl.BlockSpec((p.shape[0], tc), lambda j: (0, j)) for p in cparams]
    in_specs.append(pl.BlockSpec((t, tc), lambda j: (0, j)))
    out_specs = [pl.BlockSpec((t, tc), lambda j: (0, j)) for _ in cols]
    out_specs += [pl.BlockSpec((p.shape[0], tc), lambda j: (0, j)) for p in cparams]
    out_shape = [jax.ShapeDtypeStruct((t, width), d_dtype) for _ in cols]
    out_shape += [jax.ShapeDtypeStruct(p.shape, F32) for p in cparams]
    out = pl.pallas_call(
        body, name=name, grid=(width // tc,), in_specs=in_specs, out_specs=out_specs, out_shape=out_shape,
        compiler_params=_params("parallel"),
    )(*[a for a, _ in cols], *cparams, cot)
    return out[:ncol], out[ncol:]


def _ssd_in_specs(t):
    heads_per_group = SSM_HEADS // SSM_GROUPS
    steps_per_group = heads_per_group // SSD_HEADS_PER_STEP
    b_blk = D_MODEL // LANES
    c_blk = (D_MODEL + SSM_BC) // LANES
    return [
        pl.BlockSpec((t, SSD_X_WIDTH), lambda s: (0, s)),
        pl.BlockSpec((t, LANES), lambda s: (0, b_blk + s // steps_per_group)),
        pl.BlockSpec((t, LANES), lambda s: (0, c_blk + s // steps_per_group)),
        pl.BlockSpec((t, LANES), lambda s: (0, s)),
        pl.BlockSpec((3, LANES), lambda s: (0, s)),
    ]


def ssd_fwd(name, xa, pdt, hp):
    t = xa.shape[0]
    nc = t // CHUNK
    nh = SSD_HEADS_PER_STEP

    def body(x_ref, b_ref, c_ref, pdt_ref, hp_ref, y_ref, st_ref, s_scr):
        s_scr[...] = jnp.zeros_like(s_scr)

        def step(ci, carry):
            sl = pl.ds(pl.multiple_of(ci * CHUNK, CHUNK), CHUNK)
            sprev = s_scr[...]
            st_ref[0, ci] = sprev
            y, snew = _ssd_chunk_f(x_ref[sl, :], b_ref[sl, :], c_ref[sl, :], pdt_ref[sl, :], hp_ref[...], sprev)
            y_ref[sl, :] = y
            s_scr[...] = snew
            return carry

        lax.fori_loop(0, nc, step, 0)

    return pl.pallas_call(
        body, name=name, grid=(SSD_STEPS,), in_specs=_ssd_in_specs(t),
        out_specs=[pl.BlockSpec((t, SSD_X_WIDTH), lambda s: (0, s)),
                   pl.BlockSpec((1, nc, SSD_X_WIDTH, SSM_STATE), lambda s: (s, 0, 0, 0))],
        out_shape=[jax.ShapeDtypeStruct((t, D_MODEL), F32),
                   jax.ShapeDtypeStruct((SSD_STEPS, nc, SSD_X_WIDTH, SSM_STATE), F32)],
        scratch_shapes=[pltpu.VMEM((SSD_X_WIDTH, SSM_STATE), F32)],
        compiler_params=_params("parallel"),
    )(xa, xa, xa, pdt, hp)


def ssd_bwd(name, xa, pdt, hp, states, dy):
    t = xa.shape[0]
    nc = t // CHUNK
    nh = SSD_HEADS_PER_STEP
    steps_per_group = SSM_HEADS // SSM_GROUPS // nh

    def body(x_ref, b_ref, c_ref, pdt_ref, hp_ref, st_ref, dy_ref, dx_ref, db_ref, dc_ref, dpdt_ref, dhp_ref, ds_scr, dhp_scr):
        first = pl.program_id(0) % steps_per_group == 0
        ds_scr[...] = jnp.zeros_like(ds_scr)
        dhp_scr[...] = jnp.zeros_like(dhp_scr)

        def step(i, carry):
            ci = nc - 1 - i
            sl = pl.ds(pl.multiple_of(ci * CHUNK, CHUNK), CHUNK)
            _, vjp = jax.vjp(_ssd_chunk_f, x_ref[sl, :], b_ref[sl, :], c_ref[sl, :], pdt_ref[sl, :], hp_ref[...], st_ref[0, ci])
            dx, db, dc, dpdt, dhp, dsprev = vjp((dy_ref[sl, :], ds_scr[...]))
            dx_ref[sl, :] = dx
            dpdt_ref[sl, :] = dpdt.astype(dpdt_ref.dtype)

            @pl.when(first)
            def _():
                db_ref[sl, :] = db
                dc_ref[sl, :] = dc

            @pl.when(jnp.logical_not(first))
            def _():
                db_ref[sl, :] += db
                dc_ref[sl, :] += dc

            ds_scr[...] = dsprev
            dhp_scr[...] += dhp
            return carry

        lax.fori_loop(0, nc, step, 0)
        dhp_ref[...] = dhp_scr[...]

    in_specs = _ssd_in_specs(t) + [
        pl.BlockSpec((1, nc, SSD_X_WIDTH, SSM_STATE), lambda s: (s, 0, 0, 0)),
        pl.BlockSpec((t, SSD_X_WIDTH), lambda s: (0, s)),
    ]
    out_specs = [
        pl.BlockSpec((t, SSD_X_WIDTH), lambda s: (0, s)),
        pl.BlockSpec((t, LANES), lambda s: (0, s // steps_per_group)),
        pl.BlockSpec((t, LANES), lambda s: (0, s // steps_per_group)),
        pl.BlockSpec((t, LANES), lambda s: (0, s)),
        pl.BlockSpec((3, LANES), lambda s: (0, s)),
    ]
    out_shape = [
        jax.ShapeDtypeStruct((t, D_MODEL), F32),
        jax.ShapeDtypeStruct((t, SSM_BC), F32),
        jax.ShapeDtypeStruct((t, SSM_BC), F32),
        jax.ShapeDtypeStruct((t, SSD_STEPS * LANES), BF16),
        jax.ShapeDtypeStruct((3, SSD_STEPS * LANES), F32),
    ]
    return pl.pallas_call(
        body, name=name, grid=(SSD_STEPS,), in_specs=in_specs, out_specs=out_specs, out_shape=out_shape,
        scratch_shapes=[pltpu.VMEM((SSD_X_WIDTH, SSM_STATE), F32), pltpu.VMEM((3, LANES), F32)],
        compiler_params=_params("arbitrary"),
    )(xa, xa, xa, pdt, hp, states, dy)


ATTN_TQ = 256
ATTN_KSTEP = 512


def _attn_extents(t):
    return [min(t, (g + 1) * ATTN_KSTEP) for g in range(-(-t // ATTN_KSTEP))]


def _attn_f(q0, qh, kn, kpe, v, cos, sin):
    p = jax.nn.softmax(_attn_scores(q0, 0, qh, kn, kpe, cos, sin), axis=-1)
    return (jnp.dot(p.astype(BF16), v.astype(BF16), preferred_element_type=F32),)


def _attn_in_specs(t):
    return [
        pl.BlockSpec((ATTN_TQ, MLA_QPAD), lambda h, qi: (qi, h)),
        pl.BlockSpec((t, MLA_NOPE), lambda h, qi: (0, h)),
        pl.BlockSpec((t, LANES), lambda h, qi: (0, 0)),
        pl.BlockSpec((t, MLA_V), lambda h, qi: (0, h)),
        pl.BlockSpec((ATTN_TQ, LANES), lambda h, qi: (qi, 0)),
        pl.BlockSpec((ATTN_TQ, LANES), lambda h, qi: (qi, 0)),
    ]


def attn_fwd(name, q, kn, kpe, v, cos, sin):
    t = q.shape[0]

    def body(q_ref, kn_ref, kpe_ref, v_ref, cos_ref, sin_ref, o_ref):
        qi = pl.program_id(1)
        for span, ext in enumerate(_attn_extents(t)):
            @pl.when(qi // (ATTN_KSTEP // ATTN_TQ) == span)
            def _(ext=ext):
                (o,) = _attn_f(qi * ATTN_TQ, q_ref[...], kn_ref[0:ext, :], kpe_ref[0:ext, :], v_ref[0:ext, :],
                               cos_ref[...], sin_ref[...])
                o_ref[...] = o.astype(o_ref.dtype)

    return pl.pallas_call(
        body, name=name, grid=(MLA_HEADS, t // ATTN_TQ), in_specs=_attn_in_specs(t),
        out_specs=pl.BlockSpec((ATTN_TQ, MLA_V), lambda h, qi: (qi, h)),
        out_shape=jax.ShapeDtypeStruct((t, MLA_HEADS * MLA_V), BF16),
        compiler_params=_params("parallel", "parallel"),
    )(q, kn, kpe, v, cos, sin)


def attn_bwd(name, q, kn, kpe, v, cos, sin, do):
    t = q.shape[0]

    def body(q_ref, kn_ref, kpe_ref, v_ref, cos_ref, sin_ref, do_ref, dq_ref, dkn_ref, dkpe_ref, dv_ref):
        h, qi = pl.program_id(0), pl.program_id(1)
        q0 = qi * ATTN_TQ
        cos, sin = cos_ref[...], sin_ref[...]

        @pl.when(qi == 0)
        def _():
            dkn_ref[...] = jnp.zeros_like(dkn_ref)
            dv_ref[...] = jnp.zeros_like(dv_ref)

        @pl.when(jnp.logical_and(h == 0, qi == 0))
        def _():
            dkpe_ref[...] = jnp.zeros_like(dkpe_ref)

        for span, ext in enumerate(_attn_extents(t)):
            @pl.when(qi // (ATTN_KSTEP // ATTN_TQ) == span)
            def _(ext=ext):
                def g(qh, knv, kpev, vv):
                    return _attn_f(q0, qh, knv, kpev, vv, cos, sin)

                _, vjp = jax.vjp(g, q_ref[...].astype(F32), kn_ref[0:ext, :].astype(F32), kpe_ref[0:ext, :].astype(F32),
                                 v_ref[0:ext, :].astype(F32))
                dq, dkn, dkpe, dv = vjp((do_ref[...].astype(F32),))
                dq_ref[...] = dq.astype(dq_ref.dtype)
                dkn_ref[0:ext, :] += dkn
                dkpe_ref[0:ext, :] += dkpe
                dv_ref[0:ext, :] += dv

    in_specs = _attn_in_specs(t) + [pl.BlockSpec((ATTN_TQ, MLA_V), lambda h, qi: (qi, h))]
    out_specs = [
        pl.BlockSpec((ATTN_TQ, MLA_QPAD), lambda h, qi: (qi, h)),
        pl.BlockSpec((t, MLA_NOPE), lambda h, qi: (0, h)),
        pl.BlockSpec((t, LANES), lambda h, qi: (0, 0)),
        pl.BlockSpec((t, MLA_V), lambda h, qi: (0, h)),
    ]
    out_shape = [
        jax.ShapeDtypeStruct((t, MLA_HEADS * MLA_QPAD), BF16),
        jax.ShapeDtypeStruct((t, MLA_HEADS * MLA_NOPE), F32),
        jax.ShapeDtypeStruct((t, LANES), F32),
        jax.ShapeDtypeStruct((t, MLA_HEADS * MLA_V), F32),
    ]
    return pl.pallas_call(
        body, name=name, grid=(MLA_HEADS, t // ATTN_TQ), in_specs=in_specs, out_specs=out_specs, out_shape=out_shape,
        compiler_params=_params("arbitrary", "arbitrary"),
    )(q, kn, kpe, v, cos, sin, do)


def final_loss(name, h, nf, target, tr=256):
    t, d = h.shape

    def body(h_ref, w_ref, t_ref, loss_ref, dh_ref, dhb_ref, dw_ref):
        i = pl.program_id(0)
        tgt = t_ref[...]

        def f(hv, wv):
            err = _rms(hv, wv) - tgt
            return 0.5 * jnp.sum(jnp.mean(err * err, -1, keepdims=True), 0, keepdims=True)

        val, vjp = jax.vjp(f, h_ref[...], w_ref[...])
        dh, dw = vjp(jnp.ones((1, 1), F32))
        dh_ref[...] = dh
        dhb_ref[...] = dh.astype(dhb_ref.dtype)
        tile = jnp.broadcast_to(val, loss_ref.shape)

        @pl.when(i == 0)
        def _():
            loss_ref[...] = tile
            dw_ref[...] = dw

        @pl.when(i > 0)
        def _():
            loss_ref[...] += tile
            dw_ref[...] += dw

    row = pl.BlockSpec((tr, d), lambda i: (i, 0))
    return pl.pallas_call(
        body, name=name, grid=(t // tr,), in_specs=[row, _full_spec(nf), row],
        out_specs=[pl.BlockSpec((8, LANES), lambda i: (0, 0)), row, row, _full_spec(nf)],
        out_shape=[jax.ShapeDtypeStruct((8, LANES), F32), jax.ShapeDtypeStruct((t, d), F32), jax.ShapeDtypeStruct((t, d), BF16),
                   jax.ShapeDtypeStruct(nf.shape, F32)],
        compiler_params=_params("arbitrary"),
    )(h, nf, target)


ANY = pl.BlockSpec(memory_space=pl.ANY)
CHIP_ORDER = ((0, 0), (0, 1), (1, 0), (1, 1))


def _place():
    return lax.axis_index("x"), lax.axis_index("y"), lax.axis_index("c")


def _other_chips(x, y):
    return [(1 - x, y), (x, 1 - y), (1 - x, 1 - y)]


def _device_slot():
    x, y, c = _place()
    return 4 * x + 2 * y + c


def _row_tile(rows, cap=128):
    return next(t for t in (512, 256, 128, 64, 32, 16) if t <= cap and rows % t == 0)


def all_gather(name, shards):
    n = len(shards)

    def body(*refs):
        ins, outs = refs[:n], refs[n:2 * n]
        send_sems, recv_sems, local_sems = refs[2 * n:]
        x, y, c = _place()
        me, sibling = (x, y, c), (x, y, 1 - c)
        chips = _other_chips(x, y)

        def copy(k, j, block, to, from_input=False):
            dst = outs[k].at[4 * block[0] + 2 * block[1] + block[2]]
            return pltpu.make_async_remote_copy(
                src_ref=ins[k] if from_input else dst, dst_ref=dst,
                send_sem=send_sems.at[7 * k + j], recv_sem=recv_sems.at[7 * k + j],
                device_id=to, device_id_type=MESH)

        mine = [pltpu.make_async_copy(ins[k], outs[k].at[4 * x + 2 * y + c], local_sems.at[k]) for k in range(n)]
        for cp in mine:
            cp.start()
        first = []
        for k in range(n):
            first.append(copy(k, 0, me, sibling, True))
            first += [copy(k, 1 + j, me, (*chip, c), True) for j, chip in enumerate(chips)]
        for cp in first:
            cp.start()
        passed = []
        for j, chip in enumerate(chips):
            for k in range(n):
                copy(k, 1 + j, (*chip, c), me).wait_recv()
                fwd = copy(k, 4 + j, (*chip, c), sibling)
                fwd.start()
                passed.append(fwd)
        for k in range(n):
            copy(k, 0, sibling, me).wait_recv()
        for j, chip in enumerate(chips):
            for k in range(n):
                copy(k, 4 + j, (*chip, 1 - c), me).wait_recv()
        for cp in first + passed:
            cp.wait_send()
        for cp in mine:
            cp.wait()

    return pl.pallas_call(
        body, name=name, in_specs=[ANY] * n, out_specs=[ANY] * n,
        out_shape=[jax.ShapeDtypeStruct((N_DEV,) + s.shape, s.dtype) for s in shards],
        scratch_shapes=[pltpu.SemaphoreType.DMA((7 * n,)), pltpu.SemaphoreType.DMA((7 * n,)), pltpu.SemaphoreType.DMA((n,))],
    )(*shards)


def exchange_sibling(name, gs, after=None):
    n = len(gs)
    extra = [] if after is None else [after]

    def body(*refs):
        ins, outs = refs[:n], refs[n + len(extra):2 * n + len(extra)]
        send_sems, recv_sems = refs[2 * n + len(extra):]
        x, y, c = _place()
        copies = []
        for k in range(n):
            for q, (cx, cy) in enumerate(CHIP_ORDER):
                copies.append(pltpu.make_async_remote_copy(
                    src_ref=ins[k].at[4 * cx + 2 * cy + (1 - c)], dst_ref=outs[k].at[q],
                    send_sem=send_sems.at[4 * k + q], recv_sem=recv_sems.at[4 * k + q],
                    device_id=(x, y, 1 - c), device_id_type=MESH))
        for cp in copies:
            cp.start()
        for cp in copies:
            cp.wait()

    return pl.pallas_call(
        body, name=name, in_specs=[ANY] * (n + len(extra)), out_specs=[ANY] * n,
        out_shape=[jax.ShapeDtypeStruct((4,) + g.shape[1:], g.dtype) for g in gs],
        scratch_shapes=[pltpu.SemaphoreType.DMA((4 * n,)), pltpu.SemaphoreType.DMA((4 * n,))],
    )(*gs, *extra)


def chip_sums(name, g, recv, tr=128):
    _, r, c = g.shape

    def body(g_ref, r_ref, o_ref):
        o_ref[...] = (g_ref[...].astype(F32) + r_ref[...].astype(F32)).astype(o_ref.dtype)

    def chip(i):
        x, y, _ = _place()
        return jnp.where(i % 2 == 1, 1 - x, x), jnp.where(i >= 2, 1 - y, y)

    def g_index(i, j):
        cx, cy = chip(i)
        return 4 * cx + 2 * cy + lax.axis_index("c"), j, 0

    def recv_index(i, j):
        cx, cy = chip(i)
        return 2 * cx + cy, j, 0

    return pl.pallas_call(
        body, name=name, grid=(4, r // tr),
        in_specs=[pl.BlockSpec((1, tr, c), g_index), pl.BlockSpec((1, tr, c), recv_index)],
        out_specs=pl.BlockSpec((1, tr, c), lambda i, j: (i, j, 0)),
        out_shape=jax.ShapeDtypeStruct((4, r, c), g.dtype),
        compiler_params=_params("parallel", "parallel"),
    )(g, recv)


def sum_parts(name, parts, tr=128):
    _, r, c = parts[0][0].shape

    def body(*refs):
        total = refs[0][0].astype(F32)
        for ref in refs[1:-1]:
            total = total + ref[0].astype(F32)
        refs[-1][...] = total

    return pl.pallas_call(
        body, name=name, grid=(r // tr,),
        in_specs=[pl.BlockSpec((1, tr, c), lambda i, s=s: (s, i, 0)) for _, s in parts],
        out_specs=pl.BlockSpec((tr, c), lambda i: (i, 0)), out_shape=jax.ShapeDtypeStruct((r, c), F32),
        compiler_params=_params("parallel"),
    )(*[a for a, _ in parts])


def adamw(name, parts, w, m, v, tr=128, part=0, prev=None, after=None):
    _, r, c = parts[0][0].shape
    np_ = len(parts)
    first = part * (r // tr)

    def body(*refs):
        g = refs[0][0].astype(F32)
        for ref in refs[1:np_]:
            g = g + ref[0].astype(F32)
        w_ref, m_ref, v_ref = refs[np_:np_ + 3]
        g_out, d_out, m_out, v_out = refs[-4:]
        new_m = ADAM_B1 * m_ref[...] + (1.0 - ADAM_B1) * g
        new_v = ADAM_B2 * v_ref[...] + (1.0 - ADAM_B2) * (g * g)
        m_hat = new_m / (1.0 - ADAM_B1 ** ADAM_STEP)
        v_hat = new_v / (1.0 - ADAM_B2 ** ADAM_STEP)
        g_out[...] = g
        d_out[...] = -ADAM_LR * (m_hat / (jnp.sqrt(v_hat) + ADAM_EPS) + ADAM_WD * w_ref[...])
        m_out[...] = new_m
        v_out[...] = new_v

    tile = pl.BlockSpec((tr, c), lambda i: (first + i, 0))
    in_specs = [pl.BlockSpec((1, tr, c), lambda i, s=s: (s, i, 0)) for _, s in parts] + [tile] * 3
    ins = [a for a, _ in parts] + [w, m, v]
    aliases = {}
    if prev is not None:
        aliases = {len(ins) + k: k for k in range(4)}
        in_specs += [ANY] * 4
        ins += list(prev)
    if after is not None:
        in_specs.append(ANY)
        ins.append(after)
    return pl.pallas_call(
        body, name=name, grid=(r // tr,), in_specs=in_specs,
        out_specs=[tile] * 4, out_shape=[jax.ShapeDtypeStruct(w.shape, F32)] * 4,
        input_output_aliases=aliases, compiler_params=_params("parallel"),
    )(*ins)


HBM = pl.BlockSpec(memory_space=pltpu.HBM)
SEM = pl.BlockSpec(memory_space=pltpu.SEMAPHORE)
SIDE_EFFECT = pltpu.SideEffectType.DATAFLOW_SIDE_EFFECTING


def _split_copies(plan, src_refs, land_refs, send_sems, recv_sems):
    copies = []
    for i, (k, src_slot, land_slot, device) in enumerate(plan(*_place())):
        copies.append(pltpu.make_async_remote_copy(
            src_ref=src_refs[k] if src_slot is None else src_refs[k].at[src_slot], dst_ref=land_refs[k].at[land_slot],
            send_sem=send_sems.at[i], recv_sem=recv_sems.at[i], device_id=device, device_id_type=MESH))
    return copies


def split_start(name, srcs, land_shapes, plan, n_copies):
    n = len(srcs)

    def body(*refs):
        src_refs, land_refs = refs[:n], refs[n:2 * n]
        send_sems, recv_sems, token = refs[2 * n], refs[2 * n + 1], refs[-1]
        for cp in _split_copies(plan, src_refs, land_refs, send_sems, recv_sems):
            cp.start()
        token[...] = jnp.zeros_like(token)

    lands = [lax.empty(shape, s.dtype) for shape, s in zip(land_shapes, srcs)]
    ins = [pltpu.with_memory_space_constraint(a, pltpu.HBM) for a in list(srcs) + lands]
    out = pl.pallas_call(
        body, name=name,
        out_shape=(pltpu.SemaphoreType.DMA((n_copies,)), pltpu.SemaphoreType.DMA((n_copies,)),
                   *[pltpu.HBM(a.shape, a.dtype) for a in ins], jax.ShapeDtypeStruct((8, LANES), F32)),
        in_specs=[HBM] * (2 * n), out_specs=(SEM, SEM, *[HBM] * (2 * n), pl.BlockSpec(memory_space=pltpu.VMEM)),
        input_output_aliases={i: 2 + i for i in range(2 * n)},
        compiler_params=pltpu.CompilerParams(has_side_effects=SIDE_EFFECT),
    )(*ins)
    return out[0], out[1], list(out[2:2 + n]), list(out[2 + n:2 + 2 * n]), out[-1]


def split_wait(name, handle, plan, after):
    send_sems, recv_sems, srcs, lands, _ = handle
    n = len(srcs)

    def body(*refs):
        src_refs, land_refs = refs[:n], refs[n:2 * n]
        for cp in _split_copies(plan, src_refs, land_refs, refs[2 * n], refs[2 * n + 1]):
            cp.wait_send()
            cp.wait_recv()

    out = pl.pallas_call(
        body, name=name, out_shape=tuple(pltpu.HBM(a.shape, a.dtype) for a in srcs + lands),
        in_specs=[HBM] * (2 * n) + [SEM, SEM, ANY], out_specs=tuple([HBM] * (2 * n)),
        input_output_aliases={i: i for i in range(2 * n)},
        compiler_params=pltpu.CompilerParams(has_side_effects=SIDE_EFFECT),
    )(*srcs, *lands, send_sems, recv_sems, after)
    return list(out[:n]), list(out[n:])


def gather_plan(n):
    def plan(x, y, c):
        me = 4 * x + 2 * y + c
        peers = [(x, y, 1 - c)] + [(*chip, c) for chip in _other_chips(x, y)]
        return [(k, None, me, peer) for k in range(n) for peer in peers]
    return plan


def chips_plan(n):
    def plan(x, y, c):
        return [(k, 1 + j, j, (*chip, c)) for k in range(n) for j, chip in enumerate(_other_chips(x, y))]
    return plan


def gather_finish(name, gathered):
    n = len(gathered)

    def body(*refs):
        outs = refs[n:2 * n]
        send_sems, recv_sems = refs[2 * n:]
        x, y, c = _place()

        def passed_on(k, j, core):
            cx, cy = _other_chips(x, y)[j]
            blk = outs[k].at[4 * cx + 2 * cy + core]
            return pltpu.make_async_remote_copy(
                src_ref=blk, dst_ref=blk, send_sem=send_sems.at[3 * k + j], recv_sem=recv_sems.at[3 * k + j],
                device_id=(x, y, 1 - c), device_id_type=MESH)

        pairs = [(k, j) for k in range(n) for j in range(3)]
        sends = [passed_on(k, j, c) for k, j in pairs]
        for cp in sends:
            cp.start()
        for k, j in pairs:
            passed_on(k, j, 1 - c).wait_recv()
        for cp in sends:
            cp.wait_send()

    return pl.pallas_call(
        body, name=name, in_specs=[ANY] * n, out_specs=[ANY] * n,
        out_shape=[jax.ShapeDtypeStruct(g.shape, g.dtype) for g in gathered],
        input_output_aliases={k: k for k in range(n)},
        scratch_shapes=[pltpu.SemaphoreType.DMA((3 * n,)), pltpu.SemaphoreType.DMA((3 * n,))],
    )(*gathered)


ROW_TILE = 256
COL_TILE = 256


def _rms_fwd(tag, h, w, after=None):
    return rows_fwd(tag, _rms_f, [(h, D_MODEL, 0)], [w], [(D_MODEL, BF16)], ROW_TILE, after=after)[0]


def _rms_bwd(tag, h, w, dhn, dres):
    dh, (dw,) = rows_bwd(tag, _rms_f, [(h, D_MODEL, 0)], [w], [(dhn, D_MODEL, 0)], ROW_TILE, [(F32, BF16)], add=dres)
    return tuple(dh), dw


def even_fwd(tag, h, w, p, after=None):
    hn = _rms_fwd(tag + "_rms", h, p["nm"], after)
    uv = matmul(tag + "_uv", hn, w["uv"])
    z = matmul(tag + "_z", hn, w["z"])
    xbc = matmul(tag + "_xbc", hn, w["xbc"])
    pdt = matmul(tag + "_dt", hn, w["dt"])
    gm = [p["lng"], p["lnb"], p["ws"], p["bst"]]
    ya = rows_fwd(tag + "_gmlp", _gmlp_f, [(uv, 2 * D_MODEL, 0)], gm, [(D_MODEL, BF16)], GM_BLOCK)[0]
    xa = cols_fwd(tag + "_conv", _conv_silu_f, [(xbc, 0)], [p["cw"], p["cb"]], F32, COL_TILE)
    y, states = ssd_fwd(tag + "_ssd", xa, pdt, p["hp"])
    yb = rows_fwd(tag + "_gate", _gate_norm_f, [(y, D_MODEL, 0), (z, D_MODEL, 0)], [p["nw"]], [(D_MODEL, BF16)], ROW_TILE)[0]
    h1 = matmul(tag + "_out_b", yb, w["out_bot"], res=matmul(tag + "_out_a", ya, w["out_top"], res=h))
    return h1, dict(h=h, hn=hn, uv=uv, z=z, xbc=xbc, pdt=pdt, xa=xa, y=y, states=states, ya=ya, yb=yb)


def even_bwd(tag, dh1, s, w, p, after=None):
    dh1, dh1b = dh1
    dya = matmul(tag + "_dya", dh1b, w["out_top"], tb=True, after=after)
    dyb = matmul(tag + "_dyb", dh1b, w["out_bot"], tb=True, after=after)
    gw = dict(out_top=matmul(tag + "_gwa", s["ya"], dh1b, ta=True, out_dtype=BF16),
              out_bot=matmul(tag + "_gwb", s["yb"], dh1b, ta=True, out_dtype=BF16))
    (dy, dz), (dnw,) = rows_bwd(tag + "_gate_b", _gate_norm_f, [(s["y"], D_MODEL, 0), (s["z"], D_MODEL, 0)], [p["nw"]],
                                [(dyb, D_MODEL, 0)], ROW_TILE, [F32, BF16])
    dxs, dbm, dcm, dpdt, dhp = ssd_bwd(tag + "_ssd_b", s["xa"], s["pdt"], p["hp"], s["states"], dy)
    dxa = jnp.concatenate([dxs, dbm, dcm], axis=1)
    (dxbc,), (dcw, dcb) = cols_bwd(tag + "_conv_b", _conv_silu_f, [(s["xbc"], 0)], [p["cw"], p["cb"]], dxa, COL_TILE, BF16)
    gm = [p["lng"], p["lnb"], p["ws"], p["bst"]]
    (duv,), (dlng, dlnb, dws, dbst) = rows_bwd(tag + "_gmlp_b", _gmlp_f, [(s["uv"], 2 * D_MODEL, 0)], gm,
                                               [(dya, D_MODEL, 0)], GM_BLOCK, [BF16])
    dhn = None
    for key, d in (("uv", duv), ("z", dz), ("xbc", dxbc), ("dt", dpdt)):
        dhn = matmul(f"{tag}_dx_{key}", d, w[key], tb=True, res=dhn)
        gw[key] = matmul(f"{tag}_gw_{key}", s["hn"], d, ta=True, out_dtype=BF16)
    dh, dnm = _rms_bwd(tag + "_rms_b", s["h"], p["nm"], dhn, dh1)
    gp = dict(nm=dnm, lng=dlng, lnb=dlnb, ws=dws, bst=dbst, cw=dcw, cb=dcb, hp=dhp, nw=dnw)
    return dh, gw, gp


def odd_fwd(tag, h, w, p, cos, sin, after=None):
    hn = _rms_fwd(tag + "_rms", h, p["nm"], after)
    proj = matmul(tag + "_in", hn, w["in"])
    cq, ckv, kpe = rows_fwd(tag + "_qkvn", _qkv_norm_f, [(proj, ODD_IN_PAD, 0), (cos, LANES, 0), (sin, LANES, 0)],
                            [p["qn"], p["kvn"]], [(MLA_RANK, BF16), (MLA_RANK, BF16), (LANES, F32)], ROW_TILE)
    q = matmul(tag + "_q", cq, w["uq"])
    kn = matmul(tag + "_kn", ckv, w["kn"], out_dtype=BF16)
    v = matmul(tag + "_v", ckv, w["v"], out_dtype=BF16)
    o = attn_fwd(tag + "_attn", q, kn, kpe, v, cos, sin)
    h1 = matmul(tag + "_o", o, w["o"], res=h)
    return h1, dict(h=h, hn=hn, proj=proj, cq=cq, ckv=ckv, kpe=kpe, q=q, kn=kn, v=v, o=o)


def odd_bwd(tag, dh1, s, w, p, cos, sin, after=None):
    dh1, dh1b = dh1
    do = matmul(tag + "_do", dh1b, w["o"], tb=True, after=after)
    gw = dict(o=matmul(tag + "_gw_o", s["o"], dh1b, ta=True, out_dtype=BF16))
    dq, dkn, dkpe, dv = attn_bwd(tag + "_attn_b", s["q"], s["kn"], s["kpe"], s["v"], cos, sin, do)
    dcq = matmul(tag + "_dcq", dq, w["uq"], tb=True)
    gw["uq"] = matmul(tag + "_gw_uq", s["cq"], dq, ta=True, out_dtype=BF16)
    dckv = matmul(tag + "_dckv_v", dv, w["v"], tb=True, res=matmul(tag + "_dckv_k", dkn, w["kn"], tb=True))
    gw["kn"] = matmul(tag + "_gw_kn", s["ckv"], dkn, ta=True, out_dtype=BF16)
    gw["v"] = matmul(tag + "_gw_v", s["ckv"], dv, ta=True, out_dtype=BF16)
    (dproj,), (dqn, dkvn) = rows_bwd(
        tag + "_qkvn_b", _qkv_norm_f, [(s["proj"], ODD_IN_PAD, 0), (cos, LANES, 0), (sin, LANES, 0)], [p["qn"], p["kvn"]],
        [(dcq, MLA_RANK, 0), (dckv, MLA_RANK, 0), (dkpe, LANES, 0)], ROW_TILE, [BF16], n_nondiff=2)
    dhn = matmul(tag + "_dx_in", dproj, w["in"], tb=True)
    gw["in"] = matmul(tag + "_gw_in", s["hn"], dproj, ta=True, out_dtype=BF16)
    dh, dnm = _rms_bwd(tag + "_rms_b", s["h"], p["nm"], dhn, dh1)
    return dh, gw, dict(nm=dnm, qn=dqn, kvn=dkvn)


def ffn_fwd(tag, h, w, p, after=None):
    hn = _rms_fwd(tag + "_rms", h, p["nf"], after)
    g = matmul(tag + "_up_g", hn, w["up_g"])
    val = matmul(tag + "_up_v", hn, w["up_v"])
    act = cols_fwd(tag + "_act", _ffn_act_f, [(g, 0), (val, 0)], [p["fcw"], p["fcb"]], BF16, COL_TILE)
    h2 = matmul(tag + "_down", act, w["down"], res=h)
    return h2, dict(h=h, hn=hn, g=g, val=val, act=act)


def ffn_bwd(tag, dh2, s, w, p, after=None):
    dh2, dh2b = dh2
    dact = matmul(tag + "_dact", dh2b, w["down"], tb=True, after=after)
    gw = dict(down=matmul(tag + "_gw_down", s["act"], dh2b, ta=True, out_dtype=BF16))
    (dg, dval), (dfcw, dfcb) = cols_bwd(tag + "_act_b", _ffn_act_f, [(s["g"], 0), (s["val"], 0)], [p["fcw"], p["fcb"]],
                                        dact, COL_TILE, BF16)
    dhn = matmul(tag + "_dx_v", dval, w["up_v"], tb=True, res=matmul(tag + "_dx_g", dg, w["up_g"], tb=True))
    gw["up_g"] = matmul(tag + "_gw_up_g", s["hn"], dg, ta=True, out_dtype=BF16)
    gw["up_v"] = matmul(tag + "_gw_up_v", s["hn"], dval, ta=True, out_dtype=BF16)
    dh, dnf = _rms_bwd(tag + "_rms_b", s["h"], p["nf"], dhn, dh2)
    return dh, gw, dict(nf=dnf, fcw=dfcw, fcb=dfcb)


def _cols_from_slots(g):
    return jnp.moveaxis(g, 0, 1).reshape(g.shape[1], N_DEV * g.shape[2])


def _slots_from_cols(wmat):
    k, n = wmat.shape
    return jnp.moveaxis(wmat.reshape(k, N_DEV, n // N_DEV), 1, 0)


def _pad_last(a, width):
    return jnp.pad(a, [(0, 0)] * (a.ndim - 1) + [(0, width - a.shape[-1])])


def _heads_to_lanes(a):
    lead = a.shape[:-1]
    return _pad_last(a.reshape(lead + (SSD_STEPS, SSD_HEADS_PER_STEP)), LANES).reshape(lead + (SSD_STEPS * LANES,))


def _lanes_to_heads(a):
    lead = a.shape[:-1]
    return a.reshape(lead + (SSD_STEPS, LANES))[..., :SSD_HEADS_PER_STEP].reshape(lead + (SSM_HEADS,))


def prep_even(g_in, g_out):
    wn = _cols_from_slots(g_in)
    o1, o2, o3 = 2 * D_MODEL, 3 * D_MODEL, 3 * D_MODEL + SSM_CONV_DIM
    out = g_out.reshape(2 * D_MODEL, D_MODEL)
    return dict(uv=wn[:, :o1], z=wn[:, o1:o2], xbc=wn[:, o2:o3], dt=_heads_to_lanes(wn[:, o3:]),
                out_top=out[:D_MODEL], out_bot=out[D_MODEL:])


def unprep_even(gw):
    wn = jnp.concatenate([gw["uv"], gw["z"], gw["xbc"], _lanes_to_heads(gw["dt"])], axis=1)
    return _slots_from_cols(wn), jnp.concatenate([gw["out_top"], gw["out_bot"]], axis=0).reshape(N_DEV, -1, D_MODEL)


def prep_odd(g_in, g_uq, g_ukv, g_o):
    uq = _cols_from_slots(g_uq).reshape(MLA_RANK, MLA_HEADS, MLA_QK)
    ukv = _cols_from_slots(g_ukv).reshape(MLA_RANK, MLA_HEADS, MLA_NOPE + MLA_V)
    return dict(**{"in": _pad_last(g_in.reshape(D_MODEL, ODD_IN), ODD_IN_PAD)},
                uq=_pad_last(uq, MLA_QPAD).reshape(MLA_RANK, MLA_HEADS * MLA_QPAD),
                kn=ukv[:, :, :MLA_NOPE].reshape(MLA_RANK, MLA_HEADS * MLA_NOPE),
                v=ukv[:, :, MLA_NOPE:].reshape(MLA_RANK, MLA_HEADS * MLA_V),
                o=g_o.reshape(MLA_HEADS * MLA_V, D_MODEL))


def unprep_odd(gw):
    uq = gw["uq"].reshape(MLA_RANK, MLA_HEADS, MLA_QPAD)[:, :, :MLA_QK].reshape(MLA_RANK, MLA_HEADS * MLA_QK)
    ukv = jnp.concatenate([gw["kn"].reshape(MLA_RANK, MLA_HEADS, MLA_NOPE), gw["v"].reshape(MLA_RANK, MLA_HEADS, MLA_V)], axis=2)
    return (gw["in"][:, :ODD_IN].reshape(N_DEV, -1, ODD_IN), _slots_from_cols(uq),
            _slots_from_cols(ukv.reshape(MLA_RANK, -1)), gw["o"].reshape(N_DEV, -1, D_MODEL))


def prep_ffn(g_up, g_down):
    up = _cols_from_slots(g_up)
    return dict(up_g=up[:, :D_FF], up_v=up[:, D_FF:], down=g_down.reshape(D_FF, D_MODEL))


def unprep_ffn(gw):
    return _slots_from_cols(jnp.concatenate([gw["up_g"], gw["up_v"]], axis=1)), gw["down"].reshape(N_DEV, -1, D_MODEL)


SMALL_TILE = LANES * LANES


def _pack(arrs):
    flat = jnp.concatenate([a.reshape(-1).astype(F32) for a in arrs])
    size = -(-flat.shape[0] // SMALL_TILE) * SMALL_TILE
    return jnp.pad(flat, (0, size - flat.shape[0])).reshape(-1, LANES)


def _unpack(packed, shapes, lead=()):
    flat = packed.reshape(lead + (-1,))
    out, off = [], 0
    for shp in shapes:
        size = math.prod(shp)
        out.append(flat[..., off:off + size].reshape(lead + tuple(shp)))
        off += size
    return out


SMALL_SHARDED = {"ev_gm_ln_g": 2, "ev_gm_ln_b": 2, "ev_conv_w": 2, "od_q_norm": 1, "od_kv_norm": 1, "ff_conv_w": 2}
SMALL_REPLICATED = ["norm_mix", "norm_ffn", "norm_final", "ev_gm_ws", "ev_gm_bs", "ev_conv_b", "ev_dt_bias", "ev_a_log",
                    "ev_d_skip", "ev_ssm_norm_w", "ff_conv_b"]
MATRICES = {"ev_w_in": (2, 2048, 1156), "ev_w_out": (2, 512, 2048), "od_w_in": (2, 256, 1088), "od_w_uq": (2, 512, 384),
            "od_w_ukv": (2, 512, 512), "od_w_o": (2, 256, 2048), "ff_w_up": (4, 2048, 1408), "ff_w_down": (4, 704, 2048)}
WEIGHT_ORDER = ["norm_mix", "norm_ffn", "norm_final", "ev_w_in", "ev_gm_ln_g", "ev_gm_ln_b", "ev_gm_ws", "ev_gm_bs",
                "ev_conv_w", "ev_conv_b", "ev_dt_bias", "ev_a_log", "ev_d_skip", "ev_ssm_norm_w", "ev_w_out", "od_w_in",
                "od_q_norm", "od_kv_norm", "od_w_uq", "od_w_ukv", "od_w_o", "ff_w_up", "ff_conv_w", "ff_conv_b", "ff_w_down"]


def _full_from_shards(name, gathered):
    ax = SMALL_SHARDED[name]
    moved = jnp.moveaxis(gathered, 0, ax)
    shp = moved.shape
    return moved.reshape(shp[:ax] + (shp[ax] * shp[ax + 1],) + shp[ax + 2:])


def _my_shard(name, full, dev):
    ax = SMALL_SHARDED[name]
    shp = full.shape
    split = full.reshape(shp[:ax] + (N_DEV, shp[ax] // N_DEV) + shp[ax + 1:])
    return lax.dynamic_index_in_dim(split, dev, axis=ax, keepdims=False)


def _even_small(sm, j):
    row = lambda a: a.reshape(1, -1)
    hp = jnp.stack([sm["ev_dt_bias"][j], sm["ev_a_log"][j], sm["ev_d_skip"][j]])
    return dict(nm=row(sm["norm_mix"][2 * j]), lng=row(sm["ev_gm_ln_g"][j]), lnb=row(sm["ev_gm_ln_b"][j]),
                ws=sm["ev_gm_ws"][j], bst=sm["ev_gm_bs"][j].T, cw=sm["ev_conv_w"][j], cb=row(sm["ev_conv_b"][j]),
                hp=_heads_to_lanes(hp), nw=row(sm["ev_ssm_norm_w"][j]))


def _odd_small(sm, j):
    row = lambda a: a.reshape(1, -1)
    return dict(nm=row(sm["norm_mix"][2 * j + 1]), qn=row(sm["od_q_norm"][j]), kvn=row(sm["od_kv_norm"][j]))


def _ffn_small(sm, layer):
    row = lambda a: a.reshape(1, -1)
    return dict(nf=row(sm["norm_ffn"][layer]), fcw=sm["ff_conv_w"][layer], fcb=row(sm["ff_conv_b"][layer]))


def _rope_tables(positions):
    inv_freq = ROPE_THETA ** (-jnp.arange(0, MLA_ROPE, 2, dtype=F32) / MLA_ROPE)
    ang = positions.astype(F32).reshape(-1, 1) * inv_freq
    cos, sin = jnp.cos(ang), jnp.sin(ang)
    return _pad_last(jnp.concatenate([cos, cos], axis=1), LANES), _pad_last(jnp.concatenate([-sin, sin], axis=1), LANES)


def local_step(x, positions, target, sm, fetch_weights, emit_grads):
    cos, sin = _rope_tables(positions)
    h, saved = x, []
    for layer in range(4):
        j, tag = layer // 2, f"l{layer}"
        wm, dep = fetch_weights(2 * layer, h)
        if layer % 2 == 0:
            pm = _even_small(sm, j)
            h, sv = even_fwd(tag, h, wm, pm, dep)
        else:
            pm = _odd_small(sm, j)
            h, sv = odd_fwd(tag, h, wm, pm, cos, sin, dep)
        wf, dep = fetch_weights(2 * layer + 1, h)
        pf = _ffn_small(sm, layer)
        h, sf = ffn_fwd(tag + "f", h, wf, pf, dep)
        saved.append((pm, sv, pf, sf, wm, wf))
    loss_tile, dh32, dh16, dnfinal = final_loss("final_loss", h, sm["norm_final"].reshape(1, -1), target)
    gs = {k: [None] * v.shape[0] for k, v in sm.items() if k != "norm_final"}
    gs["norm_final"] = dnfinal.reshape(-1)
    dh, dep = (dh32, dh16), None
    for layer in reversed(range(4)):
        j, tag = layer // 2, f"l{layer}"
        pm, sv, pf, sf, wm, wf = saved[layer]
        dh, gwf, gpf = ffn_bwd(tag + "f", dh, sf, wf, pf, dep)
        gs["norm_ffn"][layer], gs["ff_conv_w"][layer], gs["ff_conv_b"][layer] = gpf["nf"][0], gpf["fcw"], gpf["fcb"][0]
        dep = emit_grads(2 * layer + 1, gwf, dh[0])
        if layer % 2 == 0:
            dh, gwm, gp = even_bwd(tag, dh, sv, wm, pm, dep)
            hp = _lanes_to_heads(gp["hp"])
            gs["norm_mix"][layer] = gp["nm"][0]
            gs["ev_gm_ln_g"][j], gs["ev_gm_ln_b"][j] = gp["lng"].reshape(GM_GROUPS, -1), gp["lnb"].reshape(GM_GROUPS, -1)
            gs["ev_gm_ws"][j], gs["ev_gm_bs"][j] = gp["ws"], gp["bst"].T
            gs["ev_conv_w"][j], gs["ev_conv_b"][j] = gp["cw"], gp["cb"][0]
            gs["ev_dt_bias"][j], gs["ev_a_log"][j], gs["ev_d_skip"][j] = hp[0], hp[1], hp[2]
            gs["ev_ssm_norm_w"][j] = gp["nw"][0]
        else:
            dh, gwm, gp = odd_bwd(tag, dh, sv, wm, pm, cos, sin, dep)
            gs["norm_mix"][layer] = gp["nm"][0]
            gs["od_q_norm"][j], gs["od_kv_norm"][j] = gp["qn"][0], gp["kvn"][0]
        if layer > 0:
            dep = emit_grads(2 * layer, gwm, dh[0])
    gs = {k: (v if k == "norm_final" else jnp.stack(v)) for k, v in gs.items()}
    return loss_tile[0, 0], dh[0], gs, gwm


def kernel(x, positions, norm_mix, norm_ffn, norm_final, ev_w_in, ev_gm_ln_g, ev_gm_ln_b, ev_gm_ws, ev_gm_bs, ev_conv_w, ev_conv_b, ev_dt_bias, ev_a_log, ev_d_skip, ev_ssm_norm_w, ev_w_out, od_w_in, od_q_norm, od_kv_norm, od_w_uq, od_w_ukv, od_w_o, ff_w_up, ff_conv_w, ff_conv_b, ff_w_down, loss_target, m_norm_mix, m_norm_ffn, m_norm_final, m_ev_w_in, m_ev_gm_ln_g, m_ev_gm_ln_b, m_ev_gm_ws, m_ev_gm_bs, m_ev_conv_w, m_ev_conv_b, m_ev_dt_bias, m_ev_a_log, m_ev_d_skip, m_ev_ssm_norm_w, m_ev_w_out, m_od_w_in, m_od_q_norm, m_od_kv_norm, m_od_w_uq, m_od_w_ukv, m_od_w_o, m_ff_w_up, m_ff_conv_w, m_ff_conv_b, m_ff_w_down, v_norm_mix, v_norm_ffn, v_norm_final, v_ev_w_in, v_ev_gm_ln_g, v_ev_gm_ln_b, v_ev_gm_ws, v_ev_gm_bs, v_ev_conv_w, v_ev_conv_b, v_ev_dt_bias, v_ev_a_log, v_ev_d_skip, v_ev_ssm_norm_w, v_ev_w_out, v_od_w_in, v_od_q_norm, v_od_kv_norm, v_od_w_uq, v_od_w_ukv, v_od_w_o, v_ff_w_up, v_ff_conv_w, v_ff_conv_b, v_ff_w_down):
    args = dict(locals())
    wts = {n: args[n] for n in WEIGHT_ORDER}
    mom = {n: args["m_" + n] for n in WEIGHT_ORDER}
    var = {n: args["v_" + n] for n in WEIGHT_ORDER}
    dev = _device_slot()

    small_names = list(SMALL_SHARDED)
    small_shapes = [wts[n].shape for n in small_names]
    (small_all,) = all_gather("ag_small", [_pack([wts[n] for n in small_names])])
    small_full = _unpack(small_all, small_shapes, lead=(N_DEV,))
    sm = {n: _full_from_shards(n, g) for n, g in zip(small_names, small_full)}
    sm.update({n: wts[n] for n in SMALL_REPLICATED})

    bf = {n: wts[n].astype(BF16) for n in MATRICES}

    def stage_matrices(stage):
        layer, is_ffn = divmod(stage, 2)
        if is_ffn:
            return [("ff_w_up", layer), ("ff_w_down", layer)]
        return [(n, layer // 2) for n in (["ev_w_in", "ev_w_out"] if layer % 2 == 0 else ["od_w_in", "od_w_uq", "od_w_ukv", "od_w_o"])]

    def stage_fns(stage):
        layer, is_ffn = divmod(stage, 2)
        if is_ffn:
            return prep_ffn, unprep_ffn
        return (prep_even, unprep_even) if layer % 2 == 0 else (prep_odd, unprep_odd)

    n_stages, ahead = 8, 2

    def start_gather(stage, earlier=None):
        shards = [bf[n][i] for n, i in stage_matrices(stage)]
        if earlier is not None:
            shards, _ = lax.optimization_barrier((shards, earlier))
        return split_start(f"ag_s{stage}_start", shards, [(N_DEV,) + s.shape for s in shards],
                           gather_plan(len(shards)), 4 * len(shards))

    gathers = {}
    for stage in range(ahead):
        gathers[stage] = start_gather(stage, gathers[stage - 1][4] if stage else None)

    def fetch_weights(stage, h):
        plan = gather_plan(len(stage_matrices(stage)))
        shards, landed = split_wait(f"ag_s{stage}_wait", gathers.pop(stage), plan, h)
        g = gather_finish(f"ag_s{stage}_finish", landed)
        g = [lax.dynamic_update_index_in_dim(gk, sk, dev, 0) for gk, sk in zip(g, shards)]
        started = None
        if stage + ahead < n_stages:
            gathers[stage + ahead] = start_gather(stage + ahead, g[0])
            started = gathers[stage + ahead][4]
        return stage_fns(stage)[0](*g), started

    scatters = []
    out = {n: None for n in MATRICES}

    held = []
    hold_below = 4

    def update(mats, sums, recv, after=None):
        for (n, i), p_, r_ in zip(mats, sums, recv):
            layers, rows, cols = MATRICES[n]
            two_d = lambda a: a.reshape(layers * rows, cols)
            out[n] = adamw(f"adamw_{n}_{i}", [(p_, 0), (r_, 0), (r_, 1), (r_, 2)], two_d(wts[n]), two_d(mom[n]),
                           two_d(var[n]), tr=_row_tile(rows), part=i, prev=out[n], after=after)
            if after is not None:
                after = out[n][0]
        return after

    def finish_scatter(after):
        stage, handle = scatters.pop(0)
        mats = stage_matrices(stage)
        sums, recv = split_wait(f"rs_s{stage}_wait", handle, chips_plan(len(mats)), after)
        if 0 < stage < hold_below:
            held.append((mats, sums, recv))
        else:
            update(mats, sums, recv)

    def emit_grads(stage, gw, dh, after=None):
        if len(scatters) >= ahead:
            finish_scatter(dh)
        send = list(stage_fns(stage)[1](gw))
        from_sibling = exchange_sibling(f"rs_s{stage}_sibling", send, after)
        sums = [chip_sums(f"rs_s{stage}_add{k}", g, r, tr=_row_tile(g.shape[1], 512)) for k, (g, r) in enumerate(zip(send, from_sibling))]
        handle = split_start(f"rs_s{stage}_start", sums, [(3,) + s.shape[1:] for s in sums], chips_plan(len(sums)), 3 * len(sums))
        scatters.append((stage, handle))
        return handle[4]

    loss_local, dx, gs, gw_first = local_step(x[0], positions[0], loss_target[0], sm, fetch_weights, emit_grads)
    loss = lax.psum(loss_local, ("x", "y", "c"))

    all_small = small_names + SMALL_REPLICATED
    (partials,) = all_gather("ar_small", [_pack([gs[n] for n in all_small])])
    last_started = emit_grads(0, gw_first, partials, after=partials)
    total = sum_parts("ar_small_sum", [(partials, s) for s in range(N_DEV)])
    g_full = dict(zip(all_small, _unpack(total, [gs[n].shape for n in all_small])))
    g_mine = {n: (_my_shard(n, g_full[n], dev) if n in SMALL_SHARDED else g_full[n]) for n in all_small}
    packed = [_pack([d[n] for n in all_small]) for d in (g_mine, wts, mom, var)]
    res = adamw("adamw_small", [(packed[0][None], 0)], packed[1], packed[2], packed[3], after=last_started)
    unpacked = [_unpack(a, [wts[n].shape for n in all_small]) for a in res]
    for i, n in enumerate(all_small):
        out[n] = [u[i] for u in unpacked]
    while len(scatters) > 1:
        finish_scatter(res[0])
    follow = res[0]
    for job in held:
        follow = update(*job, after=follow)
    finish_scatter(follow)
    for n in MATRICES:
        out[n] = [a.reshape(wts[n].shape) for a in out[n]]

    return (loss, dx[None], *[out[n][0] for n in WEIGHT_ORDER], *[out[n][1] for n in WEIGHT_ORDER],
            *[out[n][2] for n in WEIGHT_ORDER], *[out[n][3] for n in WEIGHT_ORDER])
```

```python
import functools
import math

import jax
import jax.numpy as jnp
from jax import lax
from jax.experimental import pallas as pl
from jax.experimental.pallas import tpu as pltpu

F32 = jnp.float32
BF16 = jnp.bfloat16
MESH = pl.DeviceIdType.MESH

V7X_VMEM_LIMIT_BYTES = 56 * 1024 * 1024
LANES = 128

EPS = 1e-6
D_MODEL = 2048
CHUNK = 64
GM_BLOCK = 128
GM_GROUPS = 8
GM_GROUP_DIM = D_MODEL // GM_GROUPS
SSM_HEADS = 32
SSM_HEAD_DIM = 64
SSM_GROUPS = 4
SSM_STATE = 128
SSM_CONV = 4
SSM_BC = SSM_GROUPS * SSM_STATE
SSM_CONV_DIM = D_MODEL + 2 * SSM_BC
SSD_HEADS_PER_STEP = 4
SSD_STEPS = SSM_HEADS // SSD_HEADS_PER_STEP
SSD_X_WIDTH = SSD_HEADS_PER_STEP * SSM_HEAD_DIM
MLA_HEADS = 16
MLA_RANK = 512
MLA_NOPE = 128
MLA_ROPE = 64
MLA_V = 128
MLA_QK = MLA_NOPE + MLA_ROPE
MLA_QPAD = 2 * LANES
ODD_IN = 2 * MLA_RANK + MLA_ROPE
ODD_IN_PAD = 2 * MLA_RANK + LANES
D_FF = 5632
ROPE_THETA = 10000.0
N_DEV = 8

ADAM_LR, ADAM_B1, ADAM_B2, ADAM_EPS, ADAM_WD, ADAM_STEP = 0.001, 0.9, 0.999, 1e-08, 0.01, 10


def _params(*sem):
    return pltpu.CompilerParams(dimension_semantics=sem, vmem_limit_bytes=V7X_VMEM_LIMIT_BYTES)


def _pick(dim, target):
    if dim <= target:
        return dim
    t = (target // LANES) * LANES
    while t >= LANES:
        if dim % t == 0:
            return t
        t -= LANES
    raise ValueError(f"no tile for {dim} under {target}")


MATMUL_VMEM_BUDGET = 32 * 1024 * 1024


def _matmul_tiles(m, n, k, a_bytes, b_bytes, o_bytes, has_res, ta):
    def fits(tm, tn):
        per_out = o_bytes + (4 if has_res else 0)
        return 2 * (tm * k * a_bytes + tn * k * b_bytes + tm * tn * per_out) <= MATMUL_VMEM_BUDGET

    tns = (2048, 1024, 512, 256, 128) if ta else (512, 256, 128)
    tms = (512, 256, 128) if ta else (2048, 1024, 512, 256, 128)
    for tn in tns:
        tn = _pick(n, tn)
        for tm in tms:
            tm = _pick(m, tm)
            if fits(tm, tn):
                return tm, tn
    raise ValueError(f"no matmul tiles for {m}x{n}x{k}")


def matmul(name, a, b, *, ta=False, tb=False, res=None, out_dtype=F32, after=None):
    m, k = (a.shape[1], a.shape[0]) if ta else a.shape
    n = b.shape[0] if tb else b.shape[1]
    assert k == (b.shape[1] if tb else b.shape[0]), (name, a.shape, b.shape)
    tm, tn = _matmul_tiles(m, n, k, a.dtype.itemsize, b.dtype.itemsize, jnp.dtype(out_dtype).itemsize, res is not None, ta)
    dims = (((0 if ta else 1,), (1 if tb else 0,)), ((), ()))

    def body(*refs):
        a_ref, b_ref, o_ref = refs[0], refs[1], refs[-1]
        total = lax.dot_general(a_ref[...].astype(BF16), b_ref[...].astype(BF16), dims, preferred_element_type=F32)
        if res is not None:
            total = total + refs[2][...]
        o_ref[...] = total.astype(o_ref.dtype)

    a_spec = pl.BlockSpec((k, tm), lambda i, j: (0, i)) if ta else pl.BlockSpec((tm, k), lambda i, j: (i, 0))
    b_spec = pl.BlockSpec((tn, k), lambda i, j: (j, 0)) if tb else pl.BlockSpec((k, tn), lambda i, j: (0, j))
    o_spec = pl.BlockSpec((tm, tn), lambda i, j: (i, j))
    ins, specs = [a, b], [a_spec, b_spec]
    if res is not None:
        ins.append(res)
        specs.append(o_spec)
    if after is not None:
        ins.append(after)
        specs.append(pl.BlockSpec(memory_space=pl.ANY))
    return pl.pallas_call(
        body, name=name, grid=(m // tm, n // tn), in_specs=specs, out_specs=o_spec,
        out_shape=jax.ShapeDtypeStruct((m, n), out_dtype),
        compiler_params=_params("parallel", "parallel"),
    )(*ins)


def matmul_tn_slots(name, a, bs, width, tm=512):
    k, m = a.shape
    counts = [b.shape[1] // width for b in bs]
    firsts = [sum(counts[:i]) for i in range(len(bs))]
    tm = _pick(m, tm)
    tn_dims = (((0,), (0,)), ((), ()))

    def body(*refs):
        a_ref, o_ref = refs[0], refs[-1]
        j = pl.program_id(1)
        for b_ref, first, count in zip(refs[1:-1], firsts, counts):
            @pl.when(jnp.logical_and(j >= first, j < first + count))
            def _(b_ref=b_ref):
                o_ref[0] = lax.dot_general(a_ref[...].astype(BF16), b_ref[...].astype(BF16), tn_dims,
                                           preferred_element_type=F32).astype(o_ref.dtype)

    specs = [pl.BlockSpec((k, tm), lambda i, j: (0, i))]
    specs += [pl.BlockSpec((k, width), lambda i, j, first=first, count=count: (0, jnp.clip(j - first, 0, count - 1)))
              for first, count in zip(firsts, counts)]
    return pl.pallas_call(
        body, name=name, grid=(m // tm, sum(counts)), in_specs=specs,
        out_specs=pl.BlockSpec((1, tm, width), lambda i, j: (j, i, 0)),
        out_shape=jax.ShapeDtypeStruct((sum(counts), m, width), BF16),
        compiler_params=_params("parallel", "arbitrary"),
    )(a, *bs)


@functools.partial(jax.custom_vjp, nondiff_argnums=(1, 2))
def _roll(x, shift, axis):
    return pltpu.roll(x, shift, axis)


def _roll_fwd(x, shift, axis):
    return pltpu.roll(x, shift, axis), None


def _roll_bwd(shift, axis, _, g):
    return (pltpu.roll(g, (g.shape[axis] - shift) % g.shape[axis], axis),)


_roll.defvjp(_roll_fwd, _roll_bwd)


def _shift_down(x, s):
    rows = lax.broadcasted_iota(jnp.int32, x.shape, 0)
    return jnp.where(rows >= s, _roll(x, s, 0), 0.0)


def _dwconv(x, w, b):
    taps = w.shape[0]
    y = b + w[taps - 1:taps, :] * x
    for kk in range(taps - 1):
        y = y + w[kk:kk + 1, :] * _shift_down(x, taps - 1 - kk)
    return y


def _rms(x, w):
    return x * lax.rsqrt(jnp.mean(x * x, -1, keepdims=True) + EPS) * w


def _rms_f(h, w):
    return (_rms(h, w),)


def _gmlp_f(uv, lng, lnb, ws, bst):
    r = lax.broadcasted_iota(jnp.int32, (GM_BLOCK, GM_BLOCK), 0) // CHUNK
    c = lax.broadcasted_iota(jnp.int32, (GM_BLOCK, GM_BLOCK), 1) // CHUNK
    outs = []
    for g in range(GM_GROUPS):
        lo, hi = g * GM_GROUP_DIM, (g + 1) * GM_GROUP_DIM
        gu = jax.nn.gelu(uv[:, lo:hi])
        gv = jax.nn.gelu(uv[:, D_MODEL + lo:D_MODEL + hi])
        xc = gv - jnp.mean(gv, -1, keepdims=True)
        var = jnp.mean(xc * xc, -1, keepdims=True)
        vn = xc * lax.rsqrt(var + EPS) * lng[:, lo:hi] + lnb[:, lo:hi]
        wm = jnp.where(r >= c, ws[g], 0.0).astype(BF16)
        gate = jnp.dot(wm, vn.astype(BF16), preferred_element_type=F32) + bst[:, g:g + 1]
        outs.append(gu * gate)
    return (jnp.concatenate(outs, axis=1),)


def _conv_silu_f(x, w, b):
    return (jax.nn.silu(_dwconv(x, w, b)),)


def _ffn_act_f(g, val, w, b):
    return (jax.nn.gelu(_dwconv(g, w, b)) * val,)


def _gate_norm_f(y, z, nw):
    y2 = y * jax.nn.silu(z)
    width = D_MODEL // SSM_GROUPS
    outs = []
    for g in range(SSM_GROUPS):
        blk = y2[:, g * width:(g + 1) * width]
        outs.append(blk * lax.rsqrt(jnp.mean(blk * blk, -1, keepdims=True) + EPS))
    return (jnp.concatenate(outs, axis=1) * nw,)


def _rope(x, cos, sin):
    lane = lax.broadcasted_iota(jnp.int32, x.shape, 1)
    half = MLA_ROPE // 2
    swapped = jnp.where(lane < half, _roll(x, LANES - half, 1), _roll(x, half, 1))
    return x * cos + swapped * sin


def _qkv_norm_f(proj, cos, sin, qn, kvn):
    cq = _rms(proj[:, :MLA_RANK], qn)
    ckv = _rms(proj[:, MLA_RANK:2 * MLA_RANK], kvn)
    kpe = _rope(proj[:, 2 * MLA_RANK:], cos, sin)
    return cq, ckv, kpe


def _attn_scores(q0, k0, qh, kn, kpe, cos, sin):
    qn = qh[:, :MLA_NOPE]
    qp = _rope(qh[:, MLA_NOPE:], cos, sin)
    nt = (((1,), (1,)), ((), ()))
    s = lax.dot_general(qn.astype(BF16), kn.astype(BF16), nt, preferred_element_type=F32)
    s = s + lax.dot_general(qp.astype(BF16), kpe.astype(BF16), nt, preferred_element_type=F32)
    s = s * (MLA_QK ** -0.5)
    visible_below = ((q0 + lax.broadcasted_iota(jnp.int32, (s.shape[0], 1), 0)) // CHUNK + 1) * CHUNK - k0
    return jnp.where(lax.broadcasted_iota(jnp.int32, s.shape, 1) < visible_below, s, -jnp.inf)


def _ssd_chunk_f(x, bm, cm, pdt, hp, sprev):
    nh, hd = SSD_HEADS_PER_STEP, SSM_HEAD_DIM
    dt = jax.nn.softplus(pdt + hp[0:1, :])
    cs = dt * (-jnp.exp(hp[1:2, :]))
    shift = 1
    while shift < CHUNK:
        cs = cs + _shift_down(cs, shift)
        shift *= 2
    cst = cs.T
    tot = cs[CHUNK - 1:CHUNK, :]

    def lanes(vals):
        return jnp.concatenate([jnp.broadcast_to(vals[:, e:e + 1], (vals.shape[0], hd)) for e in range(nh)], axis=1)

    r = lax.broadcasted_iota(jnp.int32, (CHUNK, CHUNK), 0)
    c = lax.broadcasted_iota(jnp.int32, (CHUNK, CHUNK), 1)
    tril = r >= c
    nt = (((1,), (1,)), ((), ()))
    tn = (((0,), (0,)), ((), ()))
    xd = x * lanes(dt)
    cb = lax.dot_general(cm.astype(BF16), bm.astype(BF16), nt, preferred_element_type=F32)
    ys = []
    for e in range(nh):
        decay = jnp.exp(jnp.where(tril, cs[:, e:e + 1] - cst[e:e + 1, :], -jnp.inf))
        ys.append(jnp.dot((cb * decay).astype(BF16), xd[:, e * hd:(e + 1) * hd].astype(BF16), preferred_element_type=F32))
    st = lax.dot_general((xd * lanes(jnp.exp(tot - cs))).astype(BF16), bm.astype(BF16), tn, preferred_element_type=F32)
    yoff = lax.dot_general(cm.astype(BF16), sprev.astype(BF16), nt, preferred_element_type=F32)
    y = jnp.concatenate(ys, axis=1) + yoff * lanes(jnp.exp(cs)) + lanes(hp[2:3, :]) * x
    carry = jnp.concatenate([jnp.broadcast_to(jnp.exp(tot[:, e:e + 1]), (hd, 1)) for e in range(nh)], axis=0)
    return y, carry * sprev + st


def _full_spec(a):
    nd = a.ndim
    return pl.BlockSpec(a.shape, lambda i, nd=nd: (0,) * nd)


def rows_fwd(name, f, rows, params, outs, tr, after=None):
    t = rows[0][0].shape[0]
    nr, npar = len(rows), len(params)
    extra = [] if after is None else [after]

    def body(*refs):
        vals = f(*[x[...].astype(F32) for x in refs[:nr + npar]])
        for o_ref, val in zip(refs[nr + npar + len(extra):], vals):
            o_ref[...] = val.astype(o_ref.dtype)

    in_specs = [pl.BlockSpec((tr, w), lambda i, cb=cb: (i, cb)) for _, w, cb in rows] + [_full_spec(p) for p in params]
    in_specs += [pl.BlockSpec(memory_space=pl.ANY)] * len(extra)
    out = pl.pallas_call(
        body, name=name, grid=(t // tr,), in_specs=in_specs,
        out_specs=[pl.BlockSpec((tr, w), lambda i: (i, 0)) for w, _ in outs],
        out_shape=[jax.ShapeDtypeStruct((t, w), dt) for w, dt in outs],
        compiler_params=_params("parallel"),
    )(*[a for a, _, _ in rows], *params, *extra)
    return out


def rows_bwd(name, f, rows, params, cots, tr, d_dtypes, n_nondiff=0, add=None):
    t = rows[0][0].shape[0]
    nr, npar, nc = len(rows), len(params), len(cots)
    nd = nr - n_nondiff
    has_add = add is not None
    copies = [(j, dt) for j in range(nd) for dt in (d_dtypes[j] if isinstance(d_dtypes[j], tuple) else (d_dtypes[j],))]
    ncp = len(copies)

    def body(*refs):
        i = pl.program_id(0)
        row_vals = [x[...].astype(F32) for x in refs[:nr]]
        par_vals = [x[...].astype(F32) for x in refs[nr:nr + npar]]
        cot_refs = refs[nr + npar:nr + npar + nc]
        pos = nr + npar + nc
        add_ref = refs[pos] if has_add else None
        pos += int(has_add)
        drow_refs = refs[pos:pos + ncp]
        dpar_refs = refs[pos + ncp:]

        def g(*diff):
            return f(*diff[:nd], *row_vals[nd:], *diff[nd:])

        _, vjp = jax.vjp(g, *row_vals[:nd], *par_vals)
        grads = vjp(tuple(cr[...].astype(F32) for cr in cot_refs))
        for (j, _), d_ref in zip(copies, drow_refs):
            val = grads[j]
            if j == 0 and has_add:
                val = val + add_ref[...]
            d_ref[...] = val.astype(d_ref.dtype)
        for j, d_ref in enumerate(dpar_refs):
            @pl.when(i == 0)
            def _(d_ref=d_ref, j=j):
                d_ref[...] = grads[nd + j]

            @pl.when(i > 0)
            def _(d_ref=d_ref, j=j):
                d_ref[...] += grads[nd + j]

    in_specs = [pl.BlockSpec((tr, w), lambda i, cb=cb: (i, cb)) for _, w, cb in rows] + [_full_spec(p) for p in params]
    in_specs += [pl.BlockSpec((tr, w), lambda i, cb=cb: (i, cb)) for _, w, cb in cots]
    ins = [a for a, _, _ in rows] + list(params) + [a for a, _, _ in cots]
    if has_add:
        in_specs.append(pl.BlockSpec((tr, rows[0][1]), lambda i: (i, 0)))
        ins.append(add)
    out_specs = [pl.BlockSpec((tr, rows[j][1]), lambda i: (i, 0)) for j, _ in copies] + [_full_spec(p) for p in params]
    out_shape = [jax.ShapeDtypeStruct((t, rows[j][1]), dt) for j, dt in copies]
    out_shape += [jax.ShapeDtypeStruct(p.shape, F32) for p in params]
    out = pl.pallas_call(
        body, name=name, grid=(t // tr,), in_specs=in_specs, out_specs=out_specs, out_shape=out_shape,
        compiler_params=_params("arbitrary"),
    )(*ins)
    return out[:ncp], out[ncp:]


def cols_fwd(name, f, cols, cparams, out_dtype, tc):
    t = cols[0][0].shape[0]
    width = cparams[0].shape[1]
    ncol = len(cols)

    def body(*refs):
        (val,) = f(*[x[...].astype(F32) for x in refs[:-1]])
        refs[-1][...] = val.astype(refs[-1].dtype)

    in_specs = [pl.BlockSpec((t, tc), lambda j, o=o: (0, o + j)) for _, o in cols]
    in_specs += [pl.BlockSpec((p.shape[0], tc), lambda j: (0, j)) for p in cparams]
    return pl.pallas_call(
        body, name=name, grid=(width // tc,), in_specs=in_specs,
        out_specs=pl.BlockSpec((t, tc), lambda j: (0, j)),
        out_shape=jax.ShapeDtypeStruct((t, width), out_dtype),
        compiler_params=_params("parallel"),
    )(*[a for a, _ in cols], *cparams)


def cols_bwd(name, f, cols, cparams, cot, tc, d_dtype):
    t = cols[0][0].shape[0]
    width = cparams[0].shape[1]
    ncol, npar = len(cols), len(cparams)

    def body(*refs):
        vals = [x[...].astype(F32) for x in refs[:ncol + npar]]
        _, vjp = jax.vjp(f, *vals)
        grads = vjp((refs[ncol + npar][...].astype(F32),))
        for d_ref, gval in zip(refs[ncol + npar + 1:], grads):
            d_ref[...] = gval.astype(d_ref.dtype)

    in_specs = [pl.BlockSpec((t, tc), lambda j, o=o: (0, o + j)) for _, o in cols]
    in_specs += [pl.BlockSpec((p.shape[0], tc), lambda j: (0, j)) for p in cparams]
    in_specs.append(pl.BlockSpec((t, tc), lambda j: (0, j)))
    out_specs = [pl.BlockSpec((t, tc), lambda j: (0, j)) for _ in cols]
    out_specs += [pl.BlockSpec((p.shape[0], tc), lambda j: (0, j)) for p in cparams]
    out_shape = [jax.ShapeDtypeStruct((t, width), d_dtype) for _ in cols]
    out_shape += [jax.ShapeDtypeStruct(p.shape, F32) for p in cparams]
    out = pl.pallas_call(
        body, name=name, grid=(width // tc,), in_specs=in_specs, out_specs=out_specs, out_shape=out_shape,
        compiler_params=_params("parallel"),
    )(*[a for a, _ in cols], *cparams, cot)
    return out[:ncol], out[ncol:]


def _ssd_in_specs(t):
    heads_per_group = SSM_HEADS // SSM_GROUPS
    steps_per_group = heads_per_group // SSD_HEADS_PER_STEP
    b_blk = D_MODEL // LANES
    c_blk = (D_MODEL + SSM_BC) // LANES
    return [
        pl.BlockSpec((t, SSD_X_WIDTH), lambda s: (0, s)),
        pl.BlockSpec((t, LANES), lambda s: (0, b_blk + s // steps_per_group)),
        pl.BlockSpec((t, LANES), lambda s: (0, c_blk + s // steps_per_group)),
        pl.BlockSpec((t, LANES), lambda s: (0, s)),
        pl.BlockSpec((3, LANES), lambda s: (0, s)),
    ]


def ssd_fwd(name, xa, pdt, hp):
    t = xa.shape[0]
    nc = t // CHUNK
    nh = SSD_HEADS_PER_STEP

    def body(x_ref, b_ref, c_ref, pdt_ref, hp_ref, y_ref, st_ref, s_scr):
        s_scr[...] = jnp.zeros_like(s_scr)

        def step(ci, carry):
            sl = pl.ds(pl.multiple_of(ci * CHUNK, CHUNK), CHUNK)
            sprev = s_scr[...]
            st_ref[0, ci] = sprev
            y, snew = _ssd_chunk_f(x_ref[sl, :], b_ref[sl, :], c_ref[sl, :], pdt_ref[sl, :], hp_ref[...], sprev)
            y_ref[sl, :] = y
            s_scr[...] = snew
            return carry

        lax.fori_loop(0, nc, step, 0)

    return pl.pallas_call(
        body, name=name, grid=(SSD_STEPS,), in_specs=_ssd_in_specs(t),
        out_specs=[pl.BlockSpec((t, SSD_X_WIDTH), lambda s: (0, s)),
                   pl.BlockSpec((1, nc, SSD_X_WIDTH, SSM_STATE), lambda s: (s, 0, 0, 0))],
        out_shape=[jax.ShapeDtypeStruct((t, D_MODEL), F32),
                   jax.ShapeDtypeStruct((SSD_STEPS, nc, SSD_X_WIDTH, SSM_STATE), F32)],
        scratch_shapes=[pltpu.VMEM((SSD_X_WIDTH, SSM_STATE), F32)],
        compiler_params=_params("parallel"),
    )(xa, xa, xa, pdt, hp)


def ssd_bwd(name, xa, pdt, hp, states, dy):
    t = xa.shape[0]
    nc = t // CHUNK
    nh = SSD_HEADS_PER_STEP
    steps_per_group = SSM_HEADS // SSM_GROUPS // nh

    def body(x_ref, b_ref, c_ref, pdt_ref, hp_ref, st_ref, dy_ref, dx_ref, db_ref, dc_ref, dpdt_ref, dhp_ref, ds_scr, dhp_scr):
        first = pl.program_id(0) % steps_per_group == 0
        ds_scr[...] = jnp.zeros_like(ds_scr)
        dhp_scr[...] = jnp.zeros_like(dhp_scr)

        def step(i, carry):
            ci = nc - 1 - i
            sl = pl.ds(pl.multiple_of(ci * CHUNK, CHUNK), CHUNK)
            _, vjp = jax.vjp(_ssd_chunk_f, x_ref[sl, :], b_ref[sl, :], c_ref[sl, :], pdt_ref[sl, :], hp_ref[...], st_ref[0, ci])
            dx, db, dc, dpdt, dhp, dsprev = vjp((dy_ref[sl, :], ds_scr[...]))
            dx_ref[sl, :] = dx
            dpdt_ref[sl, :] = dpdt.astype(dpdt_ref.dtype)

            @pl.when(first)
            def _():
                db_ref[sl, :] = db
                dc_ref[sl, :] = dc

            @pl.when(jnp.logical_not(first))
            def _():
                db_ref[sl, :] += db
                dc_ref[sl, :] += dc

            ds_scr[...] = dsprev
            dhp_scr[...] += dhp
            return carry

        lax.fori_loop(0, nc, step, 0)
        dhp_ref[...] = dhp_scr[...]

    in_specs = _ssd_in_specs(t) + [
        pl.BlockSpec((1, nc, SSD_X_WIDTH, SSM_STATE), lambda s: (s, 0, 0, 0)),
        pl.BlockSpec((t, SSD_X_WIDTH), lambda s: (0, s)),
    ]
    out_specs = [
        pl.BlockSpec((t, SSD_X_WIDTH), lambda s: (0, s)),
        pl.BlockSpec((t, LANES), lambda s: (0, s // steps_per_group)),
        pl.BlockSpec((t, LANES), lambda s: (0, s // steps_per_group)),
        pl.BlockSpec((t, LANES), lambda s: (0, s)),
        pl.BlockSpec((3, LANES), lambda s: (0, s)),
    ]
    out_shape = [
        jax.ShapeDtypeStruct((t, D_MODEL), F32),
        jax.ShapeDtypeStruct((t, SSM_BC), F32),
        jax.ShapeDtypeStruct((t, SSM_BC), F32),
        jax.ShapeDtypeStruct((t, SSD_STEPS * LANES), BF16),
        jax.ShapeDtypeStruct((3, SSD_STEPS * LANES), F32),
    ]
    return pl.pallas_call(
        body, name=name, grid=(SSD_STEPS,), in_specs=in_specs, out_specs=out_specs, out_shape=out_shape,
        scratch_shapes=[pltpu.VMEM((SSD_X_WIDTH, SSM_STATE), F32), pltpu.VMEM((3, LANES), F32)],
        compiler_params=_params("arbitrary"),
    )(xa, xa, xa, pdt, hp, states, dy)


ATTN_TQ = 256
ATTN_KSTEP = 512


def _attn_extents(t):
    return [min(t, (g + 1) * ATTN_KSTEP) for g in range(-(-t // ATTN_KSTEP))]


def _attn_f(q0, qh, kn, kpe, v, cos, sin):
    p = jax.nn.softmax(_attn_scores(q0, 0, qh, kn, kpe, cos, sin), axis=-1)
    return (jnp.dot(p.astype(BF16), v.astype(BF16), preferred_element_type=F32),)


def _attn_in_specs(t):
    return [
        pl.BlockSpec((ATTN_TQ, MLA_QPAD), lambda h, qi: (qi, h)),
        pl.BlockSpec((t, MLA_NOPE), lambda h, qi: (0, h)),
        pl.BlockSpec((t, LANES), lambda h, qi: (0, 0)),
        pl.BlockSpec((t, MLA_V), lambda h, qi: (0, h)),
        pl.BlockSpec((ATTN_TQ, LANES), lambda h, qi: (qi, 0)),
        pl.BlockSpec((ATTN_TQ, LANES), lambda h, qi: (qi, 0)),
    ]


def attn_fwd(name, q, kn, kpe, v, cos, sin):
    t = q.shape[0]

    def body(q_ref, kn_ref, kpe_ref, v_ref, cos_ref, sin_ref, o_ref):
        qi = pl.program_id(1)
        for span, ext in enumerate(_attn_extents(t)):
            @pl.when(qi // (ATTN_KSTEP // ATTN_TQ) == span)
            def _(ext=ext):
                (o,) = _attn_f(qi * ATTN_TQ, q_ref[...], kn_ref[0:ext, :], kpe_ref[0:ext, :], v_ref[0:ext, :],
                               cos_ref[...], sin_ref[...])
                o_ref[...] = o.astype(o_ref.dtype)

    return pl.pallas_call(
        body, name=name, grid=(MLA_HEADS, t // ATTN_TQ), in_specs=_attn_in_specs(t),
        out_specs=pl.BlockSpec((ATTN_TQ, MLA_V), lambda h, qi: (qi, h)),
        out_shape=jax.ShapeDtypeStruct((t, MLA_HEADS * MLA_V), BF16),
        compiler_params=_params("parallel", "parallel"),
    )(q, kn, kpe, v, cos, sin)


def attn_bwd(name, q, kn, kpe, v, cos, sin, do):
    t = q.shape[0]

    def body(q_ref, kn_ref, kpe_ref, v_ref, cos_ref, sin_ref, do_ref, dq_ref, dkn_ref, dkpe_ref, dv_ref):
        h, qi = pl.program_id(0), pl.program_id(1)
        q0 = qi * ATTN_TQ
        cos, sin = cos_ref[...], sin_ref[...]

        @pl.when(qi == 0)
        def _():
            dkn_ref[...] = jnp.zeros_like(dkn_ref)
            dv_ref[...] = jnp.zeros_like(dv_ref)

        @pl.when(jnp.logical_and(h == 0, qi == 0))
        def _():
            dkpe_ref[...] = jnp.zeros_like(dkpe_ref)

        for span, ext in enumerate(_attn_extents(t)):
            @pl.when(qi // (ATTN_KSTEP // ATTN_TQ) == span)
            def _(ext=ext):
                def g(qh, knv, kpev, vv):
                    return _attn_f(q0, qh, knv, kpev, vv, cos, sin)

                _, vjp = jax.vjp(g, q_ref[...].astype(F32), kn_ref[0:ext, :].astype(F32), kpe_ref[0:ext, :].astype(F32),
                                 v_ref[0:ext, :].astype(F32))
                dq, dkn, dkpe, dv = vjp((do_ref[...].astype(F32),))
                dq_ref[...] = dq.astype(dq_ref.dtype)
                dkn_ref[0:ext, :] += dkn
                dkpe_ref[0:ext, :] += dkpe
                dv_ref[0:ext, :] += dv

    in_specs = _attn_in_specs(t) + [pl.BlockSpec((ATTN_TQ, MLA_V), lambda h, qi: (qi, h))]
    out_specs = [
        pl.BlockSpec((ATTN_TQ, MLA_QPAD), lambda h, qi: (qi, h)),
        pl.BlockSpec((t, MLA_NOPE), lambda h, qi: (0, h)),
        pl.BlockSpec((t, LANES), lambda h, qi: (0, 0)),
        pl.BlockSpec((t, MLA_V), lambda h, qi: (0, h)),
    ]
    out_shape = [
        jax.ShapeDtypeStruct((t, MLA_HEADS * MLA_QPAD), BF16),
        jax.ShapeDtypeStruct((t, MLA_HEADS * MLA_NOPE), F32),
        jax.ShapeDtypeStruct((t, LANES), F32),
        jax.ShapeDtypeStruct((t, MLA_HEADS * MLA_V), F32),
    ]
    return pl.pallas_call(
        body, name=name, grid=(MLA_HEADS, t // ATTN_TQ), in_specs=in_specs, out_specs=out_specs, out_shape=out_shape,
        compiler_params=_params("arbitrary", "arbitrary"),
    )(q, kn, kpe, v, cos, sin, do)


def final_loss(name, h, nf, target, tr=256):
    t, d = h.shape

    def body(h_ref, w_ref, t_ref, loss_ref, dh_ref, dhb_ref, dw_ref):
        i = pl.program_id(0)
        tgt = t_ref[...]

        def f(hv, wv):
            err = _rms(hv, wv) - tgt
            return 0.5 * jnp.sum(jnp.mean(err * err, -1, keepdims=True), 0, keepdims=True)

        val, vjp = jax.vjp(f, h_ref[...], w_ref[...])
        dh, dw = vjp(jnp.ones((1, 1), F32))
        dh_ref[...] = dh
        dhb_ref[...] = dh.astype(dhb_ref.dtype)
        tile = jnp.broadcast_to(val, loss_ref.shape)

        @pl.when(i == 0)
        def _():
            loss_ref[...] = tile
            dw_ref[...] = dw

        @pl.when(i > 0)
        def _():
            loss_ref[...] += tile
            dw_ref[...] += dw

    row = pl.BlockSpec((tr, d), lambda i: (i, 0))
    return pl.pallas_call(
        body, name=name, grid=(t // tr,), in_specs=[row, _full_spec(nf), row],
        out_specs=[pl.BlockSpec((8, LANES), lambda i: (0, 0)), row, row, _full_spec(nf)],
        out_shape=[jax.ShapeDtypeStruct((8, LANES), F32), jax.ShapeDtypeStruct((t, d), F32), jax.ShapeDtypeStruct((t, d), BF16),
                   jax.ShapeDtypeStruct(nf.shape, F32)],
        compiler_params=_params("arbitrary"),
    )(h, nf, target)


ANY = pl.BlockSpec(memory_space=pl.ANY)
CHIP_ORDER = ((0, 0), (0, 1), (1, 0), (1, 1))


def _place():
    return lax.axis_index("x"), lax.axis_index("y"), lax.axis_index("c")


def _other_chips(x, y):
    return [(1 - x, y), (x, 1 - y), (1 - x, 1 - y)]


def _device_slot():
    x, y, c = _place()
    return 4 * x + 2 * y + c


def _row_tile(rows, cap=128):
    return next(t for t in (512, 256, 128, 64, 32, 16) if t <= cap and rows % t == 0)


def all_gather(name, shards):
    n = len(shards)

    def body(*refs):
        ins, outs = refs[:n], refs[n:2 * n]
        send_sems, recv_sems, local_sems = refs[2 * n:]
        x, y, c = _place()
        me, sibling = (x, y, c), (x, y, 1 - c)
        chips = _other_chips(x, y)

        def copy(k, j, block, to, from_input=False):
            dst = outs[k].at[4 * block[0] + 2 * block[1] + block[2]]
            return pltpu.make_async_remote_copy(
                src_ref=ins[k] if from_input else dst, dst_ref=dst,
                send_sem=send_sems.at[7 * k + j], recv_sem=recv_sems.at[7 * k + j],
                device_id=to, device_id_type=MESH)

        mine = [pltpu.make_async_copy(ins[k], outs[k].at[4 * x + 2 * y + c], local_sems.at[k]) for k in range(n)]
        for cp in mine:
            cp.start()
        first = []
        for k in range(n):
            first.append(copy(k, 0, me, sibling, True))
            first += [copy(k, 1 + j, me, (*chip, c), True) for j, chip in enumerate(chips)]
        for cp in first:
            cp.start()
        passed = []
        for j, chip in enumerate(chips):
            for k in range(n):
                copy(k, 1 + j, (*chip, c), me).wait_recv()
                fwd = copy(k, 4 + j, (*chip, c), sibling)
                fwd.start()
                passed.append(fwd)
        for k in range(n):
            copy(k, 0, sibling, me).wait_recv()
        for j, chip in enumerate(chips):
            for k in range(n):
                copy(k, 4 + j, (*chip, 1 - c), me).wait_recv()
        for cp in first + passed:
            cp.wait_send()
        for cp in mine:
            cp.wait()

    return pl.pallas_call(
        body, name=name, in_specs=[ANY] * n, out_specs=[ANY] * n,
        out_shape=[jax.ShapeDtypeStruct((N_DEV,) + s.shape, s.dtype) for s in shards],
        scratch_shapes=[pltpu.SemaphoreType.DMA((7 * n,)), pltpu.SemaphoreType.DMA((7 * n,)), pltpu.SemaphoreType.DMA((n,))],
    )(*shards)


def exchange_sibling(name, gs, after=None):
    n = len(gs)
    extra = [] if after is None else [after]

    def body(*refs):
        ins, outs = refs[:n], refs[n + len(extra):2 * n + len(extra)]
        send_sems, recv_sems = refs[2 * n + len(extra):]
        x, y, c = _place()
        copies = []
        for k in range(n):
            for q, (cx, cy) in enumerate(CHIP_ORDER):
                copies.append(pltpu.make_async_remote_copy(
                    src_ref=ins[k].at[4 * cx + 2 * cy + (1 - c)], dst_ref=outs[k].at[q],
                    send_sem=send_sems.at[4 * k + q], recv_sem=recv_sems.at[4 * k + q],
                    device_id=(x, y, 1 - c), device_id_type=MESH))
        for cp in copies:
            cp.start()
        for cp in copies:
            cp.wait()

    return pl.pallas_call(
        body, name=name, in_specs=[ANY] * (n + len(extra)), out_specs=[ANY] * n,
        out_shape=[jax.ShapeDtypeStruct((4,) + g.shape[1:], g.dtype) for g in gs],
        scratch_shapes=[pltpu.SemaphoreType.DMA((4 * n,)), pltpu.SemaphoreType.DMA((4 * n,))],
    )(*gs, *extra)


def chip_sums(name, g, recv, tr=128):
    _, r, c = g.shape

    def body(g_ref, r_ref, o_ref):
        o_ref[...] = (g_ref[...].astype(F32) + r_ref[...].astype(F32)).astype(o_ref.dtype)

    def chip(i):
        x, y, _ = _place()
        return jnp.where(i % 2 == 1, 1 - x, x), jnp.where(i >= 2, 1 - y, y)

    def g_index(i, j):
        cx, cy = chip(i)
        return 4 * cx + 2 * cy + lax.axis_index("c"), j, 0

    def recv_index(i, j):
        cx, cy = chip(i)
        return 2 * cx + cy, j, 0

    return pl.pallas_call(
        body, name=name, grid=(4, r // tr),
        in_specs=[pl.BlockSpec((1, tr, c), g_index), pl.BlockSpec((1, tr, c), recv_index)],
        out_specs=pl.BlockSpec((1, tr, c), lambda i, j: (i, j, 0)),
        out_shape=jax.ShapeDtypeStruct((4, r, c), g.dtype),
        compiler_params=_params("parallel", "parallel"),
    )(g, recv)


def sum_parts(name, parts, tr=128):
    _, r, c = parts[0][0].shape

    def body(*refs):
        total = refs[0][0].astype(F32)
        for ref in refs[1:-1]:
            total = total + ref[0].astype(F32)
        refs[-1][...] = total

    return pl.pallas_call(
        body, name=name, grid=(r // tr,),
        in_specs=[pl.BlockSpec((1, tr, c), lambda i, s=s: (s, i, 0)) for _, s in parts],
        out_specs=pl.BlockSpec((tr, c), lambda i: (i, 0)), out_shape=jax.ShapeDtypeStruct((r, c), F32),
        compiler_params=_params("parallel"),
    )(*[a for a, _ in parts])


def adamw(name, parts, w, m, v, tr=128, part=0, prev=None, after=None):
    _, r, c = parts[0][0].shape
    np_ = len(parts)
    first = part * (r // tr)

    def body(*refs):
        g = refs[0][0].astype(F32)
        for ref in refs[1:np_]:
            g = g + ref[0].astype(F32)
        w_ref, m_ref, v_ref = refs[np_:np_ + 3]
        g_out, d_out, m_out, v_out = refs[-4:]
        new_m = ADAM_B1 * m_ref[...] + (1.0 - ADAM_B1) * g
        new_v = ADAM_B2 * v_ref[...] + (1.0 - ADAM_B2) * (g * g)
        m_hat = new_m / (1.0 - ADAM_B1 ** ADAM_STEP)
        v_hat = new_v / (1.0 - ADAM_B2 ** ADAM_STEP)
        g_out[...] = g
        d_out[...] = -ADAM_LR * (m_hat / (jnp.sqrt(v_hat) + ADAM_EPS) + ADAM_WD * w_ref[...])
        m_out[...] = new_m
        v_out[...] = new_v

    tile = pl.BlockSpec((tr, c), lambda i: (first + i, 0))
    in_specs = [pl.BlockSpec((1, tr, c), lambda i, s=s: (s, i, 0)) for _, s in parts] + [tile] * 3
    ins = [a for a, _ in parts] + [w, m, v]
    aliases = {}
    if prev is not None:
        aliases = {len(ins) + k: k for k in range(4)}
        in_specs += [ANY] * 4
        ins += list(prev)
    if after is not None:
        in_specs.append(ANY)
        ins.append(after)
    return pl.pallas_call(
        body, name=name, grid=(r // tr,), in_specs=in_specs,
        out_specs=[tile] * 4, out_shape=[jax.ShapeDtypeStruct(w.shape, F32)] * 4,
        input_output_aliases=aliases, compiler_params=_params("parallel"),
    )(*ins)


HBM = pl.BlockSpec(memory_space=pltpu.HBM)
SEM = pl.BlockSpec(memory_space=pltpu.SEMAPHORE)
SIDE_EFFECT = pltpu.SideEffectType.DATAFLOW_SIDE_EFFECTING


def _split_copies(plan, src_refs, land_refs, send_sems, recv_sems):
    copies = []
    for i, (k, src_slot, land_slot, device) in enumerate(plan(*_place())):
        copies.append(pltpu.make_async_remote_copy(
            src_ref=src_refs[k] if src_slot is None else src_refs[k].at[src_slot], dst_ref=land_refs[k].at[land_slot],
            send_sem=send_sems.at[i], recv_sem=recv_sems.at[i], device_id=device, device_id_type=MESH))
    return copies


def split_start(name, srcs, land_shapes, plan, n_copies):
    n = len(srcs)

    def body(*refs):
        src_refs, land_refs = refs[:n], refs[n:2 * n]
        send_sems, recv_sems, token = refs[2 * n], refs[2 * n + 1], refs[-1]
        for cp in _split_copies(plan, src_refs, land_refs, send_sems, recv_sems):
            cp.start()
        token[...] = jnp.zeros_like(token)

    lands = [lax.empty(shape, s.dtype) for shape, s in zip(land_shapes, srcs)]
    ins = [pltpu.with_memory_space_constraint(a, pltpu.HBM) for a in list(srcs) + lands]
    out = pl.pallas_call(
        body, name=name,
        out_shape=(pltpu.SemaphoreType.DMA((n_copies,)), pltpu.SemaphoreType.DMA((n_copies,)),
                   *[pltpu.HBM(a.shape, a.dtype) for a in ins], jax.ShapeDtypeStruct((8, LANES), F32)),
        in_specs=[HBM] * (2 * n), out_specs=(SEM, SEM, *[HBM] * (2 * n), pl.BlockSpec(memory_space=pltpu.VMEM)),
        input_output_aliases={i: 2 + i for i in range(2 * n)},
        compiler_params=pltpu.CompilerParams(has_side_effects=SIDE_EFFECT),
    )(*ins)
    return out[0], out[1], list(out[2:2 + n]), list(out[2 + n:2 + 2 * n]), out[-1]


def split_wait(name, handle, plan, after):
    send_sems, recv_sems, srcs, lands, _ = handle
    n = len(srcs)

    def body(*refs):
        src_refs, land_refs = refs[:n], refs[n:2 * n]
        for cp in _split_copies(plan, src_refs, land_refs, refs[2 * n], refs[2 * n + 1]):
            cp.wait_send()
            cp.wait_recv()

    out = pl.pallas_call(
        body, name=name, out_shape=tuple(pltpu.HBM(a.shape, a.dtype) for a in srcs + lands),
        in_specs=[HBM] * (2 * n) + [SEM, SEM, ANY], out_specs=tuple([HBM] * (2 * n)),
        input_output_aliases={i: i for i in range(2 * n)},
        compiler_params=pltpu.CompilerParams(has_side_effects=SIDE_EFFECT),
    )(*srcs, *lands, send_sems, recv_sems, after)
    return list(out[:n]), list(out[n:])


def gather_plan(n):
    def plan(x, y, c):
        me = 4 * x + 2 * y + c
        peers = [(x, y, 1 - c)] + [(*chip, c) for chip in _other_chips(x, y)]
        return [(k, None, me, peer) for k in range(n) for peer in peers]
    return plan


def chips_plan(n):
    def plan(x, y, c):
        return [(k, 1 + j, j, (*chip, c)) for k in range(n) for j, chip in enumerate(_other_chips(x, y))]
    return plan


def gather_finish(name, gathered):
    n = len(gathered)

    def body(*refs):
        outs = refs[n:2 * n]
        send_sems, recv_sems = refs[2 * n:]
        x, y, c = _place()

        def passed_on(k, j, core):
            cx, cy = _other_chips(x, y)[j]
            blk = outs[k].at[4 * cx + 2 * cy + core]
            return pltpu.make_async_remote_copy(
                src_ref=blk, dst_ref=blk, send_sem=send_sems.at[3 * k + j], recv_sem=recv_sems.at[3 * k + j],
                device_id=(x, y, 1 - c), device_id_type=MESH)

        pairs = [(k, j) for k in range(n) for j in range(3)]
        sends = [passed_on(k, j, c) for k, j in pairs]
        for cp in sends:
            cp.start()
        for k, j in pairs:
            passed_on(k, j, 1 - c).wait_recv()
        for cp in sends:
            cp.wait_send()

    return pl.pallas_call(
        body, name=name, in_specs=[ANY] * n, out_specs=[ANY] * n,
        out_shape=[jax.ShapeDtypeStruct(g.shape, g.dtype) for g in gathered],
        input_output_aliases={k: k for k in range(n)},
        scratch_shapes=[pltpu.SemaphoreType.DMA((3 * n,)), pltpu.SemaphoreType.DMA((3 * n,))],
    )(*gathered)


ROW_TILE = 256
COL_TILE = 256


def _rms_fwd(tag, h, w, after=None):
    return rows_fwd(tag, _rms_f, [(h, D_MODEL, 0)], [w], [(D_MODEL, BF16)], ROW_TILE, after=after)[0]


def _rms_bwd(tag, h, w, dhn, dres):
    dh, (dw,) = rows_bwd(tag, _rms_f, [(h, D_MODEL, 0)], [w], [(dhn, D_MODEL, 0)], ROW_TILE, [(F32, BF16)], add=dres)
    return tuple(dh), dw


def even_fwd(tag, h, w, p, after=None):
    hn = _rms_fwd(tag + "_rms", h, p["nm"], after)
    uv = matmul(tag + "_uv", hn, w["uv"])
    z = matmul(tag + "_z", hn, w["z"])
    xbc = matmul(tag + "_xbc", hn, w["xbc"])
    pdt = matmul(tag + "_dt", hn, w["dt"])
    gm = [p["lng"], p["lnb"], p["ws"], p["bst"]]
    ya = rows_fwd(tag + "_gmlp", _gmlp_f, [(uv, 2 * D_MODEL, 0)], gm, [(D_MODEL, BF16)], GM_BLOCK)[0]
    xa = cols_fwd(tag + "_conv", _conv_silu_f, [(xbc, 0)], [p["cw"], p["cb"]], F32, COL_TILE)
    y, states = ssd_fwd(tag + "_ssd", xa, pdt, p["hp"])
    yb = rows_fwd(tag + "_gate", _gate_norm_f, [(y, D_MODEL, 0), (z, D_MODEL, 0)], [p["nw"]], [(D_MODEL, BF16)], ROW_TILE)[0]
    h1 = matmul(tag + "_out_b", yb, w["out_bot"], res=matmul(tag + "_out_a", ya, w["out_top"], res=h))
    return h1, dict(h=h, hn=hn, uv=uv, z=z, xbc=xbc, pdt=pdt, xa=xa, y=y, states=states, ya=ya, yb=yb)


def even_bwd(tag, dh1, s, w, p, after=None):
    dh1, dh1b = dh1
    dya = matmul(tag + "_dya", dh1b, w["out_top"], tb=True, after=after)
    dyb = matmul(tag + "_dyb", dh1b, w["out_bot"], tb=True, after=after)
    gw = dict(out_top=matmul(tag + "_gwa", s["ya"], dh1b, ta=True, out_dtype=BF16),
              out_bot=matmul(tag + "_gwb", s["yb"], dh1b, ta=True, out_dtype=BF16))
    (dy, dz), (dnw,) = rows_bwd(tag + "_gate_b", _gate_norm_f, [(s["y"], D_MODEL, 0), (s["z"], D_MODEL, 0)], [p["nw"]],
                                [(dyb, D_MODEL, 0)], ROW_TILE, [F32, BF16])
    dxs, dbm, dcm, dpdt, dhp = ssd_bwd(tag + "_ssd_b", s["xa"], s["pdt"], p["hp"], s["states"], dy)
    dxa = jnp.concatenate([dxs, dbm, dcm], axis=1)
    (dxbc,), (dcw, dcb) = cols_bwd(tag + "_conv_b", _conv_silu_f, [(s["xbc"], 0)], [p["cw"], p["cb"]], dxa, COL_TILE, BF16)
    gm = [p["lng"], p["lnb"], p["ws"], p["bst"]]
    (duv,), (dlng, dlnb, dws, dbst) = rows_bwd(tag + "_gmlp_b", _gmlp_f, [(s["uv"], 2 * D_MODEL, 0)], gm,
                                               [(dya, D_MODEL, 0)], GM_BLOCK, [BF16])
    dhn = None
    for key, d in (("uv", duv), ("z", dz), ("xbc", dxbc), ("dt", dpdt)):
        dhn = matmul(f"{tag}_dx_{key}", d, w[key], tb=True, res=dhn)
        gw[key] = matmul(f"{tag}_gw_{key}", s["hn"], d, ta=True, out_dtype=BF16)
    dh, dnm = _rms_bwd(tag + "_rms_b", s["h"], p["nm"], dhn, dh1)
    gp = dict(nm=dnm, lng=dlng, lnb=dlnb, ws=dws, bst=dbst, cw=dcw, cb=dcb, hp=dhp, nw=dnw)
    return dh, gw, gp


def odd_fwd(tag, h, w, p, cos, sin, after=None):
    hn = _rms_fwd(tag + "_rms", h, p["nm"], after)
    proj = matmul(tag + "_in", hn, w["in"])
    cq, ckv, kpe = rows_fwd(tag + "_qkvn", _qkv_norm_f, [(proj, ODD_IN_PAD, 0), (cos, LANES, 0), (sin, LANES, 0)],
                            [p["qn"], p["kvn"]], [(MLA_RANK, BF16), (MLA_RANK, BF16), (LANES, F32)], ROW_TILE)
    q = matmul(tag + "_q", cq, w["uq"])
    kn = matmul(tag + "_kn", ckv, w["kn"], out_dtype=BF16)
    v = matmul(tag + "_v", ckv, w["v"], out_dtype=BF16)
    o = attn_fwd(tag + "_attn", q, kn, kpe, v, cos, sin)
    h1 = matmul(tag + "_o", o, w["o"], res=h)
    return h1, dict(h=h, hn=hn, proj=proj, cq=cq, ckv=ckv, kpe=kpe, q=q, kn=kn, v=v, o=o)


def odd_bwd(tag, dh1, s, w, p, cos, sin, after=None):
    dh1, dh1b = dh1
    do = matmul(tag + "_do", dh1b, w["o"], tb=True, after=after)
    gw = dict(o=matmul(tag + "_gw_o", s["o"], dh1b, ta=True, out_dtype=BF16))
    dq, dkn, dkpe, dv = attn_bwd(tag + "_attn_b", s["q"], s["kn"], s["kpe"], s["v"], cos, sin, do)
    dcq = matmul(tag + "_dcq", dq, w["uq"], tb=True)
    gw["uq"] = matmul(tag + "_gw_uq", s["cq"], dq, ta=True, out_dtype=BF16)
    dckv = matmul(tag + "_dckv_v", dv, w["v"], tb=True, res=matmul(tag + "_dckv_k", dkn, w["kn"], tb=True))
    gw["kn"] = matmul(tag + "_gw_kn", s["ckv"], dkn, ta=True, out_dtype=BF16)
    gw["v"] = matmul(tag + "_gw_v", s["ckv"], dv, ta=True, out_dtype=BF16)
    (dproj,), (dqn, dkvn) = rows_bwd(
        tag + "_qkvn_b", _qkv_norm_f, [(s["proj"], ODD_IN_PAD, 0), (cos, LANES, 0), (sin, LANES, 0)], [p["qn"], p["kvn"]],
        [(dcq, MLA_RANK, 0), (dckv, MLA_RANK, 0), (dkpe, LANES, 0)], ROW_TILE, [BF16], n_nondiff=2)
    dhn = matmul(tag + "_dx_in", dproj, w["in"], tb=True)
    gw["in"] = matmul(tag + "_gw_in", s["hn"], dproj, ta=True, out_dtype=BF16)
    dh, dnm = _rms_bwd(tag + "_rms_b", s["h"], p["nm"], dhn, dh1)
    return dh, gw, dict(nm=dnm, qn=dqn, kvn=dkvn)


def ffn_fwd(tag, h, w, p, after=None):
    hn = _rms_fwd(tag + "_rms", h, p["nf"], after)
    g = matmul(tag + "_up_g", hn, w["up_g"])
    val = matmul(tag + "_up_v", hn, w["up_v"])
    act = cols_fwd(tag + "_act", _ffn_act_f, [(g, 0), (val, 0)], [p["fcw"], p["fcb"]], BF16, COL_TILE)
    h2 = matmul(tag + "_down", act, w["down"], res=h)
    return h2, dict(h=h, hn=hn, g=g, val=val, act=act)


def ffn_bwd(tag, dh2, s, w, p, after=None):
    dh2, dh2b = dh2
    dact = matmul(tag + "_dact", dh2b, w["down"], tb=True, after=after)
    gw = dict(down=matmul(tag + "_gw_down", s["act"], dh2b, ta=True, out_dtype=BF16))
    (dg, dval), (dfcw, dfcb) = cols_bwd(tag + "_act_b", _ffn_act_f, [(s["g"], 0), (s["val"], 0)], [p["fcw"], p["fcb"]],
                                        dact, COL_TILE, BF16)
    dhn = matmul(tag + "_dx_v", dval, w["up_v"], tb=True, res=matmul(tag + "_dx_g", dg, w["up_g"], tb=True))
    gw["up"] = matmul_tn_slots(tag + "_gw_up", s["hn"], [dg, dval], 2 * D_FF // N_DEV)
    dh, dnf = _rms_bwd(tag + "_rms_b", s["h"], p["nf"], dhn, dh2)
    return dh, gw, dict(nf=dnf, fcw=dfcw, fcb=dfcb)


def _cols_from_slots(g):
    return jnp.moveaxis(g, 0, 1).reshape(g.shape[1], N_DEV * g.shape[2])


def _slots_from_cols(wmat):
    k, n = wmat.shape
    return jnp.moveaxis(wmat.reshape(k, N_DEV, n // N_DEV), 1, 0)


def _pad_last(a, width):
    return jnp.pad(a, [(0, 0)] * (a.ndim - 1) + [(0, width - a.shape[-1])])


def _heads_to_lanes(a):
    lead = a.shape[:-1]
    return _pad_last(a.reshape(lead + (SSD_STEPS, SSD_HEADS_PER_STEP)), LANES).reshape(lead + (SSD_STEPS * LANES,))


def _lanes_to_heads(a):
    lead = a.shape[:-1]
    return a.reshape(lead + (SSD_STEPS, LANES))[..., :SSD_HEADS_PER_STEP].reshape(lead + (SSM_HEADS,))


def prep_even(g_in, g_out):
    wn = _cols_from_slots(g_in)
    o1, o2, o3 = 2 * D_MODEL, 3 * D_MODEL, 3 * D_MODEL + SSM_CONV_DIM
    out = g_out.reshape(2 * D_MODEL, D_MODEL)
    return dict(uv=wn[:, :o1], z=wn[:, o1:o2], xbc=wn[:, o2:o3], dt=_heads_to_lanes(wn[:, o3:]),
                out_top=out[:D_MODEL], out_bot=out[D_MODEL:])


def unprep_even(gw):
    wn = jnp.concatenate([gw["uv"], gw["z"], gw["xbc"], _lanes_to_heads(gw["dt"])], axis=1)
    return _slots_from_cols(wn), jnp.concatenate([gw["out_top"], gw["out_bot"]], axis=0).reshape(N_DEV, -1, D_MODEL)


def prep_odd(g_in, g_uq, g_ukv, g_o):
    uq = _cols_from_slots(g_uq).reshape(MLA_RANK, MLA_HEADS, MLA_QK)
    ukv = _cols_from_slots(g_ukv).reshape(MLA_RANK, MLA_HEADS, MLA_NOPE + MLA_V)
    return dict(**{"in": _pad_last(g_in.reshape(D_MODEL, ODD_IN), ODD_IN_PAD)},
                uq=_pad_last(uq, MLA_QPAD).reshape(MLA_RANK, MLA_HEADS * MLA_QPAD),
                kn=ukv[:, :, :MLA_NOPE].reshape(MLA_RANK, MLA_HEADS * MLA_NOPE),
                v=ukv[:, :, MLA_NOPE:].reshape(MLA_RANK, MLA_HEADS * MLA_V),
                o=g_o.reshape(MLA_HEADS * MLA_V, D_MODEL))


def unprep_odd(gw):
    uq = gw["uq"].reshape(MLA_RANK, MLA_HEADS, MLA_QPAD)[:, :, :MLA_QK].reshape(MLA_RANK, MLA_HEADS * MLA_QK)
    ukv = jnp.concatenate([gw["kn"].reshape(MLA_RANK, MLA_HEADS, MLA_NOPE), gw["v"].reshape(MLA_RANK, MLA_HEADS, MLA_V)], axis=2)
    return (gw["in"][:, :ODD_IN].reshape(N_DEV, -1, ODD_IN), _slots_from_cols(uq),
            _slots_from_cols(ukv.reshape(MLA_RANK, -1)), gw["o"].reshape(N_DEV, -1, D_MODEL))


def prep_ffn(g_up, g_down):
    up = _cols_from_slots(g_up)
    return dict(up_g=up[:, :D_FF], up_v=up[:, D_FF:], down=g_down.reshape(D_FF, D_MODEL))


def unprep_ffn(gw):
    return gw["up"], gw["down"].reshape(N_DEV, -1, D_MODEL)


SMALL_TILE = LANES * LANES


def _pack(arrs):
    flat = jnp.concatenate([a.reshape(-1).astype(F32) for a in arrs])
    size = -(-flat.shape[0] // SMALL_TILE) * SMALL_TILE
    return jnp.pad(flat, (0, size - flat.shape[0])).reshape(-1, LANES)


def _unpack(packed, shapes, lead=()):
    flat = packed.reshape(lead + (-1,))
    out, off = [], 0
    for shp in shapes:
        size = math.prod(shp)
        out.append(flat[..., off:off + size].reshape(lead + tuple(shp)))
        off += size
    return out


SMALL_SHARDED = {"ev_gm_ln_g": 2, "ev_gm_ln_b": 2, "ev_conv_w": 2, "od_q_norm": 1, "od_kv_norm": 1, "ff_conv_w": 2}
SMALL_REPLICATED = ["norm_mix", "norm_ffn", "norm_final", "ev_gm_ws", "ev_gm_bs", "ev_conv_b", "ev_dt_bias", "ev_a_log",
                    "ev_d_skip", "ev_ssm_norm_w", "ff_conv_b"]
MATRICES = {"ev_w_in": (2, 2048, 1156), "ev_w_out": (2, 512, 2048), "od_w_in": (2, 256, 1088), "od_w_uq": (2, 512, 384),
            "od_w_ukv": (2, 512, 512), "od_w_o": (2, 256, 2048), "ff_w_up": (4, 2048, 1408), "ff_w_down": (4, 704, 2048)}
WEIGHT_ORDER = ["norm_mix", "norm_ffn", "norm_final", "ev_w_in", "ev_gm_ln_g", "ev_gm_ln_b", "ev_gm_ws", "ev_gm_bs",
                "ev_conv_w", "ev_conv_b", "ev_dt_bias", "ev_a_log", "ev_d_skip", "ev_ssm_norm_w", "ev_w_out", "od_w_in",
                "od_q_norm", "od_kv_norm", "od_w_uq", "od_w_ukv", "od_w_o", "ff_w_up", "ff_conv_w", "ff_conv_b", "ff_w_down"]


def _full_from_shards(name, gathered):
    ax = SMALL_SHARDED[name]
    moved = jnp.moveaxis(gathered, 0, ax)
    shp = moved.shape
    return moved.reshape(shp[:ax] + (shp[ax] * shp[ax + 1],) + shp[ax + 2:])


def _my_shard(name, full, dev):
    ax = SMALL_SHARDED[name]
    shp = full.shape
    split = full.reshape(shp[:ax] + (N_DEV, shp[ax] // N_DEV) + shp[ax + 1:])
    return lax.dynamic_index_in_dim(split, dev, axis=ax, keepdims=False)


def _even_small(sm, j):
    row = lambda a: a.reshape(1, -1)
    hp = jnp.stack([sm["ev_dt_bias"][j], sm["ev_a_log"][j], sm["ev_d_skip"][j]])
    return dict(nm=row(sm["norm_mix"][2 * j]), lng=row(sm["ev_gm_ln_g"][j]), lnb=row(sm["ev_gm_ln_b"][j]),
                ws=sm["ev_gm_ws"][j], bst=sm["ev_gm_bs"][j].T, cw=sm["ev_conv_w"][j], cb=row(sm["ev_conv_b"][j]),
                hp=_heads_to_lanes(hp), nw=row(sm["ev_ssm_norm_w"][j]))


def _odd_small(sm, j):
    row = lambda a: a.reshape(1, -1)
    return dict(nm=row(sm["norm_mix"][2 * j + 1]), qn=row(sm["od_q_norm"][j]), kvn=row(sm["od_kv_norm"][j]))


def _ffn_small(sm, layer):
    row = lambda a: a.reshape(1, -1)
    return dict(nf=row(sm["norm_ffn"][layer]), fcw=sm["ff_conv_w"][layer], fcb=row(sm["ff_conv_b"][layer]))


def _rope_tables(positions):
    inv_freq = ROPE_THETA ** (-jnp.arange(0, MLA_ROPE, 2, dtype=F32) / MLA_ROPE)
    ang = positions.astype(F32).reshape(-1, 1) * inv_freq
    cos, sin = jnp.cos(ang), jnp.sin(ang)
    return _pad_last(jnp.concatenate([cos, cos], axis=1), LANES), _pad_last(jnp.concatenate([-sin, sin], axis=1), LANES)


def local_step(x, positions, target, sm, fetch_weights, emit_grads):
    cos, sin = _rope_tables(positions)
    h, saved = x, []
    for layer in range(4):
        j, tag = layer // 2, f"l{layer}"
        wm, dep = fetch_weights(2 * layer, h)
        if layer % 2 == 0:
            pm = _even_small(sm, j)
            h, sv = even_fwd(tag, h, wm, pm, dep)
        else:
            pm = _odd_small(sm, j)
            h, sv = odd_fwd(tag, h, wm, pm, cos, sin, dep)
        wf, dep = fetch_weights(2 * layer + 1, h)
        pf = _ffn_small(sm, layer)
        h, sf = ffn_fwd(tag + "f", h, wf, pf, dep)
        saved.append((pm, sv, pf, sf, wm, wf))
    loss_tile, dh32, dh16, dnfinal = final_loss("final_loss", h, sm["norm_final"].reshape(1, -1), target)
    gs = {k: [None] * v.shape[0] for k, v in sm.items() if k != "norm_final"}
    gs["norm_final"] = dnfinal.reshape(-1)
    dh, dep = (dh32, dh16), None
    for layer in reversed(range(4)):
        j, tag = layer // 2, f"l{layer}"
        pm, sv, pf, sf, wm, wf = saved[layer]
        dh, gwf, gpf = ffn_bwd(tag + "f", dh, sf, wf, pf, dep)
        gs["norm_ffn"][layer], gs["ff_conv_w"][layer], gs["ff_conv_b"][layer] = gpf["nf"][0], gpf["fcw"], gpf["fcb"][0]
        dep = emit_grads(2 * layer + 1, gwf, dh[0])
        if layer % 2 == 0:
            dh, gwm, gp = even_bwd(tag, dh, sv, wm, pm, dep)
            hp = _lanes_to_heads(gp["hp"])
            gs["norm_mix"][layer] = gp["nm"][0]
            gs["ev_gm_ln_g"][j], gs["ev_gm_ln_b"][j] = gp["lng"].reshape(GM_GROUPS, -1), gp["lnb"].reshape(GM_GROUPS, -1)
            gs["ev_gm_ws"][j], gs["ev_gm_bs"][j] = gp["ws"], gp["bst"].T
            gs["ev_conv_w"][j], gs["ev_conv_b"][j] = gp["cw"], gp["cb"][0]
            gs["ev_dt_bias"][j], gs["ev_a_log"][j], gs["ev_d_skip"][j] = hp[0], hp[1], hp[2]
            gs["ev_ssm_norm_w"][j] = gp["nw"][0]
        else:
            dh, gwm, gp = odd_bwd(tag, dh, sv, wm, pm, cos, sin, dep)
            gs["norm_mix"][layer] = gp["nm"][0]
            gs["od_q_norm"][j], gs["od_kv_norm"][j] = gp["qn"][0], gp["kvn"][0]
        if layer > 0:
            dep = emit_grads(2 * layer, gwm, dh[0])
    gs = {k: (v if k == "norm_final" else jnp.stack(v)) for k, v in gs.items()}
    return loss_tile[0, 0], dh[0], gs, gwm


def kernel(x, positions, norm_mix, norm_ffn, norm_final, ev_w_in, ev_gm_ln_g, ev_gm_ln_b, ev_gm_ws, ev_gm_bs, ev_conv_w, ev_conv_b, ev_dt_bias, ev_a_log, ev_d_skip, ev_ssm_norm_w, ev_w_out, od_w_in, od_q_norm, od_kv_norm, od_w_uq, od_w_ukv, od_w_o, ff_w_up, ff_conv_w, ff_conv_b, ff_w_down, loss_target, m_norm_mix, m_norm_ffn, m_norm_final, m_ev_w_in, m_ev_gm_ln_g, m_ev_gm_ln_b, m_ev_gm_ws, m_ev_gm_bs, m_ev_conv_w, m_ev_conv_b, m_ev_dt_bias, m_ev_a_log, m_ev_d_skip, m_ev_ssm_norm_w, m_ev_w_out, m_od_w_in, m_od_q_norm, m_od_kv_norm, m_od_w_uq, m_od_w_ukv, m_od_w_o, m_ff_w_up, m_ff_conv_w, m_ff_conv_b, m_ff_w_down, v_norm_mix, v_norm_ffn, v_norm_final, v_ev_w_in, v_ev_gm_ln_g, v_ev_gm_ln_b, v_ev_gm_ws, v_ev_gm_bs, v_ev_conv_w, v_ev_conv_b, v_ev_dt_bias, v_ev_a_log, v_ev_d_skip, v_ev_ssm_norm_w, v_ev_w_out, v_od_w_in, v_od_q_norm, v_od_kv_norm, v_od_w_uq, v_od_w_ukv, v_od_w_o, v_ff_w_up, v_ff_conv_w, v_ff_conv_b, v_ff_w_down):
    args = dict(locals())
    wts = {n: args[n] for n in WEIGHT_ORDER}
    mom = {n: args["m_" + n] for n in WEIGHT_ORDER}
    var = {n: args["v_" + n] for n in WEIGHT_ORDER}
    dev = _device_slot()

    small_names = list(SMALL_SHARDED)
    small_shapes = [wts[n].shape for n in small_names]
    (small_all,) = all_gather("ag_small", [_pack([wts[n] for n in small_names])])
    small_full = _unpack(small_all, small_shapes, lead=(N_DEV,))
    sm = {n: _full_from_shards(n, g) for n, g in zip(small_names, small_full)}
    sm.update({n: wts[n] for n in SMALL_REPLICATED})

    bf = {n: wts[n].astype(BF16) for n in MATRICES}

    def stage_matrices(stage):
        layer, is_ffn = divmod(stage, 2)
        if is_ffn:
            return [("ff_w_up", layer), ("ff_w_down", layer)]
        return [(n, layer // 2) for n in (["ev_w_in", "ev_w_out"] if layer % 2 == 0 else ["od_w_in", "od_w_uq", "od_w_ukv", "od_w_o"])]

    def stage_fns(stage):
        layer, is_ffn = divmod(stage, 2)
        if is_ffn:
            return prep_ffn, unprep_ffn
        return (prep_even, unprep_even) if layer % 2 == 0 else (prep_odd, unprep_odd)

    n_stages, ahead = 8, 2

    def start_gather(stage, earlier=None):
        shards = [bf[n][i] for n, i in stage_matrices(stage)]
        if earlier is not None:
            shards, _ = lax.optimization_barrier((shards, earlier))
        return split_start(f"ag_s{stage}_start", shards, [(N_DEV,) + s.shape for s in shards],
                           gather_plan(len(shards)), 4 * len(shards))

    gathers = {}
    for stage in range(ahead):
        gathers[stage] = start_gather(stage, gathers[stage - 1][4] if stage else None)

    def fetch_weights(stage, h):
        plan = gather_plan(len(stage_matrices(stage)))
        shards, landed = split_wait(f"ag_s{stage}_wait", gathers.pop(stage), plan, h)
        g = gather_finish(f"ag_s{stage}_finish", landed)
        g = [lax.dynamic_update_index_in_dim(gk, sk, dev, 0) for gk, sk in zip(g, shards)]
        started = None
        if stage + ahead < n_stages:
            gathers[stage + ahead] = start_gather(stage + ahead, g[0])
            started = gathers[stage + ahead][4]
        return stage_fns(stage)[0](*g), started

    scatters = []
    out = {n: None for n in MATRICES}

    held = []
    hold_below = 6

    def update(mats, sums, recv, after=None):
        for (n, i), p_, r_ in zip(mats, sums, recv):
            layers, rows, cols = MATRICES[n]
            two_d = lambda a: a.reshape(layers * rows, cols)
            out[n] = adamw(f"adamw_{n}_{i}", [(p_, 0), (r_, 0), (r_, 1), (r_, 2)], two_d(wts[n]), two_d(mom[n]),
                           two_d(var[n]), tr=_row_tile(rows), part=i, prev=out[n], after=after)
            if after is not None:
                after = out[n][0]
        return after

    def finish_scatter(after):
        stage, handle = scatters.pop(0)
        mats = stage_matrices(stage)
        sums, recv = split_wait(f"rs_s{stage}_wait", handle, chips_plan(len(mats)), after)
        if 0 < stage < hold_below:
            held.append((mats, sums, recv))
        else:
            update(mats, sums, recv)

    def emit_grads(stage, gw, dh, after=None):
        if len(scatters) >= ahead:
            finish_scatter(dh)
        send = list(stage_fns(stage)[1](gw))
        from_sibling = exchange_sibling(f"rs_s{stage}_sibling", send, after)
        sums = [chip_sums(f"rs_s{stage}_add{k}", g, r, tr=_row_tile(g.shape[1], 512)) for k, (g, r) in enumerate(zip(send, from_sibling))]
        handle = split_start(f"rs_s{stage}_start", sums, [(3,) + s.shape[1:] for s in sums], chips_plan(len(sums)), 3 * len(sums))
        scatters.append((stage, handle))
        return handle[4]

    loss_local, dx, gs, gw_first = local_step(x[0], positions[0], loss_target[0], sm, fetch_weights, emit_grads)
    loss = lax.psum(loss_local, ("x", "y", "c"))

    all_small = small_names + SMALL_REPLICATED
    (partials,) = all_gather("ar_small", [_pack([gs[n] for n in all_small])])
    last_started = emit_grads(0, gw_first, partials, after=partials)
    total = sum_parts("ar_small_sum", [(partials, s) for s in range(N_DEV)])
    g_full = dict(zip(all_small, _unpack(total, [gs[n].shape for n in all_small])))
    g_mine = {n: (_my_shard(n, g_full[n], dev) if n in SMALL_SHARDED else g_full[n]) for n in all_small}
    packed = [_pack([d[n] for n in all_small]) for d in (g_mine, wts, mom, var)]
    res = adamw("adamw_small", [(packed[0][None], 0)], packed[1], packed[2], packed[3], after=last_started)
    unpacked = [_unpack(a, [wts[n].shape for n in all_small]) for a in res]
    for i, n in enumerate(all_small):
        out[n] = [u[i] for u in unpacked]
    while len(scatters) > 1:
        finish_scatter(res[0])
    follow = res[0]
    for job in held:
        follow = update(*job, after=follow)
    finish_scatter(follow)
    for n in MATRICES:
        out[n] = [a.reshape(wts[n].shape) for a in out[n]]

    return (loss, dx[None], *[out[n][0] for n in WEIGHT_ORDER], *[out[n][1] for n in WEIGHT_ORDER],
            *[out[n][2] for n in WEIGHT_ORDER], *[out[n][3] for n in WEIGHT_ORDER])
```

```python
import functools
import math

import jax
import jax.numpy as jnp
from jax import lax
from jax.experimental import pallas as pl
from jax.experimental.pallas import tpu as pltpu

F32 = jnp.float32
BF16 = jnp.bfloat16
MESH = pl.DeviceIdType.MESH

V7X_VMEM_LIMIT_BYTES = 56 * 1024 * 1024
LANES = 128

EPS = 1e-6
D_MODEL = 2048
CHUNK = 64
GM_BLOCK = 128
GM_GROUPS = 8
GM_GROUP_DIM = D_MODEL // GM_GROUPS
SSM_HEADS = 32
SSM_HEAD_DIM = 64
SSM_GROUPS = 4
SSM_STATE = 128
SSM_CONV = 4
SSM_BC = SSM_GROUPS * SSM_STATE
SSM_CONV_DIM = D_MODEL + 2 * SSM_BC
SSD_HEADS_PER_STEP = 4
SSD_STEPS = SSM_HEADS // SSD_HEADS_PER_STEP
SSD_X_WIDTH = SSD_HEADS_PER_STEP * SSM_HEAD_DIM
MLA_HEADS = 16
MLA_RANK = 512
MLA_NOPE = 128
MLA_ROPE = 64
MLA_V = 128
MLA_QK = MLA_NOPE + MLA_ROPE
MLA_QPAD = 2 * LANES
ODD_IN = 2 * MLA_RANK + MLA_ROPE
ODD_IN_PAD = 2 * MLA_RANK + LANES
D_FF = 5632
ROPE_THETA = 10000.0
N_DEV = 8

ADAM_LR, ADAM_B1, ADAM_B2, ADAM_EPS, ADAM_WD, ADAM_STEP = 0.001, 0.9, 0.999, 1e-08, 0.01, 10


def _params(*sem):
    return pltpu.CompilerParams(dimension_semantics=sem, vmem_limit_bytes=V7X_VMEM_LIMIT_BYTES)


def _pick(dim, target):
    if dim <= target:
        return dim
    t = (target // LANES) * LANES
    while t >= LANES:
        if dim % t == 0:
            return t
        t -= LANES
    raise ValueError(f"no tile for {dim} under {target}")


MATMUL_VMEM_BUDGET = 32 * 1024 * 1024


def _matmul_tiles(m, n, k, a_bytes, b_bytes, o_bytes, has_res, ta):
    def fits(tm, tn):
        per_out = o_bytes + (4 if has_res else 0)
        return 2 * (tm * k * a_bytes + tn * k * b_bytes + tm * tn * per_out) <= MATMUL_VMEM_BUDGET

    tns = (2048, 1024, 512, 256, 128) if ta else (512, 256, 128)
    tms = (512, 256, 128) if ta else (2048, 1024, 512, 256, 128)
    for tn in tns:
        tn = _pick(n, tn)
        for tm in tms:
            tm = _pick(m, tm)
            if fits(tm, tn):
                return tm, tn
    raise ValueError(f"no matmul tiles for {m}x{n}x{k}")


def matmul(name, a, b, *, ta=False, tb=False, res=None, out_dtype=F32, after=None):
    m, k = (a.shape[1], a.shape[0]) if ta else a.shape
    n = b.shape[0] if tb else b.shape[1]
    assert k == (b.shape[1] if tb else b.shape[0]), (name, a.shape, b.shape)
    tm, tn = _matmul_tiles(m, n, k, a.dtype.itemsize, b.dtype.itemsize, jnp.dtype(out_dtype).itemsize, res is not None, ta)
    dims = (((0 if ta else 1,), (1 if tb else 0,)), ((), ()))

    def body(*refs):
        a_ref, b_ref, o_ref = refs[0], refs[1], refs[-1]
        total = lax.dot_general(a_ref[...].astype(BF16), b_ref[...].astype(BF16), dims, preferred_element_type=F32)
        if res is not None:
            total = total + refs[2][...]
        o_ref[...] = total.astype(o_ref.dtype)

    a_spec = pl.BlockSpec((k, tm), lambda i, j: (0, i)) if ta else pl.BlockSpec((tm, k), lambda i, j: (i, 0))
    b_spec = pl.BlockSpec((tn, k), lambda i, j: (j, 0)) if tb else pl.BlockSpec((k, tn), lambda i, j: (0, j))
    o_spec = pl.BlockSpec((tm, tn), lambda i, j: (i, j))
    ins, specs = [a, b], [a_spec, b_spec]
    if res is not None:
        ins.append(res)
        specs.append(o_spec)
    if after is not None:
        ins.append(after)
        specs.append(pl.BlockSpec(memory_space=pl.ANY))
    return pl.pallas_call(
        body, name=name, grid=(m // tm, n // tn), in_specs=specs, out_specs=o_spec,
        out_shape=jax.ShapeDtypeStruct((m, n), out_dtype),
        compiler_params=_params("parallel", "parallel"),
    )(*ins)


def matmul_tn_slots(name, a, bs, width, tm=512):
    k, m = a.shape
    counts = [b.shape[1] // width for b in bs]
    firsts = [sum(counts[:i]) for i in range(len(bs))]
    tm = _pick(m, tm)
    tn_dims = (((0,), (0,)), ((), ()))

    def body(*refs):
        a_ref, o_ref = refs[0], refs[-1]
        j = pl.program_id(1)
        for b_ref, first, count in zip(refs[1:-1], firsts, counts):
            @pl.when(jnp.logical_and(j >= first, j < first + count))
            def _(b_ref=b_ref):
                o_ref[0] = lax.dot_general(a_ref[...].astype(BF16), b_ref[...].astype(BF16), tn_dims,
                                           preferred_element_type=F32).astype(o_ref.dtype)

    specs = [pl.BlockSpec((k, tm), lambda i, j: (0, i))]
    specs += [pl.BlockSpec((k, width), lambda i, j, first=first, count=count: (0, jnp.clip(j - first, 0, count - 1)))
              for first, count in zip(firsts, counts)]
    return pl.pallas_call(
        body, name=name, grid=(m // tm, sum(counts)), in_specs=specs,
        out_specs=pl.BlockSpec((1, tm, width), lambda i, j: (j, i, 0)),
        out_shape=jax.ShapeDtypeStruct((sum(counts), m, width), BF16),
        compiler_params=_params("parallel", "arbitrary"),
    )(a, *bs)


@functools.partial(jax.custom_vjp, nondiff_argnums=(1, 2))
def _roll(x, shift, axis):
    return pltpu.roll(x, shift, axis)


def _roll_fwd(x, shift, axis):
    return pltpu.roll(x, shift, axis), None


def _roll_bwd(shift, axis, _, g):
    return (pltpu.roll(g, (g.shape[axis] - shift) % g.shape[axis], axis),)


_roll.defvjp(_roll_fwd, _roll_bwd)


def _shift_down(x, s):
    rows = lax.broadcasted_iota(jnp.int32, x.shape, 0)
    return jnp.where(rows >= s, _roll(x, s, 0), 0.0)


def _dwconv(x, w, b):
    taps = w.shape[0]
    y = b + w[taps - 1:taps, :] * x
    for kk in range(taps - 1):
        y = y + w[kk:kk + 1, :] * _shift_down(x, taps - 1 - kk)
    return y


def _rms(x, w):
    return x * lax.rsqrt(jnp.mean(x * x, -1, keepdims=True) + EPS) * w


def _rms_f(h, w):
    return (_rms(h, w),)


def _gmlp_f(uv, lng, lnb, ws, bst):
    r = lax.broadcasted_iota(jnp.int32, (GM_BLOCK, GM_BLOCK), 0) // CHUNK
    c = lax.broadcasted_iota(jnp.int32, (GM_BLOCK, GM_BLOCK), 1) // CHUNK
    outs = []
    for g in range(GM_GROUPS):
        lo, hi = g * GM_GROUP_DIM, (g + 1) * GM_GROUP_DIM
        gu = jax.nn.gelu(uv[:, lo:hi])
        gv = jax.nn.gelu(uv[:, D_MODEL + lo:D_MODEL + hi])
        xc = gv - jnp.mean(gv, -1, keepdims=True)
        var = jnp.mean(xc * xc, -1, keepdims=True)
        vn = xc * lax.rsqrt(var + EPS) * lng[:, lo:hi] + lnb[:, lo:hi]
        wm = jnp.where(r >= c, ws[g], 0.0).astype(BF16)
        gate = jnp.dot(wm, vn.astype(BF16), preferred_element_type=F32) + bst[:, g:g + 1]
        outs.append(gu * gate)
    return (jnp.concatenate(outs, axis=1),)


def _conv_silu_f(x, w, b):
    return (jax.nn.silu(_dwconv(x, w, b)),)


def _ffn_act_f(g, val, w, b):
    return (jax.nn.gelu(_dwconv(g, w, b)) * val,)


def _gate_norm_f(y, z, nw):
    y2 = y * jax.nn.silu(z)
    width = D_MODEL // SSM_GROUPS
    outs = []
    for g in range(SSM_GROUPS):
        blk = y2[:, g * width:(g + 1) * width]
        outs.append(blk * lax.rsqrt(jnp.mean(blk * blk, -1, keepdims=True) + EPS))
    return (jnp.concatenate(outs, axis=1) * nw,)


def _rope(x, cos, sin):
    lane = lax.broadcasted_iota(jnp.int32, x.shape, 1)
    half = MLA_ROPE // 2
    swapped = jnp.where(lane < half, _roll(x, LANES - half, 1), _roll(x, half, 1))
    return x * cos + swapped * sin


def _qkv_norm_f(proj, cos, sin, qn, kvn):
    cq = _rms(proj[:, :MLA_RANK], qn)
    ckv = _rms(proj[:, MLA_RANK:2 * MLA_RANK], kvn)
    kpe = _rope(proj[:, 2 * MLA_RANK:], cos, sin)
    return cq, ckv, kpe


def _attn_scores(q0, k0, qh, kn, kpe, cos, sin):
    qn = qh[:, :MLA_NOPE]
    qp = _rope(qh[:, MLA_NOPE:], cos, sin)
    nt = (((1,), (1,)), ((), ()))
    s = lax.dot_general(qn.astype(BF16), kn.astype(BF16), nt, preferred_element_type=F32)
    s = s + lax.dot_general(qp.astype(BF16), kpe.astype(BF16), nt, preferred_element_type=F32)
    s = s * (MLA_QK ** -0.5)
    visible_below = ((q0 + lax.broadcasted_iota(jnp.int32, (s.shape[0], 1), 0)) // CHUNK + 1) * CHUNK - k0
    return jnp.where(lax.broadcasted_iota(jnp.int32, s.shape, 1) < visible_below, s, -jnp.inf)


def _ssd_chunk_f(x, bm, cm, pdt, hp, sprev):
    nh, hd = SSD_HEADS_PER_STEP, SSM_HEAD_DIM
    dt = jax.nn.softplus(pdt + hp[0:1, :])
    cs = dt * (-jnp.exp(hp[1:2, :]))
    shift = 1
    while shift < CHUNK:
        cs = cs + _shift_down(cs, shift)
        shift *= 2
    cst = cs.T
    tot = cs[CHUNK - 1:CHUNK, :]

    def lanes(vals):
        return jnp.concatenate([jnp.broadcast_to(vals[:, e:e + 1], (vals.shape[0], hd)) for e in range(nh)], axis=1)

    r = lax.broadcasted_iota(jnp.int32, (CHUNK, CHUNK), 0)
    c = lax.broadcasted_iota(jnp.int32, (CHUNK, CHUNK), 1)
    tril = r >= c
    nt = (((1,), (1,)), ((), ()))
    tn = (((0,), (0,)), ((), ()))
    xd = x * lanes(dt)
    cb = lax.dot_general(cm.astype(BF16), bm.astype(BF16), nt, preferred_element_type=F32)
    ys = []
    for e in range(nh):
        decay = jnp.exp(jnp.where(tril, cs[:, e:e + 1] - cst[e:e + 1, :], -jnp.inf))
        ys.append(jnp.dot((cb * decay).astype(BF16), xd[:, e * hd:(e + 1) * hd].astype(BF16), preferred_element_type=F32))
    st = lax.dot_general((xd * lanes(jnp.exp(tot - cs))).astype(BF16), bm.astype(BF16), tn, preferred_element_type=F32)
    yoff = lax.dot_general(cm.astype(BF16), sprev.astype(BF16), nt, preferred_element_type=F32)
    y = jnp.concatenate(ys, axis=1) + yoff * lanes(jnp.exp(cs)) + lanes(hp[2:3, :]) * x
    carry = jnp.concatenate([jnp.broadcast_to(jnp.exp(tot[:, e:e + 1]), (hd, 1)) for e in range(nh)], axis=0)
    return y, carry * sprev + st


def _full_spec(a):
    nd = a.ndim
    return pl.BlockSpec(a.shape, lambda i, nd=nd: (0,) * nd)


def rows_fwd(name, f, rows, params, outs, tr, after=None):
    t = rows[0][0].shape[0]
    nr, npar = len(rows), len(params)
    extra = [] if after is None else [after]

    def body(*refs):
        vals = f(*[x[...].astype(F32) for x in refs[:nr + npar]])
        for o_ref, val in zip(refs[nr + npar + len(extra):], vals):
            o_ref[...] = val.astype(o_ref.dtype)

    in_specs = [pl.BlockSpec((tr, w), lambda i, cb=cb: (i, cb)) for _, w, cb in rows] + [_full_spec(p) for p in params]
    in_specs += [pl.BlockSpec(memory_space=pl.ANY)] * len(extra)
    out = pl.pallas_call(
        body, name=name, grid=(t // tr,), in_specs=in_specs,
        out_specs=[pl.BlockSpec((tr, w), lambda i: (i, 0)) for w, _ in outs],
        out_shape=[jax.ShapeDtypeStruct((t, w), dt) for w, dt in outs],
        compiler_params=_params("parallel"),
    )(*[a for a, _, _ in rows], *params, *extra)
    return out


def rows_bwd(name, f, rows, params, cots, tr, d_dtypes, n_nondiff=0, add=None, after=None):
    t = rows[0][0].shape[0]
    nr, npar, nc = len(rows), len(params), len(cots)
    nd = nr - n_nondiff
    has_add = add is not None
    copies = [(j, dt) for j in range(nd) for dt in (d_dtypes[j] if isinstance(d_dtypes[j], tuple) else (d_dtypes[j],))]
    ncp = len(copies)

    def body(*refs):
        i = pl.program_id(0)
        row_vals = [x[...].astype(F32) for x in refs[:nr]]
        par_vals = [x[...].astype(F32) for x in refs[nr:nr + npar]]
        cot_refs = refs[nr + npar:nr + npar + nc]
        pos = nr + npar + nc
        add_ref = refs[pos] if has_add else None
        pos += int(has_add) + int(after is not None)
        drow_refs = refs[pos:pos + ncp]
        dpar_refs = refs[pos + ncp:]

        def g(*diff):
            return f(*diff[:nd], *row_vals[nd:], *diff[nd:])

        _, vjp = jax.vjp(g, *row_vals[:nd], *par_vals)
        grads = vjp(tuple(cr[...].astype(F32) for cr in cot_refs))
        for (j, _), d_ref in zip(copies, drow_refs):
            val = grads[j]
            if j == 0 and has_add:
                val = val + add_ref[...]
            d_ref[...] = val.astype(d_ref.dtype)
        for j, d_ref in enumerate(dpar_refs):
            @pl.when(i == 0)
            def _(d_ref=d_ref, j=j):
                d_ref[...] = grads[nd + j]

            @pl.when(i > 0)
            def _(d_ref=d_ref, j=j):
                d_ref[...] += grads[nd + j]

    in_specs = [pl.BlockSpec((tr, w), lambda i, cb=cb: (i, cb)) for _, w, cb in rows] + [_full_spec(p) for p in params]
    in_specs += [pl.BlockSpec((tr, w), lambda i, cb=cb: (i, cb)) for _, w, cb in cots]
    ins = [a for a, _, _ in rows] + list(params) + [a for a, _, _ in cots]
    if has_add:
        in_specs.append(pl.BlockSpec((tr, rows[0][1]), lambda i: (i, 0)))
        ins.append(add)
    if after is not None:
        in_specs.append(pl.BlockSpec(memory_space=pl.ANY))
        ins.append(after)
    out_specs = [pl.BlockSpec((tr, rows[j][1]), lambda i: (i, 0)) for j, _ in copies] + [_full_spec(p) for p in params]
    out_shape = [jax.ShapeDtypeStruct((t, rows[j][1]), dt) for j, dt in copies]
    out_shape += [jax.ShapeDtypeStruct(p.shape, F32) for p in params]
    out = pl.pallas_call(
        body, name=name, grid=(t // tr,), in_specs=in_specs, out_specs=out_specs, out_shape=out_shape,
        compiler_params=_params("arbitrary"),
    )(*ins)
    return out[:ncp], out[ncp:]


def cols_fwd(name, f, cols, cparams, out_dtype, tc):
    t = cols[0][0].shape[0]
    width = cparams[0].shape[1]
    ncol = len(cols)

    def body(*refs):
        (val,) = f(*[x[...].astype(F32) for x in refs[:-1]])
        refs[-1][...] = val.astype(refs[-1].dtype)

    in_specs = [pl.BlockSpec((t, tc), lambda j, o=o: (0, o + j)) for _, o in cols]
    in_specs += [pl.BlockSpec((p.shape[0], tc), lambda j: (0, j)) for p in cparams]
    return pl.pallas_call(
        body, name=name, grid=(width // tc,), in_specs=in_specs,
        out_specs=pl.BlockSpec((t, tc), lambda j: (0, j)),
        out_shape=jax.ShapeDtypeStruct((t, width), out_dtype),
        compiler_params=_params("parallel"),
    )(*[a for a, _ in cols], *cparams)


def cols_bwd(name, f, cols, cparams, cot, tc, d_dtype, after=None):
    t = cols[0][0].shape[0]
    width = cparams[0].shape[1]
    ncol, npar = len(cols), len(cparams)
    extra = [] if after is None else [after]

    def body(*refs):
        vals = [x[...].astype(F32) for x in refs[:ncol + npar]]
        _, vjp = jax.vjp(f, *vals)
        grads = vjp((refs[ncol + npar][...].astype(F32),))
        for d_ref, gval in zip(refs[ncol + npar + 1 + len(extra):], grads):
            d_ref[...] = gval.astype(d_ref.dtype)

    in_specs = [pl.BlockSpec((t, tc), lambda j, o=o: (0, o + j)) for _, o in cols]
    in_specs += [pl.BlockSpec((p.shape[0], tc), lambda j: (0, j)) for p in cparams]
    in_specs.append(pl.BlockSpec((t, tc), lambda j: (0, j)))
    in_specs += [pl.BlockSpec(memory_space=pl.ANY)] * len(extra)
    out_specs = [pl.BlockSpec((t, tc), lambda j: (0, j)) for _ in cols]
    out_specs += [pl.BlockSpec((p.shape[0], tc), lambda j: (0, j)) for p in cparams]
    out_shape = [jax.ShapeDtypeStruct((t, width), d_dtype) for _ in cols]
    out_shape += [jax.ShapeDtypeStruct(p.shape, F32) for p in cparams]
    out = pl.pallas_call(
        body, name=name, grid=(width // tc,), in_specs=in_specs, out_specs=out_specs, out_shape=out_shape,
        compiler_params=_params("parallel"),
    )(*[a for a, _ in cols], *cparams, cot, *extra)
    return out[:ncol], out[ncol:]


def _ssd_in_specs(t):
    heads_per_group = SSM_HEADS // SSM_GROUPS
    steps_per_group = heads_per_group // SSD_HEADS_PER_STEP
    b_blk = D_MODEL // LANES
    c_blk = (D_MODEL + SSM_BC) // LANES
    return [
        pl.BlockSpec((t, SSD_X_WIDTH), lambda s: (0, s)),
        pl.BlockSpec((t, LANES), lambda s: (0, b_blk + s // steps_per_group)),
        pl.BlockSpec((t, LANES), lambda s: (0, c_blk + s // steps_per_group)),
        pl.BlockSpec((t, LANES), lambda s: (0, s)),
        pl.BlockSpec((3, LANES), lambda s: (0, s)),
    ]


def ssd_fwd(name, xa, pdt, hp):
    t = xa.shape[0]
    nc = t // CHUNK
    nh = SSD_HEADS_PER_STEP

    def body(x_ref, b_ref, c_ref, pdt_ref, hp_ref, y_ref, st_ref, s_scr):
        s_scr[...] = jnp.zeros_like(s_scr)

        def step(ci, carry):
            sl = pl.ds(pl.multiple_of(ci * CHUNK, CHUNK), CHUNK)
            sprev = s_scr[...]
            st_ref[0, ci] = sprev
            y, snew = _ssd_chunk_f(x_ref[sl, :], b_ref[sl, :], c_ref[sl, :], pdt_ref[sl, :], hp_ref[...], sprev)
            y_ref[sl, :] = y
            s_scr[...] = snew
            return carry

        lax.fori_loop(0, nc, step, 0)

    return pl.pallas_call(
        body, name=name, grid=(SSD_STEPS,), in_specs=_ssd_in_specs(t),
        out_specs=[pl.BlockSpec((t, SSD_X_WIDTH), lambda s: (0, s)),
                   pl.BlockSpec((1, nc, SSD_X_WIDTH, SSM_STATE), lambda s: (s, 0, 0, 0))],
        out_shape=[jax.ShapeDtypeStruct((t, D_MODEL), F32),
                   jax.ShapeDtypeStruct((SSD_STEPS, nc, SSD_X_WIDTH, SSM_STATE), F32)],
        scratch_shapes=[pltpu.VMEM((SSD_X_WIDTH, SSM_STATE), F32)],
        compiler_params=_params("parallel"),
    )(xa, xa, xa, pdt, hp)


def ssd_bwd(name, xa, pdt, hp, states, dy):
    t = xa.shape[0]
    nc = t // CHUNK
    nh = SSD_HEADS_PER_STEP
    steps_per_group = SSM_HEADS // SSM_GROUPS // nh

    def body(x_ref, b_ref, c_ref, pdt_ref, hp_ref, st_ref, dy_ref, dx_ref, db_ref, dc_ref, dpdt_ref, dhp_ref, ds_scr, dhp_scr):
        first = pl.program_id(0) % steps_per_group == 0
        ds_scr[...] = jnp.zeros_like(ds_scr)
        dhp_scr[...] = jnp.zeros_like(dhp_scr)

        def step(i, carry):
            ci = nc - 1 - i
            sl = pl.ds(pl.multiple_of(ci * CHUNK, CHUNK), CHUNK)
            _, vjp = jax.vjp(_ssd_chunk_f, x_ref[sl, :], b_ref[sl, :], c_ref[sl, :], pdt_ref[sl, :], hp_ref[...], st_ref[0, ci])
            dx, db, dc, dpdt, dhp, dsprev = vjp((dy_ref[sl, :], ds_scr[...]))
            dx_ref[sl, :] = dx
            dpdt_ref[sl, :] = dpdt.astype(dpdt_ref.dtype)

            @pl.when(first)
            def _():
                db_ref[sl, :] = db
                dc_ref[sl, :] = dc

            @pl.when(jnp.logical_not(first))
            def _():
                db_ref[sl, :] += db
                dc_ref[sl, :] += dc

            ds_scr[...] = dsprev
            dhp_scr[...] += dhp
            return carry

        lax.fori_loop(0, nc, step, 0)
        dhp_ref[...] = dhp_scr[...]

    in_specs = _ssd_in_specs(t) + [
        pl.BlockSpec((1, nc, SSD_X_WIDTH, SSM_STATE), lambda s: (s, 0, 0, 0)),
        pl.BlockSpec((t, SSD_X_WIDTH), lambda s: (0, s)),
    ]
    out_specs = [
        pl.BlockSpec((t, SSD_X_WIDTH), lambda s: (0, s)),
        pl.BlockSpec((t, LANES), lambda s: (0, s // steps_per_group)),
        pl.BlockSpec((t, LANES), lambda s: (0, s // steps_per_group)),
        pl.BlockSpec((t, LANES), lambda s: (0, s)),
        pl.BlockSpec((3, LANES), lambda s: (0, s)),
    ]
    out_shape = [
        jax.ShapeDtypeStruct((t, D_MODEL), F32),
        jax.ShapeDtypeStruct((t, SSM_BC), F32),
        jax.ShapeDtypeStruct((t, SSM_BC), F32),
        jax.ShapeDtypeStruct((t, SSD_STEPS * LANES), BF16),
        jax.ShapeDtypeStruct((3, SSD_STEPS * LANES), F32),
    ]
    return pl.pallas_call(
        body, name=name, grid=(SSD_STEPS,), in_specs=in_specs, out_specs=out_specs, out_shape=out_shape,
        scratch_shapes=[pltpu.VMEM((SSD_X_WIDTH, SSM_STATE), F32), pltpu.VMEM((3, LANES), F32)],
        compiler_params=_params("arbitrary"),
    )(xa, xa, xa, pdt, hp, states, dy)


ATTN_TQ = 256
ATTN_KSTEP = 512


def _attn_extents(t):
    return [min(t, (g + 1) * ATTN_KSTEP) for g in range(-(-t // ATTN_KSTEP))]


def _attn_f(q0, qh, kn, kpe, v, cos, sin):
    p = jax.nn.softmax(_attn_scores(q0, 0, qh, kn, kpe, cos, sin), axis=-1)
    return (jnp.dot(p.astype(BF16), v.astype(BF16), preferred_element_type=F32),)


def _attn_in_specs(t):
    return [
        pl.BlockSpec((ATTN_TQ, MLA_QPAD), lambda h, qi: (qi, h)),
        pl.BlockSpec((t, MLA_NOPE), lambda h, qi: (0, h)),
        pl.BlockSpec((t, LANES), lambda h, qi: (0, 0)),
        pl.BlockSpec((t, MLA_V), lambda h, qi: (0, h)),
        pl.BlockSpec((ATTN_TQ, LANES), lambda h, qi: (qi, 0)),
        pl.BlockSpec((ATTN_TQ, LANES), lambda h, qi: (qi, 0)),
    ]


def attn_fwd(name, q, kn, kpe, v, cos, sin):
    t = q.shape[0]

    def body(q_ref, kn_ref, kpe_ref, v_ref, cos_ref, sin_ref, o_ref):
        qi = pl.program_id(1)
        for span, ext in enumerate(_attn_extents(t)):
            @pl.when(qi // (ATTN_KSTEP // ATTN_TQ) == span)
            def _(ext=ext):
                (o,) = _attn_f(qi * ATTN_TQ, q_ref[...], kn_ref[0:ext, :], kpe_ref[0:ext, :], v_ref[0:ext, :],
                               cos_ref[...], sin_ref[...])
                o_ref[...] = o.astype(o_ref.dtype)

    return pl.pallas_call(
        body, name=name, grid=(MLA_HEADS, t // ATTN_TQ), in_specs=_attn_in_specs(t),
        out_specs=pl.BlockSpec((ATTN_TQ, MLA_V), lambda h, qi: (qi, h)),
        out_shape=jax.ShapeDtypeStruct((t, MLA_HEADS * MLA_V), BF16),
        compiler_params=_params("parallel", "parallel"),
    )(q, kn, kpe, v, cos, sin)


def attn_bwd(name, q, kn, kpe, v, cos, sin, do, after=None):
    t = q.shape[0]
    extra = [] if after is None else [after]

    def body(q_ref, kn_ref, kpe_ref, v_ref, cos_ref, sin_ref, do_ref, *rest):
        dq_ref, dkn_ref, dkpe_ref, dv_ref = rest[len(extra):]
        h, qi = pl.program_id(0), pl.program_id(1)
        q0 = qi * ATTN_TQ
        cos, sin = cos_ref[...], sin_ref[...]

        @pl.when(qi == 0)
        def _():
            dkn_ref[...] = jnp.zeros_like(dkn_ref)
            dv_ref[...] = jnp.zeros_like(dv_ref)

        @pl.when(jnp.logical_and(h == 0, qi == 0))
        def _():
            dkpe_ref[...] = jnp.zeros_like(dkpe_ref)

        for span, ext in enumerate(_attn_extents(t)):
            @pl.when(qi // (ATTN_KSTEP // ATTN_TQ) == span)
            def _(ext=ext):
                def g(qh, knv, kpev, vv):
                    return _attn_f(q0, qh, knv, kpev, vv, cos, sin)

                _, vjp = jax.vjp(g, q_ref[...].astype(F32), kn_ref[0:ext, :].astype(F32), kpe_ref[0:ext, :].astype(F32),
                                 v_ref[0:ext, :].astype(F32))
                dq, dkn, dkpe, dv = vjp((do_ref[...].astype(F32),))
                dq_ref[...] = dq.astype(dq_ref.dtype)
                dkn_ref[0:ext, :] += dkn
                dkpe_ref[0:ext, :] += dkpe
                dv_ref[0:ext, :] += dv

    in_specs = _attn_in_specs(t) + [pl.BlockSpec((ATTN_TQ, MLA_V), lambda h, qi: (qi, h))]
    in_specs += [pl.BlockSpec(memory_space=pl.ANY)] * len(extra)
    out_specs = [
        pl.BlockSpec((ATTN_TQ, MLA_QPAD), lambda h, qi: (qi, h)),
        pl.BlockSpec((t, MLA_NOPE), lambda h, qi: (0, h)),
        pl.BlockSpec((t, LANES), lambda h, qi: (0, 0)),
        pl.BlockSpec((t, MLA_V), lambda h, qi: (0, h)),
    ]
    out_shape = [
        jax.ShapeDtypeStruct((t, MLA_HEADS * MLA_QPAD), BF16),
        jax.ShapeDtypeStruct((t, MLA_HEADS * MLA_NOPE), F32),
        jax.ShapeDtypeStruct((t, LANES), F32),
        jax.ShapeDtypeStruct((t, MLA_HEADS * MLA_V), F32),
    ]
    return pl.pallas_call(
        body, name=name, grid=(MLA_HEADS, t // ATTN_TQ), in_specs=in_specs, out_specs=out_specs, out_shape=out_shape,
        compiler_params=_params("arbitrary", "arbitrary"),
    )(q, kn, kpe, v, cos, sin, do, *extra)


def final_loss(name, h, nf, target, tr=256):
    t, d = h.shape

    def body(h_ref, w_ref, t_ref, loss_ref, dh_ref, dhb_ref, dw_ref):
        i = pl.program_id(0)
        tgt = t_ref[...]

        def f(hv, wv):
            err = _rms(hv, wv) - tgt
            return 0.5 * jnp.sum(jnp.mean(err * err, -1, keepdims=True), 0, keepdims=True)

        val, vjp = jax.vjp(f, h_ref[...], w_ref[...])
        dh, dw = vjp(jnp.ones((1, 1), F32))
        dh_ref[...] = dh
        dhb_ref[...] = dh.astype(dhb_ref.dtype)
        tile = jnp.broadcast_to(val, loss_ref.shape)

        @pl.when(i == 0)
        def _():
            loss_ref[...] = tile
            dw_ref[...] = dw

        @pl.when(i > 0)
        def _():
            loss_ref[...] += tile
            dw_ref[...] += dw

    row = pl.BlockSpec((tr, d), lambda i: (i, 0))
    return pl.pallas_call(
        body, name=name, grid=(t // tr,), in_specs=[row, _full_spec(nf), row],
        out_specs=[pl.BlockSpec((8, LANES), lambda i: (0, 0)), row, row, _full_spec(nf)],
        out_shape=[jax.ShapeDtypeStruct((8, LANES), F32), jax.ShapeDtypeStruct((t, d), F32), jax.ShapeDtypeStruct((t, d), BF16),
                   jax.ShapeDtypeStruct(nf.shape, F32)],
        compiler_params=_params("arbitrary"),
    )(h, nf, target)


ANY = pl.BlockSpec(memory_space=pl.ANY)
CHIP_ORDER = ((0, 0), (0, 1), (1, 0), (1, 1))


def _place():
    return lax.axis_index("x"), lax.axis_index("y"), lax.axis_index("c")


def _other_chips(x, y):
    return [(1 - x, y), (x, 1 - y), (1 - x, 1 - y)]


def _device_slot():
    x, y, c = _place()
    return 4 * x + 2 * y + c


def _row_tile(rows, cap=128):
    return next(t for t in (512, 256, 128, 64, 32, 16) if t <= cap and rows % t == 0)


def all_gather(name, shards):
    n = len(shards)

    def body(*refs):
        ins, outs = refs[:n], refs[n:2 * n]
        send_sems, recv_sems, local_sems = refs[2 * n:]
        x, y, c = _place()
        me, sibling = (x, y, c), (x, y, 1 - c)
        chips = _other_chips(x, y)

        def copy(k, j, block, to, from_input=False):
            dst = outs[k].at[4 * block[0] + 2 * block[1] + block[2]]
            return pltpu.make_async_remote_copy(
                src_ref=ins[k] if from_input else dst, dst_ref=dst,
                send_sem=send_sems.at[7 * k + j], recv_sem=recv_sems.at[7 * k + j],
                device_id=to, device_id_type=MESH)

        mine = [pltpu.make_async_copy(ins[k], outs[k].at[4 * x + 2 * y + c], local_sems.at[k]) for k in range(n)]
        for cp in mine:
            cp.start()
        first = []
        for k in range(n):
            first.append(copy(k, 0, me, sibling, True))
            first += [copy(k, 1 + j, me, (*chip, c), True) for j, chip in enumerate(chips)]
        for cp in first:
            cp.start()
        passed = []
        for j, chip in enumerate(chips):
            for k in range(n):
                copy(k, 1 + j, (*chip, c), me).wait_recv()
                fwd = copy(k, 4 + j, (*chip, c), sibling)
                fwd.start()
                passed.append(fwd)
        for k in range(n):
            copy(k, 0, sibling, me).wait_recv()
        for j, chip in enumerate(chips):
            for k in range(n):
                copy(k, 4 + j, (*chip, 1 - c), me).wait_recv()
        for cp in first + passed:
            cp.wait_send()
        for cp in mine:
            cp.wait()

    return pl.pallas_call(
        body, name=name, in_specs=[ANY] * n, out_specs=[ANY] * n,
        out_shape=[jax.ShapeDtypeStruct((N_DEV,) + s.shape, s.dtype) for s in shards],
        scratch_shapes=[pltpu.SemaphoreType.DMA((7 * n,)), pltpu.SemaphoreType.DMA((7 * n,)), pltpu.SemaphoreType.DMA((n,))],
    )(*shards)


def exchange_sibling(name, gs, after=None):
    n = len(gs)
    extra = [] if after is None else [after]

    def body(*refs):
        ins, outs = refs[:n], refs[n + len(extra):2 * n + len(extra)]
        send_sems, recv_sems = refs[2 * n + len(extra):]
        x, y, c = _place()
        copies = []
        for k in range(n):
            for q, (cx, cy) in enumerate(CHIP_ORDER):
                copies.append(pltpu.make_async_remote_copy(
                    src_ref=ins[k].at[4 * cx + 2 * cy + (1 - c)], dst_ref=outs[k].at[q],
                    send_sem=send_sems.at[4 * k + q], recv_sem=recv_sems.at[4 * k + q],
                    device_id=(x, y, 1 - c), device_id_type=MESH))
        for cp in copies:
            cp.start()
        for cp in copies:
            cp.wait()

    return pl.pallas_call(
        body, name=name, in_specs=[ANY] * (n + len(extra)), out_specs=[ANY] * n,
        out_shape=[jax.ShapeDtypeStruct((4,) + g.shape[1:], g.dtype) for g in gs],
        scratch_shapes=[pltpu.SemaphoreType.DMA((4 * n,)), pltpu.SemaphoreType.DMA((4 * n,))],
    )(*gs, *extra)


def chip_sums(name, g, recv, tr=128):
    _, r, c = g.shape

    def body(g_ref, r_ref, o_ref):
        o_ref[...] = (g_ref[...].astype(F32) + r_ref[...].astype(F32)).astype(o_ref.dtype)

    def chip(i):
        x, y, _ = _place()
        return jnp.where(i % 2 == 1, 1 - x, x), jnp.where(i >= 2, 1 - y, y)

    def g_index(i, j):
        cx, cy = chip(i)
        return 4 * cx + 2 * cy + lax.axis_index("c"), j, 0

    def recv_index(i, j):
        cx, cy = chip(i)
        return 2 * cx + cy, j, 0

    return pl.pallas_call(
        body, name=name, grid=(4, r // tr),
        in_specs=[pl.BlockSpec((1, tr, c), g_index), pl.BlockSpec((1, tr, c), recv_index)],
        out_specs=pl.BlockSpec((1, tr, c), lambda i, j: (i, j, 0)),
        out_shape=jax.ShapeDtypeStruct((4, r, c), g.dtype),
        compiler_params=_params("parallel", "parallel"),
    )(g, recv)


def sum_parts(name, parts, tr=128):
    _, r, c = parts[0][0].shape

    def body(*refs):
        total = refs[0][0].astype(F32)
        for ref in refs[1:-1]:
            total = total + ref[0].astype(F32)
        refs[-1][...] = total

    return pl.pallas_call(
        body, name=name, grid=(r // tr,),
        in_specs=[pl.BlockSpec((1, tr, c), lambda i, s=s: (s, i, 0)) for _, s in parts],
        out_specs=pl.BlockSpec((tr, c), lambda i: (i, 0)), out_shape=jax.ShapeDtypeStruct((r, c), F32),
        compiler_params=_params("parallel"),
    )(*[a for a, _ in parts])


def adamw(name, parts, w, m, v, tr=128, part=0, prev=None, after=None):
    _, r, c = parts[0][0].shape
    np_ = len(parts)
    first = part * (r // tr)

    def body(*refs):
        g = refs[0][0].astype(F32)
        for ref in refs[1:np_]:
            g = g + ref[0].astype(F32)
        w_ref, m_ref, v_ref = refs[np_:np_ + 3]
        g_out, d_out, m_out, v_out = refs[-4:]
        new_m = ADAM_B1 * m_ref[...] + (1.0 - ADAM_B1) * g
        new_v = ADAM_B2 * v_ref[...] + (1.0 - ADAM_B2) * (g * g)
        m_hat = new_m / (1.0 - ADAM_B1 ** ADAM_STEP)
        v_hat = new_v / (1.0 - ADAM_B2 ** ADAM_STEP)
        g_out[...] = g
        d_out[...] = -ADAM_LR * (m_hat / (jnp.sqrt(v_hat) + ADAM_EPS) + ADAM_WD * w_ref[...])
        m_out[...] = new_m
        v_out[...] = new_v

    tile = pl.BlockSpec((tr, c), lambda i: (first + i, 0))
    in_specs = [pl.BlockSpec((1, tr, c), lambda i, s=s: (s, i, 0)) for _, s in parts] + [tile] * 3
    ins = [a for a, _ in parts] + [w, m, v]
    aliases = {}
    if prev is not None:
        aliases = {len(ins) + k: k for k in range(4)}
        in_specs += [ANY] * 4
        ins += list(prev)
    if after is not None:
        in_specs.append(ANY)
        ins.append(after)
    return pl.pallas_call(
        body, name=name, grid=(r // tr,), in_specs=in_specs,
        out_specs=[tile] * 4, out_shape=[jax.ShapeDtypeStruct(w.shape, F32)] * 4,
        input_output_aliases=aliases, compiler_params=_params("parallel"),
    )(*ins)


HBM = pl.BlockSpec(memory_space=pltpu.HBM)
SEM = pl.BlockSpec(memory_space=pltpu.SEMAPHORE)
SIDE_EFFECT = pltpu.SideEffectType.DATAFLOW_SIDE_EFFECTING


def _split_copies(plan, src_refs, land_refs, send_sems, recv_sems):
    copies = []
    for i, (k, src_slot, land_slot, device) in enumerate(plan(*_place())):
        copies.append(pltpu.make_async_remote_copy(
            src_ref=src_refs[k] if src_slot is None else src_refs[k].at[src_slot], dst_ref=land_refs[k].at[land_slot],
            send_sem=send_sems.at[i], recv_sem=recv_sems.at[i], device_id=device, device_id_type=MESH))
    return copies


def split_start(name, srcs, land_shapes, plan, n_copies):
    n = len(srcs)

    def body(*refs):
        src_refs, land_refs = refs[:n], refs[n:2 * n]
        send_sems, recv_sems, token = refs[2 * n], refs[2 * n + 1], refs[-1]
        for cp in _split_copies(plan, src_refs, land_refs, send_sems, recv_sems):
            cp.start()
        token[...] = jnp.zeros_like(token)

    lands = [lax.empty(shape, s.dtype) for shape, s in zip(land_shapes, srcs)]
    ins = [pltpu.with_memory_space_constraint(a, pltpu.HBM) for a in list(srcs) + lands]
    out = pl.pallas_call(
        body, name=name,
        out_shape=(pltpu.SemaphoreType.DMA((n_copies,)), pltpu.SemaphoreType.DMA((n_copies,)),
                   *[pltpu.HBM(a.shape, a.dtype) for a in ins], jax.ShapeDtypeStruct((8, LANES), F32)),
        in_specs=[HBM] * (2 * n), out_specs=(SEM, SEM, *[HBM] * (2 * n), pl.BlockSpec(memory_space=pltpu.VMEM)),
        input_output_aliases={i: 2 + i for i in range(2 * n)},
        compiler_params=pltpu.CompilerParams(has_side_effects=SIDE_EFFECT),
    )(*ins)
    return out[0], out[1], list(out[2:2 + n]), list(out[2 + n:2 + 2 * n]), out[-1]


def split_wait(name, handle, plan, after):
    send_sems, recv_sems, srcs, lands, _ = handle
    n = len(srcs)
    after = list(after) if isinstance(after, (list, tuple)) else [after]

    def body(*refs):
        src_refs, land_refs = refs[:n], refs[n:2 * n]
        for cp in _split_copies(plan, src_refs, land_refs, refs[2 * n], refs[2 * n + 1]):
            cp.wait_send()
            cp.wait_recv()

    out = pl.pallas_call(
        body, name=name, out_shape=tuple(pltpu.HBM(a.shape, a.dtype) for a in srcs + lands),
        in_specs=[HBM] * (2 * n) + [SEM, SEM] + [ANY] * len(after), out_specs=tuple([HBM] * (2 * n)),
        input_output_aliases={i: i for i in range(2 * n)},
        compiler_params=pltpu.CompilerParams(has_side_effects=SIDE_EFFECT),
    )(*srcs, *lands, send_sems, recv_sems, *after)
    return list(out[:n]), list(out[n:])


def gather_plan(n):
    def plan(x, y, c):
        me = 4 * x + 2 * y + c
        peers = [(x, y, 1 - c)] + [(*chip, c) for chip in _other_chips(x, y)]
        return [(k, None, me, peer) for k in range(n) for peer in peers]
    return plan


def sibling_plan(n):
    def plan(x, y, c):
        return [(k, 4 * cx + 2 * cy + (1 - c), q, (x, y, 1 - c)) for k in range(n) for q, (cx, cy) in enumerate(CHIP_ORDER)]
    return plan


def chips_plan(n):
    def plan(x, y, c):
        return [(k, 1 + j, j, (*chip, c)) for k in range(n) for j, chip in enumerate(_other_chips(x, y))]
    return plan


def gather_finish(name, gathered):
    n = len(gathered)

    def body(*refs):
        outs = refs[n:2 * n]
        send_sems, recv_sems = refs[2 * n:]
        x, y, c = _place()

        def passed_on(k, j, core):
            cx, cy = _other_chips(x, y)[j]
            blk = outs[k].at[4 * cx + 2 * cy + core]
            return pltpu.make_async_remote_copy(
                src_ref=blk, dst_ref=blk, send_sem=send_sems.at[3 * k + j], recv_sem=recv_sems.at[3 * k + j],
                device_id=(x, y, 1 - c), device_id_type=MESH)

        pairs = [(k, j) for k in range(n) for j in range(3)]
        sends = [passed_on(k, j, c) for k, j in pairs]
        for cp in sends:
            cp.start()
        for k, j in pairs:
            passed_on(k, j, 1 - c).wait_recv()
        for cp in sends:
            cp.wait_send()

    return pl.pallas_call(
        body, name=name, in_specs=[ANY] * n, out_specs=[ANY] * n,
        out_shape=[jax.ShapeDtypeStruct(g.shape, g.dtype) for g in gathered],
        input_output_aliases={k: k for k in range(n)},
        scratch_shapes=[pltpu.SemaphoreType.DMA((3 * n,)), pltpu.SemaphoreType.DMA((3 * n,))],
    )(*gathered)


ROW_TILE = 256
COL_TILE = 256


def _rms_fwd(tag, h, w, after=None):
    return rows_fwd(tag, _rms_f, [(h, D_MODEL, 0)], [w], [(D_MODEL, BF16)], ROW_TILE, after=after)[0]


def _rms_bwd(tag, h, w, dhn, dres):
    dh, (dw,) = rows_bwd(tag, _rms_f, [(h, D_MODEL, 0)], [w], [(dhn, D_MODEL, 0)], ROW_TILE, [(F32, BF16)], add=dres)
    return tuple(dh), dw


def even_fwd(tag, h, w, p, after=None):
    hn = _rms_fwd(tag + "_rms", h, p["nm"], after)
    uv = matmul(tag + "_uv", hn, w["uv"])
    z = matmul(tag + "_z", hn, w["z"])
    xbc = matmul(tag + "_xbc", hn, w["xbc"])
    pdt = matmul(tag + "_dt", hn, w["dt"])
    gm = [p["lng"], p["lnb"], p["ws"], p["bst"]]
    ya = rows_fwd(tag + "_gmlp", _gmlp_f, [(uv, 2 * D_MODEL, 0)], gm, [(D_MODEL, BF16)], GM_BLOCK)[0]
    xa = cols_fwd(tag + "_conv", _conv_silu_f, [(xbc, 0)], [p["cw"], p["cb"]], F32, COL_TILE)
    y, states = ssd_fwd(tag + "_ssd", xa, pdt, p["hp"])
    yb = rows_fwd(tag + "_gate", _gate_norm_f, [(y, D_MODEL, 0), (z, D_MODEL, 0)], [p["nw"]], [(D_MODEL, BF16)], ROW_TILE)[0]
    h1 = matmul(tag + "_out_b", yb, w["out_bot"], res=matmul(tag + "_out_a", ya, w["out_top"], res=h))
    return h1, dict(h=h, hn=hn, uv=uv, z=z, xbc=xbc, pdt=pdt, xa=xa, y=y, states=states, ya=ya, yb=yb)


def even_bwd(tag, dh1, s, w, p, after=None, hook=None):
    dh1, dh1b = dh1
    dya = matmul(tag + "_dya", dh1b, w["out_top"], tb=True, after=after)
    dyb = matmul(tag + "_dyb", dh1b, w["out_bot"], tb=True, after=after)
    later = hook(dyb) if hook else None
    gw = dict(out_top=matmul(tag + "_gwa", s["ya"], dh1b, ta=True, out_dtype=BF16, after=later),
              out_bot=matmul(tag + "_gwb", s["yb"], dh1b, ta=True, out_dtype=BF16, after=later))
    (dy, dz), (dnw,) = rows_bwd(tag + "_gate_b", _gate_norm_f, [(s["y"], D_MODEL, 0), (s["z"], D_MODEL, 0)], [p["nw"]],
                                [(dyb, D_MODEL, 0)], ROW_TILE, [F32, BF16], after=later)
    dxs, dbm, dcm, dpdt, dhp = ssd_bwd(tag + "_ssd_b", s["xa"], s["pdt"], p["hp"], s["states"], dy)
    dxa = jnp.concatenate([dxs, dbm, dcm], axis=1)
    (dxbc,), (dcw, dcb) = cols_bwd(tag + "_conv_b", _conv_silu_f, [(s["xbc"], 0)], [p["cw"], p["cb"]], dxa, COL_TILE, BF16)
    gm = [p["lng"], p["lnb"], p["ws"], p["bst"]]
    (duv,), (dlng, dlnb, dws, dbst) = rows_bwd(tag + "_gmlp_b", _gmlp_f, [(s["uv"], 2 * D_MODEL, 0)], gm,
                                               [(dya, D_MODEL, 0)], GM_BLOCK, [BF16])
    dhn = None
    for key, d in (("uv", duv), ("z", dz), ("xbc", dxbc), ("dt", dpdt)):
        dhn = matmul(f"{tag}_dx_{key}", d, w[key], tb=True, res=dhn)
        gw[key] = matmul(f"{tag}_gw_{key}", s["hn"], d, ta=True, out_dtype=BF16)
    dh, dnm = _rms_bwd(tag + "_rms_b", s["h"], p["nm"], dhn, dh1)
    gp = dict(nm=dnm, lng=dlng, lnb=dlnb, ws=dws, bst=dbst, cw=dcw, cb=dcb, hp=dhp, nw=dnw)
    return dh, gw, gp


def odd_fwd(tag, h, w, p, cos, sin, after=None):
    hn = _rms_fwd(tag + "_rms", h, p["nm"], after)
    proj = matmul(tag + "_in", hn, w["in"])
    cq, ckv, kpe = rows_fwd(tag + "_qkvn", _qkv_norm_f, [(proj, ODD_IN_PAD, 0), (cos, LANES, 0), (sin, LANES, 0)],
                            [p["qn"], p["kvn"]], [(MLA_RANK, BF16), (MLA_RANK, BF16), (LANES, F32)], ROW_TILE)
    q = matmul(tag + "_q", cq, w["uq"])
    kn = matmul(tag + "_kn", ckv, w["kn"], out_dtype=BF16)
    v = matmul(tag + "_v", ckv, w["v"], out_dtype=BF16)
    o = attn_fwd(tag + "_attn", q, kn, kpe, v, cos, sin)
    h1 = matmul(tag + "_o", o, w["o"], res=h)
    return h1, dict(h=h, hn=hn, proj=proj, cq=cq, ckv=ckv, kpe=kpe, q=q, kn=kn, v=v, o=o)


def odd_bwd(tag, dh1, s, w, p, cos, sin, after=None, hook=None):
    dh1, dh1b = dh1
    do = matmul(tag + "_do", dh1b, w["o"], tb=True, after=after)
    later = hook(do) if hook else None
    gw = dict(o=matmul(tag + "_gw_o", s["o"], dh1b, ta=True, out_dtype=BF16, after=later))
    dq, dkn, dkpe, dv = attn_bwd(tag + "_attn_b", s["q"], s["kn"], s["kpe"], s["v"], cos, sin, do, after=later)
    dcq = matmul(tag + "_dcq", dq, w["uq"], tb=True)
    gw["uq"] = matmul(tag + "_gw_uq", s["cq"], dq, ta=True, out_dtype=BF16)
    dckv = matmul(tag + "_dckv_v", dv, w["v"], tb=True, res=matmul(tag + "_dckv_k", dkn, w["kn"], tb=True))
    gw["kn"] = matmul(tag + "_gw_kn", s["ckv"], dkn, ta=True, out_dtype=BF16)
    gw["v"] = matmul(tag + "_gw_v", s["ckv"], dv, ta=True, out_dtype=BF16)
    (dproj,), (dqn, dkvn) = rows_bwd(
        tag + "_qkvn_b", _qkv_norm_f, [(s["proj"], ODD_IN_PAD, 0), (cos, LANES, 0), (sin, LANES, 0)], [p["qn"], p["kvn"]],
        [(dcq, MLA_RANK, 0), (dckv, MLA_RANK, 0), (dkpe, LANES, 0)], ROW_TILE, [BF16], n_nondiff=2)
    dhn = matmul(tag + "_dx_in", dproj, w["in"], tb=True)
    gw["in"] = matmul(tag + "_gw_in", s["hn"], dproj, ta=True, out_dtype=BF16)
    dh, dnm = _rms_bwd(tag + "_rms_b", s["h"], p["nm"], dhn, dh1)
    return dh, gw, dict(nm=dnm, qn=dqn, kvn=dkvn)


def ffn_fwd(tag, h, w, p, after=None):
    hn = _rms_fwd(tag + "_rms", h, p["nf"], after)
    g = matmul(tag + "_up_g", hn, w["up_g"])
    val = matmul(tag + "_up_v", hn, w["up_v"])
    act = cols_fwd(tag + "_act", _ffn_act_f, [(g, 0), (val, 0)], [p["fcw"], p["fcb"]], BF16, COL_TILE)
    h2 = matmul(tag + "_down", act, w["down"], res=h)
    return h2, dict(h=h, hn=hn, g=g, val=val, act=act)


def ffn_bwd(tag, dh2, s, w, p, after=None, hook=None):
    dh2, dh2b = dh2
    dact = matmul(tag + "_dact", dh2b, w["down"], tb=True, after=after)
    later = hook(dact) if hook else None
    gw = dict(down=matmul(tag + "_gw_down", s["act"], dh2b, ta=True, out_dtype=BF16, after=later))
    (dg, dval), (dfcw, dfcb) = cols_bwd(tag + "_act_b", _ffn_act_f, [(s["g"], 0), (s["val"], 0)], [p["fcw"], p["fcb"]],
                                        dact, COL_TILE, BF16, after=later)
    dhn = matmul(tag + "_dx_v", dval, w["up_v"], tb=True, res=matmul(tag + "_dx_g", dg, w["up_g"], tb=True))
    gw["up"] = matmul_tn_slots(tag + "_gw_up", s["hn"], [dg, dval], 2 * D_FF // N_DEV)
    dh, dnf = _rms_bwd(tag + "_rms_b", s["h"], p["nf"], dhn, dh2)
    return dh, gw, dict(nf=dnf, fcw=dfcw, fcb=dfcb)


def _cols_from_slots(g):
    return jnp.moveaxis(g, 0, 1).reshape(g.shape[1], N_DEV * g.shape[2])


def _slots_from_cols(wmat):
    k, n = wmat.shape
    return jnp.moveaxis(wmat.reshape(k, N_DEV, n // N_DEV), 1, 0)


def _pad_last(a, width):
    return jnp.pad(a, [(0, 0)] * (a.ndim - 1) + [(0, width - a.shape[-1])])


def _heads_to_lanes(a):
    lead = a.shape[:-1]
    return _pad_last(a.reshape(lead + (SSD_STEPS, SSD_HEADS_PER_STEP)), LANES).reshape(lead + (SSD_STEPS * LANES,))


def _lanes_to_heads(a):
    lead = a.shape[:-1]
    return a.reshape(lead + (SSD_STEPS, LANES))[..., :SSD_HEADS_PER_STEP].reshape(lead + (SSM_HEADS,))


def prep_even(g_in, g_out):
    wn = _cols_from_slots(g_in)
    o1, o2, o3 = 2 * D_MODEL, 3 * D_MODEL, 3 * D_MODEL + SSM_CONV_DIM
    out = g_out.reshape(2 * D_MODEL, D_MODEL)
    return dict(uv=wn[:, :o1], z=wn[:, o1:o2], xbc=wn[:, o2:o3], dt=_heads_to_lanes(wn[:, o3:]),
                out_top=out[:D_MODEL], out_bot=out[D_MODEL:])


def unprep_even(gw):
    wn = jnp.concatenate([gw["uv"], gw["z"], gw["xbc"], _lanes_to_heads(gw["dt"])], axis=1)
    return _slots_from_cols(wn), jnp.concatenate([gw["out_top"], gw["out_bot"]], axis=0).reshape(N_DEV, -1, D_MODEL)


def prep_odd(g_in, g_uq, g_ukv, g_o):
    uq = _cols_from_slots(g_uq).reshape(MLA_RANK, MLA_HEADS, MLA_QK)
    ukv = _cols_from_slots(g_ukv).reshape(MLA_RANK, MLA_HEADS, MLA_NOPE + MLA_V)
    return dict(**{"in": _pad_last(g_in.reshape(D_MODEL, ODD_IN), ODD_IN_PAD)},
                uq=_pad_last(uq, MLA_QPAD).reshape(MLA_RANK, MLA_HEADS * MLA_QPAD),
                kn=ukv[:, :, :MLA_NOPE].reshape(MLA_RANK, MLA_HEADS * MLA_NOPE),
                v=ukv[:, :, MLA_NOPE:].reshape(MLA_RANK, MLA_HEADS * MLA_V),
                o=g_o.reshape(MLA_HEADS * MLA_V, D_MODEL))


def unprep_odd(gw):
    uq = gw["uq"].reshape(MLA_RANK, MLA_HEADS, MLA_QPAD)[:, :, :MLA_QK].reshape(MLA_RANK, MLA_HEADS * MLA_QK)
    ukv = jnp.concatenate([gw["kn"].reshape(MLA_RANK, MLA_HEADS, MLA_NOPE), gw["v"].reshape(MLA_RANK, MLA_HEADS, MLA_V)], axis=2)
    return (gw["in"][:, :ODD_IN].reshape(N_DEV, -1, ODD_IN), _slots_from_cols(uq),
            _slots_from_cols(ukv.reshape(MLA_RANK, -1)), gw["o"].reshape(N_DEV, -1, D_MODEL))


def prep_ffn(g_up, g_down):
    up = _cols_from_slots(g_up)
    return dict(up_g=up[:, :D_FF], up_v=up[:, D_FF:], down=g_down.reshape(D_FF, D_MODEL))


def unprep_ffn(gw):
    return gw["up"], gw["down"].reshape(N_DEV, -1, D_MODEL)


SMALL_TILE = LANES * LANES


def _pack(arrs):
    flat = jnp.concatenate([a.reshape(-1).astype(F32) for a in arrs])
    size = -(-flat.shape[0] // SMALL_TILE) * SMALL_TILE
    return jnp.pad(flat, (0, size - flat.shape[0])).reshape(-1, LANES)


def _unpack(packed, shapes, lead=()):
    flat = packed.reshape(lead + (-1,))
    out, off = [], 0
    for shp in shapes:
        size = math.prod(shp)
        out.append(flat[..., off:off + size].reshape(lead + tuple(shp)))
        off += size
    return out


SMALL_SHARDED = {"ev_gm_ln_g": 2, "ev_gm_ln_b": 2, "ev_conv_w": 2, "od_q_norm": 1, "od_kv_norm": 1, "ff_conv_w": 2}
SMALL_REPLICATED = ["norm_mix", "norm_ffn", "norm_final", "ev_gm_ws", "ev_gm_bs", "ev_conv_b", "ev_dt_bias", "ev_a_log",
                    "ev_d_skip", "ev_ssm_norm_w", "ff_conv_b"]
MATRICES = {"ev_w_in": (2, 2048, 1156), "ev_w_out": (2, 512, 2048), "od_w_in": (2, 256, 1088), "od_w_uq": (2, 512, 384),
            "od_w_ukv": (2, 512, 512), "od_w_o": (2, 256, 2048), "ff_w_up": (4, 2048, 1408), "ff_w_down": (4, 704, 2048)}
WEIGHT_ORDER = ["norm_mix", "norm_ffn", "norm_final", "ev_w_in", "ev_gm_ln_g", "ev_gm_ln_b", "ev_gm_ws", "ev_gm_bs",
                "ev_conv_w", "ev_conv_b", "ev_dt_bias", "ev_a_log", "ev_d_skip", "ev_ssm_norm_w", "ev_w_out", "od_w_in",
                "od_q_norm", "od_kv_norm", "od_w_uq", "od_w_ukv", "od_w_o", "ff_w_up", "ff_conv_w", "ff_conv_b", "ff_w_down"]


def _full_from_shards(name, gathered):
    ax = SMALL_SHARDED[name]
    moved = jnp.moveaxis(gathered, 0, ax)
    shp = moved.shape
    return moved.reshape(shp[:ax] + (shp[ax] * shp[ax + 1],) + shp[ax + 2:])


def _my_shard(name, full, dev):
    ax = SMALL_SHARDED[name]
    shp = full.shape
    split = full.reshape(shp[:ax] + (N_DEV, shp[ax] // N_DEV) + shp[ax + 1:])
    return lax.dynamic_index_in_dim(split, dev, axis=ax, keepdims=False)


def _even_small(sm, j):
    row = lambda a: a.reshape(1, -1)
    hp = jnp.stack([sm["ev_dt_bias"][j], sm["ev_a_log"][j], sm["ev_d_skip"][j]])
    return dict(nm=row(sm["norm_mix"][2 * j]), lng=row(sm["ev_gm_ln_g"][j]), lnb=row(sm["ev_gm_ln_b"][j]),
                ws=sm["ev_gm_ws"][j], bst=sm["ev_gm_bs"][j].T, cw=sm["ev_conv_w"][j], cb=row(sm["ev_conv_b"][j]),
                hp=_heads_to_lanes(hp), nw=row(sm["ev_ssm_norm_w"][j]))


def _odd_small(sm, j):
    row = lambda a: a.reshape(1, -1)
    return dict(nm=row(sm["norm_mix"][2 * j + 1]), qn=row(sm["od_q_norm"][j]), kvn=row(sm["od_kv_norm"][j]))


def _ffn_small(sm, layer):
    row = lambda a: a.reshape(1, -1)
    return dict(nf=row(sm["norm_ffn"][layer]), fcw=sm["ff_conv_w"][layer], fcb=row(sm["ff_conv_b"][layer]))


def _rope_tables(positions):
    inv_freq = ROPE_THETA ** (-jnp.arange(0, MLA_ROPE, 2, dtype=F32) / MLA_ROPE)
    ang = positions.astype(F32).reshape(-1, 1) * inv_freq
    cos, sin = jnp.cos(ang), jnp.sin(ang)
    return _pad_last(jnp.concatenate([cos, cos], axis=1), LANES), _pad_last(jnp.concatenate([-sin, sin], axis=1), LANES)


def local_step(x, positions, target, sm, fetch_weights, emit_grads):
    cos, sin = _rope_tables(positions)
    h, saved = x, []
    for layer in range(4):
        j, tag = layer // 2, f"l{layer}"
        wm, dep = fetch_weights(2 * layer, h)
        if layer % 2 == 0:
            pm = _even_small(sm, j)
            h, sv = even_fwd(tag, h, wm, pm, dep)
        else:
            pm = _odd_small(sm, j)
            h, sv = odd_fwd(tag, h, wm, pm, cos, sin, dep)
        wf, dep = fetch_weights(2 * layer + 1, h)
        pf = _ffn_small(sm, layer)
        h, sf = ffn_fwd(tag + "f", h, wf, pf, dep)
        saved.append((pm, sv, pf, sf, wm, wf))
    loss_tile, dh32, dh16, dnfinal = final_loss("final_loss", h, sm["norm_final"].reshape(1, -1), target)
    gs = {k: [None] * v.shape[0] for k, v in sm.items() if k != "norm_final"}
    gs["norm_final"] = dnfinal.reshape(-1)
    dh = (dh32, dh16)
    dep, hook = None, None
    for layer in reversed(range(4)):
        j, tag = layer // 2, f"l{layer}"
        pm, sv, pf, sf, wm, wf = saved[layer]
        dh, gwf, gpf = ffn_bwd(tag + "f", dh, sf, wf, pf, dep, hook)
        gs["norm_ffn"][layer], gs["ff_conv_w"][layer], gs["ff_conv_b"][layer] = gpf["nf"][0], gpf["fcw"], gpf["fcb"][0]
        dep, hook = emit_grads(2 * layer + 1, gwf, dh[0])
        if layer % 2 == 0:
            dh, gwm, gp = even_bwd(tag, dh, sv, wm, pm, dep, hook)
            hp = _lanes_to_heads(gp["hp"])
            gs["norm_mix"][layer] = gp["nm"][0]
            gs["ev_gm_ln_g"][j], gs["ev_gm_ln_b"][j] = gp["lng"].reshape(GM_GROUPS, -1), gp["lnb"].reshape(GM_GROUPS, -1)
            gs["ev_gm_ws"][j], gs["ev_gm_bs"][j] = gp["ws"], gp["bst"].T
            gs["ev_conv_w"][j], gs["ev_conv_b"][j] = gp["cw"], gp["cb"][0]
            gs["ev_dt_bias"][j], gs["ev_a_log"][j], gs["ev_d_skip"][j] = hp[0], hp[1], hp[2]
            gs["ev_ssm_norm_w"][j] = gp["nw"][0]
        else:
            dh, gwm, gp = odd_bwd(tag, dh, sv, wm, pm, cos, sin, dep, hook)
            gs["norm_mix"][layer] = gp["nm"][0]
            gs["od_q_norm"][j], gs["od_kv_norm"][j] = gp["qn"][0], gp["kvn"][0]
        if layer > 0:
            dep, hook = emit_grads(2 * layer, gwm, dh[0])
    gs = {k: (v if k == "norm_final" else jnp.stack(v)) for k, v in gs.items()}
    return loss_tile[0, 0], dh[0], gs, gwm


def kernel(x, positions, norm_mix, norm_ffn, norm_final, ev_w_in, ev_gm_ln_g, ev_gm_ln_b, ev_gm_ws, ev_gm_bs, ev_conv_w, ev_conv_b, ev_dt_bias, ev_a_log, ev_d_skip, ev_ssm_norm_w, ev_w_out, od_w_in, od_q_norm, od_kv_norm, od_w_uq, od_w_ukv, od_w_o, ff_w_up, ff_conv_w, ff_conv_b, ff_w_down, loss_target, m_norm_mix, m_norm_ffn, m_norm_final, m_ev_w_in, m_ev_gm_ln_g, m_ev_gm_ln_b, m_ev_gm_ws, m_ev_gm_bs, m_ev_conv_w, m_ev_conv_b, m_ev_dt_bias, m_ev_a_log, m_ev_d_skip, m_ev_ssm_norm_w, m_ev_w_out, m_od_w_in, m_od_q_norm, m_od_kv_norm, m_od_w_uq, m_od_w_ukv, m_od_w_o, m_ff_w_up, m_ff_conv_w, m_ff_conv_b, m_ff_w_down, v_norm_mix, v_norm_ffn, v_norm_final, v_ev_w_in, v_ev_gm_ln_g, v_ev_gm_ln_b, v_ev_gm_ws, v_ev_gm_bs, v_ev_conv_w, v_ev_conv_b, v_ev_dt_bias, v_ev_a_log, v_ev_d_skip, v_ev_ssm_norm_w, v_ev_w_out, v_od_w_in, v_od_q_norm, v_od_kv_norm, v_od_w_uq, v_od_w_ukv, v_od_w_o, v_ff_w_up, v_ff_conv_w, v_ff_conv_b, v_ff_w_down):
    args = dict(locals())
    wts = {n: args[n] for n in WEIGHT_ORDER}
    mom = {n: args["m_" + n] for n in WEIGHT_ORDER}
    var = {n: args["v_" + n] for n in WEIGHT_ORDER}
    dev = _device_slot()

    small_names = list(SMALL_SHARDED)
    small_shapes = [wts[n].shape for n in small_names]
    (small_all,) = all_gather("ag_small", [_pack([wts[n] for n in small_names])])
    small_full = _unpack(small_all, small_shapes, lead=(N_DEV,))
    sm = {n: _full_from_shards(n, g) for n, g in zip(small_names, small_full)}
    sm.update({n: wts[n] for n in SMALL_REPLICATED})

    def stage_matrices(stage):
        layer, is_ffn = divmod(stage, 2)
        if is_ffn:
            return [("ff_w_up", layer), ("ff_w_down", layer)]
        return [(n, layer // 2) for n in (["ev_w_in", "ev_w_out"] if layer % 2 == 0 else ["od_w_in", "od_w_uq", "od_w_ukv", "od_w_o"])]

    def stage_fns(stage):
        layer, is_ffn = divmod(stage, 2)
        if is_ffn:
            return prep_ffn, unprep_ffn
        return (prep_even, unprep_even) if layer % 2 == 0 else (prep_odd, unprep_odd)

    n_stages, ahead = 8, 2

    bf = {n: wts[n].astype(BF16) for n in MATRICES}

    def start_gather(stage, earlier=None):
        shards = [bf[n][i] for n, i in stage_matrices(stage)]
        if earlier is not None:
            shards, _ = lax.optimization_barrier((shards, earlier))
        return split_start(f"ag_s{stage}_start", shards, [(N_DEV,) + s.shape for s in shards],
                           gather_plan(len(shards)), 4 * len(shards))

    gathers = {}
    for stage in range(ahead):
        gathers[stage] = start_gather(stage, gathers[stage - 1][4] if stage else None)

    def fetch_weights(stage, h):
        plan = gather_plan(len(stage_matrices(stage)))
        shards, landed = split_wait(f"ag_s{stage}_wait", gathers.pop(stage), plan, h)
        g = gather_finish(f"ag_s{stage}_finish", landed)
        g = [lax.dynamic_update_index_in_dim(gk, sk, dev, 0) for gk, sk in zip(g, shards)]
        started = None
        if stage + ahead < n_stages:
            gathers[stage + ahead] = start_gather(stage + ahead, g[0])
            started = gathers[stage + ahead][4]
        return stage_fns(stage)[0](*g), started

    scatters = []
    out = {n: None for n in MATRICES}

    held = []
    hold_below = 6

    def update(mats, sums, recv, after=None):
        for (n, i), p_, r_ in zip(mats, sums, recv):
            layers, rows, cols = MATRICES[n]
            two_d = lambda a: a.reshape(layers * rows, cols)
            out[n] = adamw(f"adamw_{n}_{i}", [(p_, 0), (r_, 0), (r_, 1), (r_, 2)], two_d(wts[n]), two_d(mom[n]),
                           two_d(var[n]), tr=_row_tile(rows), part=i, prev=out[n], after=after)
            if after is not None:
                after = out[n][0]
        return after

    def finish_scatter(after):
        stage, handle = scatters.pop(0)
        mats = stage_matrices(stage)
        sums, recv = split_wait(f"rs_s{stage}_wait", handle, chips_plan(len(mats)), after)
        if 0 < stage < hold_below:
            held.append((mats, sums, recv))
        else:
            update(mats, sums, recv)

    def start_chips(stage, send, from_sibling):
        sums = [chip_sums(f"rs_s{stage}_add{k}", g, r, tr=_row_tile(g.shape[1], 512)) for k, (g, r) in enumerate(zip(send, from_sibling))]
        handle = split_start(f"rs_s{stage}_start", sums, [(3,) + s.shape[1:] for s in sums], chips_plan(len(sums)), 3 * len(sums))
        scatters.append((stage, handle))
        return handle[4]

    def emit_grads(stage, gw, dh):
        if len(scatters) >= ahead:
            finish_scatter(dh)
        send = list(stage_fns(stage)[1](gw))
        plan = sibling_plan(len(send))
        handle = split_start(f"rs_s{stage}_sib_start", send, [(4,) + g.shape[1:] for g in send], plan, 4 * len(send))

        def hook(first_result):
            sent, from_sibling = split_wait(f"rs_s{stage}_sib_wait", handle, plan, first_result)
            return start_chips(stage, sent, from_sibling)

        return handle[4], hook

    def emit_last(stage, gw, after):
        finish_scatter(after)
        send = list(stage_fns(stage)[1](gw))
        return start_chips(stage, send, exchange_sibling(f"rs_s{stage}_sibling", send, after))

    loss_local, dx, gs, gw_first = local_step(x[0], positions[0], loss_target[0], sm, fetch_weights, emit_grads)
    loss = lax.psum(loss_local, ("x", "y", "c"))

    all_small = small_names + SMALL_REPLICATED
    (partials,) = all_gather("ar_small", [_pack([gs[n] for n in all_small])])
    last_started = emit_last(0, gw_first, partials)
    total = sum_parts("ar_small_sum", [(partials, s) for s in range(N_DEV)])
    g_full = dict(zip(all_small, _unpack(total, [gs[n].shape for n in all_small])))
    g_mine = {n: (_my_shard(n, g_full[n], dev) if n in SMALL_SHARDED else g_full[n]) for n in all_small}
    packed = [_pack([d[n] for n in all_small]) for d in (g_mine, wts, mom, var)]
    res = adamw("adamw_small", [(packed[0][None], 0)], packed[1], packed[2], packed[3], after=last_started)
    unpacked = [_unpack(a, [wts[n].shape for n in all_small]) for a in res]
    for i, n in enumerate(all_small):
        out[n] = [u[i] for u in unpacked]
    while len(scatters) > 1:
        finish_scatter(res[0])
    follow = res[0]
    for job in held:
        follow = update(*job, after=follow)
    finish_scatter(follow)
    for n in MATRICES:
        out[n] = [a.reshape(wts[n].shape) for a in out[n]]

    return (loss, dx[None], *[out[n][0] for n in WEIGHT_ORDER], *[out[n][1] for n in WEIGHT_ORDER],
            *[out[n][2] for n in WEIGHT_ORDER], *[out[n][3] for n in WEIGHT_ORDER])
```

```python
import functools
import math

import jax
import jax.numpy as jnp
from jax import lax
from jax.experimental import pallas as pl
from jax.experimental.pallas import tpu as pltpu

F32 = jnp.float32
BF16 = jnp.bfloat16
MESH = pl.DeviceIdType.MESH

V7X_VMEM_LIMIT_BYTES = 56 * 1024 * 1024
LANES = 128

EPS = 1e-6
D_MODEL = 2048
CHUNK = 64
GM_BLOCK = 128
GM_GROUPS = 8
GM_GROUP_DIM = D_MODEL // GM_GROUPS
SSM_HEADS = 32
SSM_HEAD_DIM = 64
SSM_GROUPS = 4
SSM_STATE = 128
SSM_CONV = 4
SSM_BC = SSM_GROUPS * SSM_STATE
SSM_CONV_DIM = D_MODEL + 2 * SSM_BC
SSD_HEADS_PER_STEP = 4
SSD_STEPS = SSM_HEADS // SSD_HEADS_PER_STEP
SSD_X_WIDTH = SSD_HEADS_PER_STEP * SSM_HEAD_DIM
MLA_HEADS = 16
MLA_RANK = 512
MLA_NOPE = 128
MLA_ROPE = 64
MLA_V = 128
MLA_QK = MLA_NOPE + MLA_ROPE
MLA_QPAD = 2 * LANES
ODD_IN = 2 * MLA_RANK + MLA_ROPE
ODD_IN_PAD = 2 * MLA_RANK + LANES
D_FF = 5632
ROPE_THETA = 10000.0
N_DEV = 8

ADAM_LR, ADAM_B1, ADAM_B2, ADAM_EPS, ADAM_WD, ADAM_STEP = 0.001, 0.9, 0.999, 1e-08, 0.01, 10


def _params(*sem):
    return pltpu.CompilerParams(dimension_semantics=sem, vmem_limit_bytes=V7X_VMEM_LIMIT_BYTES)


def _pick(dim, target):
    if dim <= target:
        return dim
    t = (target // LANES) * LANES
    while t >= LANES:
        if dim % t == 0:
            return t
        t -= LANES
    raise ValueError(f"no tile for {dim} under {target}")


MATMUL_VMEM_BUDGET = 32 * 1024 * 1024


def _matmul_tiles(m, n, k, a_bytes, b_bytes, o_bytes, has_res, ta):
    def fits(tm, tn):
        per_out = o_bytes + (4 if has_res else 0)
        return 2 * (tm * k * a_bytes + tn * k * b_bytes + tm * tn * per_out) <= MATMUL_VMEM_BUDGET

    tns = (2048, 1024, 512, 256, 128) if ta else (512, 256, 128)
    tms = (512, 256, 128) if ta else (2048, 1024, 512, 256, 128)
    for tn in tns:
        tn = _pick(n, tn)
        for tm in tms:
            tm = _pick(m, tm)
            if fits(tm, tn):
                return tm, tn
    raise ValueError(f"no matmul tiles for {m}x{n}x{k}")


def matmul(name, a, b, *, ta=False, tb=False, res=None, out_dtype=F32, after=None):
    m, k = (a.shape[1], a.shape[0]) if ta else a.shape
    n = b.shape[0] if tb else b.shape[1]
    assert k == (b.shape[1] if tb else b.shape[0]), (name, a.shape, b.shape)
    tm, tn = _matmul_tiles(m, n, k, a.dtype.itemsize, b.dtype.itemsize, jnp.dtype(out_dtype).itemsize, res is not None, ta)
    dims = (((0 if ta else 1,), (1 if tb else 0,)), ((), ()))

    def body(*refs):
        a_ref, b_ref, o_ref = refs[0], refs[1], refs[-1]
        total = lax.dot_general(a_ref[...].astype(BF16), b_ref[...].astype(BF16), dims, preferred_element_type=F32)
        if res is not None:
            total = total + refs[2][...]
        o_ref[...] = total.astype(o_ref.dtype)

    a_spec = pl.BlockSpec((k, tm), lambda i, j: (0, i)) if ta else pl.BlockSpec((tm, k), lambda i, j: (i, 0))
    b_spec = pl.BlockSpec((tn, k), lambda i, j: (j, 0)) if tb else pl.BlockSpec((k, tn), lambda i, j: (0, j))
    o_spec = pl.BlockSpec((tm, tn), lambda i, j: (i, j))
    ins, specs = [a, b], [a_spec, b_spec]
    if res is not None:
        ins.append(res)
        specs.append(o_spec)
    if after is not None:
        ins.append(after)
        specs.append(pl.BlockSpec(memory_space=pl.ANY))
    return pl.pallas_call(
        body, name=name, grid=(m // tm, n // tn), in_specs=specs, out_specs=o_spec,
        out_shape=jax.ShapeDtypeStruct((m, n), out_dtype),
        compiler_params=_params("parallel", "parallel"),
    )(*ins)


def matmul_tn_slots(name, a, bs, width, tm=512):
    k, m = a.shape
    counts = [b.shape[1] // width for b in bs]
    firsts = [sum(counts[:i]) for i in range(len(bs))]
    tm = _pick(m, tm)
    tn_dims = (((0,), (0,)), ((), ()))

    def body(*refs):
        a_ref, o_ref = refs[0], refs[-1]
        j = pl.program_id(1)
        for b_ref, first, count in zip(refs[1:-1], firsts, counts):
            @pl.when(jnp.logical_and(j >= first, j < first + count))
            def _(b_ref=b_ref):
                o_ref[0] = lax.dot_general(a_ref[...].astype(BF16), b_ref[...].astype(BF16), tn_dims,
                                           preferred_element_type=F32).astype(o_ref.dtype)

    specs = [pl.BlockSpec((k, tm), lambda i, j: (0, i))]
    specs += [pl.BlockSpec((k, width), lambda i, j, first=first, count=count: (0, jnp.clip(j - first, 0, count - 1)))
              for first, count in zip(firsts, counts)]
    return pl.pallas_call(
        body, name=name, grid=(m // tm, sum(counts)), in_specs=specs,
        out_specs=pl.BlockSpec((1, tm, width), lambda i, j: (j, i, 0)),
        out_shape=jax.ShapeDtypeStruct((sum(counts), m, width), BF16),
        compiler_params=_params("parallel", "arbitrary"),
    )(a, *bs)


@functools.partial(jax.custom_vjp, nondiff_argnums=(1, 2))
def _roll(x, shift, axis):
    return pltpu.roll(x, shift, axis)


def _roll_fwd(x, shift, axis):
    return pltpu.roll(x, shift, axis), None


def _roll_bwd(shift, axis, _, g):
    return (pltpu.roll(g, (g.shape[axis] - shift) % g.shape[axis], axis),)


_roll.defvjp(_roll_fwd, _roll_bwd)


def _shift_down(x, s):
    rows = lax.broadcasted_iota(jnp.int32, x.shape, 0)
    return jnp.where(rows >= s, _roll(x, s, 0), 0.0)


def _dwconv(x, w, b):
    taps = w.shape[0]
    y = b + w[taps - 1:taps, :] * x
    for kk in range(taps - 1):
        y = y + w[kk:kk + 1, :] * _shift_down(x, taps - 1 - kk)
    return y


def _rms(x, w):
    return x * lax.rsqrt(jnp.mean(x * x, -1, keepdims=True) + EPS) * w


def _rms_f(h, w):
    return (_rms(h, w),)


def _gmlp_f(uv, lng, lnb, ws, bst):
    r = lax.broadcasted_iota(jnp.int32, (GM_BLOCK, GM_BLOCK), 0) // CHUNK
    c = lax.broadcasted_iota(jnp.int32, (GM_BLOCK, GM_BLOCK), 1) // CHUNK
    outs = []
    for g in range(GM_GROUPS):
        lo, hi = g * GM_GROUP_DIM, (g + 1) * GM_GROUP_DIM
        gu = jax.nn.gelu(uv[:, lo:hi])
        gv = jax.nn.gelu(uv[:, D_MODEL + lo:D_MODEL + hi])
        xc = gv - jnp.mean(gv, -1, keepdims=True)
        var = jnp.mean(xc * xc, -1, keepdims=True)
        vn = xc * lax.rsqrt(var + EPS) * lng[:, lo:hi] + lnb[:, lo:hi]
        wm = jnp.where(r >= c, ws[g], 0.0).astype(BF16)
        gate = jnp.dot(wm, vn.astype(BF16), preferred_element_type=F32) + bst[:, g:g + 1]
        outs.append(gu * gate)
    return (jnp.concatenate(outs, axis=1),)


def _conv_silu_f(x, w, b):
    return (jax.nn.silu(_dwconv(x, w, b)),)


def _ffn_act_f(g, val, w, b):
    return (jax.nn.gelu(_dwconv(g, w, b)) * val,)


def _gate_norm_f(y, z, nw):
    y2 = y * jax.nn.silu(z)
    width = D_MODEL // SSM_GROUPS
    outs = []
    for g in range(SSM_GROUPS):
        blk = y2[:, g * width:(g + 1) * width]
        outs.append(blk * lax.rsqrt(jnp.mean(blk * blk, -1, keepdims=True) + EPS))
    return (jnp.concatenate(outs, axis=1) * nw,)


def _rope(x, cos, sin):
    lane = lax.broadcasted_iota(jnp.int32, x.shape, 1)
    half = MLA_ROPE // 2
    swapped = jnp.where(lane < half, _roll(x, LANES - half, 1), _roll(x, half, 1))
    return x * cos + swapped * sin


def _qkv_norm_f(proj, cos, sin, qn, kvn):
    cq = _rms(proj[:, :MLA_RANK], qn)
    ckv = _rms(proj[:, MLA_RANK:2 * MLA_RANK], kvn)
    kpe = _rope(proj[:, 2 * MLA_RANK:], cos, sin)
    return cq, ckv, kpe


def _attn_scores(q0, k0, qh, kn, kpe, cos, sin):
    qn = qh[:, :MLA_NOPE]
    qp = _rope(qh[:, MLA_NOPE:], cos, sin)
    nt = (((1,), (1,)), ((), ()))
    s = lax.dot_general(qn.astype(BF16), kn.astype(BF16), nt, preferred_element_type=F32)
    s = s + lax.dot_general(qp.astype(BF16), kpe.astype(BF16), nt, preferred_element_type=F32)
    s = s * (MLA_QK ** -0.5)
    visible_below = ((q0 + lax.broadcasted_iota(jnp.int32, (s.shape[0], 1), 0)) // CHUNK + 1) * CHUNK - k0
    return jnp.where(lax.broadcasted_iota(jnp.int32, s.shape, 1) < visible_below, s, -jnp.inf)


def _ssd_chunk_f(x, bm, cm, pdt, hp, sprev):
    nh, hd = SSD_HEADS_PER_STEP, SSM_HEAD_DIM
    dt = jax.nn.softplus(pdt + hp[0:1, :])
    cs = dt * (-jnp.exp(hp[1:2, :]))
    shift = 1
    while shift < CHUNK:
        cs = cs + _shift_down(cs, shift)
        shift *= 2
    cst = cs.T
    tot = cs[CHUNK - 1:CHUNK, :]

    def lanes(vals):
        return jnp.concatenate([jnp.broadcast_to(vals[:, e:e + 1], (vals.shape[0], hd)) for e in range(nh)], axis=1)

    r = lax.broadcasted_iota(jnp.int32, (CHUNK, CHUNK), 0)
    c = lax.broadcasted_iota(jnp.int32, (CHUNK, CHUNK), 1)
    tril = r >= c
    nt = (((1,), (1,)), ((), ()))
    tn = (((0,), (0,)), ((), ()))
    xd = x * lanes(dt)
    cb = lax.dot_general(cm.astype(BF16), bm.astype(BF16), nt, preferred_element_type=F32)
    ys = []
    for e in range(nh):
        decay = jnp.exp(jnp.where(tril, cs[:, e:e + 1] - cst[e:e + 1, :], -jnp.inf))
        ys.append(jnp.dot((cb * decay).astype(BF16), xd[:, e * hd:(e + 1) * hd].astype(BF16), preferred_element_type=F32))
    st = lax.dot_general((xd * lanes(jnp.exp(tot - cs))).astype(BF16), bm.astype(BF16), tn, preferred_element_type=F32)
    yoff = lax.dot_general(cm.astype(BF16), sprev.astype(BF16), nt, preferred_element_type=F32)
    y = jnp.concatenate(ys, axis=1) + yoff * lanes(jnp.exp(cs)) + lanes(hp[2:3, :]) * x
    carry = jnp.concatenate([jnp.broadcast_to(jnp.exp(tot[:, e:e + 1]), (hd, 1)) for e in range(nh)], axis=0)
    return y, carry * sprev + st


def _full_spec(a):
    nd = a.ndim
    return pl.BlockSpec(a.shape, lambda i, nd=nd: (0,) * nd)


def rows_fwd(name, f, rows, params, outs, tr, after=None):
    t = rows[0][0].shape[0]
    nr, npar = len(rows), len(params)
    extra = [] if after is None else [after]

    def body(*refs):
        vals = f(*[x[...].astype(F32) for x in refs[:nr + npar]])
        for o_ref, val in zip(refs[nr + npar + len(extra):], vals):
            o_ref[...] = val.astype(o_ref.dtype)

    in_specs = [pl.BlockSpec((tr, w), lambda i, cb=cb: (i, cb)) for _, w, cb in rows] + [_full_spec(p) for p in params]
    in_specs += [pl.BlockSpec(memory_space=pl.ANY)] * len(extra)
    out = pl.pallas_call(
        body, name=name, grid=(t // tr,), in_specs=in_specs,
        out_specs=[pl.BlockSpec((tr, w), lambda i: (i, 0)) for w, _ in outs],
        out_shape=[jax.ShapeDtypeStruct((t, w), dt) for w, dt in outs],
        compiler_params=_params("parallel"),
    )(*[a for a, _, _ in rows], *params, *extra)
    return out


def rows_bwd(name, f, rows, params, cots, tr, d_dtypes, n_nondiff=0, add=None, after=None):
    t = rows[0][0].shape[0]
    nr, npar, nc = len(rows), len(params), len(cots)
    nd = nr - n_nondiff
    has_add = add is not None
    copies = [(j, dt) for j in range(nd) for dt in (d_dtypes[j] if isinstance(d_dtypes[j], tuple) else (d_dtypes[j],))]
    ncp = len(copies)

    def body(*refs):
        i = pl.program_id(0)
        row_vals = [x[...].astype(F32) for x in refs[:nr]]
        par_vals = [x[...].astype(F32) for x in refs[nr:nr + npar]]
        cot_refs = refs[nr + npar:nr + npar + nc]
        pos = nr + npar + nc
        add_ref = refs[pos] if has_add else None
        pos += int(has_add) + int(after is not None)
        drow_refs = refs[pos:pos + ncp]
        dpar_refs = refs[pos + ncp:]

        def g(*diff):
            return f(*diff[:nd], *row_vals[nd:], *diff[nd:])

        _, vjp = jax.vjp(g, *row_vals[:nd], *par_vals)
        grads = vjp(tuple(cr[...].astype(F32) for cr in cot_refs))
        for (j, _), d_ref in zip(copies, drow_refs):
            val = grads[j]
            if j == 0 and has_add:
                val = val + add_ref[...]
            d_ref[...] = val.astype(d_ref.dtype)
        for j, d_ref in enumerate(dpar_refs):
            @pl.when(i == 0)
            def _(d_ref=d_ref, j=j):
                d_ref[...] = grads[nd + j]

            @pl.when(i > 0)
            def _(d_ref=d_ref, j=j):
                d_ref[...] += grads[nd + j]

    in_specs = [pl.BlockSpec((tr, w), lambda i, cb=cb: (i, cb)) for _, w, cb in rows] + [_full_spec(p) for p in params]
    in_specs += [pl.BlockSpec((tr, w), lambda i, cb=cb: (i, cb)) for _, w, cb in cots]
    ins = [a for a, _, _ in rows] + list(params) + [a for a, _, _ in cots]
    if has_add:
        in_specs.append(pl.BlockSpec((tr, rows[0][1]), lambda i: (i, 0)))
        ins.append(add)
    if after is not None:
        in_specs.append(pl.BlockSpec(memory_space=pl.ANY))
        ins.append(after)
    out_specs = [pl.BlockSpec((tr, rows[j][1]), lambda i: (i, 0)) for j, _ in copies] + [_full_spec(p) for p in params]
    out_shape = [jax.ShapeDtypeStruct((t, rows[j][1]), dt) for j, dt in copies]
    out_shape += [jax.ShapeDtypeStruct(p.shape, F32) for p in params]
    out = pl.pallas_call(
        body, name=name, grid=(t // tr,), in_specs=in_specs, out_specs=out_specs, out_shape=out_shape,
        compiler_params=_params("arbitrary"),
    )(*ins)
    return out[:ncp], out[ncp:]


def cols_fwd(name, f, cols, cparams, out_dtype, tc):
    t = cols[0][0].shape[0]
    width = cparams[0].shape[1]
    ncol = len(cols)

    def body(*refs):
        (val,) = f(*[x[...].astype(F32) for x in refs[:-1]])
        refs[-1][...] = val.astype(refs[-1].dtype)

    in_specs = [pl.BlockSpec((t, tc), lambda j, o=o: (0, o + j)) for _, o in cols]
    in_specs += [pl.BlockSpec((p.shape[0], tc), lambda j: (0, j)) for p in cparams]
    return pl.pallas_call(
        body, name=name, grid=(width // tc,), in_specs=in_specs,
        out_specs=pl.BlockSpec((t, tc), lambda j: (0, j)),
        out_shape=jax.ShapeDtypeStruct((t, width), out_dtype),
        compiler_params=_params("parallel"),
    )(*[a for a, _ in cols], *cparams)


def cols_bwd(name, f, cols, cparams, cot, tc, d_dtype, after=None):
    t = cols[0][0].shape[0]
    width = cparams[0].shape[1]
    ncol, npar = len(cols), len(cparams)
    extra = [] if after is None else [after]

    def body(*refs):
        vals = [x[...].astype(F32) for x in refs[:ncol + npar]]
        _, vjp = jax.vjp(f, *vals)
        grads = vjp((refs[ncol + npar][...].astype(F32),))
        for d_ref, gval in zip(refs[ncol + npar + 1 + len(extra):], grads):
            d_ref[...] = gval.astype(d_ref.dtype)

    in_specs = [pl.BlockSpec((t, tc), lambda j, o=o: (0, o + j)) for _, o in cols]
    in_specs += [pl.BlockSpec((p.shape[0], tc), lambda j: (0, j)) for p in cparams]
    in_specs.append(pl.BlockSpec((t, tc), lambda j: (0, j)))
    in_specs += [pl.BlockSpec(memory_space=pl.ANY)] * len(extra)
    out_specs = [pl.BlockSpec((t, tc), lambda j: (0, j)) for _ in cols]
    out_specs += [pl.BlockSpec((p.shape[0], tc), lambda j: (0, j)) for p in cparams]
    out_shape = [jax.ShapeDtypeStruct((t, width), d_dtype) for _ in cols]
    out_shape += [jax.ShapeDtypeStruct(p.shape, F32) for p in cparams]
    out = pl.pallas_call(
        body, name=name, grid=(width // tc,), in_specs=in_specs, out_specs=out_specs, out_shape=out_shape,
        compiler_params=_params("parallel"),
    )(*[a for a, _ in cols], *cparams, cot, *extra)
    return out[:ncol], out[ncol:]


def _ssd_in_specs(t):
    heads_per_group = SSM_HEADS // SSM_GROUPS
    steps_per_group = heads_per_group // SSD_HEADS_PER_STEP
    b_blk = D_MODEL // LANES
    c_blk = (D_MODEL + SSM_BC) // LANES
    return [
        pl.BlockSpec((t, SSD_X_WIDTH), lambda s: (0, s)),
        pl.BlockSpec((t, LANES), lambda s: (0, b_blk + s // steps_per_group)),
        pl.BlockSpec((t, LANES), lambda s: (0, c_blk + s // steps_per_group)),
        pl.BlockSpec((t, LANES), lambda s: (0, s)),
        pl.BlockSpec((3, LANES), lambda s: (0, s)),
    ]


def ssd_fwd(name, xa, pdt, hp):
    t = xa.shape[0]
    nc = t // CHUNK
    nh = SSD_HEADS_PER_STEP

    def body(x_ref, b_ref, c_ref, pdt_ref, hp_ref, y_ref, st_ref, s_scr):
        s_scr[...] = jnp.zeros_like(s_scr)

        def step(ci, carry):
            sl = pl.ds(pl.multiple_of(ci * CHUNK, CHUNK), CHUNK)
            sprev = s_scr[...]
            st_ref[0, ci] = sprev
            y, snew = _ssd_chunk_f(x_ref[sl, :], b_ref[sl, :], c_ref[sl, :], pdt_ref[sl, :], hp_ref[...], sprev)
            y_ref[sl, :] = y
            s_scr[...] = snew
            return carry

        lax.fori_loop(0, nc, step, 0)

    return pl.pallas_call(
        body, name=name, grid=(SSD_STEPS,), in_specs=_ssd_in_specs(t),
        out_specs=[pl.BlockSpec((t, SSD_X_WIDTH), lambda s: (0, s)),
                   pl.BlockSpec((1, nc, SSD_X_WIDTH, SSM_STATE), lambda s: (s, 0, 0, 0))],
        out_shape=[jax.ShapeDtypeStruct((t, D_MODEL), F32),
                   jax.ShapeDtypeStruct((SSD_STEPS, nc, SSD_X_WIDTH, SSM_STATE), F32)],
        scratch_shapes=[pltpu.VMEM((SSD_X_WIDTH, SSM_STATE), F32)],
        compiler_params=_params("parallel"),
    )(xa, xa, xa, pdt, hp)


def ssd_bwd(name, xa, pdt, hp, states, dy):
    t = xa.shape[0]
    nc = t // CHUNK
    nh = SSD_HEADS_PER_STEP
    steps_per_group = SSM_HEADS // SSM_GROUPS // nh

    def body(x_ref, b_ref, c_ref, pdt_ref, hp_ref, st_ref, dy_ref, dx_ref, db_ref, dc_ref, dpdt_ref, dhp_ref, ds_scr, dhp_scr):
        first = pl.program_id(0) % steps_per_group == 0
        ds_scr[...] = jnp.zeros_like(ds_scr)
        dhp_scr[...] = jnp.zeros_like(dhp_scr)

        def step(i, carry):
            ci = nc - 1 - i
            sl = pl.ds(pl.multiple_of(ci * CHUNK, CHUNK), CHUNK)
            _, vjp = jax.vjp(_ssd_chunk_f, x_ref[sl, :], b_ref[sl, :], c_ref[sl, :], pdt_ref[sl, :], hp_ref[...], st_ref[0, ci])
            dx, db, dc, dpdt, dhp, dsprev = vjp((dy_ref[sl, :], ds_scr[...]))
            dx_ref[sl, :] = dx
            dpdt_ref[sl, :] = dpdt.astype(dpdt_ref.dtype)

            @pl.when(first)
            def _():
                db_ref[sl, :] = db
                dc_ref[sl, :] = dc

            @pl.when(jnp.logical_not(first))
            def _():
                db_ref[sl, :] += db
                dc_ref[sl, :] += dc

            ds_scr[...] = dsprev
            dhp_scr[...] += dhp
            return carry

        lax.fori_loop(0, nc, step, 0)
        dhp_ref[...] = dhp_scr[...]

    in_specs = _ssd_in_specs(t) + [
        pl.BlockSpec((1, nc, SSD_X_WIDTH, SSM_STATE), lambda s: (s, 0, 0, 0)),
        pl.BlockSpec((t, SSD_X_WIDTH), lambda s: (0, s)),
    ]
    out_specs = [
        pl.BlockSpec((t, SSD_X_WIDTH), lambda s: (0, s)),
        pl.BlockSpec((t, LANES), lambda s: (0, s // steps_per_group)),
        pl.BlockSpec((t, LANES), lambda s: (0, s // steps_per_group)),
        pl.BlockSpec((t, LANES), lambda s: (0, s)),
        pl.BlockSpec((3, LANES), lambda s: (0, s)),
    ]
    out_shape = [
        jax.ShapeDtypeStruct((t, D_MODEL), F32),
        jax.ShapeDtypeStruct((t, SSM_BC), F32),
        jax.ShapeDtypeStruct((t, SSM_BC), F32),
        jax.ShapeDtypeStruct((t, SSD_STEPS * LANES), BF16),
        jax.ShapeDtypeStruct((3, SSD_STEPS * LANES), F32),
    ]
    return pl.pallas_call(
        body, name=name, grid=(SSD_STEPS,), in_specs=in_specs, out_specs=out_specs, out_shape=out_shape,
        scratch_shapes=[pltpu.VMEM((SSD_X_WIDTH, SSM_STATE), F32), pltpu.VMEM((3, LANES), F32)],
        compiler_params=_params("arbitrary"),
    )(xa, xa, xa, pdt, hp, states, dy)


ATTN_TQ = 256
ATTN_KSTEP = 512


def _attn_extents(t):
    return [min(t, (g + 1) * ATTN_KSTEP) for g in range(-(-t // ATTN_KSTEP))]


def _attn_f(q0, qh, kn, kpe, v, cos, sin):
    p = jax.nn.softmax(_attn_scores(q0, 0, qh, kn, kpe, cos, sin), axis=-1)
    return (jnp.dot(p.astype(BF16), v.astype(BF16), preferred_element_type=F32),)


def _attn_in_specs(t):
    return [
        pl.BlockSpec((ATTN_TQ, MLA_QPAD), lambda h, qi: (qi, h)),
        pl.BlockSpec((t, MLA_NOPE), lambda h, qi: (0, h)),
        pl.BlockSpec((t, LANES), lambda h, qi: (0, 0)),
        pl.BlockSpec((t, MLA_V), lambda h, qi: (0, h)),
        pl.BlockSpec((ATTN_TQ, LANES), lambda h, qi: (qi, 0)),
        pl.BlockSpec((ATTN_TQ, LANES), lambda h, qi: (qi, 0)),
    ]


def attn_fwd(name, q, kn, kpe, v, cos, sin):
    t = q.shape[0]

    def body(q_ref, kn_ref, kpe_ref, v_ref, cos_ref, sin_ref, o_ref):
        qi = pl.program_id(1)
        for span, ext in enumerate(_attn_extents(t)):
            @pl.when(qi // (ATTN_KSTEP // ATTN_TQ) == span)
            def _(ext=ext):
                (o,) = _attn_f(qi * ATTN_TQ, q_ref[...], kn_ref[0:ext, :], kpe_ref[0:ext, :], v_ref[0:ext, :],
                               cos_ref[...], sin_ref[...])
                o_ref[...] = o.astype(o_ref.dtype)

    return pl.pallas_call(
        body, name=name, grid=(MLA_HEADS, t // ATTN_TQ), in_specs=_attn_in_specs(t),
        out_specs=pl.BlockSpec((ATTN_TQ, MLA_V), lambda h, qi: (qi, h)),
        out_shape=jax.ShapeDtypeStruct((t, MLA_HEADS * MLA_V), BF16),
        compiler_params=_params("parallel", "parallel"),
    )(q, kn, kpe, v, cos, sin)


def attn_bwd(name, q, kn, kpe, v, cos, sin, do, after=None):
    t = q.shape[0]
    extra = [] if after is None else [after]

    def body(q_ref, kn_ref, kpe_ref, v_ref, cos_ref, sin_ref, do_ref, *rest):
        dq_ref, dkn_ref, dkpe_ref, dv_ref = rest[len(extra):]
        h, qi = pl.program_id(0), pl.program_id(1)
        q0 = qi * ATTN_TQ
        cos, sin = cos_ref[...], sin_ref[...]

        @pl.when(qi == 0)
        def _():
            dkn_ref[...] = jnp.zeros_like(dkn_ref)
            dv_ref[...] = jnp.zeros_like(dv_ref)

        @pl.when(jnp.logical_and(h == 0, qi == 0))
        def _():
            dkpe_ref[...] = jnp.zeros_like(dkpe_ref)

        for span, ext in enumerate(_attn_extents(t)):
            @pl.when(qi // (ATTN_KSTEP // ATTN_TQ) == span)
            def _(ext=ext):
                def g(qh, knv, kpev, vv):
                    return _attn_f(q0, qh, knv, kpev, vv, cos, sin)

                _, vjp = jax.vjp(g, q_ref[...].astype(F32), kn_ref[0:ext, :].astype(F32), kpe_ref[0:ext, :].astype(F32),
                                 v_ref[0:ext, :].astype(F32))
                dq, dkn, dkpe, dv = vjp((do_ref[...].astype(F32),))
                dq_ref[...] = dq.astype(dq_ref.dtype)
                dkn_ref[0:ext, :] += dkn
                dkpe_ref[0:ext, :] += dkpe
                dv_ref[0:ext, :] += dv

    in_specs = _attn_in_specs(t) + [pl.BlockSpec((ATTN_TQ, MLA_V), lambda h, qi: (qi, h))]
    in_specs += [pl.BlockSpec(memory_space=pl.ANY)] * len(extra)
    out_specs = [
        pl.BlockSpec((ATTN_TQ, MLA_QPAD), lambda h, qi: (qi, h)),
        pl.BlockSpec((t, MLA_NOPE), lambda h, qi: (0, h)),
        pl.BlockSpec((t, LANES), lambda h, qi: (0, 0)),
        pl.BlockSpec((t, MLA_V), lambda h, qi: (0, h)),
    ]
    out_shape = [
        jax.ShapeDtypeStruct((t, MLA_HEADS * MLA_QPAD), BF16),
        jax.ShapeDtypeStruct((t, MLA_HEADS * MLA_NOPE), F32),
        jax.ShapeDtypeStruct((t, LANES), F32),
        jax.ShapeDtypeStruct((t, MLA_HEADS * MLA_V), F32),
    ]
    return pl.pallas_call(
        body, name=name, grid=(MLA_HEADS, t // ATTN_TQ), in_specs=in_specs, out_specs=out_specs, out_shape=out_shape,
        compiler_params=_params("arbitrary", "arbitrary"),
    )(q, kn, kpe, v, cos, sin, do, *extra)


def final_loss(name, h, nf, target, tr=256):
    t, d = h.shape

    def body(h_ref, w_ref, t_ref, loss_ref, dh_ref, dhb_ref, dw_ref):
        i = pl.program_id(0)
        tgt = t_ref[...]

        def f(hv, wv):
            err = _rms(hv, wv) - tgt
            return 0.5 * jnp.sum(jnp.mean(err * err, -1, keepdims=True), 0, keepdims=True)

        val, vjp = jax.vjp(f, h_ref[...], w_ref[...])
        dh, dw = vjp(jnp.ones((1, 1), F32))
        dh_ref[...] = dh
        dhb_ref[...] = dh.astype(dhb_ref.dtype)
        tile = jnp.broadcast_to(val, loss_ref.shape)

        @pl.when(i == 0)
        def _():
            loss_ref[...] = tile
            dw_ref[...] = dw

        @pl.when(i > 0)
        def _():
            loss_ref[...] += tile
            dw_ref[...] += dw

    row = pl.BlockSpec((tr, d), lambda i: (i, 0))
    return pl.pallas_call(
        body, name=name, grid=(t // tr,), in_specs=[row, _full_spec(nf), row],
        out_specs=[pl.BlockSpec((8, LANES), lambda i: (0, 0)), row, row, _full_spec(nf)],
        out_shape=[jax.ShapeDtypeStruct((8, LANES), F32), jax.ShapeDtypeStruct((t, d), F32), jax.ShapeDtypeStruct((t, d), BF16),
                   jax.ShapeDtypeStruct(nf.shape, F32)],
        compiler_params=_params("arbitrary"),
    )(h, nf, target)


ANY = pl.BlockSpec(memory_space=pl.ANY)
CHIP_ORDER = ((0, 0), (0, 1), (1, 0), (1, 1))


def _place():
    return lax.axis_index("x"), lax.axis_index("y"), lax.axis_index("c")


def _other_chips(x, y):
    return [(1 - x, y), (x, 1 - y), (1 - x, 1 - y)]


def _device_slot():
    x, y, c = _place()
    return 4 * x + 2 * y + c


def _row_tile(rows, cap=128):
    return next(t for t in (512, 256, 128, 64, 32, 16) if t <= cap and rows % t == 0)


def all_gather(name, shards):
    n = len(shards)

    def body(*refs):
        ins, outs = refs[:n], refs[n:2 * n]
        send_sems, recv_sems, local_sems = refs[2 * n:]
        x, y, c = _place()
        me, sibling = (x, y, c), (x, y, 1 - c)
        chips = _other_chips(x, y)

        def copy(k, j, block, to, from_input=False):
            dst = outs[k].at[4 * block[0] + 2 * block[1] + block[2]]
            return pltpu.make_async_remote_copy(
                src_ref=ins[k] if from_input else dst, dst_ref=dst,
                send_sem=send_sems.at[7 * k + j], recv_sem=recv_sems.at[7 * k + j],
                device_id=to, device_id_type=MESH)

        mine = [pltpu.make_async_copy(ins[k], outs[k].at[4 * x + 2 * y + c], local_sems.at[k]) for k in range(n)]
        for cp in mine:
            cp.start()
        first = []
        for k in range(n):
            first.append(copy(k, 0, me, sibling, True))
            first += [copy(k, 1 + j, me, (*chip, c), True) for j, chip in enumerate(chips)]
        for cp in first:
            cp.start()
        passed = []
        for j, chip in enumerate(chips):
            for k in range(n):
                copy(k, 1 + j, (*chip, c), me).wait_recv()
                fwd = copy(k, 4 + j, (*chip, c), sibling)
                fwd.start()
                passed.append(fwd)
        for k in range(n):
            copy(k, 0, sibling, me).wait_recv()
        for j, chip in enumerate(chips):
            for k in range(n):
                copy(k, 4 + j, (*chip, 1 - c), me).wait_recv()
        for cp in first + passed:
            cp.wait_send()
        for cp in mine:
            cp.wait()

    return pl.pallas_call(
        body, name=name, in_specs=[ANY] * n, out_specs=[ANY] * n,
        out_shape=[jax.ShapeDtypeStruct((N_DEV,) + s.shape, s.dtype) for s in shards],
        scratch_shapes=[pltpu.SemaphoreType.DMA((7 * n,)), pltpu.SemaphoreType.DMA((7 * n,)), pltpu.SemaphoreType.DMA((n,))],
    )(*shards)


def exchange_sibling(name, gs, after=None):
    n = len(gs)
    extra = [] if after is None else [after]

    def body(*refs):
        ins, outs = refs[:n], refs[n + len(extra):2 * n + len(extra)]
        send_sems, recv_sems = refs[2 * n + len(extra):]
        x, y, c = _place()
        copies = []
        for k in range(n):
            for q, (cx, cy) in enumerate(CHIP_ORDER):
                copies.append(pltpu.make_async_remote_copy(
                    src_ref=ins[k].at[4 * cx + 2 * cy + (1 - c)], dst_ref=outs[k].at[q],
                    send_sem=send_sems.at[4 * k + q], recv_sem=recv_sems.at[4 * k + q],
                    device_id=(x, y, 1 - c), device_id_type=MESH))
        for cp in copies:
            cp.start()
        for cp in copies:
            cp.wait()

    return pl.pallas_call(
        body, name=name, in_specs=[ANY] * (n + len(extra)), out_specs=[ANY] * n,
        out_shape=[jax.ShapeDtypeStruct((4,) + g.shape[1:], g.dtype) for g in gs],
        scratch_shapes=[pltpu.SemaphoreType.DMA((4 * n,)), pltpu.SemaphoreType.DMA((4 * n,))],
    )(*gs, *extra)


def chip_sums(name, g, recv, tr=128):
    _, r, c = g.shape

    def body(g_ref, r_ref, o_ref):
        o_ref[...] = (g_ref[...].astype(F32) + r_ref[...].astype(F32)).astype(o_ref.dtype)

    def chip(i):
        x, y, _ = _place()
        return jnp.where(i % 2 == 1, 1 - x, x), jnp.where(i >= 2, 1 - y, y)

    def g_index(i, j):
        cx, cy = chip(i)
        return 4 * cx + 2 * cy + lax.axis_index("c"), j, 0

    def recv_index(i, j):
        cx, cy = chip(i)
        return 2 * cx + cy, j, 0

    return pl.pallas_call(
        body, name=name, grid=(4, r // tr),
        in_specs=[pl.BlockSpec((1, tr, c), g_index), pl.BlockSpec((1, tr, c), recv_index)],
        out_specs=pl.BlockSpec((1, tr, c), lambda i, j: (i, j, 0)),
        out_shape=jax.ShapeDtypeStruct((4, r, c), g.dtype),
        compiler_params=_params("parallel", "parallel"),
    )(g, recv)


def sum_parts(name, parts, tr=128):
    _, r, c = parts[0][0].shape

    def body(*refs):
        total = refs[0][0].astype(F32)
        for ref in refs[1:-1]:
            total = total + ref[0].astype(F32)
        refs[-1][...] = total

    return pl.pallas_call(
        body, name=name, grid=(r // tr,),
        in_specs=[pl.BlockSpec((1, tr, c), lambda i, s=s: (s, i, 0)) for _, s in parts],
        out_specs=pl.BlockSpec((tr, c), lambda i: (i, 0)), out_shape=jax.ShapeDtypeStruct((r, c), F32),
        compiler_params=_params("parallel"),
    )(*[a for a, _ in parts])


def adamw(name, parts, w, m, v, tr=128, part=0, prev=None, after=None):
    _, r, c = parts[0][0].shape
    np_ = len(parts)
    first = part * (r // tr)

    def body(*refs):
        g = refs[0][0].astype(F32)
        for ref in refs[1:np_]:
            g = g + ref[0].astype(F32)
        w_ref, m_ref, v_ref = refs[np_:np_ + 3]
        g_out, d_out, m_out, v_out = refs[-4:]
        new_m = ADAM_B1 * m_ref[...] + (1.0 - ADAM_B1) * g
        new_v = ADAM_B2 * v_ref[...] + (1.0 - ADAM_B2) * (g * g)
        m_hat = new_m / (1.0 - ADAM_B1 ** ADAM_STEP)
        v_hat = new_v / (1.0 - ADAM_B2 ** ADAM_STEP)
        g_out[...] = g
        d_out[...] = -ADAM_LR * (m_hat / (jnp.sqrt(v_hat) + ADAM_EPS) + ADAM_WD * w_ref[...])
        m_out[...] = new_m
        v_out[...] = new_v

    tile = pl.BlockSpec((tr, c), lambda i: (first + i, 0))
    in_specs = [pl.BlockSpec((1, tr, c), lambda i, s=s: (s, i, 0)) for _, s in parts] + [tile] * 3
    ins = [a for a, _ in parts] + [w, m, v]
    aliases = {}
    if prev is not None:
        aliases = {len(ins) + k: k for k in range(4)}
        in_specs += [ANY] * 4
        ins += list(prev)
    if after is not None:
        in_specs.append(ANY)
        ins.append(after)
    return pl.pallas_call(
        body, name=name, grid=(r // tr,), in_specs=in_specs,
        out_specs=[tile] * 4, out_shape=[jax.ShapeDtypeStruct(w.shape, F32)] * 4,
        input_output_aliases=aliases, compiler_params=_params("parallel"),
    )(*ins)


HBM = pl.BlockSpec(memory_space=pltpu.HBM)
SEM = pl.BlockSpec(memory_space=pltpu.SEMAPHORE)
SIDE_EFFECT = pltpu.SideEffectType.DATAFLOW_SIDE_EFFECTING


def _split_copies(plan, src_refs, land_refs, send_sems, recv_sems):
    copies = []
    for i, (k, src_slot, land_slot, device) in enumerate(plan(*_place())):
        copies.append(pltpu.make_async_remote_copy(
            src_ref=src_refs[k] if src_slot is None else src_refs[k].at[src_slot], dst_ref=land_refs[k].at[land_slot],
            send_sem=send_sems.at[i], recv_sem=recv_sems.at[i], device_id=device, device_id_type=MESH))
    return copies


def split_start(name, srcs, land_shapes, plan, n_copies):
    n = len(srcs)
    lands = [] if land_shapes is None else [lax.empty(shape, s.dtype) for shape, s in zip(land_shapes, srcs)]
    m = len(lands)

    def body(*refs):
        src_refs = refs[:n]
        land_refs = refs[n:n + m] if m else src_refs
        send_sems, recv_sems, token = refs[n + m], refs[n + m + 1], refs[-1]
        for cp in _split_copies(plan, src_refs, land_refs, send_sems, recv_sems):
            cp.start()
        token[...] = jnp.zeros_like(token)

    ins = [pltpu.with_memory_space_constraint(a, pltpu.HBM) for a in list(srcs) + lands]
    out = pl.pallas_call(
        body, name=name,
        out_shape=(pltpu.SemaphoreType.DMA((n_copies,)), pltpu.SemaphoreType.DMA((n_copies,)),
                   *[pltpu.HBM(a.shape, a.dtype) for a in ins], jax.ShapeDtypeStruct((8, LANES), F32)),
        in_specs=[HBM] * (n + m), out_specs=(SEM, SEM, *[HBM] * (n + m), pl.BlockSpec(memory_space=pltpu.VMEM)),
        input_output_aliases={i: 2 + i for i in range(n + m)},
        compiler_params=pltpu.CompilerParams(has_side_effects=SIDE_EFFECT),
    )(*ins)
    return out[0], out[1], list(out[2:2 + n]), list(out[2 + n:2 + n + m]), out[-1]


def split_wait(name, handle, plan, after):
    send_sems, recv_sems, srcs, lands, _ = handle
    n, m = len(srcs), len(lands)
    after = list(after) if isinstance(after, (list, tuple)) else [after]

    def body(*refs):
        src_refs = refs[:n]
        land_refs = refs[n:n + m] if m else src_refs
        for cp in _split_copies(plan, src_refs, land_refs, refs[n + m], refs[n + m + 1]):
            cp.wait_send()
            cp.wait_recv()

    out = pl.pallas_call(
        body, name=name, out_shape=tuple(pltpu.HBM(a.shape, a.dtype) for a in srcs + lands),
        in_specs=[HBM] * (n + m) + [SEM, SEM] + [ANY] * len(after), out_specs=tuple([HBM] * (n + m)),
        input_output_aliases={i: i for i in range(n + m)},
        compiler_params=pltpu.CompilerParams(has_side_effects=SIDE_EFFECT),
    )(*srcs, *lands, send_sems, recv_sems, *after)
    return list(out[:n]), list(out[n:])


def gather_plan(n):
    def plan(x, y, c):
        me = 4 * x + 2 * y + c
        peers = [(x, y, 1 - c)] + [(*chip, c) for chip in _other_chips(x, y)]
        return [(k, None, me, peer) for k in range(n) for peer in peers]
    return plan


def pass_on_plan(n):
    def plan(x, y, c):
        return [(k, 4 * cx + 2 * cy + c, 4 * cx + 2 * cy + c, (x, y, 1 - c)) for k in range(n) for cx, cy in _other_chips(x, y)]
    return plan


def sibling_plan(n):
    def plan(x, y, c):
        return [(k, 4 * cx + 2 * cy + (1 - c), q, (x, y, 1 - c)) for k in range(n) for q, (cx, cy) in enumerate(CHIP_ORDER)]
    return plan


def chips_plan(n):
    def plan(x, y, c):
        return [(k, 1 + j, j, (*chip, c)) for k in range(n) for j, chip in enumerate(_other_chips(x, y))]
    return plan


ROW_TILE = 256
COL_TILE = 256


def _rms_fwd(tag, h, w, after=None):
    return rows_fwd(tag, _rms_f, [(h, D_MODEL, 0)], [w], [(D_MODEL, BF16)], ROW_TILE, after=after)[0]


def _rms_bwd(tag, h, w, dhn, dres):
    dh, (dw,) = rows_bwd(tag, _rms_f, [(h, D_MODEL, 0)], [w], [(dhn, D_MODEL, 0)], ROW_TILE, [(F32, BF16)], add=dres)
    return tuple(dh), dw


def even_fwd(tag, h, get_w, p, after=None):
    hn = _rms_fwd(tag + "_rms", h, p["nm"], after)
    w, after = get_w(hn)
    uv = matmul(tag + "_uv", hn, w["uv"], after=after)
    z = matmul(tag + "_z", hn, w["z"])
    xbc = matmul(tag + "_xbc", hn, w["xbc"])
    pdt = matmul(tag + "_dt", hn, w["dt"])
    gm = [p["lng"], p["lnb"], p["ws"], p["bst"]]
    ya = rows_fwd(tag + "_gmlp", _gmlp_f, [(uv, 2 * D_MODEL, 0)], gm, [(D_MODEL, BF16)], GM_BLOCK)[0]
    xa = cols_fwd(tag + "_conv", _conv_silu_f, [(xbc, 0)], [p["cw"], p["cb"]], F32, COL_TILE)
    y, states = ssd_fwd(tag + "_ssd", xa, pdt, p["hp"])
    yb = rows_fwd(tag + "_gate", _gate_norm_f, [(y, D_MODEL, 0), (z, D_MODEL, 0)], [p["nw"]], [(D_MODEL, BF16)], ROW_TILE)[0]
    h1 = matmul(tag + "_out_b", yb, w["out_bot"], res=matmul(tag + "_out_a", ya, w["out_top"], res=h))
    return h1, dict(h=h, hn=hn, uv=uv, z=z, xbc=xbc, pdt=pdt, xa=xa, y=y, states=states, ya=ya, yb=yb, w=w)


def even_bwd(tag, dh1, s, w, p, after=None, hook=None):
    dh1, dh1b = dh1
    dya = matmul(tag + "_dya", dh1b, w["out_top"], tb=True, after=after)
    dyb = matmul(tag + "_dyb", dh1b, w["out_bot"], tb=True, after=after)
    later = hook(dyb) if hook else None
    gw = dict(out_top=matmul(tag + "_gwa", s["ya"], dh1b, ta=True, out_dtype=BF16, after=later),
              out_bot=matmul(tag + "_gwb", s["yb"], dh1b, ta=True, out_dtype=BF16, after=later))
    (dy, dz), (dnw,) = rows_bwd(tag + "_gate_b", _gate_norm_f, [(s["y"], D_MODEL, 0), (s["z"], D_MODEL, 0)], [p["nw"]],
                                [(dyb, D_MODEL, 0)], ROW_TILE, [F32, BF16], after=later)
    dxs, dbm, dcm, dpdt, dhp = ssd_bwd(tag + "_ssd_b", s["xa"], s["pdt"], p["hp"], s["states"], dy)
    dxa = jnp.concatenate([dxs, dbm, dcm], axis=1)
    (dxbc,), (dcw, dcb) = cols_bwd(tag + "_conv_b", _conv_silu_f, [(s["xbc"], 0)], [p["cw"], p["cb"]], dxa, COL_TILE, BF16)
    gm = [p["lng"], p["lnb"], p["ws"], p["bst"]]
    (duv,), (dlng, dlnb, dws, dbst) = rows_bwd(tag + "_gmlp_b", _gmlp_f, [(s["uv"], 2 * D_MODEL, 0)], gm,
                                               [(dya, D_MODEL, 0)], GM_BLOCK, [BF16])
    dhn = None
    for key, d in (("uv", duv), ("z", dz), ("xbc", dxbc), ("dt", dpdt)):
        dhn = matmul(f"{tag}_dx_{key}", d, w[key], tb=True, res=dhn)
        gw[key] = matmul(f"{tag}_gw_{key}", s["hn"], d, ta=True, out_dtype=BF16)
    dh, dnm = _rms_bwd(tag + "_rms_b", s["h"], p["nm"], dhn, dh1)
    gp = dict(nm=dnm, lng=dlng, lnb=dlnb, ws=dws, bst=dbst, cw=dcw, cb=dcb, hp=dhp, nw=dnw)
    return dh, gw, gp


def odd_fwd(tag, h, get_w, p, cos, sin, after=None):
    hn = _rms_fwd(tag + "_rms", h, p["nm"], after)
    w, after = get_w(hn)
    proj = matmul(tag + "_in", hn, w["in"], after=after)
    cq, ckv, kpe = rows_fwd(tag + "_qkvn", _qkv_norm_f, [(proj, ODD_IN_PAD, 0), (cos, LANES, 0), (sin, LANES, 0)],
                            [p["qn"], p["kvn"]], [(MLA_RANK, BF16), (MLA_RANK, BF16), (LANES, F32)], ROW_TILE)
    q = matmul(tag + "_q", cq, w["uq"])
    kn = matmul(tag + "_kn", ckv, w["kn"], out_dtype=BF16)
    v = matmul(tag + "_v", ckv, w["v"], out_dtype=BF16)
    o = attn_fwd(tag + "_attn", q, kn, kpe, v, cos, sin)
    h1 = matmul(tag + "_o", o, w["o"], res=h)
    return h1, dict(h=h, hn=hn, proj=proj, cq=cq, ckv=ckv, kpe=kpe, q=q, kn=kn, v=v, o=o, w=w)


def odd_bwd(tag, dh1, s, w, p, cos, sin, after=None, hook=None):
    dh1, dh1b = dh1
    do = matmul(tag + "_do", dh1b, w["o"], tb=True, after=after)
    later = hook(do) if hook else None
    gw = dict(o=matmul(tag + "_gw_o", s["o"], dh1b, ta=True, out_dtype=BF16, after=later))
    dq, dkn, dkpe, dv = attn_bwd(tag + "_attn_b", s["q"], s["kn"], s["kpe"], s["v"], cos, sin, do, after=later)
    dcq = matmul(tag + "_dcq", dq, w["uq"], tb=True)
    gw["uq"] = matmul(tag + "_gw_uq", s["cq"], dq, ta=True, out_dtype=BF16)
    dckv = matmul(tag + "_dckv_v", dv, w["v"], tb=True, res=matmul(tag + "_dckv_k", dkn, w["kn"], tb=True))
    gw["kn"] = matmul(tag + "_gw_kn", s["ckv"], dkn, ta=True, out_dtype=BF16)
    gw["v"] = matmul(tag + "_gw_v", s["ckv"], dv, ta=True, out_dtype=BF16)
    (dproj,), (dqn, dkvn) = rows_bwd(
        tag + "_qkvn_b", _qkv_norm_f, [(s["proj"], ODD_IN_PAD, 0), (cos, LANES, 0), (sin, LANES, 0)], [p["qn"], p["kvn"]],
        [(dcq, MLA_RANK, 0), (dckv, MLA_RANK, 0), (dkpe, LANES, 0)], ROW_TILE, [BF16], n_nondiff=2)
    dhn = matmul(tag + "_dx_in", dproj, w["in"], tb=True)
    gw["in"] = matmul(tag + "_gw_in", s["hn"], dproj, ta=True, out_dtype=BF16)
    dh, dnm = _rms_bwd(tag + "_rms_b", s["h"], p["nm"], dhn, dh1)
    return dh, gw, dict(nm=dnm, qn=dqn, kvn=dkvn)


def ffn_fwd(tag, h, get_w, p, after=None):
    hn = _rms_fwd(tag + "_rms", h, p["nf"], after)
    w, after = get_w(hn)
    g = matmul(tag + "_up_g", hn, w["up_g"], after=after)
    val = matmul(tag + "_up_v", hn, w["up_v"])
    act = cols_fwd(tag + "_act", _ffn_act_f, [(g, 0), (val, 0)], [p["fcw"], p["fcb"]], BF16, COL_TILE)
    h2 = matmul(tag + "_down", act, w["down"], res=h)
    return h2, dict(h=h, hn=hn, g=g, val=val, act=act, w=w)


def ffn_bwd(tag, dh2, s, w, p, after=None, hook=None):
    dh2, dh2b = dh2
    dact = matmul(tag + "_dact", dh2b, w["down"], tb=True, after=after)
    later = hook(dact) if hook else None
    gw = dict(down=matmul(tag + "_gw_down", s["act"], dh2b, ta=True, out_dtype=BF16, after=later))
    (dg, dval), (dfcw, dfcb) = cols_bwd(tag + "_act_b", _ffn_act_f, [(s["g"], 0), (s["val"], 0)], [p["fcw"], p["fcb"]],
                                        dact, COL_TILE, BF16, after=later)
    dhn = matmul(tag + "_dx_v", dval, w["up_v"], tb=True, res=matmul(tag + "_dx_g", dg, w["up_g"], tb=True))
    gw["up"] = matmul_tn_slots(tag + "_gw_up", s["hn"], [dg, dval], 2 * D_FF // N_DEV)
    dh, dnf = _rms_bwd(tag + "_rms_b", s["h"], p["nf"], dhn, dh2)
    return dh, gw, dict(nf=dnf, fcw=dfcw, fcb=dfcb)


def _cols_from_slots(g):
    return jnp.moveaxis(g, 0, 1).reshape(g.shape[1], N_DEV * g.shape[2])


def _slots_from_cols(wmat):
    k, n = wmat.shape
    return jnp.moveaxis(wmat.reshape(k, N_DEV, n // N_DEV), 1, 0)


def _pad_last(a, width):
    return jnp.pad(a, [(0, 0)] * (a.ndim - 1) + [(0, width - a.shape[-1])])


def _heads_to_lanes(a):
    lead = a.shape[:-1]
    return _pad_last(a.reshape(lead + (SSD_STEPS, SSD_HEADS_PER_STEP)), LANES).reshape(lead + (SSD_STEPS * LANES,))


def _lanes_to_heads(a):
    lead = a.shape[:-1]
    return a.reshape(lead + (SSD_STEPS, LANES))[..., :SSD_HEADS_PER_STEP].reshape(lead + (SSM_HEADS,))


def prep_even(g_in, g_out):
    wn = _cols_from_slots(g_in)
    o1, o2, o3 = 2 * D_MODEL, 3 * D_MODEL, 3 * D_MODEL + SSM_CONV_DIM
    out = g_out.reshape(2 * D_MODEL, D_MODEL)
    return dict(uv=wn[:, :o1], z=wn[:, o1:o2], xbc=wn[:, o2:o3], dt=_heads_to_lanes(wn[:, o3:]),
                out_top=out[:D_MODEL], out_bot=out[D_MODEL:])


def unprep_even(gw):
    wn = jnp.concatenate([gw["uv"], gw["z"], gw["xbc"], _lanes_to_heads(gw["dt"])], axis=1)
    return _slots_from_cols(wn), jnp.concatenate([gw["out_top"], gw["out_bot"]], axis=0).reshape(N_DEV, -1, D_MODEL)


def prep_odd(g_in, g_uq, g_ukv, g_o):
    uq = _cols_from_slots(g_uq).reshape(MLA_RANK, MLA_HEADS, MLA_QK)
    ukv = _cols_from_slots(g_ukv).reshape(MLA_RANK, MLA_HEADS, MLA_NOPE + MLA_V)
    return dict(**{"in": _pad_last(g_in.reshape(D_MODEL, ODD_IN), ODD_IN_PAD)},
                uq=_pad_last(uq, MLA_QPAD).reshape(MLA_RANK, MLA_HEADS * MLA_QPAD),
                kn=ukv[:, :, :MLA_NOPE].reshape(MLA_RANK, MLA_HEADS * MLA_NOPE),
                v=ukv[:, :, MLA_NOPE:].reshape(MLA_RANK, MLA_HEADS * MLA_V),
                o=g_o.reshape(MLA_HEADS * MLA_V, D_MODEL))


def unprep_odd(gw):
    uq = gw["uq"].reshape(MLA_RANK, MLA_HEADS, MLA_QPAD)[:, :, :MLA_QK].reshape(MLA_RANK, MLA_HEADS * MLA_QK)
    ukv = jnp.concatenate([gw["kn"].reshape(MLA_RANK, MLA_HEADS, MLA_NOPE), gw["v"].reshape(MLA_RANK, MLA_HEADS, MLA_V)], axis=2)
    return (gw["in"][:, :ODD_IN].reshape(N_DEV, -1, ODD_IN), _slots_from_cols(uq),
            _slots_from_cols(ukv.reshape(MLA_RANK, -1)), gw["o"].reshape(N_DEV, -1, D_MODEL))


def prep_ffn(g_up, g_down):
    up = _cols_from_slots(g_up)
    return dict(up_g=up[:, :D_FF], up_v=up[:, D_FF:], down=g_down.reshape(D_FF, D_MODEL))


def unprep_ffn(gw):
    return gw["up"], gw["down"].reshape(N_DEV, -1, D_MODEL)


SMALL_TILE = LANES * LANES


def _pack(arrs):
    flat = jnp.concatenate([a.reshape(-1).astype(F32) for a in arrs])
    size = -(-flat.shape[0] // SMALL_TILE) * SMALL_TILE
    return jnp.pad(flat, (0, size - flat.shape[0])).reshape(-1, LANES)


def _unpack(packed, shapes, lead=()):
    flat = packed.reshape(lead + (-1,))
    out, off = [], 0
    for shp in shapes:
        size = math.prod(shp)
        out.append(flat[..., off:off + size].reshape(lead + tuple(shp)))
        off += size
    return out


SMALL_SHARDED = {"ev_gm_ln_g": 2, "ev_gm_ln_b": 2, "ev_conv_w": 2, "od_q_norm": 1, "od_kv_norm": 1, "ff_conv_w": 2}
SMALL_REPLICATED = ["norm_mix", "norm_ffn", "norm_final", "ev_gm_ws", "ev_gm_bs", "ev_conv_b", "ev_dt_bias", "ev_a_log",
                    "ev_d_skip", "ev_ssm_norm_w", "ff_conv_b"]
MATRICES = {"ev_w_in": (2, 2048, 1156), "ev_w_out": (2, 512, 2048), "od_w_in": (2, 256, 1088), "od_w_uq": (2, 512, 384),
            "od_w_ukv": (2, 512, 512), "od_w_o": (2, 256, 2048), "ff_w_up": (4, 2048, 1408), "ff_w_down": (4, 704, 2048)}
WEIGHT_ORDER = ["norm_mix", "norm_ffn", "norm_final", "ev_w_in", "ev_gm_ln_g", "ev_gm_ln_b", "ev_gm_ws", "ev_gm_bs",
                "ev_conv_w", "ev_conv_b", "ev_dt_bias", "ev_a_log", "ev_d_skip", "ev_ssm_norm_w", "ev_w_out", "od_w_in",
                "od_q_norm", "od_kv_norm", "od_w_uq", "od_w_ukv", "od_w_o", "ff_w_up", "ff_conv_w", "ff_conv_b", "ff_w_down"]


def _full_from_shards(name, gathered):
    ax = SMALL_SHARDED[name]
    moved = jnp.moveaxis(gathered, 0, ax)
    shp = moved.shape
    return moved.reshape(shp[:ax] + (shp[ax] * shp[ax + 1],) + shp[ax + 2:])


def _my_shard(name, full, dev):
    ax = SMALL_SHARDED[name]
    shp = full.shape
    split = full.reshape(shp[:ax] + (N_DEV, shp[ax] // N_DEV) + shp[ax + 1:])
    return lax.dynamic_index_in_dim(split, dev, axis=ax, keepdims=False)


def _even_small(sm, j):
    row = lambda a: a.reshape(1, -1)
    hp = jnp.stack([sm["ev_dt_bias"][j], sm["ev_a_log"][j], sm["ev_d_skip"][j]])
    return dict(nm=row(sm["norm_mix"][2 * j]), lng=row(sm["ev_gm_ln_g"][j]), lnb=row(sm["ev_gm_ln_b"][j]),
                ws=sm["ev_gm_ws"][j], bst=sm["ev_gm_bs"][j].T, cw=sm["ev_conv_w"][j], cb=row(sm["ev_conv_b"][j]),
                hp=_heads_to_lanes(hp), nw=row(sm["ev_ssm_norm_w"][j]))


def _odd_small(sm, j):
    row = lambda a: a.reshape(1, -1)
    return dict(nm=row(sm["norm_mix"][2 * j + 1]), qn=row(sm["od_q_norm"][j]), kvn=row(sm["od_kv_norm"][j]))


def _ffn_small(sm, layer):
    row = lambda a: a.reshape(1, -1)
    return dict(nf=row(sm["norm_ffn"][layer]), fcw=sm["ff_conv_w"][layer], fcb=row(sm["ff_conv_b"][layer]))


def _rope_tables(positions):
    inv_freq = ROPE_THETA ** (-jnp.arange(0, MLA_ROPE, 2, dtype=F32) / MLA_ROPE)
    ang = positions.astype(F32).reshape(-1, 1) * inv_freq
    cos, sin = jnp.cos(ang), jnp.sin(ang)
    return _pad_last(jnp.concatenate([cos, cos], axis=1), LANES), _pad_last(jnp.concatenate([-sin, sin], axis=1), LANES)


def local_step(x, positions, target, sm, fetch_weights, emit_grads):
    cos, sin = _rope_tables(positions)
    h, saved = x, []
    for layer in range(4):
        j, tag = layer // 2, f"l{layer}"
        get_w, dep = fetch_weights(2 * layer, h)
        if layer % 2 == 0:
            pm = _even_small(sm, j)
            h, sv = even_fwd(tag, h, get_w, pm, dep)
        else:
            pm = _odd_small(sm, j)
            h, sv = odd_fwd(tag, h, get_w, pm, cos, sin, dep)
        get_w, dep = fetch_weights(2 * layer + 1, h)
        pf = _ffn_small(sm, layer)
        h, sf = ffn_fwd(tag + "f", h, get_w, pf, dep)
        saved.append((pm, sv, pf, sf, sv["w"], sf["w"]))
    loss_tile, dh32, dh16, dnfinal = final_loss("final_loss", h, sm["norm_final"].reshape(1, -1), target)
    gs = {k: [None] * v.shape[0] for k, v in sm.items() if k != "norm_final"}
    gs["norm_final"] = dnfinal.reshape(-1)
    dh = (dh32, dh16)
    dep, hook = None, None
    for layer in reversed(range(4)):
        j, tag = layer // 2, f"l{layer}"
        pm, sv, pf, sf, wm, wf = saved[layer]
        dh, gwf, gpf = ffn_bwd(tag + "f", dh, sf, wf, pf, dep, hook)
        gs["norm_ffn"][layer], gs["ff_conv_w"][layer], gs["ff_conv_b"][layer] = gpf["nf"][0], gpf["fcw"], gpf["fcb"][0]
        dep, hook = emit_grads(2 * layer + 1, gwf, dh[0])
        if layer % 2 == 0:
            dh, gwm, gp = even_bwd(tag, dh, sv, wm, pm, dep, hook)
            hp = _lanes_to_heads(gp["hp"])
            gs["norm_mix"][layer] = gp["nm"][0]
            gs["ev_gm_ln_g"][j], gs["ev_gm_ln_b"][j] = gp["lng"].reshape(GM_GROUPS, -1), gp["lnb"].reshape(GM_GROUPS, -1)
            gs["ev_gm_ws"][j], gs["ev_gm_bs"][j] = gp["ws"], gp["bst"].T
            gs["ev_conv_w"][j], gs["ev_conv_b"][j] = gp["cw"], gp["cb"][0]
            gs["ev_dt_bias"][j], gs["ev_a_log"][j], gs["ev_d_skip"][j] = hp[0], hp[1], hp[2]
            gs["ev_ssm_norm_w"][j] = gp["nw"][0]
        else:
            dh, gwm, gp = odd_bwd(tag, dh, sv, wm, pm, cos, sin, dep, hook)
            gs["norm_mix"][layer] = gp["nm"][0]
            gs["od_q_norm"][j], gs["od_kv_norm"][j] = gp["qn"][0], gp["kvn"][0]
        if layer > 0:
            dep, hook = emit_grads(2 * layer, gwm, dh[0])
    gs = {k: (v if k == "norm_final" else jnp.stack(v)) for k, v in gs.items()}
    return loss_tile[0, 0], dh[0], gs, gwm


def kernel(x, positions, norm_mix, norm_ffn, norm_final, ev_w_in, ev_gm_ln_g, ev_gm_ln_b, ev_gm_ws, ev_gm_bs, ev_conv_w, ev_conv_b, ev_dt_bias, ev_a_log, ev_d_skip, ev_ssm_norm_w, ev_w_out, od_w_in, od_q_norm, od_kv_norm, od_w_uq, od_w_ukv, od_w_o, ff_w_up, ff_conv_w, ff_conv_b, ff_w_down, loss_target, m_norm_mix, m_norm_ffn, m_norm_final, m_ev_w_in, m_ev_gm_ln_g, m_ev_gm_ln_b, m_ev_gm_ws, m_ev_gm_bs, m_ev_conv_w, m_ev_conv_b, m_ev_dt_bias, m_ev_a_log, m_ev_d_skip, m_ev_ssm_norm_w, m_ev_w_out, m_od_w_in, m_od_q_norm, m_od_kv_norm, m_od_w_uq, m_od_w_ukv, m_od_w_o, m_ff_w_up, m_ff_conv_w, m_ff_conv_b, m_ff_w_down, v_norm_mix, v_norm_ffn, v_norm_final, v_ev_w_in, v_ev_gm_ln_g, v_ev_gm_ln_b, v_ev_gm_ws, v_ev_gm_bs, v_ev_conv_w, v_ev_conv_b, v_ev_dt_bias, v_ev_a_log, v_ev_d_skip, v_ev_ssm_norm_w, v_ev_w_out, v_od_w_in, v_od_q_norm, v_od_kv_norm, v_od_w_uq, v_od_w_ukv, v_od_w_o, v_ff_w_up, v_ff_conv_w, v_ff_conv_b, v_ff_w_down):
    args = dict(locals())
    wts = {n: args[n] for n in WEIGHT_ORDER}
    mom = {n: args["m_" + n] for n in WEIGHT_ORDER}
    var = {n: args["v_" + n] for n in WEIGHT_ORDER}
    dev = _device_slot()

    small_names = list(SMALL_SHARDED)
    small_shapes = [wts[n].shape for n in small_names]
    (small_all,) = all_gather("ag_small", [_pack([wts[n] for n in small_names])])
    small_full = _unpack(small_all, small_shapes, lead=(N_DEV,))
    sm = {n: _full_from_shards(n, g) for n, g in zip(small_names, small_full)}
    sm.update({n: wts[n] for n in SMALL_REPLICATED})

    def stage_matrices(stage):
        layer, is_ffn = divmod(stage, 2)
        if is_ffn:
            return [("ff_w_up", layer), ("ff_w_down", layer)]
        return [(n, layer // 2) for n in (["ev_w_in", "ev_w_out"] if layer % 2 == 0 else ["od_w_in", "od_w_uq", "od_w_ukv", "od_w_o"])]

    def stage_fns(stage):
        layer, is_ffn = divmod(stage, 2)
        if is_ffn:
            return prep_ffn, unprep_ffn
        return (prep_even, unprep_even) if layer % 2 == 0 else (prep_odd, unprep_odd)

    n_stages, ahead = 8, 2

    bf = {n: wts[n].astype(BF16) for n in MATRICES}

    def start_gather(stage, earlier=None):
        shards = [bf[n][i] for n, i in stage_matrices(stage)]
        if earlier is not None:
            shards, _ = lax.optimization_barrier((shards, earlier))
        return split_start(f"ag_s{stage}_start", shards, [(N_DEV,) + s.shape for s in shards],
                           gather_plan(len(shards)), 4 * len(shards))

    gathers = {}
    for stage in range(ahead):
        gathers[stage] = start_gather(stage, gathers[stage - 1][4] if stage else None)

    def fetch_weights(stage, h):
        n = len(stage_matrices(stage))
        shards, landed = split_wait(f"ag_s{stage}_wait", gathers.pop(stage), gather_plan(n), h)
        passing = split_start(f"ag_s{stage}_pass_start", landed, None, pass_on_plan(n), 3 * n)

        def get_w(first_result):
            g, _ = split_wait(f"ag_s{stage}_pass_wait", passing, pass_on_plan(n), first_result)
            g = [lax.dynamic_update_index_in_dim(gk, sk, dev, 0) for gk, sk in zip(g, shards)]
            started = None
            if stage + ahead < n_stages:
                gathers[stage + ahead] = start_gather(stage + ahead, g[0])
                started = gathers[stage + ahead][4]
            return stage_fns(stage)[0](*g), started

        return get_w, passing[4]

    scatters = []
    out = {n: None for n in MATRICES}

    held = []
    hold_below = 6

    def update(mats, sums, recv, after=None):
        for (n, i), p_, r_ in zip(mats, sums, recv):
            layers, rows, cols = MATRICES[n]
            two_d = lambda a: a.reshape(layers * rows, cols)
            out[n] = adamw(f"adamw_{n}_{i}", [(p_, 0), (r_, 0), (r_, 1), (r_, 2)], two_d(wts[n]), two_d(mom[n]),
                           two_d(var[n]), tr=_row_tile(rows), part=i, prev=out[n], after=after)
            if after is not None:
                after = out[n][0]
        return after

    def finish_scatter(after):
        stage, handle = scatters.pop(0)
        mats = stage_matrices(stage)
        sums, recv = split_wait(f"rs_s{stage}_wait", handle, chips_plan(len(mats)), after)
        if 0 < stage < hold_below:
            held.append((mats, sums, recv))
        else:
            update(mats, sums, recv)

    def start_chips(stage, send, from_sibling):
        sums = [chip_sums(f"rs_s{stage}_add{k}", g, r, tr=_row_tile(g.shape[1], 512)) for k, (g, r) in enumerate(zip(send, from_sibling))]
        handle = split_start(f"rs_s{stage}_start", sums, [(3,) + s.shape[1:] for s in sums], chips_plan(len(sums)), 3 * len(sums))
        scatters.append((stage, handle))
        return handle[4]

    def emit_grads(stage, gw, dh):
        if len(scatters) >= ahead:
            finish_scatter(dh)
        send = list(stage_fns(stage)[1](gw))
        plan = sibling_plan(len(send))
        handle = split_start(f"rs_s{stage}_sib_start", send, [(4,) + g.shape[1:] for g in send], plan, 4 * len(send))

        def hook(first_result):
            sent, from_sibling = split_wait(f"rs_s{stage}_sib_wait", handle, plan, first_result)
            return start_chips(stage, sent, from_sibling)

        return handle[4], hook

    def emit_last(stage, gw, after):
        finish_scatter(after)
        send = list(stage_fns(stage)[1](gw))
        return start_chips(stage, send, exchange_sibling(f"rs_s{stage}_sibling", send, after))

    loss_local, dx, gs, gw_first = local_step(x[0], positions[0], loss_target[0], sm, fetch_weights, emit_grads)
    loss = lax.psum(loss_local, ("x", "y", "c"))

    all_small = small_names + SMALL_REPLICATED
    (partials,) = all_gather("ar_small", [_pack([gs[n] for n in all_small])])
    last_started = emit_last(0, gw_first, partials)
    total = sum_parts("ar_small_sum", [(partials, s) for s in range(N_DEV)])
    g_full = dict(zip(all_small, _unpack(total, [gs[n].shape for n in all_small])))
    g_mine = {n: (_my_shard(n, g_full[n], dev) if n in SMALL_SHARDED else g_full[n]) for n in all_small}
    packed = [_pack([d[n] for n in all_small]) for d in (g_mine, wts, mom, var)]
    res = adamw("adamw_small", [(packed[0][None], 0)], packed[1], packed[2], packed[3], after=last_started)
    unpacked = [_unpack(a, [wts[n].shape for n in all_small]) for a in res]
    for i, n in enumerate(all_small):
        out[n] = [u[i] for u in unpacked]
    while len(scatters) > 1:
        finish_scatter(res[0])
    follow = res[0]
    for job in held:
        follow = update(*job, after=follow)
    finish_scatter(follow)
    for n in MATRICES:
        out[n] = [a.reshape(wts[n].shape) for a in out[n]]

    return (loss, dx[None], *[out[n][0] for n in WEIGHT_ORDER], *[out[n][1] for n in WEIGHT_ORDER],
            *[out[n][2] for n in WEIGHT_ORDER], *[out[n][3] for n in WEIGHT_ORDER])
```

```python
import functools
import math

import jax
import jax.numpy as jnp
from jax import lax
from jax.experimental import pallas as pl
from jax.experimental.pallas import tpu as pltpu

F32 = jnp.float32
BF16 = jnp.bfloat16
MESH = pl.DeviceIdType.MESH

V7X_VMEM_LIMIT_BYTES = 56 * 1024 * 1024
LANES = 128

EPS = 1e-6
D_MODEL = 2048
CHUNK = 64
GM_BLOCK = 128
GM_GROUPS = 8
GM_GROUP_DIM = D_MODEL // GM_GROUPS
SSM_HEADS = 32
SSM_HEAD_DIM = 64
SSM_GROUPS = 4
SSM_STATE = 128
SSM_BC = SSM_GROUPS * SSM_STATE
SSM_CONV_DIM = D_MODEL + 2 * SSM_BC
SSD_HEADS_PER_STEP = 4
SSD_STEPS = SSM_HEADS // SSD_HEADS_PER_STEP
SSD_X_WIDTH = SSD_HEADS_PER_STEP * SSM_HEAD_DIM
MLA_HEADS = 16
MLA_RANK = 512
MLA_NOPE = 128
MLA_ROPE = 64
MLA_V = 128
MLA_QK = MLA_NOPE + MLA_ROPE
MLA_QPAD = 2 * LANES
ODD_IN = 2 * MLA_RANK + MLA_ROPE
ODD_IN_PAD = 2 * MLA_RANK + LANES
D_FF = 5632
ROPE_THETA = 10000.0
N_DEV = 8

ADAM_LR, ADAM_B1, ADAM_B2, ADAM_EPS, ADAM_WD, ADAM_STEP = 0.001, 0.9, 0.999, 1e-08, 0.01, 10


def _params(*sem):
    return pltpu.CompilerParams(dimension_semantics=sem, vmem_limit_bytes=V7X_VMEM_LIMIT_BYTES)


def _pick(dim, target):
    if dim <= target:
        return dim
    t = (target // LANES) * LANES
    while t >= LANES:
        if dim % t == 0:
            return t
        t -= LANES
    raise ValueError(f"no tile for {dim} under {target}")


MATMUL_VMEM_BUDGET = 32 * 1024 * 1024


def _matmul_tiles(m, n, k, a_bytes, b_bytes, o_bytes, has_res, ta):
    def fits(tm, tn):
        per_out = o_bytes + (4 if has_res else 0)
        return 2 * (tm * k * a_bytes + tn * k * b_bytes + tm * tn * per_out) <= MATMUL_VMEM_BUDGET

    tns = (2048, 1024, 512, 256, 128) if ta else (512, 256, 128)
    tms = (512, 256, 128) if ta else (2048, 1024, 512, 256, 128)
    for tn in tns:
        tn = _pick(n, tn)
        for tm in tms:
            tm = _pick(m, tm)
            if fits(tm, tn):
                return tm, tn
    raise ValueError(f"no matmul tiles for {m}x{n}x{k}")


def matmul(name, a, b, *, ta=False, tb=False, res=None, out_dtype=F32, after=None):
    m, k = (a.shape[1], a.shape[0]) if ta else a.shape
    n = b.shape[0] if tb else b.shape[1]
    assert k == (b.shape[1] if tb else b.shape[0]), (name, a.shape, b.shape)
    tm, tn = _matmul_tiles(m, n, k, a.dtype.itemsize, b.dtype.itemsize, jnp.dtype(out_dtype).itemsize, res is not None, ta)
    dims = (((0 if ta else 1,), (1 if tb else 0,)), ((), ()))

    def body(*refs):
        a_ref, b_ref, o_ref = refs[0], refs[1], refs[-1]
        total = lax.dot_general(a_ref[...].astype(BF16), b_ref[...].astype(BF16), dims, preferred_element_type=F32)
        if res is not None:
            total = total + refs[2][...]
        o_ref[...] = total.astype(o_ref.dtype)

    a_spec = pl.BlockSpec((k, tm), lambda i, j: (0, i)) if ta else pl.BlockSpec((tm, k), lambda i, j: (i, 0))
    b_spec = pl.BlockSpec((tn, k), lambda i, j: (j, 0)) if tb else pl.BlockSpec((k, tn), lambda i, j: (0, j))
    o_spec = pl.BlockSpec((tm, tn), lambda i, j: (i, j))
    ins, specs = [a, b], [a_spec, b_spec]
    if res is not None:
        ins.append(res)
        specs.append(o_spec)
    if after is not None:
        ins.append(after)
        specs.append(pl.BlockSpec(memory_space=pl.ANY))
    return pl.pallas_call(
        body, name=name, grid=(m // tm, n // tn), in_specs=specs, out_specs=o_spec,
        out_shape=jax.ShapeDtypeStruct((m, n), out_dtype),
        compiler_params=_params("parallel", "parallel"),
    )(*ins)


def matmul_tn_slots(name, a, bs, width, tm=512):
    k, m = a.shape
    counts = [b.shape[1] // width for b in bs]
    firsts = [sum(counts[:i]) for i in range(len(bs))]
    tm = _pick(m, tm)
    tn_dims = (((0,), (0,)), ((), ()))

    def body(*refs):
        a_ref, o_ref = refs[0], refs[-1]
        j = pl.program_id(1)
        for b_ref, first, count in zip(refs[1:-1], firsts, counts):
            @pl.when(jnp.logical_and(j >= first, j < first + count))
            def _(b_ref=b_ref):
                o_ref[0] = lax.dot_general(a_ref[...].astype(BF16), b_ref[...].astype(BF16), tn_dims,
                                           preferred_element_type=F32).astype(o_ref.dtype)

    specs = [pl.BlockSpec((k, tm), lambda i, j: (0, i))]
    specs += [pl.BlockSpec((k, width), lambda i, j, first=first, count=count: (0, jnp.clip(j - first, 0, count - 1)))
              for first, count in zip(firsts, counts)]
    return pl.pallas_call(
        body, name=name, grid=(m // tm, sum(counts)), in_specs=specs,
        out_specs=pl.BlockSpec((1, tm, width), lambda i, j: (j, i, 0)),
        out_shape=jax.ShapeDtypeStruct((sum(counts), m, width), BF16),
        compiler_params=_params("parallel", "arbitrary"),
    )(a, *bs)


@functools.partial(jax.custom_vjp, nondiff_argnums=(1, 2))
def _roll(x, shift, axis):
    return pltpu.roll(x, shift, axis)


def _roll_fwd(x, shift, axis):
    return pltpu.roll(x, shift, axis), None


def _roll_bwd(shift, axis, _, g):
    return (pltpu.roll(g, (g.shape[axis] - shift) % g.shape[axis], axis),)


_roll.defvjp(_roll_fwd, _roll_bwd)


def _shift_down(x, s):
    rows = lax.broadcasted_iota(jnp.int32, x.shape, 0)
    return jnp.where(rows >= s, _roll(x, s, 0), 0.0)


def _dwconv(x, w, b):
    taps = w.shape[0]
    y = b + w[taps - 1:taps, :] * x
    for kk in range(taps - 1):
        y = y + w[kk:kk + 1, :] * _shift_down(x, taps - 1 - kk)
    return y


def _rms(x, w):
    return x * lax.rsqrt(jnp.mean(x * x, -1, keepdims=True) + EPS) * w


def _rms_f(h, w):
    return (_rms(h, w),)


def _gmlp_f(uv, lng, lnb, ws, bst):
    r = lax.broadcasted_iota(jnp.int32, (GM_BLOCK, GM_BLOCK), 0) // CHUNK
    c = lax.broadcasted_iota(jnp.int32, (GM_BLOCK, GM_BLOCK), 1) // CHUNK
    outs = []
    for g in range(GM_GROUPS):
        lo, hi = g * GM_GROUP_DIM, (g + 1) * GM_GROUP_DIM
        gu = jax.nn.gelu(uv[:, lo:hi])
        gv = jax.nn.gelu(uv[:, D_MODEL + lo:D_MODEL + hi])
        xc = gv - jnp.mean(gv, -1, keepdims=True)
        var = jnp.mean(xc * xc, -1, keepdims=True)
        vn = xc * lax.rsqrt(var + EPS) * lng[:, lo:hi] + lnb[:, lo:hi]
        wm = jnp.where(r >= c, ws[g], 0.0).astype(BF16)
        gate = jnp.dot(wm, vn.astype(BF16), preferred_element_type=F32) + bst[:, g:g + 1]
        outs.append(gu * gate)
    return (jnp.concatenate(outs, axis=1),)


def _conv_silu_f(x, w, b):
    return (jax.nn.silu(_dwconv(x, w, b)),)


def _ffn_act_f(g, val, w, b):
    return (jax.nn.gelu(_dwconv(g, w, b)) * val,)


def _gate_norm_f(y, z, nw):
    y2 = y * jax.nn.silu(z)
    width = D_MODEL // SSM_GROUPS
    outs = []
    for g in range(SSM_GROUPS):
        blk = y2[:, g * width:(g + 1) * width]
        outs.append(blk * lax.rsqrt(jnp.mean(blk * blk, -1, keepdims=True) + EPS))
    return (jnp.concatenate(outs, axis=1) * nw,)


def _rope(x, cos, sin):
    lane = lax.broadcasted_iota(jnp.int32, x.shape, 1)
    half = MLA_ROPE // 2
    swapped = jnp.where(lane < half, _roll(x, LANES - half, 1), _roll(x, half, 1))
    return x * cos + swapped * sin


def _qkv_norm_f(proj, cos, sin, qn, kvn):
    cq = _rms(proj[:, :MLA_RANK], qn)
    ckv = _rms(proj[:, MLA_RANK:2 * MLA_RANK], kvn)
    kpe = _rope(proj[:, 2 * MLA_RANK:], cos, sin)
    return cq, ckv, kpe


def _attn_scores(q0, k0, qh, kn, kpe, cos, sin):
    qn = qh[:, :MLA_NOPE]
    qp = _rope(qh[:, MLA_NOPE:], cos, sin)
    nt = (((1,), (1,)), ((), ()))
    s = lax.dot_general(qn.astype(BF16), kn.astype(BF16), nt, preferred_element_type=F32)
    s = s + lax.dot_general(qp.astype(BF16), kpe.astype(BF16), nt, preferred_element_type=F32)
    s = s * (MLA_QK ** -0.5)
    visible_below = ((q0 + lax.broadcasted_iota(jnp.int32, (s.shape[0], 1), 0)) // CHUNK + 1) * CHUNK - k0
    return jnp.where(lax.broadcasted_iota(jnp.int32, s.shape, 1) < visible_below, s, -jnp.inf)


def _ssd_chunk_f(x, bm, cm, pdt, hp, sprev):
    nh, hd = SSD_HEADS_PER_STEP, SSM_HEAD_DIM
    dt = jax.nn.softplus(pdt + hp[0:1, :])
    cs = dt * (-jnp.exp(hp[1:2, :]))
    shift = 1
    while shift < CHUNK:
        cs = cs + _shift_down(cs, shift)
        shift *= 2
    cst = cs.T
    tot = cs[CHUNK - 1:CHUNK, :]

    def lanes(vals):
        return jnp.concatenate([jnp.broadcast_to(vals[:, e:e + 1], (vals.shape[0], hd)) for e in range(nh)], axis=1)

    r = lax.broadcasted_iota(jnp.int32, (CHUNK, CHUNK), 0)
    c = lax.broadcasted_iota(jnp.int32, (CHUNK, CHUNK), 1)
    tril = r >= c
    nt = (((1,), (1,)), ((), ()))
    tn = (((0,), (0,)), ((), ()))
    xd = x * lanes(dt)
    cb = lax.dot_general(cm.astype(BF16), bm.astype(BF16), nt, preferred_element_type=F32)
    ys = []
    for e in range(nh):
        decay = jnp.exp(jnp.where(tril, cs[:, e:e + 1] - cst[e:e + 1, :], -jnp.inf))
        ys.append(jnp.dot((cb * decay).astype(BF16), xd[:, e * hd:(e + 1) * hd].astype(BF16), preferred_element_type=F32))
    st = lax.dot_general((xd * lanes(jnp.exp(tot - cs))).astype(BF16), bm.astype(BF16), tn, preferred_element_type=F32)
    yoff = lax.dot_general(cm.astype(BF16), sprev.astype(BF16), nt, preferred_element_type=F32)
    y = jnp.concatenate(ys, axis=1) + yoff * lanes(jnp.exp(cs)) + lanes(hp[2:3, :]) * x
    carry = jnp.concatenate([jnp.broadcast_to(jnp.exp(tot[:, e:e + 1]), (hd, 1)) for e in range(nh)], axis=0)
    return y, carry * sprev + st


def _full_spec(a):
    nd = a.ndim
    return pl.BlockSpec(a.shape, lambda i, nd=nd: (0,) * nd)


def rows_fwd(name, f, rows, params, outs, tr, after=None):
    t = rows[0][0].shape[0]
    nr, npar = len(rows), len(params)
    extra = [] if after is None else [after]

    def body(*refs):
        vals = f(*[x[...].astype(F32) for x in refs[:nr + npar]])
        for o_ref, val in zip(refs[nr + npar + len(extra):], vals):
            o_ref[...] = val.astype(o_ref.dtype)

    in_specs = [pl.BlockSpec((tr, w), lambda i, cb=cb: (i, cb)) for _, w, cb in rows] + [_full_spec(p) for p in params]
    in_specs += [pl.BlockSpec(memory_space=pl.ANY)] * len(extra)
    out = pl.pallas_call(
        body, name=name, grid=(t // tr,), in_specs=in_specs,
        out_specs=[pl.BlockSpec((tr, w), lambda i: (i, 0)) for w, _ in outs],
        out_shape=[jax.ShapeDtypeStruct((t, w), dt) for w, dt in outs],
        compiler_params=_params("parallel"),
    )(*[a for a, _, _ in rows], *params, *extra)
    return out


def rows_bwd(name, f, rows, params, cots, tr, d_dtypes, n_nondiff=0, add=None, after=None):
    t = rows[0][0].shape[0]
    nr, npar, nc = len(rows), len(params), len(cots)
    nd = nr - n_nondiff
    has_add = add is not None
    copies = [(j, dt) for j in range(nd) for dt in (d_dtypes[j] if isinstance(d_dtypes[j], tuple) else (d_dtypes[j],))]
    ncp = len(copies)

    def body(*refs):
        i = pl.program_id(0)
        row_vals = [x[...].astype(F32) for x in refs[:nr]]
        par_vals = [x[...].astype(F32) for x in refs[nr:nr + npar]]
        cot_refs = refs[nr + npar:nr + npar + nc]
        pos = nr + npar + nc
        add_ref = refs[pos] if has_add else None
        pos += int(has_add) + int(after is not None)
        drow_refs = refs[pos:pos + ncp]
        dpar_refs = refs[pos + ncp:]

        def g(*diff):
            return f(*diff[:nd], *row_vals[nd:], *diff[nd:])

        _, vjp = jax.vjp(g, *row_vals[:nd], *par_vals)
        grads = vjp(tuple(cr[...].astype(F32) for cr in cot_refs))
        for (j, _), d_ref in zip(copies, drow_refs):
            val = grads[j]
            if j == 0 and has_add:
                val = val + add_ref[...]
            d_ref[...] = val.astype(d_ref.dtype)
        for j, d_ref in enumerate(dpar_refs):
            @pl.when(i == 0)
            def _(d_ref=d_ref, j=j):
                d_ref[...] = grads[nd + j]

            @pl.when(i > 0)
            def _(d_ref=d_ref, j=j):
                d_ref[...] += grads[nd + j]

    in_specs = [pl.BlockSpec((tr, w), lambda i, cb=cb: (i, cb)) for _, w, cb in rows] + [_full_spec(p) for p in params]
    in_specs += [pl.BlockSpec((tr, w), lambda i, cb=cb: (i, cb)) for _, w, cb in cots]
    ins = [a for a, _, _ in rows] + list(params) + [a for a, _, _ in cots]
    if has_add:
        in_specs.append(pl.BlockSpec((tr, rows[0][1]), lambda i: (i, 0)))
        ins.append(add)
    if after is not None:
        in_specs.append(pl.BlockSpec(memory_space=pl.ANY))
        ins.append(after)
    out_specs = [pl.BlockSpec((tr, rows[j][1]), lambda i: (i, 0)) for j, _ in copies] + [_full_spec(p) for p in params]
    out_shape = [jax.ShapeDtypeStruct((t, rows[j][1]), dt) for j, dt in copies]
    out_shape += [jax.ShapeDtypeStruct(p.shape, F32) for p in params]
    out = pl.pallas_call(
        body, name=name, grid=(t // tr,), in_specs=in_specs, out_specs=out_specs, out_shape=out_shape,
        compiler_params=_params("arbitrary"),
    )(*ins)
    return out[:ncp], out[ncp:]


def cols_fwd(name, f, cols, cparams, out_dtype, tc):
    t = cols[0][0].shape[0]
    width = cparams[0].shape[1]
    ncol = len(cols)

    def body(*refs):
        (val,) = f(*[x[...].astype(F32) for x in refs[:-1]])
        refs[-1][...] = val.astype(refs[-1].dtype)

    in_specs = [pl.BlockSpec((t, tc), lambda j, o=o: (0, o + j)) for _, o in cols]
    in_specs += [pl.BlockSpec((p.shape[0], tc), lambda j: (0, j)) for p in cparams]
    return pl.pallas_call(
        body, name=name, grid=(width // tc,), in_specs=in_specs,
        out_specs=pl.BlockSpec((t, tc), lambda j: (0, j)),
        out_shape=jax.ShapeDtypeStruct((t, width), out_dtype),
        compiler_params=_params("parallel"),
    )(*[a for a, _ in cols], *cparams)


def cols_bwd(name, f, cols, cparams, cot, tc, d_dtype, after=None):
    t = cols[0][0].shape[0]
    width = cparams[0].shape[1]
    ncol, npar = len(cols), len(cparams)
    extra = [] if after is None else [after]

    def body(*refs):
        vals = [x[...].astype(F32) for x in refs[:ncol + npar]]
        _, vjp = jax.vjp(f, *vals)
        grads = vjp((refs[ncol + npar][...].astype(F32),))
        for d_ref, gval in zip(refs[ncol + npar + 1 + len(extra):], grads):
            d_ref[...] = gval.astype(d_ref.dtype)

    in_specs = [pl.BlockSpec((t, tc), lambda j, o=o: (0, o + j)) for _, o in cols]
    in_specs += [pl.BlockSpec((p.shape[0], tc), lambda j: (0, j)) for p in cparams]
    in_specs.append(pl.BlockSpec((t, tc), lambda j: (0, j)))
    in_specs += [pl.BlockSpec(memory_space=pl.ANY)] * len(extra)
    out_specs = [pl.BlockSpec((t, tc), lambda j: (0, j)) for _ in cols]
    out_specs += [pl.BlockSpec((p.shape[0], tc), lambda j: (0, j)) for p in cparams]
    out_shape = [jax.ShapeDtypeStruct((t, width), d_dtype) for _ in cols]
    out_shape += [jax.ShapeDtypeStruct(p.shape, F32) for p in cparams]
    out = pl.pallas_call(
        body, name=name, grid=(width // tc,), in_specs=in_specs, out_specs=out_specs, out_shape=out_shape,
        compiler_params=_params("parallel"),
    )(*[a for a, _ in cols], *cparams, cot, *extra)
    return out[:ncol], out[ncol:]


def _ssd_in_specs(t):
    heads_per_group = SSM_HEADS // SSM_GROUPS
    steps_per_group = heads_per_group // SSD_HEADS_PER_STEP
    b_blk = D_MODEL // LANES
    c_blk = (D_MODEL + SSM_BC) // LANES
    return [
        pl.BlockSpec((t, SSD_X_WIDTH), lambda s: (0, s)),
        pl.BlockSpec((t, LANES), lambda s: (0, b_blk + s // steps_per_group)),
        pl.BlockSpec((t, LANES), lambda s: (0, c_blk + s // steps_per_group)),
        pl.BlockSpec((t, LANES), lambda s: (0, s)),
        pl.BlockSpec((3, LANES), lambda s: (0, s)),
    ]


def ssd_fwd(name, xa, pdt, hp):
    t = xa.shape[0]
    nc = t // CHUNK

    def body(x_ref, b_ref, c_ref, pdt_ref, hp_ref, y_ref, st_ref, s_scr):
        s_scr[...] = jnp.zeros_like(s_scr)

        def step(ci, carry):
            sl = pl.ds(pl.multiple_of(ci * CHUNK, CHUNK), CHUNK)
            sprev = s_scr[...]
            st_ref[0, ci] = sprev
            y, snew = _ssd_chunk_f(x_ref[sl, :], b_ref[sl, :], c_ref[sl, :], pdt_ref[sl, :], hp_ref[...], sprev)
            y_ref[sl, :] = y
            s_scr[...] = snew
            return carry

        lax.fori_loop(0, nc, step, 0)

    return pl.pallas_call(
        body, name=name, grid=(SSD_STEPS,), in_specs=_ssd_in_specs(t),
        out_specs=[pl.BlockSpec((t, SSD_X_WIDTH), lambda s: (0, s)),
                   pl.BlockSpec((1, nc, SSD_X_WIDTH, SSM_STATE), lambda s: (s, 0, 0, 0))],
        out_shape=[jax.ShapeDtypeStruct((t, D_MODEL), F32),
                   jax.ShapeDtypeStruct((SSD_STEPS, nc, SSD_X_WIDTH, SSM_STATE), F32)],
        scratch_shapes=[pltpu.VMEM((SSD_X_WIDTH, SSM_STATE), F32)],
        compiler_params=_params("parallel"),
    )(xa, xa, xa, pdt, hp)


def ssd_bwd(name, xa, pdt, hp, states, dy):
    t = xa.shape[0]
    nc = t // CHUNK
    steps_per_group = SSM_HEADS // SSM_GROUPS // SSD_HEADS_PER_STEP

    def body(x_ref, b_ref, c_ref, pdt_ref, hp_ref, st_ref, dy_ref, dx_ref, db_ref, dc_ref, dpdt_ref, dhp_ref, ds_scr, dhp_scr):
        first = pl.program_id(0) % steps_per_group == 0
        ds_scr[...] = jnp.zeros_like(ds_scr)
        dhp_scr[...] = jnp.zeros_like(dhp_scr)

        def step(i, carry):
            ci = nc - 1 - i
            sl = pl.ds(pl.multiple_of(ci * CHUNK, CHUNK), CHUNK)
            _, vjp = jax.vjp(_ssd_chunk_f, x_ref[sl, :], b_ref[sl, :], c_ref[sl, :], pdt_ref[sl, :], hp_ref[...], st_ref[0, ci])
            dx, db, dc, dpdt, dhp, dsprev = vjp((dy_ref[sl, :], ds_scr[...]))
            dx_ref[sl, :] = dx
            dpdt_ref[sl, :] = dpdt.astype(dpdt_ref.dtype)

            @pl.when(first)
            def _():
                db_ref[sl, :] = db
                dc_ref[sl, :] = dc

            @pl.when(jnp.logical_not(first))
            def _():
                db_ref[sl, :] += db
                dc_ref[sl, :] += dc

            ds_scr[...] = dsprev
            dhp_scr[...] += dhp
            return carry

        lax.fori_loop(0, nc, step, 0)
        dhp_ref[...] = dhp_scr[...]

    in_specs = _ssd_in_specs(t) + [
        pl.BlockSpec((1, nc, SSD_X_WIDTH, SSM_STATE), lambda s: (s, 0, 0, 0)),
        pl.BlockSpec((t, SSD_X_WIDTH), lambda s: (0, s)),
    ]
    out_specs = [
        pl.BlockSpec((t, SSD_X_WIDTH), lambda s: (0, s)),
        pl.BlockSpec((t, LANES), lambda s: (0, s // steps_per_group)),
        pl.BlockSpec((t, LANES), lambda s: (0, s // steps_per_group)),
        pl.BlockSpec((t, LANES), lambda s: (0, s)),
        pl.BlockSpec((3, LANES), lambda s: (0, s)),
    ]
    out_shape = [
        jax.ShapeDtypeStruct((t, D_MODEL), F32),
        jax.ShapeDtypeStruct((t, SSM_BC), F32),
        jax.ShapeDtypeStruct((t, SSM_BC), F32),
        jax.ShapeDtypeStruct((t, SSD_STEPS * LANES), BF16),
        jax.ShapeDtypeStruct((3, SSD_STEPS * LANES), F32),
    ]
    return pl.pallas_call(
        body, name=name, grid=(SSD_STEPS,), in_specs=in_specs, out_specs=out_specs, out_shape=out_shape,
        scratch_shapes=[pltpu.VMEM((SSD_X_WIDTH, SSM_STATE), F32), pltpu.VMEM((3, LANES), F32)],
        compiler_params=_params("arbitrary"),
    )(xa, xa, xa, pdt, hp, states, dy)


ATTN_TQ = 256
ATTN_KSTEP = 512


def _attn_extents(t):
    return [min(t, (g + 1) * ATTN_KSTEP) for g in range(-(-t // ATTN_KSTEP))]


def _attn_f(q0, qh, kn, kpe, v, cos, sin):
    p = jax.nn.softmax(_attn_scores(q0, 0, qh, kn, kpe, cos, sin), axis=-1)
    return (jnp.dot(p.astype(BF16), v.astype(BF16), preferred_element_type=F32),)


def _attn_in_specs(t):
    return [
        pl.BlockSpec((ATTN_TQ, MLA_QPAD), lambda h, qi: (qi, h)),
        pl.BlockSpec((t, MLA_NOPE), lambda h, qi: (0, h)),
        pl.BlockSpec((t, LANES), lambda h, qi: (0, 0)),
        pl.BlockSpec((t, MLA_V), lambda h, qi: (0, h)),
        pl.BlockSpec((ATTN_TQ, LANES), lambda h, qi: (qi, 0)),
        pl.BlockSpec((ATTN_TQ, LANES), lambda h, qi: (qi, 0)),
    ]


def attn_fwd(name, q, kn, kpe, v, cos, sin):
    t = q.shape[0]

    def body(q_ref, kn_ref, kpe_ref, v_ref, cos_ref, sin_ref, o_ref):
        qi = pl.program_id(1)
        for span, ext in enumerate(_attn_extents(t)):
            @pl.when(qi // (ATTN_KSTEP // ATTN_TQ) == span)
            def _(ext=ext):
                (o,) = _attn_f(qi * ATTN_TQ, q_ref[...], kn_ref[0:ext, :], kpe_ref[0:ext, :], v_ref[0:ext, :],
                               cos_ref[...], sin_ref[...])
                o_ref[...] = o.astype(o_ref.dtype)

    return pl.pallas_call(
        body, name=name, grid=(MLA_HEADS, t // ATTN_TQ), in_specs=_attn_in_specs(t),
        out_specs=pl.BlockSpec((ATTN_TQ, MLA_V), lambda h, qi: (qi, h)),
        out_shape=jax.ShapeDtypeStruct((t, MLA_HEADS * MLA_V), BF16),
        compiler_params=_params("parallel", "parallel"),
    )(q, kn, kpe, v, cos, sin)


def attn_bwd(name, q, kn, kpe, v, cos, sin, do, after=None):
    t = q.shape[0]
    extra = [] if after is None else [after]

    def body(q_ref, kn_ref, kpe_ref, v_ref, cos_ref, sin_ref, do_ref, *rest):
        dq_ref, dkn_ref, dkpe_ref, dv_ref = rest[len(extra):]
        h, qi = pl.program_id(0), pl.program_id(1)
        q0 = qi * ATTN_TQ
        cos, sin = cos_ref[...], sin_ref[...]

        @pl.when(qi == 0)
        def _():
            dkn_ref[...] = jnp.zeros_like(dkn_ref)
            dv_ref[...] = jnp.zeros_like(dv_ref)

        @pl.when(jnp.logical_and(h == 0, qi == 0))
        def _():
            dkpe_ref[...] = jnp.zeros_like(dkpe_ref)

        for span, ext in enumerate(_attn_extents(t)):
            @pl.when(qi // (ATTN_KSTEP // ATTN_TQ) == span)
            def _(ext=ext):
                def g(qh, knv, kpev, vv):
                    return _attn_f(q0, qh, knv, kpev, vv, cos, sin)

                _, vjp = jax.vjp(g, q_ref[...].astype(F32), kn_ref[0:ext, :].astype(F32), kpe_ref[0:ext, :].astype(F32),
                                 v_ref[0:ext, :].astype(F32))
                dq, dkn, dkpe, dv = vjp((do_ref[...].astype(F32),))
                dq_ref[...] = dq.astype(dq_ref.dtype)
                dkn_ref[0:ext, :] += dkn
                dkpe_ref[0:ext, :] += dkpe
                dv_ref[0:ext, :] += dv

    in_specs = _attn_in_specs(t) + [pl.BlockSpec((ATTN_TQ, MLA_V), lambda h, qi: (qi, h))]
    in_specs += [pl.BlockSpec(memory_space=pl.ANY)] * len(extra)
    out_specs = [
        pl.BlockSpec((ATTN_TQ, MLA_QPAD), lambda h, qi: (qi, h)),
        pl.BlockSpec((t, MLA_NOPE), lambda h, qi: (0, h)),
        pl.BlockSpec((t, LANES), lambda h, qi: (0, 0)),
        pl.BlockSpec((t, MLA_V), lambda h, qi: (0, h)),
    ]
    out_shape = [
        jax.ShapeDtypeStruct((t, MLA_HEADS * MLA_QPAD), BF16),
        jax.ShapeDtypeStruct((t, MLA_HEADS * MLA_NOPE), F32),
        jax.ShapeDtypeStruct((t, LANES), F32),
        jax.ShapeDtypeStruct((t, MLA_HEADS * MLA_V), F32),
    ]
    return pl.pallas_call(
        body, name=name, grid=(MLA_HEADS, t // ATTN_TQ), in_specs=in_specs, out_specs=out_specs, out_shape=out_shape,
        compiler_params=_params("arbitrary", "arbitrary"),
    )(q, kn, kpe, v, cos, sin, do, *extra)


def final_loss(name, h, nf, target, tr=256):
    t, d = h.shape

    def body(h_ref, w_ref, t_ref, loss_ref, dh_ref, dhb_ref, dw_ref):
        i = pl.program_id(0)
        tgt = t_ref[...]

        def f(hv, wv):
            err = _rms(hv, wv) - tgt
            return 0.5 * jnp.sum(jnp.mean(err * err, -1, keepdims=True), 0, keepdims=True)

        val, vjp = jax.vjp(f, h_ref[...], w_ref[...])
        dh, dw = vjp(jnp.ones((1, 1), F32))
        dh_ref[...] = dh
        dhb_ref[...] = dh.astype(dhb_ref.dtype)
        tile = jnp.broadcast_to(val, loss_ref.shape)

        @pl.when(i == 0)
        def _():
            loss_ref[...] = tile
            dw_ref[...] = dw

        @pl.when(i > 0)
        def _():
            loss_ref[...] += tile
            dw_ref[...] += dw

    row = pl.BlockSpec((tr, d), lambda i: (i, 0))
    return pl.pallas_call(
        body, name=name, grid=(t // tr,), in_specs=[row, _full_spec(nf), row],
        out_specs=[pl.BlockSpec((8, LANES), lambda i: (0, 0)), row, row, _full_spec(nf)],
        out_shape=[jax.ShapeDtypeStruct((8, LANES), F32), jax.ShapeDtypeStruct((t, d), F32), jax.ShapeDtypeStruct((t, d), BF16),
                   jax.ShapeDtypeStruct(nf.shape, F32)],
        compiler_params=_params("arbitrary"),
    )(h, nf, target)


ANY = pl.BlockSpec(memory_space=pl.ANY)
CHIP_ORDER = ((0, 0), (0, 1), (1, 0), (1, 1))


def _place():
    return lax.axis_index("x"), lax.axis_index("y"), lax.axis_index("c")


def _other_chips(x, y):
    return [(1 - x, y), (x, 1 - y), (1 - x, 1 - y)]


def _device_slot():
    x, y, c = _place()
    return 4 * x + 2 * y + c


def _row_tile(rows, cap=128):
    return next(t for t in (512, 256, 128, 64, 32, 16) if t <= cap and rows % t == 0)


def all_gather(name, shards):
    n = len(shards)

    def body(*refs):
        ins, outs = refs[:n], refs[n:2 * n]
        send_sems, recv_sems, local_sems = refs[2 * n:]
        x, y, c = _place()
        me, sibling = (x, y, c), (x, y, 1 - c)
        chips = _other_chips(x, y)

        def copy(k, j, block, to, from_input=False):
            dst = outs[k].at[4 * block[0] + 2 * block[1] + block[2]]
            return pltpu.make_async_remote_copy(
                src_ref=ins[k] if from_input else dst, dst_ref=dst,
                send_sem=send_sems.at[7 * k + j], recv_sem=recv_sems.at[7 * k + j],
                device_id=to, device_id_type=MESH)

        mine = [pltpu.make_async_copy(ins[k], outs[k].at[4 * x + 2 * y + c], local_sems.at[k]) for k in range(n)]
        for cp in mine:
            cp.start()
        first = []
        for k in range(n):
            first.append(copy(k, 0, me, sibling, True))
            first += [copy(k, 1 + j, me, (*chip, c), True) for j, chip in enumerate(chips)]
        for cp in first:
            cp.start()
        passed = []
        for j, chip in enumerate(chips):
            for k in range(n):
                copy(k, 1 + j, (*chip, c), me).wait_recv()
                fwd = copy(k, 4 + j, (*chip, c), sibling)
                fwd.start()
                passed.append(fwd)
        for k in range(n):
            copy(k, 0, sibling, me).wait_recv()
        for j, chip in enumerate(chips):
            for k in range(n):
                copy(k, 4 + j, (*chip, 1 - c), me).wait_recv()
        for cp in first + passed:
            cp.wait_send()
        for cp in mine:
            cp.wait()

    return pl.pallas_call(
        body, name=name, in_specs=[ANY] * n, out_specs=[ANY] * n,
        out_shape=[jax.ShapeDtypeStruct((N_DEV,) + s.shape, s.dtype) for s in shards],
        scratch_shapes=[pltpu.SemaphoreType.DMA((7 * n,)), pltpu.SemaphoreType.DMA((7 * n,)), pltpu.SemaphoreType.DMA((n,))],
    )(*shards)


def exchange_sibling(name, gs, after=None):
    n = len(gs)
    extra = [] if after is None else [after]

    def body(*refs):
        ins, outs = refs[:n], refs[n + len(extra):2 * n + len(extra)]
        send_sems, recv_sems = refs[2 * n + len(extra):]
        x, y, c = _place()
        copies = []
        for k in range(n):
            for q, (cx, cy) in enumerate(CHIP_ORDER):
                copies.append(pltpu.make_async_remote_copy(
                    src_ref=ins[k].at[4 * cx + 2 * cy + (1 - c)], dst_ref=outs[k].at[q],
                    send_sem=send_sems.at[4 * k + q], recv_sem=recv_sems.at[4 * k + q],
                    device_id=(x, y, 1 - c), device_id_type=MESH))
        for cp in copies:
            cp.start()
        for cp in copies:
            cp.wait()

    return pl.pallas_call(
        body, name=name, in_specs=[ANY] * (n + len(extra)), out_specs=[ANY] * n,
        out_shape=[jax.ShapeDtypeStruct((4,) + g.shape[1:], g.dtype) for g in gs],
        scratch_shapes=[pltpu.SemaphoreType.DMA((4 * n,)), pltpu.SemaphoreType.DMA((4 * n,))],
    )(*gs, *extra)


def chip_sums(name, g, recv, tr=128):
    _, r, c = g.shape

    def body(g_ref, r_ref, o_ref):
        o_ref[...] = (g_ref[...].astype(F32) + r_ref[...].astype(F32)).astype(o_ref.dtype)

    def chip(i):
        x, y, _ = _place()
        return jnp.where(i % 2 == 1, 1 - x, x), jnp.where(i >= 2, 1 - y, y)

    def g_index(i, j):
        cx, cy = chip(i)
        return 4 * cx + 2 * cy + lax.axis_index("c"), j, 0

    def recv_index(i, j):
        cx, cy = chip(i)
        return 2 * cx + cy, j, 0

    return pl.pallas_call(
        body, name=name, grid=(4, r // tr),
        in_specs=[pl.BlockSpec((1, tr, c), g_index), pl.BlockSpec((1, tr, c), recv_index)],
        out_specs=pl.BlockSpec((1, tr, c), lambda i, j: (i, j, 0)),
        out_shape=jax.ShapeDtypeStruct((4, r, c), g.dtype),
        compiler_params=_params("parallel", "parallel"),
    )(g, recv)


def sum_parts(name, parts, tr=128):
    _, r, c = parts[0][0].shape

    def body(*refs):
        total = refs[0][0].astype(F32)
        for ref in refs[1:-1]:
            total = total + ref[0].astype(F32)
        refs[-1][...] = total

    return pl.pallas_call(
        body, name=name, grid=(r // tr,),
        in_specs=[pl.BlockSpec((1, tr, c), lambda i, s=s: (s, i, 0)) for _, s in parts],
        out_specs=pl.BlockSpec((tr, c), lambda i: (i, 0)), out_shape=jax.ShapeDtypeStruct((r, c), F32),
        compiler_params=_params("parallel"),
    )(*[a for a, _ in parts])


def adamw(name, parts, w, m, v, tr=128, part=0, prev=None, after=None):
    _, r, c = parts[0][0].shape
    np_ = len(parts)
    first = part * (r // tr)

    def body(*refs):
        g = refs[0][0].astype(F32)
        for ref in refs[1:np_]:
            g = g + ref[0].astype(F32)
        w_ref, m_ref, v_ref = refs[np_:np_ + 3]
        g_out, d_out, m_out, v_out = refs[-4:]
        new_m = ADAM_B1 * m_ref[...] + (1.0 - ADAM_B1) * g
        new_v = ADAM_B2 * v_ref[...] + (1.0 - ADAM_B2) * (g * g)
        m_hat = new_m / (1.0 - ADAM_B1 ** ADAM_STEP)
        v_hat = new_v / (1.0 - ADAM_B2 ** ADAM_STEP)
        g_out[...] = g
        d_out[...] = -ADAM_LR * (m_hat / (jnp.sqrt(v_hat) + ADAM_EPS) + ADAM_WD * w_ref[...])
        m_out[...] = new_m
        v_out[...] = new_v

    tile = pl.BlockSpec((tr, c), lambda i: (first + i, 0))
    in_specs = [pl.BlockSpec((1, tr, c), lambda i, s=s: (s, i, 0)) for _, s in parts] + [tile] * 3
    ins = [a for a, _ in parts] + [w, m, v]
    aliases = {}
    if prev is not None:
        aliases = {len(ins) + k: k for k in range(4)}
        in_specs += [ANY] * 4
        ins += list(prev)
    if after is not None:
        in_specs.append(ANY)
        ins.append(after)
    return pl.pallas_call(
        body, name=name, grid=(r // tr,), in_specs=in_specs,
        out_specs=[tile] * 4, out_shape=[jax.ShapeDtypeStruct(w.shape, F32)] * 4,
        input_output_aliases=aliases, compiler_params=_params("parallel"),
    )(*ins)


HBM = pl.BlockSpec(memory_space=pltpu.HBM)
SEM = pl.BlockSpec(memory_space=pltpu.SEMAPHORE)
SIDE_EFFECT = pltpu.SideEffectType.DATAFLOW_SIDE_EFFECTING


def _split_copies(plan, src_refs, land_refs, send_sems, recv_sems):
    copies = []
    for i, (k, src_slot, land_slot, device) in enumerate(plan(*_place())):
        copies.append(pltpu.make_async_remote_copy(
            src_ref=src_refs[k] if src_slot is None else src_refs[k].at[src_slot], dst_ref=land_refs[k].at[land_slot],
            send_sem=send_sems.at[i], recv_sem=recv_sems.at[i], device_id=device, device_id_type=MESH))
    return copies


def split_start(name, srcs, land_shapes, plan, n_copies):
    n = len(srcs)
    lands = [] if land_shapes is None else [lax.empty(shape, s.dtype) for shape, s in zip(land_shapes, srcs)]
    m = len(lands)

    def body(*refs):
        src_refs = refs[:n]
        land_refs = refs[n:n + m] if m else src_refs
        send_sems, recv_sems, token = refs[n + m], refs[n + m + 1], refs[-1]
        for cp in _split_copies(plan, src_refs, land_refs, send_sems, recv_sems):
            cp.start()
        token[...] = jnp.zeros_like(token)

    ins = [pltpu.with_memory_space_constraint(a, pltpu.HBM) for a in list(srcs) + lands]
    out = pl.pallas_call(
        body, name=name,
        out_shape=(pltpu.SemaphoreType.DMA((n_copies,)), pltpu.SemaphoreType.DMA((n_copies,)),
                   *[pltpu.HBM(a.shape, a.dtype) for a in ins], jax.ShapeDtypeStruct((8, LANES), F32)),
        in_specs=[HBM] * (n + m), out_specs=(SEM, SEM, *[HBM] * (n + m), pl.BlockSpec(memory_space=pltpu.VMEM)),
        input_output_aliases={i: 2 + i for i in range(n + m)},
        compiler_params=pltpu.CompilerParams(has_side_effects=SIDE_EFFECT),
    )(*ins)
    return out[0], out[1], list(out[2:2 + n]), list(out[2 + n:2 + n + m]), out[-1]


def split_wait(name, handle, plan, after):
    send_sems, recv_sems, srcs, lands, _ = handle
    n, m = len(srcs), len(lands)
    after = list(after) if isinstance(after, (list, tuple)) else [after]

    def body(*refs):
        src_refs = refs[:n]
        land_refs = refs[n:n + m] if m else src_refs
        for cp in _split_copies(plan, src_refs, land_refs, refs[n + m], refs[n + m + 1]):
            cp.wait_send()
            cp.wait_recv()

    out = pl.pallas_call(
        body, name=name, out_shape=tuple(pltpu.HBM(a.shape, a.dtype) for a in srcs + lands),
        in_specs=[HBM] * (n + m) + [SEM, SEM] + [ANY] * len(after), out_specs=tuple([HBM] * (n + m)),
        input_output_aliases={i: i for i in range(n + m)},
        compiler_params=pltpu.CompilerParams(has_side_effects=SIDE_EFFECT),
    )(*srcs, *lands, send_sems, recv_sems, *after)
    return list(out[:n]), list(out[n:])


def gather_plan(n):
    def plan(x, y, c):
        me = 4 * x + 2 * y + c
        peers = [(x, y, 1 - c)] + [(*chip, c) for chip in _other_chips(x, y)]
        return [(k, None, me, peer) for k in range(n) for peer in peers]
    return plan


def pass_on_plan(n):
    def plan(x, y, c):
        return [(k, 4 * cx + 2 * cy + c, 4 * cx + 2 * cy + c, (x, y, 1 - c)) for k in range(n) for cx, cy in _other_chips(x, y)]
    return plan


def sibling_plan(n):
    def plan(x, y, c):
        return [(k, 4 * cx + 2 * cy + (1 - c), q, (x, y, 1 - c)) for k in range(n) for q, (cx, cy) in enumerate(CHIP_ORDER)]
    return plan


def chips_plan(n):
    def plan(x, y, c):
        return [(k, 1 + j, j, (*chip, c)) for k in range(n) for j, chip in enumerate(_other_chips(x, y))]
    return plan


ROW_TILE = 256
COL_TILE = 256


def _rms_fwd(tag, h, w, after=None):
    return rows_fwd(tag, _rms_f, [(h, D_MODEL, 0)], [w], [(D_MODEL, BF16)], ROW_TILE, after=after)[0]


def _rms_bwd(tag, h, w, dhn, dres):
    dh, (dw,) = rows_bwd(tag, _rms_f, [(h, D_MODEL, 0)], [w], [(dhn, D_MODEL, 0)], ROW_TILE, [(F32, BF16)], add=dres)
    return tuple(dh), dw


def even_fwd(tag, h, get_w, p, after=None):
    hn = _rms_fwd(tag + "_rms", h, p["nm"], after)
    w, after = get_w(hn)
    uv = matmul(tag + "_uv", hn, w["uv"], after=after)
    z = matmul(tag + "_z", hn, w["z"])
    xbc = matmul(tag + "_xbc", hn, w["xbc"])
    pdt = matmul(tag + "_dt", hn, w["dt"])
    gm = [p["lng"], p["lnb"], p["ws"], p["bst"]]
    ya = rows_fwd(tag + "_gmlp", _gmlp_f, [(uv, 2 * D_MODEL, 0)], gm, [(D_MODEL, BF16)], GM_BLOCK)[0]
    xa = cols_fwd(tag + "_conv", _conv_silu_f, [(xbc, 0)], [p["cw"], p["cb"]], F32, COL_TILE)
    y, states = ssd_fwd(tag + "_ssd", xa, pdt, p["hp"])
    yb = rows_fwd(tag + "_gate", _gate_norm_f, [(y, D_MODEL, 0), (z, D_MODEL, 0)], [p["nw"]], [(D_MODEL, BF16)], ROW_TILE)[0]
    h1 = matmul(tag + "_out_b", yb, w["out_bot"], res=matmul(tag + "_out_a", ya, w["out_top"], res=h))
    return h1, dict(h=h, hn=hn, uv=uv, z=z, xbc=xbc, pdt=pdt, xa=xa, y=y, states=states, ya=ya, yb=yb, w=w)


def even_bwd(tag, dh1, s, w, p, after=None, hook=None):
    dh1, dh1b = dh1
    dya = matmul(tag + "_dya", dh1b, w["out_top"], tb=True, after=after)
    dyb = matmul(tag + "_dyb", dh1b, w["out_bot"], tb=True, after=after)
    later = hook(dyb) if hook else None
    gw = dict(out_top=matmul(tag + "_gwa", s["ya"], dh1b, ta=True, out_dtype=BF16, after=later),
              out_bot=matmul(tag + "_gwb", s["yb"], dh1b, ta=True, out_dtype=BF16, after=later))
    (dy, dz), (dnw,) = rows_bwd(tag + "_gate_b", _gate_norm_f, [(s["y"], D_MODEL, 0), (s["z"], D_MODEL, 0)], [p["nw"]],
                                [(dyb, D_MODEL, 0)], ROW_TILE, [F32, BF16], after=later)
    dxs, dbm, dcm, dpdt, dhp = ssd_bwd(tag + "_ssd_b", s["xa"], s["pdt"], p["hp"], s["states"], dy)
    dxa = jnp.concatenate([dxs, dbm, dcm], axis=1)
    (dxbc,), (dcw, dcb) = cols_bwd(tag + "_conv_b", _conv_silu_f, [(s["xbc"], 0)], [p["cw"], p["cb"]], dxa, COL_TILE, BF16)
    gm = [p["lng"], p["lnb"], p["ws"], p["bst"]]
    (duv,), (dlng, dlnb, dws, dbst) = rows_bwd(tag + "_gmlp_b", _gmlp_f, [(s["uv"], 2 * D_MODEL, 0)], gm,
                                               [(dya, D_MODEL, 0)], GM_BLOCK, [BF16])
    dhn = None
    for key, d in (("uv", duv), ("z", dz), ("xbc", dxbc), ("dt", dpdt)):
        dhn = matmul(f"{tag}_dx_{key}", d, w[key], tb=True, res=dhn)
        gw[key] = matmul(f"{tag}_gw_{key}", s["hn"], d, ta=True, out_dtype=BF16)
    dh, dnm = _rms_bwd(tag + "_rms_b", s["h"], p["nm"], dhn, dh1)
    gp = dict(nm=dnm, lng=dlng, lnb=dlnb, ws=dws, bst=dbst, cw=dcw, cb=dcb, hp=dhp, nw=dnw)
    return dh, gw, gp


def odd_fwd(tag, h, get_w, p, cos, sin, after=None):
    hn = _rms_fwd(tag + "_rms", h, p["nm"], after)
    w, after = get_w(hn)
    proj = matmul(tag + "_in", hn, w["in"], after=after)
    cq, ckv, kpe = rows_fwd(tag + "_qkvn", _qkv_norm_f, [(proj, ODD_IN_PAD, 0), (cos, LANES, 0), (sin, LANES, 0)],
                            [p["qn"], p["kvn"]], [(MLA_RANK, BF16), (MLA_RANK, BF16), (LANES, F32)], ROW_TILE)
    q = matmul(tag + "_q", cq, w["uq"])
    kn = matmul(tag + "_kn", ckv, w["kn"], out_dtype=BF16)
    v = matmul(tag + "_v", ckv, w["v"], out_dtype=BF16)
    o = attn_fwd(tag + "_attn", q, kn, kpe, v, cos, sin)
    h1 = matmul(tag + "_o", o, w["o"], res=h)
    return h1, dict(h=h, hn=hn, proj=proj, cq=cq, ckv=ckv, kpe=kpe, q=q, kn=kn, v=v, o=o, w=w)


def odd_bwd(tag, dh1, s, w, p, cos, sin, after=None, hook=None):
    dh1, dh1b = dh1
    do = matmul(tag + "_do", dh1b, w["o"], tb=True, after=after)
    later = hook(do) if hook else None
    gw = dict(o=matmul(tag + "_gw_o", s["o"], dh1b, ta=True, out_dtype=BF16, after=later))
    dq, dkn, dkpe, dv = attn_bwd(tag + "_attn_b", s["q"], s["kn"], s["kpe"], s["v"], cos, sin, do, after=later)
    dcq = matmul(tag + "_dcq", dq, w["uq"], tb=True)
    gw["uq"] = matmul(tag + "_gw_uq", s["cq"], dq, ta=True, out_dtype=BF16)
    dckv = matmul(tag + "_dckv_v", dv, w["v"], tb=True, res=matmul(tag + "_dckv_k", dkn, w["kn"], tb=True))
    gw["kn"] = matmul(tag + "_gw_kn", s["ckv"], dkn, ta=True, out_dtype=BF16)
    gw["v"] = matmul(tag + "_gw_v", s["ckv"], dv, ta=True, out_dtype=BF16)
    (dproj,), (dqn, dkvn) = rows_bwd(
        tag + "_qkvn_b", _qkv_norm_f, [(s["proj"], ODD_IN_PAD, 0), (cos, LANES, 0), (sin, LANES, 0)], [p["qn"], p["kvn"]],
        [(dcq, MLA_RANK, 0), (dckv, MLA_RANK, 0), (dkpe, LANES, 0)], ROW_TILE, [BF16], n_nondiff=2)
    dhn = matmul(tag + "_dx_in", dproj, w["in"], tb=True)
    gw["in"] = matmul(tag + "_gw_in", s["hn"], dproj, ta=True, out_dtype=BF16)
    dh, dnm = _rms_bwd(tag + "_rms_b", s["h"], p["nm"], dhn, dh1)
    return dh, gw, dict(nm=dnm, qn=dqn, kvn=dkvn)


def ffn_fwd(tag, h, get_w, p, after=None):
    hn = _rms_fwd(tag + "_rms", h, p["nf"], after)
    w, after = get_w(hn)
    g = matmul(tag + "_up_g", hn, w["up_g"], after=after)
    val = matmul(tag + "_up_v", hn, w["up_v"])
    act = cols_fwd(tag + "_act", _ffn_act_f, [(g, 0), (val, 0)], [p["fcw"], p["fcb"]], BF16, COL_TILE)
    h2 = matmul(tag + "_down", act, w["down"], res=h)
    return h2, dict(h=h, hn=hn, g=g, val=val, act=act, w=w)


def ffn_bwd(tag, dh2, s, w, p, after=None, hook=None):
    dh2, dh2b = dh2
    dact = matmul(tag + "_dact", dh2b, w["down"], tb=True, after=after)
    later = hook(dact) if hook else None
    gw = dict(down=matmul(tag + "_gw_down", s["act"], dh2b, ta=True, out_dtype=BF16, after=later))
    (dg, dval), (dfcw, dfcb) = cols_bwd(tag + "_act_b", _ffn_act_f, [(s["g"], 0), (s["val"], 0)], [p["fcw"], p["fcb"]],
                                        dact, COL_TILE, BF16, after=later)
    dhn = matmul(tag + "_dx_v", dval, w["up_v"], tb=True, res=matmul(tag + "_dx_g", dg, w["up_g"], tb=True))
    gw["up"] = matmul_tn_slots(tag + "_gw_up", s["hn"], [dg, dval], 2 * D_FF // N_DEV)
    dh, dnf = _rms_bwd(tag + "_rms_b", s["h"], p["nf"], dhn, dh2)
    return dh, gw, dict(nf=dnf, fcw=dfcw, fcb=dfcb)


def _cols_from_slots(g):
    return jnp.moveaxis(g, 0, 1).reshape(g.shape[1], N_DEV * g.shape[2])


def _slots_from_cols(wmat):
    k, n = wmat.shape
    return jnp.moveaxis(wmat.reshape(k, N_DEV, n // N_DEV), 1, 0)


def _pad_last(a, width):
    return jnp.pad(a, [(0, 0)] * (a.ndim - 1) + [(0, width - a.shape[-1])])


def _heads_to_lanes(a):
    lead = a.shape[:-1]
    return _pad_last(a.reshape(lead + (SSD_STEPS, SSD_HEADS_PER_STEP)), LANES).reshape(lead + (SSD_STEPS * LANES,))


def _lanes_to_heads(a):
    lead = a.shape[:-1]
    return a.reshape(lead + (SSD_STEPS, LANES))[..., :SSD_HEADS_PER_STEP].reshape(lead + (SSM_HEADS,))


def prep_even(g_in, g_out):
    wn = _cols_from_slots(g_in)
    o1, o2, o3 = 2 * D_MODEL, 3 * D_MODEL, 3 * D_MODEL + SSM_CONV_DIM
    out = g_out.reshape(2 * D_MODEL, D_MODEL)
    return dict(uv=wn[:, :o1], z=wn[:, o1:o2], xbc=wn[:, o2:o3], dt=_heads_to_lanes(wn[:, o3:]),
                out_top=out[:D_MODEL], out_bot=out[D_MODEL:])


def unprep_even(gw):
    wn = jnp.concatenate([gw["uv"], gw["z"], gw["xbc"], _lanes_to_heads(gw["dt"])], axis=1)
    return _slots_from_cols(wn), jnp.concatenate([gw["out_top"], gw["out_bot"]], axis=0).reshape(N_DEV, -1, D_MODEL)


def prep_odd(g_in, g_uq, g_ukv, g_o):
    uq = _cols_from_slots(g_uq).reshape(MLA_RANK, MLA_HEADS, MLA_QK)
    ukv = _cols_from_slots(g_ukv).reshape(MLA_RANK, MLA_HEADS, MLA_NOPE + MLA_V)
    return dict(**{"in": _pad_last(g_in.reshape(D_MODEL, ODD_IN), ODD_IN_PAD)},
                uq=_pad_last(uq, MLA_QPAD).reshape(MLA_RANK, MLA_HEADS * MLA_QPAD),
                kn=ukv[:, :, :MLA_NOPE].reshape(MLA_RANK, MLA_HEADS * MLA_NOPE),
                v=ukv[:, :, MLA_NOPE:].reshape(MLA_RANK, MLA_HEADS * MLA_V),
                o=g_o.reshape(MLA_HEADS * MLA_V, D_MODEL))


def unprep_odd(gw):
    uq = gw["uq"].reshape(MLA_RANK, MLA_HEADS, MLA_QPAD)[:, :, :MLA_QK].reshape(MLA_RANK, MLA_HEADS * MLA_QK)
    ukv = jnp.concatenate([gw["kn"].reshape(MLA_RANK, MLA_HEADS, MLA_NOPE), gw["v"].reshape(MLA_RANK, MLA_HEADS, MLA_V)], axis=2)
    return (gw["in"][:, :ODD_IN].reshape(N_DEV, -1, ODD_IN), _slots_from_cols(uq),
            _slots_from_cols(ukv.reshape(MLA_RANK, -1)), gw["o"].reshape(N_DEV, -1, D_MODEL))


def prep_ffn(g_up, g_down):
    up = _cols_from_slots(g_up)
    return dict(up_g=up[:, :D_FF], up_v=up[:, D_FF:], down=g_down.reshape(D_FF, D_MODEL))


def unprep_ffn(gw):
    return gw["up"], gw["down"].reshape(N_DEV, -1, D_MODEL)


SMALL_TILE = LANES * LANES


def _pack(arrs):
    flat = jnp.concatenate([a.reshape(-1).astype(F32) for a in arrs])
    size = -(-flat.shape[0] // SMALL_TILE) * SMALL_TILE
    return jnp.pad(flat, (0, size - flat.shape[0])).reshape(-1, LANES)


def _unpack(packed, shapes, lead=()):
    flat = packed.reshape(lead + (-1,))
    out, off = [], 0
    for shp in shapes:
        size = math.prod(shp)
        out.append(flat[..., off:off + size].reshape(lead + tuple(shp)))
        off += size
    return out


SMALL_SHARDED = {"ev_gm_ln_g": 2, "ev_gm_ln_b": 2, "ev_conv_w": 2, "od_q_norm": 1, "od_kv_norm": 1, "ff_conv_w": 2}
SMALL_REPLICATED = ["norm_mix", "norm_ffn", "norm_final", "ev_gm_ws", "ev_gm_bs", "ev_conv_b", "ev_dt_bias", "ev_a_log",
                    "ev_d_skip", "ev_ssm_norm_w", "ff_conv_b"]
MATRICES = {"ev_w_in": (2, 2048, 1156), "ev_w_out": (2, 512, 2048), "od_w_in": (2, 256, 1088), "od_w_uq": (2, 512, 384),
            "od_w_ukv": (2, 512, 512), "od_w_o": (2, 256, 2048), "ff_w_up": (4, 2048, 1408), "ff_w_down": (4, 704, 2048)}
WEIGHT_ORDER = ["norm_mix", "norm_ffn", "norm_final", "ev_w_in", "ev_gm_ln_g", "ev_gm_ln_b", "ev_gm_ws", "ev_gm_bs",
                "ev_conv_w", "ev_conv_b", "ev_dt_bias", "ev_a_log", "ev_d_skip", "ev_ssm_norm_w", "ev_w_out", "od_w_in",
                "od_q_norm", "od_kv_norm", "od_w_uq", "od_w_ukv", "od_w_o", "ff_w_up", "ff_conv_w", "ff_conv_b", "ff_w_down"]


def _full_from_shards(name, gathered):
    ax = SMALL_SHARDED[name]
    moved = jnp.moveaxis(gathered, 0, ax)
    shp = moved.shape
    return moved.reshape(shp[:ax] + (shp[ax] * shp[ax + 1],) + shp[ax + 2:])


def _my_shard(name, full, dev):
    ax = SMALL_SHARDED[name]
    shp = full.shape
    split = full.reshape(shp[:ax] + (N_DEV, shp[ax] // N_DEV) + shp[ax + 1:])
    return lax.dynamic_index_in_dim(split, dev, axis=ax, keepdims=False)


def _even_small(sm, j):
    row = lambda a: a.reshape(1, -1)
    hp = jnp.stack([sm["ev_dt_bias"][j], sm["ev_a_log"][j], sm["ev_d_skip"][j]])
    return dict(nm=row(sm["norm_mix"][2 * j]), lng=row(sm["ev_gm_ln_g"][j]), lnb=row(sm["ev_gm_ln_b"][j]),
                ws=sm["ev_gm_ws"][j], bst=sm["ev_gm_bs"][j].T, cw=sm["ev_conv_w"][j], cb=row(sm["ev_conv_b"][j]),
                hp=_heads_to_lanes(hp), nw=row(sm["ev_ssm_norm_w"][j]))


def _odd_small(sm, j):
    row = lambda a: a.reshape(1, -1)
    return dict(nm=row(sm["norm_mix"][2 * j + 1]), qn=row(sm["od_q_norm"][j]), kvn=row(sm["od_kv_norm"][j]))


def _ffn_small(sm, layer):
    row = lambda a: a.reshape(1, -1)
    return dict(nf=row(sm["norm_ffn"][layer]), fcw=sm["ff_conv_w"][layer], fcb=row(sm["ff_conv_b"][layer]))


def _rope_tables(positions):
    inv_freq = ROPE_THETA ** (-jnp.arange(0, MLA_ROPE, 2, dtype=F32) / MLA_ROPE)
    ang = positions.astype(F32).reshape(-1, 1) * inv_freq
    cos, sin = jnp.cos(ang), jnp.sin(ang)
    return _pad_last(jnp.concatenate([cos, cos], axis=1), LANES), _pad_last(jnp.concatenate([-sin, sin], axis=1), LANES)


def local_step(x, positions, target, sm, fetch_weights, emit_grads):
    cos, sin = _rope_tables(positions)
    h, saved = x, []
    for layer in range(4):
        j, tag = layer // 2, f"l{layer}"
        get_w, dep = fetch_weights(2 * layer, h)
        if layer % 2 == 0:
            pm = _even_small(sm, j)
            h, sv = even_fwd(tag, h, get_w, pm, dep)
        else:
            pm = _odd_small(sm, j)
            h, sv = odd_fwd(tag, h, get_w, pm, cos, sin, dep)
        get_w, dep = fetch_weights(2 * layer + 1, h)
        pf = _ffn_small(sm, layer)
        h, sf = ffn_fwd(tag + "f", h, get_w, pf, dep)
        saved.append((pm, sv, pf, sf, sv["w"], sf["w"]))
    loss_tile, dh32, dh16, dnfinal = final_loss("final_loss", h, sm["norm_final"].reshape(1, -1), target)
    gs = {k: [None] * v.shape[0] for k, v in sm.items() if k != "norm_final"}
    gs["norm_final"] = dnfinal.reshape(-1)
    dh = (dh32, dh16)
    dep, hook = None, None
    for layer in reversed(range(4)):
        j, tag = layer // 2, f"l{layer}"
        pm, sv, pf, sf, wm, wf = saved[layer]
        dh, gwf, gpf = ffn_bwd(tag + "f", dh, sf, wf, pf, dep, hook)
        gs["norm_ffn"][layer], gs["ff_conv_w"][layer], gs["ff_conv_b"][layer] = gpf["nf"][0], gpf["fcw"], gpf["fcb"][0]
        dep, hook = emit_grads(2 * layer + 1, gwf, dh[0])
        if layer % 2 == 0:
            dh, gwm, gp = even_bwd(tag, dh, sv, wm, pm, dep, hook)
            hp = _lanes_to_heads(gp["hp"])
            gs["norm_mix"][layer] = gp["nm"][0]
            gs["ev_gm_ln_g"][j], gs["ev_gm_ln_b"][j] = gp["lng"].reshape(GM_GROUPS, -1), gp["lnb"].reshape(GM_GROUPS, -1)
            gs["ev_gm_ws"][j], gs["ev_gm_bs"][j] = gp["ws"], gp["bst"].T
            gs["ev_conv_w"][j], gs["ev_conv_b"][j] = gp["cw"], gp["cb"][0]
            gs["ev_dt_bias"][j], gs["ev_a_log"][j], gs["ev_d_skip"][j] = hp[0], hp[1], hp[2]
            gs["ev_ssm_norm_w"][j] = gp["nw"][0]
        else:
            dh, gwm, gp = odd_bwd(tag, dh, sv, wm, pm, cos, sin, dep, hook)
            gs["norm_mix"][layer] = gp["nm"][0]
            gs["od_q_norm"][j], gs["od_kv_norm"][j] = gp["qn"][0], gp["kvn"][0]
        if layer > 0:
            dep, hook = emit_grads(2 * layer, gwm, dh[0])
    gs = {k: (v if k == "norm_final" else jnp.stack(v)) for k, v in gs.items()}
    return loss_tile[0, 0], dh[0], gs, gwm


def kernel(x, positions, norm_mix, norm_ffn, norm_final, ev_w_in, ev_gm_ln_g, ev_gm_ln_b, ev_gm_ws, ev_gm_bs, ev_conv_w, ev_conv_b, ev_dt_bias, ev_a_log, ev_d_skip, ev_ssm_norm_w, ev_w_out, od_w_in, od_q_norm, od_kv_norm, od_w_uq, od_w_ukv, od_w_o, ff_w_up, ff_conv_w, ff_conv_b, ff_w_down, loss_target, m_norm_mix, m_norm_ffn, m_norm_final, m_ev_w_in, m_ev_gm_ln_g, m_ev_gm_ln_b, m_ev_gm_ws, m_ev_gm_bs, m_ev_conv_w, m_ev_conv_b, m_ev_dt_bias, m_ev_a_log, m_ev_d_skip, m_ev_ssm_norm_w, m_ev_w_out, m_od_w_in, m_od_q_norm, m_od_kv_norm, m_od_w_uq, m_od_w_ukv, m_od_w_o, m_ff_w_up, m_ff_conv_w, m_ff_conv_b, m_ff_w_down, v_norm_mix, v_norm_ffn, v_norm_final, v_ev_w_in, v_ev_gm_ln_g, v_ev_gm_ln_b, v_ev_gm_ws, v_ev_gm_bs, v_ev_conv_w, v_ev_conv_b, v_ev_dt_bias, v_ev_a_log, v_ev_d_skip, v_ev_ssm_norm_w, v_ev_w_out, v_od_w_in, v_od_q_norm, v_od_kv_norm, v_od_w_uq, v_od_w_ukv, v_od_w_o, v_ff_w_up, v_ff_conv_w, v_ff_conv_b, v_ff_w_down):
    args = dict(locals())
    wts = {n: args[n] for n in WEIGHT_ORDER}
    mom = {n: args["m_" + n] for n in WEIGHT_ORDER}
    var = {n: args["v_" + n] for n in WEIGHT_ORDER}
    dev = _device_slot()

    small_names = list(SMALL_SHARDED)
    small_shapes = [wts[n].shape for n in small_names]
    (small_all,) = all_gather("ag_small", [_pack([wts[n] for n in small_names])])
    small_full = _unpack(small_all, small_shapes, lead=(N_DEV,))
    sm = {n: _full_from_shards(n, g) for n, g in zip(small_names, small_full)}
    sm.update({n: wts[n] for n in SMALL_REPLICATED})

    def stage_matrices(stage):
        layer, is_ffn = divmod(stage, 2)
        if is_ffn:
            return [("ff_w_up", layer), ("ff_w_down", layer)]
        return [(n, layer // 2) for n in (["ev_w_in", "ev_w_out"] if layer % 2 == 0 else ["od_w_in", "od_w_uq", "od_w_ukv", "od_w_o"])]

    def stage_fns(stage):
        layer, is_ffn = divmod(stage, 2)
        if is_ffn:
            return prep_ffn, unprep_ffn
        return (prep_even, unprep_even) if layer % 2 == 0 else (prep_odd, unprep_odd)

    n_stages, ahead = 8, 2

    bf = {n: wts[n].astype(BF16) for n in MATRICES}

    def start_gather(stage, earlier=None):
        shards = [bf[n][i] for n, i in stage_matrices(stage)]
        if earlier is not None:
            shards, _ = lax.optimization_barrier((shards, earlier))
        return split_start(f"ag_s{stage}_start", shards, [(N_DEV,) + s.shape for s in shards],
                           gather_plan(len(shards)), 4 * len(shards))

    gathers = {}
    for stage in range(ahead):
        gathers[stage] = start_gather(stage, gathers[stage - 1][4] if stage else None)

    def fetch_weights(stage, h):
        n = len(stage_matrices(stage))
        shards, landed = split_wait(f"ag_s{stage}_wait", gathers.pop(stage), gather_plan(n), h)
        passing = split_start(f"ag_s{stage}_pass_start", landed, None, pass_on_plan(n), 3 * n)

        def get_w(first_result):
            g, _ = split_wait(f"ag_s{stage}_pass_wait", passing, pass_on_plan(n), first_result)
            g = [lax.dynamic_update_index_in_dim(gk, sk, dev, 0) for gk, sk in zip(g, shards)]
            started = None
            if stage + ahead < n_stages:
                gathers[stage + ahead] = start_gather(stage + ahead, g[0])
                started = gathers[stage + ahead][4]
            return stage_fns(stage)[0](*g), started

        return get_w, passing[4]

    scatters = []
    out = {n: None for n in MATRICES}

    held = []
    hold_below = 6

    def update(mats, sums, recv, after=None):
        for (n, i), p_, r_ in zip(mats, sums, recv):
            layers, rows, cols = MATRICES[n]
            two_d = lambda a: a.reshape(layers * rows, cols)
            out[n] = adamw(f"adamw_{n}_{i}", [(p_, 0), (r_, 0), (r_, 1), (r_, 2)], two_d(wts[n]), two_d(mom[n]),
                           two_d(var[n]), tr=_row_tile(rows, 256), part=i, prev=out[n], after=after)
            if after is not None:
                after = out[n][0]
        return after

    def finish_scatter(after):
        stage, handle = scatters.pop(0)
        mats = stage_matrices(stage)
        sums, recv = split_wait(f"rs_s{stage}_wait", handle, chips_plan(len(mats)), after)
        if 0 < stage < hold_below:
            held.append((mats, sums, recv))
        else:
            update(mats, sums, recv)

    def start_chips(stage, send, from_sibling):
        sums = [chip_sums(f"rs_s{stage}_add{k}", g, r, tr=_row_tile(g.shape[1], 512)) for k, (g, r) in enumerate(zip(send, from_sibling))]
        handle = split_start(f"rs_s{stage}_start", sums, [(3,) + s.shape[1:] for s in sums], chips_plan(len(sums)), 3 * len(sums))
        scatters.append((stage, handle))
        return handle[4]

    def emit_grads(stage, gw, dh):
        if len(scatters) >= ahead:
            finish_scatter(dh)
        send = list(stage_fns(stage)[1](gw))
        plan = sibling_plan(len(send))
        handle = split_start(f"rs_s{stage}_sib_start", send, [(4,) + g.shape[1:] for g in send], plan, 4 * len(send))

        def hook(first_result):
            sent, from_sibling = split_wait(f"rs_s{stage}_sib_wait", handle, plan, first_result)
            return start_chips(stage, sent, from_sibling)

        return handle[4], hook

    def emit_last(stage, gw, after):
        finish_scatter(after)
        send = list(stage_fns(stage)[1](gw))
        return start_chips(stage, send, exchange_sibling(f"rs_s{stage}_sibling", send, after))

    loss_local, dx, gs, gw_first = local_step(x[0], positions[0], loss_target[0], sm, fetch_weights, emit_grads)
    loss = lax.psum(loss_local, ("x", "y", "c"))

    all_small = small_names + SMALL_REPLICATED
    (partials,) = all_gather("ar_small", [_pack([gs[n] for n in all_small])])
    last_started = emit_last(0, gw_first, partials)
    total = sum_parts("ar_small_sum", [(partials, s) for s in range(N_DEV)])
    g_full = dict(zip(all_small, _unpack(total, [gs[n].shape for n in all_small])))
    g_mine = {n: (_my_shard(n, g_full[n], dev) if n in SMALL_SHARDED else g_full[n]) for n in all_small}
    packed = [_pack([d[n] for n in all_small]) for d in (g_mine, wts, mom, var)]
    res = adamw("adamw_small", [(packed[0][None], 0)], packed[1], packed[2], packed[3], after=last_started)
    unpacked = [_unpack(a, [wts[n].shape for n in all_small]) for a in res]
    for i, n in enumerate(all_small):
        out[n] = [u[i] for u in unpacked]
    while len(scatters) > 1:
        finish_scatter(res[0])
    follow = res[0]
    for job in held:
        follow = update(*job, after=follow)
    finish_scatter(follow)
    for n in MATRICES:
        out[n] = [a.reshape(wts[n].shape) for a in out[n]]

    return (loss, dx[None], *[out[n][0] for n in WEIGHT_ORDER], *[out[n][1] for n in WEIGHT_ORDER],
            *[out[n][2] for n in WEIGHT_ORDER], *[out[n][3] for n in WEIGHT_ORDER])
```

```python
import functools
import math

import jax
import jax.numpy as jnp
from jax import lax
from jax.experimental import pallas as pl
from jax.experimental.pallas import tpu as pltpu

F32 = jnp.float32
BF16 = jnp.bfloat16
MESH = pl.DeviceIdType.MESH

V7X_VMEM_LIMIT_BYTES = 56 * 1024 * 1024
LANES = 128

EPS = 1e-6
D_MODEL = 2048
CHUNK = 64
GM_BLOCK = 128
GM_GROUPS = 8
GM_GROUP_DIM = D_MODEL // GM_GROUPS
SSM_HEADS = 32
SSM_HEAD_DIM = 64
SSM_GROUPS = 4
SSM_STATE = 128
SSM_BC = SSM_GROUPS * SSM_STATE
SSM_CONV_DIM = D_MODEL + 2 * SSM_BC
SSD_HEADS_PER_STEP = 4
SSD_STEPS = SSM_HEADS // SSD_HEADS_PER_STEP
SSD_X_WIDTH = SSD_HEADS_PER_STEP * SSM_HEAD_DIM
MLA_HEADS = 16
MLA_RANK = 512
MLA_NOPE = 128
MLA_ROPE = 64
MLA_V = 128
MLA_QK = MLA_NOPE + MLA_ROPE
MLA_QPAD = 2 * LANES
ODD_IN = 2 * MLA_RANK + MLA_ROPE
ODD_IN_PAD = 2 * MLA_RANK + LANES
D_FF = 5632
ROPE_THETA = 10000.0
N_DEV = 8

ADAM_LR, ADAM_B1, ADAM_B2, ADAM_EPS, ADAM_WD, ADAM_STEP = 0.001, 0.9, 0.999, 1e-08, 0.01, 10


def _params(*sem):
    return pltpu.CompilerParams(dimension_semantics=sem, vmem_limit_bytes=V7X_VMEM_LIMIT_BYTES)


def _pick(dim, target):
    if dim <= target:
        return dim
    t = (target // LANES) * LANES
    while t >= LANES:
        if dim % t == 0:
            return t
        t -= LANES
    raise ValueError(f"no tile for {dim} under {target}")


MATMUL_VMEM_BUDGET = 32 * 1024 * 1024


def _matmul_tiles(m, n, k, a_bytes, b_bytes, o_bytes, has_res, ta):
    def fits(tm, tn):
        per_out = o_bytes + (4 if has_res else 0)
        return 2 * (tm * k * a_bytes + tn * k * b_bytes + tm * tn * per_out) <= MATMUL_VMEM_BUDGET

    tns = (2048, 1024, 512, 256, 128) if ta else (512, 256, 128)
    tms = (512, 256, 128) if ta else (2048, 1024, 512, 256, 128)
    for tn in tns:
        tn = _pick(n, tn)
        for tm in tms:
            tm = _pick(m, tm)
            if fits(tm, tn):
                return tm, tn
    raise ValueError(f"no matmul tiles for {m}x{n}x{k}")


def matmul(name, a, b, *, ta=False, tb=False, res=None, out_dtype=F32, after=None):
    m, k = (a.shape[1], a.shape[0]) if ta else a.shape
    n = b.shape[0] if tb else b.shape[1]
    assert k == (b.shape[1] if tb else b.shape[0]), (name, a.shape, b.shape)
    tm, tn = _matmul_tiles(m, n, k, a.dtype.itemsize, b.dtype.itemsize, jnp.dtype(out_dtype).itemsize, res is not None, ta)
    dims = (((0 if ta else 1,), (1 if tb else 0,)), ((), ()))

    def body(*refs):
        a_ref, b_ref, o_ref = refs[0], refs[1], refs[-1]
        total = lax.dot_general(a_ref[...].astype(BF16), b_ref[...].astype(BF16), dims, preferred_element_type=F32)
        if res is not None:
            total = total + refs[2][...]
        o_ref[...] = total.astype(o_ref.dtype)

    a_spec = pl.BlockSpec((k, tm), lambda i, j: (0, i)) if ta else pl.BlockSpec((tm, k), lambda i, j: (i, 0))
    b_spec = pl.BlockSpec((tn, k), lambda i, j: (j, 0)) if tb else pl.BlockSpec((k, tn), lambda i, j: (0, j))
    o_spec = pl.BlockSpec((tm, tn), lambda i, j: (i, j))
    ins, specs = [a, b], [a_spec, b_spec]
    if res is not None:
        ins.append(res)
        specs.append(o_spec)
    if after is not None:
        ins.append(after)
        specs.append(pl.BlockSpec(memory_space=pl.ANY))
    return pl.pallas_call(
        body, name=name, grid=(m // tm, n // tn), in_specs=specs, out_specs=o_spec,
        out_shape=jax.ShapeDtypeStruct((m, n), out_dtype),
        compiler_params=_params("parallel", "parallel"),
    )(*ins)


def matmul_tn_slots(name, a, bs, width, tm=512):
    k, m = a.shape
    counts = [b.shape[1] // width for b in bs]
    firsts = [sum(counts[:i]) for i in range(len(bs))]
    tm = _pick(m, tm)
    tn_dims = (((0,), (0,)), ((), ()))

    def body(*refs):
        a_ref, o_ref = refs[0], refs[-1]
        j = pl.program_id(1)
        for b_ref, first, count in zip(refs[1:-1], firsts, counts):
            @pl.when(jnp.logical_and(j >= first, j < first + count))
            def _(b_ref=b_ref):
                o_ref[0] = lax.dot_general(a_ref[...].astype(BF16), b_ref[...].astype(BF16), tn_dims,
                                           preferred_element_type=F32).astype(o_ref.dtype)

    specs = [pl.BlockSpec((k, tm), lambda i, j: (0, i))]
    specs += [pl.BlockSpec((k, width), lambda i, j, first=first, count=count: (0, jnp.clip(j - first, 0, count - 1)))
              for first, count in zip(firsts, counts)]
    return pl.pallas_call(
        body, name=name, grid=(m // tm, sum(counts)), in_specs=specs,
        out_specs=pl.BlockSpec((1, tm, width), lambda i, j: (j, i, 0)),
        out_shape=jax.ShapeDtypeStruct((sum(counts), m, width), BF16),
        compiler_params=_params("parallel", "arbitrary"),
    )(a, *bs)


@functools.partial(jax.custom_vjp, nondiff_argnums=(1, 2))
def _roll(x, shift, axis):
    return pltpu.roll(x, shift, axis)


def _roll_fwd(x, shift, axis):
    return pltpu.roll(x, shift, axis), None


def _roll_bwd(shift, axis, _, g):
    return (pltpu.roll(g, (g.shape[axis] - shift) % g.shape[axis], axis),)


_roll.defvjp(_roll_fwd, _roll_bwd)


def _shift_down(x, s):
    rows = lax.broadcasted_iota(jnp.int32, x.shape, 0)
    return jnp.where(rows >= s, _roll(x, s, 0), 0.0)


def _dwconv(x, w, b):
    taps = w.shape[0]
    y = b + w[taps - 1:taps, :] * x
    for kk in range(taps - 1):
        y = y + w[kk:kk + 1, :] * _shift_down(x, taps - 1 - kk)
    return y


def _rms(x, w):
    return x * lax.rsqrt(jnp.mean(x * x, -1, keepdims=True) + EPS) * w


def _rms_f(h, w):
    return (_rms(h, w),)


def _gmlp_f(uv, lng, lnb, ws, bst):
    r = lax.broadcasted_iota(jnp.int32, (GM_BLOCK, GM_BLOCK), 0) // CHUNK
    c = lax.broadcasted_iota(jnp.int32, (GM_BLOCK, GM_BLOCK), 1) // CHUNK
    outs = []
    for g in range(GM_GROUPS):
        lo, hi = g * GM_GROUP_DIM, (g + 1) * GM_GROUP_DIM
        gu = jax.nn.gelu(uv[:, lo:hi])
        gv = jax.nn.gelu(uv[:, D_MODEL + lo:D_MODEL + hi])
        xc = gv - jnp.mean(gv, -1, keepdims=True)
        var = jnp.mean(xc * xc, -1, keepdims=True)
        vn = xc * lax.rsqrt(var + EPS) * lng[:, lo:hi] + lnb[:, lo:hi]
        wm = jnp.where(r >= c, ws[g], 0.0).astype(BF16)
        gate = jnp.dot(wm, vn.astype(BF16), preferred_element_type=F32) + bst[:, g:g + 1]
        outs.append(gu * gate)
    return (jnp.concatenate(outs, axis=1),)


def _conv_silu_f(x, w, b):
    return (jax.nn.silu(_dwconv(x, w, b)),)


def _ffn_act_f(g, val, w, b):
    return (jax.nn.gelu(_dwconv(g, w, b)) * val,)


def _gate_norm_f(y, z, nw):
    y2 = y * jax.nn.silu(z)
    width = D_MODEL // SSM_GROUPS
    outs = []
    for g in range(SSM_GROUPS):
        blk = y2[:, g * width:(g + 1) * width]
        outs.append(blk * lax.rsqrt(jnp.mean(blk * blk, -1, keepdims=True) + EPS))
    return (jnp.concatenate(outs, axis=1) * nw,)


def _rope(x, cos, sin):
    lane = lax.broadcasted_iota(jnp.int32, x.shape, 1)
    half = MLA_ROPE // 2
    swapped = jnp.where(lane < half, _roll(x, LANES - half, 1), _roll(x, half, 1))
    return x * cos + swapped * sin


def _qkv_norm_f(proj, cos, sin, qn, kvn):
    cq = _rms(proj[:, :MLA_RANK], qn)
    ckv = _rms(proj[:, MLA_RANK:2 * MLA_RANK], kvn)
    kpe = _rope(proj[:, 2 * MLA_RANK:], cos, sin)
    return cq, ckv, kpe


def _attn_scores(q0, k0, qh, kn, kpe, cos, sin):
    qn = qh[:, :MLA_NOPE]
    qp = _rope(qh[:, MLA_NOPE:], cos, sin)
    nt = (((1,), (1,)), ((), ()))
    s = lax.dot_general(qn.astype(BF16), kn.astype(BF16), nt, preferred_element_type=F32)
    s = s + lax.dot_general(qp.astype(BF16), kpe.astype(BF16), nt, preferred_element_type=F32)
    s = s * (MLA_QK ** -0.5)
    visible_below = ((q0 + lax.broadcasted_iota(jnp.int32, (s.shape[0], 1), 0)) // CHUNK + 1) * CHUNK - k0
    return jnp.where(lax.broadcasted_iota(jnp.int32, s.shape, 1) < visible_below, s, -jnp.inf)


def _ssd_chunk_f(x, bm, cm, pdt, hp, sprev):
    nh, hd = SSD_HEADS_PER_STEP, SSM_HEAD_DIM
    dt = jax.nn.softplus(pdt + hp[0:1, :])
    cs = dt * (-jnp.exp(hp[1:2, :]))
    shift = 1
    while shift < CHUNK:
        cs = cs + _shift_down(cs, shift)
        shift *= 2
    cst = cs.T
    tot = cs[CHUNK - 1:CHUNK, :]

    def lanes(vals):
        return jnp.concatenate([jnp.broadcast_to(vals[:, e:e + 1], (vals.shape[0], hd)) for e in range(nh)], axis=1)

    r = lax.broadcasted_iota(jnp.int32, (CHUNK, CHUNK), 0)
    c = lax.broadcasted_iota(jnp.int32, (CHUNK, CHUNK), 1)
    tril = r >= c
    nt = (((1,), (1,)), ((), ()))
    tn = (((0,), (0,)), ((), ()))
    xd = x * lanes(dt)
    cb = lax.dot_general(cm.astype(BF16), bm.astype(BF16), nt, preferred_element_type=F32)
    ys = []
    for e in range(nh):
        decay = jnp.exp(jnp.where(tril, cs[:, e:e + 1] - cst[e:e + 1, :], -jnp.inf))
        ys.append(jnp.dot((cb * decay).astype(BF16), xd[:, e * hd:(e + 1) * hd].astype(BF16), preferred_element_type=F32))
    st = lax.dot_general((xd * lanes(jnp.exp(tot - cs))).astype(BF16), bm.astype(BF16), tn, preferred_element_type=F32)
    yoff = lax.dot_general(cm.astype(BF16), sprev.astype(BF16), nt, preferred_element_type=F32)
    y = jnp.concatenate(ys, axis=1) + yoff * lanes(jnp.exp(cs)) + lanes(hp[2:3, :]) * x
    carry = jnp.concatenate([jnp.broadcast_to(jnp.exp(tot[:, e:e + 1]), (hd, 1)) for e in range(nh)], axis=0)
    return y, carry * sprev + st


def _full_spec(a):
    nd = a.ndim
    return pl.BlockSpec(a.shape, lambda i, nd=nd: (0,) * nd)


def rows_fwd(name, f, rows, params, outs, tr, after=None):
    t = rows[0][0].shape[0]
    nr, npar = len(rows), len(params)
    extra = [] if after is None else [after]

    def body(*refs):
        vals = f(*[x[...].astype(F32) for x in refs[:nr + npar]])
        for o_ref, val in zip(refs[nr + npar + len(extra):], vals):
            o_ref[...] = val.astype(o_ref.dtype)

    in_specs = [pl.BlockSpec((tr, w), lambda i, cb=cb: (i, cb)) for _, w, cb in rows] + [_full_spec(p) for p in params]
    in_specs += [pl.BlockSpec(memory_space=pl.ANY)] * len(extra)
    out = pl.pallas_call(
        body, name=name, grid=(t // tr,), in_specs=in_specs,
        out_specs=[pl.BlockSpec((tr, w), lambda i: (i, 0)) for w, _ in outs],
        out_shape=[jax.ShapeDtypeStruct((t, w), dt) for w, dt in outs],
        compiler_params=_params("parallel"),
    )(*[a for a, _, _ in rows], *params, *extra)
    return out


def rows_bwd(name, f, rows, params, cots, tr, d_dtypes, n_nondiff=0, add=None, after=None):
    t = rows[0][0].shape[0]
    nr, npar, nc = len(rows), len(params), len(cots)
    nd = nr - n_nondiff
    has_add = add is not None
    copies = [(j, dt) for j in range(nd) for dt in (d_dtypes[j] if isinstance(d_dtypes[j], tuple) else (d_dtypes[j],))]
    ncp = len(copies)

    def body(*refs):
        i = pl.program_id(0)
        row_vals = [x[...].astype(F32) for x in refs[:nr]]
        par_vals = [x[...].astype(F32) for x in refs[nr:nr + npar]]
        cot_refs = refs[nr + npar:nr + npar + nc]
        pos = nr + npar + nc
        add_ref = refs[pos] if has_add else None
        pos += int(has_add) + int(after is not None)
        drow_refs = refs[pos:pos + ncp]
        dpar_refs = refs[pos + ncp:]

        def g(*diff):
            return f(*diff[:nd], *row_vals[nd:], *diff[nd:])

        _, vjp = jax.vjp(g, *row_vals[:nd], *par_vals)
        grads = vjp(tuple(cr[...].astype(F32) for cr in cot_refs))
        for (j, _), d_ref in zip(copies, drow_refs):
            val = grads[j]
            if j == 0 and has_add:
                val = val + add_ref[...]
            d_ref[...] = val.astype(d_ref.dtype)
        for j, d_ref in enumerate(dpar_refs):
            @pl.when(i == 0)
            def _(d_ref=d_ref, j=j):
                d_ref[...] = grads[nd + j]

            @pl.when(i > 0)
            def _(d_ref=d_ref, j=j):
                d_ref[...] += grads[nd + j]

    in_specs = [pl.BlockSpec((tr, w), lambda i, cb=cb: (i, cb)) for _, w, cb in rows] + [_full_spec(p) for p in params]
    in_specs += [pl.BlockSpec((tr, w), lambda i, cb=cb: (i, cb)) for _, w, cb in cots]
    ins = [a for a, _, _ in rows] + list(params) + [a for a, _, _ in cots]
    if has_add:
        in_specs.append(pl.BlockSpec((tr, rows[0][1]), lambda i: (i, 0)))
        ins.append(add)
    if after is not None:
        in_specs.append(pl.BlockSpec(memory_space=pl.ANY))
        ins.append(after)
    out_specs = [pl.BlockSpec((tr, rows[j][1]), lambda i: (i, 0)) for j, _ in copies] + [_full_spec(p) for p in params]
    out_shape = [jax.ShapeDtypeStruct((t, rows[j][1]), dt) for j, dt in copies]
    out_shape += [jax.ShapeDtypeStruct(p.shape, F32) for p in params]
    out = pl.pallas_call(
        body, name=name, grid=(t // tr,), in_specs=in_specs, out_specs=out_specs, out_shape=out_shape,
        compiler_params=_params("arbitrary"),
    )(*ins)
    return out[:ncp], out[ncp:]


def cols_fwd(name, f, cols, cparams, out_dtype, tc):
    t = cols[0][0].shape[0]
    width = cparams[0].shape[1]
    ncol = len(cols)

    def body(*refs):
        (val,) = f(*[x[...].astype(F32) for x in refs[:-1]])
        refs[-1][...] = val.astype(refs[-1].dtype)

    in_specs = [pl.BlockSpec((t, tc), lambda j, o=o: (0, o + j)) for _, o in cols]
    in_specs += [pl.BlockSpec((p.shape[0], tc), lambda j: (0, j)) for p in cparams]
    return pl.pallas_call(
        body, name=name, grid=(width // tc,), in_specs=in_specs,
        out_specs=pl.BlockSpec((t, tc), lambda j: (0, j)),
        out_shape=jax.ShapeDtypeStruct((t, width), out_dtype),
        compiler_params=_params("parallel"),
    )(*[a for a, _ in cols], *cparams)


def cols_bwd(name, f, cols, cparams, cot, tc, d_dtype, after=None):
    t = cols[0][0].shape[0]
    width = cparams[0].shape[1]
    ncol, npar = len(cols), len(cparams)
    extra = [] if after is None else [after]

    def body(*refs):
        vals = [x[...].astype(F32) for x in refs[:ncol + npar]]
        _, vjp = jax.vjp(f, *vals)
        grads = vjp((refs[ncol + npar][...].astype(F32),))
        for d_ref, gval in zip(refs[ncol + npar + 1 + len(extra):], grads):
            d_ref[...] = gval.astype(d_ref.dtype)

    in_specs = [pl.BlockSpec((t, tc), lambda j, o=o: (0, o + j)) for _, o in cols]
    in_specs += [pl.BlockSpec((p.shape[0], tc), lambda j: (0, j)) for p in cparams]
    in_specs.append(pl.BlockSpec((t, tc), lambda j: (0, j)))
    in_specs += [pl.BlockSpec(memory_space=pl.ANY)] * len(extra)
    out_specs = [pl.BlockSpec((t, tc), lambda j: (0, j)) for _ in cols]
    out_specs += [pl.BlockSpec((p.shape[0], tc), lambda j: (0, j)) for p in cparams]
    out_shape = [jax.ShapeDtypeStruct((t, width), d_dtype) for _ in cols]
    out_shape += [jax.ShapeDtypeStruct(p.shape, F32) for p in cparams]
    out = pl.pallas_call(
        body, name=name, grid=(width // tc,), in_specs=in_specs, out_specs=out_specs, out_shape=out_shape,
        compiler_params=_params("parallel"),
    )(*[a for a, _ in cols], *cparams, cot, *extra)
    return out[:ncol], out[ncol:]


def _ssd_in_specs(t):
    heads_per_group = SSM_HEADS // SSM_GROUPS
    steps_per_group = heads_per_group // SSD_HEADS_PER_STEP
    b_blk = D_MODEL // LANES
    c_blk = (D_MODEL + SSM_BC) // LANES
    return [
        pl.BlockSpec((t, SSD_X_WIDTH), lambda s: (0, s)),
        pl.BlockSpec((t, LANES), lambda s: (0, b_blk + s // steps_per_group)),
        pl.BlockSpec((t, LANES), lambda s: (0, c_blk + s // steps_per_group)),
        pl.BlockSpec((t, LANES), lambda s: (0, s)),
        pl.BlockSpec((3, LANES), lambda s: (0, s)),
    ]


def ssd_fwd(name, xa, pdt, hp):
    t = xa.shape[0]
    nc = t // CHUNK

    def body(x_ref, b_ref, c_ref, pdt_ref, hp_ref, y_ref, st_ref, s_scr):
        s_scr[...] = jnp.zeros_like(s_scr)

        def step(ci, carry):
            sl = pl.ds(pl.multiple_of(ci * CHUNK, CHUNK), CHUNK)
            sprev = s_scr[...]
            st_ref[0, ci] = sprev
            y, snew = _ssd_chunk_f(x_ref[sl, :], b_ref[sl, :], c_ref[sl, :], pdt_ref[sl, :], hp_ref[...], sprev)
            y_ref[sl, :] = y
            s_scr[...] = snew
            return carry

        lax.fori_loop(0, nc, step, 0)

    return pl.pallas_call(
        body, name=name, grid=(SSD_STEPS,), in_specs=_ssd_in_specs(t),
        out_specs=[pl.BlockSpec((t, SSD_X_WIDTH), lambda s: (0, s)),
                   pl.BlockSpec((1, nc, SSD_X_WIDTH, SSM_STATE), lambda s: (s, 0, 0, 0))],
        out_shape=[jax.ShapeDtypeStruct((t, D_MODEL), F32),
                   jax.ShapeDtypeStruct((SSD_STEPS, nc, SSD_X_WIDTH, SSM_STATE), F32)],
        scratch_shapes=[pltpu.VMEM((SSD_X_WIDTH, SSM_STATE), F32)],
        compiler_params=_params("parallel"),
    )(xa, xa, xa, pdt, hp)


def ssd_bwd(name, xa, pdt, hp, states, dy):
    t = xa.shape[0]
    nc = t // CHUNK
    steps_per_group = SSM_HEADS // SSM_GROUPS // SSD_HEADS_PER_STEP

    def body(x_ref, b_ref, c_ref, pdt_ref, hp_ref, st_ref, dy_ref, dx_ref, db_ref, dc_ref, dpdt_ref, dhp_ref, ds_scr, dhp_scr):
        first = pl.program_id(0) % steps_per_group == 0
        ds_scr[...] = jnp.zeros_like(ds_scr)
        dhp_scr[...] = jnp.zeros_like(dhp_scr)

        def step(i, carry):
            ci = nc - 1 - i
            sl = pl.ds(pl.multiple_of(ci * CHUNK, CHUNK), CHUNK)
            _, vjp = jax.vjp(_ssd_chunk_f, x_ref[sl, :], b_ref[sl, :], c_ref[sl, :], pdt_ref[sl, :], hp_ref[...], st_ref[0, ci])
            dx, db, dc, dpdt, dhp, dsprev = vjp((dy_ref[sl, :], ds_scr[...]))
            dx_ref[sl, :] = dx
            dpdt_ref[sl, :] = dpdt.astype(dpdt_ref.dtype)

            @pl.when(first)
            def _():
                db_ref[sl, :] = db
                dc_ref[sl, :] = dc

            @pl.when(jnp.logical_not(first))
            def _():
                db_ref[sl, :] += db
                dc_ref[sl, :] += dc

            ds_scr[...] = dsprev
            dhp_scr[...] += dhp
            return carry

        lax.fori_loop(0, nc, step, 0)
        dhp_ref[...] = dhp_scr[...]

    in_specs = _ssd_in_specs(t) + [
        pl.BlockSpec((1, nc, SSD_X_WIDTH, SSM_STATE), lambda s: (s, 0, 0, 0)),
        pl.BlockSpec((t, SSD_X_WIDTH), lambda s: (0, s)),
    ]
    out_specs = [
        pl.BlockSpec((t, SSD_X_WIDTH), lambda s: (0, s)),
        pl.BlockSpec((t, LANES), lambda s: (0, s // steps_per_group)),
        pl.BlockSpec((t, LANES), lambda s: (0, s // steps_per_group)),
        pl.BlockSpec((t, LANES), lambda s: (0, s)),
        pl.BlockSpec((3, LANES), lambda s: (0, s)),
    ]
    out_shape = [
        jax.ShapeDtypeStruct((t, D_MODEL), F32),
        jax.ShapeDtypeStruct((t, SSM_BC), F32),
        jax.ShapeDtypeStruct((t, SSM_BC), F32),
        jax.ShapeDtypeStruct((t, SSD_STEPS * LANES), BF16),
        jax.ShapeDtypeStruct((3, SSD_STEPS * LANES), F32),
    ]
    return pl.pallas_call(
        body, name=name, grid=(SSD_STEPS,), in_specs=in_specs, out_specs=out_specs, out_shape=out_shape,
        scratch_shapes=[pltpu.VMEM((SSD_X_WIDTH, SSM_STATE), F32), pltpu.VMEM((3, LANES), F32)],
        compiler_params=_params("arbitrary"),
    )(xa, xa, xa, pdt, hp, states, dy)


ATTN_TQ = 512
ATTN_KSTEP = 512


def _attn_extents(t):
    return [min(t, (g + 1) * ATTN_KSTEP) for g in range(-(-t // ATTN_KSTEP))]


def _attn_f(q0, qh, kn, kpe, v, cos, sin):
    p = jax.nn.softmax(_attn_scores(q0, 0, qh, kn, kpe, cos, sin), axis=-1)
    return (jnp.dot(p.astype(BF16), v.astype(BF16), preferred_element_type=F32),)


def _attn_in_specs(t):
    return [
        pl.BlockSpec((ATTN_TQ, MLA_QPAD), lambda h, qi: (qi, h)),
        pl.BlockSpec((t, MLA_NOPE), lambda h, qi: (0, h)),
        pl.BlockSpec((t, LANES), lambda h, qi: (0, 0)),
        pl.BlockSpec((t, MLA_V), lambda h, qi: (0, h)),
        pl.BlockSpec((ATTN_TQ, LANES), lambda h, qi: (qi, 0)),
        pl.BlockSpec((ATTN_TQ, LANES), lambda h, qi: (qi, 0)),
    ]


def attn_fwd(name, q, kn, kpe, v, cos, sin):
    t = q.shape[0]

    def body(q_ref, kn_ref, kpe_ref, v_ref, cos_ref, sin_ref, o_ref):
        qi = pl.program_id(1)
        for span, ext in enumerate(_attn_extents(t)):
            @pl.when(qi // (ATTN_KSTEP // ATTN_TQ) == span)
            def _(ext=ext):
                (o,) = _attn_f(qi * ATTN_TQ, q_ref[...], kn_ref[0:ext, :], kpe_ref[0:ext, :], v_ref[0:ext, :],
                               cos_ref[...], sin_ref[...])
                o_ref[...] = o.astype(o_ref.dtype)

    return pl.pallas_call(
        body, name=name, grid=(MLA_HEADS, t // ATTN_TQ), in_specs=_attn_in_specs(t),
        out_specs=pl.BlockSpec((ATTN_TQ, MLA_V), lambda h, qi: (qi, h)),
        out_shape=jax.ShapeDtypeStruct((t, MLA_HEADS * MLA_V), BF16),
        compiler_params=_params("parallel", "parallel"),
    )(q, kn, kpe, v, cos, sin)


def attn_bwd(name, q, kn, kpe, v, cos, sin, do, after=None):
    t = q.shape[0]
    extra = [] if after is None else [after]

    def body(q_ref, kn_ref, kpe_ref, v_ref, cos_ref, sin_ref, do_ref, *rest):
        dq_ref, dkn_ref, dkpe_ref, dv_ref = rest[len(extra):]
        h, qi = pl.program_id(0), pl.program_id(1)
        q0 = qi * ATTN_TQ
        cos, sin = cos_ref[...], sin_ref[...]

        @pl.when(qi == 0)
        def _():
            dkn_ref[...] = jnp.zeros_like(dkn_ref)
            dv_ref[...] = jnp.zeros_like(dv_ref)

        @pl.when(jnp.logical_and(h == 0, qi == 0))
        def _():
            dkpe_ref[...] = jnp.zeros_like(dkpe_ref)

        for span, ext in enumerate(_attn_extents(t)):
            @pl.when(qi // (ATTN_KSTEP // ATTN_TQ) == span)
            def _(ext=ext):
                def g(qh, knv, kpev, vv):
                    return _attn_f(q0, qh, knv, kpev, vv, cos, sin)

                _, vjp = jax.vjp(g, q_ref[...].astype(F32), kn_ref[0:ext, :].astype(F32), kpe_ref[0:ext, :].astype(F32),
                                 v_ref[0:ext, :].astype(F32))
                dq, dkn, dkpe, dv = vjp((do_ref[...].astype(F32),))
                dq_ref[...] = dq.astype(dq_ref.dtype)
                dkn_ref[0:ext, :] += dkn
                dkpe_ref[0:ext, :] += dkpe
                dv_ref[0:ext, :] += dv

    in_specs = _attn_in_specs(t) + [pl.BlockSpec((ATTN_TQ, MLA_V), lambda h, qi: (qi, h))]
    in_specs += [pl.BlockSpec(memory_space=pl.ANY)] * len(extra)
    out_specs = [
        pl.BlockSpec((ATTN_TQ, MLA_QPAD), lambda h, qi: (qi, h)),
        pl.BlockSpec((t, MLA_NOPE), lambda h, qi: (0, h)),
        pl.BlockSpec((t, LANES), lambda h, qi: (0, 0)),
        pl.BlockSpec((t, MLA_V), lambda h, qi: (0, h)),
    ]
    out_shape = [
        jax.ShapeDtypeStruct((t, MLA_HEADS * MLA_QPAD), BF16),
        jax.ShapeDtypeStruct((t, MLA_HEADS * MLA_NOPE), F32),
        jax.ShapeDtypeStruct((t, LANES), F32),
        jax.ShapeDtypeStruct((t, MLA_HEADS * MLA_V), F32),
    ]
    return pl.pallas_call(
        body, name=name, grid=(MLA_HEADS, t // ATTN_TQ), in_specs=in_specs, out_specs=out_specs, out_shape=out_shape,
        compiler_params=_params("arbitrary", "arbitrary"),
    )(q, kn, kpe, v, cos, sin, do, *extra)


def final_loss(name, h, nf, target, tr=256):
    t, d = h.shape

    def body(h_ref, w_ref, t_ref, loss_ref, dh_ref, dhb_ref, dw_ref):
        i = pl.program_id(0)
        tgt = t_ref[...]

        def f(hv, wv):
            err = _rms(hv, wv) - tgt
            return 0.5 * jnp.sum(jnp.mean(err * err, -1, keepdims=True), 0, keepdims=True)

        val, vjp = jax.vjp(f, h_ref[...], w_ref[...])
        dh, dw = vjp(jnp.ones((1, 1), F32))
        dh_ref[...] = dh
        dhb_ref[...] = dh.astype(dhb_ref.dtype)
        tile = jnp.broadcast_to(val, loss_ref.shape)

        @pl.when(i == 0)
        def _():
            loss_ref[...] = tile
            dw_ref[...] = dw

        @pl.when(i > 0)
        def _():
            loss_ref[...] += tile
            dw_ref[...] += dw

    row = pl.BlockSpec((tr, d), lambda i: (i, 0))
    return pl.pallas_call(
        body, name=name, grid=(t // tr,), in_specs=[row, _full_spec(nf), row],
        out_specs=[pl.BlockSpec((8, LANES), lambda i: (0, 0)), row, row, _full_spec(nf)],
        out_shape=[jax.ShapeDtypeStruct((8, LANES), F32), jax.ShapeDtypeStruct((t, d), F32), jax.ShapeDtypeStruct((t, d), BF16),
                   jax.ShapeDtypeStruct(nf.shape, F32)],
        compiler_params=_params("arbitrary"),
    )(h, nf, target)


ANY = pl.BlockSpec(memory_space=pl.ANY)
CHIP_ORDER = ((0, 0), (0, 1), (1, 0), (1, 1))


def _place():
    return lax.axis_index("x"), lax.axis_index("y"), lax.axis_index("c")


def _other_chips(x, y):
    return [(1 - x, y), (x, 1 - y), (1 - x, 1 - y)]


def _device_slot():
    x, y, c = _place()
    return 4 * x + 2 * y + c


def _row_tile(rows, cap=128):
    return next(t for t in (512, 256, 128, 64, 32, 16) if t <= cap and rows % t == 0)


def all_gather(name, shards):
    n = len(shards)

    def body(*refs):
        ins, outs = refs[:n], refs[n:2 * n]
        send_sems, recv_sems, local_sems = refs[2 * n:]
        x, y, c = _place()
        me, sibling = (x, y, c), (x, y, 1 - c)
        chips = _other_chips(x, y)

        def copy(k, j, block, to, from_input=False):
            dst = outs[k].at[4 * block[0] + 2 * block[1] + block[2]]
            return pltpu.make_async_remote_copy(
                src_ref=ins[k] if from_input else dst, dst_ref=dst,
                send_sem=send_sems.at[7 * k + j], recv_sem=recv_sems.at[7 * k + j],
                device_id=to, device_id_type=MESH)

        mine = [pltpu.make_async_copy(ins[k], outs[k].at[4 * x + 2 * y + c], local_sems.at[k]) for k in range(n)]
        for cp in mine:
            cp.start()
        first = []
        for k in range(n):
            first.append(copy(k, 0, me, sibling, True))
            first += [copy(k, 1 + j, me, (*chip, c), True) for j, chip in enumerate(chips)]
        for cp in first:
            cp.start()
        passed = []
        for j, chip in enumerate(chips):
            for k in range(n):
                copy(k, 1 + j, (*chip, c), me).wait_recv()
                fwd = copy(k, 4 + j, (*chip, c), sibling)
                fwd.start()
                passed.append(fwd)
        for k in range(n):
            copy(k, 0, sibling, me).wait_recv()
        for j, chip in enumerate(chips):
            for k in range(n):
                copy(k, 4 + j, (*chip, 1 - c), me).wait_recv()
        for cp in first + passed:
            cp.wait_send()
        for cp in mine:
            cp.wait()

    return pl.pallas_call(
        body, name=name, in_specs=[ANY] * n, out_specs=[ANY] * n,
        out_shape=[jax.ShapeDtypeStruct((N_DEV,) + s.shape, s.dtype) for s in shards],
        scratch_shapes=[pltpu.SemaphoreType.DMA((7 * n,)), pltpu.SemaphoreType.DMA((7 * n,)), pltpu.SemaphoreType.DMA((n,))],
    )(*shards)


def exchange_sibling(name, gs, after=None):
    n = len(gs)
    extra = [] if after is None else [after]

    def body(*refs):
        ins, outs = refs[:n], refs[n + len(extra):2 * n + len(extra)]
        send_sems, recv_sems = refs[2 * n + len(extra):]
        x, y, c = _place()
        copies = []
        for k in range(n):
            for q, (cx, cy) in enumerate(CHIP_ORDER):
                copies.append(pltpu.make_async_remote_copy(
                    src_ref=ins[k].at[4 * cx + 2 * cy + (1 - c)], dst_ref=outs[k].at[q],
                    send_sem=send_sems.at[4 * k + q], recv_sem=recv_sems.at[4 * k + q],
                    device_id=(x, y, 1 - c), device_id_type=MESH))
        for cp in copies:
            cp.start()
        for cp in copies:
            cp.wait()

    return pl.pallas_call(
        body, name=name, in_specs=[ANY] * (n + len(extra)), out_specs=[ANY] * n,
        out_shape=[jax.ShapeDtypeStruct((4,) + g.shape[1:], g.dtype) for g in gs],
        scratch_shapes=[pltpu.SemaphoreType.DMA((4 * n,)), pltpu.SemaphoreType.DMA((4 * n,))],
    )(*gs, *extra)


def chip_sums(name, g, recv, tr=128):
    _, r, c = g.shape

    def body(g_ref, r_ref, o_ref):
        o_ref[...] = (g_ref[...].astype(F32) + r_ref[...].astype(F32)).astype(o_ref.dtype)

    def chip(i):
        x, y, _ = _place()
        return jnp.where(i % 2 == 1, 1 - x, x), jnp.where(i >= 2, 1 - y, y)

    def g_index(i, j):
        cx, cy = chip(i)
        return 4 * cx + 2 * cy + lax.axis_index("c"), j, 0

    def recv_index(i, j):
        cx, cy = chip(i)
        return 2 * cx + cy, j, 0

    return pl.pallas_call(
        body, name=name, grid=(4, r // tr),
        in_specs=[pl.BlockSpec((1, tr, c), g_index), pl.BlockSpec((1, tr, c), recv_index)],
        out_specs=pl.BlockSpec((1, tr, c), lambda i, j: (i, j, 0)),
        out_shape=jax.ShapeDtypeStruct((4, r, c), g.dtype),
        compiler_params=_params("parallel", "parallel"),
    )(g, recv)


def sum_parts(name, parts, tr=128):
    _, r, c = parts[0][0].shape

    def body(*refs):
        total = refs[0][0].astype(F32)
        for ref in refs[1:-1]:
            total = total + ref[0].astype(F32)
        refs[-1][...] = total

    return pl.pallas_call(
        body, name=name, grid=(r // tr,),
        in_specs=[pl.BlockSpec((1, tr, c), lambda i, s=s: (s, i, 0)) for _, s in parts],
        out_specs=pl.BlockSpec((tr, c), lambda i: (i, 0)), out_shape=jax.ShapeDtypeStruct((r, c), F32),
        compiler_params=_params("parallel"),
    )(*[a for a, _ in parts])


def adamw(name, parts, w, m, v, tr=128, part=0, prev=None, after=None):
    _, r, c = parts[0][0].shape
    np_ = len(parts)
    first = part * (r // tr)

    def body(*refs):
        g = refs[0][0].astype(F32)
        for ref in refs[1:np_]:
            g = g + ref[0].astype(F32)
        w_ref, m_ref, v_ref = refs[np_:np_ + 3]
        g_out, d_out, m_out, v_out = refs[-4:]
        new_m = ADAM_B1 * m_ref[...] + (1.0 - ADAM_B1) * g
        new_v = ADAM_B2 * v_ref[...] + (1.0 - ADAM_B2) * (g * g)
        m_hat = new_m / (1.0 - ADAM_B1 ** ADAM_STEP)
        v_hat = new_v / (1.0 - ADAM_B2 ** ADAM_STEP)
        g_out[...] = g
        d_out[...] = -ADAM_LR * (m_hat / (jnp.sqrt(v_hat) + ADAM_EPS) + ADAM_WD * w_ref[...])
        m_out[...] = new_m
        v_out[...] = new_v

    tile = pl.BlockSpec((tr, c), lambda i: (first + i, 0))
    in_specs = [pl.BlockSpec((1, tr, c), lambda i, s=s: (s, i, 0)) for _, s in parts] + [tile] * 3
    ins = [a for a, _ in parts] + [w, m, v]
    aliases = {}
    if prev is not None:
        aliases = {len(ins) + k: k for k in range(4)}
        in_specs += [ANY] * 4
        ins += list(prev)
    if after is not None:
        in_specs.append(ANY)
        ins.append(after)
    return pl.pallas_call(
        body, name=name, grid=(r // tr,), in_specs=in_specs,
        out_specs=[tile] * 4, out_shape=[jax.ShapeDtypeStruct(w.shape, F32)] * 4,
        input_output_aliases=aliases, compiler_params=_params("parallel"),
    )(*ins)


HBM = pl.BlockSpec(memory_space=pltpu.HBM)
SEM = pl.BlockSpec(memory_space=pltpu.SEMAPHORE)
SIDE_EFFECT = pltpu.SideEffectType.DATAFLOW_SIDE_EFFECTING


def _split_copies(plan, src_refs, land_refs, send_sems, recv_sems):
    copies = []
    for i, (k, src_slot, land_slot, device) in enumerate(plan(*_place())):
        copies.append(pltpu.make_async_remote_copy(
            src_ref=src_refs[k] if src_slot is None else src_refs[k].at[src_slot], dst_ref=land_refs[k].at[land_slot],
            send_sem=send_sems.at[i], recv_sem=recv_sems.at[i], device_id=device, device_id_type=MESH))
    return copies


def split_start(name, srcs, land_shapes, plan, n_copies):
    n = len(srcs)
    lands = [] if land_shapes is None else [lax.empty(shape, s.dtype) for shape, s in zip(land_shapes, srcs)]
    m = len(lands)

    def body(*refs):
        src_refs = refs[:n]
        land_refs = refs[n:n + m] if m else src_refs
        send_sems, recv_sems, token = refs[n + m], refs[n + m + 1], refs[-1]
        for cp in _split_copies(plan, src_refs, land_refs, send_sems, recv_sems):
            cp.start()
        token[...] = jnp.zeros_like(token)

    ins = [pltpu.with_memory_space_constraint(a, pltpu.HBM) for a in list(srcs) + lands]
    out = pl.pallas_call(
        body, name=name,
        out_shape=(pltpu.SemaphoreType.DMA((n_copies,)), pltpu.SemaphoreType.DMA((n_copies,)),
                   *[pltpu.HBM(a.shape, a.dtype) for a in ins], jax.ShapeDtypeStruct((8, LANES), F32)),
        in_specs=[HBM] * (n + m), out_specs=(SEM, SEM, *[HBM] * (n + m), pl.BlockSpec(memory_space=pltpu.VMEM)),
        input_output_aliases={i: 2 + i for i in range(n + m)},
        compiler_params=pltpu.CompilerParams(has_side_effects=SIDE_EFFECT),
    )(*ins)
    return out[0], out[1], list(out[2:2 + n]), list(out[2 + n:2 + n + m]), out[-1]


def split_wait(name, handle, plan, after):
    send_sems, recv_sems, srcs, lands, _ = handle
    n, m = len(srcs), len(lands)
    after = list(after) if isinstance(after, (list, tuple)) else [after]

    def body(*refs):
        src_refs = refs[:n]
        land_refs = refs[n:n + m] if m else src_refs
        for cp in _split_copies(plan, src_refs, land_refs, refs[n + m], refs[n + m + 1]):
            cp.wait_send()
            cp.wait_recv()

    out = pl.pallas_call(
        body, name=name, out_shape=tuple(pltpu.HBM(a.shape, a.dtype) for a in srcs + lands),
        in_specs=[HBM] * (n + m) + [SEM, SEM] + [ANY] * len(after), out_specs=tuple([HBM] * (n + m)),
        input_output_aliases={i: i for i in range(n + m)},
        compiler_params=pltpu.CompilerParams(has_side_effects=SIDE_EFFECT),
    )(*srcs, *lands, send_sems, recv_sems, *after)
    return list(out[:n]), list(out[n:])


def gather_plan(n):
    def plan(x, y, c):
        me = 4 * x + 2 * y + c
        peers = [(x, y, 1 - c)] + [(*chip, c) for chip in _other_chips(x, y)]
        return [(k, None, me, peer) for k in range(n) for peer in peers]
    return plan


def pass_on_plan(n):
    def plan(x, y, c):
        return [(k, 4 * cx + 2 * cy + c, 4 * cx + 2 * cy + c, (x, y, 1 - c)) for k in range(n) for cx, cy in _other_chips(x, y)]
    return plan


def sibling_plan(n):
    def plan(x, y, c):
        return [(k, 4 * cx + 2 * cy + (1 - c), q, (x, y, 1 - c)) for k in range(n) for q, (cx, cy) in enumerate(CHIP_ORDER)]
    return plan


def chips_plan(n):
    def plan(x, y, c):
        return [(k, 1 + j, j, (*chip, c)) for k in range(n) for j, chip in enumerate(_other_chips(x, y))]
    return plan


ROW_TILE = 256
COL_TILE = 256


def _rms_fwd(tag, h, w, after=None):
    return rows_fwd(tag, _rms_f, [(h, D_MODEL, 0)], [w], [(D_MODEL, BF16)], ROW_TILE, after=after)[0]


def _rms_bwd(tag, h, w, dhn, dres):
    dh, (dw,) = rows_bwd(tag, _rms_f, [(h, D_MODEL, 0)], [w], [(dhn, D_MODEL, 0)], ROW_TILE, [(F32, BF16)], add=dres)
    return tuple(dh), dw


def even_fwd(tag, h, get_w, p, after=None):
    hn = _rms_fwd(tag + "_rms", h, p["nm"], after)
    w, after = get_w(hn)
    uv = matmul(tag + "_uv", hn, w["uv"], after=after)
    z = matmul(tag + "_z", hn, w["z"])
    xbc = matmul(tag + "_xbc", hn, w["xbc"])
    pdt = matmul(tag + "_dt", hn, w["dt"])
    gm = [p["lng"], p["lnb"], p["ws"], p["bst"]]
    ya = rows_fwd(tag + "_gmlp", _gmlp_f, [(uv, 2 * D_MODEL, 0)], gm, [(D_MODEL, BF16)], GM_BLOCK)[0]
    xa = cols_fwd(tag + "_conv", _conv_silu_f, [(xbc, 0)], [p["cw"], p["cb"]], F32, COL_TILE)
    y, states = ssd_fwd(tag + "_ssd", xa, pdt, p["hp"])
    yb = rows_fwd(tag + "_gate", _gate_norm_f, [(y, D_MODEL, 0), (z, D_MODEL, 0)], [p["nw"]], [(D_MODEL, BF16)], ROW_TILE)[0]
    h1 = matmul(tag + "_out_b", yb, w["out_bot"], res=matmul(tag + "_out_a", ya, w["out_top"], res=h))
    return h1, dict(h=h, hn=hn, uv=uv, z=z, xbc=xbc, pdt=pdt, xa=xa, y=y, states=states, ya=ya, yb=yb, w=w)


def even_bwd(tag, dh1, s, w, p, after=None, hook=None):
    dh1, dh1b = dh1
    dya = matmul(tag + "_dya", dh1b, w["out_top"], tb=True, after=after)
    dyb = matmul(tag + "_dyb", dh1b, w["out_bot"], tb=True, after=after)
    later = hook(dyb) if hook else None
    gw = dict(out_top=matmul(tag + "_gwa", s["ya"], dh1b, ta=True, out_dtype=BF16, after=later),
              out_bot=matmul(tag + "_gwb", s["yb"], dh1b, ta=True, out_dtype=BF16, after=later))
    (dy, dz), (dnw,) = rows_bwd(tag + "_gate_b", _gate_norm_f, [(s["y"], D_MODEL, 0), (s["z"], D_MODEL, 0)], [p["nw"]],
                                [(dyb, D_MODEL, 0)], ROW_TILE, [F32, BF16], after=later)
    dxs, dbm, dcm, dpdt, dhp = ssd_bwd(tag + "_ssd_b", s["xa"], s["pdt"], p["hp"], s["states"], dy)
    dxa = jnp.concatenate([dxs, dbm, dcm], axis=1)
    (dxbc,), (dcw, dcb) = cols_bwd(tag + "_conv_b", _conv_silu_f, [(s["xbc"], 0)], [p["cw"], p["cb"]], dxa, COL_TILE, BF16)
    gm = [p["lng"], p["lnb"], p["ws"], p["bst"]]
    (duv,), (dlng, dlnb, dws, dbst) = rows_bwd(tag + "_gmlp_b", _gmlp_f, [(s["uv"], 2 * D_MODEL, 0)], gm,
                                               [(dya, D_MODEL, 0)], GM_BLOCK, [BF16])
    dhn = None
    for key, d in (("uv", duv), ("z", dz), ("xbc", dxbc), ("dt", dpdt)):
        dhn = matmul(f"{tag}_dx_{key}", d, w[key], tb=True, res=dhn)
        gw[key] = matmul(f"{tag}_gw_{key}", s["hn"], d, ta=True, out_dtype=BF16)
    dh, dnm = _rms_bwd(tag + "_rms_b", s["h"], p["nm"], dhn, dh1)
    gp = dict(nm=dnm, lng=dlng, lnb=dlnb, ws=dws, bst=dbst, cw=dcw, cb=dcb, hp=dhp, nw=dnw)
    return dh, gw, gp


def odd_fwd(tag, h, get_w, p, cos, sin, after=None):
    hn = _rms_fwd(tag + "_rms", h, p["nm"], after)
    w, after = get_w(hn)
    proj = matmul(tag + "_in", hn, w["in"], after=after)
    cq, ckv, kpe = rows_fwd(tag + "_qkvn", _qkv_norm_f, [(proj, ODD_IN_PAD, 0), (cos, LANES, 0), (sin, LANES, 0)],
                            [p["qn"], p["kvn"]], [(MLA_RANK, BF16), (MLA_RANK, BF16), (LANES, F32)], ROW_TILE)
    q = matmul(tag + "_q", cq, w["uq"])
    kn = matmul(tag + "_kn", ckv, w["kn"], out_dtype=BF16)
    v = matmul(tag + "_v", ckv, w["v"], out_dtype=BF16)
    o = attn_fwd(tag + "_attn", q, kn, kpe, v, cos, sin)
    h1 = matmul(tag + "_o", o, w["o"], res=h)
    return h1, dict(h=h, hn=hn, proj=proj, cq=cq, ckv=ckv, kpe=kpe, q=q, kn=kn, v=v, o=o, w=w)


def odd_bwd(tag, dh1, s, w, p, cos, sin, after=None, hook=None):
    dh1, dh1b = dh1
    do = matmul(tag + "_do", dh1b, w["o"], tb=True, after=after)
    later = hook(do) if hook else None
    gw = dict(o=matmul(tag + "_gw_o", s["o"], dh1b, ta=True, out_dtype=BF16, after=later))
    dq, dkn, dkpe, dv = attn_bwd(tag + "_attn_b", s["q"], s["kn"], s["kpe"], s["v"], cos, sin, do, after=later)
    dcq = matmul(tag + "_dcq", dq, w["uq"], tb=True)
    gw["uq"] = matmul(tag + "_gw_uq", s["cq"], dq, ta=True, out_dtype=BF16)
    dckv = matmul(tag + "_dckv_v", dv, w["v"], tb=True, res=matmul(tag + "_dckv_k", dkn, w["kn"], tb=True))
    gw["kn"] = matmul(tag + "_gw_kn", s["ckv"], dkn, ta=True, out_dtype=BF16)
    gw["v"] = matmul(tag + "_gw_v", s["ckv"], dv, ta=True, out_dtype=BF16)
    (dproj,), (dqn, dkvn) = rows_bwd(
        tag + "_qkvn_b", _qkv_norm_f, [(s["proj"], ODD_IN_PAD, 0), (cos, LANES, 0), (sin, LANES, 0)], [p["qn"], p["kvn"]],
        [(dcq, MLA_RANK, 0), (dckv, MLA_RANK, 0), (dkpe, LANES, 0)], ROW_TILE, [BF16], n_nondiff=2)
    dhn = matmul(tag + "_dx_in", dproj, w["in"], tb=True)
    gw["in"] = matmul(tag + "_gw_in", s["hn"], dproj, ta=True, out_dtype=BF16)
    dh, dnm = _rms_bwd(tag + "_rms_b", s["h"], p["nm"], dhn, dh1)
    return dh, gw, dict(nm=dnm, qn=dqn, kvn=dkvn)


def ffn_fwd(tag, h, get_w, p, after=None):
    hn = _rms_fwd(tag + "_rms", h, p["nf"], after)
    w, after = get_w(hn)
    g = matmul(tag + "_up_g", hn, w["up_g"], after=after)
    val = matmul(tag + "_up_v", hn, w["up_v"])
    act = cols_fwd(tag + "_act", _ffn_act_f, [(g, 0), (val, 0)], [p["fcw"], p["fcb"]], BF16, COL_TILE)
    h2 = matmul(tag + "_down", act, w["down"], res=h)
    return h2, dict(h=h, hn=hn, g=g, val=val, act=act, w=w)


def ffn_bwd(tag, dh2, s, w, p, after=None, hook=None):
    dh2, dh2b = dh2
    dact = matmul(tag + "_dact", dh2b, w["down"], tb=True, after=after)
    later = hook(dact) if hook else None
    gw = dict(down=matmul(tag + "_gw_down", s["act"], dh2b, ta=True, out_dtype=BF16, after=later))
    (dg, dval), (dfcw, dfcb) = cols_bwd(tag + "_act_b", _ffn_act_f, [(s["g"], 0), (s["val"], 0)], [p["fcw"], p["fcb"]],
                                        dact, COL_TILE, BF16, after=later)
    dhn = matmul(tag + "_dx_v", dval, w["up_v"], tb=True, res=matmul(tag + "_dx_g", dg, w["up_g"], tb=True))
    gw["up"] = matmul_tn_slots(tag + "_gw_up", s["hn"], [dg, dval], 2 * D_FF // N_DEV)
    dh, dnf = _rms_bwd(tag + "_rms_b", s["h"], p["nf"], dhn, dh2)
    return dh, gw, dict(nf=dnf, fcw=dfcw, fcb=dfcb)


def _cols_from_slots(g):
    return jnp.moveaxis(g, 0, 1).reshape(g.shape[1], N_DEV * g.shape[2])


def _slots_from_cols(wmat):
    k, n = wmat.shape
    return jnp.moveaxis(wmat.reshape(k, N_DEV, n // N_DEV), 1, 0)


def _pad_last(a, width):
    return jnp.pad(a, [(0, 0)] * (a.ndim - 1) + [(0, width - a.shape[-1])])


def _heads_to_lanes(a):
    lead = a.shape[:-1]
    return _pad_last(a.reshape(lead + (SSD_STEPS, SSD_HEADS_PER_STEP)), LANES).reshape(lead + (SSD_STEPS * LANES,))


def _lanes_to_heads(a):
    lead = a.shape[:-1]
    return a.reshape(lead + (SSD_STEPS, LANES))[..., :SSD_HEADS_PER_STEP].reshape(lead + (SSM_HEADS,))


def prep_even(g_in, g_out):
    wn = _cols_from_slots(g_in)
    o1, o2, o3 = 2 * D_MODEL, 3 * D_MODEL, 3 * D_MODEL + SSM_CONV_DIM
    out = g_out.reshape(2 * D_MODEL, D_MODEL)
    return dict(uv=wn[:, :o1], z=wn[:, o1:o2], xbc=wn[:, o2:o3], dt=_heads_to_lanes(wn[:, o3:]),
                out_top=out[:D_MODEL], out_bot=out[D_MODEL:])


def unprep_even(gw):
    wn = jnp.concatenate([gw["uv"], gw["z"], gw["xbc"], _lanes_to_heads(gw["dt"])], axis=1)
    return _slots_from_cols(wn), jnp.concatenate([gw["out_top"], gw["out_bot"]], axis=0).reshape(N_DEV, -1, D_MODEL)


def prep_odd(g_in, g_uq, g_ukv, g_o):
    uq = _cols_from_slots(g_uq).reshape(MLA_RANK, MLA_HEADS, MLA_QK)
    ukv = _cols_from_slots(g_ukv).reshape(MLA_RANK, MLA_HEADS, MLA_NOPE + MLA_V)
    return dict(**{"in": _pad_last(g_in.reshape(D_MODEL, ODD_IN), ODD_IN_PAD)},
                uq=_pad_last(uq, MLA_QPAD).reshape(MLA_RANK, MLA_HEADS * MLA_QPAD),
                kn=ukv[:, :, :MLA_NOPE].reshape(MLA_RANK, MLA_HEADS * MLA_NOPE),
                v=ukv[:, :, MLA_NOPE:].reshape(MLA_RANK, MLA_HEADS * MLA_V),
                o=g_o.reshape(MLA_HEADS * MLA_V, D_MODEL))


def unprep_odd(gw):
    uq = gw["uq"].reshape(MLA_RANK, MLA_HEADS, MLA_QPAD)[:, :, :MLA_QK].reshape(MLA_RANK, MLA_HEADS * MLA_QK)
    ukv = jnp.concatenate([gw["kn"].reshape(MLA_RANK, MLA_HEADS, MLA_NOPE), gw["v"].reshape(MLA_RANK, MLA_HEADS, MLA_V)], axis=2)
    return (gw["in"][:, :ODD_IN].reshape(N_DEV, -1, ODD_IN), _slots_from_cols(uq),
            _slots_from_cols(ukv.reshape(MLA_RANK, -1)), gw["o"].reshape(N_DEV, -1, D_MODEL))


def prep_ffn(g_up, g_down):
    up = _cols_from_slots(g_up)
    return dict(up_g=up[:, :D_FF], up_v=up[:, D_FF:], down=g_down.reshape(D_FF, D_MODEL))


def unprep_ffn(gw):
    return gw["up"], gw["down"].reshape(N_DEV, -1, D_MODEL)


SMALL_TILE = LANES * LANES


def _pack(arrs):
    flat = jnp.concatenate([a.reshape(-1).astype(F32) for a in arrs])
    size = -(-flat.shape[0] // SMALL_TILE) * SMALL_TILE
    return jnp.pad(flat, (0, size - flat.shape[0])).reshape(-1, LANES)


def _unpack(packed, shapes, lead=()):
    flat = packed.reshape(lead + (-1,))
    out, off = [], 0
    for shp in shapes:
        size = math.prod(shp)
        out.append(flat[..., off:off + size].reshape(lead + tuple(shp)))
        off += size
    return out


SMALL_SHARDED = {"ev_gm_ln_g": 2, "ev_gm_ln_b": 2, "ev_conv_w": 2, "od_q_norm": 1, "od_kv_norm": 1, "ff_conv_w": 2}
SMALL_REPLICATED = ["norm_mix", "norm_ffn", "norm_final", "ev_gm_ws", "ev_gm_bs", "ev_conv_b", "ev_dt_bias", "ev_a_log",
                    "ev_d_skip", "ev_ssm_norm_w", "ff_conv_b"]
MATRICES = {"ev_w_in": (2, 2048, 1156), "ev_w_out": (2, 512, 2048), "od_w_in": (2, 256, 1088), "od_w_uq": (2, 512, 384),
            "od_w_ukv": (2, 512, 512), "od_w_o": (2, 256, 2048), "ff_w_up": (4, 2048, 1408), "ff_w_down": (4, 704, 2048)}
WEIGHT_ORDER = ["norm_mix", "norm_ffn", "norm_final", "ev_w_in", "ev_gm_ln_g", "ev_gm_ln_b", "ev_gm_ws", "ev_gm_bs",
                "ev_conv_w", "ev_conv_b", "ev_dt_bias", "ev_a_log", "ev_d_skip", "ev_ssm_norm_w", "ev_w_out", "od_w_in",
                "od_q_norm", "od_kv_norm", "od_w_uq", "od_w_ukv", "od_w_o", "ff_w_up", "ff_conv_w", "ff_conv_b", "ff_w_down"]


def _full_from_shards(name, gathered):
    ax = SMALL_SHARDED[name]
    moved = jnp.moveaxis(gathered, 0, ax)
    shp = moved.shape
    return moved.reshape(shp[:ax] + (shp[ax] * shp[ax + 1],) + shp[ax + 2:])


def _my_shard(name, full, dev):
    ax = SMALL_SHARDED[name]
    shp = full.shape
    split = full.reshape(shp[:ax] + (N_DEV, shp[ax] // N_DEV) + shp[ax + 1:])
    return lax.dynamic_index_in_dim(split, dev, axis=ax, keepdims=False)


def _even_small(sm, j):
    row = lambda a: a.reshape(1, -1)
    hp = jnp.stack([sm["ev_dt_bias"][j], sm["ev_a_log"][j], sm["ev_d_skip"][j]])
    return dict(nm=row(sm["norm_mix"][2 * j]), lng=row(sm["ev_gm_ln_g"][j]), lnb=row(sm["ev_gm_ln_b"][j]),
                ws=sm["ev_gm_ws"][j], bst=sm["ev_gm_bs"][j].T, cw=sm["ev_conv_w"][j], cb=row(sm["ev_conv_b"][j]),
                hp=_heads_to_lanes(hp), nw=row(sm["ev_ssm_norm_w"][j]))


def _odd_small(sm, j):
    row = lambda a: a.reshape(1, -1)
    return dict(nm=row(sm["norm_mix"][2 * j + 1]), qn=row(sm["od_q_norm"][j]), kvn=row(sm["od_kv_norm"][j]))


def _ffn_small(sm, layer):
    row = lambda a: a.reshape(1, -1)
    return dict(nf=row(sm["norm_ffn"][layer]), fcw=sm["ff_conv_w"][layer], fcb=row(sm["ff_conv_b"][layer]))


def _rope_tables(positions):
    inv_freq = ROPE_THETA ** (-jnp.arange(0, MLA_ROPE, 2, dtype=F32) / MLA_ROPE)
    ang = positions.astype(F32).reshape(-1, 1) * inv_freq
    cos, sin = jnp.cos(ang), jnp.sin(ang)
    return _pad_last(jnp.concatenate([cos, cos], axis=1), LANES), _pad_last(jnp.concatenate([-sin, sin], axis=1), LANES)


def local_step(x, positions, target, sm, fetch_weights, emit_grads):
    cos, sin = _rope_tables(positions)
    h, saved = x, []
    for layer in range(4):
        j, tag = layer // 2, f"l{layer}"
        get_w, dep = fetch_weights(2 * layer, h)
        if layer % 2 == 0:
            pm = _even_small(sm, j)
            h, sv = even_fwd(tag, h, get_w, pm, dep)
        else:
            pm = _odd_small(sm, j)
            h, sv = odd_fwd(tag, h, get_w, pm, cos, sin, dep)
        get_w, dep = fetch_weights(2 * layer + 1, h)
        pf = _ffn_small(sm, layer)
        h, sf = ffn_fwd(tag + "f", h, get_w, pf, dep)
        saved.append((pm, sv, pf, sf, sv["w"], sf["w"]))
    loss_tile, dh32, dh16, dnfinal = final_loss("final_loss", h, sm["norm_final"].reshape(1, -1), target)
    gs = {k: [None] * v.shape[0] for k, v in sm.items() if k != "norm_final"}
    gs["norm_final"] = dnfinal.reshape(-1)
    dh = (dh32, dh16)
    dep, hook = None, None
    for layer in reversed(range(4)):
        j, tag = layer // 2, f"l{layer}"
        pm, sv, pf, sf, wm, wf = saved[layer]
        dh, gwf, gpf = ffn_bwd(tag + "f", dh, sf, wf, pf, dep, hook)
        gs["norm_ffn"][layer], gs["ff_conv_w"][layer], gs["ff_conv_b"][layer] = gpf["nf"][0], gpf["fcw"], gpf["fcb"][0]
        dep, hook = emit_grads(2 * layer + 1, gwf, dh[0])
        if layer % 2 == 0:
            dh, gwm, gp = even_bwd(tag, dh, sv, wm, pm, dep, hook)
            hp = _lanes_to_heads(gp["hp"])
            gs["norm_mix"][layer] = gp["nm"][0]
            gs["ev_gm_ln_g"][j], gs["ev_gm_ln_b"][j] = gp["lng"].reshape(GM_GROUPS, -1), gp["lnb"].reshape(GM_GROUPS, -1)
            gs["ev_gm_ws"][j], gs["ev_gm_bs"][j] = gp["ws"], gp["bst"].T
            gs["ev_conv_w"][j], gs["ev_conv_b"][j] = gp["cw"], gp["cb"][0]
            gs["ev_dt_bias"][j], gs["ev_a_log"][j], gs["ev_d_skip"][j] = hp[0], hp[1], hp[2]
            gs["ev_ssm_norm_w"][j] = gp["nw"][0]
        else:
            dh, gwm, gp = odd_bwd(tag, dh, sv, wm, pm, cos, sin, dep, hook)
            gs["norm_mix"][layer] = gp["nm"][0]
            gs["od_q_norm"][j], gs["od_kv_norm"][j] = gp["qn"][0], gp["kvn"][0]
        if layer > 0:
            dep, hook = emit_grads(2 * layer, gwm, dh[0])
    gs = {k: (v if k == "norm_final" else jnp.stack(v)) for k, v in gs.items()}
    return loss_tile[0, 0], dh[0], gs, gwm


def kernel(x, positions, norm_mix, norm_ffn, norm_final, ev_w_in, ev_gm_ln_g, ev_gm_ln_b, ev_gm_ws, ev_gm_bs, ev_conv_w, ev_conv_b, ev_dt_bias, ev_a_log, ev_d_skip, ev_ssm_norm_w, ev_w_out, od_w_in, od_q_norm, od_kv_norm, od_w_uq, od_w_ukv, od_w_o, ff_w_up, ff_conv_w, ff_conv_b, ff_w_down, loss_target, m_norm_mix, m_norm_ffn, m_norm_final, m_ev_w_in, m_ev_gm_ln_g, m_ev_gm_ln_b, m_ev_gm_ws, m_ev_gm_bs, m_ev_conv_w, m_ev_conv_b, m_ev_dt_bias, m_ev_a_log, m_ev_d_skip, m_ev_ssm_norm_w, m_ev_w_out, m_od_w_in, m_od_q_norm, m_od_kv_norm, m_od_w_uq, m_od_w_ukv, m_od_w_o, m_ff_w_up, m_ff_conv_w, m_ff_conv_b, m_ff_w_down, v_norm_mix, v_norm_ffn, v_norm_final, v_ev_w_in, v_ev_gm_ln_g, v_ev_gm_ln_b, v_ev_gm_ws, v_ev_gm_bs, v_ev_conv_w, v_ev_conv_b, v_ev_dt_bias, v_ev_a_log, v_ev_d_skip, v_ev_ssm_norm_w, v_ev_w_out, v_od_w_in, v_od_q_norm, v_od_kv_norm, v_od_w_uq, v_od_w_ukv, v_od_w_o, v_ff_w_up, v_ff_conv_w, v_ff_conv_b, v_ff_w_down):
    args = dict(locals())
    wts = {n: args[n] for n in WEIGHT_ORDER}
    mom = {n: args["m_" + n] for n in WEIGHT_ORDER}
    var = {n: args["v_" + n] for n in WEIGHT_ORDER}
    dev = _device_slot()

    small_names = list(SMALL_SHARDED)
    small_shapes = [wts[n].shape for n in small_names]
    (small_all,) = all_gather("ag_small", [_pack([wts[n] for n in small_names])])
    small_full = _unpack(small_all, small_shapes, lead=(N_DEV,))
    sm = {n: _full_from_shards(n, g) for n, g in zip(small_names, small_full)}
    sm.update({n: wts[n] for n in SMALL_REPLICATED})

    def stage_matrices(stage):
        layer, is_ffn = divmod(stage, 2)
        if is_ffn:
            return [("ff_w_up", layer), ("ff_w_down", layer)]
        return [(n, layer // 2) for n in (["ev_w_in", "ev_w_out"] if layer % 2 == 0 else ["od_w_in", "od_w_uq", "od_w_ukv", "od_w_o"])]

    def stage_fns(stage):
        layer, is_ffn = divmod(stage, 2)
        if is_ffn:
            return prep_ffn, unprep_ffn
        return (prep_even, unprep_even) if layer % 2 == 0 else (prep_odd, unprep_odd)

    n_stages, ahead = 8, 2

    bf = {n: wts[n].astype(BF16) for n in MATRICES}

    def start_gather(stage, earlier=None):
        shards = [bf[n][i] for n, i in stage_matrices(stage)]
        if earlier is not None:
            shards, _ = lax.optimization_barrier((shards, earlier))
        return split_start(f"ag_s{stage}_start", shards, [(N_DEV,) + s.shape for s in shards],
                           gather_plan(len(shards)), 4 * len(shards))

    gathers = {}
    for stage in range(ahead):
        gathers[stage] = start_gather(stage, gathers[stage - 1][4] if stage else None)

    def fetch_weights(stage, h):
        n = len(stage_matrices(stage))
        shards, landed = split_wait(f"ag_s{stage}_wait", gathers.pop(stage), gather_plan(n), h)
        passing = split_start(f"ag_s{stage}_pass_start", landed, None, pass_on_plan(n), 3 * n)

        def get_w(first_result):
            g, _ = split_wait(f"ag_s{stage}_pass_wait", passing, pass_on_plan(n), first_result)
            g = [lax.dynamic_update_index_in_dim(gk, sk, dev, 0) for gk, sk in zip(g, shards)]
            started = None
            if stage + ahead < n_stages:
                gathers[stage + ahead] = start_gather(stage + ahead, g[0])
                started = gathers[stage + ahead][4]
            return stage_fns(stage)[0](*g), started

        return get_w, passing[4]

    scatters = []
    out = {n: None for n in MATRICES}

    held = []
    hold_below = 6

    def update(mats, sums, recv, after=None):
        for (n, i), p_, r_ in zip(mats, sums, recv):
            layers, rows, cols = MATRICES[n]
            two_d = lambda a: a.reshape(layers * rows, cols)
            out[n] = adamw(f"adamw_{n}_{i}", [(p_, 0), (r_, 0), (r_, 1), (r_, 2)], two_d(wts[n]), two_d(mom[n]),
                           two_d(var[n]), tr=_row_tile(rows, 256), part=i, prev=out[n], after=after)
            if after is not None:
                after = out[n][0]
        return after

    def finish_scatter(after):
        stage, handle = scatters.pop(0)
        mats = stage_matrices(stage)
        sums, recv = split_wait(f"rs_s{stage}_wait", handle, chips_plan(len(mats)), after)
        if 0 < stage < hold_below:
            held.append((mats, sums, recv))
        else:
            update(mats, sums, recv)

    def start_chips(stage, send, from_sibling):
        sums = [chip_sums(f"rs_s{stage}_add{k}", g, r, tr=_row_tile(g.shape[1], 512)) for k, (g, r) in enumerate(zip(send, from_sibling))]
        handle = split_start(f"rs_s{stage}_start", sums, [(3,) + s.shape[1:] for s in sums], chips_plan(len(sums)), 3 * len(sums))
        scatters.append((stage, handle))
        return handle[4]

    def emit_grads(stage, gw, dh):
        if len(scatters) >= ahead:
            finish_scatter(dh)
        send = list(stage_fns(stage)[1](gw))
        plan = sibling_plan(len(send))
        handle = split_start(f"rs_s{stage}_sib_start", send, [(4,) + g.shape[1:] for g in send], plan, 4 * len(send))

        def hook(first_result):
            sent, from_sibling = split_wait(f"rs_s{stage}_sib_wait", handle, plan, first_result)
            return start_chips(stage, sent, from_sibling)

        return handle[4], hook

    def emit_last(stage, gw, after):
        finish_scatter(after)
        send = list(stage_fns(stage)[1](gw))
        return start_chips(stage, send, exchange_sibling(f"rs_s{stage}_sibling", send, after))

    loss_local, dx, gs, gw_first = local_step(x[0], positions[0], loss_target[0], sm, fetch_weights, emit_grads)
    loss = lax.psum(loss_local, ("x", "y", "c"))

    all_small = small_names + SMALL_REPLICATED
    (partials,) = all_gather("ar_small", [_pack([gs[n] for n in all_small])])
    last_started = emit_last(0, gw_first, partials)
    total = sum_parts("ar_small_sum", [(partials, s) for s in range(N_DEV)])
    g_full = dict(zip(all_small, _unpack(total, [gs[n].shape for n in all_small])))
    g_mine = {n: (_my_shard(n, g_full[n], dev) if n in SMALL_SHARDED else g_full[n]) for n in all_small}
    packed = [_pack([d[n] for n in all_small]) for d in (g_mine, wts, mom, var)]
    res = adamw("adamw_small", [(packed[0][None], 0)], packed[1], packed[2], packed[3], after=last_started)
    unpacked = [_unpack(a, [wts[n].shape for n in all_small]) for a in res]
    for i, n in enumerate(all_small):
        out[n] = [u[i] for u in unpacked]
    while len(scatters) > 1:
        finish_scatter(res[0])
    follow = res[0]
    for job in held:
        follow = update(*job, after=follow)
    finish_scatter(follow)
    for n in MATRICES:
        out[n] = [a.reshape(wts[n].shape) for a in out[n]]

    return (loss, dx[None], *[out[n][0] for n in WEIGHT_ORDER], *[out[n][1] for n in WEIGHT_ORDER],
            *[out[n][2] for n in WEIGHT_ORDER], *[out[n][3] for n in WEIGHT_ORDER])
```

```python
import functools
import math

import jax
import jax.numpy as jnp
from jax import lax
from jax.experimental import pallas as pl
from jax.experimental.pallas import tpu as pltpu

F32 = jnp.float32
BF16 = jnp.bfloat16
MESH = pl.DeviceIdType.MESH

V7X_VMEM_LIMIT_BYTES = 56 * 1024 * 1024
LANES = 128

EPS = 1e-6
D_MODEL = 2048
CHUNK = 64
GM_BLOCK = 128
GM_GROUPS = 8
GM_GROUP_DIM = D_MODEL // GM_GROUPS
SSM_HEADS = 32
SSM_HEAD_DIM = 64
SSM_GROUPS = 4
SSM_STATE = 128
SSM_BC = SSM_GROUPS * SSM_STATE
SSM_CONV_DIM = D_MODEL + 2 * SSM_BC
SSD_HEADS_PER_STEP = 2
SSD_STEPS = SSM_HEADS // SSD_HEADS_PER_STEP
SSD_X_WIDTH = SSD_HEADS_PER_STEP * SSM_HEAD_DIM
MLA_HEADS = 16
MLA_RANK = 512
MLA_NOPE = 128
MLA_ROPE = 64
MLA_V = 128
MLA_QK = MLA_NOPE + MLA_ROPE
MLA_QPAD = 2 * LANES
ODD_IN = 2 * MLA_RANK + MLA_ROPE
ODD_IN_PAD = 2 * MLA_RANK + LANES
D_FF = 5632
ROPE_THETA = 10000.0
N_DEV = 8

ADAM_LR, ADAM_B1, ADAM_B2, ADAM_EPS, ADAM_WD, ADAM_STEP = 0.001, 0.9, 0.999, 1e-08, 0.01, 10


def _params(*sem):
    return pltpu.CompilerParams(dimension_semantics=sem, vmem_limit_bytes=V7X_VMEM_LIMIT_BYTES)


def _pick(dim, target):
    if dim <= target:
        return dim
    t = (target // LANES) * LANES
    while t >= LANES:
        if dim % t == 0:
            return t
        t -= LANES
    raise ValueError(f"no tile for {dim} under {target}")


MATMUL_VMEM_BUDGET = 32 * 1024 * 1024


def _matmul_tiles(m, n, k, a_bytes, b_bytes, o_bytes, has_res, ta):
    def fits(tm, tn):
        per_out = o_bytes + (4 if has_res else 0)
        return 2 * (tm * k * a_bytes + tn * k * b_bytes + tm * tn * per_out) <= MATMUL_VMEM_BUDGET

    tns = (2048, 1024, 512, 256, 128) if ta else (512, 256, 128)
    tms = (512, 256, 128) if ta else (2048, 1024, 512, 256, 128)
    for tn in tns:
        tn = _pick(n, tn)
        for tm in tms:
            tm = _pick(m, tm)
            if fits(tm, tn):
                return tm, tn
    raise ValueError(f"no matmul tiles for {m}x{n}x{k}")


def matmul(name, a, b, *, ta=False, tb=False, res=None, out_dtype=F32, after=None):
    m, k = (a.shape[1], a.shape[0]) if ta else a.shape
    n = b.shape[0] if tb else b.shape[1]
    assert k == (b.shape[1] if tb else b.shape[0]), (name, a.shape, b.shape)
    tm, tn = _matmul_tiles(m, n, k, a.dtype.itemsize, b.dtype.itemsize, jnp.dtype(out_dtype).itemsize, res is not None, ta)
    dims = (((0 if ta else 1,), (1 if tb else 0,)), ((), ()))

    def body(*refs):
        a_ref, b_ref, o_ref = refs[0], refs[1], refs[-1]
        total = lax.dot_general(a_ref[...].astype(BF16), b_ref[...].astype(BF16), dims, preferred_element_type=F32)
        if res is not None:
            total = total + refs[2][...]
        o_ref[...] = total.astype(o_ref.dtype)

    a_spec = pl.BlockSpec((k, tm), lambda i, j: (0, i)) if ta else pl.BlockSpec((tm, k), lambda i, j: (i, 0))
    b_spec = pl.BlockSpec((tn, k), lambda i, j: (j, 0)) if tb else pl.BlockSpec((k, tn), lambda i, j: (0, j))
    o_spec = pl.BlockSpec((tm, tn), lambda i, j: (i, j))
    ins, specs = [a, b], [a_spec, b_spec]
    if res is not None:
        ins.append(res)
        specs.append(o_spec)
    if after is not None:
        ins.append(after)
        specs.append(pl.BlockSpec(memory_space=pl.ANY))
    return pl.pallas_call(
        body, name=name, grid=(m // tm, n // tn), in_specs=specs, out_specs=o_spec,
        out_shape=jax.ShapeDtypeStruct((m, n), out_dtype),
        compiler_params=_params("parallel", "parallel"),
    )(*ins)


def matmul_tn_slots(name, a, bs, width, tm=512):
    k, m = a.shape
    counts = [b.shape[1] // width for b in bs]
    firsts = [sum(counts[:i]) for i in range(len(bs))]
    tm = _pick(m, tm)
    tn_dims = (((0,), (0,)), ((), ()))

    def body(*refs):
        a_ref, o_ref = refs[0], refs[-1]
        j = pl.program_id(1)
        for b_ref, first, count in zip(refs[1:-1], firsts, counts):
            @pl.when(jnp.logical_and(j >= first, j < first + count))
            def _(b_ref=b_ref):
                o_ref[0] = lax.dot_general(a_ref[...].astype(BF16), b_ref[...].astype(BF16), tn_dims,
                                           preferred_element_type=F32).astype(o_ref.dtype)

    specs = [pl.BlockSpec((k, tm), lambda i, j: (0, i))]
    specs += [pl.BlockSpec((k, width), lambda i, j, first=first, count=count: (0, jnp.clip(j - first, 0, count - 1)))
              for first, count in zip(firsts, counts)]
    return pl.pallas_call(
        body, name=name, grid=(m // tm, sum(counts)), in_specs=specs,
        out_specs=pl.BlockSpec((1, tm, width), lambda i, j: (j, i, 0)),
        out_shape=jax.ShapeDtypeStruct((sum(counts), m, width), BF16),
        compiler_params=_params("parallel", "arbitrary"),
    )(a, *bs)


@functools.partial(jax.custom_vjp, nondiff_argnums=(1, 2))
def _roll(x, shift, axis):
    return pltpu.roll(x, shift, axis)


def _roll_fwd(x, shift, axis):
    return pltpu.roll(x, shift, axis), None


def _roll_bwd(shift, axis, _, g):
    return (pltpu.roll(g, (g.shape[axis] - shift) % g.shape[axis], axis),)


_roll.defvjp(_roll_fwd, _roll_bwd)


def _shift_down(x, s):
    rows = lax.broadcasted_iota(jnp.int32, x.shape, 0)
    return jnp.where(rows >= s, _roll(x, s, 0), 0.0)


def _dwconv(x, w, b):
    taps = w.shape[0]
    y = b + w[taps - 1:taps, :] * x
    for kk in range(taps - 1):
        y = y + w[kk:kk + 1, :] * _shift_down(x, taps - 1 - kk)
    return y


def _rms(x, w):
    return x * lax.rsqrt(jnp.mean(x * x, -1, keepdims=True) + EPS) * w


def _rms_f(h, w):
    return (_rms(h, w),)


def _gmlp_f(uv, lng, lnb, ws, bst):
    r = lax.broadcasted_iota(jnp.int32, (GM_BLOCK, GM_BLOCK), 0) // CHUNK
    c = lax.broadcasted_iota(jnp.int32, (GM_BLOCK, GM_BLOCK), 1) // CHUNK
    outs = []
    for g in range(GM_GROUPS):
        lo, hi = g * GM_GROUP_DIM, (g + 1) * GM_GROUP_DIM
        gu = jax.nn.gelu(uv[:, lo:hi])
        gv = jax.nn.gelu(uv[:, D_MODEL + lo:D_MODEL + hi])
        xc = gv - jnp.mean(gv, -1, keepdims=True)
        var = jnp.mean(xc * xc, -1, keepdims=True)
        vn = xc * lax.rsqrt(var + EPS) * lng[:, lo:hi] + lnb[:, lo:hi]
        wm = jnp.where(r >= c, ws[g], 0.0).astype(BF16)
        gate = jnp.dot(wm, vn.astype(BF16), preferred_element_type=F32) + bst[:, g:g + 1]
        outs.append(gu * gate)
    return (jnp.concatenate(outs, axis=1),)


def _conv_silu_f(x, w, b):
    return (jax.nn.silu(_dwconv(x, w, b)),)


def _ffn_act_f(g, val, w, b):
    return (jax.nn.gelu(_dwconv(g, w, b)) * val,)


def _gate_norm_f(y, z, nw):
    y2 = y * jax.nn.silu(z)
    width = D_MODEL // SSM_GROUPS
    outs = []
    for g in range(SSM_GROUPS):
        blk = y2[:, g * width:(g + 1) * width]
        outs.append(blk * lax.rsqrt(jnp.mean(blk * blk, -1, keepdims=True) + EPS))
    return (jnp.concatenate(outs, axis=1) * nw,)


def _rope(x, cos, sin):
    lane = lax.broadcasted_iota(jnp.int32, x.shape, 1)
    half = MLA_ROPE // 2
    swapped = jnp.where(lane < half, _roll(x, LANES - half, 1), _roll(x, half, 1))
    return x * cos + swapped * sin


def _qkv_norm_f(proj, cos, sin, qn, kvn):
    cq = _rms(proj[:, :MLA_RANK], qn)
    ckv = _rms(proj[:, MLA_RANK:2 * MLA_RANK], kvn)
    kpe = _rope(proj[:, 2 * MLA_RANK:], cos, sin)
    return cq, ckv, kpe


def _attn_scores(q0, k0, qh, kn, kpe, cos, sin):
    qn = qh[:, :MLA_NOPE]
    qp = _rope(qh[:, MLA_NOPE:], cos, sin)
    nt = (((1,), (1,)), ((), ()))
    s = lax.dot_general(qn.astype(BF16), kn.astype(BF16), nt, preferred_element_type=F32)
    s = s + lax.dot_general(qp.astype(BF16), kpe.astype(BF16), nt, preferred_element_type=F32)
    s = s * (MLA_QK ** -0.5)
    visible_below = ((q0 + lax.broadcasted_iota(jnp.int32, (s.shape[0], 1), 0)) // CHUNK + 1) * CHUNK - k0
    return jnp.where(lax.broadcasted_iota(jnp.int32, s.shape, 1) < visible_below, s, -jnp.inf)


def _ssd_chunk_f(x, bm, cm, pdt, hp, sprev):
    nh, hd = SSD_HEADS_PER_STEP, SSM_HEAD_DIM
    dt = jax.nn.softplus(pdt + hp[0:1, :])
    cs = dt * (-jnp.exp(hp[1:2, :]))
    shift = 1
    while shift < CHUNK:
        cs = cs + _shift_down(cs, shift)
        shift *= 2
    cst = cs.T
    tot = cs[CHUNK - 1:CHUNK, :]

    def lanes(vals):
        return jnp.concatenate([jnp.broadcast_to(vals[:, e:e + 1], (vals.shape[0], hd)) for e in range(nh)], axis=1)

    r = lax.broadcasted_iota(jnp.int32, (CHUNK, CHUNK), 0)
    c = lax.broadcasted_iota(jnp.int32, (CHUNK, CHUNK), 1)
    tril = r >= c
    nt = (((1,), (1,)), ((), ()))
    tn = (((0,), (0,)), ((), ()))
    xd = x * lanes(dt)
    cb = lax.dot_general(cm.astype(BF16), bm.astype(BF16), nt, preferred_element_type=F32)
    ys = []
    for e in range(nh):
        decay = jnp.exp(jnp.where(tril, cs[:, e:e + 1] - cst[e:e + 1, :], -jnp.inf))
        ys.append(jnp.dot((cb * decay).astype(BF16), xd[:, e * hd:(e + 1) * hd].astype(BF16), preferred_element_type=F32))
    st = lax.dot_general((xd * lanes(jnp.exp(tot - cs))).astype(BF16), bm.astype(BF16), tn, preferred_element_type=F32)
    yoff = lax.dot_general(cm.astype(BF16), sprev.astype(BF16), nt, preferred_element_type=F32)
    y = jnp.concatenate(ys, axis=1) + yoff * lanes(jnp.exp(cs)) + lanes(hp[2:3, :]) * x
    carry = jnp.concatenate([jnp.broadcast_to(jnp.exp(tot[:, e:e + 1]), (hd, 1)) for e in range(nh)], axis=0)
    return y, carry * sprev + st


def _full_spec(a):
    nd = a.ndim
    return pl.BlockSpec(a.shape, lambda i, nd=nd: (0,) * nd)


def rows_fwd(name, f, rows, params, outs, tr, after=None):
    t = rows[0][0].shape[0]
    nr, npar = len(rows), len(params)
    extra = [] if after is None else [after]

    def body(*refs):
        vals = f(*[x[...].astype(F32) for x in refs[:nr + npar]])
        for o_ref, val in zip(refs[nr + npar + len(extra):], vals):
            o_ref[...] = val.astype(o_ref.dtype)

    in_specs = [pl.BlockSpec((tr, w), lambda i, cb=cb: (i, cb)) for _, w, cb in rows] + [_full_spec(p) for p in params]
    in_specs += [pl.BlockSpec(memory_space=pl.ANY)] * len(extra)
    out = pl.pallas_call(
        body, name=name, grid=(t // tr,), in_specs=in_specs,
        out_specs=[pl.BlockSpec((tr, w), lambda i: (i, 0)) for w, _ in outs],
        out_shape=[jax.ShapeDtypeStruct((t, w), dt) for w, dt in outs],
        compiler_params=_params("parallel"),
    )(*[a for a, _, _ in rows], *params, *extra)
    return out


def rows_bwd(name, f, rows, params, cots, tr, d_dtypes, n_nondiff=0, add=None, after=None):
    t = rows[0][0].shape[0]
    nr, npar, nc = len(rows), len(params), len(cots)
    nd = nr - n_nondiff
    has_add = add is not None
    copies = [(j, dt) for j in range(nd) for dt in (d_dtypes[j] if isinstance(d_dtypes[j], tuple) else (d_dtypes[j],))]
    ncp = len(copies)

    def body(*refs):
        i = pl.program_id(0)
        row_vals = [x[...].astype(F32) for x in refs[:nr]]
        par_vals = [x[...].astype(F32) for x in refs[nr:nr + npar]]
        cot_refs = refs[nr + npar:nr + npar + nc]
        pos = nr + npar + nc
        add_ref = refs[pos] if has_add else None
        pos += int(has_add) + int(after is not None)
        drow_refs = refs[pos:pos + ncp]
        dpar_refs = refs[pos + ncp:]

        def g(*diff):
            return f(*diff[:nd], *row_vals[nd:], *diff[nd:])

        _, vjp = jax.vjp(g, *row_vals[:nd], *par_vals)
        grads = vjp(tuple(cr[...].astype(F32) for cr in cot_refs))
        for (j, _), d_ref in zip(copies, drow_refs):
            val = grads[j]
            if j == 0 and has_add:
                val = val + add_ref[...]
            d_ref[...] = val.astype(d_ref.dtype)
        for j, d_ref in enumerate(dpar_refs):
            @pl.when(i == 0)
            def _(d_ref=d_ref, j=j):
                d_ref[...] = grads[nd + j]

            @pl.when(i > 0)
            def _(d_ref=d_ref, j=j):
                d_ref[...] += grads[nd + j]

    in_specs = [pl.BlockSpec((tr, w), lambda i, cb=cb: (i, cb)) for _, w, cb in rows] + [_full_spec(p) for p in params]
    in_specs += [pl.BlockSpec((tr, w), lambda i, cb=cb: (i, cb)) for _, w, cb in cots]
    ins = [a for a, _, _ in rows] + list(params) + [a for a, _, _ in cots]
    if has_add:
        in_specs.append(pl.BlockSpec((tr, rows[0][1]), lambda i: (i, 0)))
        ins.append(add)
    if after is not None:
        in_specs.append(pl.BlockSpec(memory_space=pl.ANY))
        ins.append(after)
    out_specs = [pl.BlockSpec((tr, rows[j][1]), lambda i: (i, 0)) for j, _ in copies] + [_full_spec(p) for p in params]
    out_shape = [jax.ShapeDtypeStruct((t, rows[j][1]), dt) for j, dt in copies]
    out_shape += [jax.ShapeDtypeStruct(p.shape, F32) for p in params]
    out = pl.pallas_call(
        body, name=name, grid=(t // tr,), in_specs=in_specs, out_specs=out_specs, out_shape=out_shape,
        compiler_params=_params("arbitrary"),
    )(*ins)
    return out[:ncp], out[ncp:]


def cols_fwd(name, f, cols, cparams, out_dtype, tc):
    t = cols[0][0].shape[0]
    width = cparams[0].shape[1]
    ncol = len(cols)

    def body(*refs):
        (val,) = f(*[x[...].astype(F32) for x in refs[:-1]])
        refs[-1][...] = val.astype(refs[-1].dtype)

    in_specs = [pl.BlockSpec((t, tc), lambda j, o=o: (0, o + j)) for _, o in cols]
    in_specs += [pl.BlockSpec((p.shape[0], tc), lambda j: (0, j)) for p in cparams]
    return pl.pallas_call(
        body, name=name, grid=(width // tc,), in_specs=in_specs,
        out_specs=pl.BlockSpec((t, tc), lambda j: (0, j)),
        out_shape=jax.ShapeDtypeStruct((t, width), out_dtype),
        compiler_params=_params("parallel"),
    )(*[a for a, _ in cols], *cparams)


def cols_bwd(name, f, cols, cparams, cot, tc, d_dtype, after=None):
    t = cols[0][0].shape[0]
    width = cparams[0].shape[1]
    ncol, npar = len(cols), len(cparams)
    extra = [] if after is None else [after]

    def body(*refs):
        vals = [x[...].astype(F32) for x in refs[:ncol + npar]]
        _, vjp = jax.vjp(f, *vals)
        grads = vjp((refs[ncol + npar][...].astype(F32),))
        for d_ref, gval in zip(refs[ncol + npar + 1 + len(extra):], grads):
            d_ref[...] = gval.astype(d_ref.dtype)

    in_specs = [pl.BlockSpec((t, tc), lambda j, o=o: (0, o + j)) for _, o in cols]
    in_specs += [pl.BlockSpec((p.shape[0], tc), lambda j: (0, j)) for p in cparams]
    in_specs.append(pl.BlockSpec((t, tc), lambda j: (0, j)))
    in_specs += [pl.BlockSpec(memory_space=pl.ANY)] * len(extra)
    out_specs = [pl.BlockSpec((t, tc), lambda j: (0, j)) for _ in cols]
    out_specs += [pl.BlockSpec((p.shape[0], tc), lambda j: (0, j)) for p in cparams]
    out_shape = [jax.ShapeDtypeStruct((t, width), d_dtype) for _ in cols]
    out_shape += [jax.ShapeDtypeStruct(p.shape, F32) for p in cparams]
    out = pl.pallas_call(
        body, name=name, grid=(width // tc,), in_specs=in_specs, out_specs=out_specs, out_shape=out_shape,
        compiler_params=_params("parallel"),
    )(*[a for a, _ in cols], *cparams, cot, *extra)
    return out[:ncol], out[ncol:]


def _ssd_in_specs(t):
    heads_per_group = SSM_HEADS // SSM_GROUPS
    steps_per_group = heads_per_group // SSD_HEADS_PER_STEP
    b_blk = D_MODEL // LANES
    c_blk = (D_MODEL + SSM_BC) // LANES
    return [
        pl.BlockSpec((t, SSD_X_WIDTH), lambda s: (0, s)),
        pl.BlockSpec((t, LANES), lambda s: (0, b_blk + s // steps_per_group)),
        pl.BlockSpec((t, LANES), lambda s: (0, c_blk + s // steps_per_group)),
        pl.BlockSpec((t, LANES), lambda s: (0, s)),
        pl.BlockSpec((3, LANES), lambda s: (0, s)),
    ]


def ssd_fwd(name, xa, pdt, hp):
    t = xa.shape[0]
    nc = t // CHUNK

    def body(x_ref, b_ref, c_ref, pdt_ref, hp_ref, y_ref, st_ref, s_scr):
        s_scr[...] = jnp.zeros_like(s_scr)

        def step(ci, carry):
            sl = pl.ds(pl.multiple_of(ci * CHUNK, CHUNK), CHUNK)
            sprev = s_scr[...]
            st_ref[0, ci] = sprev
            y, snew = _ssd_chunk_f(x_ref[sl, :], b_ref[sl, :], c_ref[sl, :], pdt_ref[sl, :], hp_ref[...], sprev)
            y_ref[sl, :] = y
            s_scr[...] = snew
            return carry

        lax.fori_loop(0, nc, step, 0)

    return pl.pallas_call(
        body, name=name, grid=(SSD_STEPS,), in_specs=_ssd_in_specs(t),
        out_specs=[pl.BlockSpec((t, SSD_X_WIDTH), lambda s: (0, s)),
                   pl.BlockSpec((1, nc, SSD_X_WIDTH, SSM_STATE), lambda s: (s, 0, 0, 0))],
        out_shape=[jax.ShapeDtypeStruct((t, D_MODEL), F32),
                   jax.ShapeDtypeStruct((SSD_STEPS, nc, SSD_X_WIDTH, SSM_STATE), F32)],
        scratch_shapes=[pltpu.VMEM((SSD_X_WIDTH, SSM_STATE), F32)],
        compiler_params=_params("parallel"),
    )(xa, xa, xa, pdt, hp)


def ssd_bwd(name, xa, pdt, hp, states, dy):
    t = xa.shape[0]
    nc = t // CHUNK
    steps_per_group = SSM_HEADS // SSM_GROUPS // SSD_HEADS_PER_STEP

    def body(x_ref, b_ref, c_ref, pdt_ref, hp_ref, st_ref, dy_ref, dx_ref, db_ref, dc_ref, dpdt_ref, dhp_ref, ds_scr, dhp_scr):
        first = pl.program_id(0) % steps_per_group == 0
        ds_scr[...] = jnp.zeros_like(ds_scr)
        dhp_scr[...] = jnp.zeros_like(dhp_scr)

        def step(i, carry):
            ci = nc - 1 - i
            sl = pl.ds(pl.multiple_of(ci * CHUNK, CHUNK), CHUNK)
            _, vjp = jax.vjp(_ssd_chunk_f, x_ref[sl, :], b_ref[sl, :], c_ref[sl, :], pdt_ref[sl, :], hp_ref[...], st_ref[0, ci])
            dx, db, dc, dpdt, dhp, dsprev = vjp((dy_ref[sl, :], ds_scr[...]))
            dx_ref[sl, :] = dx
            dpdt_ref[sl, :] = dpdt.astype(dpdt_ref.dtype)

            @pl.when(first)
            def _():
                db_ref[sl, :] = db
                dc_ref[sl, :] = dc

            @pl.when(jnp.logical_not(first))
            def _():
                db_ref[sl, :] += db
                dc_ref[sl, :] += dc

            ds_scr[...] = dsprev
            dhp_scr[...] += dhp
            return carry

        lax.fori_loop(0, nc, step, 0)
        dhp_ref[...] = dhp_scr[...]

    in_specs = _ssd_in_specs(t) + [
        pl.BlockSpec((1, nc, SSD_X_WIDTH, SSM_STATE), lambda s: (s, 0, 0, 0)),
        pl.BlockSpec((t, SSD_X_WIDTH), lambda s: (0, s)),
    ]
    out_specs = [
        pl.BlockSpec((t, SSD_X_WIDTH), lambda s: (0, s)),
        pl.BlockSpec((t, LANES), lambda s: (0, s // steps_per_group)),
        pl.BlockSpec((t, LANES), lambda s: (0, s // steps_per_group)),
        pl.BlockSpec((t, LANES), lambda s: (0, s)),
        pl.BlockSpec((3, LANES), lambda s: (0, s)),
    ]
    out_shape = [
        jax.ShapeDtypeStruct((t, D_MODEL), F32),
        jax.ShapeDtypeStruct((t, SSM_BC), F32),
        jax.ShapeDtypeStruct((t, SSM_BC), F32),
        jax.ShapeDtypeStruct((t, SSD_STEPS * LANES), BF16),
        jax.ShapeDtypeStruct((3, SSD_STEPS * LANES), F32),
    ]
    return pl.pallas_call(
        body, name=name, grid=(SSD_STEPS,), in_specs=in_specs, out_specs=out_specs, out_shape=out_shape,
        scratch_shapes=[pltpu.VMEM((SSD_X_WIDTH, SSM_STATE), F32), pltpu.VMEM((3, LANES), F32)],
        compiler_params=_params("arbitrary"),
    )(xa, xa, xa, pdt, hp, states, dy)


ATTN_TQ = 512
ATTN_KSTEP = 512


def _attn_extents(t):
    return [min(t, (g + 1) * ATTN_KSTEP) for g in range(-(-t // ATTN_KSTEP))]


def _attn_f(q0, qh, kn, kpe, v, cos, sin):
    p = jax.nn.softmax(_attn_scores(q0, 0, qh, kn, kpe, cos, sin), axis=-1)
    return (jnp.dot(p.astype(BF16), v.astype(BF16), preferred_element_type=F32),)


def _attn_in_specs(t):
    return [
        pl.BlockSpec((ATTN_TQ, MLA_QPAD), lambda h, qi: (qi, h)),
        pl.BlockSpec((t, MLA_NOPE), lambda h, qi: (0, h)),
        pl.BlockSpec((t, LANES), lambda h, qi: (0, 0)),
        pl.BlockSpec((t, MLA_V), lambda h, qi: (0, h)),
        pl.BlockSpec((ATTN_TQ, LANES), lambda h, qi: (qi, 0)),
        pl.BlockSpec((ATTN_TQ, LANES), lambda h, qi: (qi, 0)),
    ]


def attn_fwd(name, q, kn, kpe, v, cos, sin):
    t = q.shape[0]

    def body(q_ref, kn_ref, kpe_ref, v_ref, cos_ref, sin_ref, o_ref):
        qi = pl.program_id(1)
        for span, ext in enumerate(_attn_extents(t)):
            @pl.when(qi // (ATTN_KSTEP // ATTN_TQ) == span)
            def _(ext=ext):
                (o,) = _attn_f(qi * ATTN_TQ, q_ref[...], kn_ref[0:ext, :], kpe_ref[0:ext, :], v_ref[0:ext, :],
                               cos_ref[...], sin_ref[...])
                o_ref[...] = o.astype(o_ref.dtype)

    return pl.pallas_call(
        body, name=name, grid=(MLA_HEADS, t // ATTN_TQ), in_specs=_attn_in_specs(t),
        out_specs=pl.BlockSpec((ATTN_TQ, MLA_V), lambda h, qi: (qi, h)),
        out_shape=jax.ShapeDtypeStruct((t, MLA_HEADS * MLA_V), BF16),
        compiler_params=_params("parallel", "parallel"),
    )(q, kn, kpe, v, cos, sin)


def attn_bwd(name, q, kn, kpe, v, cos, sin, do, after=None):
    t = q.shape[0]
    extra = [] if after is None else [after]

    def body(q_ref, kn_ref, kpe_ref, v_ref, cos_ref, sin_ref, do_ref, *rest):
        dq_ref, dkn_ref, dkpe_ref, dv_ref = rest[len(extra):]
        h, qi = pl.program_id(0), pl.program_id(1)
        q0 = qi * ATTN_TQ
        cos, sin = cos_ref[...], sin_ref[...]

        @pl.when(qi == 0)
        def _():
            dkn_ref[...] = jnp.zeros_like(dkn_ref)
            dv_ref[...] = jnp.zeros_like(dv_ref)

        @pl.when(jnp.logical_and(h == 0, qi == 0))
        def _():
            dkpe_ref[...] = jnp.zeros_like(dkpe_ref)

        for span, ext in enumerate(_attn_extents(t)):
            @pl.when(qi // (ATTN_KSTEP // ATTN_TQ) == span)
            def _(ext=ext):
                def g(qh, knv, kpev, vv):
                    return _attn_f(q0, qh, knv, kpev, vv, cos, sin)

                _, vjp = jax.vjp(g, q_ref[...].astype(F32), kn_ref[0:ext, :].astype(F32), kpe_ref[0:ext, :].astype(F32),
                                 v_ref[0:ext, :].astype(F32))
                dq, dkn, dkpe, dv = vjp((do_ref[...].astype(F32),))
                dq_ref[...] = dq.astype(dq_ref.dtype)
                dkn_ref[0:ext, :] += dkn
                dkpe_ref[0:ext, :] += dkpe
                dv_ref[0:ext, :] += dv

    in_specs = _attn_in_specs(t) + [pl.BlockSpec((ATTN_TQ, MLA_V), lambda h, qi: (qi, h))]
    in_specs += [pl.BlockSpec(memory_space=pl.ANY)] * len(extra)
    out_specs = [
        pl.BlockSpec((ATTN_TQ, MLA_QPAD), lambda h, qi: (qi, h)),
        pl.BlockSpec((t, MLA_NOPE), lambda h, qi: (0, h)),
        pl.BlockSpec((t, LANES), lambda h, qi: (0, 0)),
        pl.BlockSpec((t, MLA_V), lambda h, qi: (0, h)),
    ]
    out_shape = [
        jax.ShapeDtypeStruct((t, MLA_HEADS * MLA_QPAD), BF16),
        jax.ShapeDtypeStruct((t, MLA_HEADS * MLA_NOPE), F32),
        jax.ShapeDtypeStruct((t, LANES), F32),
        jax.ShapeDtypeStruct((t, MLA_HEADS * MLA_V), F32),
    ]
    return pl.pallas_call(
        body, name=name, grid=(MLA_HEADS, t // ATTN_TQ), in_specs=in_specs, out_specs=out_specs, out_shape=out_shape,
        compiler_params=_params("arbitrary", "arbitrary"),
    )(q, kn, kpe, v, cos, sin, do, *extra)


def final_loss(name, h, nf, target, tr=256):
    t, d = h.shape

    def body(h_ref, w_ref, t_ref, loss_ref, dh_ref, dhb_ref, dw_ref):
        i = pl.program_id(0)
        tgt = t_ref[...]

        def f(hv, wv):
            err = _rms(hv, wv) - tgt
            return 0.5 * jnp.sum(jnp.mean(err * err, -1, keepdims=True), 0, keepdims=True)

        val, vjp = jax.vjp(f, h_ref[...], w_ref[...])
        dh, dw = vjp(jnp.ones((1, 1), F32))
        dh_ref[...] = dh
        dhb_ref[...] = dh.astype(dhb_ref.dtype)
        tile = jnp.broadcast_to(val, loss_ref.shape)

        @pl.when(i == 0)
        def _():
            loss_ref[...] = tile
            dw_ref[...] = dw

        @pl.when(i > 0)
        def _():
            loss_ref[...] += tile
            dw_ref[...] += dw

    row = pl.BlockSpec((tr, d), lambda i: (i, 0))
    return pl.pallas_call(
        body, name=name, grid=(t // tr,), in_specs=[row, _full_spec(nf), row],
        out_specs=[pl.BlockSpec((8, LANES), lambda i: (0, 0)), row, row, _full_spec(nf)],
        out_shape=[jax.ShapeDtypeStruct((8, LANES), F32), jax.ShapeDtypeStruct((t, d), F32), jax.ShapeDtypeStruct((t, d), BF16),
                   jax.ShapeDtypeStruct(nf.shape, F32)],
        compiler_params=_params("arbitrary"),
    )(h, nf, target)


ANY = pl.BlockSpec(memory_space=pl.ANY)
CHIP_ORDER = ((0, 0), (0, 1), (1, 0), (1, 1))


def _place():
    return lax.axis_index("x"), lax.axis_index("y"), lax.axis_index("c")


def _other_chips(x, y):
    return [(1 - x, y), (x, 1 - y), (1 - x, 1 - y)]


def _device_slot():
    x, y, c = _place()
    return 4 * x + 2 * y + c


def _row_tile(rows, cap=128):
    return next(t for t in (512, 256, 128, 64, 32, 16) if t <= cap and rows % t == 0)


def all_gather(name, shards):
    n = len(shards)

    def body(*refs):
        ins, outs = refs[:n], refs[n:2 * n]
        send_sems, recv_sems, local_sems = refs[2 * n:]
        x, y, c = _place()
        me, sibling = (x, y, c), (x, y, 1 - c)
        chips = _other_chips(x, y)

        def copy(k, j, block, to, from_input=False):
            dst = outs[k].at[4 * block[0] + 2 * block[1] + block[2]]
            return pltpu.make_async_remote_copy(
                src_ref=ins[k] if from_input else dst, dst_ref=dst,
                send_sem=send_sems.at[7 * k + j], recv_sem=recv_sems.at[7 * k + j],
                device_id=to, device_id_type=MESH)

        mine = [pltpu.make_async_copy(ins[k], outs[k].at[4 * x + 2 * y + c], local_sems.at[k]) for k in range(n)]
        for cp in mine:
            cp.start()
        first = []
        for k in range(n):
            first.append(copy(k, 0, me, sibling, True))
            first += [copy(k, 1 + j, me, (*chip, c), True) for j, chip in enumerate(chips)]
        for cp in first:
            cp.start()
        passed = []
        for j, chip in enumerate(chips):
            for k in range(n):
                copy(k, 1 + j, (*chip, c), me).wait_recv()
                fwd = copy(k, 4 + j, (*chip, c), sibling)
                fwd.start()
                passed.append(fwd)
        for k in range(n):
            copy(k, 0, sibling, me).wait_recv()
        for j, chip in enumerate(chips):
            for k in range(n):
                copy(k, 4 + j, (*chip, 1 - c), me).wait_recv()
        for cp in first + passed:
            cp.wait_send()
        for cp in mine:
            cp.wait()

    return pl.pallas_call(
        body, name=name, in_specs=[ANY] * n, out_specs=[ANY] * n,
        out_shape=[jax.ShapeDtypeStruct((N_DEV,) + s.shape, s.dtype) for s in shards],
        scratch_shapes=[pltpu.SemaphoreType.DMA((7 * n,)), pltpu.SemaphoreType.DMA((7 * n,)), pltpu.SemaphoreType.DMA((n,))],
    )(*shards)


def exchange_sibling(name, gs, after=None):
    n = len(gs)
    extra = [] if after is None else [after]

    def body(*refs):
        ins, outs = refs[:n], refs[n + len(extra):2 * n + len(extra)]
        send_sems, recv_sems = refs[2 * n + len(extra):]
        x, y, c = _place()
        copies = []
        for k in range(n):
            for q, (cx, cy) in enumerate(CHIP_ORDER):
                copies.append(pltpu.make_async_remote_copy(
                    src_ref=ins[k].at[4 * cx + 2 * cy + (1 - c)], dst_ref=outs[k].at[q],
                    send_sem=send_sems.at[4 * k + q], recv_sem=recv_sems.at[4 * k + q],
                    device_id=(x, y, 1 - c), device_id_type=MESH))
        for cp in copies:
            cp.start()
        for cp in copies:
            cp.wait()

    return pl.pallas_call(
        body, name=name, in_specs=[ANY] * (n + len(extra)), out_specs=[ANY] * n,
        out_shape=[jax.ShapeDtypeStruct((4,) + g.shape[1:], g.dtype) for g in gs],
        scratch_shapes=[pltpu.SemaphoreType.DMA((4 * n,)), pltpu.SemaphoreType.DMA((4 * n,))],
    )(*gs, *extra)


def chip_sums(name, g, recv, tr=128):
    _, r, c = g.shape

    def body(g_ref, r_ref, o_ref):
        o_ref[...] = (g_ref[...].astype(F32) + r_ref[...].astype(F32)).astype(o_ref.dtype)

    def chip(i):
        x, y, _ = _place()
        return jnp.where(i % 2 == 1, 1 - x, x), jnp.where(i >= 2, 1 - y, y)

    def g_index(i, j):
        cx, cy = chip(i)
        return 4 * cx + 2 * cy + lax.axis_index("c"), j, 0

    def recv_index(i, j):
        cx, cy = chip(i)
        return 2 * cx + cy, j, 0

    return pl.pallas_call(
        body, name=name, grid=(4, r // tr),
        in_specs=[pl.BlockSpec((1, tr, c), g_index), pl.BlockSpec((1, tr, c), recv_index)],
        out_specs=pl.BlockSpec((1, tr, c), lambda i, j: (i, j, 0)),
        out_shape=jax.ShapeDtypeStruct((4, r, c), g.dtype),
        compiler_params=_params("parallel", "parallel"),
    )(g, recv)


def sum_parts(name, parts, tr=128):
    _, r, c = parts[0][0].shape

    def body(*refs):
        total = refs[0][0].astype(F32)
        for ref in refs[1:-1]:
            total = total + ref[0].astype(F32)
        refs[-1][...] = total

    return pl.pallas_call(
        body, name=name, grid=(r // tr,),
        in_specs=[pl.BlockSpec((1, tr, c), lambda i, s=s: (s, i, 0)) for _, s in parts],
        out_specs=pl.BlockSpec((tr, c), lambda i: (i, 0)), out_shape=jax.ShapeDtypeStruct((r, c), F32),
        compiler_params=_params("parallel"),
    )(*[a for a, _ in parts])


def adamw(name, parts, w, m, v, tr=128, part=0, prev=None, after=None):
    _, r, c = parts[0][0].shape
    np_ = len(parts)
    first = part * (r // tr)

    def body(*refs):
        g = refs[0][0].astype(F32)
        for ref in refs[1:np_]:
            g = g + ref[0].astype(F32)
        w_ref, m_ref, v_ref = refs[np_:np_ + 3]
        g_out, d_out, m_out, v_out = refs[-4:]
        new_m = ADAM_B1 * m_ref[...] + (1.0 - ADAM_B1) * g
        new_v = ADAM_B2 * v_ref[...] + (1.0 - ADAM_B2) * (g * g)
        m_hat = new_m / (1.0 - ADAM_B1 ** ADAM_STEP)
        v_hat = new_v / (1.0 - ADAM_B2 ** ADAM_STEP)
        g_out[...] = g
        d_out[...] = -ADAM_LR * (m_hat / (jnp.sqrt(v_hat) + ADAM_EPS) + ADAM_WD * w_ref[...])
        m_out[...] = new_m
        v_out[...] = new_v

    tile = pl.BlockSpec((tr, c), lambda i: (first + i, 0))
    in_specs = [pl.BlockSpec((1, tr, c), lambda i, s=s: (s, i, 0)) for _, s in parts] + [tile] * 3
    ins = [a for a, _ in parts] + [w, m, v]
    aliases = {}
    if prev is not None:
        aliases = {len(ins) + k: k for k in range(4)}
        in_specs += [ANY] * 4
        ins += list(prev)
    if after is not None:
        in_specs.append(ANY)
        ins.append(after)
    return pl.pallas_call(
        body, name=name, grid=(r // tr,), in_specs=in_specs,
        out_specs=[tile] * 4, out_shape=[jax.ShapeDtypeStruct(w.shape, F32)] * 4,
        input_output_aliases=aliases, compiler_params=_params("parallel"),
    )(*ins)


HBM = pl.BlockSpec(memory_space=pltpu.HBM)
SEM = pl.BlockSpec(memory_space=pltpu.SEMAPHORE)
SIDE_EFFECT = pltpu.SideEffectType.DATAFLOW_SIDE_EFFECTING


def _split_copies(plan, src_refs, land_refs, send_sems, recv_sems):
    copies = []
    for i, (k, src_slot, land_slot, device) in enumerate(plan(*_place())):
        copies.append(pltpu.make_async_remote_copy(
            src_ref=src_refs[k] if src_slot is None else src_refs[k].at[src_slot], dst_ref=land_refs[k].at[land_slot],
            send_sem=send_sems.at[i], recv_sem=recv_sems.at[i], device_id=device, device_id_type=MESH))
    return copies


def split_start(name, srcs, land_shapes, plan, n_copies):
    n = len(srcs)
    lands = [] if land_shapes is None else [lax.empty(shape, s.dtype) for shape, s in zip(land_shapes, srcs)]
    m = len(lands)

    def body(*refs):
        src_refs = refs[:n]
        land_refs = refs[n:n + m] if m else src_refs
        send_sems, recv_sems, token = refs[n + m], refs[n + m + 1], refs[-1]
        for cp in _split_copies(plan, src_refs, land_refs, send_sems, recv_sems):
            cp.start()
        token[...] = jnp.zeros_like(token)

    ins = [pltpu.with_memory_space_constraint(a, pltpu.HBM) for a in list(srcs) + lands]
    out = pl.pallas_call(
        body, name=name,
        out_shape=(pltpu.SemaphoreType.DMA((n_copies,)), pltpu.SemaphoreType.DMA((n_copies,)),
                   *[pltpu.HBM(a.shape, a.dtype) for a in ins], jax.ShapeDtypeStruct((8, LANES), F32)),
        in_specs=[HBM] * (n + m), out_specs=(SEM, SEM, *[HBM] * (n + m), pl.BlockSpec(memory_space=pltpu.VMEM)),
        input_output_aliases={i: 2 + i for i in range(n + m)},
        compiler_params=pltpu.CompilerParams(has_side_effects=SIDE_EFFECT),
    )(*ins)
    return out[0], out[1], list(out[2:2 + n]), list(out[2 + n:2 + n + m]), out[-1]


def split_wait(name, handle, plan, after):
    send_sems, recv_sems, srcs, lands, _ = handle
    n, m = len(srcs), len(lands)
    after = list(after) if isinstance(after, (list, tuple)) else [after]

    def body(*refs):
        src_refs = refs[:n]
        land_refs = refs[n:n + m] if m else src_refs
        for cp in _split_copies(plan, src_refs, land_refs, refs[n + m], refs[n + m + 1]):
            cp.wait_send()
            cp.wait_recv()

    out = pl.pallas_call(
        body, name=name, out_shape=tuple(pltpu.HBM(a.shape, a.dtype) for a in srcs + lands),
        in_specs=[HBM] * (n + m) + [SEM, SEM] + [ANY] * len(after), out_specs=tuple([HBM] * (n + m)),
        input_output_aliases={i: i for i in range(n + m)},
        compiler_params=pltpu.CompilerParams(has_side_effects=SIDE_EFFECT),
    )(*srcs, *lands, send_sems, recv_sems, *after)
    return list(out[:n]), list(out[n:])


def gather_plan(n):
    def plan(x, y, c):
        me = 4 * x + 2 * y + c
        peers = [(x, y, 1 - c)] + [(*chip, c) for chip in _other_chips(x, y)]
        return [(k, None, me, peer) for k in range(n) for peer in peers]
    return plan


def pass_on_plan(n):
    def plan(x, y, c):
        return [(k, 4 * cx + 2 * cy + c, 4 * cx + 2 * cy + c, (x, y, 1 - c)) for k in range(n) for cx, cy in _other_chips(x, y)]
    return plan


def sibling_plan(n):
    def plan(x, y, c):
        return [(k, 4 * cx + 2 * cy + (1 - c), q, (x, y, 1 - c)) for k in range(n) for q, (cx, cy) in enumerate(CHIP_ORDER)]
    return plan


def chips_plan(n):
    def plan(x, y, c):
        return [(k, 1 + j, j, (*chip, c)) for k in range(n) for j, chip in enumerate(_other_chips(x, y))]
    return plan


ROW_TILE = 256
COL_TILE = 256


def _rms_fwd(tag, h, w, after=None):
    return rows_fwd(tag, _rms_f, [(h, D_MODEL, 0)], [w], [(D_MODEL, BF16)], ROW_TILE, after=after)[0]


def _rms_bwd(tag, h, w, dhn, dres):
    dh, (dw,) = rows_bwd(tag, _rms_f, [(h, D_MODEL, 0)], [w], [(dhn, D_MODEL, 0)], ROW_TILE, [(F32, BF16)], add=dres)
    return tuple(dh), dw


def even_fwd(tag, h, get_w, p, after=None):
    hn = _rms_fwd(tag + "_rms", h, p["nm"], after)
    w, after = get_w(hn)
    uv = matmul(tag + "_uv", hn, w["uv"], after=after)
    z = matmul(tag + "_z", hn, w["z"])
    xbc = matmul(tag + "_xbc", hn, w["xbc"])
    pdt = matmul(tag + "_dt", hn, w["dt"])
    gm = [p["lng"], p["lnb"], p["ws"], p["bst"]]
    ya = rows_fwd(tag + "_gmlp", _gmlp_f, [(uv, 2 * D_MODEL, 0)], gm, [(D_MODEL, BF16)], GM_BLOCK)[0]
    xa = cols_fwd(tag + "_conv", _conv_silu_f, [(xbc, 0)], [p["cw"], p["cb"]], F32, COL_TILE)
    y, states = ssd_fwd(tag + "_ssd", xa, pdt, p["hp"])
    yb = rows_fwd(tag + "_gate", _gate_norm_f, [(y, D_MODEL, 0), (z, D_MODEL, 0)], [p["nw"]], [(D_MODEL, BF16)], ROW_TILE)[0]
    h1 = matmul(tag + "_out_b", yb, w["out_bot"], res=matmul(tag + "_out_a", ya, w["out_top"], res=h))
    return h1, dict(h=h, hn=hn, uv=uv, z=z, xbc=xbc, pdt=pdt, xa=xa, y=y, states=states, ya=ya, yb=yb, w=w)


def even_bwd(tag, dh1, s, w, p, after=None, hook=None):
    dh1, dh1b = dh1
    dya = matmul(tag + "_dya", dh1b, w["out_top"], tb=True, after=after)
    dyb = matmul(tag + "_dyb", dh1b, w["out_bot"], tb=True, after=after)
    later = hook(dyb) if hook else None
    gw = dict(out_top=matmul(tag + "_gwa", s["ya"], dh1b, ta=True, out_dtype=BF16, after=later),
              out_bot=matmul(tag + "_gwb", s["yb"], dh1b, ta=True, out_dtype=BF16, after=later))
    (dy, dz), (dnw,) = rows_bwd(tag + "_gate_b", _gate_norm_f, [(s["y"], D_MODEL, 0), (s["z"], D_MODEL, 0)], [p["nw"]],
                                [(dyb, D_MODEL, 0)], ROW_TILE, [F32, BF16], after=later)
    dxs, dbm, dcm, dpdt, dhp = ssd_bwd(tag + "_ssd_b", s["xa"], s["pdt"], p["hp"], s["states"], dy)
    dxa = jnp.concatenate([dxs, dbm, dcm], axis=1)
    (dxbc,), (dcw, dcb) = cols_bwd(tag + "_conv_b", _conv_silu_f, [(s["xbc"], 0)], [p["cw"], p["cb"]], dxa, COL_TILE, BF16)
    gm = [p["lng"], p["lnb"], p["ws"], p["bst"]]
    (duv,), (dlng, dlnb, dws, dbst) = rows_bwd(tag + "_gmlp_b", _gmlp_f, [(s["uv"], 2 * D_MODEL, 0)], gm,
                                               [(dya, D_MODEL, 0)], GM_BLOCK, [BF16])
    dhn = None
    for key, d in (("uv", duv), ("z", dz), ("xbc", dxbc), ("dt", dpdt)):
        dhn = matmul(f"{tag}_dx_{key}", d, w[key], tb=True, res=dhn)
        gw[key] = matmul(f"{tag}_gw_{key}", s["hn"], d, ta=True, out_dtype=BF16)
    dh, dnm = _rms_bwd(tag + "_rms_b", s["h"], p["nm"], dhn, dh1)
    gp = dict(nm=dnm, lng=dlng, lnb=dlnb, ws=dws, bst=dbst, cw=dcw, cb=dcb, hp=dhp, nw=dnw)
    return dh, gw, gp


def odd_fwd(tag, h, get_w, p, cos, sin, after=None):
    hn = _rms_fwd(tag + "_rms", h, p["nm"], after)
    w, after = get_w(hn)
    proj = matmul(tag + "_in", hn, w["in"], after=after)
    cq, ckv, kpe = rows_fwd(tag + "_qkvn", _qkv_norm_f, [(proj, ODD_IN_PAD, 0), (cos, LANES, 0), (sin, LANES, 0)],
                            [p["qn"], p["kvn"]], [(MLA_RANK, BF16), (MLA_RANK, BF16), (LANES, F32)], ROW_TILE)
    q = matmul(tag + "_q", cq, w["uq"])
    kn = matmul(tag + "_kn", ckv, w["kn"], out_dtype=BF16)
    v = matmul(tag + "_v", ckv, w["v"], out_dtype=BF16)
    o = attn_fwd(tag + "_attn", q, kn, kpe, v, cos, sin)
    h1 = matmul(tag + "_o", o, w["o"], res=h)
    return h1, dict(h=h, hn=hn, proj=proj, cq=cq, ckv=ckv, kpe=kpe, q=q, kn=kn, v=v, o=o, w=w)


def odd_bwd(tag, dh1, s, w, p, cos, sin, after=None, hook=None):
    dh1, dh1b = dh1
    do = matmul(tag + "_do", dh1b, w["o"], tb=True, after=after)
    later = hook(do) if hook else None
    gw = dict(o=matmul(tag + "_gw_o", s["o"], dh1b, ta=True, out_dtype=BF16, after=later))
    dq, dkn, dkpe, dv = attn_bwd(tag + "_attn_b", s["q"], s["kn"], s["kpe"], s["v"], cos, sin, do, after=later)
    dcq = matmul(tag + "_dcq", dq, w["uq"], tb=True)
    gw["uq"] = matmul(tag + "_gw_uq", s["cq"], dq, ta=True, out_dtype=BF16)
    dckv = matmul(tag + "_dckv_v", dv, w["v"], tb=True, res=matmul(tag + "_dckv_k", dkn, w["kn"], tb=True))
    gw["kn"] = matmul(tag + "_gw_kn", s["ckv"], dkn, ta=True, out_dtype=BF16)
    gw["v"] = matmul(tag + "_gw_v", s["ckv"], dv, ta=True, out_dtype=BF16)
    (dproj,), (dqn, dkvn) = rows_bwd(
        tag + "_qkvn_b", _qkv_norm_f, [(s["proj"], ODD_IN_PAD, 0), (cos, LANES, 0), (sin, LANES, 0)], [p["qn"], p["kvn"]],
        [(dcq, MLA_RANK, 0), (dckv, MLA_RANK, 0), (dkpe, LANES, 0)], ROW_TILE, [BF16], n_nondiff=2)
    dhn = matmul(tag + "_dx_in", dproj, w["in"], tb=True)
    gw["in"] = matmul(tag + "_gw_in", s["hn"], dproj, ta=True, out_dtype=BF16)
    dh, dnm = _rms_bwd(tag + "_rms_b", s["h"], p["nm"], dhn, dh1)
    return dh, gw, dict(nm=dnm, qn=dqn, kvn=dkvn)


def ffn_fwd(tag, h, get_w, p, after=None):
    hn = _rms_fwd(tag + "_rms", h, p["nf"], after)
    w, after = get_w(hn)
    g = matmul(tag + "_up_g", hn, w["up_g"], after=after)
    val = matmul(tag + "_up_v", hn, w["up_v"])
    act = cols_fwd(tag + "_act", _ffn_act_f, [(g, 0), (val, 0)], [p["fcw"], p["fcb"]], BF16, COL_TILE)
    h2 = matmul(tag + "_down", act, w["down"], res=h)
    return h2, dict(h=h, hn=hn, g=g, val=val, act=act, w=w)


def ffn_bwd(tag, dh2, s, w, p, after=None, hook=None):
    dh2, dh2b = dh2
    dact = matmul(tag + "_dact", dh2b, w["down"], tb=True, after=after)
    later = hook(dact) if hook else None
    gw = dict(down=matmul(tag + "_gw_down", s["act"], dh2b, ta=True, out_dtype=BF16, after=later))
    (dg, dval), (dfcw, dfcb) = cols_bwd(tag + "_act_b", _ffn_act_f, [(s["g"], 0), (s["val"], 0)], [p["fcw"], p["fcb"]],
                                        dact, COL_TILE, BF16, after=later)
    dhn = matmul(tag + "_dx_v", dval, w["up_v"], tb=True, res=matmul(tag + "_dx_g", dg, w["up_g"], tb=True))
    gw["up"] = matmul_tn_slots(tag + "_gw_up", s["hn"], [dg, dval], 2 * D_FF // N_DEV)
    dh, dnf = _rms_bwd(tag + "_rms_b", s["h"], p["nf"], dhn, dh2)
    return dh, gw, dict(nf=dnf, fcw=dfcw, fcb=dfcb)


def _cols_from_slots(g):
    return jnp.moveaxis(g, 0, 1).reshape(g.shape[1], N_DEV * g.shape[2])


def _slots_from_cols(wmat):
    k, n = wmat.shape
    return jnp.moveaxis(wmat.reshape(k, N_DEV, n // N_DEV), 1, 0)


def _pad_last(a, width):
    return jnp.pad(a, [(0, 0)] * (a.ndim - 1) + [(0, width - a.shape[-1])])


def _heads_to_lanes(a):
    lead = a.shape[:-1]
    return _pad_last(a.reshape(lead + (SSD_STEPS, SSD_HEADS_PER_STEP)), LANES).reshape(lead + (SSD_STEPS * LANES,))


def _lanes_to_heads(a):
    lead = a.shape[:-1]
    return a.reshape(lead + (SSD_STEPS, LANES))[..., :SSD_HEADS_PER_STEP].reshape(lead + (SSM_HEADS,))


def prep_even(g_in, g_out):
    wn = _cols_from_slots(g_in)
    o1, o2, o3 = 2 * D_MODEL, 3 * D_MODEL, 3 * D_MODEL + SSM_CONV_DIM
    out = g_out.reshape(2 * D_MODEL, D_MODEL)
    return dict(uv=wn[:, :o1], z=wn[:, o1:o2], xbc=wn[:, o2:o3], dt=_heads_to_lanes(wn[:, o3:]),
                out_top=out[:D_MODEL], out_bot=out[D_MODEL:])


def unprep_even(gw):
    wn = jnp.concatenate([gw["uv"], gw["z"], gw["xbc"], _lanes_to_heads(gw["dt"])], axis=1)
    return _slots_from_cols(wn), jnp.concatenate([gw["out_top"], gw["out_bot"]], axis=0).reshape(N_DEV, -1, D_MODEL)


def prep_odd(g_in, g_uq, g_ukv, g_o):
    uq = _cols_from_slots(g_uq).reshape(MLA_RANK, MLA_HEADS, MLA_QK)
    ukv = _cols_from_slots(g_ukv).reshape(MLA_RANK, MLA_HEADS, MLA_NOPE + MLA_V)
    return dict(**{"in": _pad_last(g_in.reshape(D_MODEL, ODD_IN), ODD_IN_PAD)},
                uq=_pad_last(uq, MLA_QPAD).reshape(MLA_RANK, MLA_HEADS * MLA_QPAD),
                kn=ukv[:, :, :MLA_NOPE].reshape(MLA_RANK, MLA_HEADS * MLA_NOPE),
                v=ukv[:, :, MLA_NOPE:].reshape(MLA_RANK, MLA_HEADS * MLA_V),
                o=g_o.reshape(MLA_HEADS * MLA_V, D_MODEL))


def unprep_odd(gw):
    uq = gw["uq"].reshape(MLA_RANK, MLA_HEADS, MLA_QPAD)[:, :, :MLA_QK].reshape(MLA_RANK, MLA_HEADS * MLA_QK)
    ukv = jnp.concatenate([gw["kn"].reshape(MLA_RANK, MLA_HEADS, MLA_NOPE), gw["v"].reshape(MLA_RANK, MLA_HEADS, MLA_V)], axis=2)
    return (gw["in"][:, :ODD_IN].reshape(N_DEV, -1, ODD_IN), _slots_from_cols(uq),
            _slots_from_cols(ukv.reshape(MLA_RANK, -1)), gw["o"].reshape(N_DEV, -1, D_MODEL))


def prep_ffn(g_up, g_down):
    up = _cols_from_slots(g_up)
    return dict(up_g=up[:, :D_FF], up_v=up[:, D_FF:], down=g_down.reshape(D_FF, D_MODEL))


def unprep_ffn(gw):
    return gw["up"], gw["down"].reshape(N_DEV, -1, D_MODEL)


SMALL_TILE = LANES * LANES


def _pack(arrs):
    flat = jnp.concatenate([a.reshape(-1).astype(F32) for a in arrs])
    size = -(-flat.shape[0] // SMALL_TILE) * SMALL_TILE
    return jnp.pad(flat, (0, size - flat.shape[0])).reshape(-1, LANES)


def _unpack(packed, shapes, lead=()):
    flat = packed.reshape(lead + (-1,))
    out, off = [], 0
    for shp in shapes:
        size = math.prod(shp)
        out.append(flat[..., off:off + size].reshape(lead + tuple(shp)))
        off += size
    return out


SMALL_SHARDED = {"ev_gm_ln_g": 2, "ev_gm_ln_b": 2, "ev_conv_w": 2, "od_q_norm": 1, "od_kv_norm": 1, "ff_conv_w": 2}
SMALL_REPLICATED = ["norm_mix", "norm_ffn", "norm_final", "ev_gm_ws", "ev_gm_bs", "ev_conv_b", "ev_dt_bias", "ev_a_log",
                    "ev_d_skip", "ev_ssm_norm_w", "ff_conv_b"]
MATRICES = {"ev_w_in": (2, 2048, 1156), "ev_w_out": (2, 512, 2048), "od_w_in": (2, 256, 1088), "od_w_uq": (2, 512, 384),
            "od_w_ukv": (2, 512, 512), "od_w_o": (2, 256, 2048), "ff_w_up": (4, 2048, 1408), "ff_w_down": (4, 704, 2048)}
WEIGHT_ORDER = ["norm_mix", "norm_ffn", "norm_final", "ev_w_in", "ev_gm_ln_g", "ev_gm_ln_b", "ev_gm_ws", "ev_gm_bs",
                "ev_conv_w", "ev_conv_b", "ev_dt_bias", "ev_a_log", "ev_d_skip", "ev_ssm_norm_w", "ev_w_out", "od_w_in",
                "od_q_norm", "od_kv_norm", "od_w_uq", "od_w_ukv", "od_w_o", "ff_w_up", "ff_conv_w", "ff_conv_b", "ff_w_down"]


def _full_from_shards(name, gathered):
    ax = SMALL_SHARDED[name]
    moved = jnp.moveaxis(gathered, 0, ax)
    shp = moved.shape
    return moved.reshape(shp[:ax] + (shp[ax] * shp[ax + 1],) + shp[ax + 2:])


def _my_shard(name, full, dev):
    ax = SMALL_SHARDED[name]
    shp = full.shape
    split = full.reshape(shp[:ax] + (N_DEV, shp[ax] // N_DEV) + shp[ax + 1:])
    return lax.dynamic_index_in_dim(split, dev, axis=ax, keepdims=False)


def _even_small(sm, j):
    row = lambda a: a.reshape(1, -1)
    hp = jnp.stack([sm["ev_dt_bias"][j], sm["ev_a_log"][j], sm["ev_d_skip"][j]])
    return dict(nm=row(sm["norm_mix"][2 * j]), lng=row(sm["ev_gm_ln_g"][j]), lnb=row(sm["ev_gm_ln_b"][j]),
                ws=sm["ev_gm_ws"][j], bst=sm["ev_gm_bs"][j].T, cw=sm["ev_conv_w"][j], cb=row(sm["ev_conv_b"][j]),
                hp=_heads_to_lanes(hp), nw=row(sm["ev_ssm_norm_w"][j]))


def _odd_small(sm, j):
    row = lambda a: a.reshape(1, -1)
    return dict(nm=row(sm["norm_mix"][2 * j + 1]), qn=row(sm["od_q_norm"][j]), kvn=row(sm["od_kv_norm"][j]))


def _ffn_small(sm, layer):
    row = lambda a: a.reshape(1, -1)
    return dict(nf=row(sm["norm_ffn"][layer]), fcw=sm["ff_conv_w"][layer], fcb=row(sm["ff_conv_b"][layer]))


def _rope_tables(positions):
    inv_freq = ROPE_THETA ** (-jnp.arange(0, MLA_ROPE, 2, dtype=F32) / MLA_ROPE)
    ang = positions.astype(F32).reshape(-1, 1) * inv_freq
    cos, sin = jnp.cos(ang), jnp.sin(ang)
    return _pad_last(jnp.concatenate([cos, cos], axis=1), LANES), _pad_last(jnp.concatenate([-sin, sin], axis=1), LANES)


def local_step(x, positions, target, sm, fetch_weights, emit_grads):
    cos, sin = _rope_tables(positions)
    h, saved = x, []
    for layer in range(4):
        j, tag = layer // 2, f"l{layer}"
        get_w, dep = fetch_weights(2 * layer, h)
        if layer % 2 == 0:
            pm = _even_small(sm, j)
            h, sv = even_fwd(tag, h, get_w, pm, dep)
        else:
            pm = _odd_small(sm, j)
            h, sv = odd_fwd(tag, h, get_w, pm, cos, sin, dep)
        get_w, dep = fetch_weights(2 * layer + 1, h)
        pf = _ffn_small(sm, layer)
        h, sf = ffn_fwd(tag + "f", h, get_w, pf, dep)
        saved.append((pm, sv, pf, sf, sv["w"], sf["w"]))
    loss_tile, dh32, dh16, dnfinal = final_loss("final_loss", h, sm["norm_final"].reshape(1, -1), target)
    gs = {k: [None] * v.shape[0] for k, v in sm.items() if k != "norm_final"}
    gs["norm_final"] = dnfinal.reshape(-1)
    dh = (dh32, dh16)
    dep, hook = None, None
    for layer in reversed(range(4)):
        j, tag = layer // 2, f"l{layer}"
        pm, sv, pf, sf, wm, wf = saved[layer]
        dh, gwf, gpf = ffn_bwd(tag + "f", dh, sf, wf, pf, dep, hook)
        gs["norm_ffn"][layer], gs["ff_conv_w"][layer], gs["ff_conv_b"][layer] = gpf["nf"][0], gpf["fcw"], gpf["fcb"][0]
        dep, hook = emit_grads(2 * layer + 1, gwf, dh[0])
        if layer % 2 == 0:
            dh, gwm, gp = even_bwd(tag, dh, sv, wm, pm, dep, hook)
            hp = _lanes_to_heads(gp["hp"])
            gs["norm_mix"][layer] = gp["nm"][0]
            gs["ev_gm_ln_g"][j], gs["ev_gm_ln_b"][j] = gp["lng"].reshape(GM_GROUPS, -1), gp["lnb"].reshape(GM_GROUPS, -1)
            gs["ev_gm_ws"][j], gs["ev_gm_bs"][j] = gp["ws"], gp["bst"].T
            gs["ev_conv_w"][j], gs["ev_conv_b"][j] = gp["cw"], gp["cb"][0]
            gs["ev_dt_bias"][j], gs["ev_a_log"][j], gs["ev_d_skip"][j] = hp[0], hp[1], hp[2]
            gs["ev_ssm_norm_w"][j] = gp["nw"][0]
        else:
            dh, gwm, gp = odd_bwd(tag, dh, sv, wm, pm, cos, sin, dep, hook)
            gs["norm_mix"][layer] = gp["nm"][0]
            gs["od_q_norm"][j], gs["od_kv_norm"][j] = gp["qn"][0], gp["kvn"][0]
        if layer > 0:
            dep, hook = emit_grads(2 * layer, gwm, dh[0])
    gs = {k: (v if k == "norm_final" else jnp.stack(v)) for k, v in gs.items()}
    return loss_tile[0, 0], dh[0], gs, gwm


def kernel(x, positions, norm_mix, norm_ffn, norm_final, ev_w_in, ev_gm_ln_g, ev_gm_ln_b, ev_gm_ws, ev_gm_bs, ev_conv_w, ev_conv_b, ev_dt_bias, ev_a_log, ev_d_skip, ev_ssm_norm_w, ev_w_out, od_w_in, od_q_norm, od_kv_norm, od_w_uq, od_w_ukv, od_w_o, ff_w_up, ff_conv_w, ff_conv_b, ff_w_down, loss_target, m_norm_mix, m_norm_ffn, m_norm_final, m_ev_w_in, m_ev_gm_ln_g, m_ev_gm_ln_b, m_ev_gm_ws, m_ev_gm_bs, m_ev_conv_w, m_ev_conv_b, m_ev_dt_bias, m_ev_a_log, m_ev_d_skip, m_ev_ssm_norm_w, m_ev_w_out, m_od_w_in, m_od_q_norm, m_od_kv_norm, m_od_w_uq, m_od_w_ukv, m_od_w_o, m_ff_w_up, m_ff_conv_w, m_ff_conv_b, m_ff_w_down, v_norm_mix, v_norm_ffn, v_norm_final, v_ev_w_in, v_ev_gm_ln_g, v_ev_gm_ln_b, v_ev_gm_ws, v_ev_gm_bs, v_ev_conv_w, v_ev_conv_b, v_ev_dt_bias, v_ev_a_log, v_ev_d_skip, v_ev_ssm_norm_w, v_ev_w_out, v_od_w_in, v_od_q_norm, v_od_kv_norm, v_od_w_uq, v_od_w_ukv, v_od_w_o, v_ff_w_up, v_ff_conv_w, v_ff_conv_b, v_ff_w_down):
    args = dict(locals())
    wts = {n: args[n] for n in WEIGHT_ORDER}
    mom = {n: args["m_" + n] for n in WEIGHT_ORDER}
    var = {n: args["v_" + n] for n in WEIGHT_ORDER}
    dev = _device_slot()

    small_names = list(SMALL_SHARDED)
    small_shapes = [wts[n].shape for n in small_names]
    (small_all,) = all_gather("ag_small", [_pack([wts[n] for n in small_names])])
    small_full = _unpack(small_all, small_shapes, lead=(N_DEV,))
    sm = {n: _full_from_shards(n, g) for n, g in zip(small_names, small_full)}
    sm.update({n: wts[n] for n in SMALL_REPLICATED})

    def stage_matrices(stage):
        layer, is_ffn = divmod(stage, 2)
        if is_ffn:
            return [("ff_w_up", layer), ("ff_w_down", layer)]
        return [(n, layer // 2) for n in (["ev_w_in", "ev_w_out"] if layer % 2 == 0 else ["od_w_in", "od_w_uq", "od_w_ukv", "od_w_o"])]

    def stage_fns(stage):
        layer, is_ffn = divmod(stage, 2)
        if is_ffn:
            return prep_ffn, unprep_ffn
        return (prep_even, unprep_even) if layer % 2 == 0 else (prep_odd, unprep_odd)

    n_stages, ahead = 8, 2

    bf = {n: wts[n].astype(BF16) for n in MATRICES}

    def start_gather(stage, earlier=None):
        shards = [bf[n][i] for n, i in stage_matrices(stage)]
        if earlier is not None:
            shards, _ = lax.optimization_barrier((shards, earlier))
        return split_start(f"ag_s{stage}_start", shards, [(N_DEV,) + s.shape for s in shards],
                           gather_plan(len(shards)), 4 * len(shards))

    gathers = {}
    for stage in range(ahead):
        gathers[stage] = start_gather(stage, gathers[stage - 1][4] if stage else None)

    def fetch_weights(stage, h):
        n = len(stage_matrices(stage))
        shards, landed = split_wait(f"ag_s{stage}_wait", gathers.pop(stage), gather_plan(n), h)
        passing = split_start(f"ag_s{stage}_pass_start", landed, None, pass_on_plan(n), 3 * n)

        def get_w(first_result):
            g, _ = split_wait(f"ag_s{stage}_pass_wait", passing, pass_on_plan(n), first_result)
            g = [lax.dynamic_update_index_in_dim(gk, sk, dev, 0) for gk, sk in zip(g, shards)]
            started = None
            if stage + ahead < n_stages:
                gathers[stage + ahead] = start_gather(stage + ahead, g[0])
                started = gathers[stage + ahead][4]
            return stage_fns(stage)[0](*g), started

        return get_w, passing[4]

    scatters = []
    out = {n: None for n in MATRICES}

    held = []
    hold_below = 6

    def update(mats, sums, recv, after=None):
        for (n, i), p_, r_ in zip(mats, sums, recv):
            layers, rows, cols = MATRICES[n]
            two_d = lambda a: a.reshape(layers * rows, cols)
            out[n] = adamw(f"adamw_{n}_{i}", [(p_, 0), (r_, 0), (r_, 1), (r_, 2)], two_d(wts[n]), two_d(mom[n]),
                           two_d(var[n]), tr=_row_tile(rows, 256), part=i, prev=out[n], after=after)
            if after is not None:
                after = out[n][0]
        return after

    def finish_scatter(after):
        stage, handle = scatters.pop(0)
        mats = stage_matrices(stage)
        sums, recv = split_wait(f"rs_s{stage}_wait", handle, chips_plan(len(mats)), after)
        if 0 < stage < hold_below:
            held.append((mats, sums, recv))
        else:
            update(mats, sums, recv)

    def start_chips(stage, send, from_sibling):
        sums = [chip_sums(f"rs_s{stage}_add{k}", g, r, tr=_row_tile(g.shape[1], 512)) for k, (g, r) in enumerate(zip(send, from_sibling))]
        handle = split_start(f"rs_s{stage}_start", sums, [(3,) + s.shape[1:] for s in sums], chips_plan(len(sums)), 3 * len(sums))
        scatters.append((stage, handle))
        return handle[4]

    def emit_grads(stage, gw, dh):
        if len(scatters) >= ahead:
            finish_scatter(dh)
        send = list(stage_fns(stage)[1](gw))
        plan = sibling_plan(len(send))
        handle = split_start(f"rs_s{stage}_sib_start", send, [(4,) + g.shape[1:] for g in send], plan, 4 * len(send))

        def hook(first_result):
            sent, from_sibling = split_wait(f"rs_s{stage}_sib_wait", handle, plan, first_result)
            return start_chips(stage, sent, from_sibling)

        return handle[4], hook

    def emit_last(stage, gw, after):
        finish_scatter(after)
        send = list(stage_fns(stage)[1](gw))
        return start_chips(stage, send, exchange_sibling(f"rs_s{stage}_sibling", send, after))

    loss_local, dx, gs, gw_first = local_step(x[0], positions[0], loss_target[0], sm, fetch_weights, emit_grads)
    loss = lax.psum(loss_local, ("x", "y", "c"))

    all_small = small_names + SMALL_REPLICATED
    (partials,) = all_gather("ar_small", [_pack([gs[n] for n in all_small])])
    last_started = emit_last(0, gw_first, partials)
    total = sum_parts("ar_small_sum", [(partials, s) for s in range(N_DEV)])
    g_full = dict(zip(all_small, _unpack(total, [gs[n].shape for n in all_small])))
    g_mine = {n: (_my_shard(n, g_full[n], dev) if n in SMALL_SHARDED else g_full[n]) for n in all_small}
    packed = [_pack([d[n] for n in all_small]) for d in (g_mine, wts, mom, var)]
    res = adamw("adamw_small", [(packed[0][None], 0)], packed[1], packed[2], packed[3], after=last_started)
    unpacked = [_unpack(a, [wts[n].shape for n in all_small]) for a in res]
    for i, n in enumerate(all_small):
        out[n] = [u[i] for u in unpacked]
    while len(scatters) > 1:
        finish_scatter(res[0])
    follow = res[0]
    for job in held:
        follow = update(*job, after=follow)
    finish_scatter(follow)
    for n in MATRICES:
        out[n] = [a.reshape(wts[n].shape) for a in out[n]]

    return (loss, dx[None], *[out[n][0] for n in WEIGHT_ORDER], *[out[n][1] for n in WEIGHT_ORDER],
            *[out[n][2] for n in WEIGHT_ORDER], *[out[n][3] for n in WEIGHT_ORDER])
```
